```python
import math
import jax, jax.numpy as jnp
from jax import lax
import numpy as np

D_MODEL = 1024
BATCH = 32
SEQ = 2048
DEPTH = 2

CTX_LEN = 256
GRID_W = 64
CHUNK = 128
GM_GROUPS = 4
GM_GDIM = 128
GM_WIDTH = GM_GROUPS * GM_GDIM
HEAD_DIM = 128
N_Q_HEADS = 4
N_KV_HEADS = 2
Q_PER_KV = N_Q_HEADS // N_KV_HEADS
ATTN_WIDTH = N_Q_HEADS * HEAD_DIM
KV_W = N_KV_HEADS * HEAD_DIM
ROPE_THETA = 10000.0
FN_GROUPS = 4
FN_GDIM = 128
FN_WIDTH = FN_GROUPS * FN_GDIM
SSD_HEADDIM = 64
SSD_HEADS = 8
SSD_GROUPS = 2
SSD_HPG = SSD_HEADS // SSD_GROUPS
SSD_STATE = 128
SSD_WIDTH = SSD_HEADS * SSD_HEADDIM
SSD_GN = SSD_GROUPS * SSD_STATE
SSD_CONV_DIM = SSD_WIDTH + 2 * SSD_GN
D_FF = 2816
CONV_W = 3
KV_OFF = 2 * GM_WIDTH + ATTN_WIDTH
AB_IN = KV_OFF + 2 * KV_W
AB_OUT = GM_WIDTH + ATTN_WIDTH
SSM_OFF = FN_WIDTH + SSD_WIDTH
CD_IN = SSM_OFF + SSD_CONV_DIM + 2 * SSD_HEADS
CD_OUT = FN_WIDTH + SSD_WIDTH
N_EVEN = (DEPTH + 1) // 2
N_ODD = DEPTH // 2
DN_ALPHA = (2 * DEPTH) ** 0.25
DN_BETA = (8 * DEPTH) ** -0.25
EPS = 1e-6

kernel_name = 'hybrid_gmlp_gqa_fnet_ssd_dit_block'


def layer_norm(x, g, b):
    xf = x.astype(jnp.float32)
    mu = jnp.mean(xf, axis=-1, keepdims=True)
    xc = xf - mu
    var = jnp.mean(xc * xc, axis=-1, keepdims=True)
    return (xc * lax.rsqrt(var + EPS) * g + b).astype(x.dtype)


def rms_norm(x, g):
    xf = x.astype(jnp.float32)
    return (xf * lax.rsqrt(jnp.mean(xf * xf, axis=-1, keepdims=True) + EPS) * g).astype(x.dtype)


def modulate(x, shift, scale):
    return x * (1 + scale) + shift


def dwconv(x, w, b):
    L = x.shape[1]
    pad = CONV_W // 2
    xp = jnp.pad(x, ((0, 0), (pad, pad), (0, 0)))
    out = b
    for k in range(CONV_W):
        out = out + xp[:, k:k + L] * w[k]
    return out


def rope_tables(L):
    rows = L // GRID_W
    row = jnp.repeat(jnp.arange(rows, dtype=jnp.float32), GRID_W)
    col = jnp.broadcast_to(jnp.arange(GRID_W, dtype=jnp.float32), (rows, GRID_W)).reshape(-1)
    n_freq = HEAD_DIM // 4
    inv = ROPE_THETA ** (-jnp.arange(n_freq, dtype=jnp.float32) / n_freq)
    ang = jnp.stack([row, col], axis=-1)[:, :, None] * inv
    return jnp.cos(ang), jnp.sin(ang)


def apply_rope(x, cos, sin):
    xf = x.astype(jnp.float32).reshape(x.shape[:-1] + (2, 2, HEAD_DIM // 4))
    x1, x2 = xf[..., 0, :], xf[..., 1, :]
    c = cos[None, :, None]
    s = sin[None, :, None]
    out = jnp.stack([x1 * c - x2 * s, x2 * c + x1 * s], axis=-2)
    return out.reshape(x.shape).astype(x.dtype)


def attend(q, k, v):
    s = jnp.einsum('bqkgd,bskd->bkgqs', q, k).astype(jnp.float32)
    p = jax.nn.softmax(s, axis=-1).astype(v.dtype)
    return jnp.einsum('bkgqs,bskd->bqkgd', p, v)


def blocked_attention(q, k, v):
    Bn, L = q.shape[:2]
    nb = L // CHUNK
    qb = (q * HEAD_DIM ** -0.5).reshape(Bn, nb, CHUNK, N_KV_HEADS, Q_PER_KV, HEAD_DIM)
    qb = jnp.moveaxis(qb, 1, 0)
    ob = lax.map(lambda qq: attend(qq, k, v), qb)
    return jnp.moveaxis(ob, 0, 1).reshape(Bn, L, ATTN_WIDTH)


def chunk_gmlp(u, v, ln_g, ln_b, ws, bs):
    u = jax.nn.gelu(u)
    v = layer_norm(jax.nn.gelu(v), ln_g, ln_b)
    Bn, L, _ = v.shape
    vr = v.reshape(Bn, L // CHUNK, CHUNK, GM_GROUPS, GM_GDIM)
    mixed = jnp.einsum('gpq,bnqgd->bnpgd', ws, vr) + bs.T[:, :, None]
    return u * mixed.reshape(Bn, L, GM_WIDTH)


def fourier_mix(h):
    Bn, L, _ = h.shape
    hf = h.astype(jnp.float32).reshape(Bn, L, FN_GROUPS, FN_GDIM)
    out = jnp.fft.fft2(hf, axes=(1, 3), norm='ortho').real
    return out.reshape(Bn, L, FN_WIDTH).astype(h.dtype)


def ssd_inputs(p, conv_w, conv_b, dt_bias):
    Bn, L = p.shape[:2]
    xbc = jax.nn.silu(dwconv(p[..., :SSD_CONV_DIM], conv_w, conv_b))
    xs = xbc[..., :SSD_WIDTH].reshape(Bn, L, SSD_GROUPS, SSD_HPG, SSD_HEADDIM)
    bm = xbc[..., SSD_WIDTH:SSD_WIDTH + SSD_GN].reshape(Bn, L, SSD_GROUPS, SSD_STATE)
    cm = xbc[..., SSD_WIDTH + SSD_GN:].reshape(Bn, L, SSD_GROUPS, SSD_STATE)
    dt = jax.nn.softplus(p[..., SSD_CONV_DIM:].astype(jnp.float32).reshape(Bn, L, 2, SSD_GROUPS, SSD_HPG)
                         + dt_bias.astype(jnp.float32).reshape(2, SSD_GROUPS, SSD_HPG))
    return xs, bm, cm, dt[:, :, 0], dt[:, :, 1]


def ssd_scan(x, dt, a, bm, cm, h0, with_y):
    Bn, L = x.shape[:2]
    nc = L // CHUNK
    xc = x.reshape(Bn, nc, CHUNK, SSD_GROUPS, SSD_HPG, SSD_HEADDIM)
    dtc = dt.reshape(Bn, nc, CHUNK, SSD_GROUPS, SSD_HPG)
    bc = bm.reshape(Bn, nc, CHUNK, SSD_GROUPS, SSD_STATE)
    cc = cm.reshape(Bn, nc, CHUNK, SSD_GROUPS, SSD_STATE)
    a_cum = jnp.cumsum(dtc * a, axis=2)
    xdt = xc * dtc[..., None]
    to_end = jnp.exp(a_cum[:, :, -1:] - a_cum)
    states = jnp.einsum('bcsgn,bcsgjp->bcgjpn', bc, xdt * to_end[..., None])
    chunk_decay = jnp.exp(a_cum[:, :, -1])

    def step(h, inp):
        dec, st = inp
        return h * dec[..., None, None] + st, h

    h_final, h_in = lax.scan(step, h0, (jnp.moveaxis(chunk_decay, 1, 0), jnp.moveaxis(states, 1, 0)))
    if not with_y:
        return None, h_final
    h_in = jnp.moveaxis(h_in, 0, 1)
    seg = a_cum[:, :, :, None] - a_cum[:, :, None]
    mask = jnp.tril(jnp.ones((CHUNK, CHUNK), dtype=bool))[:, :, None, None]
    decay = jnp.exp(jnp.where(mask, seg, -jnp.inf))
    m = jnp.einsum('bctgn,bcsgn->bctsg', cc, bc)[..., None] * decay
    y_diag = jnp.einsum('bctsgj,bcsgjp->bctgjp', m, xdt)
    y_off = jnp.einsum('bctgn,bcgjpn->bctgjp', cc, h_in) * jnp.exp(a_cum)[..., None]
    return (y_diag + y_off).reshape(Bn, L, SSD_GROUPS, SSD_HPG, SSD_HEADDIM), h_final


def ssd_out(xs, y_f, y_b, z, d_skip, norm_g):
    Bn, L = xs.shape[:2]
    y = y_f + y_b + xs.astype(jnp.float32) * d_skip.astype(jnp.float32).reshape(SSD_GROUPS, SSD_HPG, 1)
    y = y.reshape(Bn, L, SSD_WIDTH) * jax.nn.silu(z.astype(jnp.float32))
    return rms_norm(y, norm_g).astype(z.dtype)


def flip(t):
    return jnp.flip(t, axis=1)


def ab_mixer(h_l, h_c, w_in, w_out, gm_ln_g, gm_ln_b, gm_ws, gm_bs, q_norm_g, k_norm_g, cos, sin, with_ctx_out):
    Bn, L, _ = h_l.shape
    Lc = h_c.shape[1]
    p_l = h_l @ w_in
    p_c = h_c @ (w_in if with_ctx_out else w_in[:, KV_OFF:])
    kv_c = p_c[..., KV_OFF:] if with_ctx_out else p_c
    k_c = rms_norm(kv_c[..., :KV_W].reshape(Bn, Lc, N_KV_HEADS, HEAD_DIM), k_norm_g)
    v_c = kv_c[..., KV_W:].reshape(Bn, Lc, N_KV_HEADS, HEAD_DIM)
    q_l = apply_rope(rms_norm(p_l[..., 2 * GM_WIDTH:KV_OFF].reshape(Bn, L, N_Q_HEADS, HEAD_DIM), q_norm_g), cos, sin)
    k_l = apply_rope(rms_norm(p_l[..., KV_OFF:KV_OFF + KV_W].reshape(Bn, L, N_KV_HEADS, HEAD_DIM), k_norm_g), cos, sin)
    v_l = p_l[..., KV_OFF + KV_W:].reshape(Bn, L, N_KV_HEADS, HEAD_DIM)
    k_all = jnp.concatenate([k_c, k_l], axis=1)
    v_all = jnp.concatenate([v_c, v_l], axis=1)
    o_l = blocked_attention(q_l, k_all, v_all)
    a_l = chunk_gmlp(p_l[..., :GM_WIDTH], p_l[..., GM_WIDTH:2 * GM_WIDTH], gm_ln_g, gm_ln_b, gm_ws, gm_bs)
    out_l = jnp.concatenate([a_l, o_l], axis=-1) @ w_out
    if not with_ctx_out:
        return out_l, None
    q_c = rms_norm(p_c[..., 2 * GM_WIDTH:KV_OFF].reshape(Bn, Lc, N_Q_HEADS, HEAD_DIM), q_norm_g) * HEAD_DIM ** -0.5
    o_c = attend(q_c.reshape(Bn, Lc, N_KV_HEADS, Q_PER_KV, HEAD_DIM), k_c, v_c).reshape(Bn, Lc, ATTN_WIDTH)
    a_c = chunk_gmlp(p_c[..., :GM_WIDTH], p_c[..., GM_WIDTH:2 * GM_WIDTH], gm_ln_g, gm_ln_b, gm_ws, gm_bs)
    out_c = jnp.concatenate([a_c, o_c], axis=-1) @ w_out
    return out_l, out_c


def cd_mixer(h_l, h_c, w_in, w_out, conv_w, conv_b, dt_bias, a_log, d_skip, norm_g, with_ctx_out):
    Bn = h_l.shape[0]
    p_l = h_l @ w_in
    p_c = h_c @ (w_in if with_ctx_out else w_in[:, SSM_OFF:])
    ssm_c = p_c[..., SSM_OFF:] if with_ctx_out else p_c
    a = -jnp.exp(a_log.astype(jnp.float32)).reshape(2, SSD_GROUPS, SSD_HPG)
    h0 = jnp.zeros((Bn, SSD_GROUPS, SSD_HPG, SSD_HEADDIM, SSD_STATE), jnp.float32)
    x_c, b_c, c_c, dtf_c, dtb_c = ssd_inputs(ssm_c, conv_w, conv_b, dt_bias)
    yf_c, hf = ssd_scan(x_c, dtf_c, a[0], b_c, c_c, h0, with_ctx_out)
    yb_c, hb = ssd_scan(flip(x_c), flip(dtb_c), a[1], flip(b_c), flip(c_c), h0, with_ctx_out)
    x_l, b_l, c_l, dtf_l, dtb_l = ssd_inputs(p_l[..., SSM_OFF:], conv_w, conv_b, dt_bias)
    yf_l, _ = ssd_scan(x_l, dtf_l, a[0], b_l, c_l, hf, True)
    yb_l, _ = ssd_scan(flip(x_l), flip(dtb_l), a[1], flip(b_l), flip(c_l), hb, True)
    s_l = ssd_out(x_l, yf_l, flip(yb_l), p_l[..., FN_WIDTH:SSM_OFF], d_skip, norm_g)
    out_l = jnp.concatenate([fourier_mix(p_l[..., :FN_WIDTH]), s_l], axis=-1) @ w_out
    if not with_ctx_out:
        return out_l, None
    s_c = ssd_out(x_c, yf_c, flip(yb_c), p_c[..., FN_WIDTH:SSM_OFF], d_skip, norm_g)
    out_c = jnp.concatenate([fourier_mix(p_c[..., :FN_WIDTH]), s_c], axis=-1) @ w_out
    return out_l, out_c


def conv_ffn(h, w_up, conv_w, conv_b, w_down):
    a = dwconv(h @ w_up, conv_w, conv_b)
    return (a[..., :D_FF] * jax.nn.silu(a[..., D_FF:])) @ w_down


def setup_inputs(seed: int = 0) -> dict:
    key = jax.random.key(seed)
    ks = iter(jax.random.split(key, 40))

    def nrm(shape, scale):
        return jax.random.normal(next(ks), shape, jnp.float32) * scale

    def gain(shape):
        return 1.0 + nrm(shape, 0.02)

    dt0 = jnp.exp(jax.random.uniform(next(ks), (N_ODD, 2, SSD_HEADS), jnp.float32,
                                     minval=math.log(1e-3), maxval=math.log(1e-1)))
    dt_bias = dt0 + jnp.log(-jnp.expm1(-dt0))
    a_log = jnp.log(jax.random.uniform(next(ks), (N_ODD, 2, SSD_HEADS), jnp.float32, minval=1.0, maxval=16.0))
    return {
        'x': nrm((BATCH, SEQ, D_MODEL), 1.0),
        'c': nrm((BATCH, D_MODEL), 1.0),
        'ctx': nrm((BATCH, CTX_LEN, D_MODEL), 1.0),
        'c_ctx': nrm((D_MODEL,), 1.0),
        'mod_w': nrm((DEPTH, D_MODEL, 6 * D_MODEL), D_MODEL ** -0.5),
        'mod_b': nrm((DEPTH, 6 * D_MODEL), 0.01),
        'ln1_g': gain((DEPTH, D_MODEL)),
        'ln1_b': nrm((DEPTH, D_MODEL), 0.02),
        'ln2_g': gain((DEPTH, D_MODEL)),
        'ln2_b': nrm((DEPTH, D_MODEL), 0.02),
        'ffn_up': nrm((DEPTH, D_MODEL, 2 * D_FF), D_MODEL ** -0.5),
        'ffn_conv_w': nrm((DEPTH, CONV_W, 2 * D_FF), CONV_W ** -0.5),
        'ffn_conv_b': nrm((DEPTH, 2 * D_FF), 0.02),
        'ffn_down': nrm((DEPTH, D_FF, D_MODEL), DN_BETA * D_FF ** -0.5),
        'ab_w_in': nrm((N_EVEN, D_MODEL, AB_IN), D_MODEL ** -0.5),
        'ab_w_out': nrm((N_EVEN, AB_OUT, D_MODEL), DN_BETA * AB_OUT ** -0.5),
        'gm_ln_g': gain((N_EVEN, GM_WIDTH)),
        'gm_ln_b': nrm((N_EVEN, GM_WIDTH), 0.02),
        'gm_ws': nrm((N_EVEN, GM_GROUPS, CHUNK, CHUNK), CHUNK ** -0.5),
        'gm_bs': gain((N_EVEN, GM_GROUPS, CHUNK)),
        'q_norm_g': gain((N_EVEN, HEAD_DIM)),
        'k_norm_g': gain((N_EVEN, HEAD_DIM)),
        'cd_w_in': nrm((N_ODD, D_MODEL, CD_IN), D_MODEL ** -0.5),
        'cd_w_out': nrm((N_ODD, CD_OUT, D_MODEL), DN_BETA * CD_OUT ** -0.5),
        'ssd_conv_w': nrm((N_ODD, CONV_W, SSD_CONV_DIM), CONV_W ** -0.5),
        'ssd_conv_b': nrm((N_ODD, SSD_CONV_DIM), 0.02),
        'ssd_dt_bias': dt_bias,
        'ssd_a_log': a_log,
        'ssd_d': gain((N_ODD, SSD_HEADS)),
        'ssd_norm_g': gain((N_ODD, SSD_WIDTH)),
    }


def reference(x, c, ctx, c_ctx, mod_w, mod_b, ln1_g, ln1_b, ln2_g, ln2_b,
              ffn_up, ffn_conv_w, ffn_conv_b, ffn_down,
              ab_w_in, ab_w_out, gm_ln_g, gm_ln_b, gm_ws, gm_bs, q_norm_g, k_norm_g,
              cd_w_in, cd_w_out, ssd_conv_w, ssd_conv_b, ssd_dt_bias, ssd_a_log, ssd_d, ssd_norm_g):
    L = x.shape[1]
    cos, sin = rope_tables(L)
    c_act = jax.nn.silu(c)
    ctx_act = jax.nn.silu(c_ctx)
    xl, xc = x, ctx
    for i in range(DEPTH):
        last = i == DEPTH - 1
        j = i // 2
        mod_l = (c_act @ mod_w[i] + mod_b[i])[:, None, :]
        mod_c = ctx_act @ mod_w[i] + mod_b[i]
        sh1, sc1, g1, sh2, sc2, g2 = jnp.split(mod_l, 6, axis=-1)
        csh1, csc1, cg1, csh2, csc2, cg2 = jnp.split(mod_c, 6, axis=-1)
        h_l = modulate(xl, sh1, sc1)
        h_c = modulate(xc, csh1, csc1)
        if i % 2 == 0:
            out_l, out_c = ab_mixer(h_l, h_c, ab_w_in[j], ab_w_out[j], gm_ln_g[j], gm_ln_b[j], gm_ws[j], gm_bs[j],
                                    q_norm_g[j], k_norm_g[j], cos, sin, not last)
        else:
            out_l, out_c = cd_mixer(h_l, h_c, cd_w_in[j], cd_w_out[j], ssd_conv_w[j], ssd_conv_b[j],
                                    ssd_dt_bias[j], ssd_a_log[j], ssd_d[j], ssd_norm_g[j], not last)
        xl = layer_norm(DN_ALPHA * xl + g1 * out_l, ln1_g[i], ln1_b[i])
        f_l = conv_ffn(modulate(xl, sh2, sc2), ffn_up[i], ffn_conv_w[i], ffn_conv_b[i], ffn_down[i])
        xl = layer_norm(DN_ALPHA * xl + g2 * f_l, ln2_g[i], ln2_b[i])
        if not last:
            xc = layer_norm(DN_ALPHA * xc + cg1 * out_c, ln1_g[i], ln1_b[i])
            f_c = conv_ffn(modulate(xc, csh2, csc2), ffn_up[i], ffn_conv_w[i], ffn_conv_b[i], ffn_down[i])
            xc = layer_norm(DN_ALPHA * xc + cg2 * f_c, ln2_g[i], ln2_b[i])
    return xl
```

```python
import functools
import math

import jax
import jax.numpy as jnp
from jax import lax
from jax.experimental import pallas as pl
from jax.experimental.pallas import tpu as pltpu

D_MODEL = 1024
DEPTH = 2
GRID_W = 64
CHUNK = 128
GM_GROUPS = 4
GM_WIDTH = 512
HEAD_DIM = 128
N_Q_HEADS = 4
N_KV_HEADS = 2
ATTN_WIDTH = 512
KV_W = 256
ROPE_THETA = 10000.0
FN_GROUPS = 4
FN_GDIM = 128
FN_WIDTH = 512
SSD_HEADDIM = 64
SSD_HEADS = 8
SSD_GROUPS = 2
SSD_HPG = 4
SSD_STATE = 128
SSD_WIDTH = 512
SSD_GN = 256
SSD_CONV_DIM = 1024
D_FF = 2816
CONV_W = 3
KV_OFF = 2 * GM_WIDTH + ATTN_WIDTH
SSM_OFF = FN_WIDTH + SSD_WIDTH
DN_ALPHA = (2 * DEPTH) ** 0.25
EPS = 1e-6

V7X_LANES = 128
V7X_SUBLANES = 8
V7X_VMEM_BYTES = 64 * 1024 * 1024
FF_CHUNK = 256

BF16 = jnp.bfloat16
F32 = jnp.float32
HALO = V7X_SUBLANES


def _vmem_limit(nbytes):
    return int(min(nbytes * 3 // 2 + (8 << 20), V7X_VMEM_BYTES - (6 << 20)))


def _resident(shape):
    nd = len(shape)
    return pl.BlockSpec(shape, lambda *_: (0,) * nd, pipeline_mode=pl.Buffered(1))


def _dot(a, b):
    return jnp.dot(a, b, preferred_element_type=F32)


def _dot_nt(a, b):
    return lax.dot_general(a, b, (((1,), (1,)), ((), ())), preferred_element_type=F32)


def _layer_norm(z, g, b):
    mu = jnp.mean(z, axis=-1, keepdims=True)
    zc = z - mu
    var = jnp.mean(zc * zc, axis=-1, keepdims=True)
    return zc * lax.rsqrt(var + EPS) * g + b


def _rms_norm(z, g):
    return z * lax.rsqrt(jnp.mean(z * z, axis=-1, keepdims=True) + EPS) * g


def _softplus(z):
    return jnp.maximum(z, 0.0) + jnp.log1p(jnp.exp(-jnp.abs(z)))


def _conv3(u, w, b, rows):
    n = u.shape[0]
    um = pltpu.roll(u, 1, 0)[HALO:HALO + rows]
    up = pltpu.roll(u, n - 1, 0)[HALO:HALO + rows]
    return b + w[0:1] * um + w[1:2] * u[HALO:HALO + rows] + w[2:3] * up


def _fill_halo_tile(hm_ref, x_ref, xp_ref, xn_ref, sh, sc, rows):
    t = pl.program_id(1)
    nt = pl.num_programs(1)
    keep_prev = jnp.where(t > 0, 1.0, 0.0)
    keep_next = jnp.where(t < nt - 1, 1.0, 0.0)
    hm_ref[HALO:HALO + rows, :] = (x_ref[0] * (1.0 + sc) + sh).astype(BF16)
    hm_ref[0:HALO, :] = ((xp_ref[0] * (1.0 + sc) + sh) * keep_prev).astype(BF16)
    hm_ref[HALO + rows:2 * HALO + rows, :] = ((xn_ref[0] * (1.0 + sc) + sh) * keep_next).astype(BF16)


def _halo_specs(rows, seq, d):
    nb = rows // HALO
    last = seq // HALO - 1
    return [
        pl.BlockSpec((1, rows, d), lambda b, t: (b, t, 0)),
        pl.BlockSpec((1, HALO, d), lambda b, t: (b, jnp.maximum(t * nb - 1, 0), 0)),
        pl.BlockSpec((1, HALO, d), lambda b, t: (b, jnp.minimum((t + 1) * nb, last), 0)),
    ]


def _row_spec(d):
    return pl.BlockSpec((1, 1, d), lambda b, t: (b, 0, 0))


def _mod_kernel(c_ref, w_ref, b_ref, o_ref):
    c = c_ref[...]
    act = c * jax.nn.sigmoid(c)
    o_ref[0] = jnp.dot(act, w_ref[0], preferred_element_type=F32,
                       precision=lax.Precision.HIGHEST) + b_ref[0]


def _modulation(cc, mod_w, mod_b):
    rows, d = cc.shape
    depth, _, n = mod_w.shape
    tn = 2048
    return pl.pallas_call(
        _mod_kernel,
        out_shape=jax.ShapeDtypeStruct((depth, rows, n), F32),
        grid=(depth, n // tn),
        in_specs=[
            pl.BlockSpec((rows, d), lambda i, j: (0, 0)),
            pl.BlockSpec((1, d, tn), lambda i, j: (i, 0, j)),
            pl.BlockSpec((1, 1, tn), lambda i, j: (i, 0, j)),
        ],
        out_specs=pl.BlockSpec((1, rows, tn), lambda i, j: (i, 0, j)),
        compiler_params=pltpu.CompilerParams(
            dimension_semantics=("arbitrary", "arbitrary"),
            vmem_limit_bytes=_vmem_limit(2 * d * tn * 4)),
        name="modulation",
    )(cc, mod_w, mod_b.reshape(depth, 1, n))


def _ab_in_kernel(*refs, use_rope, rows):
    (x_ref, sh_ref, sc_ref, w_ref, lng_ref, lnb_ref, ws_ref, bias_ref, qg_ref, kg_ref) = refs[:10]
    if use_rope:
        cos_ref, s1_ref, s2_ref = refs[10:13]
        a_ref, q_ref, k_ref, v_ref = refs[13:]
    else:
        a_ref, q_ref, k_ref, v_ref = refs[10:]

    h = (x_ref[0] * (1.0 + sc_ref[0]) + sh_ref[0]).astype(BF16)

    u = jax.nn.gelu(_dot(h, w_ref[:, 0:GM_WIDTH]))
    vn = _layer_norm(jax.nn.gelu(_dot(h, w_ref[:, GM_WIDTH:2 * GM_WIDTH])), lng_ref[...], lnb_ref[...])
    vb = vn.astype(BF16)
    gd = GM_WIDTH // GM_GROUPS
    for c in range(rows // CHUNK):
        r = slice(c * CHUNK, (c + 1) * CHUNK)
        for g in range(GM_GROUPS):
            l = slice(g * gd, (g + 1) * gd)
            mixed = _dot(ws_ref[g], vb[r, l]) + bias_ref[:, l]
            a_ref[0, r, l] = (u[r, l] * mixed).astype(BF16)

    def rope(z):
        if not use_rope:
            return z
        return (z * cos_ref[...] + pltpu.roll(z, 3 * HEAD_DIM // 4, 1) * s1_ref[...]
                + pltpu.roll(z, HEAD_DIM // 4, 1) * s2_ref[...])

    scale = HEAD_DIM ** -0.5
    pq = _dot(h, w_ref[:, 2 * GM_WIDTH:KV_OFF])
    for hh in range(N_Q_HEADS):
        l = slice(hh * HEAD_DIM, (hh + 1) * HEAD_DIM)
        q_ref[0, :, l] = (rope(_rms_norm(pq[:, l], qg_ref[...])) * scale).astype(BF16)
    pk = _dot(h, w_ref[:, KV_OFF:KV_OFF + KV_W])
    for hh in range(N_KV_HEADS):
        l = slice(hh * HEAD_DIM, (hh + 1) * HEAD_DIM)
        k_ref[0, :, l] = rope(_rms_norm(pk[:, l], kg_ref[...])).astype(BF16)
    v_ref[0] = _dot(h, w_ref[:, KV_OFF + KV_W:KV_OFF + 2 * KV_W]).astype(BF16)


def _ab_in(x, sh, sc, w, lng, lnb, ws, bias2d, qg, kg, rope_tabs, rows):
    bsz, seq, d = x.shape
    n = w.shape[1]
    use_rope = rope_tabs is not None
    in_specs = [
        pl.BlockSpec((1, rows, d), lambda b, t: (b, t, 0)),
        _row_spec(d), _row_spec(d),
        _resident((d, n)),
        _resident((1, GM_WIDTH)), _resident((1, GM_WIDTH)),
        _resident((GM_GROUPS, CHUNK, CHUNK)),
        _resident((CHUNK, GM_WIDTH)),
        _resident((1, HEAD_DIM)), _resident((1, HEAD_DIM)),
    ]
    args = [x, sh, sc, w, lng, lnb, ws, bias2d, qg, kg]
    if use_rope:
        in_specs += [pl.BlockSpec((rows, HEAD_DIM), lambda b, t: (t, 0))] * 3
        args += list(rope_tabs)
    out_shape = [
        jax.ShapeDtypeStruct((bsz, seq, GM_WIDTH), BF16),
        jax.ShapeDtypeStruct((bsz, seq, ATTN_WIDTH), BF16),
        jax.ShapeDtypeStruct((bsz, seq, KV_W), BF16),
        jax.ShapeDtypeStruct((bsz, seq, KV_W), BF16),
    ]
    out_specs = [
        pl.BlockSpec((1, rows, GM_WIDTH), lambda b, t: (b, t, 0)),
        pl.BlockSpec((1, rows, ATTN_WIDTH), lambda b, t: (b, t, 0)),
        pl.BlockSpec((1, rows, KV_W), lambda b, t: (b, t, 0)),
        pl.BlockSpec((1, rows, KV_W), lambda b, t: (b, t, 0)),
    ]
    est = d * n * 2 + 2 * rows * d * 4 + 6 * rows * n * 4
    return pl.pallas_call(
        functools.partial(_ab_in_kernel, use_rope=use_rope, rows=rows),
        out_shape=out_shape,
        grid=(bsz, seq // rows),
        in_specs=in_specs,
        out_specs=out_specs,
        compiler_params=pltpu.CompilerParams(
            dimension_semantics=("arbitrary", "arbitrary"),
            vmem_limit_bytes=_vmem_limit(est)),
        name="ab_in_rope" if use_rope else "ab_in_ctx",
    )(*args)


def _attn_kernel(q_ref, k_ref, v_ref, o_ref, *, rows):
    q = q_ref[0]
    q2 = jnp.concatenate([q[:, 0:HEAD_DIM], q[:, HEAD_DIM:2 * HEAD_DIM]], axis=0)
    s = _dot_nt(q2, k_ref[0])
    m = jnp.max(s, axis=-1, keepdims=True)
    p = jnp.exp(s - m)
    l = jnp.sum(p, axis=-1, keepdims=True)
    o = _dot(p.astype(BF16), v_ref[0]) / l
    o_ref[0, :, 0:HEAD_DIM] = o[0:rows].astype(BF16)
    o_ref[0, :, HEAD_DIM:2 * HEAD_DIM] = o[rows:2 * rows].astype(BF16)


def _attention(q, k, v, rows):
    bsz, sq, _ = q.shape
    sk = k.shape[1]
    gw = (N_Q_HEADS // N_KV_HEADS) * HEAD_DIM
    est = 2 * rows * sk * 12 + 4 * sk * HEAD_DIM * 2 + 4 * rows * gw * 2
    return pl.pallas_call(
        functools.partial(_attn_kernel, rows=rows),
        out_shape=jax.ShapeDtypeStruct((bsz, sq, ATTN_WIDTH), BF16),
        grid=(bsz, N_KV_HEADS, sq // rows),
        in_specs=[
            pl.BlockSpec((1, rows, gw), lambda b, h, t: (b, t, h)),
            pl.BlockSpec((1, sk, HEAD_DIM), lambda b, h, t: (b, 0, h)),
            pl.BlockSpec((1, sk, HEAD_DIM), lambda b, h, t: (b, 0, h)),
        ],
        out_specs=pl.BlockSpec((1, rows, gw), lambda b, h, t: (b, t, h)),
        compiler_params=pltpu.CompilerParams(
            dimension_semantics=("arbitrary", "arbitrary", "arbitrary"),
            vmem_limit_bytes=_vmem_limit(est)),
        name="attention",
    )(q, k, v)


def _out_ln_kernel(a1_ref, a2_ref, w_ref, x_ref, g_ref, lng_ref, lnb_ref, o_ref):
    half = a1_ref.shape[2]
    y = _dot(a1_ref[0], w_ref[0:half, :]) + _dot(a2_ref[0], w_ref[half:2 * half, :])
    z = DN_ALPHA * x_ref[0] + g_ref[0] * y
    o_ref[0] = _layer_norm(z, lng_ref[...], lnb_ref[...])


def _out_ln(a1, a2, w, x, gate, lng, lnb, rows):
    bsz, seq, d = x.shape
    half = a1.shape[2]
    est = 2 * half * d * 2 + 4 * rows * d * 4 + 4 * rows * half * 2 + 2 * rows * d * 4
    return pl.pallas_call(
        _out_ln_kernel,
        out_shape=jax.ShapeDtypeStruct((bsz, seq, d), F32),
        grid=(bsz, seq // rows),
        in_specs=[
            pl.BlockSpec((1, rows, half), lambda b, t: (b, t, 0)),
            pl.BlockSpec((1, rows, half), lambda b, t: (b, t, 0)),
            _resident((2 * half, d)),
            pl.BlockSpec((1, rows, d), lambda b, t: (b, t, 0)),
            _row_spec(d),
            _resident((1, d)), _resident((1, d)),
        ],
        out_specs=pl.BlockSpec((1, rows, d), lambda b, t: (b, t, 0)),
        compiler_params=pltpu.CompilerParams(
            dimension_semantics=("arbitrary", "arbitrary"),
            vmem_limit_bytes=_vmem_limit(est)),
        name="out_ln",
    )(a1, a2, w, x, gate, lng, lnb)


def _ffn_kernel(x_ref, xp_ref, xn_ref, sh_ref, sc_ref, g_ref, wu1_ref, wu2_ref, cw1_ref, cw2_ref,
                cb1_ref, cb2_ref, wd_ref, lng_ref, lnb_ref, o_ref, hm_ref, gt_ref, *, rows):
    _fill_halo_tile(hm_ref, x_ref, xp_ref, xn_ref, sh_ref[0], sc_ref[0], rows)
    n_chunks = gt_ref.shape[0]

    def chunk(c, carry):
        hm = hm_ref[...]
        a1 = _conv3(_dot(hm, wu1_ref[c]), cw1_ref[c], cb1_ref[c], rows)
        a2 = _conv3(_dot(hm, wu2_ref[c]), cw2_ref[c], cb2_ref[c], rows)
        gt_ref[c] = (a1 * (a2 * jax.nn.sigmoid(a2))).astype(BF16)
        return carry

    lax.fori_loop(0, n_chunks, chunk, 0)
    f = _dot(gt_ref[0], wd_ref[0])
    for c in range(1, n_chunks):
        f = f + _dot(gt_ref[c], wd_ref[c])
    z = DN_ALPHA * x_ref[0] + g_ref[0] * f
    o_ref[0] = _layer_norm(z, lng_ref[...], lnb_ref[...])


def _ffn(x, sh, sc, gate, wu1, wu2, cw1, cw2, cb1, cb2, wd, lng, lnb, rows):
    bsz, seq, d = x.shape
    nc = wu1.shape[0]
    est = (3 * nc * d * FF_CHUNK * 2 + 4 * rows * d * 4 + (rows + 2 * HALO) * d * 2
           + nc * rows * FF_CHUNK * 2 + 8 * (rows + 2 * HALO) * FF_CHUNK * 4 + 2 * rows * d * 4)
    return pl.pallas_call(
        functools.partial(_ffn_kernel, rows=rows),
        out_shape=jax.ShapeDtypeStruct((bsz, seq, d), F32),
        grid=(bsz, seq // rows),
        in_specs=_halo_specs(rows, seq, d) + [
            _row_spec(d), _row_spec(d), _row_spec(d),
            _resident((nc, d, FF_CHUNK)), _resident((nc, d, FF_CHUNK)),
            _resident((nc, CONV_W, FF_CHUNK)), _resident((nc, CONV_W, FF_CHUNK)),
            _resident((nc, 1, FF_CHUNK)), _resident((nc, 1, FF_CHUNK)),
            _resident((nc, FF_CHUNK, d)),
            _resident((1, d)), _resident((1, d)),
        ],
        out_specs=pl.BlockSpec((1, rows, d), lambda b, t: (b, t, 0)),
        scratch_shapes=[
            pltpu.VMEM((rows + 2 * HALO, d), BF16),
            pltpu.VMEM((nc, rows, FF_CHUNK), BF16),
        ],
        compiler_params=pltpu.CompilerParams(
            dimension_semantics=("arbitrary", "arbitrary"),
            vmem_limit_bytes=_vmem_limit(est)),
        name="conv_ffn",
    )(x, x, x, sh, sc, gate, wu1, wu2, cw1, cw2, cb1, cb2, wd, lng, lnb)


def _cd_in_kernel(*refs, with_fz, rows):
    x_ref, xp_ref, xn_ref, sh_ref, sc_ref = refs[:5]
    if with_fz:
        wfz_ref, wx_ref, wdt_ref, cw_ref, cb_ref, dtb_ref = refs[5:11]
        f_ref, z_ref, xs_ref, b_ref, c_ref, dt_ref, hm_ref = refs[11:]
    else:
        wx_ref, wdt_ref, cw_ref, cb_ref, dtb_ref = refs[5:10]
        xs_ref, b_ref, c_ref, dt_ref, hm_ref = refs[10:]
    _fill_halo_tile(hm_ref, x_ref, xp_ref, xn_ref, sh_ref[0], sc_ref[0], rows)
    hc = hm_ref[HALO:HALO + rows, :]
    if with_fz:
        pfz = _dot(hc, wfz_ref[...])
        f_ref[0] = pfz[:, 0:FN_WIDTH].astype(BF16)
        z_ref[0] = pfz[:, FN_WIDTH:SSM_OFF]
    a = _conv3(_dot(hm_ref[...], wx_ref[...]), cw_ref[...], cb_ref[...], rows)
    s = a * jax.nn.sigmoid(a)
    xs_ref[0] = s[:, 0:SSD_WIDTH]
    b_ref[0] = s[:, SSD_WIDTH:SSD_WIDTH + SSD_GN].astype(BF16)
    c_ref[0] = s[:, SSD_WIDTH + SSD_GN:SSD_CONV_DIM].astype(BF16)
    dt_ref[0] = _softplus(_dot(hc, wdt_ref[...]) + dtb_ref[...])


def _cd_in(x, sh, sc, wfz, wx, wdt, cw, cb, dtb, rows):
    bsz, seq, d = x.shape
    with_fz = wfz is not None
    in_specs = _halo_specs(rows, seq, d) + [_row_spec(d), _row_spec(d)]
    args = [x, x, x, sh, sc]
    out_shape, out_specs = [], []
    if with_fz:
        in_specs.append(_resident((d, SSM_OFF)))
        args.append(wfz)
        out_shape += [jax.ShapeDtypeStruct((bsz, seq, FN_WIDTH), BF16),
                      jax.ShapeDtypeStruct((bsz, seq, SSD_WIDTH), F32)]
        out_specs += [pl.BlockSpec((1, rows, FN_WIDTH), lambda b, t: (b, t, 0)),
                      pl.BlockSpec((1, rows, SSD_WIDTH), lambda b, t: (b, t, 0))]
    in_specs += [_resident((d, SSD_CONV_DIM)), _resident((d, V7X_LANES)),
                 _resident((CONV_W, SSD_CONV_DIM)), _resident((1, SSD_CONV_DIM)),
                 _resident((1, V7X_LANES))]
    args += [wx, wdt, cw, cb, dtb]
    out_shape += [jax.ShapeDtypeStruct((bsz, seq, SSD_WIDTH), F32),
                  jax.ShapeDtypeStruct((bsz, seq, SSD_GN), BF16),
                  jax.ShapeDtypeStruct((bsz, seq, SSD_GN), BF16),
                  jax.ShapeDtypeStruct((bsz, seq, V7X_LANES), F32)]
    out_specs += [pl.BlockSpec((1, rows, SSD_WIDTH), lambda b, t: (b, t, 0)),
                  pl.BlockSpec((1, rows, SSD_GN), lambda b, t: (b, t, 0)),
                  pl.BlockSpec((1, rows, SSD_GN), lambda b, t: (b, t, 0)),
                  pl.BlockSpec((1, rows, V7X_LANES), lambda b, t: (b, t, 0))]
    est = d * 2200 * 2 + 2 * rows * d * 4 + 8 * (rows + 2 * HALO) * SSD_CONV_DIM * 4
    return pl.pallas_call(
        functools.partial(_cd_in_kernel, with_fz=with_fz, rows=rows),
        out_shape=out_shape,
        grid=(bsz, seq // rows),
        in_specs=in_specs,
        out_specs=out_specs,
        scratch_shapes=[pltpu.VMEM((rows + 2 * HALO, d), BF16)],
        compiler_params=pltpu.CompilerParams(
            dimension_semantics=("arbitrary", "arbitrary"),
            vmem_limit_bytes=_vmem_limit(est)),
        name="cd_in" if with_fz else "cd_in_ctx",
    )(*args)


def _cumsum_rows(z, reverse):
    n = z.shape[0]
    idx = lax.broadcasted_iota(jnp.int32, z.shape, 0)
    k = 1
    while k < n:
        if reverse:
            z = z + jnp.where(idx < n - k, pltpu.roll(z, n - k, 0), 0.0)
        else:
            z = z + jnp.where(idx >= k, pltpu.roll(z, k, 0), 0.0)
        k *= 2
    return z


def _lane_expand(row, lane0, width):
    lane = lax.broadcasted_iota(jnp.int32, (1, SSD_HPG * width), 1)
    out = jnp.zeros((1, SSD_HPG * width), F32)
    for j in range(SSD_HPG):
        val = jnp.broadcast_to(row[:, lane0 + j:lane0 + j + 1], (1, SSD_HPG * width))
        out = jnp.where(lane // width == j, val, out)
    return out


def _ssd_chunk(x, bm, cm, dt, a_row, h_ref, lane0, reverse, need_y):
    q = x.shape[0]
    gw = SSD_HPG * SSD_HEADDIM
    acum = _cumsum_rows(dt * a_row, reverse)
    total = acum[0:1, :] if reverse else acum[q - 1:q, :]
    wgt = dt * jnp.exp(total - acum)
    if need_y:
        acum_t = acum.T
        dt_t = dt.T
        ti = lax.broadcasted_iota(jnp.int32, (q, q), 0)
        si = lax.broadcasted_iota(jnp.int32, (q, q), 1)
        mask = (si >= ti) if reverse else (si <= ti)
        lane_g = lax.broadcasted_iota(jnp.int32, (q, gw), 1) // SSD_HEADDIM
    ys = []
    for g in range(SSD_GROUPS):
        bg = bm[:, g * SSD_STATE:(g + 1) * SSD_STATE]
        xg = x[:, g * gw:(g + 1) * gw]
        hg = h_ref[g]
        if need_y:
            cg = cm[:, g * SSD_STATE:(g + 1) * SSD_STATE]
            cb = _dot_nt(cg, bg)
            cgf = cg.astype(F32)
            xgb = xg.astype(BF16)
            hgb = hg.astype(BF16)
            lhs, rhs = [], []
            for j in range(SSD_HPG):
                jl = lane0 + g * SSD_HPG + j
                col = jnp.broadcast_to(acum[:, jl:jl + 1], (q, q))
                row = jnp.broadcast_to(acum_t[jl:jl + 1, :], (q, q))
                dec = jnp.exp(jnp.where(mask, col - row, -jnp.inf))
                dtr = jnp.broadcast_to(dt_t[jl:jl + 1, :], (q, q))
                lhs.append((cb * dec * dtr).astype(BF16))
                lhs.append((cgf * jnp.exp(col)).astype(BF16))
                rhs.append(jnp.where(lane_g == j, xgb, jnp.zeros_like(xgb)))
                rhs.append(jnp.where(lane_g == j, hgb, jnp.zeros_like(hgb)))
            ys.append(_dot(jnp.concatenate(lhs, axis=1), jnp.concatenate(rhs, axis=0)))
        wexp = jnp.concatenate(
            [jnp.broadcast_to(wgt[:, lane0 + g * SSD_HPG + j:lane0 + g * SSD_HPG + j + 1],
                              (q, SSD_HEADDIM)) for j in range(SSD_HPG)], axis=1)
        bgt = bg.astype(F32).T.astype(BF16)
        st = _dot(bgt, (xg * wexp).astype(BF16))
        h_ref[g] = hg * jnp.exp(_lane_expand(total, lane0 + g * SSD_HPG, SSD_HEADDIM)) + st
    if need_y:
        return jnp.concatenate(ys, axis=1)
    return None


def _ssd_kernel(xc_ref, bc_ref, dtc_ref, xl_ref, bl_ref, cl_ref, dtl_ref, z_ref, alog_ref,
                dskip_ref, ng_ref, o_ref, y_ref, h_ref):
    a_row = -jnp.exp(alog_ref[...])
    nc_ctx = xc_ref.shape[1] // CHUNK
    nc_lat = xl_ref.shape[1] // CHUNK

    def rows_of(c):
        return pl.ds(pl.multiple_of(c * CHUNK, CHUNK), CHUNK)

    def ctx_step(c, lane0, reverse):
        r = rows_of(c)
        _ssd_chunk(xc_ref[0, r, :], bc_ref[0, r, :], None, dtc_ref[0, r, :], a_row, h_ref,
                   lane0, reverse, False)

    def lat_step(c, lane0, reverse):
        r = rows_of(c)
        return r, _ssd_chunk(xl_ref[0, r, :], bl_ref[0, r, :], cl_ref[0, r, :], dtl_ref[0, r, :],
                             a_row, h_ref, lane0, reverse, True)

    h_ref[...] = jnp.zeros_like(h_ref)

    def fwd_ctx(c, carry):
        ctx_step(c, 0, False)
        return carry

    lax.fori_loop(0, nc_ctx, fwd_ctx, 0)

    def fwd_lat(c, carry):
        r, y = lat_step(c, 0, False)
        y_ref[r, :] = y
        return carry

    lax.fori_loop(0, nc_lat, fwd_lat, 0)

    h_ref[...] = jnp.zeros_like(h_ref)

    def bwd_ctx(i, carry):
        ctx_step(nc_ctx - 1 - i, SSD_HEADS, True)
        return carry

    lax.fori_loop(0, nc_ctx, bwd_ctx, 0)

    def bwd_lat(i, carry):
        r, y = lat_step(nc_lat - 1 - i, SSD_HEADS, True)
        zz = z_ref[0, r, :]
        yt = (y_ref[r, :] + y + xl_ref[0, r, :] * dskip_ref[...]) * (zz * jax.nn.sigmoid(zz))
        o_ref[0, r, :] = _rms_norm(yt, ng_ref[...]).astype(BF16)
        return carry

    lax.fori_loop(0, nc_lat, bwd_lat, 0)


def _ssd(xs_c, b_c, dt_c, xs_l, b_l, c_l, dt_l, z, alog_row, dskip_row, ng_row):
    bsz, sl, w = xs_l.shape
    sc = xs_c.shape[1]
    gw = SSD_HPG * SSD_HEADDIM

    def full(s, n):
        return pl.BlockSpec((1, s, n), lambda b: (b, 0, 0))

    est = (2 * (sl + sc) * (w * 4 + 2 * SSD_GN * 2 + V7X_LANES * 4) + 2 * sl * w * 4
           + 2 * sl * w * 2 + sl * w * 4)
    return pl.pallas_call(
        _ssd_kernel,
        out_shape=jax.ShapeDtypeStruct((bsz, sl, w), BF16),
        grid=(bsz,),
        in_specs=[
            full(sc, w), full(sc, SSD_GN), full(sc, V7X_LANES),
            full(sl, w), full(sl, SSD_GN), full(sl, SSD_GN), full(sl, V7X_LANES),
            full(sl, w),
            _resident((1, V7X_LANES)), _resident((1, w)), _resident((1, w)),
        ],
        out_specs=full(sl, w),
        scratch_shapes=[
            pltpu.VMEM((sl, w), F32),
            pltpu.VMEM((SSD_GROUPS, SSD_STATE, gw), F32),
        ],
        compiler_params=pltpu.CompilerParams(
            dimension_semantics=("arbitrary",),
            vmem_limit_bytes=_vmem_limit(est)),
        name="ssd_scan",
    )(xs_c, b_c, dt_c, xs_l, b_l, c_l, dt_l, z, alog_row, dskip_row, ng_row)


def _fourier_kernel(f_ref, cc_ref, sc_ref, w2_ref, o_ref, y_ref, *, rows):
    seq = f_ref.shape[1]
    for g in range(FN_GROUPS):
        l = slice(g * FN_GDIM, (g + 1) * FN_GDIM)
        fg = f_ref[0, :, l]
        y_ref[0:seq, l] = _dot(fg, cc_ref[...]).astype(BF16)
        y_ref[seq:2 * seq, l] = _dot(fg, sc_ref[...]).astype(BF16)
    norm = 1.0 / math.sqrt(seq * FN_GDIM)

    def tile(i, carry):
        r = pl.ds(pl.multiple_of(i * rows, rows), rows)
        o_ref[0, r, :] = (_dot(w2_ref[r, :], y_ref[...]) * norm).astype(BF16)
        return carry

    lax.fori_loop(0, seq // rows, tile, 0)


def _fourier(f, cc, sc, w2, rows):
    bsz, seq, w = f.shape
    est = seq * 2 * seq * 2 + 4 * seq * w * 2 + 2 * seq * w * 2 + 4 * rows * 2 * seq * 2
    return pl.pallas_call(
        functools.partial(_fourier_kernel, rows=rows),
        out_shape=jax.ShapeDtypeStruct((bsz, seq, w), BF16),
        grid=(bsz,),
        in_specs=[
            pl.BlockSpec((1, seq, w), lambda b: (b, 0, 0)),
            _resident((FN_GDIM, FN_GDIM)), _resident((FN_GDIM, FN_GDIM)),
            _resident((seq, 2 * seq)),
        ],
        out_specs=pl.BlockSpec((1, seq, w), lambda b: (b, 0, 0)),
        scratch_shapes=[pltpu.VMEM((2 * seq, w), BF16)],
        compiler_params=pltpu.CompilerParams(
            dimension_semantics=("arbitrary",),
            vmem_limit_bytes=_vmem_limit(est)),
        name="fourier_mix",
    )(f, cc, sc, w2)


def _rope_tables(seq):
    rows = seq // GRID_W
    row = jnp.repeat(jnp.arange(rows, dtype=F32), GRID_W)
    col = jnp.broadcast_to(jnp.arange(GRID_W, dtype=F32), (rows, GRID_W)).reshape(-1)
    n_freq = HEAD_DIM // 4
    inv = ROPE_THETA ** (-jnp.arange(n_freq, dtype=F32) / n_freq)
    ang = jnp.stack([row, col], axis=-1)[:, :, None] * inv
    cos, sin = jnp.cos(ang), jnp.sin(ang)
    zero = jnp.zeros_like(sin[:, 0])
    cos_t = jnp.concatenate([cos[:, 0], cos[:, 0], cos[:, 1], cos[:, 1]], axis=-1)
    s1 = jnp.concatenate([-sin[:, 0], zero, -sin[:, 1], zero], axis=-1)
    s2 = jnp.concatenate([zero, sin[:, 0], zero, sin[:, 1]], axis=-1)
    return cos_t, s1, s2


def _dft_tables(seq):
    def tw(n):
        i = jnp.arange(n, dtype=jnp.int32)
        ang = ((i[:, None] * i[None, :]) % n).astype(F32) * (2.0 * math.pi / n)
        return jnp.cos(ang), jnp.sin(ang)

    cc, sc = tw(FN_GDIM)
    cl, sl = tw(seq)
    return cc.astype(BF16), sc.astype(BF16), jnp.concatenate([cl, -sl], axis=1).astype(BF16)


def _ffn_params(w_up, conv_w, conv_b, w_down):
    d = w_up.shape[0]
    nc = D_FF // FF_CHUNK

    def cols(m, lo):
        r = m.shape[0]
        return m[:, lo:lo + D_FF].reshape(r, nc, FF_CHUNK).transpose(1, 0, 2)

    wu = w_up.astype(BF16)
    cb = conv_b.reshape(1, 2 * D_FF)
    return (cols(wu, 0), cols(wu, D_FF), cols(conv_w, 0), cols(conv_w, D_FF),
            cols(cb, 0), cols(cb, D_FF), w_down.astype(BF16).reshape(nc, FF_CHUNK, d))


def kernel(x, c, ctx, c_ctx, mod_w, mod_b, ln1_g, ln1_b, ln2_g, ln2_b, ffn_up, ffn_conv_w, ffn_conv_b,
           ffn_down, ab_w_in, ab_w_out, gm_ln_g, gm_ln_b, gm_ws, gm_bs, q_norm_g, k_norm_g, cd_w_in,
           cd_w_out, ssd_conv_w, ssd_conv_b, ssd_dt_bias, ssd_a_log, ssd_d, ssd_norm_g):
    bsz, seq, d = x.shape
    ctx_len = ctx.shape[1]
    lat_rows = 256
    ffn_rows = 512
    ctx_rows = min(256, ctx_len)

    pad = (-(bsz + 1)) % V7X_SUBLANES
    cc = jnp.concatenate([c, c_ctx[None], jnp.zeros((pad, d), F32)], axis=0)
    mod = _modulation(cc, mod_w, mod_b)
    rope_tabs = _rope_tables(seq)

    def row(v):
        return v.reshape(1, -1)

    xl, xc = x, ctx
    for i in range(DEPTH):
        last = i == DEPTH - 1
        j = i // 2
        ml = mod[i, :bsz].reshape(bsz, 6, 1, d)
        mc = jnp.broadcast_to(mod[i, bsz].reshape(1, 6, 1, d), (bsz, 6, 1, d))
        sh1, sc1, g1, sh2, sc2, g2 = (ml[:, k] for k in range(6))
        csh1, csc1, cg1, csh2, csc2, cg2 = (mc[:, k] for k in range(6))
        ln1 = (row(ln1_g[i]), row(ln1_b[i]))
        ln2 = (row(ln2_g[i]), row(ln2_b[i]))
        ffn_p = _ffn_params(ffn_up[i], ffn_conv_w[i], ffn_conv_b[i], ffn_down[i])

        if i % 2 == 0:
            w_in = ab_w_in[j].astype(BF16)
            w_out = ab_w_out[j].astype(BF16)
            gm = (row(gm_ln_g[j]), row(gm_ln_b[j]), gm_ws[j].astype(BF16),
                  jnp.repeat(gm_bs[j].T, GM_WIDTH // GM_GROUPS, axis=1),
                  row(q_norm_g[j]), row(k_norm_g[j]))
            a_l, q_l, k_l, v_l = _ab_in(xl, sh1, sc1, w_in, *gm, rope_tabs, lat_rows)
            a_c, q_c, k_c, v_c = _ab_in(xc, csh1, csc1, w_in, *gm, None, ctx_rows)
            o_l = _attention(q_l, jnp.concatenate([k_c, k_l], axis=1),
                             jnp.concatenate([v_c, v_l], axis=1), lat_rows)
            mix_l = (a_l, o_l)
            if not last:
                mix_c = (a_c, _attention(q_c, k_c, v_c, ctx_rows))
        else:
            w_in = cd_w_in[j].astype(BF16)
            w_out = cd_w_out[j].astype(BF16)
            wfz = w_in[:, :SSM_OFF]
            wx = w_in[:, SSM_OFF:SSM_OFF + SSD_CONV_DIM]
            ndt = 2 * SSD_HEADS
            wdt = jnp.pad(w_in[:, SSM_OFF + SSD_CONV_DIM:], ((0, 0), (0, V7X_LANES - ndt)))
            dtb = jnp.pad(ssd_dt_bias[j].reshape(1, ndt), ((0, 0), (0, V7X_LANES - ndt)))
            alog = jnp.pad(ssd_a_log[j].reshape(1, ndt), ((0, 0), (0, V7X_LANES - ndt)))
            cw, cb = ssd_conv_w[j], row(ssd_conv_b[j])
            dskip = row(jnp.repeat(ssd_d[j], SSD_HEADDIM))
            f_l, z_l, xs_l, b_l, c_l, dt_l = _cd_in(xl, sh1, sc1, wfz, wx, wdt, cw, cb, dtb, lat_rows)
            if last:
                xs_c, b_c, _, dt_c = _cd_in(xc, csh1, csc1, None, wx, wdt, cw, cb, dtb, ctx_rows)
            else:
                raise NotImplementedError("odd layers with a context output are not part of this block")
            s_l = _ssd(xs_c, b_c, dt_c, xs_l, b_l, c_l, dt_l, z_l, alog, dskip, row(ssd_norm_g[j]))
            mix_l = (_fourier(f_l, *_dft_tables(seq), 512), s_l)

        xl = _out_ln(*mix_l, w_out, xl, g1, *ln1, 512)
        xl = _ffn(xl, sh2, sc2, g2, *ffn_p, *ln2, ffn_rows)
        if not last:
            xc = _out_ln(*mix_c, w_out, xc, cg1, *ln1, ctx_rows)
            xc = _ffn(xc, csh2, csc2, cg2, *ffn_p, *ln2, ctx_rows)
    return xl
```

```python
import functools
import math

import jax
import jax.numpy as jnp
from jax import lax
from jax.experimental import pallas as pl
from jax.experimental.pallas import tpu as pltpu

D_MODEL = 1024
DEPTH = 2
GRID_W = 64
CHUNK = 128
GM_GROUPS = 4
GM_WIDTH = 512
HEAD_DIM = 128
N_Q_HEADS = 4
N_KV_HEADS = 2
ATTN_WIDTH = 512
KV_W = 256
ROPE_THETA = 10000.0
FN_GROUPS = 4
FN_GDIM = 128
FN_WIDTH = 512
SSD_HEADDIM = 64
SSD_HEADS = 8
SSD_GROUPS = 2
SSD_HPG = 4
SSD_STATE = 128
SSD_WIDTH = 512
SSD_GN = 256
SSD_CONV_DIM = 1024
D_FF = 2816
CONV_W = 3
KV_OFF = 2 * GM_WIDTH + ATTN_WIDTH
SSM_OFF = FN_WIDTH + SSD_WIDTH
DN_ALPHA = (2 * DEPTH) ** 0.25
EPS = 1e-6

V7X_LANES = 128
V7X_SUBLANES = 8
V7X_VMEM_BYTES = 64 * 1024 * 1024
FF_CHUNK = 256

BF16 = jnp.bfloat16
F32 = jnp.float32
HALO = V7X_SUBLANES


def _vmem_limit(nbytes):
    return int(min(nbytes * 3 // 2 + (8 << 20), V7X_VMEM_BYTES - (6 << 20)))


def _resident(shape):
    nd = len(shape)
    return pl.BlockSpec(shape, lambda *_: (0,) * nd, pipeline_mode=pl.Buffered(1))


def _dot(a, b):
    return jnp.dot(a, b, preferred_element_type=F32)


def _dot_nt(a, b):
    return lax.dot_general(a, b, (((1,), (1,)), ((), ())), preferred_element_type=F32)


def _layer_norm(z, g, b):
    mu = jnp.mean(z, axis=-1, keepdims=True)
    zc = z - mu
    var = jnp.mean(zc * zc, axis=-1, keepdims=True)
    return zc * lax.rsqrt(var + EPS) * g + b


def _rms_norm(z, g):
    return z * lax.rsqrt(jnp.mean(z * z, axis=-1, keepdims=True) + EPS) * g


def _softplus(z):
    return jnp.maximum(z, 0.0) + jnp.log1p(jnp.exp(-jnp.abs(z)))


def _conv3(u, w, b, rows):
    n = u.shape[0]
    um = pltpu.roll(u, 1, 0)[HALO:HALO + rows]
    up = pltpu.roll(u, n - 1, 0)[HALO:HALO + rows]
    return b + w[0:1] * um + w[1:2] * u[HALO:HALO + rows] + w[2:3] * up


def _fill_halo_tile(hm_ref, x_ref, xp_ref, xn_ref, sh, sc, rows):
    t = pl.program_id(1)
    nt = pl.num_programs(1)
    keep_prev = jnp.where(t > 0, 1.0, 0.0)
    keep_next = jnp.where(t < nt - 1, 1.0, 0.0)
    hm_ref[HALO:HALO + rows, :] = (x_ref[0] * (1.0 + sc) + sh).astype(BF16)
    hm_ref[0:HALO, :] = ((xp_ref[0] * (1.0 + sc) + sh) * keep_prev).astype(BF16)
    hm_ref[HALO + rows:2 * HALO + rows, :] = ((xn_ref[0] * (1.0 + sc) + sh) * keep_next).astype(BF16)


def _halo_specs(rows, seq, d):
    nb = rows // HALO
    last = seq // HALO - 1
    return [
        pl.BlockSpec((1, rows, d), lambda b, t: (b, t, 0)),
        pl.BlockSpec((1, HALO, d), lambda b, t: (b, jnp.maximum(t * nb - 1, 0), 0)),
        pl.BlockSpec((1, HALO, d), lambda b, t: (b, jnp.minimum((t + 1) * nb, last), 0)),
    ]


def _row_spec(d):
    return pl.BlockSpec((1, 1, d), lambda b, t: (b, 0, 0))


def _mod_kernel(c_ref, w_ref, b_ref, o_ref):
    c = c_ref[...]
    act = c * jax.nn.sigmoid(c)
    o_ref[0] = jnp.dot(act, w_ref[0], preferred_element_type=F32,
                       precision=lax.Precision.HIGHEST) + b_ref[0]


def _modulation(cc, mod_w, mod_b):
    rows, d = cc.shape
    depth, _, n = mod_w.shape
    tn = 2048
    return pl.pallas_call(
        _mod_kernel,
        out_shape=jax.ShapeDtypeStruct((depth, rows, n), F32),
        grid=(depth, n // tn),
        in_specs=[
            pl.BlockSpec((rows, d), lambda i, j: (0, 0)),
            pl.BlockSpec((1, d, tn), lambda i, j: (i, 0, j)),
            pl.BlockSpec((1, 1, tn), lambda i, j: (i, 0, j)),
        ],
        out_specs=pl.BlockSpec((1, rows, tn), lambda i, j: (i, 0, j)),
        compiler_params=pltpu.CompilerParams(
            dimension_semantics=("arbitrary", "arbitrary"),
            vmem_limit_bytes=_vmem_limit(2 * d * tn * 4)),
        name="modulation",
    )(cc, mod_w, mod_b.reshape(depth, 1, n))


def _ab_in_kernel(*refs, use_rope, rows):
    (x_ref, sh_ref, sc_ref, w_ref, lng_ref, lnb_ref, ws_ref, bias_ref, qg_ref, kg_ref) = refs[:10]
    if use_rope:
        cos_ref, s1_ref, s2_ref = refs[10:13]
        a_ref, q_ref, k_ref, v_ref = refs[13:]
    else:
        a_ref, q_ref, k_ref, v_ref = refs[10:]

    h = (x_ref[0] * (1.0 + sc_ref[0]) + sh_ref[0]).astype(BF16)

    u = jax.nn.gelu(_dot(h, w_ref[:, 0:GM_WIDTH]))
    vn = _layer_norm(jax.nn.gelu(_dot(h, w_ref[:, GM_WIDTH:2 * GM_WIDTH])), lng_ref[...], lnb_ref[...])
    vb = vn.astype(BF16)
    gd = GM_WIDTH // GM_GROUPS
    for c in range(rows // CHUNK):
        r = slice(c * CHUNK, (c + 1) * CHUNK)
        for g in range(GM_GROUPS):
            l = slice(g * gd, (g + 1) * gd)
            mixed = _dot(ws_ref[g], vb[r, l]) + bias_ref[:, l]
            a_ref[0, r, l] = (u[r, l] * mixed).astype(BF16)

    def rope(z):
        if not use_rope:
            return z
        return (z * cos_ref[...] + pltpu.roll(z, 3 * HEAD_DIM // 4, 1) * s1_ref[...]
                + pltpu.roll(z, HEAD_DIM // 4, 1) * s2_ref[...])

    scale = HEAD_DIM ** -0.5
    pq = _dot(h, w_ref[:, 2 * GM_WIDTH:KV_OFF])
    for hh in range(N_Q_HEADS):
        l = slice(hh * HEAD_DIM, (hh + 1) * HEAD_DIM)
        q_ref[0, :, l] = (rope(_rms_norm(pq[:, l], qg_ref[...])) * scale).astype(BF16)
    pk = _dot(h, w_ref[:, KV_OFF:KV_OFF + KV_W])
    for hh in range(N_KV_HEADS):
        l = slice(hh * HEAD_DIM, (hh + 1) * HEAD_DIM)
        k_ref[0, :, l] = rope(_rms_norm(pk[:, l], kg_ref[...])).astype(BF16)
    v_ref[0] = _dot(h, w_ref[:, KV_OFF + KV_W:KV_OFF + 2 * KV_W]).astype(BF16)


def _ab_in(x, sh, sc, w, lng, lnb, ws, bias2d, qg, kg, rope_tabs, rows):
    bsz, seq, d = x.shape
    n = w.shape[1]
    use_rope = rope_tabs is not None
    in_specs = [
        pl.BlockSpec((1, rows, d), lambda b, t: (b, t, 0)),
        _row_spec(d), _row_spec(d),
        _resident((d, n)),
        _resident((1, GM_WIDTH)), _resident((1, GM_WIDTH)),
        _resident((GM_GROUPS, CHUNK, CHUNK)),
        _resident((CHUNK, GM_WIDTH)),
        _resident((1, HEAD_DIM)), _resident((1, HEAD_DIM)),
    ]
    args = [x, sh, sc, w, lng, lnb, ws, bias2d, qg, kg]
    if use_rope:
        in_specs += [pl.BlockSpec((rows, HEAD_DIM), lambda b, t: (t, 0))] * 3
        args += list(rope_tabs)
    out_shape = [
        jax.ShapeDtypeStruct((bsz, seq, GM_WIDTH), BF16),
        jax.ShapeDtypeStruct((bsz, seq, ATTN_WIDTH), BF16),
        jax.ShapeDtypeStruct((bsz, seq, KV_W), BF16),
        jax.ShapeDtypeStruct((bsz, seq, KV_W), BF16),
    ]
    out_specs = [
        pl.BlockSpec((1, rows, GM_WIDTH), lambda b, t: (b, t, 0)),
        pl.BlockSpec((1, rows, ATTN_WIDTH), lambda b, t: (b, t, 0)),
        pl.BlockSpec((1, rows, KV_W), lambda b, t: (b, t, 0)),
        pl.BlockSpec((1, rows, KV_W), lambda b, t: (b, t, 0)),
    ]
    est = d * n * 2 + 2 * rows * d * 4 + 6 * rows * n * 4
    return pl.pallas_call(
        functools.partial(_ab_in_kernel, use_rope=use_rope, rows=rows),
        out_shape=out_shape,
        grid=(bsz, seq // rows),
        in_specs=in_specs,
        out_specs=out_specs,
        compiler_params=pltpu.CompilerParams(
            dimension_semantics=("arbitrary", "arbitrary"),
            vmem_limit_bytes=_vmem_limit(est)),
        name="ab_in_rope" if use_rope else "ab_in_ctx",
    )(*args)


def _attn_kernel(q_ref, k_ref, v_ref, o_ref, *, rows, blk):
    k = k_ref[0]
    v = v_ref[0]
    for h in range(N_Q_HEADS // N_KV_HEADS):
        l_ = slice(h * HEAD_DIM, (h + 1) * HEAD_DIM)
        for r0 in range(0, rows, blk):
            s = _dot_nt(q_ref[0, r0:r0 + blk, l_], k)
            m = jnp.max(s, axis=-1, keepdims=True)
            p = jnp.exp(s - m)
            l = jnp.sum(p, axis=-1, keepdims=True)
            o_ref[0, r0:r0 + blk, l_] = (_dot(p.astype(BF16), v) / l).astype(BF16)


def _attention(q, k, v, rows, blk=128):
    bsz, sq, _ = q.shape
    sk = k.shape[1]
    gw = (N_Q_HEADS // N_KV_HEADS) * HEAD_DIM
    est = 4 * blk * sk * 12 + 4 * sk * HEAD_DIM * 2 + 4 * rows * gw * 2
    return pl.pallas_call(
        functools.partial(_attn_kernel, rows=rows, blk=min(blk, rows)),
        out_shape=jax.ShapeDtypeStruct((bsz, sq, ATTN_WIDTH), BF16),
        grid=(bsz, N_KV_HEADS, sq // rows),
        in_specs=[
            pl.BlockSpec((1, rows, gw), lambda b, h, t: (b, t, h)),
            pl.BlockSpec((1, sk, HEAD_DIM), lambda b, h, t: (b, 0, h)),
            pl.BlockSpec((1, sk, HEAD_DIM), lambda b, h, t: (b, 0, h)),
        ],
        out_specs=pl.BlockSpec((1, rows, gw), lambda b, h, t: (b, t, h)),
        compiler_params=pltpu.CompilerParams(
            dimension_semantics=("arbitrary", "arbitrary", "arbitrary"),
            vmem_limit_bytes=_vmem_limit(est)),
        name="attention",
    )(q, k, v)


def _out_ln_kernel(a1_ref, a2_ref, w_ref, x_ref, g_ref, lng_ref, lnb_ref, o_ref):
    half = a1_ref.shape[2]
    y = _dot(a1_ref[0], w_ref[0:half, :]) + _dot(a2_ref[0], w_ref[half:2 * half, :])
    z = DN_ALPHA * x_ref[0] + g_ref[0] * y
    o_ref[0] = _layer_norm(z, lng_ref[...], lnb_ref[...])


def _out_ln(a1, a2, w, x, gate, lng, lnb, rows):
    bsz, seq, d = x.shape
    half = a1.shape[2]
    est = 2 * half * d * 2 + 4 * rows * d * 4 + 4 * rows * half * 2 + 2 * rows * d * 4
    return pl.pallas_call(
        _out_ln_kernel,
        out_shape=jax.ShapeDtypeStruct((bsz, seq, d), F32),
        grid=(bsz, seq // rows),
        in_specs=[
            pl.BlockSpec((1, rows, half), lambda b, t: (b, t, 0)),
            pl.BlockSpec((1, rows, half), lambda b, t: (b, t, 0)),
            _resident((2 * half, d)),
            pl.BlockSpec((1, rows, d), lambda b, t: (b, t, 0)),
            _row_spec(d),
            _resident((1, d)), _resident((1, d)),
        ],
        out_specs=pl.BlockSpec((1, rows, d), lambda b, t: (b, t, 0)),
        compiler_params=pltpu.CompilerParams(
            dimension_semantics=("arbitrary", "arbitrary"),
            vmem_limit_bytes=_vmem_limit(est)),
        name="out_ln",
    )(a1, a2, w, x, gate, lng, lnb)


def _ffn_kernel(x_ref, xp_ref, xn_ref, sh_ref, sc_ref, g_ref, wu1_ref, wu2_ref, cw1_ref, cw2_ref,
                cb1_ref, cb2_ref, wd_ref, lng_ref, lnb_ref, o_ref, hm_ref, u_ref, gt_ref, f_ref, *, rows):
    _fill_halo_tile(hm_ref, x_ref, xp_ref, xn_ref, sh_ref[0], sc_ref[0], rows)
    n_chunks = wu1_ref.shape[0]
    tiles = rows // V7X_SUBLANES
    sub = lax.broadcasted_iota(jnp.int32, (tiles, V7X_SUBLANES, FF_CHUNK), 1)

    def up_proj(c, slot):
        hm = hm_ref[...]
        u_ref[slot, 0] = _dot(hm, wu1_ref[c]).reshape(tiles + 2, V7X_SUBLANES, FF_CHUNK)
        u_ref[slot, 1] = _dot(hm, wu2_ref[c]).reshape(tiles + 2, V7X_SUBLANES, FF_CHUNK)

    def conv_gate(c, slot):
        def conv(half, cw_ref, cb_ref):
            u = u_ref[slot, half]
            dn = pltpu.roll(u, 1, 1)
            up = pltpu.roll(u, V7X_SUBLANES - 1, 1)
            prev = jnp.where(sub == 0, dn[0:tiles], dn[1:tiles + 1])
            nxt = jnp.where(sub == V7X_SUBLANES - 1, up[2:tiles + 2], up[1:tiles + 1])
            w = cw_ref[c]
            return cb_ref[c] + w[0:1] * prev + w[1:2] * u[1:tiles + 1] + w[2:3] * nxt

        a1 = conv(0, cw1_ref, cb1_ref)
        a2 = conv(1, cw2_ref, cb2_ref)
        gt_ref[slot] = (a1 * (a2 * jax.nn.sigmoid(a2))).reshape(rows, FF_CHUNK)

    def down_proj(c, slot):
        f_ref[...] += _dot(gt_ref[slot].astype(BF16), wd_ref[c])

    f_ref[...] = jnp.zeros_like(f_ref)
    up_proj(0, 0)
    up_proj(1, 1)
    conv_gate(0, 0)

    def pair(k, carry):
        c = 2 * k + 1
        up_proj(c + 1, 0)
        conv_gate(c, 1)
        down_proj(c - 1, 0)
        up_proj(c + 2, 1)
        conv_gate(c + 1, 0)
        down_proj(c, 1)
        return carry

    lax.fori_loop(0, (n_chunks - 3) // 2, pair, 0)
    up_proj(n_chunks - 1, 0)
    conv_gate(n_chunks - 2, 1)
    down_proj(n_chunks - 3, 0)
    conv_gate(n_chunks - 1, 0)
    down_proj(n_chunks - 2, 1)
    down_proj(n_chunks - 1, 0)
    z = DN_ALPHA * x_ref[0] + g_ref[0] * f_ref[...]
    o_ref[0] = _layer_norm(z, lng_ref[...], lnb_ref[...])


def _ffn(x, sh, sc, gate, wu1, wu2, cw1, cw2, cb1, cb2, wd, lng, lnb, rows):
    bsz, seq, d = x.shape
    nc = wu1.shape[0]
    assert nc % 2 == 1 and nc >= 3, "the pipeline peels three chunks and pairs the rest"
    est = (3 * nc * d * FF_CHUNK * 2 + 4 * rows * d * 4 + (rows + 2 * HALO) * d * 2
           + nc * rows * FF_CHUNK * 2 + 8 * (rows + 2 * HALO) * FF_CHUNK * 4 + 2 * rows * d * 4)
    return pl.pallas_call(
        functools.partial(_ffn_kernel, rows=rows),
        out_shape=jax.ShapeDtypeStruct((bsz, seq, d), F32),
        grid=(bsz, seq // rows),
        in_specs=_halo_specs(rows, seq, d) + [
            _row_spec(d), _row_spec(d), _row_spec(d),
            _resident((nc, d, FF_CHUNK)), _resident((nc, d, FF_CHUNK)),
            _resident((nc, CONV_W, FF_CHUNK)), _resident((nc, CONV_W, FF_CHUNK)),
            _resident((nc, 1, FF_CHUNK)), _resident((nc, 1, FF_CHUNK)),
            _resident((nc, FF_CHUNK, d)),
            _resident((1, d)), _resident((1, d)),
        ],
        out_specs=pl.BlockSpec((1, rows, d), lambda b, t: (b, t, 0)),
        scratch_shapes=[
            pltpu.VMEM((rows + 2 * HALO, d), BF16),
            pltpu.VMEM((2, 2, rows // HALO + 2, HALO, FF_CHUNK), F32),
            pltpu.VMEM((2, rows, FF_CHUNK), F32),
            pltpu.VMEM((rows, d), F32),
        ],
        compiler_params=pltpu.CompilerParams(
            dimension_semantics=("arbitrary", "arbitrary"),
            vmem_limit_bytes=_vmem_limit(est)),
        name="conv_ffn",
    )(x, x, x, sh, sc, gate, wu1, wu2, cw1, cw2, cb1, cb2, wd, lng, lnb)


def _cd_in_kernel(*refs, with_fz, rows):
    x_ref, xp_ref, xn_ref, sh_ref, sc_ref = refs[:5]
    if with_fz:
        wfz_ref, wx_ref, wdt_ref, cw_ref, cb_ref, dtb_ref = refs[5:11]
        f_ref, z_ref, xs_ref, b_ref, c_ref, dt_ref, hm_ref = refs[11:]
    else:
        wx_ref, wdt_ref, cw_ref, cb_ref, dtb_ref = refs[5:10]
        xs_ref, b_ref, c_ref, dt_ref, hm_ref = refs[10:]
    _fill_halo_tile(hm_ref, x_ref, xp_ref, xn_ref, sh_ref[0], sc_ref[0], rows)
    hc = hm_ref[HALO:HALO + rows, :]
    if with_fz:
        pfz = _dot(hc, wfz_ref[...])
        f_ref[0] = pfz[:, 0:FN_WIDTH].astype(BF16)
        z_ref[0] = pfz[:, FN_WIDTH:SSM_OFF]
    a = _conv3(_dot(hm_ref[...], wx_ref[...]), cw_ref[...], cb_ref[...], rows)
    s = a * jax.nn.sigmoid(a)
    xs_ref[0] = s[:, 0:SSD_WIDTH]
    b_ref[0] = s[:, SSD_WIDTH:SSD_WIDTH + SSD_GN].astype(BF16)
    c_ref[0] = s[:, SSD_WIDTH + SSD_GN:SSD_CONV_DIM].astype(BF16)
    dt_ref[0] = _softplus(_dot(hc, wdt_ref[...]) + dtb_ref[...])


def _cd_in(x, sh, sc, wfz, wx, wdt, cw, cb, dtb, rows):
    bsz, seq, d = x.shape
    with_fz = wfz is not None
    in_specs = _halo_specs(rows, seq, d) + [_row_spec(d), _row_spec(d)]
    args = [x, x, x, sh, sc]
    out_shape, out_specs = [], []
    if with_fz:
        in_specs.append(_resident((d, SSM_OFF)))
        args.append(wfz)
        out_shape += [jax.ShapeDtypeStruct((bsz, seq, FN_WIDTH), BF16),
                      jax.ShapeDtypeStruct((bsz, seq, SSD_WIDTH), F32)]
        out_specs += [pl.BlockSpec((1, rows, FN_WIDTH), lambda b, t: (b, t, 0)),
                      pl.BlockSpec((1, rows, SSD_WIDTH), lambda b, t: (b, t, 0))]
    in_specs += [_resident((d, SSD_CONV_DIM)), _resident((d, V7X_LANES)),
                 _resident((CONV_W, SSD_CONV_DIM)), _resident((1, SSD_CONV_DIM)),
                 _resident((1, V7X_LANES))]
    args += [wx, wdt, cw, cb, dtb]
    out_shape += [jax.ShapeDtypeStruct((bsz, seq, SSD_WIDTH), F32),
                  jax.ShapeDtypeStruct((bsz, seq, SSD_GN), BF16),
                  jax.ShapeDtypeStruct((bsz, seq, SSD_GN), BF16),
                  jax.ShapeDtypeStruct((bsz, seq, V7X_LANES), F32)]
    out_specs += [pl.BlockSpec((1, rows, SSD_WIDTH), lambda b, t: (b, t, 0)),
                  pl.BlockSpec((1, rows, SSD_GN), lambda b, t: (b, t, 0)),
                  pl.BlockSpec((1, rows, SSD_GN), lambda b, t: (b, t, 0)),
                  pl.BlockSpec((1, rows, V7X_LANES), lambda b, t: (b, t, 0))]
    est = d * 2200 * 2 + 2 * rows * d * 4 + 8 * (rows + 2 * HALO) * SSD_CONV_DIM * 4
    return pl.pallas_call(
        functools.partial(_cd_in_kernel, with_fz=with_fz, rows=rows),
        out_shape=out_shape,
        grid=(bsz, seq // rows),
        in_specs=in_specs,
        out_specs=out_specs,
        scratch_shapes=[pltpu.VMEM((rows + 2 * HALO, d), BF16)],
        compiler_params=pltpu.CompilerParams(
            dimension_semantics=("arbitrary", "arbitrary"),
            vmem_limit_bytes=_vmem_limit(est)),
        name="cd_in" if with_fz else "cd_in_ctx",
    )(*args)


def _cumsum_rows(z, reverse):
    n = z.shape[0]
    idx = lax.broadcasted_iota(jnp.int32, z.shape, 0)
    k = 1
    while k < n:
        if reverse:
            z = z + jnp.where(idx < n - k, pltpu.roll(z, n - k, 0), 0.0)
        else:
            z = z + jnp.where(idx >= k, pltpu.roll(z, k, 0), 0.0)
        k *= 2
    return z


def _lane_expand(row, lane0, width):
    lane = lax.broadcasted_iota(jnp.int32, (1, SSD_HPG * width), 1)
    out = jnp.zeros((1, SSD_HPG * width), F32)
    for j in range(SSD_HPG):
        val = jnp.broadcast_to(row[:, lane0 + j:lane0 + j + 1], (1, SSD_HPG * width))
        out = jnp.where(lane // width == j, val, out)
    return out


def _ssd_chunk(x, bm, cm, dt, a_row, h_ref, lane0, reverse, need_y):
    q = x.shape[0]
    gw = SSD_HPG * SSD_HEADDIM
    acum = _cumsum_rows(dt * a_row, reverse)
    total = acum[0:1, :] if reverse else acum[q - 1:q, :]
    acum_t = acum.T
    dt_t = dt.T
    total_t = jnp.broadcast_to(acum_t[:, 0:1] if reverse else acum_t[:, q - 1:q], (q, q))
    wgt_t = dt_t * jnp.exp(total_t - acum_t)
    lane_g = lax.broadcasted_iota(jnp.int32, (q, gw), 1) // SSD_HEADDIM
    if need_y:
        ti = lax.broadcasted_iota(jnp.int32, (q, q), 0)
        si = lax.broadcasted_iota(jnp.int32, (q, q), 1)
        mask = (si >= ti) if reverse else (si <= ti)
    ys = []
    for g in range(SSD_GROUPS):
        bg = bm[:, g * SSD_STATE:(g + 1) * SSD_STATE]
        xgb = x[:, g * gw:(g + 1) * gw].astype(BF16)
        x_heads = [jnp.where(lane_g == j, xgb, jnp.zeros_like(xgb)) for j in range(SSD_HPG)]
        hg = h_ref[g]
        if need_y:
            cg = cm[:, g * SSD_STATE:(g + 1) * SSD_STATE]
            cb = _dot_nt(cg, bg)
            cgf = cg.astype(F32)
            hgb = hg.astype(BF16)
            lhs, rhs = [], []
            for j in range(SSD_HPG):
                jl = lane0 + g * SSD_HPG + j
                col = jnp.broadcast_to(acum[:, jl:jl + 1], (q, q))
                row = jnp.broadcast_to(acum_t[jl:jl + 1, :], (q, q))
                dec = jnp.exp(jnp.where(mask, col - row, -jnp.inf))
                dtr = jnp.broadcast_to(dt_t[jl:jl + 1, :], (q, q))
                lhs.append((cb * dec * dtr).astype(BF16))
                lhs.append((cgf * jnp.exp(col)).astype(BF16))
                rhs.append(x_heads[j])
                rhs.append(jnp.where(lane_g == j, hgb, jnp.zeros_like(hgb)))
            ys.append(_dot(jnp.concatenate(lhs, axis=1), jnp.concatenate(rhs, axis=0)))
        bgt = bg.astype(F32).T
        wb = [(bgt * jnp.broadcast_to(wgt_t[lane0 + g * SSD_HPG + j:lane0 + g * SSD_HPG + j + 1, :],
                                      (SSD_STATE, q))).astype(BF16) for j in range(SSD_HPG)]
        st = _dot(jnp.concatenate(wb, axis=1), jnp.concatenate(x_heads, axis=0))
        h_ref[g] = hg * jnp.exp(_lane_expand(total, lane0 + g * SSD_HPG, SSD_HEADDIM)) + st
    if need_y:
        return jnp.concatenate(ys, axis=1)
    return None


def _ssd_kernel(xc_ref, bc_ref, dtc_ref, xl_ref, bl_ref, cl_ref, dtl_ref, z_ref, alog_ref,
                dskip_ref, ng_ref, o_ref, y_ref, h_ref):
    a_row = -jnp.exp(alog_ref[...])
    nc_ctx = xc_ref.shape[1] // CHUNK
    nc_lat = xl_ref.shape[1] // CHUNK

    def rows_of(c):
        return pl.ds(pl.multiple_of(c * CHUNK, CHUNK), CHUNK)

    def ctx_step(c, direction):
        r = rows_of(c)
        _ssd_chunk(xc_ref[0, r, :], bc_ref[0, r, :], None, dtc_ref[0, r, :], a_row,
                   h_ref.at[direction], direction * SSD_HEADS, direction == 1, False)

    def lat_step(c, direction):
        r = rows_of(c)
        y_ref[direction, r, :] = _ssd_chunk(
            xl_ref[0, r, :], bl_ref[0, r, :], cl_ref[0, r, :], dtl_ref[0, r, :], a_row,
            h_ref.at[direction], direction * SSD_HEADS, direction == 1, True)

    h_ref[...] = jnp.zeros_like(h_ref)

    def ctx_pair(i, carry):
        ctx_step(i, 0)
        ctx_step(nc_ctx - 1 - i, 1)
        return carry

    lax.fori_loop(0, nc_ctx, ctx_pair, 0)

    def lat_pair(i, carry):
        lat_step(i, 0)
        lat_step(nc_lat - 1 - i, 1)
        return carry

    lax.fori_loop(0, nc_lat, lat_pair, 0)

    def finish(c, carry):
        r = rows_of(c)
        zz = z_ref[0, r, :]
        yt = (y_ref[0, r, :] + y_ref[1, r, :] + xl_ref[0, r, :] * dskip_ref[...]) * (zz * jax.nn.sigmoid(zz))
        o_ref[0, r, :] = _rms_norm(yt, ng_ref[...]).astype(BF16)
        return carry

    lax.fori_loop(0, nc_lat, finish, 0)


def _ssd(xs_c, b_c, dt_c, xs_l, b_l, c_l, dt_l, z, alog_row, dskip_row, ng_row):
    bsz, sl, w = xs_l.shape
    sc = xs_c.shape[1]
    gw = SSD_HPG * SSD_HEADDIM

    def full(s, n):
        return pl.BlockSpec((1, s, n), lambda b: (b, 0, 0))

    est = (2 * (sl + sc) * (w * 4 + 2 * SSD_GN * 2 + V7X_LANES * 4) + 2 * sl * w * 4
           + 2 * sl * w * 2 + 2 * sl * w * 4)
    return pl.pallas_call(
        _ssd_kernel,
        out_shape=jax.ShapeDtypeStruct((bsz, sl, w), BF16),
        grid=(bsz,),
        in_specs=[
            full(sc, w), full(sc, SSD_GN), full(sc, V7X_LANES),
            full(sl, w), full(sl, SSD_GN), full(sl, SSD_GN), full(sl, V7X_LANES),
            full(sl, w),
            _resident((1, V7X_LANES)), _resident((1, w)), _resident((1, w)),
        ],
        out_specs=full(sl, w),
        scratch_shapes=[
            pltpu.VMEM((2, sl, w), F32),
            pltpu.VMEM((2, SSD_GROUPS, SSD_STATE, gw), F32),
        ],
        compiler_params=pltpu.CompilerParams(
            dimension_semantics=("arbitrary",),
            vmem_limit_bytes=_vmem_limit(est)),
        name="ssd_scan",
    )(xs_c, b_c, dt_c, xs_l, b_l, c_l, dt_l, z, alog_row, dskip_row, ng_row)


def _fourier_kernel(f_ref, cc_ref, sc_ref, w2_ref, o_ref, y_ref, *, rows):
    seq = f_ref.shape[1]
    for g in range(FN_GROUPS):
        l = slice(g * FN_GDIM, (g + 1) * FN_GDIM)
        fg = f_ref[0, :, l]
        y_ref[0:seq, l] = _dot(fg, cc_ref[...]).astype(BF16)
        y_ref[seq:2 * seq, l] = _dot(fg, sc_ref[...]).astype(BF16)
    norm = 1.0 / math.sqrt(seq * FN_GDIM)

    def tile(i, carry):
        r = pl.ds(pl.multiple_of(i * rows, rows), rows)
        o_ref[0, r, :] = (_dot(w2_ref[r, :], y_ref[...]) * norm).astype(BF16)
        return carry

    lax.fori_loop(0, seq // rows, tile, 0)


def _fourier(f, cc, sc, w2, rows):
    bsz, seq, w = f.shape
    est = seq * 2 * seq * 2 + 4 * seq * w * 2 + 2 * seq * w * 2 + 4 * rows * 2 * seq * 2
    return pl.pallas_call(
        functools.partial(_fourier_kernel, rows=rows),
        out_shape=jax.ShapeDtypeStruct((bsz, seq, w), BF16),
        grid=(bsz,),
        in_specs=[
            pl.BlockSpec((1, seq, w), lambda b: (b, 0, 0)),
            _resident((FN_GDIM, FN_GDIM)), _resident((FN_GDIM, FN_GDIM)),
            _resident((seq, 2 * seq)),
        ],
        out_specs=pl.BlockSpec((1, seq, w), lambda b: (b, 0, 0)),
        scratch_shapes=[pltpu.VMEM((2 * seq, w), BF16)],
        compiler_params=pltpu.CompilerParams(
            dimension_semantics=("arbitrary",),
            vmem_limit_bytes=_vmem_limit(est)),
        name="fourier_mix",
    )(f, cc, sc, w2)


def _rope_tables(seq):
    rows = seq // GRID_W
    row = jnp.repeat(jnp.arange(rows, dtype=F32), GRID_W)
    col = jnp.broadcast_to(jnp.arange(GRID_W, dtype=F32), (rows, GRID_W)).reshape(-1)
    n_freq = HEAD_DIM // 4
    inv = ROPE_THETA ** (-jnp.arange(n_freq, dtype=F32) / n_freq)
    ang = jnp.stack([row, col], axis=-1)[:, :, None] * inv
    cos, sin = jnp.cos(ang), jnp.sin(ang)
    zero = jnp.zeros_like(sin[:, 0])
    cos_t = jnp.concatenate([cos[:, 0], cos[:, 0], cos[:, 1], cos[:, 1]], axis=-1)
    s1 = jnp.concatenate([-sin[:, 0], zero, -sin[:, 1], zero], axis=-1)
    s2 = jnp.concatenate([zero, sin[:, 0], zero, sin[:, 1]], axis=-1)
    return cos_t, s1, s2


def _dft_tables(seq):
    def tw(n):
        i = jnp.arange(n, dtype=jnp.int32)
        ang = ((i[:, None] * i[None, :]) % n).astype(F32) * (2.0 * math.pi / n)
        return jnp.cos(ang), jnp.sin(ang)

    cc, sc = tw(FN_GDIM)
    cl, sl = tw(seq)
    return cc.astype(BF16), sc.astype(BF16), jnp.concatenate([cl, -sl], axis=1).astype(BF16)


def _ffn_params(w_up, conv_w, conv_b, w_down):
    d = w_up.shape[0]
    nc = D_FF // FF_CHUNK

    def cols(m, lo):
        r = m.shape[0]
        return m[:, lo:lo + D_FF].reshape(r, nc, FF_CHUNK).transpose(1, 0, 2)

    wu = w_up.astype(BF16)
    cb = conv_b.reshape(1, 2 * D_FF)
    return (cols(wu, 0), cols(wu, D_FF), cols(conv_w, 0), cols(conv_w, D_FF),
            cols(cb, 0), cols(cb, D_FF), w_down.astype(BF16).reshape(nc, FF_CHUNK, d))


def kernel(x, c, ctx, c_ctx, mod_w, mod_b, ln1_g, ln1_b, ln2_g, ln2_b, ffn_up, ffn_conv_w, ffn_conv_b,
           ffn_down, ab_w_in, ab_w_out, gm_ln_g, gm_ln_b, gm_ws, gm_bs, q_norm_g, k_norm_g, cd_w_in,
           cd_w_out, ssd_conv_w, ssd_conv_b, ssd_dt_bias, ssd_a_log, ssd_d, ssd_norm_g):
    bsz, seq, d = x.shape
    ctx_len = ctx.shape[1]
    lat_rows = 256
    ffn_rows = 512
    ctx_rows = min(256, ctx_len)

    pad = (-(bsz + 1)) % V7X_SUBLANES
    cc = jnp.concatenate([c, c_ctx[None], jnp.zeros((pad, d), F32)], axis=0)
    mod = _modulation(cc, mod_w, mod_b)
    rope_tabs = _rope_tables(seq)

    def row(v):
        return v.reshape(1, -1)

    xl, xc = x, ctx
    for i in range(DEPTH):
        last = i == DEPTH - 1
        j = i // 2
        ml = mod[i, :bsz].reshape(bsz, 6, 1, d)
        mc = jnp.broadcast_to(mod[i, bsz].reshape(1, 6, 1, d), (bsz, 6, 1, d))
        sh1, sc1, g1, sh2, sc2, g2 = (ml[:, k] for k in range(6))
        csh1, csc1, cg1, csh2, csc2, cg2 = (mc[:, k] for k in range(6))
        ln1 = (row(ln1_g[i]), row(ln1_b[i]))
        ln2 = (row(ln2_g[i]), row(ln2_b[i]))
        ffn_p = _ffn_params(ffn_up[i], ffn_conv_w[i], ffn_conv_b[i], ffn_down[i])

        if i % 2 == 0:
            w_in = ab_w_in[j].astype(BF16)
            w_out = ab_w_out[j].astype(BF16)
            gm = (row(gm_ln_g[j]), row(gm_ln_b[j]), gm_ws[j].astype(BF16),
                  jnp.repeat(gm_bs[j].T, GM_WIDTH // GM_GROUPS, axis=1),
                  row(q_norm_g[j]), row(k_norm_g[j]))
            a_l, q_l, k_l, v_l = _ab_in(xl, sh1, sc1, w_in, *gm, rope_tabs, lat_rows)
            a_c, q_c, k_c, v_c = _ab_in(xc, csh1, csc1, w_in, *gm, None, ctx_rows)
            o_l = _attention(q_l, jnp.concatenate([k_c, k_l], axis=1),
                             jnp.concatenate([v_c, v_l], axis=1), 512)
            mix_l = (a_l, o_l)
            if not last:
                mix_c = (a_c, _attention(q_c, k_c, v_c, ctx_rows))
        else:
            w_in = cd_w_in[j].astype(BF16)
            w_out = cd_w_out[j].astype(BF16)
            wfz = w_in[:, :SSM_OFF]
            wx = w_in[:, SSM_OFF:SSM_OFF + SSD_CONV_DIM]
            ndt = 2 * SSD_HEADS
            wdt = jnp.pad(w_in[:, SSM_OFF + SSD_CONV_DIM:], ((0, 0), (0, V7X_LANES - ndt)))
            dtb = jnp.pad(ssd_dt_bias[j].reshape(1, ndt), ((0, 0), (0, V7X_LANES - ndt)))
            alog = jnp.pad(ssd_a_log[j].reshape(1, ndt), ((0, 0), (0, V7X_LANES - ndt)))
            cw, cb = ssd_conv_w[j], row(ssd_conv_b[j])
            dskip = row(jnp.repeat(ssd_d[j], SSD_HEADDIM))
            f_l, z_l, xs_l, b_l, c_l, dt_l = _cd_in(xl, sh1, sc1, wfz, wx, wdt, cw, cb, dtb, lat_rows)
            if last:
                xs_c, b_c, _, dt_c = _cd_in(xc, csh1, csc1, None, wx, wdt, cw, cb, dtb, ctx_rows)
            else:
                raise NotImplementedError("odd layers with a context output are not part of this block")
            s_l = _ssd(xs_c, b_c, dt_c, xs_l, b_l, c_l, dt_l, z_l, alog, dskip, row(ssd_norm_g[j]))
            mix_l = (_fourier(f_l, *_dft_tables(seq), 512), s_l)

        xl = _out_ln(*mix_l, w_out, xl, g1, *ln1, 512)
        xl = _ffn(xl, sh2, sc2, g2, *ffn_p, *ln2, ffn_rows)
        if not last:
            xc = _out_ln(*mix_c, w_out, xc, cg1, *ln1, ctx_rows)
            xc = _ffn(xc, csh2, csc2, cg2, *ffn_p, *ln2, ctx_rows)
    return xl
```

```python
import functools
import math

import jax
import jax.numpy as jnp
from jax import lax
from jax.experimental import pallas as pl
from jax.experimental.pallas import tpu as pltpu

D_MODEL = 1024
DEPTH = 2
GRID_W = 64
CHUNK = 128
GM_GROUPS = 4
GM_WIDTH = 512
HEAD_DIM = 128
N_Q_HEADS = 4
N_KV_HEADS = 2
ATTN_WIDTH = 512
KV_W = 256
ROPE_THETA = 10000.0
FN_GROUPS = 4
FN_GDIM = 128
FN_WIDTH = 512
SSD_HEADDIM = 64
SSD_HEADS = 8
SSD_GROUPS = 2
SSD_HPG = 4
SSD_STATE = 128
SSD_WIDTH = 512
SSD_GN = 256
SSD_CONV_DIM = 1024
D_FF = 2816
CONV_W = 3
KV_OFF = 2 * GM_WIDTH + ATTN_WIDTH
SSM_OFF = FN_WIDTH + SSD_WIDTH
DN_ALPHA = (2 * DEPTH) ** 0.25
EPS = 1e-6

V7X_LANES = 128
V7X_SUBLANES = 8
V7X_VMEM_BYTES = 64 * 1024 * 1024
FF_CHUNK = 256

BF16 = jnp.bfloat16
F32 = jnp.float32
HALO = V7X_SUBLANES


def _vmem_limit(nbytes):
    return int(min(nbytes * 3 // 2 + (8 << 20), V7X_VMEM_BYTES - (6 << 20)))


def _resident(shape):
    nd = len(shape)
    return pl.BlockSpec(shape, lambda *_: (0,) * nd, pipeline_mode=pl.Buffered(1))


def _dot(a, b):
    return jnp.dot(a, b, preferred_element_type=F32)


def _dot_nt(a, b):
    return lax.dot_general(a, b, (((1,), (1,)), ((), ())), preferred_element_type=F32)


def _layer_norm(z, g, b):
    mu = jnp.mean(z, axis=-1, keepdims=True)
    zc = z - mu
    var = jnp.mean(zc * zc, axis=-1, keepdims=True)
    return zc * lax.rsqrt(var + EPS) * g + b


def _rms_norm(z, g):
    return z * lax.rsqrt(jnp.mean(z * z, axis=-1, keepdims=True) + EPS) * g


def _softplus(z):
    return jnp.maximum(z, 0.0) + jnp.log1p(jnp.exp(-jnp.abs(z)))


def _conv3(u, w, b, rows):
    n = u.shape[0]
    um = pltpu.roll(u, 1, 0)[HALO:HALO + rows]
    up = pltpu.roll(u, n - 1, 0)[HALO:HALO + rows]
    return b + w[0:1] * um + w[1:2] * u[HALO:HALO + rows] + w[2:3] * up


def _fill_halo_tile(hm_ref, x_ref, xp_ref, xn_ref, sh, sc, rows):
    t = pl.program_id(1)
    nt = pl.num_programs(1)
    keep_prev = jnp.where(t > 0, 1.0, 0.0)
    keep_next = jnp.where(t < nt - 1, 1.0, 0.0)
    hm_ref[HALO:HALO + rows, :] = (x_ref[0] * (1.0 + sc) + sh).astype(BF16)
    hm_ref[0:HALO, :] = ((xp_ref[0] * (1.0 + sc) + sh) * keep_prev).astype(BF16)
    hm_ref[HALO + rows:2 * HALO + rows, :] = ((xn_ref[0] * (1.0 + sc) + sh) * keep_next).astype(BF16)


def _halo_specs(rows, seq, d):
    nb = rows // HALO
    last = seq // HALO - 1
    return [
        pl.BlockSpec((1, rows, d), lambda b, t: (b, t, 0)),
        pl.BlockSpec((1, HALO, d), lambda b, t: (b, jnp.maximum(t * nb - 1, 0), 0)),
        pl.BlockSpec((1, HALO, d), lambda b, t: (b, jnp.minimum((t + 1) * nb, last), 0)),
    ]


def _row_spec(d):
    return pl.BlockSpec((1, 1, d), lambda b, t: (b, 0, 0))


def _mod_kernel(c_ref, w_ref, b_ref, o_ref):
    c = c_ref[...]
    act = c * jax.nn.sigmoid(c)
    o_ref[0] = jnp.dot(act, w_ref[0], preferred_element_type=F32,
                       precision=lax.Precision.HIGHEST) + b_ref[0]


def _modulation(cc, mod_w, mod_b):
    rows, d = cc.shape
    depth, _, n = mod_w.shape
    tn = 2048
    return pl.pallas_call(
        _mod_kernel,
        out_shape=jax.ShapeDtypeStruct((depth, rows, n), F32),
        grid=(depth, n // tn),
        in_specs=[
            pl.BlockSpec((rows, d), lambda i, j: (0, 0)),
            pl.BlockSpec((1, d, tn), lambda i, j: (i, 0, j)),
            pl.BlockSpec((1, 1, tn), lambda i, j: (i, 0, j)),
        ],
        out_specs=pl.BlockSpec((1, rows, tn), lambda i, j: (i, 0, j)),
        compiler_params=pltpu.CompilerParams(
            dimension_semantics=("arbitrary", "arbitrary"),
            vmem_limit_bytes=_vmem_limit(2 * d * tn * 4)),
        name="modulation",
    )(cc, mod_w, mod_b.reshape(depth, 1, n))


def _ab_in_kernel(*refs, use_rope, rows):
    (x_ref, sh_ref, sc_ref, w_ref, lng_ref, lnb_ref, ws_ref, bias_ref, qg_ref, kg_ref) = refs[:10]
    if use_rope:
        cos_ref, s1_ref, s2_ref = refs[10:13]
        a_ref, q_ref, k_ref, v_ref = refs[13:]
    else:
        a_ref, q_ref, k_ref, v_ref = refs[10:]

    h = (x_ref[0] * (1.0 + sc_ref[0]) + sh_ref[0]).astype(BF16)

    u = jax.nn.gelu(_dot(h, w_ref[:, 0:GM_WIDTH]))
    vn = _layer_norm(jax.nn.gelu(_dot(h, w_ref[:, GM_WIDTH:2 * GM_WIDTH])), lng_ref[...], lnb_ref[...])
    vb = vn.astype(BF16)
    gd = GM_WIDTH // GM_GROUPS
    for c in range(rows // CHUNK):
        r = slice(c * CHUNK, (c + 1) * CHUNK)
        for g in range(GM_GROUPS):
            l = slice(g * gd, (g + 1) * gd)
            mixed = _dot(ws_ref[g], vb[r, l]) + bias_ref[:, l]
            a_ref[0, r, l] = (u[r, l] * mixed).astype(BF16)

    def rope(z):
        if not use_rope:
            return z
        return (z * cos_ref[...] + pltpu.roll(z, 3 * HEAD_DIM // 4, 1) * s1_ref[...]
                + pltpu.roll(z, HEAD_DIM // 4, 1) * s2_ref[...])

    scale = HEAD_DIM ** -0.5
    pq = _dot(h, w_ref[:, 2 * GM_WIDTH:KV_OFF])
    for hh in range(N_Q_HEADS):
        l = slice(hh * HEAD_DIM, (hh + 1) * HEAD_DIM)
        q_ref[0, :, l] = (rope(_rms_norm(pq[:, l], qg_ref[...])) * scale).astype(BF16)
    pk = _dot(h, w_ref[:, KV_OFF:KV_OFF + KV_W])
    for hh in range(N_KV_HEADS):
        l = slice(hh * HEAD_DIM, (hh + 1) * HEAD_DIM)
        k_ref[0, :, l] = rope(_rms_norm(pk[:, l], kg_ref[...])).astype(BF16)
    v_ref[0] = _dot(h, w_ref[:, KV_OFF + KV_W:KV_OFF + 2 * KV_W]).astype(BF16)


def _ab_in(x, sh, sc, w, lng, lnb, ws, bias2d, qg, kg, rope_tabs, rows):
    bsz, seq, d = x.shape
    n = w.shape[1]
    use_rope = rope_tabs is not None
    in_specs = [
        pl.BlockSpec((1, rows, d), lambda b, t: (b, t, 0)),
        _row_spec(d), _row_spec(d),
        _resident((d, n)),
        _resident((1, GM_WIDTH)), _resident((1, GM_WIDTH)),
        _resident((GM_GROUPS, CHUNK, CHUNK)),
        _resident((CHUNK, GM_WIDTH)),
        _resident((1, HEAD_DIM)), _resident((1, HEAD_DIM)),
    ]
    args = [x, sh, sc, w, lng, lnb, ws, bias2d, qg, kg]
    if use_rope:
        in_specs += [pl.BlockSpec((rows, HEAD_DIM), lambda b, t: (t, 0))] * 3
        args += list(rope_tabs)
    out_shape = [
        jax.ShapeDtypeStruct((bsz, seq, GM_WIDTH), BF16),
        jax.ShapeDtypeStruct((bsz, seq, ATTN_WIDTH), BF16),
        jax.ShapeDtypeStruct((bsz, seq, KV_W), BF16),
        jax.ShapeDtypeStruct((bsz, seq, KV_W), BF16),
    ]
    out_specs = [
        pl.BlockSpec((1, rows, GM_WIDTH), lambda b, t: (b, t, 0)),
        pl.BlockSpec((1, rows, ATTN_WIDTH), lambda b, t: (b, t, 0)),
        pl.BlockSpec((1, rows, KV_W), lambda b, t: (b, t, 0)),
        pl.BlockSpec((1, rows, KV_W), lambda b, t: (b, t, 0)),
    ]
    est = d * n * 2 + 2 * rows * d * 4 + 6 * rows * n * 4
    return pl.pallas_call(
        functools.partial(_ab_in_kernel, use_rope=use_rope, rows=rows),
        out_shape=out_shape,
        grid=(bsz, seq // rows),
        in_specs=in_specs,
        out_specs=out_specs,
        compiler_params=pltpu.CompilerParams(
            dimension_semantics=("arbitrary", "arbitrary"),
            vmem_limit_bytes=_vmem_limit(est)),
        name="ab_in_rope" if use_rope else "ab_in_ctx",
    )(*args)


def _attn_kernel(q_ref, k_ref, v_ref, o_ref, *, rows, blk):
    k = k_ref[0]
    v = v_ref[0]
    for h in range(N_Q_HEADS // N_KV_HEADS):
        l_ = slice(h * HEAD_DIM, (h + 1) * HEAD_DIM)
        for r0 in range(0, rows, blk):
            s = _dot_nt(q_ref[0, r0:r0 + blk, l_], k)
            m = jnp.max(s, axis=-1, keepdims=True)
            p = jnp.exp(s - m)
            l = jnp.sum(p, axis=-1, keepdims=True)
            o_ref[0, r0:r0 + blk, l_] = (_dot(p.astype(BF16), v) / l).astype(BF16)


def _attention(q, k, v, rows, blk=128):
    bsz, sq, _ = q.shape
    sk = k.shape[1]
    gw = (N_Q_HEADS // N_KV_HEADS) * HEAD_DIM
    est = 4 * blk * sk * 12 + 4 * sk * HEAD_DIM * 2 + 4 * rows * gw * 2
    return pl.pallas_call(
        functools.partial(_attn_kernel, rows=rows, blk=min(blk, rows)),
        out_shape=jax.ShapeDtypeStruct((bsz, sq, ATTN_WIDTH), BF16),
        grid=(bsz, N_KV_HEADS, sq // rows),
        in_specs=[
            pl.BlockSpec((1, rows, gw), lambda b, h, t: (b, t, h)),
            pl.BlockSpec((1, sk, HEAD_DIM), lambda b, h, t: (b, 0, h)),
            pl.BlockSpec((1, sk, HEAD_DIM), lambda b, h, t: (b, 0, h)),
        ],
        out_specs=pl.BlockSpec((1, rows, gw), lambda b, h, t: (b, t, h)),
        compiler_params=pltpu.CompilerParams(
            dimension_semantics=("arbitrary", "arbitrary", "arbitrary"),
            vmem_limit_bytes=_vmem_limit(est)),
        name="attention",
    )(q, k, v)


def _out_ln_kernel(a1_ref, a2_ref, w_ref, x_ref, g_ref, lng_ref, lnb_ref, o_ref):
    half = a1_ref.shape[2]
    y = _dot(a1_ref[0], w_ref[0:half, :]) + _dot(a2_ref[0], w_ref[half:2 * half, :])
    z = DN_ALPHA * x_ref[0] + g_ref[0] * y
    o_ref[0] = _layer_norm(z, lng_ref[...], lnb_ref[...])


def _out_ln(a1, a2, w, x, gate, lng, lnb, rows):
    bsz, seq, d = x.shape
    half = a1.shape[2]
    est = 2 * half * d * 2 + 4 * rows * d * 4 + 4 * rows * half * 2 + 2 * rows * d * 4
    return pl.pallas_call(
        _out_ln_kernel,
        out_shape=jax.ShapeDtypeStruct((bsz, seq, d), F32),
        grid=(bsz, seq // rows),
        in_specs=[
            pl.BlockSpec((1, rows, half), lambda b, t: (b, t, 0)),
            pl.BlockSpec((1, rows, half), lambda b, t: (b, t, 0)),
            _resident((2 * half, d)),
            pl.BlockSpec((1, rows, d), lambda b, t: (b, t, 0)),
            _row_spec(d),
            _resident((1, d)), _resident((1, d)),
        ],
        out_specs=pl.BlockSpec((1, rows, d), lambda b, t: (b, t, 0)),
        compiler_params=pltpu.CompilerParams(
            dimension_semantics=("arbitrary", "arbitrary"),
            vmem_limit_bytes=_vmem_limit(est)),
        name="out_ln",
    )(a1, a2, w, x, gate, lng, lnb)


def _ffn_kernel(x_ref, xp_ref, xn_ref, sh_ref, sc_ref, g_ref, wu1_ref, wu2_ref, cw1_ref, cw2_ref,
                cb1_ref, cb2_ref, wd_ref, lng_ref, lnb_ref, o_ref, hm_ref, ua_ref, ub_ref, ga_ref, gb_ref,
                f_ref, *, rows):
    _fill_halo_tile(hm_ref, x_ref, xp_ref, xn_ref, sh_ref[0], sc_ref[0], rows)
    n_chunks = wu1_ref.shape[0]
    tiles = rows // V7X_SUBLANES
    sub = lax.broadcasted_iota(jnp.int32, (V7X_SUBLANES, FF_CHUNK), 0)
    u_refs = (ua_ref, ub_ref)
    gt_refs = (ga_ref, gb_ref)

    def up_half(c, slot, half):
        w_ref = wu1_ref if half == 0 else wu2_ref
        u_refs[slot][half] = _dot(hm_ref[...], w_ref[c]).reshape(tiles + 2, V7X_SUBLANES, FF_CHUNK)

    def up_proj(c, slot):
        up_half(c, slot, 0)
        up_half(c, slot, 1)

    def conv_gate(c, slot, parts=1):
        ur = u_refs[slot]
        taps = (cw1_ref[c], cw2_ref[c])
        bias = (cb1_ref[c], cb2_ref[c])
        cur = [ur[h, 1] for h in range(2)]
        dn_prev = [pltpu.roll(ur[h, 0], 1, 0) for h in range(2)]
        dn_cur = [pltpu.roll(cur[h], 1, 0) for h in range(2)]
        up_cur = [pltpu.roll(cur[h], V7X_SUBLANES - 1, 0) for h in range(2)]
        for i in range(tiles):
            a = []
            for h in range(2):
                nxt_tile = ur[h, i + 2]
                up_next = pltpu.roll(nxt_tile, V7X_SUBLANES - 1, 0)
                prev = jnp.where(sub == 0, dn_prev[h], dn_cur[h])
                nxt = jnp.where(sub == V7X_SUBLANES - 1, up_next, up_cur[h])
                w = taps[h]
                a.append(bias[h] + w[0:1] * prev + w[1:2] * cur[h] + w[2:3] * nxt)
                dn_prev[h], cur[h], up_cur[h] = dn_cur[h], nxt_tile, up_next
                dn_cur[h] = pltpu.roll(nxt_tile, 1, 0)
            gt_refs[slot][i * V7X_SUBLANES:(i + 1) * V7X_SUBLANES, :] = a[0] * (a[1] * jax.nn.sigmoid(a[1]))
            if (i + 1) % (tiles // parts) == 0:
                yield

    def down_half(c, slot, half):
        dh = f_ref.shape[1] // 2
        l = slice(half * dh, (half + 1) * dh)
        f_ref[:, l] += _dot(gt_refs[slot][...].astype(BF16), wd_ref[c, :, l])

    def down_proj(c, slot):
        down_half(c, slot, 0)
        down_half(c, slot, 1)

    def step(c, slot, refill=True, drain=True):
        nop = lambda: None
        up0, up1 = ((lambda: up_half(c + 1, 1 - slot, 0), lambda: up_half(c + 1, 1 - slot, 1))
                    if refill else (nop, nop))
        dn0, dn1 = ((lambda: down_half(c - 1, 1 - slot, 0), lambda: down_half(c - 1, 1 - slot, 1))
                    if drain else (nop, nop))
        for _, piece in zip(conv_gate(c, slot, parts=8), (up0, nop, nop, up1, nop, nop, dn0, dn1)):
            piece()

    f_ref[...] = jnp.zeros_like(f_ref)
    up_proj(0, 0)
    step(0, 0, drain=False)

    def one_step(c, carry):
        @pl.when(c % 2 == 1)
        def _():
            step(c, 1)

        @pl.when(c % 2 == 0)
        def _():
            step(c, 0)

        return carry

    lax.fori_loop(1, n_chunks - 1, one_step, 0)
    step(n_chunks - 1, 0, refill=False)
    down_proj(n_chunks - 1, 0)
    z = DN_ALPHA * x_ref[0] + g_ref[0] * f_ref[...]
    o_ref[0] = _layer_norm(z, lng_ref[...], lnb_ref[...])


def _ffn(x, sh, sc, gate, wu1, wu2, cw1, cw2, cb1, cb2, wd, lng, lnb, rows):
    bsz, seq, d = x.shape
    nc = wu1.shape[0]
    assert nc % 2 == 1 and nc >= 3, "the pipeline peels three chunks and pairs the rest"
    est = (3 * nc * d * FF_CHUNK * 2 + 4 * rows * d * 4 + (rows + 2 * HALO) * d * 2
           + nc * rows * FF_CHUNK * 2 + 8 * (rows + 2 * HALO) * FF_CHUNK * 4 + 2 * rows * d * 4)
    return pl.pallas_call(
        functools.partial(_ffn_kernel, rows=rows),
        out_shape=jax.ShapeDtypeStruct((bsz, seq, d), F32),
        grid=(bsz, seq // rows),
        in_specs=_halo_specs(rows, seq, d) + [
            _row_spec(d), _row_spec(d), _row_spec(d),
            _resident((nc, d, FF_CHUNK)), _resident((nc, d, FF_CHUNK)),
            _resident((nc, CONV_W, FF_CHUNK)), _resident((nc, CONV_W, FF_CHUNK)),
            _resident((nc, 1, FF_CHUNK)), _resident((nc, 1, FF_CHUNK)),
            _resident((nc, FF_CHUNK, d)),
            _resident((1, d)), _resident((1, d)),
        ],
        out_specs=pl.BlockSpec((1, rows, d), lambda b, t: (b, t, 0)),
        scratch_shapes=[
            pltpu.VMEM((rows + 2 * HALO, d), BF16),
            pltpu.VMEM((2, rows // HALO + 2, HALO, FF_CHUNK), F32),
            pltpu.VMEM((2, rows // HALO + 2, HALO, FF_CHUNK), F32),
            pltpu.VMEM((rows, FF_CHUNK), F32),
            pltpu.VMEM((rows, FF_CHUNK), F32),
            pltpu.VMEM((rows, d), F32),
        ],
        compiler_params=pltpu.CompilerParams(
            dimension_semantics=("arbitrary", "arbitrary"),
            vmem_limit_bytes=_vmem_limit(est), ),
        name="conv_ffn",
    )(x, x, x, sh, sc, gate, wu1, wu2, cw1, cw2, cb1, cb2, wd, lng, lnb)


def _cd_in_kernel(*refs, with_fz, rows):
    x_ref, xp_ref, xn_ref, sh_ref, sc_ref = refs[:5]
    if with_fz:
        wfz_ref, wx_ref, wdt_ref, cw_ref, cb_ref, dtb_ref = refs[5:11]
        f_ref, z_ref, xs_ref, b_ref, c_ref, dt_ref, hm_ref = refs[11:]
    else:
        wx_ref, wdt_ref, cw_ref, cb_ref, dtb_ref = refs[5:10]
        xs_ref, b_ref, c_ref, dt_ref, hm_ref = refs[10:]
    _fill_halo_tile(hm_ref, x_ref, xp_ref, xn_ref, sh_ref[0], sc_ref[0], rows)
    hc = hm_ref[HALO:HALO + rows, :]
    if with_fz:
        pfz = _dot(hc, wfz_ref[...])
        f_ref[0] = pfz[:, 0:FN_WIDTH].astype(BF16)
        z_ref[0] = pfz[:, FN_WIDTH:SSM_OFF]
    a = _conv3(_dot(hm_ref[...], wx_ref[...]), cw_ref[...], cb_ref[...], rows)
    s = a * jax.nn.sigmoid(a)
    xs_ref[0] = s[:, 0:SSD_WIDTH]
    b_ref[0] = s[:, SSD_WIDTH:SSD_WIDTH + SSD_GN].astype(BF16)
    c_ref[0] = s[:, SSD_WIDTH + SSD_GN:SSD_CONV_DIM].astype(BF16)
    dt_ref[0] = _softplus(_dot(hc, wdt_ref[...]) + dtb_ref[...])


def _cd_in(x, sh, sc, wfz, wx, wdt, cw, cb, dtb, rows):
    bsz, seq, d = x.shape
    with_fz = wfz is not None
    in_specs = _halo_specs(rows, seq, d) + [_row_spec(d), _row_spec(d)]
    args = [x, x, x, sh, sc]
    out_shape, out_specs = [], []
    if with_fz:
        in_specs.append(_resident((d, SSM_OFF)))
        args.append(wfz)
        out_shape += [jax.ShapeDtypeStruct((bsz, seq, FN_WIDTH), BF16),
                      jax.ShapeDtypeStruct((bsz, seq, SSD_WIDTH), F32)]
        out_specs += [pl.BlockSpec((1, rows, FN_WIDTH), lambda b, t: (b, t, 0)),
                      pl.BlockSpec((1, rows, SSD_WIDTH), lambda b, t: (b, t, 0))]
    in_specs += [_resident((d, SSD_CONV_DIM)), _resident((d, V7X_LANES)),
                 _resident((CONV_W, SSD_CONV_DIM)), _resident((1, SSD_CONV_DIM)),
                 _resident((1, V7X_LANES))]
    args += [wx, wdt, cw, cb, dtb]
    out_shape += [jax.ShapeDtypeStruct((bsz, seq, SSD_WIDTH), F32),
                  jax.ShapeDtypeStruct((bsz, seq, SSD_GN), BF16),
                  jax.ShapeDtypeStruct((bsz, seq, SSD_GN), BF16),
                  jax.ShapeDtypeStruct((bsz, seq, V7X_LANES), F32)]
    out_specs += [pl.BlockSpec((1, rows, SSD_WIDTH), lambda b, t: (b, t, 0)),
                  pl.BlockSpec((1, rows, SSD_GN), lambda b, t: (b, t, 0)),
                  pl.BlockSpec((1, rows, SSD_GN), lambda b, t: (b, t, 0)),
                  pl.BlockSpec((1, rows, V7X_LANES), lambda b, t: (b, t, 0))]
    est = d * 2200 * 2 + 2 * rows * d * 4 + 8 * (rows + 2 * HALO) * SSD_CONV_DIM * 4
    return pl.pallas_call(
        functools.partial(_cd_in_kernel, with_fz=with_fz, rows=rows),
        out_shape=out_shape,
        grid=(bsz, seq // rows),
        in_specs=in_specs,
        out_specs=out_specs,
        scratch_shapes=[pltpu.VMEM((rows + 2 * HALO, d), BF16)],
        compiler_params=pltpu.CompilerParams(
            dimension_semantics=("arbitrary", "arbitrary"),
            vmem_limit_bytes=_vmem_limit(est)),
        name="cd_in" if with_fz else "cd_in_ctx",
    )(*args)


def _cumsum_rows(z, reverse):
    n = z.shape[0]
    idx = lax.broadcasted_iota(jnp.int32, z.shape, 0)
    k = 1
    while k < n:
        if reverse:
            z = z + jnp.where(idx < n - k, pltpu.roll(z, n - k, 0), 0.0)
        else:
            z = z + jnp.where(idx >= k, pltpu.roll(z, k, 0), 0.0)
        k *= 2
    return z


def _lane_expand(row, lane0, width):
    lane = lax.broadcasted_iota(jnp.int32, (1, SSD_HPG * width), 1)
    out = jnp.zeros((1, SSD_HPG * width), F32)
    for j in range(SSD_HPG):
        val = jnp.broadcast_to(row[:, lane0 + j:lane0 + j + 1], (1, SSD_HPG * width))
        out = jnp.where(lane // width == j, val, out)
    return out


def _ssd_chunk(x, bm, cm, dt, a_row, h_ref, lane0, reverse, need_y):
    q = x.shape[0]
    gw = SSD_HPG * SSD_HEADDIM
    acum = _cumsum_rows(dt * a_row, reverse)
    total = acum[0:1, :] if reverse else acum[q - 1:q, :]
    acum_t = acum.T
    dt_t = dt.T
    total_t = jnp.broadcast_to(acum_t[:, 0:1] if reverse else acum_t[:, q - 1:q], (q, q))
    wgt_t = dt_t * jnp.exp(total_t - acum_t)
    lane_g = lax.broadcasted_iota(jnp.int32, (q, gw), 1) // SSD_HEADDIM
    if need_y:
        ti = lax.broadcasted_iota(jnp.int32, (q, q), 0)
        si = lax.broadcasted_iota(jnp.int32, (q, q), 1)
        mask = (si >= ti) if reverse else (si <= ti)
    ys = []
    for g in range(SSD_GROUPS):
        bg = bm[:, g * SSD_STATE:(g + 1) * SSD_STATE]
        xgb = x[:, g * gw:(g + 1) * gw].astype(BF16)
        x_heads = [jnp.where(lane_g == j, xgb, jnp.zeros_like(xgb)) for j in range(SSD_HPG)]
        hg = h_ref[g]
        if need_y:
            cg = cm[:, g * SSD_STATE:(g + 1) * SSD_STATE]
            cb = _dot_nt(cg, bg)
            cgf = cg.astype(F32)
            hgb = hg.astype(BF16)
            lhs, rhs = [], []
            for j in range(SSD_HPG):
                jl = lane0 + g * SSD_HPG + j
                col = jnp.broadcast_to(acum[:, jl:jl + 1], (q, q))
                row = jnp.broadcast_to(acum_t[jl:jl + 1, :], (q, q))
                dec = jnp.exp(jnp.where(mask, col - row, -jnp.inf))
                dtr = jnp.broadcast_to(dt_t[jl:jl + 1, :], (q, q))
                lhs.append((cb * dec * dtr).astype(BF16))
                lhs.append((cgf * jnp.exp(col)).astype(BF16))
                rhs.append(x_heads[j])
                rhs.append(jnp.where(lane_g == j, hgb, jnp.zeros_like(hgb)))
            ys.append(_dot(jnp.concatenate(lhs, axis=1), jnp.concatenate(rhs, axis=0)))
        bgt = bg.astype(F32).T
        wb = [(bgt * jnp.broadcast_to(wgt_t[lane0 + g * SSD_HPG + j:lane0 + g * SSD_HPG + j + 1, :],
                                      (SSD_STATE, q))).astype(BF16) for j in range(SSD_HPG)]
        st = _dot(jnp.concatenate(wb, axis=1), jnp.concatenate(x_heads, axis=0))
        h_ref[g] = hg * jnp.exp(_lane_expand(total, lane0 + g * SSD_HPG, SSD_HEADDIM)) + st
    if need_y:
        return jnp.concatenate(ys, axis=1)
    return None


def _ssd_kernel(xc_ref, bc_ref, dtc_ref, xl_ref, bl_ref, cl_ref, dtl_ref, z_ref, alog_ref,
                dskip_ref, ng_ref, o_ref, y_ref, h_ref):
    a_row = -jnp.exp(alog_ref[...])
    nc_ctx = xc_ref.shape[1] // CHUNK
    nc_lat = xl_ref.shape[1] // CHUNK

    def rows_of(c):
        return pl.ds(pl.multiple_of(c * CHUNK, CHUNK), CHUNK)

    def ctx_step(c, direction):
        r = rows_of(c)
        _ssd_chunk(xc_ref[0, r, :], bc_ref[0, r, :], None, dtc_ref[0, r, :], a_row,
                   h_ref.at[direction], direction * SSD_HEADS, direction == 1, False)

    def lat_step(c, direction):
        r = rows_of(c)
        y_ref[direction, r, :] = _ssd_chunk(
            xl_ref[0, r, :], bl_ref[0, r, :], cl_ref[0, r, :], dtl_ref[0, r, :], a_row,
            h_ref.at[direction], direction * SSD_HEADS, direction == 1, True)

    h_ref[...] = jnp.zeros_like(h_ref)

    def ctx_pair(i, carry):
        ctx_step(i, 0)
        ctx_step(nc_ctx - 1 - i, 1)
        return carry

    lax.fori_loop(0, nc_ctx, ctx_pair, 0)

    def lat_pair(i, carry):
        lat_step(i, 0)
        lat_step(nc_lat - 1 - i, 1)
        return carry

    lax.fori_loop(0, nc_lat, lat_pair, 0)

    def finish(c, carry):
        r = rows_of(c)
        zz = z_ref[0, r, :]
        yt = (y_ref[0, r, :] + y_ref[1, r, :] + xl_ref[0, r, :] * dskip_ref[...]) * (zz * jax.nn.sigmoid(zz))
        o_ref[0, r, :] = _rms_norm(yt, ng_ref[...]).astype(BF16)
        return carry

    lax.fori_loop(0, nc_lat, finish, 0)


def _ssd(xs_c, b_c, dt_c, xs_l, b_l, c_l, dt_l, z, alog_row, dskip_row, ng_row):
    bsz, sl, w = xs_l.shape
    sc = xs_c.shape[1]
    gw = SSD_HPG * SSD_HEADDIM

    def full(s, n):
        return pl.BlockSpec((1, s, n), lambda b: (b, 0, 0))

    est = (2 * (sl + sc) * (w * 4 + 2 * SSD_GN * 2 + V7X_LANES * 4) + 2 * sl * w * 4
           + 2 * sl * w * 2 + 2 * sl * w * 4)
    return pl.pallas_call(
        _ssd_kernel,
        out_shape=jax.ShapeDtypeStruct((bsz, sl, w), BF16),
        grid=(bsz,),
        in_specs=[
            full(sc, w), full(sc, SSD_GN), full(sc, V7X_LANES),
            full(sl, w), full(sl, SSD_GN), full(sl, SSD_GN), full(sl, V7X_LANES),
            full(sl, w),
            _resident((1, V7X_LANES)), _resident((1, w)), _resident((1, w)),
        ],
        out_specs=full(sl, w),
        scratch_shapes=[
            pltpu.VMEM((2, sl, w), F32),
            pltpu.VMEM((2, SSD_GROUPS, SSD_STATE, gw), F32),
        ],
        compiler_params=pltpu.CompilerParams(
            dimension_semantics=("arbitrary",),
            vmem_limit_bytes=_vmem_limit(est)),
        name="ssd_scan",
    )(xs_c, b_c, dt_c, xs_l, b_l, c_l, dt_l, z, alog_row, dskip_row, ng_row)


def _fourier_kernel(f_ref, cc_ref, sc_ref, w2_ref, o_ref, y_ref, *, rows):
    seq = f_ref.shape[1]
    for g in range(FN_GROUPS):
        l = slice(g * FN_GDIM, (g + 1) * FN_GDIM)
        fg = f_ref[0, :, l]
        y_ref[0:seq, l] = _dot(fg, cc_ref[...]).astype(BF16)
        y_ref[seq:2 * seq, l] = _dot(fg, sc_ref[...]).astype(BF16)
    norm = 1.0 / math.sqrt(seq * FN_GDIM)

    def tile(i, carry):
        r = pl.ds(pl.multiple_of(i * rows, rows), rows)
        o_ref[0, r, :] = (_dot(w2_ref[r, :], y_ref[...]) * norm).astype(BF16)
        return carry

    lax.fori_loop(0, seq // rows, tile, 0)


def _fourier(f, cc, sc, w2, rows):
    bsz, seq, w = f.shape
    est = seq * 2 * seq * 2 + 4 * seq * w * 2 + 2 * seq * w * 2 + 4 * rows * 2 * seq * 2
    return pl.pallas_call(
        functools.partial(_fourier_kernel, rows=rows),
        out_shape=jax.ShapeDtypeStruct((bsz, seq, w), BF16),
        grid=(bsz,),
        in_specs=[
            pl.BlockSpec((1, seq, w), lambda b: (b, 0, 0)),
            _resident((FN_GDIM, FN_GDIM)), _resident((FN_GDIM, FN_GDIM)),
            _resident((seq, 2 * seq)),
        ],
        out_specs=pl.BlockSpec((1, seq, w), lambda b: (b, 0, 0)),
        scratch_shapes=[pltpu.VMEM((2 * seq, w), BF16)],
        compiler_params=pltpu.CompilerParams(
            dimension_semantics=("arbitrary",),
            vmem_limit_bytes=_vmem_limit(est)),
        name="fourier_mix",
    )(f, cc, sc, w2)


def _rope_tables(seq):
    rows = seq // GRID_W
    row = jnp.repeat(jnp.arange(rows, dtype=F32), GRID_W)
    col = jnp.broadcast_to(jnp.arange(GRID_W, dtype=F32), (rows, GRID_W)).reshape(-1)
    n_freq = HEAD_DIM // 4
    inv = ROPE_THETA ** (-jnp.arange(n_freq, dtype=F32) / n_freq)
    ang = jnp.stack([row, col], axis=-1)[:, :, None] * inv
    cos, sin = jnp.cos(ang), jnp.sin(ang)
    zero = jnp.zeros_like(sin[:, 0])
    cos_t = jnp.concatenate([cos[:, 0], cos[:, 0], cos[:, 1], cos[:, 1]], axis=-1)
    s1 = jnp.concatenate([-sin[:, 0], zero, -sin[:, 1], zero], axis=-1)
    s2 = jnp.concatenate([zero, sin[:, 0], zero, sin[:, 1]], axis=-1)
    return cos_t, s1, s2


def _dft_tables(seq):
    def tw(n):
        i = jnp.arange(n, dtype=jnp.int32)
        ang = ((i[:, None] * i[None, :]) % n).astype(F32) * (2.0 * math.pi / n)
        return jnp.cos(ang), jnp.sin(ang)

    cc, sc = tw(FN_GDIM)
    cl, sl = tw(seq)
    return cc.astype(BF16), sc.astype(BF16), jnp.concatenate([cl, -sl], axis=1).astype(BF16)


def _ffn_params(w_up, conv_w, conv_b, w_down):
    d = w_up.shape[0]
    nc = D_FF // FF_CHUNK

    def cols(m, lo):
        r = m.shape[0]
        return m[:, lo:lo + D_FF].reshape(r, nc, FF_CHUNK).transpose(1, 0, 2)

    wu = w_up.astype(BF16)
    cb = conv_b.reshape(1, 2 * D_FF)
    return (cols(wu, 0), cols(wu, D_FF), cols(conv_w, 0), cols(conv_w, D_FF),
            cols(cb, 0), cols(cb, D_FF), w_down.astype(BF16).reshape(nc, FF_CHUNK, d))


def kernel(x, c, ctx, c_ctx, mod_w, mod_b, ln1_g, ln1_b, ln2_g, ln2_b, ffn_up, ffn_conv_w, ffn_conv_b,
           ffn_down, ab_w_in, ab_w_out, gm_ln_g, gm_ln_b, gm_ws, gm_bs, q_norm_g, k_norm_g, cd_w_in,
           cd_w_out, ssd_conv_w, ssd_conv_b, ssd_dt_bias, ssd_a_log, ssd_d, ssd_norm_g):
    bsz, seq, d = x.shape
    ctx_len = ctx.shape[1]
    lat_rows = 256
    ffn_rows = 512
    ctx_rows = min(256, ctx_len)

    pad = (-(bsz + 1)) % V7X_SUBLANES
    cc = jnp.concatenate([c, c_ctx[None], jnp.zeros((pad, d), F32)], axis=0)
    mod = _modulation(cc, mod_w, mod_b)
    rope_tabs = _rope_tables(seq)

    def row(v):
        return v.reshape(1, -1)

    xl, xc = x, ctx
    for i in range(DEPTH):
        last = i == DEPTH - 1
        j = i // 2
        ml = mod[i, :bsz].reshape(bsz, 6, 1, d)
        mc = jnp.broadcast_to(mod[i, bsz].reshape(1, 6, 1, d), (bsz, 6, 1, d))
        sh1, sc1, g1, sh2, sc2, g2 = (ml[:, k] for k in range(6))
        csh1, csc1, cg1, csh2, csc2, cg2 = (mc[:, k] for k in range(6))
        ln1 = (row(ln1_g[i]), row(ln1_b[i]))
        ln2 = (row(ln2_g[i]), row(ln2_b[i]))
        ffn_p = _ffn_params(ffn_up[i], ffn_conv_w[i], ffn_conv_b[i], ffn_down[i])

        if i % 2 == 0:
            w_in = ab_w_in[j].astype(BF16)
            w_out = ab_w_out[j].astype(BF16)
            gm = (row(gm_ln_g[j]), row(gm_ln_b[j]), gm_ws[j].astype(BF16),
                  jnp.repeat(gm_bs[j].T, GM_WIDTH // GM_GROUPS, axis=1),
                  row(q_norm_g[j]), row(k_norm_g[j]))
            a_l, q_l, k_l, v_l = _ab_in(xl, sh1, sc1, w_in, *gm, rope_tabs, lat_rows)
            a_c, q_c, k_c, v_c = _ab_in(xc, csh1, csc1, w_in, *gm, None, ctx_rows)
            o_l = _attention(q_l, jnp.concatenate([k_c, k_l], axis=1),
                             jnp.concatenate([v_c, v_l], axis=1), 512)
            mix_l = (a_l, o_l)
            if not last:
                mix_c = (a_c, _attention(q_c, k_c, v_c, ctx_rows))
        else:
            w_in = cd_w_in[j].astype(BF16)
            w_out = cd_w_out[j].astype(BF16)
            wfz = w_in[:, :SSM_OFF]
            wx = w_in[:, SSM_OFF:SSM_OFF + SSD_CONV_DIM]
            ndt = 2 * SSD_HEADS
            wdt = jnp.pad(w_in[:, SSM_OFF + SSD_CONV_DIM:], ((0, 0), (0, V7X_LANES - ndt)))
            dtb = jnp.pad(ssd_dt_bias[j].reshape(1, ndt), ((0, 0), (0, V7X_LANES - ndt)))
            alog = jnp.pad(ssd_a_log[j].reshape(1, ndt), ((0, 0), (0, V7X_LANES - ndt)))
            cw, cb = ssd_conv_w[j], row(ssd_conv_b[j])
            dskip = row(jnp.repeat(ssd_d[j], SSD_HEADDIM))
            f_l, z_l, xs_l, b_l, c_l, dt_l = _cd_in(xl, sh1, sc1, wfz, wx, wdt, cw, cb, dtb, lat_rows)
            if last:
                xs_c, b_c, _, dt_c = _cd_in(xc, csh1, csc1, None, wx, wdt, cw, cb, dtb, ctx_rows)
            else:
                raise NotImplementedError("odd layers with a context output are not part of this block")
            s_l = _ssd(xs_c, b_c, dt_c, xs_l, b_l, c_l, dt_l, z_l, alog, dskip, row(ssd_norm_g[j]))
            mix_l = (_fourier(f_l, *_dft_tables(seq), 512), s_l)

        xl = _out_ln(*mix_l, w_out, xl, g1, *ln1, 512)
        xl = _ffn(xl, sh2, sc2, g2, *ffn_p, *ln2, ffn_rows)
        if not last:
            xc = _out_ln(*mix_c, w_out, xc, cg1, *ln1, ctx_rows)
            xc = _ffn(xc, csh2, csc2, cg2, *ffn_p, *ln2, ctx_rows)
    return xl
```

```python
import functools
import math

import jax
import jax.numpy as jnp
from jax import lax
from jax.experimental import pallas as pl
from jax.experimental.pallas import tpu as pltpu

D_MODEL = 1024
DEPTH = 2
GRID_W = 64
CHUNK = 128
GM_GROUPS = 4
GM_WIDTH = 512
HEAD_DIM = 128
N_Q_HEADS = 4
N_KV_HEADS = 2
ATTN_WIDTH = 512
KV_W = 256
ROPE_THETA = 10000.0
FN_GROUPS = 4
FN_GDIM = 128
FN_WIDTH = 512
SSD_HEADDIM = 64
SSD_HEADS = 8
SSD_GROUPS = 2
SSD_HPG = 4
SSD_STATE = 128
SSD_WIDTH = 512
SSD_GN = 256
SSD_CONV_DIM = 1024
D_FF = 2816
CONV_W = 3
KV_OFF = 2 * GM_WIDTH + ATTN_WIDTH
SSM_OFF = FN_WIDTH + SSD_WIDTH
DN_ALPHA = (2 * DEPTH) ** 0.25
EPS = 1e-6

V7X_LANES = 128
V7X_SUBLANES = 8
V7X_VMEM_BYTES = 64 * 1024 * 1024
FF_CHUNK = 256

BF16 = jnp.bfloat16
F32 = jnp.float32
HALO = V7X_SUBLANES


def _vmem_limit(nbytes):
    return int(min(nbytes * 3 // 2 + (8 << 20), V7X_VMEM_BYTES - (6 << 20)))


def _resident(shape):
    nd = len(shape)
    return pl.BlockSpec(shape, lambda *_: (0,) * nd, pipeline_mode=pl.Buffered(1))


def _dot(a, b):
    return jnp.dot(a, b, preferred_element_type=F32)


def _dot_nt(a, b):
    return lax.dot_general(a, b, (((1,), (1,)), ((), ())), preferred_element_type=F32)


def _layer_norm(z, g, b):
    mu = jnp.mean(z, axis=-1, keepdims=True)
    zc = z - mu
    var = jnp.mean(zc * zc, axis=-1, keepdims=True)
    return zc * lax.rsqrt(var + EPS) * g + b


def _rms_norm(z, g):
    return z * lax.rsqrt(jnp.mean(z * z, axis=-1, keepdims=True) + EPS) * g


def _softplus(z):
    return jnp.maximum(z, 0.0) + jnp.log1p(jnp.exp(-jnp.abs(z)))


def _conv3(u, w, b, rows):
    n = u.shape[0]
    um = pltpu.roll(u, 1, 0)[HALO:HALO + rows]
    up = pltpu.roll(u, n - 1, 0)[HALO:HALO + rows]
    return b + w[0:1] * um + w[1:2] * u[HALO:HALO + rows] + w[2:3] * up


def _fill_halo_tile(hm_ref, x_ref, xp_ref, xn_ref, sh, sc, rows):
    t = pl.program_id(1)
    nt = pl.num_programs(1)
    keep_prev = jnp.where(t > 0, 1.0, 0.0)
    keep_next = jnp.where(t < nt - 1, 1.0, 0.0)
    hm_ref[HALO:HALO + rows, :] = (x_ref[0] * (1.0 + sc) + sh).astype(BF16)
    hm_ref[0:HALO, :] = ((xp_ref[0] * (1.0 + sc) + sh) * keep_prev).astype(BF16)
    hm_ref[HALO + rows:2 * HALO + rows, :] = ((xn_ref[0] * (1.0 + sc) + sh) * keep_next).astype(BF16)


def _halo_specs(rows, seq, d):
    nb = rows // HALO
    last = seq // HALO - 1
    return [
        pl.BlockSpec((1, rows, d), lambda b, t: (b, t, 0)),
        pl.BlockSpec((1, HALO, d), lambda b, t: (b, jnp.maximum(t * nb - 1, 0), 0)),
        pl.BlockSpec((1, HALO, d), lambda b, t: (b, jnp.minimum((t + 1) * nb, last), 0)),
    ]


def _row_spec(d):
    return pl.BlockSpec((1, 1, d), lambda b, t: (b, 0, 0))


def _mod_kernel(c_ref, w_ref, b_ref, o_ref):
    c = c_ref[...]
    act = c * jax.nn.sigmoid(c)
    o_ref[0] = jnp.dot(act, w_ref[0], preferred_element_type=F32,
                       precision=lax.Precision.HIGHEST) + b_ref[0]


def _modulation(cc, mod_w, mod_b):
    rows, d = cc.shape
    depth, _, n = mod_w.shape
    tn = 2048
    return pl.pallas_call(
        _mod_kernel,
        out_shape=jax.ShapeDtypeStruct((depth, rows, n), F32),
        grid=(depth, n // tn),
        in_specs=[
            pl.BlockSpec((rows, d), lambda i, j: (0, 0)),
            pl.BlockSpec((1, d, tn), lambda i, j: (i, 0, j)),
            pl.BlockSpec((1, 1, tn), lambda i, j: (i, 0, j)),
        ],
        out_specs=pl.BlockSpec((1, rows, tn), lambda i, j: (i, 0, j)),
        compiler_params=pltpu.CompilerParams(
            dimension_semantics=("arbitrary", "arbitrary"),
            vmem_limit_bytes=_vmem_limit(2 * d * tn * 4)),
        name="modulation",
    )(cc, mod_w, mod_b.reshape(depth, 1, n))


def _ab_in_kernel(*refs, use_rope, rows):
    (x_ref, sh_ref, sc_ref, w_ref, lng_ref, lnb_ref, ws_ref, bias_ref, qg_ref, kg_ref) = refs[:10]
    if use_rope:
        cos_ref, s1_ref, s2_ref = refs[10:13]
        a_ref, q_ref, k_ref, v_ref = refs[13:]
    else:
        a_ref, q_ref, k_ref, v_ref = refs[10:]

    h = (x_ref[0] * (1.0 + sc_ref[0]) + sh_ref[0]).astype(BF16)

    u = jax.nn.gelu(_dot(h, w_ref[:, 0:GM_WIDTH]))
    vn = _layer_norm(jax.nn.gelu(_dot(h, w_ref[:, GM_WIDTH:2 * GM_WIDTH])), lng_ref[...], lnb_ref[...])
    vb = vn.astype(BF16)
    gd = GM_WIDTH // GM_GROUPS
    for c in range(rows // CHUNK):
        r = slice(c * CHUNK, (c + 1) * CHUNK)
        for g in range(GM_GROUPS):
            l = slice(g * gd, (g + 1) * gd)
            mixed = _dot(ws_ref[g], vb[r, l]) + bias_ref[:, l]
            a_ref[0, r, l] = (u[r, l] * mixed).astype(BF16)

    def rope(z):
        if not use_rope:
            return z
        return (z * cos_ref[...] + pltpu.roll(z, 3 * HEAD_DIM // 4, 1) * s1_ref[...]
                + pltpu.roll(z, HEAD_DIM // 4, 1) * s2_ref[...])

    scale = HEAD_DIM ** -0.5 * math.log2(math.e)
    pq = _dot(h, w_ref[:, 2 * GM_WIDTH:KV_OFF])
    for hh in range(N_Q_HEADS):
        l = slice(hh * HEAD_DIM, (hh + 1) * HEAD_DIM)
        q_ref[0, :, l] = (rope(_rms_norm(pq[:, l], qg_ref[...])) * scale).astype(BF16)
    pk = _dot(h, w_ref[:, KV_OFF:KV_OFF + KV_W])
    for hh in range(N_KV_HEADS):
        l = slice(hh * HEAD_DIM, (hh + 1) * HEAD_DIM)
        k_ref[0, :, l] = rope(_rms_norm(pk[:, l], kg_ref[...])).astype(BF16)
    v_ref[0] = _dot(h, w_ref[:, KV_OFF + KV_W:KV_OFF + 2 * KV_W]).astype(BF16)


def _ab_in(x, sh, sc, w, lng, lnb, ws, bias2d, qg, kg, rope_tabs, rows):
    bsz, seq, d = x.shape
    n = w.shape[1]
    use_rope = rope_tabs is not None
    in_specs = [
        pl.BlockSpec((1, rows, d), lambda b, t: (b, t, 0)),
        _row_spec(d), _row_spec(d),
        _resident((d, n)),
        _resident((1, GM_WIDTH)), _resident((1, GM_WIDTH)),
        _resident((GM_GROUPS, CHUNK, CHUNK)),
        _resident((CHUNK, GM_WIDTH)),
        _resident((1, HEAD_DIM)), _resident((1, HEAD_DIM)),
    ]
    args = [x, sh, sc, w, lng, lnb, ws, bias2d, qg, kg]
    if use_rope:
        in_specs += [pl.BlockSpec((rows, HEAD_DIM), lambda b, t: (t, 0))] * 3
        args += list(rope_tabs)
    out_shape = [
        jax.ShapeDtypeStruct((bsz, seq, GM_WIDTH), BF16),
        jax.ShapeDtypeStruct((bsz, seq, ATTN_WIDTH), BF16),
        jax.ShapeDtypeStruct((bsz, seq, KV_W), BF16),
        jax.ShapeDtypeStruct((bsz, seq, KV_W), BF16),
    ]
    out_specs = [
        pl.BlockSpec((1, rows, GM_WIDTH), lambda b, t: (b, t, 0)),
        pl.BlockSpec((1, rows, ATTN_WIDTH), lambda b, t: (b, t, 0)),
        pl.BlockSpec((1, rows, KV_W), lambda b, t: (b, t, 0)),
        pl.BlockSpec((1, rows, KV_W), lambda b, t: (b, t, 0)),
    ]
    est = d * n * 2 + 2 * rows * d * 4 + 6 * rows * n * 4
    return pl.pallas_call(
        functools.partial(_ab_in_kernel, use_rope=use_rope, rows=rows),
        out_shape=out_shape,
        grid=(bsz, seq // rows),
        in_specs=in_specs,
        out_specs=out_specs,
        compiler_params=pltpu.CompilerParams(
            dimension_semantics=("arbitrary", "arbitrary"),
            vmem_limit_bytes=_vmem_limit(est)),
        name="ab_in_rope" if use_rope else "ab_in_ctx",
    )(*args)


def _attn_kernel(q_ref, k_ref, v_ref, o_ref, vt_ref, *, rows, blk):
    @pl.when(pl.program_id(2) == 0)
    def _():
        vt_ref[...] = v_ref[0].astype(F32).T

    k = k_ref[0]
    vt = vt_ref[...].astype(BF16)
    for h in range(N_Q_HEADS // N_KV_HEADS):
        l_ = slice(h * HEAD_DIM, (h + 1) * HEAD_DIM)
        for r0 in range(0, rows, blk):
            qt = q_ref[0, r0:r0 + blk, l_].astype(F32).T.astype(BF16)
            st = _dot(k, qt)
            m = jnp.max(st, axis=0, keepdims=True)
            p = jnp.exp2(st - m)
            l = jnp.sum(p, axis=0, keepdims=True)
            ot = _dot(vt, p.astype(BF16)) / l
            o_ref[0, r0:r0 + blk, l_] = ot.T.astype(BF16)


def _attention(q, k, v, rows, blk=512):
    bsz, sq, _ = q.shape
    sk = k.shape[1]
    gw = (N_Q_HEADS // N_KV_HEADS) * HEAD_DIM
    est = 4 * blk * sk * 12 + 4 * sk * HEAD_DIM * 2 + 4 * rows * gw * 2 + sk * HEAD_DIM * 4
    return pl.pallas_call(
        functools.partial(_attn_kernel, rows=rows, blk=min(blk, rows)),
        out_shape=jax.ShapeDtypeStruct((bsz, sq, ATTN_WIDTH), BF16),
        grid=(bsz, N_KV_HEADS, sq // rows),
        in_specs=[
            pl.BlockSpec((1, rows, gw), lambda b, h, t: (b, t, h)),
            pl.BlockSpec((1, sk, HEAD_DIM), lambda b, h, t: (b, 0, h)),
            pl.BlockSpec((1, sk, HEAD_DIM), lambda b, h, t: (b, 0, h)),
        ],
        out_specs=pl.BlockSpec((1, rows, gw), lambda b, h, t: (b, t, h)),
        scratch_shapes=[pltpu.VMEM((HEAD_DIM, sk), F32)],
        compiler_params=pltpu.CompilerParams(
            dimension_semantics=("arbitrary", "arbitrary", "arbitrary"),
            vmem_limit_bytes=_vmem_limit(est)),
        name="attention",
    )(q, k, v)


def _out_ln_kernel(a1_ref, a2_ref, w_ref, x_ref, g_ref, lng_ref, lnb_ref, o_ref):
    half = a1_ref.shape[2]
    y = _dot(a1_ref[0], w_ref[0:half, :]) + _dot(a2_ref[0], w_ref[half:2 * half, :])
    z = DN_ALPHA * x_ref[0] + g_ref[0] * y
    o_ref[0] = _layer_norm(z, lng_ref[...], lnb_ref[...])


def _out_ln(a1, a2, w, x, gate, lng, lnb, rows):
    bsz, seq, d = x.shape
    half = a1.shape[2]
    est = 2 * half * d * 2 + 4 * rows * d * 4 + 4 * rows * half * 2 + 2 * rows * d * 4
    return pl.pallas_call(
        _out_ln_kernel,
        out_shape=jax.ShapeDtypeStruct((bsz, seq, d), F32),
        grid=(bsz, seq // rows),
        in_specs=[
            pl.BlockSpec((1, rows, half), lambda b, t: (b, t, 0)),
            pl.BlockSpec((1, rows, half), lambda b, t: (b, t, 0)),
            _resident((2 * half, d)),
            pl.BlockSpec((1, rows, d), lambda b, t: (b, t, 0)),
            _row_spec(d),
            _resident((1, d)), _resident((1, d)),
        ],
        out_specs=pl.BlockSpec((1, rows, d), lambda b, t: (b, t, 0)),
        compiler_params=pltpu.CompilerParams(
            dimension_semantics=("arbitrary", "arbitrary"),
            vmem_limit_bytes=_vmem_limit(est)),
        name="out_ln",
    )(a1, a2, w, x, gate, lng, lnb)


def _ffn_kernel(x_ref, xp_ref, xn_ref, sh_ref, sc_ref, g_ref, wu1_ref, wu2_ref, cw1_ref, cw2_ref,
                cb1_ref, cb2_ref, wd_ref, lng_ref, lnb_ref, o_ref, hm_ref, u_ref, gt_ref, f_ref, *, rows):
    _fill_halo_tile(hm_ref, x_ref, xp_ref, xn_ref, sh_ref[0], sc_ref[0], rows)
    n_chunks = wu1_ref.shape[0]
    tiles = rows // V7X_SUBLANES
    sub = lax.broadcasted_iota(jnp.int32, (tiles, V7X_SUBLANES, FF_CHUNK), 1)

    def up_proj(c, slot):
        hm = hm_ref[...]
        u_ref[slot, 0] = _dot(hm, wu1_ref[c]).reshape(tiles + 2, V7X_SUBLANES, FF_CHUNK)
        u_ref[slot, 1] = _dot(hm, wu2_ref[c]).reshape(tiles + 2, V7X_SUBLANES, FF_CHUNK)

    def conv_gate(c, slot):
        def conv(half, cw_ref, cb_ref):
            u = u_ref[slot, half]
            dn = pltpu.roll(u, 1, 1)
            up = pltpu.roll(u, V7X_SUBLANES - 1, 1)
            prev = jnp.where(sub == 0, dn[0:tiles], dn[1:tiles + 1])
            nxt = jnp.where(sub == V7X_SUBLANES - 1, up[2:tiles + 2], up[1:tiles + 1])
            w = cw_ref[c]
            return cb_ref[c] + w[0:1] * prev + w[1:2] * u[1:tiles + 1] + w[2:3] * nxt

        a1 = conv(0, cw1_ref, cb1_ref)
        a2 = conv(1, cw2_ref, cb2_ref)
        gt_ref[slot] = (a1 * (a2 * jax.nn.sigmoid(a2))).reshape(rows, FF_CHUNK)

    def down_proj(c, slot):
        f_ref[...] += _dot(gt_ref[slot].astype(BF16), wd_ref[c])

    f_ref[...] = jnp.zeros_like(f_ref)
    up_proj(0, 0)
    up_proj(1, 1)
    conv_gate(0, 0)

    def pair(k, carry):
        c = 2 * k + 1
        up_proj(c + 1, 0)
        conv_gate(c, 1)
        down_proj(c - 1, 0)
        up_proj(c + 2, 1)
        conv_gate(c + 1, 0)
        down_proj(c, 1)
        return carry

    lax.fori_loop(0, (n_chunks - 3) // 2, pair, 0)
    up_proj(n_chunks - 1, 0)
    conv_gate(n_chunks - 2, 1)
    down_proj(n_chunks - 3, 0)
    conv_gate(n_chunks - 1, 0)
    down_proj(n_chunks - 2, 1)
    down_proj(n_chunks - 1, 0)
    z = DN_ALPHA * x_ref[0] + g_ref[0] * f_ref[...]
    o_ref[0] = _layer_norm(z, lng_ref[...], lnb_ref[...])


def _ffn(x, sh, sc, gate, wu1, wu2, cw1, cw2, cb1, cb2, wd, lng, lnb, rows):
    bsz, seq, d = x.shape
    nc = wu1.shape[0]
    assert nc % 2 == 1 and nc >= 3, "the pipeline peels three chunks and pairs the rest"
    est = (3 * nc * d * FF_CHUNK * 2 + 4 * rows * d * 4 + (rows + 2 * HALO) * d * 2
           + nc * rows * FF_CHUNK * 2 + 8 * (rows + 2 * HALO) * FF_CHUNK * 4 + 2 * rows * d * 4)
    return pl.pallas_call(
        functools.partial(_ffn_kernel, rows=rows),
        out_shape=jax.ShapeDtypeStruct((bsz, seq, d), F32),
        grid=(bsz, seq // rows),
        in_specs=_halo_specs(rows, seq, d) + [
            _row_spec(d), _row_spec(d), _row_spec(d),
            _resident((nc, d, FF_CHUNK)), _resident((nc, d, FF_CHUNK)),
            _resident((nc, CONV_W, FF_CHUNK)), _resident((nc, CONV_W, FF_CHUNK)),
            _resident((nc, 1, FF_CHUNK)), _resident((nc, 1, FF_CHUNK)),
            _resident((nc, FF_CHUNK, d)),
            _resident((1, d)), _resident((1, d)),
        ],
        out_specs=pl.BlockSpec((1, rows, d), lambda b, t: (b, t, 0)),
        scratch_shapes=[
            pltpu.VMEM((rows + 2 * HALO, d), BF16),
            pltpu.VMEM((2, 2, rows // HALO + 2, HALO, FF_CHUNK), F32),
            pltpu.VMEM((2, rows, FF_CHUNK), F32),
            pltpu.VMEM((rows, d), F32),
        ],
        compiler_params=pltpu.CompilerParams(
            dimension_semantics=("arbitrary", "arbitrary"),
            vmem_limit_bytes=_vmem_limit(est)),
        name="conv_ffn",
    )(x, x, x, sh, sc, gate, wu1, wu2, cw1, cw2, cb1, cb2, wd, lng, lnb)


def _cd_in_kernel(*refs, with_fz, rows):
    x_ref, xp_ref, xn_ref, sh_ref, sc_ref = refs[:5]
    if with_fz:
        wfz_ref, wx_ref, wdt_ref, cw_ref, cb_ref, dtb_ref = refs[5:11]
        f_ref, z_ref, xs_ref, b_ref, c_ref, dt_ref, hm_ref = refs[11:]
    else:
        wx_ref, wdt_ref, cw_ref, cb_ref, dtb_ref = refs[5:10]
        xs_ref, b_ref, c_ref, dt_ref, hm_ref = refs[10:]
    _fill_halo_tile(hm_ref, x_ref, xp_ref, xn_ref, sh_ref[0], sc_ref[0], rows)
    hc = hm_ref[HALO:HALO + rows, :]
    if with_fz:
        pfz = _dot(hc, wfz_ref[...])
        f_ref[0] = pfz[:, 0:FN_WIDTH].astype(BF16)
        z_ref[0] = pfz[:, FN_WIDTH:SSM_OFF]
    a = _conv3(_dot(hm_ref[...], wx_ref[...]), cw_ref[...], cb_ref[...], rows)
    s = a * jax.nn.sigmoid(a)
    xs_ref[0] = s[:, 0:SSD_WIDTH]
    b_ref[0] = s[:, SSD_WIDTH:SSD_WIDTH + SSD_GN].astype(BF16)
    c_ref[0] = s[:, SSD_WIDTH + SSD_GN:SSD_CONV_DIM].astype(BF16)
    dt_ref[0] = _softplus(_dot(hc, wdt_ref[...]) + dtb_ref[...])


def _cd_in(x, sh, sc, wfz, wx, wdt, cw, cb, dtb, rows):
    bsz, seq, d = x.shape
    with_fz = wfz is not None
    in_specs = _halo_specs(rows, seq, d) + [_row_spec(d), _row_spec(d)]
    args = [x, x, x, sh, sc]
    out_shape, out_specs = [], []
    if with_fz:
        in_specs.append(_resident((d, SSM_OFF)))
        args.append(wfz)
        out_shape += [jax.ShapeDtypeStruct((bsz, seq, FN_WIDTH), BF16),
                      jax.ShapeDtypeStruct((bsz, seq, SSD_WIDTH), F32)]
        out_specs += [pl.BlockSpec((1, rows, FN_WIDTH), lambda b, t: (b, t, 0)),
                      pl.BlockSpec((1, rows, SSD_WIDTH), lambda b, t: (b, t, 0))]
    in_specs += [_resident((d, SSD_CONV_DIM)), _resident((d, V7X_LANES)),
                 _resident((CONV_W, SSD_CONV_DIM)), _resident((1, SSD_CONV_DIM)),
                 _resident((1, V7X_LANES))]
    args += [wx, wdt, cw, cb, dtb]
    out_shape += [jax.ShapeDtypeStruct((bsz, seq, SSD_WIDTH), F32),
                  jax.ShapeDtypeStruct((bsz, seq, SSD_GN), BF16),
                  jax.ShapeDtypeStruct((bsz, seq, SSD_GN), BF16),
                  jax.ShapeDtypeStruct((bsz, seq, V7X_LANES), F32)]
    out_specs += [pl.BlockSpec((1, rows, SSD_WIDTH), lambda b, t: (b, t, 0)),
                  pl.BlockSpec((1, rows, SSD_GN), lambda b, t: (b, t, 0)),
                  pl.BlockSpec((1, rows, SSD_GN), lambda b, t: (b, t, 0)),
                  pl.BlockSpec((1, rows, V7X_LANES), lambda b, t: (b, t, 0))]
    est = d * 2200 * 2 + 2 * rows * d * 4 + 8 * (rows + 2 * HALO) * SSD_CONV_DIM * 4
    return pl.pallas_call(
        functools.partial(_cd_in_kernel, with_fz=with_fz, rows=rows),
        out_shape=out_shape,
        grid=(bsz, seq // rows),
        in_specs=in_specs,
        out_specs=out_specs,
        scratch_shapes=[pltpu.VMEM((rows + 2 * HALO, d), BF16)],
        compiler_params=pltpu.CompilerParams(
            dimension_semantics=("arbitrary", "arbitrary"),
            vmem_limit_bytes=_vmem_limit(est)),
        name="cd_in" if with_fz else "cd_in_ctx",
    )(*args)


def _cumsum_rows(z, reverse):
    n = z.shape[0]
    idx = lax.broadcasted_iota(jnp.int32, z.shape, 0)
    k = 1
    while k < n:
        if reverse:
            z = z + jnp.where(idx < n - k, pltpu.roll(z, n - k, 0), 0.0)
        else:
            z = z + jnp.where(idx >= k, pltpu.roll(z, k, 0), 0.0)
        k *= 2
    return z


def _lane_expand(row, lane0, width):
    lane = lax.broadcasted_iota(jnp.int32, (1, SSD_HPG * width), 1)
    out = jnp.zeros((1, SSD_HPG * width), F32)
    for j in range(SSD_HPG):
        val = jnp.broadcast_to(row[:, lane0 + j:lane0 + j + 1], (1, SSD_HPG * width))
        out = jnp.where(lane // width == j, val, out)
    return out


def _ssd_chunk(x, bm, cm, dt, a_row, h_ref, lane0, reverse, need_y):
    q = x.shape[0]
    gw = SSD_HPG * SSD_HEADDIM
    acum = _cumsum_rows(dt * a_row, reverse)
    total = acum[0:1, :] if reverse else acum[q - 1:q, :]
    acum_t = acum.T
    dt_t = dt.T
    total_t = jnp.broadcast_to(acum_t[:, 0:1] if reverse else acum_t[:, q - 1:q], (q, q))
    wgt_t = dt_t * jnp.exp(total_t - acum_t)
    lane_g = lax.broadcasted_iota(jnp.int32, (q, gw), 1) // SSD_HEADDIM
    if need_y:
        ti = lax.broadcasted_iota(jnp.int32, (q, q), 0)
        si = lax.broadcasted_iota(jnp.int32, (q, q), 1)
        mask = (si >= ti) if reverse else (si <= ti)
    ys = []
    for g in range(SSD_GROUPS):
        bg = bm[:, g * SSD_STATE:(g + 1) * SSD_STATE]
        xgb = x[:, g * gw:(g + 1) * gw].astype(BF16)
        x_heads = [jnp.where(lane_g == j, xgb, jnp.zeros_like(xgb)) for j in range(SSD_HPG)]
        hg = h_ref[g]
        if need_y:
            cg = cm[:, g * SSD_STATE:(g + 1) * SSD_STATE]
            cb = _dot_nt(cg, bg)
            cgf = cg.astype(F32)
            hgb = hg.astype(BF16)
            lhs, rhs = [], []
            for j in range(SSD_HPG):
                jl = lane0 + g * SSD_HPG + j
                col = jnp.broadcast_to(acum[:, jl:jl + 1], (q, q))
                row = jnp.broadcast_to(acum_t[jl:jl + 1, :], (q, q))
                dec = jnp.exp(jnp.where(mask, col - row, -jnp.inf))
                dtr = jnp.broadcast_to(dt_t[jl:jl + 1, :], (q, q))
                lhs.append((cb * dec * dtr).astype(BF16))
                lhs.append((cgf * jnp.exp(col)).astype(BF16))
                rhs.append(x_heads[j])
                rhs.append(jnp.where(lane_g == j, hgb, jnp.zeros_like(hgb)))
            ys.append(_dot(jnp.concatenate(lhs, axis=1), jnp.concatenate(rhs, axis=0)))
        bgt = bg.astype(F32).T
        wb = [(bgt * jnp.broadcast_to(wgt_t[lane0 + g * SSD_HPG + j:lane0 + g * SSD_HPG + j + 1, :],
                                      (SSD_STATE, q))).astype(BF16) for j in range(SSD_HPG)]
        st = _dot(jnp.concatenate(wb, axis=1), jnp.concatenate(x_heads, axis=0))
        h_ref[g] = hg * jnp.exp(_lane_expand(total, lane0 + g * SSD_HPG, SSD_HEADDIM)) + st
    if need_y:
        return jnp.concatenate(ys, axis=1)
    return None


def _ssd_kernel(xc_ref, bc_ref, dtc_ref, xl_ref, bl_ref, cl_ref, dtl_ref, z_ref, alog_ref,
                dskip_ref, ng_ref, o_ref, y_ref, h_ref):
    a_row = -jnp.exp(alog_ref[...])
    nc_ctx = xc_ref.shape[1] // CHUNK
    nc_lat = xl_ref.shape[1] // CHUNK

    def rows_of(c):
        return pl.ds(pl.multiple_of(c * CHUNK, CHUNK), CHUNK)

    def ctx_step(c, direction):
        r = rows_of(c)
        _ssd_chunk(xc_ref[0, r, :], bc_ref[0, r, :], None, dtc_ref[0, r, :], a_row,
                   h_ref.at[direction], direction * SSD_HEADS, direction == 1, False)

    def lat_step(c, direction):
        r = rows_of(c)
        y_ref[direction, r, :] = _ssd_chunk(
            xl_ref[0, r, :], bl_ref[0, r, :], cl_ref[0, r, :], dtl_ref[0, r, :], a_row,
            h_ref.at[direction], direction * SSD_HEADS, direction == 1, True)

    h_ref[...] = jnp.zeros_like(h_ref)

    def ctx_pair(i, carry):
        ctx_step(i, 0)
        ctx_step(nc_ctx - 1 - i, 1)
        return carry

    lax.fori_loop(0, nc_ctx, ctx_pair, 0)

    def lat_pair(i, carry):
        lat_step(i, 0)
        lat_step(nc_lat - 1 - i, 1)
        return carry

    lax.fori_loop(0, nc_lat, lat_pair, 0)

    def finish(c, carry):
        r = rows_of(c)
        zz = z_ref[0, r, :]
        yt = (y_ref[0, r, :] + y_ref[1, r, :] + xl_ref[0, r, :] * dskip_ref[...]) * (zz * jax.nn.sigmoid(zz))
        o_ref[0, r, :] = _rms_norm(yt, ng_ref[...]).astype(BF16)
        return carry

    lax.fori_loop(0, nc_lat, finish, 0)


def _ssd(xs_c, b_c, dt_c, xs_l, b_l, c_l, dt_l, z, alog_row, dskip_row, ng_row):
    bsz, sl, w = xs_l.shape
    sc = xs_c.shape[1]
    gw = SSD_HPG * SSD_HEADDIM

    def full(s, n):
        return pl.BlockSpec((1, s, n), lambda b: (b, 0, 0))

    est = (2 * (sl + sc) * (w * 4 + 2 * SSD_GN * 2 + V7X_LANES * 4) + 2 * sl * w * 4
           + 2 * sl * w * 2 + 2 * sl * w * 4)
    return pl.pallas_call(
        _ssd_kernel,
        out_shape=jax.ShapeDtypeStruct((bsz, sl, w), BF16),
        grid=(bsz,),
        in_specs=[
            full(sc, w), full(sc, SSD_GN), full(sc, V7X_LANES),
            full(sl, w), full(sl, SSD_GN), full(sl, SSD_GN), full(sl, V7X_LANES),
            full(sl, w),
            _resident((1, V7X_LANES)), _resident((1, w)), _resident((1, w)),
        ],
        out_specs=full(sl, w),
        scratch_shapes=[
            pltpu.VMEM((2, sl, w), F32),
            pltpu.VMEM((2, SSD_GROUPS, SSD_STATE, gw), F32),
        ],
        compiler_params=pltpu.CompilerParams(
            dimension_semantics=("arbitrary",),
            vmem_limit_bytes=_vmem_limit(est)),
        name="ssd_scan",
    )(xs_c, b_c, dt_c, xs_l, b_l, c_l, dt_l, z, alog_row, dskip_row, ng_row)


def _fourier_kernel(f_ref, cc_ref, sc_ref, w2_ref, o_ref, y_ref, *, rows):
    seq = f_ref.shape[1]
    for g in range(FN_GROUPS):
        l = slice(g * FN_GDIM, (g + 1) * FN_GDIM)
        fg = f_ref[0, :, l]
        y_ref[0:seq, l] = _dot(fg, cc_ref[...]).astype(BF16)
        y_ref[seq:2 * seq, l] = _dot(fg, sc_ref[...]).astype(BF16)
    norm = 1.0 / math.sqrt(seq * FN_GDIM)

    def tile(i, carry):
        r = pl.ds(pl.multiple_of(i * rows, rows), rows)
        o_ref[0, r, :] = (_dot(w2_ref[r, :], y_ref[...]) * norm).astype(BF16)
        return carry

    lax.fori_loop(0, seq // rows, tile, 0)


def _fourier(f, cc, sc, w2, rows):
    bsz, seq, w = f.shape
    est = seq * 2 * seq * 2 + 4 * seq * w * 2 + 2 * seq * w * 2 + 4 * rows * 2 * seq * 2
    return pl.pallas_call(
        functools.partial(_fourier_kernel, rows=rows),
        out_shape=jax.ShapeDtypeStruct((bsz, seq, w), BF16),
        grid=(bsz,),
        in_specs=[
            pl.BlockSpec((1, seq, w), lambda b: (b, 0, 0)),
            _resident((FN_GDIM, FN_GDIM)), _resident((FN_GDIM, FN_GDIM)),
            _resident((seq, 2 * seq)),
        ],
        out_specs=pl.BlockSpec((1, seq, w), lambda b: (b, 0, 0)),
        scratch_shapes=[pltpu.VMEM((2 * seq, w), BF16)],
        compiler_params=pltpu.CompilerParams(
            dimension_semantics=("arbitrary",),
            vmem_limit_bytes=_vmem_limit(est)),
        name="fourier_mix",
    )(f, cc, sc, w2)


def _rope_tables(seq):
    rows = seq // GRID_W
    row = jnp.repeat(jnp.arange(rows, dtype=F32), GRID_W)
    col = jnp.broadcast_to(jnp.arange(GRID_W, dtype=F32), (rows, GRID_W)).reshape(-1)
    n_freq = HEAD_DIM // 4
    inv = ROPE_THETA ** (-jnp.arange(n_freq, dtype=F32) / n_freq)
    ang = jnp.stack([row, col], axis=-1)[:, :, None] * inv
    cos, sin = jnp.cos(ang), jnp.sin(ang)
    zero = jnp.zeros_like(sin[:, 0])
    cos_t = jnp.concatenate([cos[:, 0], cos[:, 0], cos[:, 1], cos[:, 1]], axis=-1)
    s1 = jnp.concatenate([-sin[:, 0], zero, -sin[:, 1], zero], axis=-1)
    s2 = jnp.concatenate([zero, sin[:, 0], zero, sin[:, 1]], axis=-1)
    return cos_t, s1, s2


def _dft_tables(seq):
    def tw(n):
        i = jnp.arange(n, dtype=jnp.int32)
        ang = ((i[:, None] * i[None, :]) % n).astype(F32) * (2.0 * math.pi / n)
        return jnp.cos(ang), jnp.sin(ang)

    cc, sc = tw(FN_GDIM)
    cl, sl = tw(seq)
    return cc.astype(BF16), sc.astype(BF16), jnp.concatenate([cl, -sl], axis=1).astype(BF16)


def _ffn_params(w_up, conv_w, conv_b, w_down):
    d = w_up.shape[0]
    nc = D_FF // FF_CHUNK

    def cols(m, lo):
        r = m.shape[0]
        return m[:, lo:lo + D_FF].reshape(r, nc, FF_CHUNK).transpose(1, 0, 2)

    wu = w_up.astype(BF16)
    cb = conv_b.reshape(1, 2 * D_FF)
    return (cols(wu, 0), cols(wu, D_FF), cols(conv_w, 0), cols(conv_w, D_FF),
            cols(cb, 0), cols(cb, D_FF), w_down.astype(BF16).reshape(nc, FF_CHUNK, d))


def kernel(x, c, ctx, c_ctx, mod_w, mod_b, ln1_g, ln1_b, ln2_g, ln2_b, ffn_up, ffn_conv_w, ffn_conv_b,
           ffn_down, ab_w_in, ab_w_out, gm_ln_g, gm_ln_b, gm_ws, gm_bs, q_norm_g, k_norm_g, cd_w_in,
           cd_w_out, ssd_conv_w, ssd_conv_b, ssd_dt_bias, ssd_a_log, ssd_d, ssd_norm_g):
    bsz, seq, d = x.shape
    ctx_len = ctx.shape[1]
    lat_rows = 256
    ffn_rows = 512
    ctx_rows = min(256, ctx_len)

    pad = (-(bsz + 1)) % V7X_SUBLANES
    cc = jnp.concatenate([c, c_ctx[None], jnp.zeros((pad, d), F32)], axis=0)
    mod = _modulation(cc, mod_w, mod_b)
    rope_tabs = _rope_tables(seq)

    def row(v):
        return v.reshape(1, -1)

    xl, xc = x, ctx
    for i in range(DEPTH):
        last = i == DEPTH - 1
        j = i // 2
        ml = mod[i, :bsz].reshape(bsz, 6, 1, d)
        mc = jnp.broadcast_to(mod[i, bsz].reshape(1, 6, 1, d), (bsz, 6, 1, d))
        sh1, sc1, g1, sh2, sc2, g2 = (ml[:, k] for k in range(6))
        csh1, csc1, cg1, csh2, csc2, cg2 = (mc[:, k] for k in range(6))
        ln1 = (row(ln1_g[i]), row(ln1_b[i]))
        ln2 = (row(ln2_g[i]), row(ln2_b[i]))
        ffn_p = _ffn_params(ffn_up[i], ffn_conv_w[i], ffn_conv_b[i], ffn_down[i])

        if i % 2 == 0:
            w_in = ab_w_in[j].astype(BF16)
            w_out = ab_w_out[j].astype(BF16)
            gm = (row(gm_ln_g[j]), row(gm_ln_b[j]), gm_ws[j].astype(BF16),
                  jnp.repeat(gm_bs[j].T, GM_WIDTH // GM_GROUPS, axis=1),
                  row(q_norm_g[j]), row(k_norm_g[j]))
            a_l, q_l, k_l, v_l = _ab_in(xl, sh1, sc1, w_in, *gm, rope_tabs, lat_rows)
            a_c, q_c, k_c, v_c = _ab_in(xc, csh1, csc1, w_in, *gm, None, ctx_rows)
            o_l = _attention(q_l, jnp.concatenate([k_c, k_l], axis=1),
                             jnp.concatenate([v_c, v_l], axis=1), 512)
            mix_l = (a_l, o_l)
            if not last:
                mix_c = (a_c, _attention(q_c, k_c, v_c, ctx_rows))
        else:
            w_in = cd_w_in[j].astype(BF16)
            w_out = cd_w_out[j].astype(BF16)
            wfz = w_in[:, :SSM_OFF]
            wx = w_in[:, SSM_OFF:SSM_OFF + SSD_CONV_DIM]
            ndt = 2 * SSD_HEADS
            wdt = jnp.pad(w_in[:, SSM_OFF + SSD_CONV_DIM:], ((0, 0), (0, V7X_LANES - ndt)))
            dtb = jnp.pad(ssd_dt_bias[j].reshape(1, ndt), ((0, 0), (0, V7X_LANES - ndt)))
            alog = jnp.pad(ssd_a_log[j].reshape(1, ndt), ((0, 0), (0, V7X_LANES - ndt)))
            cw, cb = ssd_conv_w[j], row(ssd_conv_b[j])
            dskip = row(jnp.repeat(ssd_d[j], SSD_HEADDIM))
            f_l, z_l, xs_l, b_l, c_l, dt_l = _cd_in(xl, sh1, sc1, wfz, wx, wdt, cw, cb, dtb, lat_rows)
            if last:
                xs_c, b_c, _, dt_c = _cd_in(xc, csh1, csc1, None, wx, wdt, cw, cb, dtb, ctx_rows)
            else:
                raise NotImplementedError("odd layers with a context output are not part of this block")
            s_l = _ssd(xs_c, b_c, dt_c, xs_l, b_l, c_l, dt_l, z_l, alog, dskip, row(ssd_norm_g[j]))
            mix_l = (_fourier(f_l, *_dft_tables(seq), 512), s_l)

        xl = _out_ln(*mix_l, w_out, xl, g1, *ln1, 512)
        xl = _ffn(xl, sh2, sc2, g2, *ffn_p, *ln2, ffn_rows)
        if not last:
            xc = _out_ln(*mix_c, w_out, xc, cg1, *ln1, ctx_rows)
            xc = _ffn(xc, csh2, csc2, cg2, *ffn_p, *ln2, ctx_rows)
    return xl
```

```python
import functools
import math

import jax
import jax.numpy as jnp
from jax import lax
from jax.experimental import pallas as pl
from jax.experimental.pallas import tpu as pltpu

D_MODEL = 1024
DEPTH = 2
GRID_W = 64
CHUNK = 128
GM_GROUPS = 4
GM_WIDTH = 512
HEAD_DIM = 128
N_Q_HEADS = 4
N_KV_HEADS = 2
ATTN_WIDTH = 512
KV_W = 256
ROPE_THETA = 10000.0
FN_GROUPS = 4
FN_GDIM = 128
FN_WIDTH = 512
SSD_HEADDIM = 64
SSD_HEADS = 8
SSD_GROUPS = 2
SSD_HPG = 4
SSD_STATE = 128
SSD_WIDTH = 512
SSD_GN = 256
SSD_CONV_DIM = 1024
D_FF = 2816
CONV_W = 3
KV_OFF = 2 * GM_WIDTH + ATTN_WIDTH
SSM_OFF = FN_WIDTH + SSD_WIDTH
DN_ALPHA = (2 * DEPTH) ** 0.25
EPS = 1e-6

V7X_LANES = 128
V7X_SUBLANES = 8
V7X_VMEM_BYTES = 64 * 1024 * 1024
FF_CHUNK = 256

BF16 = jnp.bfloat16
F32 = jnp.float32
HALO = V7X_SUBLANES


def _vmem_limit(nbytes):
    return int(min(nbytes * 3 // 2 + (8 << 20), V7X_VMEM_BYTES - (6 << 20)))


def _resident(shape):
    nd = len(shape)
    return pl.BlockSpec(shape, lambda *_: (0,) * nd, pipeline_mode=pl.Buffered(1))


def _dot(a, b):
    return jnp.dot(a, b, preferred_element_type=F32)


def _dot_nt(a, b):
    return lax.dot_general(a, b, (((1,), (1,)), ((), ())), preferred_element_type=F32)


def _layer_norm(z, g, b):
    mu = jnp.mean(z, axis=-1, keepdims=True)
    zc = z - mu
    var = jnp.mean(zc * zc, axis=-1, keepdims=True)
    return zc * lax.rsqrt(var + EPS) * g + b


def _rms_norm(z, g):
    return z * lax.rsqrt(jnp.mean(z * z, axis=-1, keepdims=True) + EPS) * g


def _softplus(z):
    return jnp.maximum(z, 0.0) + jnp.log1p(jnp.exp(-jnp.abs(z)))


def _conv3(u, w, b, rows):
    n = u.shape[0]
    um = pltpu.roll(u, 1, 0)[HALO:HALO + rows]
    up = pltpu.roll(u, n - 1, 0)[HALO:HALO + rows]
    return b + w[0:1] * um + w[1:2] * u[HALO:HALO + rows] + w[2:3] * up


def _fill_halo_tile(hm_ref, x_ref, xp_ref, xn_ref, sh, sc, rows):
    t = pl.program_id(1)
    nt = pl.num_programs(1)
    keep_prev = jnp.where(t > 0, 1.0, 0.0)
    keep_next = jnp.where(t < nt - 1, 1.0, 0.0)
    hm_ref[HALO:HALO + rows, :] = (x_ref[0] * (1.0 + sc) + sh).astype(BF16)
    hm_ref[0:HALO, :] = ((xp_ref[0] * (1.0 + sc) + sh) * keep_prev).astype(BF16)
    hm_ref[HALO + rows:2 * HALO + rows, :] = ((xn_ref[0] * (1.0 + sc) + sh) * keep_next).astype(BF16)


def _halo_specs(rows, seq, d):
    nb = rows // HALO
    last = seq // HALO - 1
    return [
        pl.BlockSpec((1, rows, d), lambda b, t: (b, t, 0)),
        pl.BlockSpec((1, HALO, d), lambda b, t: (b, jnp.maximum(t * nb - 1, 0), 0)),
        pl.BlockSpec((1, HALO, d), lambda b, t: (b, jnp.minimum((t + 1) * nb, last), 0)),
    ]


def _row_spec(d):
    return pl.BlockSpec((1, 1, d), lambda b, t: (b, 0, 0))


def _mod_kernel(c_ref, w_ref, b_ref, o_ref):
    c = c_ref[...]
    act = c * jax.nn.sigmoid(c)
    o_ref[0] = jnp.dot(act, w_ref[0], preferred_element_type=F32,
                       precision=lax.Precision.HIGHEST) + b_ref[0]


def _modulation(cc, mod_w, mod_b):
    rows, d = cc.shape
    depth, _, n = mod_w.shape
    tn = 2048
    return pl.pallas_call(
        _mod_kernel,
        out_shape=jax.ShapeDtypeStruct((depth, rows, n), F32),
        grid=(depth, n // tn),
        in_specs=[
            pl.BlockSpec((rows, d), lambda i, j: (0, 0)),
            pl.BlockSpec((1, d, tn), lambda i, j: (i, 0, j)),
            pl.BlockSpec((1, 1, tn), lambda i, j: (i, 0, j)),
        ],
        out_specs=pl.BlockSpec((1, rows, tn), lambda i, j: (i, 0, j)),
        compiler_params=pltpu.CompilerParams(
            dimension_semantics=("arbitrary", "arbitrary"),
            vmem_limit_bytes=_vmem_limit(2 * d * tn * 4)),
        name="modulation",
    )(cc, mod_w, mod_b.reshape(depth, 1, n))


def _ab_in_kernel(*refs, use_rope, rows):
    (x_ref, sh_ref, sc_ref, w_ref, lng_ref, lnb_ref, ws_ref, bias_ref, qg_ref, kg_ref) = refs[:10]
    if use_rope:
        cos_ref, s1_ref, s2_ref = refs[10:13]
        a_ref, q_ref, k_ref, v_ref = refs[13:]
    else:
        a_ref, q_ref, k_ref, v_ref = refs[10:]

    h = (x_ref[0] * (1.0 + sc_ref[0]) + sh_ref[0]).astype(BF16)

    u = jax.nn.gelu(_dot(h, w_ref[:, 0:GM_WIDTH]))
    vn = _layer_norm(jax.nn.gelu(_dot(h, w_ref[:, GM_WIDTH:2 * GM_WIDTH])), lng_ref[...], lnb_ref[...])
    vb = vn.astype(BF16)
    gd = GM_WIDTH // GM_GROUPS
    for c in range(rows // CHUNK):
        r = slice(c * CHUNK, (c + 1) * CHUNK)
        for g in range(GM_GROUPS):
            l = slice(g * gd, (g + 1) * gd)
            mixed = _dot(ws_ref[g], vb[r, l]) + bias_ref[:, l]
            a_ref[0, r, l] = (u[r, l] * mixed).astype(BF16)

    def rope(z):
        if not use_rope:
            return z
        return (z * cos_ref[...] + pltpu.roll(z, 3 * HEAD_DIM // 4, 1) * s1_ref[...]
                + pltpu.roll(z, HEAD_DIM // 4, 1) * s2_ref[...])

    scale = HEAD_DIM ** -0.5 * math.log2(math.e)
    pq = _dot(h, w_ref[:, 2 * GM_WIDTH:KV_OFF])
    for hh in range(N_Q_HEADS):
        l = slice(hh * HEAD_DIM, (hh + 1) * HEAD_DIM)
        q_ref[0, :, l] = (rope(_rms_norm(pq[:, l], qg_ref[...])) * scale).astype(BF16)
    pk = _dot(h, w_ref[:, KV_OFF:KV_OFF + KV_W])
    for hh in range(N_KV_HEADS):
        l = slice(hh * HEAD_DIM, (hh + 1) * HEAD_DIM)
        k_ref[0, :, l] = rope(_rms_norm(pk[:, l], kg_ref[...])).astype(BF16)
    v_ref[0] = _dot(h, w_ref[:, KV_OFF + KV_W:KV_OFF + 2 * KV_W]).astype(BF16)


def _ab_in(x, sh, sc, w, lng, lnb, ws, bias2d, qg, kg, rope_tabs, rows):
    bsz, seq, d = x.shape
    n = w.shape[1]
    use_rope = rope_tabs is not None
    in_specs = [
        pl.BlockSpec((1, rows, d), lambda b, t: (b, t, 0)),
        _row_spec(d), _row_spec(d),
        _resident((d, n)),
        _resident((1, GM_WIDTH)), _resident((1, GM_WIDTH)),
        _resident((GM_GROUPS, CHUNK, CHUNK)),
        _resident((CHUNK, GM_WIDTH)),
        _resident((1, HEAD_DIM)), _resident((1, HEAD_DIM)),
    ]
    args = [x, sh, sc, w, lng, lnb, ws, bias2d, qg, kg]
    if use_rope:
        in_specs += [pl.BlockSpec((rows, HEAD_DIM), lambda b, t: (t, 0))] * 3
        args += list(rope_tabs)
    out_shape = [
        jax.ShapeDtypeStruct((bsz, seq, GM_WIDTH), BF16),
        jax.ShapeDtypeStruct((bsz, seq, ATTN_WIDTH), BF16),
        jax.ShapeDtypeStruct((bsz, seq, KV_W), BF16),
        jax.ShapeDtypeStruct((bsz, seq, KV_W), BF16),
    ]
    out_specs = [
        pl.BlockSpec((1, rows, GM_WIDTH), lambda b, t: (b, t, 0)),
        pl.BlockSpec((1, rows, ATTN_WIDTH), lambda b, t: (b, t, 0)),
        pl.BlockSpec((1, rows, KV_W), lambda b, t: (b, t, 0)),
        pl.BlockSpec((1, rows, KV_W), lambda b, t: (b, t, 0)),
    ]
    est = d * n * 2 + 2 * rows * d * 4 + 6 * rows * n * 4
    return pl.pallas_call(
        functools.partial(_ab_in_kernel, use_rope=use_rope, rows=rows),
        out_shape=out_shape,
        grid=(bsz, seq // rows),
        in_specs=in_specs,
        out_specs=out_specs,
        compiler_params=pltpu.CompilerParams(
            dimension_semantics=("arbitrary", "arbitrary"),
            vmem_limit_bytes=_vmem_limit(est)),
        name="ab_in_rope" if use_rope else "ab_in_ctx",
    )(*args)


def _attn_kernel(q_ref, k_ref, v_ref, o_ref, vt_ref, *, rows, blk):
    @pl.when(pl.program_id(2) == 0)
    def _():
        vt_ref[...] = v_ref[0].astype(F32).T

    k = k_ref[0]
    vt = vt_ref[...].astype(BF16)
    for h in range(N_Q_HEADS // N_KV_HEADS):
        l_ = slice(h * HEAD_DIM, (h + 1) * HEAD_DIM)
        for r0 in range(0, rows, blk):
            qt = q_ref[0, r0:r0 + blk, l_].astype(F32).T.astype(BF16)
            st = _dot(k, qt)
            m = jnp.max(st, axis=0, keepdims=True)
            p = jnp.exp2(st - m)
            l = jnp.sum(p, axis=0, keepdims=True)
            ot = _dot(vt, p.astype(BF16)) / l
            o_ref[0, r0:r0 + blk, l_] = ot.T.astype(BF16)


def _attention(q, k, v, rows, blk=512):
    bsz, sq, _ = q.shape
    sk = k.shape[1]
    gw = (N_Q_HEADS // N_KV_HEADS) * HEAD_DIM
    est = 4 * blk * sk * 12 + 4 * sk * HEAD_DIM * 2 + 4 * rows * gw * 2 + sk * HEAD_DIM * 4
    return pl.pallas_call(
        functools.partial(_attn_kernel, rows=rows, blk=min(blk, rows)),
        out_shape=jax.ShapeDtypeStruct((bsz, sq, ATTN_WIDTH), BF16),
        grid=(bsz, N_KV_HEADS, sq // rows),
        in_specs=[
            pl.BlockSpec((1, rows, gw), lambda b, h, t: (b, t, h)),
            pl.BlockSpec((1, sk, HEAD_DIM), lambda b, h, t: (b, 0, h)),
            pl.BlockSpec((1, sk, HEAD_DIM), lambda b, h, t: (b, 0, h)),
        ],
        out_specs=pl.BlockSpec((1, rows, gw), lambda b, h, t: (b, t, h)),
        scratch_shapes=[pltpu.VMEM((HEAD_DIM, sk), F32)],
        compiler_params=pltpu.CompilerParams(
            dimension_semantics=("arbitrary", "arbitrary", "arbitrary"),
            vmem_limit_bytes=_vmem_limit(est)),
        name="attention",
    )(q, k, v)


def _out_ln_kernel(a1_ref, a2_ref, w_ref, x_ref, g_ref, lng_ref, lnb_ref, o_ref):
    half = a1_ref.shape[2]
    y = _dot(a1_ref[0], w_ref[0:half, :]) + _dot(a2_ref[0], w_ref[half:2 * half, :])
    z = DN_ALPHA * x_ref[0] + g_ref[0] * y
    o_ref[0] = _layer_norm(z, lng_ref[...], lnb_ref[...])


def _out_ln(a1, a2, w, x, gate, lng, lnb, rows):
    bsz, seq, d = x.shape
    half = a1.shape[2]
    est = 2 * half * d * 2 + 4 * rows * d * 4 + 4 * rows * half * 2 + 2 * rows * d * 4
    return pl.pallas_call(
        _out_ln_kernel,
        out_shape=jax.ShapeDtypeStruct((bsz, seq, d), F32),
        grid=(bsz, seq // rows),
        in_specs=[
            pl.BlockSpec((1, rows, half), lambda b, t: (b, t, 0)),
            pl.BlockSpec((1, rows, half), lambda b, t: (b, t, 0)),
            _resident((2 * half, d)),
            pl.BlockSpec((1, rows, d), lambda b, t: (b, t, 0)),
            _row_spec(d),
            _resident((1, d)), _resident((1, d)),
        ],
        out_specs=pl.BlockSpec((1, rows, d), lambda b, t: (b, t, 0)),
        compiler_params=pltpu.CompilerParams(
            dimension_semantics=("arbitrary", "arbitrary"),
            vmem_limit_bytes=_vmem_limit(est)),
        name="out_ln",
    )(a1, a2, w, x, gate, lng, lnb)


def _ffn_kernel(x_ref, xp_ref, xn_ref, sh_ref, sc_ref, g_ref, wu1_ref, wu2_ref, cw1_ref, cw2_ref,
                cb1_ref, cb2_ref, wd_ref, lng_ref, lnb_ref, o_ref, hm_ref, u_ref, gt_ref, *, rows):
    _fill_halo_tile(hm_ref, x_ref, xp_ref, xn_ref, sh_ref[0], sc_ref[0], rows)
    n_chunks = wu1_ref.shape[0]
    tiles = rows // V7X_SUBLANES
    sub = lax.broadcasted_iota(jnp.int32, (tiles, V7X_SUBLANES, FF_CHUNK), 1)

    def up_proj(c, slot):
        hm = hm_ref[...]
        u_ref[slot, 0] = _dot(hm, wu1_ref[c]).reshape(tiles + 2, V7X_SUBLANES, FF_CHUNK)
        u_ref[slot, 1] = _dot(hm, wu2_ref[c]).reshape(tiles + 2, V7X_SUBLANES, FF_CHUNK)

    def conv_gate(c, slot):
        def conv(half, cw_ref, cb_ref):
            u = u_ref[slot, half]
            dn = pltpu.roll(u, 1, 1)
            up = pltpu.roll(u, V7X_SUBLANES - 1, 1)
            prev = jnp.where(sub == 0, dn[0:tiles], dn[1:tiles + 1])
            nxt = jnp.where(sub == V7X_SUBLANES - 1, up[2:tiles + 2], up[1:tiles + 1])
            w = cw_ref[c]
            return cb_ref[c] + w[0:1] * prev + w[1:2] * u[1:tiles + 1] + w[2:3] * nxt

        a1 = conv(0, cw1_ref, cb1_ref)
        a2 = conv(1, cw2_ref, cb2_ref)
        gt_ref[c] = (a1 * (a2 * jax.nn.sigmoid(a2))).reshape(rows, FF_CHUNK)

    up_proj(0, 0)

    def pair(k, carry):
        c = 2 * k
        up_proj(c + 1, 1)
        conv_gate(c, 0)
        up_proj(c + 2, 0)
        conv_gate(c + 1, 1)
        return carry

    lax.fori_loop(0, (n_chunks - 1) // 2, pair, 0)
    conv_gate(n_chunks - 1, 0)

    f = _dot(gt_ref[0].astype(BF16), wd_ref[0])
    for c in range(1, n_chunks):
        f = f + _dot(gt_ref[c].astype(BF16), wd_ref[c])
    z = DN_ALPHA * x_ref[0] + g_ref[0] * f
    o_ref[0] = _layer_norm(z, lng_ref[...], lnb_ref[...])


def _ffn(x, sh, sc, gate, wu1, wu2, cw1, cw2, cb1, cb2, wd, lng, lnb, rows):
    bsz, seq, d = x.shape
    nc = wu1.shape[0]
    assert nc % 2 == 1, "the pipeline peels one chunk and pairs the rest"
    est = (3 * nc * d * FF_CHUNK * 2 + 4 * rows * d * 4 + (rows + 2 * HALO) * d * 2
           + nc * rows * FF_CHUNK * 4 + 8 * (rows + 2 * HALO) * FF_CHUNK * 4 + 2 * rows * d * 4)
    return pl.pallas_call(
        functools.partial(_ffn_kernel, rows=rows),
        out_shape=jax.ShapeDtypeStruct((bsz, seq, d), F32),
        grid=(bsz, seq // rows),
        in_specs=_halo_specs(rows, seq, d) + [
            _row_spec(d), _row_spec(d), _row_spec(d),
            _resident((nc, d, FF_CHUNK)), _resident((nc, d, FF_CHUNK)),
            _resident((nc, CONV_W, FF_CHUNK)), _resident((nc, CONV_W, FF_CHUNK)),
            _resident((nc, 1, FF_CHUNK)), _resident((nc, 1, FF_CHUNK)),
            _resident((nc, FF_CHUNK, d)),
            _resident((1, d)), _resident((1, d)),
        ],
        out_specs=pl.BlockSpec((1, rows, d), lambda b, t: (b, t, 0)),
        scratch_shapes=[
            pltpu.VMEM((rows + 2 * HALO, d), BF16),
            pltpu.VMEM((2, 2, rows // HALO + 2, HALO, FF_CHUNK), F32),
            pltpu.VMEM((nc, rows, FF_CHUNK), F32),
        ],
        compiler_params=pltpu.CompilerParams(
            dimension_semantics=("arbitrary", "arbitrary"),
            vmem_limit_bytes=_vmem_limit(est)),
        name="conv_ffn",
    )(x, x, x, sh, sc, gate, wu1, wu2, cw1, cw2, cb1, cb2, wd, lng, lnb)


def _cd_in_kernel(*refs, with_fz, rows):
    x_ref, xp_ref, xn_ref, sh_ref, sc_ref = refs[:5]
    if with_fz:
        wfz_ref, wx_ref, wdt_ref, cw_ref, cb_ref, dtb_ref = refs[5:11]
        f_ref, z_ref, xs_ref, b_ref, c_ref, dt_ref, hm_ref = refs[11:]
    else:
        wx_ref, wdt_ref, cw_ref, cb_ref, dtb_ref = refs[5:10]
        xs_ref, b_ref, c_ref, dt_ref, hm_ref = refs[10:]
    _fill_halo_tile(hm_ref, x_ref, xp_ref, xn_ref, sh_ref[0], sc_ref[0], rows)
    hc = hm_ref[HALO:HALO + rows, :]
    if with_fz:
        pfz = _dot(hc, wfz_ref[...])
        f_ref[0] = pfz[:, 0:FN_WIDTH].astype(BF16)
        z_ref[0] = pfz[:, FN_WIDTH:SSM_OFF]
    a = _conv3(_dot(hm_ref[...], wx_ref[...]), cw_ref[...], cb_ref[...], rows)
    s = a * jax.nn.sigmoid(a)
    xs_ref[0] = s[:, 0:SSD_WIDTH]
    b_ref[0] = s[:, SSD_WIDTH:SSD_WIDTH + SSD_GN].astype(BF16)
    c_ref[0] = s[:, SSD_WIDTH + SSD_GN:SSD_CONV_DIM].astype(BF16)
    dt_ref[0] = _softplus(_dot(hc, wdt_ref[...]) + dtb_ref[...])


def _cd_in(x, sh, sc, wfz, wx, wdt, cw, cb, dtb, rows):
    bsz, seq, d = x.shape
    with_fz = wfz is not None
    in_specs = _halo_specs(rows, seq, d) + [_row_spec(d), _row_spec(d)]
    args = [x, x, x, sh, sc]
    out_shape, out_specs = [], []
    if with_fz:
        in_specs.append(_resident((d, SSM_OFF)))
        args.append(wfz)
        out_shape += [jax.ShapeDtypeStruct((bsz, seq, FN_WIDTH), BF16),
                      jax.ShapeDtypeStruct((bsz, seq, SSD_WIDTH), F32)]
        out_specs += [pl.BlockSpec((1, rows, FN_WIDTH), lambda b, t: (b, t, 0)),
                      pl.BlockSpec((1, rows, SSD_WIDTH), lambda b, t: (b, t, 0))]
    in_specs += [_resident((d, SSD_CONV_DIM)), _resident((d, V7X_LANES)),
                 _resident((CONV_W, SSD_CONV_DIM)), _resident((1, SSD_CONV_DIM)),
                 _resident((1, V7X_LANES))]
    args += [wx, wdt, cw, cb, dtb]
    out_shape += [jax.ShapeDtypeStruct((bsz, seq, SSD_WIDTH), F32),
                  jax.ShapeDtypeStruct((bsz, seq, SSD_GN), BF16),
                  jax.ShapeDtypeStruct((bsz, seq, SSD_GN), BF16),
                  jax.ShapeDtypeStruct((bsz, seq, V7X_LANES), F32)]
    out_specs += [pl.BlockSpec((1, rows, SSD_WIDTH), lambda b, t: (b, t, 0)),
                  pl.BlockSpec((1, rows, SSD_GN), lambda b, t: (b, t, 0)),
                  pl.BlockSpec((1, rows, SSD_GN), lambda b, t: (b, t, 0)),
                  pl.BlockSpec((1, rows, V7X_LANES), lambda b, t: (b, t, 0))]
    est = d * 2200 * 2 + 2 * rows * d * 4 + 8 * (rows + 2 * HALO) * SSD_CONV_DIM * 4
    return pl.pallas_call(
        functools.partial(_cd_in_kernel, with_fz=with_fz, rows=rows),
        out_shape=out_shape,
        grid=(bsz, seq // rows),
        in_specs=in_specs,
        out_specs=out_specs,
        scratch_shapes=[pltpu.VMEM((rows + 2 * HALO, d), BF16)],
        compiler_params=pltpu.CompilerParams(
            dimension_semantics=("arbitrary", "arbitrary"),
            vmem_limit_bytes=_vmem_limit(est)),
        name="cd_in" if with_fz else "cd_in_ctx",
    )(*args)


def _cumsum_rows(z, reverse):
    n = z.shape[0]
    idx = lax.broadcasted_iota(jnp.int32, z.shape, 0)
    k = 1
    while k < n:
        if reverse:
            z = z + jnp.where(idx < n - k, pltpu.roll(z, n - k, 0), 0.0)
        else:
            z = z + jnp.where(idx >= k, pltpu.roll(z, k, 0), 0.0)
        k *= 2
    return z


def _lane_expand(row, lane0, width):
    lane = lax.broadcasted_iota(jnp.int32, (1, SSD_HPG * width), 1)
    out = jnp.zeros((1, SSD_HPG * width), F32)
    for j in range(SSD_HPG):
        val = jnp.broadcast_to(row[:, lane0 + j:lane0 + j + 1], (1, SSD_HPG * width))
        out = jnp.where(lane // width == j, val, out)
    return out


def _ssd_chunk(x, bm, cm, dt, a_row, h_ref, lane0, reverse, need_y):
    q = x.shape[0]
    gw = SSD_HPG * SSD_HEADDIM
    acum = _cumsum_rows(dt * a_row, reverse)
    total = acum[0:1, :] if reverse else acum[q - 1:q, :]
    acum_t = acum.T
    dt_t = dt.T
    total_t = jnp.broadcast_to(acum_t[:, 0:1] if reverse else acum_t[:, q - 1:q], (q, q))
    wgt_t = dt_t * jnp.exp(total_t - acum_t)
    lane_g = lax.broadcasted_iota(jnp.int32, (q, gw), 1) // SSD_HEADDIM
    if need_y:
        srow_t = acum_t - jnp.log(dt_t)
        ti = lax.broadcasted_iota(jnp.int32, (q, q), 0)
        si = lax.broadcasted_iota(jnp.int32, (q, q), 1)
        mask = (si >= ti) if reverse else (si <= ti)
    ys = []
    for g in range(SSD_GROUPS):
        bg = bm[:, g * SSD_STATE:(g + 1) * SSD_STATE]
        xgb = x[:, g * gw:(g + 1) * gw].astype(BF16)
        x_heads = [jnp.where(lane_g == j, xgb, jnp.zeros_like(xgb)) for j in range(SSD_HPG)]
        hg = h_ref[g]
        if need_y:
            cg = cm[:, g * SSD_STATE:(g + 1) * SSD_STATE]
            cb = _dot_nt(cg, bg)
            cgf = cg.astype(F32)
            hgb = hg.astype(BF16)
            lhs, rhs = [], []
            for j in range(SSD_HPG):
                jl = lane0 + g * SSD_HPG + j
                col = jnp.broadcast_to(acum[:, jl:jl + 1], (q, q))
                row = jnp.broadcast_to(srow_t[jl:jl + 1, :], (q, q))
                lhs.append((cb * jnp.exp(jnp.where(mask, col - row, -jnp.inf))).astype(BF16))
                lhs.append((cgf * jnp.exp(col)).astype(BF16))
                rhs.append(x_heads[j])
                rhs.append(jnp.where(lane_g == j, hgb, jnp.zeros_like(hgb)))
            ys.append(_dot(jnp.concatenate(lhs, axis=1), jnp.concatenate(rhs, axis=0)))
        bgt = bg.astype(F32).T
        wb = [(bgt * jnp.broadcast_to(wgt_t[lane0 + g * SSD_HPG + j:lane0 + g * SSD_HPG + j + 1, :],
                                      (SSD_STATE, q))).astype(BF16) for j in range(SSD_HPG)]
        st = _dot(jnp.concatenate(wb, axis=1), jnp.concatenate(x_heads, axis=0))
        h_ref[g] = hg * jnp.exp(_lane_expand(total, lane0 + g * SSD_HPG, SSD_HEADDIM)) + st
    if need_y:
        return jnp.concatenate(ys, axis=1)
    return None


def _ssd_kernel(xc_ref, bc_ref, dtc_ref, xl_ref, bl_ref, cl_ref, dtl_ref, z_ref, alog_ref,
                dskip_ref, ng_ref, o_ref, y_ref, h_ref):
    a_row = -jnp.exp(alog_ref[...])
    nc_ctx = xc_ref.shape[1] // CHUNK
    nc_lat = xl_ref.shape[1] // CHUNK

    def rows_of(c):
        return pl.ds(pl.multiple_of(c * CHUNK, CHUNK), CHUNK)

    def ctx_step(c, direction):
        r = rows_of(c)
        _ssd_chunk(xc_ref[0, r, :], bc_ref[0, r, :], None, dtc_ref[0, r, :], a_row,
                   h_ref.at[direction], direction * SSD_HEADS, direction == 1, False)

    def lat_step(c, direction):
        r = rows_of(c)
        y_ref[direction, r, :] = _ssd_chunk(
            xl_ref[0, r, :], bl_ref[0, r, :], cl_ref[0, r, :], dtl_ref[0, r, :], a_row,
            h_ref.at[direction], direction * SSD_HEADS, direction == 1, True)

    h_ref[...] = jnp.zeros_like(h_ref)

    def ctx_pair(i, carry):
        ctx_step(i, 0)
        ctx_step(nc_ctx - 1 - i, 1)
        return carry

    lax.fori_loop(0, nc_ctx, ctx_pair, 0)

    def lat_pair(i, carry):
        lat_step(i, 0)
        lat_step(nc_lat - 1 - i, 1)
        return carry

    lax.fori_loop(0, nc_lat, lat_pair, 0)

    def finish(c, carry):
        r = rows_of(c)
        zz = z_ref[0, r, :]
        yt = (y_ref[0, r, :] + y_ref[1, r, :] + xl_ref[0, r, :] * dskip_ref[...]) * (zz * jax.nn.sigmoid(zz))
        o_ref[0, r, :] = _rms_norm(yt, ng_ref[...]).astype(BF16)
        return carry

    lax.fori_loop(0, nc_lat, finish, 0)


def _ssd(xs_c, b_c, dt_c, xs_l, b_l, c_l, dt_l, z, alog_row, dskip_row, ng_row):
    bsz, sl, w = xs_l.shape
    sc = xs_c.shape[1]
    gw = SSD_HPG * SSD_HEADDIM

    def full(s, n):
        return pl.BlockSpec((1, s, n), lambda b: (b, 0, 0))

    est = (2 * (sl + sc) * (w * 4 + 2 * SSD_GN * 2 + V7X_LANES * 4) + 2 * sl * w * 4
           + 2 * sl * w * 2 + 2 * sl * w * 4)
    return pl.pallas_call(
        _ssd_kernel,
        out_shape=jax.ShapeDtypeStruct((bsz, sl, w), BF16),
        grid=(bsz,),
        in_specs=[
            full(sc, w), full(sc, SSD_GN), full(sc, V7X_LANES),
            full(sl, w), full(sl, SSD_GN), full(sl, SSD_GN), full(sl, V7X_LANES),
            full(sl, w),
            _resident((1, V7X_LANES)), _resident((1, w)), _resident((1, w)),
        ],
        out_specs=full(sl, w),
        scratch_shapes=[
            pltpu.VMEM((2, sl, w), F32),
            pltpu.VMEM((2, SSD_GROUPS, SSD_STATE, gw), F32),
        ],
        compiler_params=pltpu.CompilerParams(
            dimension_semantics=("arbitrary",),
            vmem_limit_bytes=_vmem_limit(est)),
        name="ssd_scan",
    )(xs_c, b_c, dt_c, xs_l, b_l, c_l, dt_l, z, alog_row, dskip_row, ng_row)


def _fourier_kernel(f_ref, cc_ref, sc_ref, w2_ref, o_ref, y_ref, *, rows):
    seq = f_ref.shape[1]
    for g in range(FN_GROUPS):
        l = slice(g * FN_GDIM, (g + 1) * FN_GDIM)
        fg = f_ref[0, :, l]
        y_ref[0:seq, l] = _dot(fg, cc_ref[...]).astype(BF16)
        y_ref[seq:2 * seq, l] = _dot(fg, sc_ref[...]).astype(BF16)
    norm = 1.0 / math.sqrt(seq * FN_GDIM)

    def tile(i, carry):
        r = pl.ds(pl.multiple_of(i * rows, rows), rows)
        o_ref[0, r, :] = (_dot(w2_ref[r, :], y_ref[...]) * norm).astype(BF16)
        return carry

    lax.fori_loop(0, seq // rows, tile, 0)


def _fourier(f, cc, sc, w2, rows):
    bsz, seq, w = f.shape
    est = seq * 2 * seq * 2 + 4 * seq * w * 2 + 2 * seq * w * 2 + 4 * rows * 2 * seq * 2
    return pl.pallas_call(
        functools.partial(_fourier_kernel, rows=rows),
        out_shape=jax.ShapeDtypeStruct((bsz, seq, w), BF16),
        grid=(bsz,),
        in_specs=[
            pl.BlockSpec((1, seq, w), lambda b: (b, 0, 0)),
            _resident((FN_GDIM, FN_GDIM)), _resident((FN_GDIM, FN_GDIM)),
            _resident((seq, 2 * seq)),
        ],
        out_specs=pl.BlockSpec((1, seq, w), lambda b: (b, 0, 0)),
        scratch_shapes=[pltpu.VMEM((2 * seq, w), BF16)],
        compiler_params=pltpu.CompilerParams(
            dimension_semantics=("arbitrary",),
            vmem_limit_bytes=_vmem_limit(est)),
        name="fourier_mix",
    )(f, cc, sc, w2)


def _rope_tables(seq):
    rows = seq // GRID_W
    row = jnp.repeat(jnp.arange(rows, dtype=F32), GRID_W)
    col = jnp.broadcast_to(jnp.arange(GRID_W, dtype=F32), (rows, GRID_W)).reshape(-1)
    n_freq = HEAD_DIM // 4
    inv = ROPE_THETA ** (-jnp.arange(n_freq, dtype=F32) / n_freq)
    ang = jnp.stack([row, col], axis=-1)[:, :, None] * inv
    cos, sin = jnp.cos(ang), jnp.sin(ang)
    zero = jnp.zeros_like(sin[:, 0])
    cos_t = jnp.concatenate([cos[:, 0], cos[:, 0], cos[:, 1], cos[:, 1]], axis=-1)
    s1 = jnp.concatenate([-sin[:, 0], zero, -sin[:, 1], zero], axis=-1)
    s2 = jnp.concatenate([zero, sin[:, 0], zero, sin[:, 1]], axis=-1)
    return cos_t, s1, s2


def _dft_tables(seq):
    def tw(n):
        i = jnp.arange(n, dtype=jnp.int32)
        ang = ((i[:, None] * i[None, :]) % n).astype(F32) * (2.0 * math.pi / n)
        return jnp.cos(ang), jnp.sin(ang)

    cc, sc = tw(FN_GDIM)
    cl, sl = tw(seq)
    return cc.astype(BF16), sc.astype(BF16), jnp.concatenate([cl, -sl], axis=1).astype(BF16)


def _ffn_params(w_up, conv_w, conv_b, w_down):
    d = w_up.shape[0]
    nc = D_FF // FF_CHUNK

    def cols(m, lo):
        r = m.shape[0]
        return m[:, lo:lo + D_FF].reshape(r, nc, FF_CHUNK).transpose(1, 0, 2)

    wu = w_up.astype(BF16)
    cb = conv_b.reshape(1, 2 * D_FF)
    return (cols(wu, 0), cols(wu, D_FF), cols(conv_w, 0), cols(conv_w, D_FF),
            cols(cb, 0), cols(cb, D_FF), w_down.astype(BF16).reshape(nc, FF_CHUNK, d))


def kernel(x, c, ctx, c_ctx, mod_w, mod_b, ln1_g, ln1_b, ln2_g, ln2_b, ffn_up, ffn_conv_w, ffn_conv_b,
           ffn_down, ab_w_in, ab_w_out, gm_ln_g, gm_ln_b, gm_ws, gm_bs, q_norm_g, k_norm_g, cd_w_in,
           cd_w_out, ssd_conv_w, ssd_conv_b, ssd_dt_bias, ssd_a_log, ssd_d, ssd_norm_g):
    bsz, seq, d = x.shape
    ctx_len = ctx.shape[1]
    lat_rows = 512
    ffn_rows = 512
    ctx_rows = min(256, ctx_len)

    pad = (-(bsz + 1)) % V7X_SUBLANES
    cc = jnp.concatenate([c, c_ctx[None], jnp.zeros((pad, d), F32)], axis=0)
    mod = _modulation(cc, mod_w, mod_b)
    rope_tabs = _rope_tables(seq)

    def row(v):
        return v.reshape(1, -1)

    xl, xc = x, ctx
    for i in range(DEPTH):
        last = i == DEPTH - 1
        j = i // 2
        ml = mod[i, :bsz].reshape(bsz, 6, 1, d)
        mc = jnp.broadcast_to(mod[i, bsz].reshape(1, 6, 1, d), (bsz, 6, 1, d))
        sh1, sc1, g1, sh2, sc2, g2 = (ml[:, k] for k in range(6))
        csh1, csc1, cg1, csh2, csc2, cg2 = (mc[:, k] for k in range(6))
        ln1 = (row(ln1_g[i]), row(ln1_b[i]))
        ln2 = (row(ln2_g[i]), row(ln2_b[i]))
        ffn_p = _ffn_params(ffn_up[i], ffn_conv_w[i], ffn_conv_b[i], ffn_down[i])

        if i % 2 == 0:
            w_in = ab_w_in[j].astype(BF16)
            w_out = ab_w_out[j].astype(BF16)
            gm = (row(gm_ln_g[j]), row(gm_ln_b[j]), gm_ws[j].astype(BF16),
                  jnp.repeat(gm_bs[j].T, GM_WIDTH // GM_GROUPS, axis=1),
                  row(q_norm_g[j]), row(k_norm_g[j]))
            a_l, q_l, k_l, v_l = _ab_in(xl, sh1, sc1, w_in, *gm, rope_tabs, lat_rows)
            a_c, q_c, k_c, v_c = _ab_in(xc, csh1, csc1, w_in, *gm, None, ctx_rows)
            o_l = _attention(q_l, jnp.concatenate([k_c, k_l], axis=1),
                             jnp.concatenate([v_c, v_l], axis=1), 512)
            mix_l = (a_l, o_l)
            if not last:
                mix_c = (a_c, _attention(q_c, k_c, v_c, ctx_rows))
        else:
            w_in = cd_w_in[j].astype(BF16)
            w_out = cd_w_out[j].astype(BF16)
            wfz = w_in[:, :SSM_OFF]
            wx = w_in[:, SSM_OFF:SSM_OFF + SSD_CONV_DIM]
            ndt = 2 * SSD_HEADS
            wdt = jnp.pad(w_in[:, SSM_OFF + SSD_CONV_DIM:], ((0, 0), (0, V7X_LANES - ndt)))
            dtb = jnp.pad(ssd_dt_bias[j].reshape(1, ndt), ((0, 0), (0, V7X_LANES - ndt)))
            alog = jnp.pad(ssd_a_log[j].reshape(1, ndt), ((0, 0), (0, V7X_LANES - ndt)))
            cw, cb = ssd_conv_w[j], row(ssd_conv_b[j])
            dskip = row(jnp.repeat(ssd_d[j], SSD_HEADDIM))
            f_l, z_l, xs_l, b_l, c_l, dt_l = _cd_in(xl, sh1, sc1, wfz, wx, wdt, cw, cb, dtb, lat_rows)
            if last:
                xs_c, b_c, _, dt_c = _cd_in(xc, csh1, csc1, None, wx, wdt, cw, cb, dtb, ctx_rows)
            else:
                raise NotImplementedError("odd layers with a context output are not part of this block")
            s_l = _ssd(xs_c, b_c, dt_c, xs_l, b_l, c_l, dt_l, z_l, alog, dskip, row(ssd_norm_g[j]))
            mix_l = (_fourier(f_l, *_dft_tables(seq), 512), s_l)

        xl = _out_ln(*mix_l, w_out, xl, g1, *ln1, 1024)
        xl = _ffn(xl, sh2, sc2, g2, *ffn_p, *ln2, ffn_rows)
        if not last:
            xc = _out_ln(*mix_c, w_out, xc, cg1, *ln1, ctx_rows)
            xc = _ffn(xc, csh2, csc2, cg2, *ffn_p, *ln2, ctx_rows)
    return xl
```

```python
import functools
import math

import jax
import jax.numpy as jnp
import numpy as np
from jax import lax
from jax.experimental import pallas as pl
from jax.experimental.pallas import tpu as pltpu

D_MODEL = 1024
DEPTH = 2
GRID_W = 64
CHUNK = 128
GM_GROUPS = 4
GM_WIDTH = 512
HEAD_DIM = 128
N_Q_HEADS = 4
N_KV_HEADS = 2
ATTN_WIDTH = 512
KV_W = 256
ROPE_THETA = 10000.0
FN_GROUPS = 4
FN_GDIM = 128
FN_WIDTH = 512
SSD_HEADDIM = 64
SSD_HEADS = 8
SSD_GROUPS = 2
SSD_HPG = 4
SSD_STATE = 128
SSD_WIDTH = 512
SSD_GN = 256
SSD_CONV_DIM = 1024
D_FF = 2816
CONV_W = 3
KV_OFF = 2 * GM_WIDTH + ATTN_WIDTH
SSM_OFF = FN_WIDTH + SSD_WIDTH
DN_ALPHA = (2 * DEPTH) ** 0.25
EPS = 1e-6

V7X_LANES = 128
V7X_SUBLANES = 8
V7X_VMEM_BYTES = 64 * 1024 * 1024
FF_CHUNK = 256

BF16 = jnp.bfloat16
F32 = jnp.float32
HALO = V7X_SUBLANES


def _vmem_limit(nbytes):
    return int(min(nbytes * 3 // 2 + (8 << 20), V7X_VMEM_BYTES - (6 << 20)))


def _resident(shape):
    nd = len(shape)
    return pl.BlockSpec(shape, lambda *_: (0,) * nd, pipeline_mode=pl.Buffered(1))


def _dot(a, b):
    return jnp.dot(a, b, preferred_element_type=F32)


def _dot_nt(a, b):
    return lax.dot_general(a, b, (((1,), (1,)), ((), ())), preferred_element_type=F32)


def _layer_norm(z, g, b):
    mu = jnp.mean(z, axis=-1, keepdims=True)
    zc = z - mu
    var = jnp.mean(zc * zc, axis=-1, keepdims=True)
    return zc * lax.rsqrt(var + EPS) * g + b


def _rms_norm(z, g):
    return z * lax.rsqrt(jnp.mean(z * z, axis=-1, keepdims=True) + EPS) * g


def _softplus(z):
    return jnp.maximum(z, 0.0) + jnp.log1p(jnp.exp(-jnp.abs(z)))


def _gelu_tanh(z):
    c1 = math.sqrt(2.0 / math.pi)
    h = 0.5 * z
    return h + h * jnp.tanh(z * (c1 + (c1 * 0.044715) * (z * z)))


def _silu(z):
    h = 0.5 * z
    return h + h * jnp.tanh(h)


def _conv3(u, w, b, rows):
    n = u.shape[0]
    um = pltpu.roll(u, 1, 0)[HALO:HALO + rows]
    up = pltpu.roll(u, n - 1, 0)[HALO:HALO + rows]
    return b + w[0:1] * um + w[1:2] * u[HALO:HALO + rows] + w[2:3] * up


def _fill_halo_tile(hm_ref, x_ref, xp_ref, xn_ref, sh, sc, rows):
    t = pl.program_id(1)
    nt = pl.num_programs(1)
    keep_prev = jnp.where(t > 0, 1.0, 0.0)
    keep_next = jnp.where(t < nt - 1, 1.0, 0.0)
    hm_ref[HALO:HALO + rows, :] = (x_ref[0] * (1.0 + sc) + sh).astype(BF16)
    hm_ref[0:HALO, :] = ((xp_ref[0] * (1.0 + sc) + sh) * keep_prev).astype(BF16)
    hm_ref[HALO + rows:2 * HALO + rows, :] = ((xn_ref[0] * (1.0 + sc) + sh) * keep_next).astype(BF16)


def _halo_specs(rows, seq, d):
    nb = rows // HALO
    last = seq // HALO - 1
    return [
        pl.BlockSpec((1, rows, d), lambda b, t: (b, t, 0)),
        pl.BlockSpec((1, HALO, d), lambda b, t: (b, jnp.maximum(t * nb - 1, 0), 0)),
        pl.BlockSpec((1, HALO, d), lambda b, t: (b, jnp.minimum((t + 1) * nb, last), 0)),
    ]


def _row_spec(d):
    return pl.BlockSpec((1, 1, d), lambda b, t: (b, 0, 0))


def _mod_kernel(c_ref, w_ref, b_ref, o_ref):
    c = c_ref[...]
    act = c * jax.nn.sigmoid(c)
    o_ref[0] = jnp.dot(act, w_ref[0], preferred_element_type=F32,
                       precision=lax.Precision.HIGHEST) + b_ref[0]


def _modulation(cc, mod_w, mod_b):
    rows, d = cc.shape
    depth, _, n = mod_w.shape
    tn = 2048
    return pl.pallas_call(
        _mod_kernel,
        out_shape=jax.ShapeDtypeStruct((depth, rows, n), F32),
        grid=(depth, n // tn),
        in_specs=[
            pl.BlockSpec((rows, d), lambda i, j: (0, 0)),
            pl.BlockSpec((1, d, tn), lambda i, j: (i, 0, j)),
            pl.BlockSpec((1, 1, tn), lambda i, j: (i, 0, j)),
        ],
        out_specs=pl.BlockSpec((1, rows, tn), lambda i, j: (i, 0, j)),
        compiler_params=pltpu.CompilerParams(
            dimension_semantics=("arbitrary", "arbitrary"),
            vmem_limit_bytes=_vmem_limit(2 * d * tn * 4)),
        name="modulation",
    )(cc, mod_w, mod_b.reshape(depth, 1, n))


def _ab_in_kernel(*refs, use_rope, rows):
    (x_ref, sh_ref, sc_ref, w_ref, lng_ref, lnb_ref, ws_ref, bias_ref, qg_ref, kg_ref) = refs[:10]
    if use_rope:
        cos_ref, s1_ref, s2_ref = refs[10:13]
    a_ref, q_ref, k_ref, v_ref = refs[-4:]

    h = (x_ref[0] * (1.0 + sc_ref[0]) + sh_ref[0]).astype(BF16)

    u = _gelu_tanh(_dot(h, w_ref[:, 0:GM_WIDTH]))
    vn = _layer_norm(_gelu_tanh(_dot(h, w_ref[:, GM_WIDTH:2 * GM_WIDTH])), lng_ref[...], lnb_ref[...])
    vb = vn.astype(BF16)
    gd = GM_WIDTH // GM_GROUPS
    for c in range(rows // CHUNK):
        r = slice(c * CHUNK, (c + 1) * CHUNK)
        for g in range(GM_GROUPS):
            l = slice(g * gd, (g + 1) * gd)
            mixed = _dot(ws_ref[g], vb[r, l]) + bias_ref[:, l]
            a_ref[0, r, l] = (u[r, l] * mixed).astype(BF16)

    def rope(z):
        if not use_rope:
            return z
        return (z * cos_ref[...] + pltpu.roll(z, 3 * HEAD_DIM // 4, 1) * s1_ref[...]
                + pltpu.roll(z, HEAD_DIM // 4, 1) * s2_ref[...])

    scale = HEAD_DIM ** -0.5 * math.log2(math.e)
    pq = _dot(h, w_ref[:, 2 * GM_WIDTH:KV_OFF])
    for hh in range(N_Q_HEADS):
        l = slice(hh * HEAD_DIM, (hh + 1) * HEAD_DIM)
        q_ref[0, :, l] = (rope(_rms_norm(pq[:, l], qg_ref[...])) * scale).astype(BF16)
    pk = _dot(h, w_ref[:, KV_OFF:KV_OFF + KV_W])
    for hh in range(N_KV_HEADS):
        l = slice(hh * HEAD_DIM, (hh + 1) * HEAD_DIM)
        k_ref[0, :, l] = rope(_rms_norm(pk[:, l], kg_ref[...])).astype(BF16)
    v_ref[0] = _dot(h, w_ref[:, KV_OFF + KV_W:KV_OFF + 2 * KV_W]).astype(BF16)


def _ab_in(x, sh, sc, w, lng, lnb, ws, bias2d, qg, kg, rope_tabs, rows, kv_len, kv_row0, kv_into=None):
    bsz, seq, d = x.shape
    assert kv_row0 % rows == 0
    kv_blk0 = kv_row0 // rows
    n = w.shape[1]
    use_rope = rope_tabs is not None
    in_specs = [
        pl.BlockSpec((1, rows, d), lambda b, t: (b, t, 0)),
        _row_spec(d), _row_spec(d),
        _resident((d, n)),
        _resident((1, GM_WIDTH)), _resident((1, GM_WIDTH)),
        _resident((GM_GROUPS, CHUNK, CHUNK)),
        _resident((CHUNK, GM_WIDTH)),
        _resident((1, HEAD_DIM)), _resident((1, HEAD_DIM)),
    ]
    args = [x, sh, sc, w, lng, lnb, ws, bias2d, qg, kg]
    if use_rope:
        in_specs += [pl.BlockSpec((rows, HEAD_DIM), lambda b, t: (t, 0))] * 3
        args += list(rope_tabs)
    aliases = {}
    if kv_into is not None:
        aliases = {len(args): 2, len(args) + 1: 3}
        in_specs += [pl.BlockSpec(memory_space=pl.ANY)] * 2
        args += list(kv_into)
    out_shape = [
        jax.ShapeDtypeStruct((bsz, seq, GM_WIDTH), BF16),
        jax.ShapeDtypeStruct((bsz, seq, ATTN_WIDTH), BF16),
        jax.ShapeDtypeStruct((bsz, kv_len, KV_W), BF16),
        jax.ShapeDtypeStruct((bsz, kv_len, KV_W), BF16),
    ]
    out_specs = [
        pl.BlockSpec((1, rows, GM_WIDTH), lambda b, t: (b, t, 0)),
        pl.BlockSpec((1, rows, ATTN_WIDTH), lambda b, t: (b, t, 0)),
        pl.BlockSpec((1, rows, KV_W), lambda b, t: (b, t + kv_blk0, 0)),
        pl.BlockSpec((1, rows, KV_W), lambda b, t: (b, t + kv_blk0, 0)),
    ]
    est = d * n * 2 + 2 * rows * d * 4 + 6 * rows * n * 4
    return pl.pallas_call(
        functools.partial(_ab_in_kernel, use_rope=use_rope, rows=rows),
        out_shape=out_shape,
        grid=(bsz, seq // rows),
        in_specs=in_specs,
        out_specs=out_specs,
        input_output_aliases=aliases,
        compiler_params=pltpu.CompilerParams(
            dimension_semantics=("arbitrary", "arbitrary"),
            vmem_limit_bytes=_vmem_limit(est)),
        name="ab_in_rope" if use_rope else "ab_in_ctx",
    )(*args)


def _attn_kernel(q_ref, k_ref, v_ref, o_ref, vt_ref, *, rows, blk):
    @pl.when(pl.program_id(2) == 0)
    def _():
        vt_ref[...] = v_ref[0].astype(F32).T

    k = k_ref[0]
    vt = vt_ref[...].astype(BF16)
    for h in range(N_Q_HEADS // N_KV_HEADS):
        l_ = slice(h * HEAD_DIM, (h + 1) * HEAD_DIM)
        for r0 in range(0, rows, blk):
            qt = q_ref[0, r0:r0 + blk, l_].astype(F32).T.astype(BF16)
            st = _dot(k, qt)
            m = jnp.max(st, axis=0, keepdims=True)
            p = jnp.exp2(st - m)
            l = jnp.sum(p, axis=0, keepdims=True)
            ot = _dot(vt, p.astype(BF16)) / l
            o_ref[0, r0:r0 + blk, l_] = ot.T.astype(BF16)


def _attention(q, k, v, sk, rows, blk=512):
    bsz, sq, _ = q.shape
    gw = (N_Q_HEADS // N_KV_HEADS) * HEAD_DIM
    est = 4 * blk * sk * 12 + 4 * sk * HEAD_DIM * 2 + 4 * rows * gw * 2 + sk * HEAD_DIM * 4
    return pl.pallas_call(
        functools.partial(_attn_kernel, rows=rows, blk=min(blk, rows)),
        out_shape=jax.ShapeDtypeStruct((bsz, sq, ATTN_WIDTH), BF16),
        grid=(bsz, N_KV_HEADS, sq // rows),
        in_specs=[
            pl.BlockSpec((1, rows, gw), lambda b, h, t: (b, t, h)),
            pl.BlockSpec((1, sk, HEAD_DIM), lambda b, h, t: (b, 0, h)),
            pl.BlockSpec((1, sk, HEAD_DIM), lambda b, h, t: (b, 0, h)),
        ],
        out_specs=pl.BlockSpec((1, rows, gw), lambda b, h, t: (b, t, h)),
        scratch_shapes=[pltpu.VMEM((HEAD_DIM, sk), F32)],
        compiler_params=pltpu.CompilerParams(
            dimension_semantics=("arbitrary", "arbitrary", "arbitrary"),
            vmem_limit_bytes=_vmem_limit(est)),
        name="attention",
    )(q, k, v)


def _out_ln_kernel(a1_ref, a2_ref, w_ref, x_ref, g_ref, lng_ref, lnb_ref, o_ref):
    half = a1_ref.shape[2]
    y = _dot(a1_ref[0], w_ref[0:half, :]) + _dot(a2_ref[0], w_ref[half:2 * half, :])
    z = DN_ALPHA * x_ref[0] + g_ref[0] * y
    o_ref[0] = _layer_norm(z, lng_ref[...], lnb_ref[...])


def _out_ln(a1, a2, w, x, gate, lng, lnb, rows):
    bsz, seq, d = x.shape
    half = a1.shape[2]
    est = 2 * half * d * 2 + 4 * rows * d * 4 + 4 * rows * half * 2 + 2 * rows * d * 4
    return pl.pallas_call(
        _out_ln_kernel,
        out_shape=jax.ShapeDtypeStruct((bsz, seq, d), F32),
        grid=(bsz, seq // rows),
        in_specs=[
            pl.BlockSpec((1, rows, half), lambda b, t: (b, t, 0)),
            pl.BlockSpec((1, rows, half), lambda b, t: (b, t, 0)),
            _resident((2 * half, d)),
            pl.BlockSpec((1, rows, d), lambda b, t: (b, t, 0)),
            _row_spec(d),
            _resident((1, d)), _resident((1, d)),
        ],
        out_specs=pl.BlockSpec((1, rows, d), lambda b, t: (b, t, 0)),
        compiler_params=pltpu.CompilerParams(
            dimension_semantics=("arbitrary", "arbitrary"),
            vmem_limit_bytes=_vmem_limit(est)),
        name="out_ln",
    )(a1, a2, w, x, gate, lng, lnb)


def _ffn_kernel(x_ref, xp_ref, xn_ref, sh_ref, sc_ref, g_ref, wu1_ref, wu2_ref, cw1_ref, cw2_ref,
                cb1_ref, cb2_ref, wd_ref, lng_ref, lnb_ref, o_ref, hm_ref, u_ref, gt_ref, *, rows):
    _fill_halo_tile(hm_ref, x_ref, xp_ref, xn_ref, sh_ref[0], sc_ref[0], rows)
    n_chunks = wu1_ref.shape[0]
    tiles = rows // V7X_SUBLANES
    sub = lax.broadcasted_iota(jnp.int32, (tiles, V7X_SUBLANES, FF_CHUNK), 1)

    def up_proj(c, slot):
        hm = hm_ref[...]
        u_ref[slot, 0] = _dot(hm, wu1_ref[c]).reshape(tiles + 2, V7X_SUBLANES, FF_CHUNK)
        u_ref[slot, 1] = _dot(hm, wu2_ref[c]).reshape(tiles + 2, V7X_SUBLANES, FF_CHUNK)

    def conv_gate(c, slot):
        def conv(half, cw_ref, cb_ref):
            u = u_ref[slot, half]
            dn = pltpu.roll(u, 1, 1)
            up = pltpu.roll(u, V7X_SUBLANES - 1, 1)
            prev = jnp.where(sub == 0, dn[0:tiles], dn[1:tiles + 1])
            nxt = jnp.where(sub == V7X_SUBLANES - 1, up[2:tiles + 2], up[1:tiles + 1])
            w = cw_ref[c]
            return cb_ref[c] + w[0:1] * prev + w[1:2] * u[1:tiles + 1] + w[2:3] * nxt

        a1 = conv(0, cw1_ref, cb1_ref)
        a2 = conv(1, cw2_ref, cb2_ref)
        gt_ref[c] = (a1 * _silu(a2)).reshape(rows, FF_CHUNK)

    up_proj(0, 0)

    def pair(k, carry):
        c = 2 * k
        up_proj(c + 1, 1)
        conv_gate(c, 0)
        up_proj(c + 2, 0)
        conv_gate(c + 1, 1)
        return carry

    lax.fori_loop(0, (n_chunks - 1) // 2, pair, 0)
    conv_gate(n_chunks - 1, 0)

    f = _dot(gt_ref[0].astype(BF16), wd_ref[0])
    for c in range(1, n_chunks):
        f = f + _dot(gt_ref[c].astype(BF16), wd_ref[c])
    z = DN_ALPHA * x_ref[0] + g_ref[0] * f
    o_ref[0] = _layer_norm(z, lng_ref[...], lnb_ref[...])


def _ffn(x, sh, sc, gate, wu1, wu2, cw1, cw2, cb1, cb2, wd, lng, lnb, rows):
    bsz, seq, d = x.shape
    nc = wu1.shape[0]
    assert nc % 2 == 1, "the pipeline peels one chunk and pairs the rest"
    est = (3 * nc * d * FF_CHUNK * 2 + 4 * rows * d * 4 + (rows + 2 * HALO) * d * 2
           + nc * rows * FF_CHUNK * 4 + 8 * (rows + 2 * HALO) * FF_CHUNK * 4 + 2 * rows * d * 4)
    return pl.pallas_call(
        functools.partial(_ffn_kernel, rows=rows),
        out_shape=jax.ShapeDtypeStruct((bsz, seq, d), F32),
        grid=(bsz, seq // rows),
        in_specs=_halo_specs(rows, seq, d) + [
            _row_spec(d), _row_spec(d), _row_spec(d),
            _resident((nc, d, FF_CHUNK)), _resident((nc, d, FF_CHUNK)),
            _resident((nc, CONV_W, FF_CHUNK)), _resident((nc, CONV_W, FF_CHUNK)),
            _resident((nc, 1, FF_CHUNK)), _resident((nc, 1, FF_CHUNK)),
            _resident((nc, FF_CHUNK, d)),
            _resident((1, d)), _resident((1, d)),
        ],
        out_specs=pl.BlockSpec((1, rows, d), lambda b, t: (b, t, 0)),
        scratch_shapes=[
            pltpu.VMEM((rows + 2 * HALO, d), BF16),
            pltpu.VMEM((2, 2, rows // HALO + 2, HALO, FF_CHUNK), F32),
            pltpu.VMEM((nc, rows, FF_CHUNK), F32),
        ],
        compiler_params=pltpu.CompilerParams(
            dimension_semantics=("arbitrary", "arbitrary"),
            vmem_limit_bytes=_vmem_limit(est)),
        name="conv_ffn",
    )(x, x, x, sh, sc, gate, wu1, wu2, cw1, cw2, cb1, cb2, wd, lng, lnb)


def _cd_in_kernel(*refs, with_fz, rows):
    x_ref, xp_ref, xn_ref, sh_ref, sc_ref = refs[:5]
    if with_fz:
        wfz_ref, wx_ref, wdt_ref, cw_ref, cb_ref, dtb_ref = refs[5:11]
        f_ref, z_ref, xs_ref, b_ref, c_ref, dt_ref, hm_ref = refs[11:]
    else:
        wx_ref, wdt_ref, cw_ref, cb_ref, dtb_ref = refs[5:10]
        xs_ref, b_ref, c_ref, dt_ref, hm_ref = refs[10:]
    _fill_halo_tile(hm_ref, x_ref, xp_ref, xn_ref, sh_ref[0], sc_ref[0], rows)
    hc = hm_ref[HALO:HALO + rows, :]
    if with_fz:
        pfz = _dot(hc, wfz_ref[...])
        f_ref[0] = pfz[:, 0:FN_WIDTH].astype(BF16)
        z_ref[0] = pfz[:, FN_WIDTH:SSM_OFF]
    a = _conv3(_dot(hm_ref[...], wx_ref[...]), cw_ref[...], cb_ref[...], rows)
    s = _silu(a)
    xs_ref[0] = s[:, 0:SSD_WIDTH]
    b_ref[0] = s[:, SSD_WIDTH:SSD_WIDTH + SSD_GN].astype(BF16)
    c_ref[0] = s[:, SSD_WIDTH + SSD_GN:SSD_CONV_DIM].astype(BF16)
    dt_ref[0] = _softplus(_dot(hc, wdt_ref[...]) + dtb_ref[...])


def _cd_in(x, sh, sc, wfz, wx, wdt, cw, cb, dtb, rows):
    bsz, seq, d = x.shape
    with_fz = wfz is not None
    in_specs = _halo_specs(rows, seq, d) + [_row_spec(d), _row_spec(d)]
    args = [x, x, x, sh, sc]
    out_shape, out_specs = [], []
    if with_fz:
        in_specs.append(_resident((d, SSM_OFF)))
        args.append(wfz)
        out_shape += [jax.ShapeDtypeStruct((bsz, seq, FN_WIDTH), BF16),
                      jax.ShapeDtypeStruct((bsz, seq, SSD_WIDTH), F32)]
        out_specs += [pl.BlockSpec((1, rows, FN_WIDTH), lambda b, t: (b, t, 0)),
                      pl.BlockSpec((1, rows, SSD_WIDTH), lambda b, t: (b, t, 0))]
    in_specs += [_resident((d, SSD_CONV_DIM)), _resident((d, V7X_LANES)),
                 _resident((CONV_W, SSD_CONV_DIM)), _resident((1, SSD_CONV_DIM)),
                 _resident((1, V7X_LANES))]
    args += [wx, wdt, cw, cb, dtb]
    out_shape += [jax.ShapeDtypeStruct((bsz, seq, SSD_WIDTH), F32),
                  jax.ShapeDtypeStruct((bsz, seq, SSD_GN), BF16),
                  jax.ShapeDtypeStruct((bsz, seq, SSD_GN), BF16),
                  jax.ShapeDtypeStruct((bsz, seq, V7X_LANES), F32)]
    out_specs += [pl.BlockSpec((1, rows, SSD_WIDTH), lambda b, t: (b, t, 0)),
                  pl.BlockSpec((1, rows, SSD_GN), lambda b, t: (b, t, 0)),
                  pl.BlockSpec((1, rows, SSD_GN), lambda b, t: (b, t, 0)),
                  pl.BlockSpec((1, rows, V7X_LANES), lambda b, t: (b, t, 0))]
    est = d * 2200 * 2 + 2 * rows * d * 4 + 8 * (rows + 2 * HALO) * SSD_CONV_DIM * 4
    return pl.pallas_call(
        functools.partial(_cd_in_kernel, with_fz=with_fz, rows=rows),
        out_shape=out_shape,
        grid=(bsz, seq // rows),
        in_specs=in_specs,
        out_specs=out_specs,
        scratch_shapes=[pltpu.VMEM((rows + 2 * HALO, d), BF16)],
        compiler_params=pltpu.CompilerParams(
            dimension_semantics=("arbitrary", "arbitrary"),
            vmem_limit_bytes=_vmem_limit(est)),
        name="cd_in" if with_fz else "cd_in_ctx",
    )(*args)


def _cumsum_rows(z, reverse):
    n = z.shape[0]
    idx = lax.broadcasted_iota(jnp.int32, z.shape, 0)
    k = 1
    while k < n:
        if reverse:
            z = z + jnp.where(idx < n - k, pltpu.roll(z, n - k, 0), 0.0)
        else:
            z = z + jnp.where(idx >= k, pltpu.roll(z, k, 0), 0.0)
        k *= 2
    return z


def _lane_expand(row, lane0, width):
    lane = lax.broadcasted_iota(jnp.int32, (1, SSD_HPG * width), 1)
    out = jnp.zeros((1, SSD_HPG * width), F32)
    for j in range(SSD_HPG):
        val = jnp.broadcast_to(row[:, lane0 + j:lane0 + j + 1], (1, SSD_HPG * width))
        out = jnp.where(lane // width == j, val, out)
    return out


def _ssd_chunk(x, bm, cm, dt, a_row, h_ref, lane0, reverse, need_y):
    q = x.shape[0]
    gw = SSD_HPG * SSD_HEADDIM
    acum = _cumsum_rows(dt * a_row, reverse)
    total = acum[0:1, :] if reverse else acum[q - 1:q, :]
    acum_t = acum.T
    dt_t = dt.T
    total_t = jnp.broadcast_to(acum_t[:, 0:1] if reverse else acum_t[:, q - 1:q], (q, q))
    wgt_t = dt_t * jnp.exp(total_t - acum_t)
    lane_g = lax.broadcasted_iota(jnp.int32, (q, gw), 1) // SSD_HEADDIM
    if need_y:
        srow_t = acum_t - jnp.log(dt_t)
        ti = lax.broadcasted_iota(jnp.int32, (q, q), 0)
        si = lax.broadcasted_iota(jnp.int32, (q, q), 1)
        mask = (si >= ti) if reverse else (si <= ti)
    ys = []
    for g in range(SSD_GROUPS):
        bg = bm[:, g * SSD_STATE:(g + 1) * SSD_STATE]
        xgb = x[:, g * gw:(g + 1) * gw].astype(BF16)
        x_heads = [jnp.where(lane_g == j, xgb, jnp.zeros_like(xgb)) for j in range(SSD_HPG)]
        hg = h_ref[g]
        if need_y:
            cg = cm[:, g * SSD_STATE:(g + 1) * SSD_STATE]
            cb = _dot_nt(cg, bg)
            cgf = cg.astype(F32)
            hgb = hg.astype(BF16)
            lhs, rhs = [], []
            for j in range(SSD_HPG):
                jl = lane0 + g * SSD_HPG + j
                col = jnp.broadcast_to(acum[:, jl:jl + 1], (q, q))
                row = jnp.broadcast_to(srow_t[jl:jl + 1, :], (q, q))
                lhs.append((cb * jnp.exp(jnp.where(mask, col - row, -jnp.inf))).astype(BF16))
                lhs.append((cgf * jnp.exp(col)).astype(BF16))
                rhs.append(x_heads[j])
                rhs.append(jnp.where(lane_g == j, hgb, jnp.zeros_like(hgb)))
            ys.append(_dot(jnp.concatenate(lhs, axis=1), jnp.concatenate(rhs, axis=0)))
        bgt = bg.astype(F32).T
        wb = [(bgt * jnp.broadcast_to(wgt_t[lane0 + g * SSD_HPG + j:lane0 + g * SSD_HPG + j + 1, :],
                                      (SSD_STATE, q))).astype(BF16) for j in range(SSD_HPG)]
        st = _dot(jnp.concatenate(wb, axis=1), jnp.concatenate(x_heads, axis=0))
        h_ref[g] = hg * jnp.exp(_lane_expand(total, lane0 + g * SSD_HPG, SSD_HEADDIM)) + st
    if need_y:
        return jnp.concatenate(ys, axis=1)
    return None


def _ssd_kernel(xc_ref, bc_ref, dtc_ref, xl_ref, bl_ref, cl_ref, dtl_ref, z_ref, alog_ref,
                dskip_ref, ng_ref, o_ref, y_ref, h_ref):
    a_row = -jnp.exp(alog_ref[...])
    nc_ctx = xc_ref.shape[1] // CHUNK
    nc_lat = xl_ref.shape[1] // CHUNK

    def rows_of(c):
        return pl.ds(pl.multiple_of(c * CHUNK, CHUNK), CHUNK)

    def ctx_step(c, direction):
        r = rows_of(c)
        _ssd_chunk(xc_ref[0, r, :], bc_ref[0, r, :], None, dtc_ref[0, r, :], a_row,
                   h_ref.at[direction], direction * SSD_HEADS, direction == 1, False)

    def lat_step(c, direction):
        r = rows_of(c)
        y_ref[direction, r, :] = _ssd_chunk(
            xl_ref[0, r, :], bl_ref[0, r, :], cl_ref[0, r, :], dtl_ref[0, r, :], a_row,
            h_ref.at[direction], direction * SSD_HEADS, direction == 1, True)

    h_ref[...] = jnp.zeros_like(h_ref)

    def ctx_pair(i, carry):
        ctx_step(i, 0)
        ctx_step(nc_ctx - 1 - i, 1)
        return carry

    lax.fori_loop(0, nc_ctx, ctx_pair, 0)

    def lat_pair(i, carry):
        lat_step(i, 0)
        lat_step(nc_lat - 1 - i, 1)
        return carry

    lax.fori_loop(0, nc_lat, lat_pair, 0)

    def finish(c, carry):
        r = rows_of(c)
        zz = z_ref[0, r, :]
        yt = (y_ref[0, r, :] + y_ref[1, r, :] + xl_ref[0, r, :] * dskip_ref[...]) * _silu(zz)
        o_ref[0, r, :] = _rms_norm(yt, ng_ref[...]).astype(BF16)
        return carry

    lax.fori_loop(0, nc_lat, finish, 0)


def _ssd(xs_c, b_c, dt_c, xs_l, b_l, c_l, dt_l, z, alog_row, dskip_row, ng_row):
    bsz, sl, w = xs_l.shape
    sc = xs_c.shape[1]
    gw = SSD_HPG * SSD_HEADDIM

    def full(s, n):
        return pl.BlockSpec((1, s, n), lambda b: (b, 0, 0))

    est = (2 * (sl + sc) * (w * 4 + 2 * SSD_GN * 2 + V7X_LANES * 4) + 2 * sl * w * 4
           + 2 * sl * w * 2 + 2 * sl * w * 4)
    return pl.pallas_call(
        _ssd_kernel,
        out_shape=jax.ShapeDtypeStruct((bsz, sl, w), BF16),
        grid=(bsz,),
        in_specs=[
            full(sc, w), full(sc, SSD_GN), full(sc, V7X_LANES),
            full(sl, w), full(sl, SSD_GN), full(sl, SSD_GN), full(sl, V7X_LANES),
            full(sl, w),
            _resident((1, V7X_LANES)), _resident((1, w)), _resident((1, w)),
        ],
        out_specs=full(sl, w),
        scratch_shapes=[
            pltpu.VMEM((2, sl, w), F32),
            pltpu.VMEM((2, SSD_GROUPS, SSD_STATE, gw), F32),
        ],
        compiler_params=pltpu.CompilerParams(
            dimension_semantics=("arbitrary",),
            vmem_limit_bytes=_vmem_limit(est)),
        name="ssd_scan",
    )(xs_c, b_c, dt_c, xs_l, b_l, c_l, dt_l, z, alog_row, dskip_row, ng_row)


def _fourier_kernel(f_ref, cc_ref, sc_ref, w2_ref, o_ref, y_ref, *, rows):
    seq = f_ref.shape[1]
    for g in range(FN_GROUPS):
        l = slice(g * FN_GDIM, (g + 1) * FN_GDIM)
        fg = f_ref[0, :, l]
        y_ref[0:seq, l] = _dot(fg, cc_ref[...]).astype(BF16)
        y_ref[seq:2 * seq, l] = _dot(fg, sc_ref[...]).astype(BF16)
    norm = 1.0 / math.sqrt(seq * FN_GDIM)

    def tile(i, carry):
        r = pl.ds(pl.multiple_of(i * rows, rows), rows)
        o_ref[0, r, :] = (_dot(w2_ref[r, :], y_ref[...]) * norm).astype(BF16)
        return carry

    lax.fori_loop(0, seq // rows, tile, 0)


def _fourier(f, cc, sc, w2, rows):
    bsz, seq, w = f.shape
    est = seq * 2 * seq * 2 + 4 * seq * w * 2 + 2 * seq * w * 2 + 4 * rows * 2 * seq * 2
    return pl.pallas_call(
        functools.partial(_fourier_kernel, rows=rows),
        out_shape=jax.ShapeDtypeStruct((bsz, seq, w), BF16),
        grid=(bsz,),
        in_specs=[
            pl.BlockSpec((1, seq, w), lambda b: (b, 0, 0)),
            _resident((FN_GDIM, FN_GDIM)), _resident((FN_GDIM, FN_GDIM)),
            _resident((seq, 2 * seq)),
        ],
        out_specs=pl.BlockSpec((1, seq, w), lambda b: (b, 0, 0)),
        scratch_shapes=[pltpu.VMEM((2 * seq, w), BF16)],
        compiler_params=pltpu.CompilerParams(
            dimension_semantics=("arbitrary",),
            vmem_limit_bytes=_vmem_limit(est)),
        name="fourier_mix",
    )(f, cc, sc, w2)


def _rope_tables(seq):
    rows = seq // GRID_W
    row = jnp.repeat(jnp.arange(rows, dtype=F32), GRID_W)
    col = jnp.broadcast_to(jnp.arange(GRID_W, dtype=F32), (rows, GRID_W)).reshape(-1)
    n_freq = HEAD_DIM // 4
    inv = ROPE_THETA ** (-jnp.arange(n_freq, dtype=F32) / n_freq)
    ang = jnp.stack([row, col], axis=-1)[:, :, None] * inv
    cos, sin = jnp.cos(ang), jnp.sin(ang)
    zero = jnp.zeros_like(sin[:, 0])
    cos_t = jnp.concatenate([cos[:, 0], cos[:, 0], cos[:, 1], cos[:, 1]], axis=-1)
    s1 = jnp.concatenate([-sin[:, 0], zero, -sin[:, 1], zero], axis=-1)
    s2 = jnp.concatenate([zero, sin[:, 0], zero, sin[:, 1]], axis=-1)
    return cos_t, s1, s2


def _dft_tables(seq):
    def tw(n):
        i = np.arange(n, dtype=np.int64)
        ang = ((i[:, None] * i[None, :]) % n).astype(np.float64) * (2.0 * math.pi / n)
        return np.cos(ang), np.sin(ang)

    cc, sc = tw(FN_GDIM)
    cl, sl = tw(seq)
    w2 = np.concatenate([cl, -sl], axis=1).astype(np.float32)
    return (jnp.asarray(cc.astype(np.float32), BF16), jnp.asarray(sc.astype(np.float32), BF16),
            jnp.asarray(w2, BF16))


def _ffn_params(w_up, conv_w, conv_b, w_down):
    d = w_up.shape[0]
    nc = D_FF // FF_CHUNK

    def cols(m, lo):
        r = m.shape[0]
        return m[:, lo:lo + D_FF].reshape(r, nc, FF_CHUNK).transpose(1, 0, 2)

    wu = w_up.astype(BF16)
    cb = conv_b.reshape(1, 2 * D_FF)
    return (cols(wu, 0), cols(wu, D_FF), cols(conv_w, 0), cols(conv_w, D_FF),
            cols(cb, 0), cols(cb, D_FF), w_down.astype(BF16).reshape(nc, FF_CHUNK, d))


def kernel(x, c, ctx, c_ctx, mod_w, mod_b, ln1_g, ln1_b, ln2_g, ln2_b, ffn_up, ffn_conv_w, ffn_conv_b,
           ffn_down, ab_w_in, ab_w_out, gm_ln_g, gm_ln_b, gm_ws, gm_bs, q_norm_g, k_norm_g, cd_w_in,
           cd_w_out, ssd_conv_w, ssd_conv_b, ssd_dt_bias, ssd_a_log, ssd_d, ssd_norm_g):
    bsz, seq, d = x.shape
    ctx_len = ctx.shape[1]
    lat_rows = 512
    ffn_rows = 512
    ctx_rows = min(256, ctx_len)

    pad = (-(bsz + 1)) % V7X_SUBLANES
    cc = jnp.concatenate([c, c_ctx[None], jnp.zeros((pad, d), F32)], axis=0)
    mod = _modulation(cc, mod_w, mod_b)
    rope_tabs = _rope_tables(seq)

    def row(v):
        return v.reshape(1, -1)

    xl, xc = x, ctx
    for i in range(DEPTH):
        last = i == DEPTH - 1
        j = i // 2
        ml = mod[i, :bsz].reshape(bsz, 6, 1, d)
        mc = jnp.broadcast_to(mod[i, bsz].reshape(1, 6, 1, d), (bsz, 6, 1, d))
        sh1, sc1, g1, sh2, sc2, g2 = (ml[:, k] for k in range(6))
        csh1, csc1, cg1, csh2, csc2, cg2 = (mc[:, k] for k in range(6))
        ln1 = (row(ln1_g[i]), row(ln1_b[i]))
        ln2 = (row(ln2_g[i]), row(ln2_b[i]))
        ffn_p = _ffn_params(ffn_up[i], ffn_conv_w[i], ffn_conv_b[i], ffn_down[i])

        if i % 2 == 0:
            w_in = ab_w_in[j].astype(BF16)
            w_out = ab_w_out[j].astype(BF16)
            gm = (row(gm_ln_g[j]), row(gm_ln_b[j]), gm_ws[j].astype(BF16),
                  jnp.repeat(gm_bs[j].T, GM_WIDTH // GM_GROUPS, axis=1),
                  row(q_norm_g[j]), row(k_norm_g[j]))
            kv_len = ctx_len + seq
            a_c, q_c, k_all, v_all = _ab_in(xc, csh1, csc1, w_in, *gm, None, ctx_rows, kv_len, 0)
            a_l, q_l, k_all, v_all = _ab_in(xl, sh1, sc1, w_in, *gm, rope_tabs, 256, kv_len, ctx_len,
                                            kv_into=(k_all, v_all))
            mix_l = (a_l, _attention(q_l, k_all, v_all, kv_len, 512))
            if not last:
                mix_c = (a_c, _attention(q_c, k_all, v_all, ctx_len, ctx_rows))
        else:
            w_in = cd_w_in[j].astype(BF16)
            w_out = cd_w_out[j].astype(BF16)
            wfz = w_in[:, :SSM_OFF]
            wx = w_in[:, SSM_OFF:SSM_OFF + SSD_CONV_DIM]
            ndt = 2 * SSD_HEADS
            wdt = jnp.pad(w_in[:, SSM_OFF + SSD_CONV_DIM:], ((0, 0), (0, V7X_LANES - ndt)))
            dtb = jnp.pad(ssd_dt_bias[j].reshape(1, ndt), ((0, 0), (0, V7X_LANES - ndt)))
            alog = jnp.pad(ssd_a_log[j].reshape(1, ndt), ((0, 0), (0, V7X_LANES - ndt)))
            cw, cb = ssd_conv_w[j], row(ssd_conv_b[j])
            dskip = row(jnp.repeat(ssd_d[j], SSD_HEADDIM))
            f_l, z_l, xs_l, b_l, c_l, dt_l = _cd_in(xl, sh1, sc1, wfz, wx, wdt, cw, cb, dtb, lat_rows)
            if last:
                xs_c, b_c, _, dt_c = _cd_in(xc, csh1, csc1, None, wx, wdt, cw, cb, dtb, ctx_rows)
            else:
                raise NotImplementedError("odd layers with a context output are not part of this block")
            s_l = _ssd(xs_c, b_c, dt_c, xs_l, b_l, c_l, dt_l, z_l, alog, dskip, row(ssd_norm_g[j]))
            mix_l = (_fourier(f_l, *_dft_tables(seq), 512), s_l)

        xl = _out_ln(*mix_l, w_out, xl, g1, *ln1, 1024)
        xl = _ffn(xl, sh2, sc2, g2, *ffn_p, *ln2, ffn_rows)
        if not last:
            xc = _out_ln(*mix_c, w_out, xc, cg1, *ln1, ctx_rows)
            xc = _ffn(xc, csh2, csc2, cg2, *ffn_p, *ln2, ctx_rows)
    return xl
```

```python
import functools
import math

import jax
import jax.numpy as jnp
import numpy as np
from jax import lax
from jax.experimental import pallas as pl
from jax.experimental.pallas import tpu as pltpu

D_MODEL = 1024
DEPTH = 2
GRID_W = 64
CHUNK = 128
GM_GROUPS = 4
GM_WIDTH = 512
HEAD_DIM = 128
N_Q_HEADS = 4
N_KV_HEADS = 2
ATTN_WIDTH = 512
KV_W = 256
ROPE_THETA = 10000.0
FN_GROUPS = 4
FN_GDIM = 128
FN_WIDTH = 512
SSD_HEADDIM = 64
SSD_HEADS = 8
SSD_GROUPS = 2
SSD_HPG = 4
SSD_STATE = 128
SSD_WIDTH = 512
SSD_GN = 256
SSD_CONV_DIM = 1024
D_FF = 2816
CONV_W = 3
KV_OFF = 2 * GM_WIDTH + ATTN_WIDTH
SSM_OFF = FN_WIDTH + SSD_WIDTH
DN_ALPHA = (2 * DEPTH) ** 0.25
EPS = 1e-6

V7X_LANES = 128
V7X_SUBLANES = 8
V7X_VMEM_BYTES = 64 * 1024 * 1024
FF_CHUNK = 256
SSD_CHUNK = 256

BF16 = jnp.bfloat16
F32 = jnp.float32
HALO = V7X_SUBLANES


def _vmem_limit(nbytes):
    return int(min(nbytes * 3 // 2 + (8 << 20), V7X_VMEM_BYTES - (3 << 20)))


def _resident(shape):
    nd = len(shape)
    return pl.BlockSpec(shape, lambda *_: (0,) * nd, pipeline_mode=pl.Buffered(1))


def _dot(a, b):
    return jnp.dot(a, b, preferred_element_type=F32)


def _dot_nt(a, b):
    return lax.dot_general(a, b, (((1,), (1,)), ((), ())), preferred_element_type=F32)


def _layer_norm(z, g, b):
    mu = jnp.mean(z, axis=-1, keepdims=True)
    zc = z - mu
    var = jnp.mean(zc * zc, axis=-1, keepdims=True)
    return zc * lax.rsqrt(var + EPS) * g + b


def _rms_norm(z, g):
    return z * lax.rsqrt(jnp.mean(z * z, axis=-1, keepdims=True) + EPS) * g


def _softplus(z):
    return jnp.maximum(z, 0.0) + jnp.log1p(jnp.exp(-jnp.abs(z)))


def _gelu_tanh(z):
    c1 = math.sqrt(2.0 / math.pi)
    h = 0.5 * z
    return h + h * jnp.tanh(z * (c1 + (c1 * 0.044715) * (z * z)))


def _silu(z):
    h = 0.5 * z
    return h + h * jnp.tanh(h)


def _conv3(u, w, b, rows):
    n = u.shape[0]
    um = pltpu.roll(u, 1, 0)[HALO:HALO + rows]
    up = pltpu.roll(u, n - 1, 0)[HALO:HALO + rows]
    return b + w[0:1] * um + w[1:2] * u[HALO:HALO + rows] + w[2:3] * up


def _fill_halo_tile(hm_ref, x_ref, xp_ref, xn_ref, sh, sc, rows):
    t = pl.program_id(1)
    nt = pl.num_programs(1)
    keep_prev = jnp.where(t > 0, 1.0, 0.0)
    keep_next = jnp.where(t < nt - 1, 1.0, 0.0)
    hm_ref[HALO:HALO + rows, :] = (x_ref[0] * (1.0 + sc) + sh).astype(BF16)
    hm_ref[0:HALO, :] = ((xp_ref[0] * (1.0 + sc) + sh) * keep_prev).astype(BF16)
    hm_ref[HALO + rows:2 * HALO + rows, :] = ((xn_ref[0] * (1.0 + sc) + sh) * keep_next).astype(BF16)


def _halo_specs(rows, seq, d):
    nb = rows // HALO
    last = seq // HALO - 1
    return [
        pl.BlockSpec((1, rows, d), lambda b, t: (b, t, 0)),
        pl.BlockSpec((1, HALO, d), lambda b, t: (b, jnp.maximum(t * nb - 1, 0), 0)),
        pl.BlockSpec((1, HALO, d), lambda b, t: (b, jnp.minimum((t + 1) * nb, last), 0)),
    ]


def _row_spec(d):
    return pl.BlockSpec((1, 1, d), lambda b, t: (b, 0, 0))


def _mod_kernel(c_ref, w_ref, b_ref, o_ref):
    c = c_ref[...]
    act = c * jax.nn.sigmoid(c)
    o_ref[0] = jnp.dot(act, w_ref[0], preferred_element_type=F32,
                       precision=lax.Precision.HIGHEST) + b_ref[0]


def _modulation(cc, mod_w, mod_b):
    rows, d = cc.shape
    depth, _, n = mod_w.shape
    tn = 2048
    return pl.pallas_call(
        _mod_kernel,
        out_shape=jax.ShapeDtypeStruct((depth, rows, n), F32),
        grid=(depth, n // tn),
        in_specs=[
            pl.BlockSpec((rows, d), lambda i, j: (0, 0)),
            pl.BlockSpec((1, d, tn), lambda i, j: (i, 0, j)),
            pl.BlockSpec((1, 1, tn), lambda i, j: (i, 0, j)),
        ],
        out_specs=pl.BlockSpec((1, rows, tn), lambda i, j: (i, 0, j)),
        compiler_params=pltpu.CompilerParams(
            dimension_semantics=("arbitrary", "arbitrary"),
            vmem_limit_bytes=_vmem_limit(2 * d * tn * 4)),
        name="modulation",
    )(cc, mod_w, mod_b.reshape(depth, 1, n))


def _ab_in_kernel(*refs, use_rope, rows):
    (x_ref, sh_ref, sc_ref, w_ref, lng_ref, lnb_ref, ws_ref, bias_ref, qg_ref, kg_ref) = refs[:10]
    if use_rope:
        cos_ref, s1_ref, s2_ref = refs[10:13]
    a_ref, q_ref, k_ref, v_ref = refs[-4:]

    h = (x_ref[0] * (1.0 + sc_ref[0]) + sh_ref[0]).astype(BF16)

    u = _gelu_tanh(_dot(h, w_ref[:, 0:GM_WIDTH]))
    vn = _layer_norm(_gelu_tanh(_dot(h, w_ref[:, GM_WIDTH:2 * GM_WIDTH])), lng_ref[...], lnb_ref[...])
    vb = vn.astype(BF16)
    gd = GM_WIDTH // GM_GROUPS
    for c in range(rows // CHUNK):
        r = slice(c * CHUNK, (c + 1) * CHUNK)
        for g in range(GM_GROUPS):
            l = slice(g * gd, (g + 1) * gd)
            mixed = _dot(ws_ref[g], vb[r, l]) + bias_ref[:, l]
            a_ref[0, r, l] = (u[r, l] * mixed).astype(BF16)

    def rope(z):
        if not use_rope:
            return z
        return (z * cos_ref[...] + pltpu.roll(z, 3 * HEAD_DIM // 4, 1) * s1_ref[...]
                + pltpu.roll(z, HEAD_DIM // 4, 1) * s2_ref[...])

    scale = HEAD_DIM ** -0.5 * math.log2(math.e)
    pq = _dot(h, w_ref[:, 2 * GM_WIDTH:KV_OFF])
    for hh in range(N_Q_HEADS):
        l = slice(hh * HEAD_DIM, (hh + 1) * HEAD_DIM)
        q_ref[0, :, l] = (rope(_rms_norm(pq[:, l], qg_ref[...])) * scale).astype(BF16)
    pk = _dot(h, w_ref[:, KV_OFF:KV_OFF + KV_W])
    for hh in range(N_KV_HEADS):
        l = slice(hh * HEAD_DIM, (hh + 1) * HEAD_DIM)
        k_ref[0, :, l] = rope(_rms_norm(pk[:, l], kg_ref[...])).astype(BF16)
    v_ref[0] = _dot(h, w_ref[:, KV_OFF + KV_W:KV_OFF + 2 * KV_W]).astype(BF16)


def _ab_in(x, sh, sc, w, lng, lnb, ws, bias2d, qg, kg, rope_tabs, rows, kv_len, kv_row0, kv_into=None):
    bsz, seq, d = x.shape
    assert kv_row0 % rows == 0
    kv_blk0 = kv_row0 // rows
    n = w.shape[1]
    use_rope = rope_tabs is not None
    in_specs = [
        pl.BlockSpec((1, rows, d), lambda b, t: (b, t, 0)),
        _row_spec(d), _row_spec(d),
        _resident((d, n)),
        _resident((1, GM_WIDTH)), _resident((1, GM_WIDTH)),
        _resident((GM_GROUPS, CHUNK, CHUNK)),
        _resident((CHUNK, GM_WIDTH)),
        _resident((1, HEAD_DIM)), _resident((1, HEAD_DIM)),
    ]
    args = [x, sh, sc, w, lng, lnb, ws, bias2d, qg, kg]
    if use_rope:
        in_specs += [pl.BlockSpec((rows, HEAD_DIM), lambda b, t: (t, 0))] * 3
        args += list(rope_tabs)
    aliases = {}
    if kv_into is not None:
        aliases = {len(args): 2, len(args) + 1: 3}
        in_specs += [pl.BlockSpec(memory_space=pl.ANY)] * 2
        args += list(kv_into)
    out_shape = [
        jax.ShapeDtypeStruct((bsz, seq, GM_WIDTH), BF16),
        jax.ShapeDtypeStruct((bsz, seq, ATTN_WIDTH), BF16),
        jax.ShapeDtypeStruct((bsz, kv_len, KV_W), BF16),
        jax.ShapeDtypeStruct((bsz, kv_len, KV_W), BF16),
    ]
    out_specs = [
        pl.BlockSpec((1, rows, GM_WIDTH), lambda b, t: (b, t, 0)),
        pl.BlockSpec((1, rows, ATTN_WIDTH), lambda b, t: (b, t, 0)),
        pl.BlockSpec((1, rows, KV_W), lambda b, t: (b, t + kv_blk0, 0)),
        pl.BlockSpec((1, rows, KV_W), lambda b, t: (b, t + kv_blk0, 0)),
    ]
    est = d * n * 2 + 2 * rows * d * 4 + 6 * rows * n * 4
    return pl.pallas_call(
        functools.partial(_ab_in_kernel, use_rope=use_rope, rows=rows),
        out_shape=out_shape,
        grid=(bsz, seq // rows),
        in_specs=in_specs,
        out_specs=out_specs,
        input_output_aliases=aliases,
        compiler_params=pltpu.CompilerParams(
            dimension_semantics=("arbitrary", "arbitrary"),
            vmem_limit_bytes=_vmem_limit(est)),
        name="ab_in_rope" if use_rope else "ab_in_ctx",
    )(*args)


def _attn_kernel(q_ref, k_ref, v_ref, o_ref, vt_ref, *, rows, blk):
    @pl.when(pl.program_id(2) == 0)
    def _():
        vt_ref[...] = v_ref[0].astype(F32).T

    k = k_ref[0]
    vt = vt_ref[...].astype(BF16)
    for h in range(N_Q_HEADS // N_KV_HEADS):
        l_ = slice(h * HEAD_DIM, (h + 1) * HEAD_DIM)
        for r0 in range(0, rows, blk):
            qt = q_ref[0, r0:r0 + blk, l_].astype(F32).T.astype(BF16)
            st = _dot(k, qt)
            m = jnp.max(st, axis=0, keepdims=True)
            p = jnp.exp2(st - m)
            l = jnp.sum(p, axis=0, keepdims=True)
            ot = _dot(vt, p.astype(BF16)) / l
            o_ref[0, r0:r0 + blk, l_] = ot.T.astype(BF16)


def _attention(q, k, v, sk, rows, blk=512):
    bsz, sq, _ = q.shape
    gw = (N_Q_HEADS // N_KV_HEADS) * HEAD_DIM
    est = 4 * blk * sk * 12 + 4 * sk * HEAD_DIM * 2 + 4 * rows * gw * 2 + sk * HEAD_DIM * 4
    return pl.pallas_call(
        functools.partial(_attn_kernel, rows=rows, blk=min(blk, rows)),
        out_shape=jax.ShapeDtypeStruct((bsz, sq, ATTN_WIDTH), BF16),
        grid=(bsz, N_KV_HEADS, sq // rows),
        in_specs=[
            pl.BlockSpec((1, rows, gw), lambda b, h, t: (b, t, h)),
            pl.BlockSpec((1, sk, HEAD_DIM), lambda b, h, t: (b, 0, h)),
            pl.BlockSpec((1, sk, HEAD_DIM), lambda b, h, t: (b, 0, h)),
        ],
        out_specs=pl.BlockSpec((1, rows, gw), lambda b, h, t: (b, t, h)),
        scratch_shapes=[pltpu.VMEM((HEAD_DIM, sk), F32)],
        compiler_params=pltpu.CompilerParams(
            dimension_semantics=("arbitrary", "arbitrary", "arbitrary"),
            vmem_limit_bytes=_vmem_limit(est)),
        name="attention",
    )(q, k, v)


def _out_ln_kernel(a1_ref, a2_ref, w_ref, x_ref, g_ref, lng_ref, lnb_ref, o_ref):
    half = a1_ref.shape[2]
    y = _dot(a1_ref[0], w_ref[0:half, :]) + _dot(a2_ref[0], w_ref[half:2 * half, :])
    z = DN_ALPHA * x_ref[0] + g_ref[0] * y
    o_ref[0] = _layer_norm(z, lng_ref[...], lnb_ref[...])


def _out_ln(a1, a2, w, x, gate, lng, lnb, rows):
    bsz, seq, d = x.shape
    half = a1.shape[2]
    est = 2 * half * d * 2 + 4 * rows * d * 4 + 4 * rows * half * 2 + 2 * rows * d * 4
    return pl.pallas_call(
        _out_ln_kernel,
        out_shape=jax.ShapeDtypeStruct((bsz, seq, d), F32),
        grid=(bsz, seq // rows),
        in_specs=[
            pl.BlockSpec((1, rows, half), lambda b, t: (b, t, 0)),
            pl.BlockSpec((1, rows, half), lambda b, t: (b, t, 0)),
            _resident((2 * half, d)),
            pl.BlockSpec((1, rows, d), lambda b, t: (b, t, 0)),
            _row_spec(d),
            _resident((1, d)), _resident((1, d)),
        ],
        out_specs=pl.BlockSpec((1, rows, d), lambda b, t: (b, t, 0)),
        compiler_params=pltpu.CompilerParams(
            dimension_semantics=("arbitrary", "arbitrary"),
            vmem_limit_bytes=_vmem_limit(est)),
        name="out_ln",
    )(a1, a2, w, x, gate, lng, lnb)


def _ffn_kernel(x_ref, xp_ref, xn_ref, sh_ref, sc_ref, g_ref, wu1_ref, wu2_ref, cw1_ref, cw2_ref,
                cb1_ref, cb2_ref, wd_ref, lng_ref, lnb_ref, o_ref, hm_ref, u_ref, gt_ref, *, rows):
    _fill_halo_tile(hm_ref, x_ref, xp_ref, xn_ref, sh_ref[0], sc_ref[0], rows)
    n_chunks = wu1_ref.shape[0]
    tiles = rows // V7X_SUBLANES
    sub = lax.broadcasted_iota(jnp.int32, (tiles, V7X_SUBLANES, FF_CHUNK), 1)

    def up_proj(c, slot):
        hm = hm_ref[...]
        u_ref[slot, 0] = _dot(hm, wu1_ref[c]).reshape(tiles + 2, V7X_SUBLANES, FF_CHUNK)
        u_ref[slot, 1] = _dot(hm, wu2_ref[c]).reshape(tiles + 2, V7X_SUBLANES, FF_CHUNK)

    def conv_gate(c, slot):
        def conv(half, cw_ref, cb_ref):
            u = u_ref[slot, half]
            dn = pltpu.roll(u, 1, 1)
            up = pltpu.roll(u, V7X_SUBLANES - 1, 1)
            prev = jnp.where(sub == 0, dn[0:tiles], dn[1:tiles + 1])
            nxt = jnp.where(sub == V7X_SUBLANES - 1, up[2:tiles + 2], up[1:tiles + 1])
            w = cw_ref[c]
            return cb_ref[c] + w[0:1] * prev + w[1:2] * u[1:tiles + 1] + w[2:3] * nxt

        a1 = conv(0, cw1_ref, cb1_ref)
        a2 = conv(1, cw2_ref, cb2_ref)
        gt_ref[c] = (a1 * _silu(a2)).reshape(rows, FF_CHUNK)

    up_proj(0, 0)

    def pair(k, carry):
        c = 2 * k
        up_proj(c + 1, 1)
        conv_gate(c, 0)
        up_proj(c + 2, 0)
        conv_gate(c + 1, 1)
        return carry

    lax.fori_loop(0, (n_chunks - 1) // 2, pair, 0)
    conv_gate(n_chunks - 1, 0)

    f = _dot(gt_ref[0].astype(BF16), wd_ref[0])
    for c in range(1, n_chunks):
        f = f + _dot(gt_ref[c].astype(BF16), wd_ref[c])
    z = DN_ALPHA * x_ref[0] + g_ref[0] * f
    o_ref[0] = _layer_norm(z, lng_ref[...], lnb_ref[...])


def _ffn(x, sh, sc, gate, wu1, wu2, cw1, cw2, cb1, cb2, wd, lng, lnb, rows):
    bsz, seq, d = x.shape
    nc = wu1.shape[0]
    assert nc % 2 == 1, "the pipeline peels one chunk and pairs the rest"
    est = (3 * nc * d * FF_CHUNK * 2 + 4 * rows * d * 4 + (rows + 2 * HALO) * d * 2
           + nc * rows * FF_CHUNK * 4 + 8 * (rows + 2 * HALO) * FF_CHUNK * 4 + 2 * rows * d * 4)
    return pl.pallas_call(
        functools.partial(_ffn_kernel, rows=rows),
        out_shape=jax.ShapeDtypeStruct((bsz, seq, d), F32),
        grid=(bsz, seq // rows),
        in_specs=_halo_specs(rows, seq, d) + [
            _row_spec(d), _row_spec(d), _row_spec(d),
            _resident((nc, d, FF_CHUNK)), _resident((nc, d, FF_CHUNK)),
            _resident((nc, CONV_W, FF_CHUNK)), _resident((nc, CONV_W, FF_CHUNK)),
            _resident((nc, 1, FF_CHUNK)), _resident((nc, 1, FF_CHUNK)),
            _resident((nc, FF_CHUNK, d)),
            _resident((1, d)), _resident((1, d)),
        ],
        out_specs=pl.BlockSpec((1, rows, d), lambda b, t: (b, t, 0)),
        scratch_shapes=[
            pltpu.VMEM((rows + 2 * HALO, d), BF16),
            pltpu.VMEM((2, 2, rows // HALO + 2, HALO, FF_CHUNK), F32),
            pltpu.VMEM((nc, rows, FF_CHUNK), F32),
        ],
        compiler_params=pltpu.CompilerParams(
            dimension_semantics=("arbitrary", "arbitrary"),
            vmem_limit_bytes=_vmem_limit(est)),
        name="conv_ffn",
    )(x, x, x, sh, sc, gate, wu1, wu2, cw1, cw2, cb1, cb2, wd, lng, lnb)


def _cd_in_kernel(*refs, with_fz, rows):
    x_ref, xp_ref, xn_ref, sh_ref, sc_ref = refs[:5]
    if with_fz:
        wfz_ref, wx_ref, wdt_ref, cw_ref, cb_ref, dtb_ref = refs[5:11]
        f_ref, z_ref, xs_ref, b_ref, c_ref, dt_ref, hm_ref = refs[11:]
    else:
        wx_ref, wdt_ref, cw_ref, cb_ref, dtb_ref = refs[5:10]
        xs_ref, b_ref, c_ref, dt_ref, hm_ref = refs[10:]
    _fill_halo_tile(hm_ref, x_ref, xp_ref, xn_ref, sh_ref[0], sc_ref[0], rows)
    hc = hm_ref[HALO:HALO + rows, :]
    if with_fz:
        pfz = _dot(hc, wfz_ref[...])
        f_ref[0] = pfz[:, 0:FN_WIDTH].astype(BF16)
        z_ref[0] = pfz[:, FN_WIDTH:SSM_OFF]
    a = _conv3(_dot(hm_ref[...], wx_ref[...]), cw_ref[...], cb_ref[...], rows)
    s = _silu(a)
    xs_ref[0] = s[:, 0:SSD_WIDTH]
    b_ref[0] = s[:, SSD_WIDTH:SSD_WIDTH + SSD_GN].astype(BF16)
    c_ref[0] = s[:, SSD_WIDTH + SSD_GN:SSD_CONV_DIM].astype(BF16)
    dt_ref[0] = _softplus(_dot(hc, wdt_ref[...]) + dtb_ref[...])


def _cd_in(x, sh, sc, wfz, wx, wdt, cw, cb, dtb, rows):
    bsz, seq, d = x.shape
    with_fz = wfz is not None
    in_specs = _halo_specs(rows, seq, d) + [_row_spec(d), _row_spec(d)]
    args = [x, x, x, sh, sc]
    out_shape, out_specs = [], []
    if with_fz:
        in_specs.append(_resident((d, SSM_OFF)))
        args.append(wfz)
        out_shape += [jax.ShapeDtypeStruct((bsz, seq, FN_WIDTH), BF16),
                      jax.ShapeDtypeStruct((bsz, seq, SSD_WIDTH), F32)]
        out_specs += [pl.BlockSpec((1, rows, FN_WIDTH), lambda b, t: (b, t, 0)),
                      pl.BlockSpec((1, rows, SSD_WIDTH), lambda b, t: (b, t, 0))]
    in_specs += [_resident((d, SSD_CONV_DIM)), _resident((d, V7X_LANES)),
                 _resident((CONV_W, SSD_CONV_DIM)), _resident((1, SSD_CONV_DIM)),
                 _resident((1, V7X_LANES))]
    args += [wx, wdt, cw, cb, dtb]
    out_shape += [jax.ShapeDtypeStruct((bsz, seq, SSD_WIDTH), F32),
                  jax.ShapeDtypeStruct((bsz, seq, SSD_GN), BF16),
                  jax.ShapeDtypeStruct((bsz, seq, SSD_GN), BF16),
                  jax.ShapeDtypeStruct((bsz, seq, V7X_LANES), F32)]
    out_specs += [pl.BlockSpec((1, rows, SSD_WIDTH), lambda b, t: (b, t, 0)),
                  pl.BlockSpec((1, rows, SSD_GN), lambda b, t: (b, t, 0)),
                  pl.BlockSpec((1, rows, SSD_GN), lambda b, t: (b, t, 0)),
                  pl.BlockSpec((1, rows, V7X_LANES), lambda b, t: (b, t, 0))]
    est = d * 2200 * 2 + 2 * rows * d * 4 + 8 * (rows + 2 * HALO) * SSD_CONV_DIM * 4
    return pl.pallas_call(
        functools.partial(_cd_in_kernel, with_fz=with_fz, rows=rows),
        out_shape=out_shape,
        grid=(bsz, seq // rows),
        in_specs=in_specs,
        out_specs=out_specs,
        scratch_shapes=[pltpu.VMEM((rows + 2 * HALO, d), BF16)],
        compiler_params=pltpu.CompilerParams(
            dimension_semantics=("arbitrary", "arbitrary"),
            vmem_limit_bytes=_vmem_limit(est)),
        name="cd_in" if with_fz else "cd_in_ctx",
    )(*args)


def _cumsum_rows(z, reverse):
    n = z.shape[0]
    idx = lax.broadcasted_iota(jnp.int32, z.shape, 0)
    k = 1
    while k < n:
        if reverse:
            z = z + jnp.where(idx < n - k, pltpu.roll(z, n - k, 0), 0.0)
        else:
            z = z + jnp.where(idx >= k, pltpu.roll(z, k, 0), 0.0)
        k *= 2
    return z


def _lane_expand(row, lane0, width):
    lane = lax.broadcasted_iota(jnp.int32, (1, SSD_HPG * width), 1)
    out = jnp.zeros((1, SSD_HPG * width), F32)
    for j in range(SSD_HPG):
        val = jnp.broadcast_to(row[:, lane0 + j:lane0 + j + 1], (1, SSD_HPG * width))
        out = jnp.where(lane // width == j, val, out)
    return out


def _ssd_chunk(x, bm, cm, dt, a_row, h_ref, lane0, reverse, need_y):
    q = x.shape[0]
    gw = SSD_HPG * SSD_HEADDIM
    acum = _cumsum_rows(dt * a_row, reverse)
    total = acum[0:1, :] if reverse else acum[q - 1:q, :]
    acum_t = acum.T
    dt_t = dt.T
    total_t = jnp.broadcast_to(acum_t[:, 0:1] if reverse else acum_t[:, q - 1:q], acum_t.shape)
    wgt_t = dt_t * jnp.exp(total_t - acum_t)
    lane_g = lax.broadcasted_iota(jnp.int32, (q, gw), 1) // SSD_HEADDIM
    lane_gh = lax.broadcasted_iota(jnp.int32, (SSD_STATE, gw), 1) // SSD_HEADDIM
    if need_y:
        srow_t = acum_t - jnp.log(dt_t)
        ti = lax.broadcasted_iota(jnp.int32, (q, q), 0)
        si = lax.broadcasted_iota(jnp.int32, (q, q), 1)
        mask = (si >= ti) if reverse else (si <= ti)
    ys = []
    for g in range(SSD_GROUPS):
        bg = bm[:, g * SSD_STATE:(g + 1) * SSD_STATE]
        xgb = x[:, g * gw:(g + 1) * gw].astype(BF16)
        x_heads = [jnp.where(lane_g == j, xgb, jnp.zeros_like(xgb)) for j in range(SSD_HPG)]
        hg = h_ref[g]
        if need_y:
            cg = cm[:, g * SSD_STATE:(g + 1) * SSD_STATE]
            cb = _dot_nt(cg, bg)
            cgf = cg.astype(F32)
            hgb = hg.astype(BF16)
            lhs, rhs = [], []
            for j in range(SSD_HPG):
                jl = lane0 + g * SSD_HPG + j
                col = jnp.broadcast_to(acum[:, jl:jl + 1], (q, q))
                row = jnp.broadcast_to(srow_t[jl:jl + 1, :], (q, q))
                lhs.append((cb * jnp.exp(jnp.where(mask, col - row, -jnp.inf))).astype(BF16))
                lhs.append((cgf * jnp.exp(col[:, 0:SSD_STATE])).astype(BF16))
                rhs.append(x_heads[j])
                rhs.append(jnp.where(lane_gh == j, hgb, jnp.zeros_like(hgb)))
            ys.append(_dot(jnp.concatenate(lhs, axis=1), jnp.concatenate(rhs, axis=0)))
        bgt = bg.astype(F32).T
        wb = [(bgt * jnp.broadcast_to(wgt_t[lane0 + g * SSD_HPG + j:lane0 + g * SSD_HPG + j + 1, :],
                                      (SSD_STATE, q))).astype(BF16) for j in range(SSD_HPG)]
        st = _dot(jnp.concatenate(wb, axis=1), jnp.concatenate(x_heads, axis=0))
        h_ref[g] = hg * jnp.exp(_lane_expand(total, lane0 + g * SSD_HPG, SSD_HEADDIM)) + st
    if need_y:
        return jnp.concatenate(ys, axis=1)
    return None


def _ssd_kernel(xc_ref, bc_ref, dtc_ref, xl_ref, bl_ref, cl_ref, dtl_ref, z_ref, alog_ref,
                dskip_ref, ng_ref, o_ref, y_ref, h_ref):
    a_row = -jnp.exp(alog_ref[...])
    nc_ctx = xc_ref.shape[1] // SSD_CHUNK
    nc_lat = xl_ref.shape[1] // SSD_CHUNK

    def rows_of(c):
        return pl.ds(pl.multiple_of(c * SSD_CHUNK, SSD_CHUNK), SSD_CHUNK)

    def ctx_step(c, direction):
        r = rows_of(c)
        _ssd_chunk(xc_ref[0, r, :], bc_ref[0, r, :], None, dtc_ref[0, r, :], a_row,
                   h_ref.at[direction], direction * SSD_HEADS, direction == 1, False)

    def lat_step(c, direction):
        r = rows_of(c)
        y_ref[direction, r, :] = _ssd_chunk(
            xl_ref[0, r, :], bl_ref[0, r, :], cl_ref[0, r, :], dtl_ref[0, r, :], a_row,
            h_ref.at[direction], direction * SSD_HEADS, direction == 1, True)

    h_ref[...] = jnp.zeros_like(h_ref)

    def ctx_pair(i, carry):
        ctx_step(i, 0)
        ctx_step(nc_ctx - 1 - i, 1)
        return carry

    lax.fori_loop(0, nc_ctx, ctx_pair, 0)

    def lat_pair(i, carry):
        lat_step(i, 0)
        lat_step(nc_lat - 1 - i, 1)
        return carry

    lax.fori_loop(0, nc_lat, lat_pair, 0)

    def finish(c, carry):
        r = rows_of(c)
        zz = z_ref[0, r, :]
        yt = (y_ref[0, r, :] + y_ref[1, r, :] + xl_ref[0, r, :] * dskip_ref[...]) * _silu(zz)
        o_ref[0, r, :] = _rms_norm(yt, ng_ref[...]).astype(BF16)
        return carry

    lax.fori_loop(0, nc_lat, finish, 0)


def _ssd(xs_c, b_c, dt_c, xs_l, b_l, c_l, dt_l, z, alog_row, dskip_row, ng_row):
    bsz, sl, w = xs_l.shape
    sc = xs_c.shape[1]
    gw = SSD_HPG * SSD_HEADDIM

    def full(s, n):
        return pl.BlockSpec((1, s, n), lambda b: (b, 0, 0))

    est = (2 * (sl + sc) * (w * 4 + 2 * SSD_GN * 2 + V7X_LANES * 4) + 2 * sl * w * 4
           + 2 * sl * w * 2 + 2 * sl * w * 4)
    return pl.pallas_call(
        _ssd_kernel,
        out_shape=jax.ShapeDtypeStruct((bsz, sl, w), BF16),
        grid=(bsz,),
        in_specs=[
            full(sc, w), full(sc, SSD_GN), full(sc, V7X_LANES),
            full(sl, w), full(sl, SSD_GN), full(sl, SSD_GN), full(sl, V7X_LANES),
            full(sl, w),
            _resident((1, V7X_LANES)), _resident((1, w)), _resident((1, w)),
        ],
        out_specs=full(sl, w),
        scratch_shapes=[
            pltpu.VMEM((2, sl, w), F32),
            pltpu.VMEM((2, SSD_GROUPS, SSD_STATE, gw), F32),
        ],
        compiler_params=pltpu.CompilerParams(
            dimension_semantics=("arbitrary",),
            vmem_limit_bytes=_vmem_limit(est)),
        name="ssd_scan",
    )(xs_c, b_c, dt_c, xs_l, b_l, c_l, dt_l, z, alog_row, dskip_row, ng_row)


def _fourier_kernel(f_ref, cc_ref, sc_ref, w2_ref, o_ref, y_ref, *, rows):
    seq = f_ref.shape[1]
    for g in range(FN_GROUPS):
        l = slice(g * FN_GDIM, (g + 1) * FN_GDIM)
        fg = f_ref[0, :, l]
        y_ref[0:seq, l] = _dot(fg, cc_ref[...]).astype(BF16)
        y_ref[seq:2 * seq, l] = _dot(fg, sc_ref[...]).astype(BF16)
    norm = 1.0 / math.sqrt(seq * FN_GDIM)

    def tile(i, carry):
        r = pl.ds(pl.multiple_of(i * rows, rows), rows)
        o_ref[0, r, :] = (_dot(w2_ref[r, :], y_ref[...]) * norm).astype(BF16)
        return carry

    lax.fori_loop(0, seq // rows, tile, 0)


def _fourier(f, cc, sc, w2, rows):
    bsz, seq, w = f.shape
    est = seq * 2 * seq * 2 + 4 * seq * w * 2 + 2 * seq * w * 2 + 4 * rows * 2 * seq * 2
    return pl.pallas_call(
        functools.partial(_fourier_kernel, rows=rows),
        out_shape=jax.ShapeDtypeStruct((bsz, seq, w), BF16),
        grid=(bsz,),
        in_specs=[
            pl.BlockSpec((1, seq, w), lambda b: (b, 0, 0)),
            _resident((FN_GDIM, FN_GDIM)), _resident((FN_GDIM, FN_GDIM)),
            _resident((seq, 2 * seq)),
        ],
        out_specs=pl.BlockSpec((1, seq, w), lambda b: (b, 0, 0)),
        scratch_shapes=[pltpu.VMEM((2 * seq, w), BF16)],
        compiler_params=pltpu.CompilerParams(
            dimension_semantics=("arbitrary",),
            vmem_limit_bytes=_vmem_limit(est)),
        name="fourier_mix",
    )(f, cc, sc, w2)


def _rope_tables(seq):
    rows = seq // GRID_W
    row = jnp.repeat(jnp.arange(rows, dtype=F32), GRID_W)
    col = jnp.broadcast_to(jnp.arange(GRID_W, dtype=F32), (rows, GRID_W)).reshape(-1)
    n_freq = HEAD_DIM // 4
    inv = ROPE_THETA ** (-jnp.arange(n_freq, dtype=F32) / n_freq)
    ang = jnp.stack([row, col], axis=-1)[:, :, None] * inv
    cos, sin = jnp.cos(ang), jnp.sin(ang)
    zero = jnp.zeros_like(sin[:, 0])
    cos_t = jnp.concatenate([cos[:, 0], cos[:, 0], cos[:, 1], cos[:, 1]], axis=-1)
    s1 = jnp.concatenate([-sin[:, 0], zero, -sin[:, 1], zero], axis=-1)
    s2 = jnp.concatenate([zero, sin[:, 0], zero, sin[:, 1]], axis=-1)
    return cos_t, s1, s2


def _dft_tables(seq):
    def tw(n):
        i = np.arange(n, dtype=np.int64)
        ang = ((i[:, None] * i[None, :]) % n).astype(np.float64) * (2.0 * math.pi / n)
        return np.cos(ang), np.sin(ang)

    cc, sc = tw(FN_GDIM)
    cl, sl = tw(seq)
    w2 = np.concatenate([cl, -sl], axis=1).astype(np.float32)
    return (jnp.asarray(cc.astype(np.float32), BF16), jnp.asarray(sc.astype(np.float32), BF16),
            jnp.asarray(w2, BF16))


def _ffn_params(w_up, conv_w, conv_b, w_down):
    d = w_up.shape[0]
    nc = D_FF // FF_CHUNK

    def cols(m, lo):
        r = m.shape[0]
        return m[:, lo:lo + D_FF].reshape(r, nc, FF_CHUNK).transpose(1, 0, 2)

    wu = w_up.astype(BF16)
    cb = conv_b.reshape(1, 2 * D_FF)
    return (cols(wu, 0), cols(wu, D_FF), cols(conv_w, 0), cols(conv_w, D_FF),
            cols(cb, 0), cols(cb, D_FF), w_down.astype(BF16).reshape(nc, FF_CHUNK, d))


def kernel(x, c, ctx, c_ctx, mod_w, mod_b, ln1_g, ln1_b, ln2_g, ln2_b, ffn_up, ffn_conv_w, ffn_conv_b,
           ffn_down, ab_w_in, ab_w_out, gm_ln_g, gm_ln_b, gm_ws, gm_bs, q_norm_g, k_norm_g, cd_w_in,
           cd_w_out, ssd_conv_w, ssd_conv_b, ssd_dt_bias, ssd_a_log, ssd_d, ssd_norm_g):
    bsz, seq, d = x.shape
    ctx_len = ctx.shape[1]
    lat_rows = 1024
    ffn_rows = 1024
    ctx_rows = min(256, ctx_len)

    pad = (-(bsz + 1)) % V7X_SUBLANES
    cc = jnp.concatenate([c, c_ctx[None], jnp.zeros((pad, d), F32)], axis=0)
    mod = _modulation(cc, mod_w, mod_b)
    rope_tabs = _rope_tables(seq)

    def row(v):
        return v.reshape(1, -1)

    xl, xc = x, ctx
    for i in range(DEPTH):
        last = i == DEPTH - 1
        j = i // 2
        ml = mod[i, :bsz].reshape(bsz, 6, 1, d)
        mc = jnp.broadcast_to(mod[i, bsz].reshape(1, 6, 1, d), (bsz, 6, 1, d))
        sh1, sc1, g1, sh2, sc2, g2 = (ml[:, k] for k in range(6))
        csh1, csc1, cg1, csh2, csc2, cg2 = (mc[:, k] for k in range(6))
        ln1 = (row(ln1_g[i]), row(ln1_b[i]))
        ln2 = (row(ln2_g[i]), row(ln2_b[i]))
        ffn_p = _ffn_params(ffn_up[i], ffn_conv_w[i], ffn_conv_b[i], ffn_down[i])

        if i % 2 == 0:
            w_in = ab_w_in[j].astype(BF16)
            w_out = ab_w_out[j].astype(BF16)
            gm = (row(gm_ln_g[j]), row(gm_ln_b[j]), gm_ws[j].astype(BF16),
                  jnp.repeat(gm_bs[j].T, GM_WIDTH // GM_GROUPS, axis=1),
                  row(q_norm_g[j]), row(k_norm_g[j]))
            kv_len = ctx_len + seq
            a_c, q_c, k_all, v_all = _ab_in(xc, csh1, csc1, w_in, *gm, None, ctx_rows, kv_len, 0)
            a_l, q_l, k_all, v_all = _ab_in(xl, sh1, sc1, w_in, *gm, rope_tabs, 256, kv_len, ctx_len,
                                            kv_into=(k_all, v_all))
            mix_l = (a_l, _attention(q_l, k_all, v_all, kv_len, 512))
            if not last:
                mix_c = (a_c, _attention(q_c, k_all, v_all, ctx_len, ctx_rows))
        else:
            w_in = cd_w_in[j].astype(BF16)
            w_out = cd_w_out[j].astype(BF16)
            wfz = w_in[:, :SSM_OFF]
            wx = w_in[:, SSM_OFF:SSM_OFF + SSD_CONV_DIM]
            ndt = 2 * SSD_HEADS
            wdt = jnp.pad(w_in[:, SSM_OFF + SSD_CONV_DIM:], ((0, 0), (0, V7X_LANES - ndt)))
            dtb = jnp.pad(ssd_dt_bias[j].reshape(1, ndt), ((0, 0), (0, V7X_LANES - ndt)))
            alog = jnp.pad(ssd_a_log[j].reshape(1, ndt), ((0, 0), (0, V7X_LANES - ndt)))
            cw, cb = ssd_conv_w[j], row(ssd_conv_b[j])
            dskip = row(jnp.repeat(ssd_d[j], SSD_HEADDIM))
            f_l, z_l, xs_l, b_l, c_l, dt_l = _cd_in(xl, sh1, sc1, wfz, wx, wdt, cw, cb, dtb, lat_rows)
            if last:
                xs_c, b_c, _, dt_c = _cd_in(xc, csh1, csc1, None, wx, wdt, cw, cb, dtb, ctx_rows)
            else:
                raise NotImplementedError("odd layers with a context output are not part of this block")
            s_l = _ssd(xs_c, b_c, dt_c, xs_l, b_l, c_l, dt_l, z_l, alog, dskip, row(ssd_norm_g[j]))
            mix_l = (_fourier(f_l, *_dft_tables(seq), 512), s_l)

        xl = _out_ln(*mix_l, w_out, xl, g1, *ln1, 1024)
        xl = _ffn(xl, sh2, sc2, g2, *ffn_p, *ln2, ffn_rows)
        if not last:
            xc = _out_ln(*mix_c, w_out, xc, cg1, *ln1, ctx_rows)
            xc = _ffn(xc, csh2, csc2, cg2, *ffn_p, *ln2, ctx_rows)
    return xl
```

```python
import functools
import math

import jax
import jax.numpy as jnp
import numpy as np
from jax import lax
from jax.experimental import pallas as pl
from jax.experimental.pallas import tpu as pltpu

D_MODEL = 1024
DEPTH = 2
GRID_W = 64
CHUNK = 128
GM_GROUPS = 4
GM_WIDTH = 512
HEAD_DIM = 128
N_Q_HEADS = 4
N_KV_HEADS = 2
ATTN_WIDTH = 512
KV_W = 256
ROPE_THETA = 10000.0
FN_GROUPS = 4
FN_GDIM = 128
FN_WIDTH = 512
SSD_HEADDIM = 64
SSD_HEADS = 8
SSD_GROUPS = 2
SSD_HPG = 4
SSD_STATE = 128
SSD_WIDTH = 512
SSD_GN = 256
SSD_CONV_DIM = 1024
D_FF = 2816
CONV_W = 3
KV_OFF = 2 * GM_WIDTH + ATTN_WIDTH
SSM_OFF = FN_WIDTH + SSD_WIDTH
DN_ALPHA = (2 * DEPTH) ** 0.25
EPS = 1e-6

V7X_LANES = 128
V7X_SUBLANES = 8
V7X_VMEM_BYTES = 64 * 1024 * 1024
FF_CHUNK = 256
SSD_CHUNK = 256

BF16 = jnp.bfloat16
F32 = jnp.float32
HALO = V7X_SUBLANES


def _vmem_limit(nbytes):
    return int(min(nbytes * 3 // 2 + (8 << 20), V7X_VMEM_BYTES - (3 << 20)))


def _resident(shape):
    nd = len(shape)
    return pl.BlockSpec(shape, lambda *_: (0,) * nd, pipeline_mode=pl.Buffered(1))


def _dot(a, b):
    return jnp.dot(a, b, preferred_element_type=F32)


def _dot_nt(a, b):
    return lax.dot_general(a, b, (((1,), (1,)), ((), ())), preferred_element_type=F32)


def _layer_norm(z, g, b):
    mu = jnp.mean(z, axis=-1, keepdims=True)
    zc = z - mu
    var = jnp.mean(zc * zc, axis=-1, keepdims=True)
    return zc * lax.rsqrt(var + EPS) * g + b


def _rms_norm(z, g):
    return z * lax.rsqrt(jnp.mean(z * z, axis=-1, keepdims=True) + EPS) * g


def _softplus(z):
    return jnp.maximum(z, 0.0) + jnp.log1p(jnp.exp(-jnp.abs(z)))


def _gelu_tanh(z):
    c1 = math.sqrt(2.0 / math.pi)
    h = 0.5 * z
    return h + h * jnp.tanh(z * (c1 + (c1 * 0.044715) * (z * z)))


def _silu(z):
    h = 0.5 * z
    return h + h * jnp.tanh(h)


def _conv3(u, w, b, rows):
    n = u.shape[0]
    um = pltpu.roll(u, 1, 0)[HALO:HALO + rows]
    up = pltpu.roll(u, n - 1, 0)[HALO:HALO + rows]
    return b + w[0:1] * um + w[1:2] * u[HALO:HALO + rows] + w[2:3] * up


def _fill_halo_tile(hm_ref, x_ref, xp_ref, xn_ref, sh, sc, rows):
    t = pl.program_id(1)
    nt = pl.num_programs(1)
    keep_prev = jnp.where(t > 0, 1.0, 0.0)
    keep_next = jnp.where(t < nt - 1, 1.0, 0.0)
    hm_ref[HALO:HALO + rows, :] = (x_ref[0] * (1.0 + sc) + sh).astype(BF16)
    hm_ref[0:HALO, :] = ((xp_ref[0] * (1.0 + sc) + sh) * keep_prev).astype(BF16)
    hm_ref[HALO + rows:2 * HALO + rows, :] = ((xn_ref[0] * (1.0 + sc) + sh) * keep_next).astype(BF16)


def _halo_specs(rows, seq, d):
    nb = rows // HALO
    last = seq // HALO - 1
    return [
        pl.BlockSpec((1, rows, d), lambda b, t: (b, t, 0)),
        pl.BlockSpec((1, HALO, d), lambda b, t: (b, jnp.maximum(t * nb - 1, 0), 0)),
        pl.BlockSpec((1, HALO, d), lambda b, t: (b, jnp.minimum((t + 1) * nb, last), 0)),
    ]


def _row_spec(d):
    return pl.BlockSpec((1, 1, d), lambda b, t: (b, 0, 0))


def _mod_kernel(c_ref, w_ref, b_ref, o_ref):
    c = c_ref[...]
    act = c * jax.nn.sigmoid(c)
    o_ref[0] = jnp.dot(act, w_ref[0], preferred_element_type=F32,
                       precision=lax.Precision.HIGHEST) + b_ref[0]


def _modulation(cc, mod_w, mod_b):
    rows, d = cc.shape
    depth, _, n = mod_w.shape
    tn = 2048
    return pl.pallas_call(
        _mod_kernel,
        out_shape=jax.ShapeDtypeStruct((depth, rows, n), F32),
        grid=(depth, n // tn),
        in_specs=[
            pl.BlockSpec((rows, d), lambda i, j: (0, 0)),
            pl.BlockSpec((1, d, tn), lambda i, j: (i, 0, j)),
            pl.BlockSpec((1, 1, tn), lambda i, j: (i, 0, j)),
        ],
        out_specs=pl.BlockSpec((1, rows, tn), lambda i, j: (i, 0, j)),
        compiler_params=pltpu.CompilerParams(
            dimension_semantics=("arbitrary", "arbitrary"),
            vmem_limit_bytes=_vmem_limit(2 * d * tn * 4)),
        name="modulation",
    )(cc, mod_w, mod_b.reshape(depth, 1, n))


def _ab_in_kernel(*refs, use_rope, rows):
    (x_ref, sh_ref, sc_ref, w_ref, lng_ref, lnb_ref, ws_ref, bias_ref, qg_ref, kg_ref) = refs[:10]
    if use_rope:
        cos_ref, s1_ref, s2_ref = refs[10:13]
    a_ref, q_ref, k_ref, v_ref = refs[-4:]

    h = (x_ref[0] * (1.0 + sc_ref[0]) + sh_ref[0]).astype(BF16)

    u = _gelu_tanh(_dot(h, w_ref[:, 0:GM_WIDTH]))
    vn = _layer_norm(_gelu_tanh(_dot(h, w_ref[:, GM_WIDTH:2 * GM_WIDTH])), lng_ref[...], lnb_ref[...])
    vb = vn.astype(BF16)
    gd = GM_WIDTH // GM_GROUPS
    for c in range(rows // CHUNK):
        r = slice(c * CHUNK, (c + 1) * CHUNK)
        for g in range(GM_GROUPS):
            l = slice(g * gd, (g + 1) * gd)
            mixed = _dot(ws_ref[g], vb[r, l]) + bias_ref[:, l]
            a_ref[0, r, l] = (u[r, l] * mixed).astype(BF16)

    def rope(z):
        if not use_rope:
            return z
        return (z * cos_ref[...] + pltpu.roll(z, 3 * HEAD_DIM // 4, 1) * s1_ref[...]
                + pltpu.roll(z, HEAD_DIM // 4, 1) * s2_ref[...])

    scale = HEAD_DIM ** -0.5 * math.log2(math.e)
    pq = _dot(h, w_ref[:, 2 * GM_WIDTH:KV_OFF])
    for hh in range(N_Q_HEADS):
        l = slice(hh * HEAD_DIM, (hh + 1) * HEAD_DIM)
        q_ref[0, :, l] = (rope(_rms_norm(pq[:, l], qg_ref[...])) * scale).astype(BF16)
    pk = _dot(h, w_ref[:, KV_OFF:KV_OFF + KV_W])
    for hh in range(N_KV_HEADS):
        l = slice(hh * HEAD_DIM, (hh + 1) * HEAD_DIM)
        k_ref[0, :, l] = rope(_rms_norm(pk[:, l], kg_ref[...])).astype(BF16)
    v_ref[0] = _dot(h, w_ref[:, KV_OFF + KV_W:KV_OFF + 2 * KV_W]).astype(BF16)


def _ab_in(x, sh, sc, w, lng, lnb, ws, bias2d, qg, kg, rope_tabs, rows, kv_len, kv_row0, kv_into=None):
    bsz, seq, d = x.shape
    assert kv_row0 % rows == 0
    kv_blk0 = kv_row0 // rows
    n = w.shape[1]
    use_rope = rope_tabs is not None
    in_specs = [
        pl.BlockSpec((1, rows, d), lambda b, t: (b, t, 0)),
        _row_spec(d), _row_spec(d),
        _resident((d, n)),
        _resident((1, GM_WIDTH)), _resident((1, GM_WIDTH)),
        _resident((GM_GROUPS, CHUNK, CHUNK)),
        _resident((CHUNK, GM_WIDTH)),
        _resident((1, HEAD_DIM)), _resident((1, HEAD_DIM)),
    ]
    args = [x, sh, sc, w, lng, lnb, ws, bias2d, qg, kg]
    if use_rope:
        in_specs += [pl.BlockSpec((rows, HEAD_DIM), lambda b, t: (t, 0))] * 3
        args += list(rope_tabs)
    aliases = {}
    if kv_into is not None:
        aliases = {len(args): 2, len(args) + 1: 3}
        in_specs += [pl.BlockSpec(memory_space=pl.ANY)] * 2
        args += list(kv_into)
    out_shape = [
        jax.ShapeDtypeStruct((bsz, seq, GM_WIDTH), BF16),
        jax.ShapeDtypeStruct((bsz, seq, ATTN_WIDTH), BF16),
        jax.ShapeDtypeStruct((bsz, kv_len, KV_W), BF16),
        jax.ShapeDtypeStruct((bsz, kv_len, KV_W), BF16),
    ]
    out_specs = [
        pl.BlockSpec((1, rows, GM_WIDTH), lambda b, t: (b, t, 0)),
        pl.BlockSpec((1, rows, ATTN_WIDTH), lambda b, t: (b, t, 0)),
        pl.BlockSpec((1, rows, KV_W), lambda b, t: (b, t + kv_blk0, 0)),
        pl.BlockSpec((1, rows, KV_W), lambda b, t: (b, t + kv_blk0, 0)),
    ]
    est = d * n * 2 + 2 * rows * d * 4 + 6 * rows * n * 4
    return pl.pallas_call(
        functools.partial(_ab_in_kernel, use_rope=use_rope, rows=rows),
        out_shape=out_shape,
        grid=(bsz, seq // rows),
        in_specs=in_specs,
        out_specs=out_specs,
        input_output_aliases=aliases,
        compiler_params=pltpu.CompilerParams(
            dimension_semantics=("arbitrary", "arbitrary"),
            vmem_limit_bytes=_vmem_limit(est)),
        name="ab_in_rope" if use_rope else "ab_in_ctx",
    )(*args)


def _attn_kernel(q_ref, k_ref, v_ref, o_ref, vt_ref, *, rows, blk):
    @pl.when(pl.program_id(2) == 0)
    def _():
        vt_ref[...] = v_ref[0].astype(F32).T

    k = k_ref[0]
    vt = vt_ref[...].astype(BF16)
    for h in range(N_Q_HEADS // N_KV_HEADS):
        l_ = slice(h * HEAD_DIM, (h + 1) * HEAD_DIM)
        for r0 in range(0, rows, blk):
            qt = q_ref[0, r0:r0 + blk, l_].astype(F32).T.astype(BF16)
            st = _dot(k, qt)
            m = jnp.max(st, axis=0, keepdims=True)
            p = jnp.exp2(st - m)
            l = jnp.sum(p, axis=0, keepdims=True)
            ot = _dot(vt, p.astype(BF16)) / l
            o_ref[0, r0:r0 + blk, l_] = ot.T.astype(BF16)


def _attention(q, k, v, sk, rows, blk=512):
    bsz, sq, _ = q.shape
    gw = (N_Q_HEADS // N_KV_HEADS) * HEAD_DIM
    est = 4 * blk * sk * 12 + 4 * sk * HEAD_DIM * 2 + 4 * rows * gw * 2 + sk * HEAD_DIM * 4
    return pl.pallas_call(
        functools.partial(_attn_kernel, rows=rows, blk=min(blk, rows)),
        out_shape=jax.ShapeDtypeStruct((bsz, sq, ATTN_WIDTH), BF16),
        grid=(bsz, N_KV_HEADS, sq // rows),
        in_specs=[
            pl.BlockSpec((1, rows, gw), lambda b, h, t: (b, t, h)),
            pl.BlockSpec((1, sk, HEAD_DIM), lambda b, h, t: (b, 0, h)),
            pl.BlockSpec((1, sk, HEAD_DIM), lambda b, h, t: (b, 0, h)),
        ],
        out_specs=pl.BlockSpec((1, rows, gw), lambda b, h, t: (b, t, h)),
        scratch_shapes=[pltpu.VMEM((HEAD_DIM, sk), F32)],
        compiler_params=pltpu.CompilerParams(
            dimension_semantics=("arbitrary", "arbitrary", "arbitrary"),
            vmem_limit_bytes=_vmem_limit(est)),
        name="attention",
    )(q, k, v)


def _out_ln_kernel(a1_ref, a2_ref, w_ref, x_ref, g_ref, lng_ref, lnb_ref, o_ref):
    half = a1_ref.shape[2]
    y = _dot(a1_ref[0], w_ref[0:half, :]) + _dot(a2_ref[0], w_ref[half:2 * half, :])
    z = DN_ALPHA * x_ref[0] + g_ref[0] * y
    o_ref[0] = _layer_norm(z, lng_ref[...], lnb_ref[...])


def _out_ln(a1, a2, w, x, gate, lng, lnb, rows):
    bsz, seq, d = x.shape
    half = a1.shape[2]
    est = 2 * half * d * 2 + 4 * rows * d * 4 + 4 * rows * half * 2 + 2 * rows * d * 4
    return pl.pallas_call(
        _out_ln_kernel,
        out_shape=jax.ShapeDtypeStruct((bsz, seq, d), F32),
        grid=(bsz, seq // rows),
        in_specs=[
            pl.BlockSpec((1, rows, half), lambda b, t: (b, t, 0)),
            pl.BlockSpec((1, rows, half), lambda b, t: (b, t, 0)),
            _resident((2 * half, d)),
            pl.BlockSpec((1, rows, d), lambda b, t: (b, t, 0)),
            _row_spec(d),
            _resident((1, d)), _resident((1, d)),
        ],
        out_specs=pl.BlockSpec((1, rows, d), lambda b, t: (b, t, 0)),
        compiler_params=pltpu.CompilerParams(
            dimension_semantics=("arbitrary", "arbitrary"),
            vmem_limit_bytes=_vmem_limit(est)),
        name="out_ln",
    )(a1, a2, w, x, gate, lng, lnb)


def _ffn_kernel(x_ref, xp_ref, xn_ref, sh_ref, sc_ref, g_ref, wu1_ref, wu2_ref, cw1_ref, cw2_ref,
                cb1_ref, cb2_ref, wd_ref, lng_ref, lnb_ref, o_ref, hm_ref, u_ref, gt_ref, *, rows):
    _fill_halo_tile(hm_ref, x_ref, xp_ref, xn_ref, sh_ref[0], sc_ref[0], rows)
    n_chunks = wu1_ref.shape[0]
    tiles = rows // V7X_SUBLANES
    sub = lax.broadcasted_iota(jnp.int32, (tiles, V7X_SUBLANES, FF_CHUNK), 1)

    def up_proj(c, slot):
        hm = hm_ref[...]
        u_ref[slot, 0] = _dot(hm, wu1_ref[c]).reshape(tiles + 2, V7X_SUBLANES, FF_CHUNK)
        u_ref[slot, 1] = _dot(hm, wu2_ref[c]).reshape(tiles + 2, V7X_SUBLANES, FF_CHUNK)

    def conv_gate(c, slot):
        def conv(half, cw_ref, cb_ref):
            u = u_ref[slot, half]
            dn = pltpu.roll(u, 1, 1)
            up = pltpu.roll(u, V7X_SUBLANES - 1, 1)
            prev = jnp.where(sub == 0, dn[0:tiles], dn[1:tiles + 1])
            nxt = jnp.where(sub == V7X_SUBLANES - 1, up[2:tiles + 2], up[1:tiles + 1])
            w = cw_ref[c]
            return cb_ref[c] + w[0:1] * prev + w[1:2] * u[1:tiles + 1] + w[2:3] * nxt

        a1 = conv(0, cw1_ref, cb1_ref)
        a2 = conv(1, cw2_ref, cb2_ref)
        gt_ref[c] = (a1 * _silu(a2)).reshape(rows, FF_CHUNK)

    up_proj(0, 0)

    def pair(k, carry):
        c = 2 * k
        up_proj(c + 1, 1)
        conv_gate(c, 0)
        up_proj(c + 2, 0)
        conv_gate(c + 1, 1)
        return carry

    lax.fori_loop(0, (n_chunks - 1) // 2, pair, 0)
    conv_gate(n_chunks - 1, 0)

    f = _dot(gt_ref[0].astype(BF16), wd_ref[0])
    for c in range(1, n_chunks):
        f = f + _dot(gt_ref[c].astype(BF16), wd_ref[c])
    z = DN_ALPHA * x_ref[0] + g_ref[0] * f
    o_ref[0] = _layer_norm(z, lng_ref[...], lnb_ref[...])


def _ffn(x, sh, sc, gate, wu1, wu2, cw1, cw2, cb1, cb2, wd, lng, lnb, rows):
    bsz, seq, d = x.shape
    nc = wu1.shape[0]
    assert nc % 2 == 1, "the pipeline peels one chunk and pairs the rest"
    est = (3 * nc * d * FF_CHUNK * 2 + 4 * rows * d * 4 + (rows + 2 * HALO) * d * 2
           + nc * rows * FF_CHUNK * 4 + 8 * (rows + 2 * HALO) * FF_CHUNK * 4 + 2 * rows * d * 4)
    return pl.pallas_call(
        functools.partial(_ffn_kernel, rows=rows),
        out_shape=jax.ShapeDtypeStruct((bsz, seq, d), F32),
        grid=(bsz, seq // rows),
        in_specs=_halo_specs(rows, seq, d) + [
            _row_spec(d), _row_spec(d), _row_spec(d),
            _resident((nc, d, FF_CHUNK)), _resident((nc, d, FF_CHUNK)),
            _resident((nc, CONV_W, FF_CHUNK)), _resident((nc, CONV_W, FF_CHUNK)),
            _resident((nc, 1, FF_CHUNK)), _resident((nc, 1, FF_CHUNK)),
            _resident((nc, FF_CHUNK, d)),
            _resident((1, d)), _resident((1, d)),
        ],
        out_specs=pl.BlockSpec((1, rows, d), lambda b, t: (b, t, 0)),
        scratch_shapes=[
            pltpu.VMEM((rows + 2 * HALO, d), BF16),
            pltpu.VMEM((2, 2, rows // HALO + 2, HALO, FF_CHUNK), F32),
            pltpu.VMEM((nc, rows, FF_CHUNK), F32),
        ],
        compiler_params=pltpu.CompilerParams(
            dimension_semantics=("arbitrary", "arbitrary"),
            vmem_limit_bytes=_vmem_limit(est)),
        name="conv_ffn",
    )(x, x, x, sh, sc, gate, wu1, wu2, cw1, cw2, cb1, cb2, wd, lng, lnb)


def _cd_in_kernel(*refs, with_fz, rows):
    x_ref, xp_ref, xn_ref, sh_ref, sc_ref = refs[:5]
    if with_fz:
        wfz_ref, wx_ref, wdt_ref, cw_ref, cb_ref, dtb_ref = refs[5:11]
        f_ref, z_ref, xs_ref, b_ref, c_ref, dt_ref, hm_ref = refs[11:]
    else:
        wx_ref, wdt_ref, cw_ref, cb_ref, dtb_ref = refs[5:10]
        xs_ref, b_ref, c_ref, dt_ref, hm_ref = refs[10:]
    _fill_halo_tile(hm_ref, x_ref, xp_ref, xn_ref, sh_ref[0], sc_ref[0], rows)
    hc = hm_ref[HALO:HALO + rows, :]
    if with_fz:
        pfz = _dot(hc, wfz_ref[...])
        f_ref[0] = pfz[:, 0:FN_WIDTH].astype(BF16)
        z_ref[0] = pfz[:, FN_WIDTH:SSM_OFF]
    a = _conv3(_dot(hm_ref[...], wx_ref[...]), cw_ref[...], cb_ref[...], rows)
    s = _silu(a)
    xs_ref[0] = s[:, 0:SSD_WIDTH]
    b_ref[0] = s[:, SSD_WIDTH:SSD_WIDTH + SSD_GN].astype(BF16)
    c_ref[0] = s[:, SSD_WIDTH + SSD_GN:SSD_CONV_DIM].astype(BF16)
    dt_ref[0] = _softplus(_dot(hc, wdt_ref[...]) + dtb_ref[...])


def _cd_in(x, sh, sc, wfz, wx, wdt, cw, cb, dtb, rows):
    bsz, seq, d = x.shape
    with_fz = wfz is not None
    in_specs = _halo_specs(rows, seq, d) + [_row_spec(d), _row_spec(d)]
    args = [x, x, x, sh, sc]
    out_shape, out_specs = [], []
    if with_fz:
        in_specs.append(_resident((d, SSM_OFF)))
        args.append(wfz)
        out_shape += [jax.ShapeDtypeStruct((bsz, seq, FN_WIDTH), BF16),
                      jax.ShapeDtypeStruct((bsz, seq, SSD_WIDTH), F32)]
        out_specs += [pl.BlockSpec((1, rows, FN_WIDTH), lambda b, t: (b, t, 0)),
                      pl.BlockSpec((1, rows, SSD_WIDTH), lambda b, t: (b, t, 0))]
    in_specs += [_resident((d, SSD_CONV_DIM)), _resident((d, V7X_LANES)),
                 _resident((CONV_W, SSD_CONV_DIM)), _resident((1, SSD_CONV_DIM)),
                 _resident((1, V7X_LANES))]
    args += [wx, wdt, cw, cb, dtb]
    out_shape += [jax.ShapeDtypeStruct((bsz, seq, SSD_WIDTH), F32),
                  jax.ShapeDtypeStruct((bsz, seq, SSD_GN), BF16),
                  jax.ShapeDtypeStruct((bsz, seq, SSD_GN), BF16),
                  jax.ShapeDtypeStruct((bsz, seq, V7X_LANES), F32)]
    out_specs += [pl.BlockSpec((1, rows, SSD_WIDTH), lambda b, t: (b, t, 0)),
                  pl.BlockSpec((1, rows, SSD_GN), lambda b, t: (b, t, 0)),
                  pl.BlockSpec((1, rows, SSD_GN), lambda b, t: (b, t, 0)),
                  pl.BlockSpec((1, rows, V7X_LANES), lambda b, t: (b, t, 0))]
    est = d * 2200 * 2 + 2 * rows * d * 4 + 8 * (rows + 2 * HALO) * SSD_CONV_DIM * 4
    return pl.pallas_call(
        functools.partial(_cd_in_kernel, with_fz=with_fz, rows=rows),
        out_shape=out_shape,
        grid=(bsz, seq // rows),
        in_specs=in_specs,
        out_specs=out_specs,
        scratch_shapes=[pltpu.VMEM((rows + 2 * HALO, d), BF16)],
        compiler_params=pltpu.CompilerParams(
            dimension_semantics=("arbitrary", "arbitrary"),
            vmem_limit_bytes=_vmem_limit(est)),
        name="cd_in" if with_fz else "cd_in_ctx",
    )(*args)


def _cumsum_rows(z, reverse):
    n = z.shape[0]
    idx = lax.broadcasted_iota(jnp.int32, z.shape, 0)
    k = 1
    while k < n:
        if reverse:
            z = z + jnp.where(idx < n - k, pltpu.roll(z, n - k, 0), 0.0)
        else:
            z = z + jnp.where(idx >= k, pltpu.roll(z, k, 0), 0.0)
        k *= 2
    return z


def _lane_expand(row, lane0, width):
    lane = lax.broadcasted_iota(jnp.int32, (1, SSD_HPG * width), 1)
    out = jnp.zeros((1, SSD_HPG * width), F32)
    for j in range(SSD_HPG):
        val = jnp.broadcast_to(row[:, lane0 + j:lane0 + j + 1], (1, SSD_HPG * width))
        out = jnp.where(lane // width == j, val, out)
    return out


def _ssd_chunk(x, bm, cm, dt, a_row, h_ref, lane0, reverse, need_y):
    q = x.shape[0]
    gw = SSD_HPG * SSD_HEADDIM
    acum = _cumsum_rows(dt * a_row, reverse)
    total = acum[0:1, :] if reverse else acum[q - 1:q, :]
    acum_t = acum.T
    dt_t = dt.T
    total_t = jnp.broadcast_to(acum_t[:, 0:1] if reverse else acum_t[:, q - 1:q], acum_t.shape)
    wgt_t = dt_t * jnp.exp(total_t - acum_t)
    lane_g = lax.broadcasted_iota(jnp.int32, (q, gw), 1) // SSD_HEADDIM
    lane_gh = lax.broadcasted_iota(jnp.int32, (SSD_STATE, gw), 1) // SSD_HEADDIM
    if need_y:
        srow_t = acum_t - jnp.log(dt_t)
        ti = lax.broadcasted_iota(jnp.int32, (q, q), 0)
        si = lax.broadcasted_iota(jnp.int32, (q, q), 1)
        mask = (si >= ti) if reverse else (si <= ti)
    ys = []
    for g in range(SSD_GROUPS):
        bg = bm[:, g * SSD_STATE:(g + 1) * SSD_STATE]
        xgb = x[:, g * gw:(g + 1) * gw].astype(BF16)
        x_heads = [jnp.where(lane_g == j, xgb, jnp.zeros_like(xgb)) for j in range(SSD_HPG)]
        hg = h_ref[g]
        if need_y:
            cg = cm[:, g * SSD_STATE:(g + 1) * SSD_STATE]
            cb = _dot_nt(cg, bg)
            cgf = cg.astype(F32)
            hgb = hg.astype(BF16)
            lhs, rhs = [], []
            for j in range(SSD_HPG):
                jl = lane0 + g * SSD_HPG + j
                col = jnp.broadcast_to(acum[:, jl:jl + 1], (q, q))
                row = jnp.broadcast_to(srow_t[jl:jl + 1, :], (q, q))
                lhs.append((cb * jnp.exp(jnp.where(mask, col - row, -jnp.inf))).astype(BF16))
                lhs.append((cgf * jnp.exp(col[:, 0:SSD_STATE])).astype(BF16))
                rhs.append(x_heads[j])
                rhs.append(jnp.where(lane_gh == j, hgb, jnp.zeros_like(hgb)))
            ys.append(_dot(jnp.concatenate(lhs, axis=1), jnp.concatenate(rhs, axis=0)))
        bgt = bg.astype(F32).T
        wb = [(bgt * jnp.broadcast_to(wgt_t[lane0 + g * SSD_HPG + j:lane0 + g * SSD_HPG + j + 1, :],
                                      (SSD_STATE, q))).astype(BF16) for j in range(SSD_HPG)]
        st = _dot(jnp.concatenate(wb, axis=1), jnp.concatenate(x_heads, axis=0))
        h_ref[g] = hg * jnp.exp(_lane_expand(total, lane0 + g * SSD_HPG, SSD_HEADDIM)) + st
    if need_y:
        return jnp.concatenate(ys, axis=1)
    return None


def _ssd_kernel(xc_ref, bc_ref, dtc_ref, xl_ref, bl_ref, cl_ref, dtl_ref, z_ref, alog_ref,
                dskip_ref, ng_ref, o_ref, y_ref, h_ref):
    a_row = -jnp.exp(alog_ref[...])
    nc_ctx = xc_ref.shape[1] // SSD_CHUNK
    nc_lat = xl_ref.shape[1] // SSD_CHUNK

    def rows_of(c):
        return pl.ds(pl.multiple_of(c * SSD_CHUNK, SSD_CHUNK), SSD_CHUNK)

    def ctx_step(c, direction):
        r = rows_of(c)
        _ssd_chunk(xc_ref[0, r, :], bc_ref[0, r, :], None, dtc_ref[0, r, :], a_row,
                   h_ref.at[direction], direction * SSD_HEADS, direction == 1, False)

    def lat_step(c, direction):
        r = rows_of(c)
        y_ref[direction, r, :] = _ssd_chunk(
            xl_ref[0, r, :], bl_ref[0, r, :], cl_ref[0, r, :], dtl_ref[0, r, :], a_row,
            h_ref.at[direction], direction * SSD_HEADS, direction == 1, True)

    h_ref[...] = jnp.zeros_like(h_ref)

    def ctx_pair(i, carry):
        ctx_step(i, 0)
        ctx_step(nc_ctx - 1 - i, 1)
        return carry

    lax.fori_loop(0, nc_ctx, ctx_pair, 0)

    def lat_pair(i, carry):
        lat_step(i, 0)
        lat_step(nc_lat - 1 - i, 1)
        return carry

    lax.fori_loop(0, nc_lat, lat_pair, 0)

    def finish(c, carry):
        r = rows_of(c)
        zz = z_ref[0, r, :]
        yt = (y_ref[0, r, :] + y_ref[1, r, :] + xl_ref[0, r, :] * dskip_ref[...]) * _silu(zz)
        o_ref[0, r, :] = _rms_norm(yt, ng_ref[...]).astype(BF16)
        return carry

    lax.fori_loop(0, nc_lat, finish, 0)


def _ssd(xs_c, b_c, dt_c, xs_l, b_l, c_l, dt_l, z, alog_row, dskip_row, ng_row):
    bsz, sl, w = xs_l.shape
    sc = xs_c.shape[1]
    gw = SSD_HPG * SSD_HEADDIM

    def full(s, n):
        return pl.BlockSpec((1, s, n), lambda b: (b, 0, 0))

    est = (2 * (sl + sc) * (w * 4 + 2 * SSD_GN * 2 + V7X_LANES * 4) + 2 * sl * w * 4
           + 2 * sl * w * 2 + 2 * sl * w * 4)
    return pl.pallas_call(
        _ssd_kernel,
        out_shape=jax.ShapeDtypeStruct((bsz, sl, w), BF16),
        grid=(bsz,),
        in_specs=[
            full(sc, w), full(sc, SSD_GN), full(sc, V7X_LANES),
            full(sl, w), full(sl, SSD_GN), full(sl, SSD_GN), full(sl, V7X_LANES),
            full(sl, w),
            _resident((1, V7X_LANES)), _resident((1, w)), _resident((1, w)),
        ],
        out_specs=full(sl, w),
        scratch_shapes=[
            pltpu.VMEM((2, sl, w), F32),
            pltpu.VMEM((2, SSD_GROUPS, SSD_STATE, gw), F32),
        ],
        compiler_params=pltpu.CompilerParams(
            dimension_semantics=("arbitrary",),
            vmem_limit_bytes=_vmem_limit(est)),
        name="ssd_scan",
    )(xs_c, b_c, dt_c, xs_l, b_l, c_l, dt_l, z, alog_row, dskip_row, ng_row)


def _fourier_kernel(f_ref, cc_ref, sc_ref, w2_ref, o_ref, y_ref, *, rows):
    seq = f_ref.shape[1]
    for g in range(FN_GROUPS):
        l = slice(g * FN_GDIM, (g + 1) * FN_GDIM)
        fg = f_ref[0, :, l]
        y_ref[0:seq, l] = _dot(fg, cc_ref[...]).astype(BF16)
        y_ref[seq:2 * seq, l] = _dot(fg, sc_ref[...]).astype(BF16)
    norm = 1.0 / math.sqrt(seq * FN_GDIM)

    def tile(i, carry):
        r = pl.ds(pl.multiple_of(i * rows, rows), rows)
        o_ref[0, r, :] = (_dot(w2_ref[r, :], y_ref[...]) * norm).astype(BF16)
        return carry

    lax.fori_loop(0, seq // rows, tile, 0)


def _fourier(f, cc, sc, w2, rows):
    bsz, seq, w = f.shape
    est = seq * 2 * seq * 2 + 4 * seq * w * 2 + 2 * seq * w * 2 + 4 * rows * 2 * seq * 2
    return pl.pallas_call(
        functools.partial(_fourier_kernel, rows=rows),
        out_shape=jax.ShapeDtypeStruct((bsz, seq, w), BF16),
        grid=(bsz,),
        in_specs=[
            pl.BlockSpec((1, seq, w), lambda b: (b, 0, 0)),
            _resident((FN_GDIM, FN_GDIM)), _resident((FN_GDIM, FN_GDIM)),
            _resident((seq, 2 * seq)),
        ],
        out_specs=pl.BlockSpec((1, seq, w), lambda b: (b, 0, 0)),
        scratch_shapes=[pltpu.VMEM((2 * seq, w), BF16)],
        compiler_params=pltpu.CompilerParams(
            dimension_semantics=("arbitrary",),
            vmem_limit_bytes=_vmem_limit(est)),
        name="fourier_mix",
    )(f, cc, sc, w2)


def _rope_tables(seq):
    rows = seq // GRID_W
    row = jnp.repeat(jnp.arange(rows, dtype=F32), GRID_W)
    col = jnp.broadcast_to(jnp.arange(GRID_W, dtype=F32), (rows, GRID_W)).reshape(-1)
    n_freq = HEAD_DIM // 4
    inv = ROPE_THETA ** (-jnp.arange(n_freq, dtype=F32) / n_freq)
    ang = jnp.stack([row, col], axis=-1)[:, :, None] * inv
    cos, sin = jnp.cos(ang), jnp.sin(ang)
    zero = jnp.zeros_like(sin[:, 0])
    cos_t = jnp.concatenate([cos[:, 0], cos[:, 0], cos[:, 1], cos[:, 1]], axis=-1)
    s1 = jnp.concatenate([-sin[:, 0], zero, -sin[:, 1], zero], axis=-1)
    s2 = jnp.concatenate([zero, sin[:, 0], zero, sin[:, 1]], axis=-1)
    return cos_t, s1, s2


def _dft_tables(seq):
    def tw(n):
        i = np.arange(n, dtype=np.int64)
        ang = ((i[:, None] * i[None, :]) % n).astype(np.float64) * (2.0 * math.pi / n)
        return np.cos(ang), np.sin(ang)

    cc, sc = tw(FN_GDIM)
    cl, sl = tw(seq)
    w2 = np.concatenate([cl, -sl], axis=1).astype(np.float32)
    return (jnp.asarray(cc.astype(np.float32), BF16), jnp.asarray(sc.astype(np.float32), BF16),
            jnp.asarray(w2, BF16))


def _ffn_params(w_up, conv_w, conv_b, w_down):
    d = w_up.shape[0]
    nc = D_FF // FF_CHUNK

    def cols(m, lo):
        r = m.shape[0]
        return m[:, lo:lo + D_FF].reshape(r, nc, FF_CHUNK).transpose(1, 0, 2)

    wu = w_up.astype(BF16)
    cb = conv_b.reshape(1, 2 * D_FF)
    return (cols(wu, 0), cols(wu, D_FF), cols(conv_w, 0), cols(conv_w, D_FF),
            cols(cb, 0), cols(cb, D_FF), w_down.astype(BF16).reshape(nc, FF_CHUNK, d))


def kernel(x, c, ctx, c_ctx, mod_w, mod_b, ln1_g, ln1_b, ln2_g, ln2_b, ffn_up, ffn_conv_w, ffn_conv_b,
           ffn_down, ab_w_in, ab_w_out, gm_ln_g, gm_ln_b, gm_ws, gm_bs, q_norm_g, k_norm_g, cd_w_in,
           cd_w_out, ssd_conv_w, ssd_conv_b, ssd_dt_bias, ssd_a_log, ssd_d, ssd_norm_g):
    bsz, seq, d = x.shape
    ctx_len = ctx.shape[1]
    lat_rows = 1024
    ffn_rows = 1024
    ctx_rows = min(256, ctx_len)

    pad = (-(bsz + 1)) % V7X_SUBLANES
    cc = jnp.concatenate([c, c_ctx[None], jnp.zeros((pad, d), F32)], axis=0)
    mod = _modulation(cc, mod_w, mod_b)
    rope_tabs = _rope_tables(seq)

    def row(v):
        return v.reshape(1, -1)

    xl, xc = x, ctx
    for i in range(DEPTH):
        last = i == DEPTH - 1
        j = i // 2
        ml = mod[i, :bsz].reshape(bsz, 6, 1, d)
        mc = jnp.broadcast_to(mod[i, bsz].reshape(1, 6, 1, d), (bsz, 6, 1, d))
        sh1, sc1, g1, sh2, sc2, g2 = (ml[:, k] for k in range(6))
        csh1, csc1, cg1, csh2, csc2, cg2 = (mc[:, k] for k in range(6))
        ln1 = (row(ln1_g[i]), row(ln1_b[i]))
        ln2 = (row(ln2_g[i]), row(ln2_b[i]))
        ffn_p = _ffn_params(ffn_up[i], ffn_conv_w[i], ffn_conv_b[i], ffn_down[i])

        if i % 2 == 0:
            w_in = ab_w_in[j].astype(BF16)
            w_out = ab_w_out[j].astype(BF16)
            gm = (row(gm_ln_g[j]), row(gm_ln_b[j]), gm_ws[j].astype(BF16),
                  jnp.repeat(gm_bs[j].T, GM_WIDTH // GM_GROUPS, axis=1),
                  row(q_norm_g[j]), row(k_norm_g[j]))
            kv_len = ctx_len + seq
            a_c, q_c, k_all, v_all = _ab_in(xc, csh1, csc1, w_in, *gm, None, ctx_rows, kv_len, 0)
            a_l, q_l, k_all, v_all = _ab_in(xl, sh1, sc1, w_in, *gm, rope_tabs, 256, kv_len, ctx_len,
                                            kv_into=(k_all, v_all))
            mix_l = (a_l, _attention(q_l, k_all, v_all, kv_len, 1024))
            if not last:
                mix_c = (a_c, _attention(q_c, k_all, v_all, ctx_len, ctx_rows))
        else:
            w_in = cd_w_in[j].astype(BF16)
            w_out = cd_w_out[j].astype(BF16)
            wfz = w_in[:, :SSM_OFF]
            wx = w_in[:, SSM_OFF:SSM_OFF + SSD_CONV_DIM]
            ndt = 2 * SSD_HEADS
            wdt = jnp.pad(w_in[:, SSM_OFF + SSD_CONV_DIM:], ((0, 0), (0, V7X_LANES - ndt)))
            dtb = jnp.pad(ssd_dt_bias[j].reshape(1, ndt), ((0, 0), (0, V7X_LANES - ndt)))
            alog = jnp.pad(ssd_a_log[j].reshape(1, ndt), ((0, 0), (0, V7X_LANES - ndt)))
            cw, cb = ssd_conv_w[j], row(ssd_conv_b[j])
            dskip = row(jnp.repeat(ssd_d[j], SSD_HEADDIM))
            f_l, z_l, xs_l, b_l, c_l, dt_l = _cd_in(xl, sh1, sc1, wfz, wx, wdt, cw, cb, dtb, lat_rows)
            if last:
                xs_c, b_c, _, dt_c = _cd_in(xc, csh1, csc1, None, wx, wdt, cw, cb, dtb, ctx_rows)
            else:
                raise NotImplementedError("odd layers with a context output are not part of this block")
            s_l = _ssd(xs_c, b_c, dt_c, xs_l, b_l, c_l, dt_l, z_l, alog, dskip, row(ssd_norm_g[j]))
            mix_l = (_fourier(f_l, *_dft_tables(seq), 512), s_l)

        xl = _out_ln(*mix_l, w_out, xl, g1, *ln1, 2048)
        xl = _ffn(xl, sh2, sc2, g2, *ffn_p, *ln2, ffn_rows)
        if not last:
            xc = _out_ln(*mix_c, w_out, xc, cg1, *ln1, ctx_rows)
            xc = _ffn(xc, csh2, csc2, cg2, *ffn_p, *ln2, ctx_rows)
    return xl
```

```python
import functools
import math

import jax
import jax.numpy as jnp
import numpy as np
from jax import lax
from jax.experimental import pallas as pl
from jax.experimental.pallas import tpu as pltpu

D_MODEL = 1024
DEPTH = 2
GRID_W = 64
CHUNK = 128
GM_GROUPS = 4
GM_WIDTH = 512
HEAD_DIM = 128
N_Q_HEADS = 4
N_KV_HEADS = 2
ATTN_WIDTH = 512
KV_W = 256
ROPE_THETA = 10000.0
FN_GROUPS = 4
FN_GDIM = 128
FN_WIDTH = 512
SSD_HEADDIM = 64
SSD_HEADS = 8
SSD_GROUPS = 2
SSD_HPG = 4
SSD_STATE = 128
SSD_WIDTH = 512
SSD_GN = 256
SSD_CONV_DIM = 1024
D_FF = 2816
CONV_W = 3
KV_OFF = 2 * GM_WIDTH + ATTN_WIDTH
SSM_OFF = FN_WIDTH + SSD_WIDTH
DN_ALPHA = (2 * DEPTH) ** 0.25
EPS = 1e-6

V7X_LANES = 128
V7X_SUBLANES = 8
V7X_VMEM_BYTES = 64 * 1024 * 1024
AB_SUB = 256
FF_CHUNK = 256
SSD_CHUNK = 256

BF16 = jnp.bfloat16
F32 = jnp.float32
HALO = V7X_SUBLANES


def _vmem_limit(nbytes):
    return int(min(nbytes * 3 // 2 + (8 << 20), V7X_VMEM_BYTES - (3 << 20)))


def _resident(shape):
    nd = len(shape)
    return pl.BlockSpec(shape, lambda *_: (0,) * nd, pipeline_mode=pl.Buffered(1))


def _dot(a, b):
    return jnp.dot(a, b, preferred_element_type=F32)


def _dot_nt(a, b):
    return lax.dot_general(a, b, (((1,), (1,)), ((), ())), preferred_element_type=F32)


def _layer_norm(z, g, b):
    mu = jnp.mean(z, axis=-1, keepdims=True)
    zc = z - mu
    var = jnp.mean(zc * zc, axis=-1, keepdims=True)
    return zc * lax.rsqrt(var + EPS) * g + b


def _rms_norm(z, g):
    return z * lax.rsqrt(jnp.mean(z * z, axis=-1, keepdims=True) + EPS) * g


def _softplus(z):
    return jnp.maximum(z, 0.0) + jnp.log1p(jnp.exp(-jnp.abs(z)))


def _gelu_tanh(z):
    c1 = math.sqrt(2.0 / math.pi)
    h = 0.5 * z
    return h + h * jnp.tanh(z * (c1 + (c1 * 0.044715) * (z * z)))


def _silu(z):
    h = 0.5 * z
    return h + h * jnp.tanh(h)


def _conv3(u, w, b, rows):
    n = u.shape[0]
    um = pltpu.roll(u, 1, 0)[HALO:HALO + rows]
    up = pltpu.roll(u, n - 1, 0)[HALO:HALO + rows]
    return b + w[0:1] * um + w[1:2] * u[HALO:HALO + rows] + w[2:3] * up


def _fill_halo_tile(hm_ref, x_ref, xp_ref, xn_ref, sh, sc, rows):
    t = pl.program_id(1)
    nt = pl.num_programs(1)
    keep_prev = jnp.where(t > 0, 1.0, 0.0)
    keep_next = jnp.where(t < nt - 1, 1.0, 0.0)
    hm_ref[HALO:HALO + rows, :] = (x_ref[0] * (1.0 + sc) + sh).astype(BF16)
    hm_ref[0:HALO, :] = ((xp_ref[0] * (1.0 + sc) + sh) * keep_prev).astype(BF16)
    hm_ref[HALO + rows:2 * HALO + rows, :] = ((xn_ref[0] * (1.0 + sc) + sh) * keep_next).astype(BF16)


def _halo_specs(rows, seq, d):
    nb = rows // HALO
    last = seq // HALO - 1
    return [
        pl.BlockSpec((1, rows, d), lambda b, t: (b, t, 0)),
        pl.BlockSpec((1, HALO, d), lambda b, t: (b, jnp.maximum(t * nb - 1, 0), 0)),
        pl.BlockSpec((1, HALO, d), lambda b, t: (b, jnp.minimum((t + 1) * nb, last), 0)),
    ]


def _row_spec(d):
    return pl.BlockSpec((1, 1, d), lambda b, t: (b, 0, 0))


def _mod_kernel(c_ref, w_ref, b_ref, o_ref):
    c = c_ref[...]
    act = c * jax.nn.sigmoid(c)
    o_ref[0] = jnp.dot(act, w_ref[0], preferred_element_type=F32,
                       precision=lax.Precision.HIGHEST) + b_ref[0]


def _modulation(cc, mod_w, mod_b):
    rows, d = cc.shape
    depth, _, n = mod_w.shape
    tn = 2048
    return pl.pallas_call(
        _mod_kernel,
        out_shape=jax.ShapeDtypeStruct((depth, rows, n), F32),
        grid=(depth, n // tn),
        in_specs=[
            pl.BlockSpec((rows, d), lambda i, j: (0, 0)),
            pl.BlockSpec((1, d, tn), lambda i, j: (i, 0, j)),
            pl.BlockSpec((1, 1, tn), lambda i, j: (i, 0, j)),
        ],
        out_specs=pl.BlockSpec((1, rows, tn), lambda i, j: (i, 0, j)),
        compiler_params=pltpu.CompilerParams(
            dimension_semantics=("arbitrary", "arbitrary"),
            vmem_limit_bytes=_vmem_limit(2 * d * tn * 4)),
        name="modulation",
    )(cc, mod_w, mod_b.reshape(depth, 1, n))


def _ab_in_kernel(*refs, use_rope, rows):
    (x_ref, sh_ref, sc_ref, w_ref, lng_ref, lnb_ref, ws_ref, bias_ref, qg_ref, kg_ref) = refs[:10]
    if use_rope:
        cos_ref, s1_ref, s2_ref = refs[10:13]
    a_ref, q_ref, k_ref, v_ref = refs[-4:]

    gd = GM_WIDTH // GM_GROUPS
    scale = HEAD_DIM ** -0.5 * math.log2(math.e)

    def slab(i, carry):
        r0 = pl.multiple_of(i * AB_SUB, AB_SUB)
        rs = pl.ds(r0, AB_SUB)
        h = (x_ref[0, rs, :] * (1.0 + sc_ref[0]) + sh_ref[0]).astype(BF16)

        u = _gelu_tanh(_dot(h, w_ref[:, 0:GM_WIDTH]))
        vn = _layer_norm(_gelu_tanh(_dot(h, w_ref[:, GM_WIDTH:2 * GM_WIDTH])), lng_ref[...], lnb_ref[...])
        vb = vn.astype(BF16)
        for c in range(AB_SUB // CHUNK):
            r = slice(c * CHUNK, (c + 1) * CHUNK)
            for g in range(GM_GROUPS):
                l = slice(g * gd, (g + 1) * gd)
                mixed = _dot(ws_ref[g], vb[r, l]) + bias_ref[:, l]
                a_ref[0, pl.ds(r0 + c * CHUNK, CHUNK), l] = (u[r, l] * mixed).astype(BF16)

        def rope(z):
            if not use_rope:
                return z
            return (z * cos_ref[rs, :] + pltpu.roll(z, 3 * HEAD_DIM // 4, 1) * s1_ref[rs, :]
                    + pltpu.roll(z, HEAD_DIM // 4, 1) * s2_ref[rs, :])

        pq = _dot(h, w_ref[:, 2 * GM_WIDTH:KV_OFF])
        for hh in range(N_Q_HEADS):
            l = slice(hh * HEAD_DIM, (hh + 1) * HEAD_DIM)
            q_ref[0, rs, l] = (rope(_rms_norm(pq[:, l], qg_ref[...])) * scale).astype(BF16)
        pk = _dot(h, w_ref[:, KV_OFF:KV_OFF + KV_W])
        for hh in range(N_KV_HEADS):
            l = slice(hh * HEAD_DIM, (hh + 1) * HEAD_DIM)
            k_ref[0, rs, l] = rope(_rms_norm(pk[:, l], kg_ref[...])).astype(BF16)
        v_ref[0, rs, :] = _dot(h, w_ref[:, KV_OFF + KV_W:KV_OFF + 2 * KV_W]).astype(BF16)
        return carry

    lax.fori_loop(0, rows // AB_SUB, slab, 0)


def _ab_in(x, sh, sc, w, lng, lnb, ws, bias2d, qg, kg, rope_tabs, rows, kv_len, kv_row0, kv_into=None):
    bsz, seq, d = x.shape
    assert kv_row0 % rows == 0
    kv_blk0 = kv_row0 // rows
    n = w.shape[1]
    use_rope = rope_tabs is not None
    in_specs = [
        pl.BlockSpec((1, rows, d), lambda b, t: (b, t, 0)),
        _row_spec(d), _row_spec(d),
        _resident((d, n)),
        _resident((1, GM_WIDTH)), _resident((1, GM_WIDTH)),
        _resident((GM_GROUPS, CHUNK, CHUNK)),
        _resident((CHUNK, GM_WIDTH)),
        _resident((1, HEAD_DIM)), _resident((1, HEAD_DIM)),
    ]
    args = [x, sh, sc, w, lng, lnb, ws, bias2d, qg, kg]
    if use_rope:
        in_specs += [pl.BlockSpec((rows, HEAD_DIM), lambda b, t: (t, 0))] * 3
        args += list(rope_tabs)
    aliases = {}
    if kv_into is not None:
        aliases = {len(args): 2, len(args) + 1: 3}
        in_specs += [pl.BlockSpec(memory_space=pl.ANY)] * 2
        args += list(kv_into)
    out_shape = [
        jax.ShapeDtypeStruct((bsz, seq, GM_WIDTH), BF16),
        jax.ShapeDtypeStruct((bsz, seq, ATTN_WIDTH), BF16),
        jax.ShapeDtypeStruct((bsz, kv_len, KV_W), BF16),
        jax.ShapeDtypeStruct((bsz, kv_len, KV_W), BF16),
    ]
    out_specs = [
        pl.BlockSpec((1, rows, GM_WIDTH), lambda b, t: (b, t, 0)),
        pl.BlockSpec((1, rows, ATTN_WIDTH), lambda b, t: (b, t, 0)),
        pl.BlockSpec((1, rows, KV_W), lambda b, t: (b, t + kv_blk0, 0)),
        pl.BlockSpec((1, rows, KV_W), lambda b, t: (b, t + kv_blk0, 0)),
    ]
    est = d * n * 2 + 2 * rows * d * 4 + 6 * rows * n * 4
    return pl.pallas_call(
        functools.partial(_ab_in_kernel, use_rope=use_rope, rows=rows),
        out_shape=out_shape,
        grid=(bsz, seq // rows),
        in_specs=in_specs,
        out_specs=out_specs,
        input_output_aliases=aliases,
        compiler_params=pltpu.CompilerParams(
            dimension_semantics=("arbitrary", "arbitrary"),
            vmem_limit_bytes=_vmem_limit(est)),
        name="ab_in_rope" if use_rope else "ab_in_ctx",
    )(*args)


def _attn_kernel(q_ref, k_ref, v_ref, o_ref, vt_ref, *, rows, blk):
    @pl.when(pl.program_id(2) == 0)
    def _():
        vt_ref[...] = v_ref[0].astype(F32).T

    k = k_ref[0]
    vt = vt_ref[...].astype(BF16)
    for h in range(N_Q_HEADS // N_KV_HEADS):
        l_ = slice(h * HEAD_DIM, (h + 1) * HEAD_DIM)
        for r0 in range(0, rows, blk):
            qt = q_ref[0, r0:r0 + blk, l_].astype(F32).T.astype(BF16)
            st = _dot(k, qt)
            m = jnp.max(st, axis=0, keepdims=True)
            p = jnp.exp2(st - m)
            l = jnp.sum(p, axis=0, keepdims=True)
            ot = _dot(vt, p.astype(BF16)) / l
            o_ref[0, r0:r0 + blk, l_] = ot.T.astype(BF16)


def _attention(q, k, v, k_row0, sk, rows, blk=512):
    bsz, sq, _ = q.shape
    assert k_row0 % sk == 0
    kb = k_row0 // sk
    gw = (N_Q_HEADS // N_KV_HEADS) * HEAD_DIM
    est = 4 * blk * sk * 12 + 4 * sk * HEAD_DIM * 2 + 4 * rows * gw * 2 + sk * HEAD_DIM * 4
    return pl.pallas_call(
        functools.partial(_attn_kernel, rows=rows, blk=min(blk, rows)),
        out_shape=jax.ShapeDtypeStruct((bsz, sq, ATTN_WIDTH), BF16),
        grid=(bsz, N_KV_HEADS, sq // rows),
        in_specs=[
            pl.BlockSpec((1, rows, gw), lambda b, h, t: (b, t, h)),
            pl.BlockSpec((1, sk, HEAD_DIM), lambda b, h, t: (b, kb, h)),
            pl.BlockSpec((1, sk, HEAD_DIM), lambda b, h, t: (b, kb, h)),
        ],
        out_specs=pl.BlockSpec((1, rows, gw), lambda b, h, t: (b, t, h)),
        scratch_shapes=[pltpu.VMEM((HEAD_DIM, sk), F32)],
        compiler_params=pltpu.CompilerParams(
            dimension_semantics=("arbitrary", "arbitrary", "arbitrary"),
            vmem_limit_bytes=_vmem_limit(est)),
        name="attention",
    )(q, k, v)


def _out_ln_kernel(a1_ref, a2_ref, w_ref, x_ref, g_ref, lng_ref, lnb_ref, o_ref):
    half = a1_ref.shape[2]
    y = _dot(a1_ref[0], w_ref[0:half, :]) + _dot(a2_ref[0], w_ref[half:2 * half, :])
    z = DN_ALPHA * x_ref[0] + g_ref[0] * y
    o_ref[0] = _layer_norm(z, lng_ref[...], lnb_ref[...])


def _out_ln(a1, a2, w, x, gate, lng, lnb, rows):
    bsz, seq, d = x.shape
    half = a1.shape[2]
    est = 2 * half * d * 2 + 4 * rows * d * 4 + 4 * rows * half * 2 + 2 * rows * d * 4
    return pl.pallas_call(
        _out_ln_kernel,
        out_shape=jax.ShapeDtypeStruct((bsz, seq, d), F32),
        grid=(bsz, seq // rows),
        in_specs=[
            pl.BlockSpec((1, rows, half), lambda b, t: (b, t, 0)),
            pl.BlockSpec((1, rows, half), lambda b, t: (b, t, 0)),
            _resident((2 * half, d)),
            pl.BlockSpec((1, rows, d), lambda b, t: (b, t, 0)),
            _row_spec(d),
            _resident((1, d)), _resident((1, d)),
        ],
        out_specs=pl.BlockSpec((1, rows, d), lambda b, t: (b, t, 0)),
        compiler_params=pltpu.CompilerParams(
            dimension_semantics=("arbitrary", "arbitrary"),
            vmem_limit_bytes=_vmem_limit(est)),
        name="out_ln",
    )(a1, a2, w, x, gate, lng, lnb)


def _ffn_kernel(x_ref, xp_ref, xn_ref, sh_ref, sc_ref, g_ref, wu1_ref, wu2_ref, cw1_ref, cw2_ref,
                cb1_ref, cb2_ref, wd_ref, lng_ref, lnb_ref, o_ref, hm_ref, u_ref, gt_ref, *, rows):
    _fill_halo_tile(hm_ref, x_ref, xp_ref, xn_ref, sh_ref[0], sc_ref[0], rows)
    n_chunks = wu1_ref.shape[0]
    tiles = rows // V7X_SUBLANES
    sub = lax.broadcasted_iota(jnp.int32, (tiles, V7X_SUBLANES, FF_CHUNK), 1)

    def up_proj(c, slot):
        hm = hm_ref[...]
        u_ref[slot, 0] = _dot(hm, wu1_ref[c]).reshape(tiles + 2, V7X_SUBLANES, FF_CHUNK)
        u_ref[slot, 1] = _dot(hm, wu2_ref[c]).reshape(tiles + 2, V7X_SUBLANES, FF_CHUNK)

    def conv_gate(c, slot):
        def conv(half, cw_ref, cb_ref):
            u = u_ref[slot, half]
            dn = pltpu.roll(u, 1, 1)
            up = pltpu.roll(u, V7X_SUBLANES - 1, 1)
            prev = jnp.where(sub == 0, dn[0:tiles], dn[1:tiles + 1])
            nxt = jnp.where(sub == V7X_SUBLANES - 1, up[2:tiles + 2], up[1:tiles + 1])
            w = cw_ref[c]
            return cb_ref[c] + w[0:1] * prev + w[1:2] * u[1:tiles + 1] + w[2:3] * nxt

        a1 = conv(0, cw1_ref, cb1_ref)
        a2 = conv(1, cw2_ref, cb2_ref)
        gt_ref[c] = (a1 * _silu(a2)).reshape(rows, FF_CHUNK)

    up_proj(0, 0)

    def pair(k, carry):
        c = 2 * k
        up_proj(c + 1, 1)
        conv_gate(c, 0)
        up_proj(c + 2, 0)
        conv_gate(c + 1, 1)
        return carry

    lax.fori_loop(0, (n_chunks - 1) // 2, pair, 0)
    conv_gate(n_chunks - 1, 0)

    f = _dot(gt_ref[0].astype(BF16), wd_ref[0])
    for c in range(1, n_chunks):
        f = f + _dot(gt_ref[c].astype(BF16), wd_ref[c])
    z = DN_ALPHA * x_ref[0] + g_ref[0] * f
    o_ref[0] = _layer_norm(z, lng_ref[...], lnb_ref[...])


def _ffn(x, sh, sc, gate, wu1, wu2, cw1, cw2, cb1, cb2, wd, lng, lnb, rows):
    bsz, seq, d = x.shape
    nc = wu1.shape[0]
    assert nc % 2 == 1, "the pipeline peels one chunk and pairs the rest"
    est = (3 * nc * d * FF_CHUNK * 2 + 4 * rows * d * 4 + (rows + 2 * HALO) * d * 2
           + nc * rows * FF_CHUNK * 4 + 8 * (rows + 2 * HALO) * FF_CHUNK * 4 + 2 * rows * d * 4)
    return pl.pallas_call(
        functools.partial(_ffn_kernel, rows=rows),
        out_shape=jax.ShapeDtypeStruct((bsz, seq, d), F32),
        grid=(bsz, seq // rows),
        in_specs=_halo_specs(rows, seq, d) + [
            _row_spec(d), _row_spec(d), _row_spec(d),
            _resident((nc, d, FF_CHUNK)), _resident((nc, d, FF_CHUNK)),
            _resident((nc, CONV_W, FF_CHUNK)), _resident((nc, CONV_W, FF_CHUNK)),
            _resident((nc, 1, FF_CHUNK)), _resident((nc, 1, FF_CHUNK)),
            _resident((nc, FF_CHUNK, d)),
            _resident((1, d)), _resident((1, d)),
        ],
        out_specs=pl.BlockSpec((1, rows, d), lambda b, t: (b, t, 0)),
        scratch_shapes=[
            pltpu.VMEM((rows + 2 * HALO, d), BF16),
            pltpu.VMEM((2, 2, rows // HALO + 2, HALO, FF_CHUNK), F32),
            pltpu.VMEM((nc, rows, FF_CHUNK), F32),
        ],
        compiler_params=pltpu.CompilerParams(
            dimension_semantics=("arbitrary", "arbitrary"),
            vmem_limit_bytes=_vmem_limit(est)),
        name="conv_ffn",
    )(x, x, x, sh, sc, gate, wu1, wu2, cw1, cw2, cb1, cb2, wd, lng, lnb)


def _cd_in_kernel(*refs, with_fz, rows):
    x_ref, xp_ref, xn_ref, sh_ref, sc_ref = refs[:5]
    if with_fz:
        wfz_ref, wx_ref, wdt_ref, cw_ref, cb_ref, dtb_ref = refs[5:11]
        f_ref, z_ref, xs_ref, b_ref, c_ref, dt_ref, hm_ref = refs[11:]
    else:
        wx_ref, wdt_ref, cw_ref, cb_ref, dtb_ref = refs[5:10]
        xs_ref, b_ref, c_ref, dt_ref, hm_ref = refs[10:]
    _fill_halo_tile(hm_ref, x_ref, xp_ref, xn_ref, sh_ref[0], sc_ref[0], rows)
    hc = hm_ref[HALO:HALO + rows, :]
    if with_fz:
        pfz = _dot(hc, wfz_ref[...])
        f_ref[0] = pfz[:, 0:FN_WIDTH].astype(BF16)
        z_ref[0] = pfz[:, FN_WIDTH:SSM_OFF]
    a = _conv3(_dot(hm_ref[...], wx_ref[...]), cw_ref[...], cb_ref[...], rows)
    s = _silu(a)
    xs_ref[0] = s[:, 0:SSD_WIDTH]
    b_ref[0] = s[:, SSD_WIDTH:SSD_WIDTH + SSD_GN].astype(BF16)
    c_ref[0] = s[:, SSD_WIDTH + SSD_GN:SSD_CONV_DIM].astype(BF16)
    dt_ref[0] = _softplus(_dot(hc, wdt_ref[...]) + dtb_ref[...])


def _cd_in(x, sh, sc, wfz, wx, wdt, cw, cb, dtb, rows):
    bsz, seq, d = x.shape
    with_fz = wfz is not None
    in_specs = _halo_specs(rows, seq, d) + [_row_spec(d), _row_spec(d)]
    args = [x, x, x, sh, sc]
    out_shape, out_specs = [], []
    if with_fz:
        in_specs.append(_resident((d, SSM_OFF)))
        args.append(wfz)
        out_shape += [jax.ShapeDtypeStruct((bsz, seq, FN_WIDTH), BF16),
                      jax.ShapeDtypeStruct((bsz, seq, SSD_WIDTH), F32)]
        out_specs += [pl.BlockSpec((1, rows, FN_WIDTH), lambda b, t: (b, t, 0)),
                      pl.BlockSpec((1, rows, SSD_WIDTH), lambda b, t: (b, t, 0))]
    in_specs += [_resident((d, SSD_CONV_DIM)), _resident((d, V7X_LANES)),
                 _resident((CONV_W, SSD_CONV_DIM)), _resident((1, SSD_CONV_DIM)),
                 _resident((1, V7X_LANES))]
    args += [wx, wdt, cw, cb, dtb]
    out_shape += [jax.ShapeDtypeStruct((bsz, seq, SSD_WIDTH), F32),
                  jax.ShapeDtypeStruct((bsz, seq, SSD_GN), BF16),
                  jax.ShapeDtypeStruct((bsz, seq, SSD_GN), BF16),
                  jax.ShapeDtypeStruct((bsz, seq, V7X_LANES), F32)]
    out_specs += [pl.BlockSpec((1, rows, SSD_WIDTH), lambda b, t: (b, t, 0)),
                  pl.BlockSpec((1, rows, SSD_GN), lambda b, t: (b, t, 0)),
                  pl.BlockSpec((1, rows, SSD_GN), lambda b, t: (b, t, 0)),
                  pl.BlockSpec((1, rows, V7X_LANES), lambda b, t: (b, t, 0))]
    est = d * 2200 * 2 + 2 * rows * d * 4 + 8 * (rows + 2 * HALO) * SSD_CONV_DIM * 4
    return pl.pallas_call(
        functools.partial(_cd_in_kernel, with_fz=with_fz, rows=rows),
        out_shape=out_shape,
        grid=(bsz, seq // rows),
        in_specs=in_specs,
        out_specs=out_specs,
        scratch_shapes=[pltpu.VMEM((rows + 2 * HALO, d), BF16)],
        compiler_params=pltpu.CompilerParams(
            dimension_semantics=("arbitrary", "arbitrary"),
            vmem_limit_bytes=_vmem_limit(est)),
        name="cd_in" if with_fz else "cd_in_ctx",
    )(*args)


def _cumsum_rows(z, reverse):
    n = z.shape[0]
    idx = lax.broadcasted_iota(jnp.int32, z.shape, 0)
    k = 1
    while k < n:
        if reverse:
            z = z + jnp.where(idx < n - k, pltpu.roll(z, n - k, 0), 0.0)
        else:
            z = z + jnp.where(idx >= k, pltpu.roll(z, k, 0), 0.0)
        k *= 2
    return z


def _lane_expand(row, lane0, width):
    lane = lax.broadcasted_iota(jnp.int32, (1, SSD_HPG * width), 1)
    out = jnp.zeros((1, SSD_HPG * width), F32)
    for j in range(SSD_HPG):
        val = jnp.broadcast_to(row[:, lane0 + j:lane0 + j + 1], (1, SSD_HPG * width))
        out = jnp.where(lane // width == j, val, out)
    return out


def _ssd_chunk(x, bm, cm, dt, a_row, h_ref, lane0, reverse, need_y):
    q = x.shape[0]
    gw = SSD_HPG * SSD_HEADDIM
    acum = _cumsum_rows(dt * a_row, reverse)
    total = acum[0:1, :] if reverse else acum[q - 1:q, :]
    acum_t = acum.T
    dt_t = dt.T
    total_t = jnp.broadcast_to(acum_t[:, 0:1] if reverse else acum_t[:, q - 1:q], acum_t.shape)
    wgt_t = dt_t * jnp.exp(total_t - acum_t)
    lane_g = lax.broadcasted_iota(jnp.int32, (q, gw), 1) // SSD_HEADDIM
    lane_gh = lax.broadcasted_iota(jnp.int32, (SSD_STATE, gw), 1) // SSD_HEADDIM
    if need_y:
        srow_t = acum_t - jnp.log(dt_t)
        ti = lax.broadcasted_iota(jnp.int32, (q, q), 0)
        si = lax.broadcasted_iota(jnp.int32, (q, q), 1)
        mask = (si >= ti) if reverse else (si <= ti)
    ys = []
    for g in range(SSD_GROUPS):
        bg = bm[:, g * SSD_STATE:(g + 1) * SSD_STATE]
        xgb = x[:, g * gw:(g + 1) * gw].astype(BF16)
        x_heads = [jnp.where(lane_g == j, xgb, jnp.zeros_like(xgb)) for j in range(SSD_HPG)]
        hg = h_ref[g]
        if need_y:
            cg = cm[:, g * SSD_STATE:(g + 1) * SSD_STATE]
            cb = _dot_nt(cg, bg)
            cgf = cg.astype(F32)
            hgb = hg.astype(BF16)
            lhs, rhs = [], []
            for j in range(SSD_HPG):
                jl = lane0 + g * SSD_HPG + j
                col = jnp.broadcast_to(acum[:, jl:jl + 1], (q, q))
                row = jnp.broadcast_to(srow_t[jl:jl + 1, :], (q, q))
                lhs.append((cb * jnp.exp(jnp.where(mask, col - row, -jnp.inf))).astype(BF16))
                lhs.append((cgf * jnp.exp(col[:, 0:SSD_STATE])).astype(BF16))
                rhs.append(x_heads[j])
                rhs.append(jnp.where(lane_gh == j, hgb, jnp.zeros_like(hgb)))
            ys.append(_dot(jnp.concatenate(lhs, axis=1), jnp.concatenate(rhs, axis=0)))
        bgt = bg.astype(F32).T
        wb = [(bgt * jnp.broadcast_to(wgt_t[lane0 + g * SSD_HPG + j:lane0 + g * SSD_HPG + j + 1, :],
                                      (SSD_STATE, q))).astype(BF16) for j in range(SSD_HPG)]
        st = _dot(jnp.concatenate(wb, axis=1), jnp.concatenate(x_heads, axis=0))
        h_ref[g] = hg * jnp.exp(_lane_expand(total, lane0 + g * SSD_HPG, SSD_HEADDIM)) + st
    if need_y:
        return jnp.concatenate(ys, axis=1)
    return None


def _ssd_kernel(xc_ref, bc_ref, dtc_ref, xl_ref, bl_ref, cl_ref, dtl_ref, z_ref, alog_ref,
                dskip_ref, ng_ref, o_ref, y_ref, h_ref):
    a_row = -jnp.exp(alog_ref[...])
    nc_ctx = xc_ref.shape[1] // SSD_CHUNK
    nc_lat = xl_ref.shape[1] // SSD_CHUNK

    def rows_of(c):
        return pl.ds(pl.multiple_of(c * SSD_CHUNK, SSD_CHUNK), SSD_CHUNK)

    def ctx_step(c, direction):
        r = rows_of(c)
        _ssd_chunk(xc_ref[0, r, :], bc_ref[0, r, :], None, dtc_ref[0, r, :], a_row,
                   h_ref.at[direction], direction * SSD_HEADS, direction == 1, False)

    def lat_step(c, direction):
        r = rows_of(c)
        y_ref[direction, r, :] = _ssd_chunk(
            xl_ref[0, r, :], bl_ref[0, r, :], cl_ref[0, r, :], dtl_ref[0, r, :], a_row,
            h_ref.at[direction], direction * SSD_HEADS, direction == 1, True)

    h_ref[...] = jnp.zeros_like(h_ref)

    def ctx_pair(i, carry):
        ctx_step(i, 0)
        ctx_step(nc_ctx - 1 - i, 1)
        return carry

    lax.fori_loop(0, nc_ctx, ctx_pair, 0)

    def lat_pair(i, carry):
        lat_step(i, 0)
        lat_step(nc_lat - 1 - i, 1)
        return carry

    lax.fori_loop(0, nc_lat, lat_pair, 0)

    def finish(c, carry):
        r = rows_of(c)
        zz = z_ref[0, r, :]
        yt = (y_ref[0, r, :] + y_ref[1, r, :] + xl_ref[0, r, :] * dskip_ref[...]) * _silu(zz)
        o_ref[0, r, :] = _rms_norm(yt, ng_ref[...]).astype(BF16)
        return carry

    lax.fori_loop(0, nc_lat, finish, 0)


def _ssd(xs_c, b_c, dt_c, xs_l, b_l, c_l, dt_l, z, alog_row, dskip_row, ng_row):
    bsz, sl, w = xs_l.shape
    sc = xs_c.shape[1]
    gw = SSD_HPG * SSD_HEADDIM

    def full(s, n):
        return pl.BlockSpec((1, s, n), lambda b: (b, 0, 0))

    est = (2 * (sl + sc) * (w * 4 + 2 * SSD_GN * 2 + V7X_LANES * 4) + 2 * sl * w * 4
           + 2 * sl * w * 2 + 2 * sl * w * 4)
    return pl.pallas_call(
        _ssd_kernel,
        out_shape=jax.ShapeDtypeStruct((bsz, sl, w), BF16),
        grid=(bsz,),
        in_specs=[
            full(sc, w), full(sc, SSD_GN), full(sc, V7X_LANES),
            full(sl, w), full(sl, SSD_GN), full(sl, SSD_GN), full(sl, V7X_LANES),
            full(sl, w),
            _resident((1, V7X_LANES)), _resident((1, w)), _resident((1, w)),
        ],
        out_specs=full(sl, w),
        scratch_shapes=[
            pltpu.VMEM((2, sl, w), F32),
            pltpu.VMEM((2, SSD_GROUPS, SSD_STATE, gw), F32),
        ],
        compiler_params=pltpu.CompilerParams(
            dimension_semantics=("arbitrary",),
            vmem_limit_bytes=_vmem_limit(est)),
        name="ssd_scan",
    )(xs_c, b_c, dt_c, xs_l, b_l, c_l, dt_l, z, alog_row, dskip_row, ng_row)


def _fourier_kernel(f_ref, cc_ref, sc_ref, w2_ref, o_ref, y_ref, *, rows):
    seq = f_ref.shape[1]
    for g in range(FN_GROUPS):
        l = slice(g * FN_GDIM, (g + 1) * FN_GDIM)
        fg = f_ref[0, :, l]
        y_ref[0:seq, l] = _dot(fg, cc_ref[...]).astype(BF16)
        y_ref[seq:2 * seq, l] = _dot(fg, sc_ref[...]).astype(BF16)
    norm = 1.0 / math.sqrt(seq * FN_GDIM)

    def tile(i, carry):
        r = pl.ds(pl.multiple_of(i * rows, rows), rows)
        o_ref[0, r, :] = (_dot(w2_ref[r, :], y_ref[...]) * norm).astype(BF16)
        return carry

    lax.fori_loop(0, seq // rows, tile, 0)


def _fourier(f, cc, sc, w2, rows):
    bsz, seq, w = f.shape
    est = seq * 2 * seq * 2 + 4 * seq * w * 2 + 2 * seq * w * 2 + 4 * rows * 2 * seq * 2
    return pl.pallas_call(
        functools.partial(_fourier_kernel, rows=rows),
        out_shape=jax.ShapeDtypeStruct((bsz, seq, w), BF16),
        grid=(bsz,),
        in_specs=[
            pl.BlockSpec((1, seq, w), lambda b: (b, 0, 0)),
            _resident((FN_GDIM, FN_GDIM)), _resident((FN_GDIM, FN_GDIM)),
            _resident((seq, 2 * seq)),
        ],
        out_specs=pl.BlockSpec((1, seq, w), lambda b: (b, 0, 0)),
        scratch_shapes=[pltpu.VMEM((2 * seq, w), BF16)],
        compiler_params=pltpu.CompilerParams(
            dimension_semantics=("arbitrary",),
            vmem_limit_bytes=_vmem_limit(est)),
        name="fourier_mix",
    )(f, cc, sc, w2)


def _rope_tables(seq):
    rows = seq // GRID_W
    row = jnp.repeat(jnp.arange(rows, dtype=F32), GRID_W)
    col = jnp.broadcast_to(jnp.arange(GRID_W, dtype=F32), (rows, GRID_W)).reshape(-1)
    n_freq = HEAD_DIM // 4
    inv = ROPE_THETA ** (-jnp.arange(n_freq, dtype=F32) / n_freq)
    ang = jnp.stack([row, col], axis=-1)[:, :, None] * inv
    cos, sin = jnp.cos(ang), jnp.sin(ang)
    zero = jnp.zeros_like(sin[:, 0])
    cos_t = jnp.concatenate([cos[:, 0], cos[:, 0], cos[:, 1], cos[:, 1]], axis=-1)
    s1 = jnp.concatenate([-sin[:, 0], zero, -sin[:, 1], zero], axis=-1)
    s2 = jnp.concatenate([zero, sin[:, 0], zero, sin[:, 1]], axis=-1)
    return cos_t, s1, s2


def _dft_tables(seq):
    def tw(n):
        i = np.arange(n, dtype=np.int64)
        ang = ((i[:, None] * i[None, :]) % n).astype(np.float64) * (2.0 * math.pi / n)
        return np.cos(ang), np.sin(ang)

    cc, sc = tw(FN_GDIM)
    cl, sl = tw(seq)
    w2 = np.concatenate([cl, -sl], axis=1).astype(np.float32)
    return (jnp.asarray(cc.astype(np.float32), BF16), jnp.asarray(sc.astype(np.float32), BF16),
            jnp.asarray(w2, BF16))


def _ffn_params(w_up, conv_w, conv_b, w_down):
    d = w_up.shape[0]
    nc = D_FF // FF_CHUNK

    def cols(m, lo):
        r = m.shape[0]
        return m[:, lo:lo + D_FF].reshape(r, nc, FF_CHUNK).transpose(1, 0, 2)

    wu = w_up.astype(BF16)
    cb = conv_b.reshape(1, 2 * D_FF)
    return (cols(wu, 0), cols(wu, D_FF), cols(conv_w, 0), cols(conv_w, D_FF),
            cols(cb, 0), cols(cb, D_FF), w_down.astype(BF16).reshape(nc, FF_CHUNK, d))


def kernel(x, c, ctx, c_ctx, mod_w, mod_b, ln1_g, ln1_b, ln2_g, ln2_b, ffn_up, ffn_conv_w, ffn_conv_b,
           ffn_down, ab_w_in, ab_w_out, gm_ln_g, gm_ln_b, gm_ws, gm_bs, q_norm_g, k_norm_g, cd_w_in,
           cd_w_out, ssd_conv_w, ssd_conv_b, ssd_dt_bias, ssd_a_log, ssd_d, ssd_norm_g):
    bsz, seq, d = x.shape
    ctx_len = ctx.shape[1]
    lat_rows = 1024
    ffn_rows = 1024
    ctx_rows = min(256, ctx_len)

    pad = (-(bsz + 1)) % V7X_SUBLANES
    cc = jnp.concatenate([c, c_ctx[None], jnp.zeros((pad, d), F32)], axis=0)
    mod = _modulation(cc, mod_w, mod_b)
    rope_tabs = _rope_tables(seq)

    def row(v):
        return v.reshape(1, -1)

    xl, xc = x, ctx
    for i in range(DEPTH):
        last = i == DEPTH - 1
        j = i // 2
        ml = mod[i, :bsz].reshape(bsz, 6, 1, d)
        mc = jnp.broadcast_to(mod[i, bsz].reshape(1, 6, 1, d), (bsz, 6, 1, d))
        sh1, sc1, g1, sh2, sc2, g2 = (ml[:, k] for k in range(6))
        csh1, csc1, cg1, csh2, csc2, cg2 = (mc[:, k] for k in range(6))
        ln1 = (row(ln1_g[i]), row(ln1_b[i]))
        ln2 = (row(ln2_g[i]), row(ln2_b[i]))
        ffn_p = _ffn_params(ffn_up[i], ffn_conv_w[i], ffn_conv_b[i], ffn_down[i])

        if i % 2 == 0:
            w_in = ab_w_in[j].astype(BF16)
            w_out = ab_w_out[j].astype(BF16)
            gm = (row(gm_ln_g[j]), row(gm_ln_b[j]), gm_ws[j].astype(BF16),
                  jnp.repeat(gm_bs[j].T, GM_WIDTH // GM_GROUPS, axis=1),
                  row(q_norm_g[j]), row(k_norm_g[j]))
            kv_len = seq + ctx_len
            a_c, q_c, k_all, v_all = _ab_in(xc, csh1, csc1, w_in, *gm, None, ctx_rows, kv_len, seq)
            a_l, q_l, k_all, v_all = _ab_in(xl, sh1, sc1, w_in, *gm, rope_tabs, lat_rows, kv_len, 0,
                                            kv_into=(k_all, v_all))
            mix_l = (a_l, _attention(q_l, k_all, v_all, 0, kv_len, 1024))
            if not last:
                mix_c = (a_c, _attention(q_c, k_all, v_all, seq, ctx_len, ctx_rows))
        else:
            w_in = cd_w_in[j].astype(BF16)
            w_out = cd_w_out[j].astype(BF16)
            wfz = w_in[:, :SSM_OFF]
            wx = w_in[:, SSM_OFF:SSM_OFF + SSD_CONV_DIM]
            ndt = 2 * SSD_HEADS
            wdt = jnp.pad(w_in[:, SSM_OFF + SSD_CONV_DIM:], ((0, 0), (0, V7X_LANES - ndt)))
            dtb = jnp.pad(ssd_dt_bias[j].reshape(1, ndt), ((0, 0), (0, V7X_LANES - ndt)))
            alog = jnp.pad(ssd_a_log[j].reshape(1, ndt), ((0, 0), (0, V7X_LANES - ndt)))
            cw, cb = ssd_conv_w[j], row(ssd_conv_b[j])
            dskip = row(jnp.repeat(ssd_d[j], SSD_HEADDIM))
            f_l, z_l, xs_l, b_l, c_l, dt_l = _cd_in(xl, sh1, sc1, wfz, wx, wdt, cw, cb, dtb, lat_rows)
            if last:
                xs_c, b_c, _, dt_c = _cd_in(xc, csh1, csc1, None, wx, wdt, cw, cb, dtb, ctx_rows)
            else:
                raise NotImplementedError("odd layers with a context output are not part of this block")
            s_l = _ssd(xs_c, b_c, dt_c, xs_l, b_l, c_l, dt_l, z_l, alog, dskip, row(ssd_norm_g[j]))
            mix_l = (_fourier(f_l, *_dft_tables(seq), 512), s_l)

        xl = _out_ln(*mix_l, w_out, xl, g1, *ln1, 2048)
        xl = _ffn(xl, sh2, sc2, g2, *ffn_p, *ln2, ffn_rows)
        if not last:
            xc = _out_ln(*mix_c, w_out, xc, cg1, *ln1, ctx_rows)
            xc = _ffn(xc, csh2, csc2, cg2, *ffn_p, *ln2, ctx_rows)
    return xl
```

```python
import functools
import math

import jax
import jax.numpy as jnp
import numpy as np
from jax import lax
from jax.experimental import pallas as pl
from jax.experimental.pallas import tpu as pltpu

D_MODEL = 1024
DEPTH = 2
GRID_W = 64
CHUNK = 128
GM_GROUPS = 4
GM_WIDTH = 512
HEAD_DIM = 128
N_Q_HEADS = 4
N_KV_HEADS = 2
ATTN_WIDTH = 512
KV_W = 256
ROPE_THETA = 10000.0
FN_GROUPS = 4
FN_GDIM = 128
FN_WIDTH = 512
SSD_HEADDIM = 64
SSD_HEADS = 8
SSD_GROUPS = 2
SSD_HPG = 4
SSD_STATE = 128
SSD_WIDTH = 512
SSD_GN = 256
SSD_CONV_DIM = 1024
D_FF = 2816
CONV_W = 3
KV_OFF = 2 * GM_WIDTH + ATTN_WIDTH
SSM_OFF = FN_WIDTH + SSD_WIDTH
DN_ALPHA = (2 * DEPTH) ** 0.25
EPS = 1e-6

V7X_LANES = 128
V7X_SUBLANES = 8
V7X_VMEM_BYTES = 64 * 1024 * 1024
AB_SUB = 256
ATTN_PAD = 16
FF_CHUNK = 256
SSD_CHUNK = 256

BF16 = jnp.bfloat16
F32 = jnp.float32
HALO = V7X_SUBLANES


def _vmem_limit(nbytes):
    return int(min(nbytes * 3 // 2 + (8 << 20), V7X_VMEM_BYTES - (3 << 20)))


def _resident(shape):
    nd = len(shape)
    return pl.BlockSpec(shape, lambda *_: (0,) * nd, pipeline_mode=pl.Buffered(1))


def _dot(a, b):
    return jnp.dot(a, b, preferred_element_type=F32)


def _dot_nt(a, b):
    return lax.dot_general(a, b, (((1,), (1,)), ((), ())), preferred_element_type=F32)


def _layer_norm(z, g, b):
    mu = jnp.mean(z, axis=-1, keepdims=True)
    zc = z - mu
    var = jnp.mean(zc * zc, axis=-1, keepdims=True)
    return zc * lax.rsqrt(var + EPS) * g + b


def _rms_norm(z, g):
    return z * lax.rsqrt(jnp.mean(z * z, axis=-1, keepdims=True) + EPS) * g


def _softplus(z):
    return jnp.maximum(z, 0.0) + jnp.log1p(jnp.exp(-jnp.abs(z)))


def _gelu_tanh(z):
    c1 = math.sqrt(2.0 / math.pi)
    h = 0.5 * z
    return h + h * jnp.tanh(z * (c1 + (c1 * 0.044715) * (z * z)))


def _silu(z):
    h = 0.5 * z
    return h + h * jnp.tanh(h)


def _conv3(u, w, b, rows):
    n = u.shape[0]
    um = pltpu.roll(u, 1, 0)[HALO:HALO + rows]
    up = pltpu.roll(u, n - 1, 0)[HALO:HALO + rows]
    return b + w[0:1] * um + w[1:2] * u[HALO:HALO + rows] + w[2:3] * up


def _fill_halo_tile(hm_ref, x_ref, xp_ref, xn_ref, sh, sc, rows):
    t = pl.program_id(1)
    nt = pl.num_programs(1)
    keep_prev = jnp.where(t > 0, 1.0, 0.0)
    keep_next = jnp.where(t < nt - 1, 1.0, 0.0)
    hm_ref[HALO:HALO + rows, :] = (x_ref[0] * (1.0 + sc) + sh).astype(BF16)
    hm_ref[0:HALO, :] = ((xp_ref[0] * (1.0 + sc) + sh) * keep_prev).astype(BF16)
    hm_ref[HALO + rows:2 * HALO + rows, :] = ((xn_ref[0] * (1.0 + sc) + sh) * keep_next).astype(BF16)


def _halo_specs(rows, seq, d):
    nb = rows // HALO
    last = seq // HALO - 1
    return [
        pl.BlockSpec((1, rows, d), lambda b, t: (b, t, 0)),
        pl.BlockSpec((1, HALO, d), lambda b, t: (b, jnp.maximum(t * nb - 1, 0), 0)),
        pl.BlockSpec((1, HALO, d), lambda b, t: (b, jnp.minimum((t + 1) * nb, last), 0)),
    ]


def _row_spec(d):
    return pl.BlockSpec((1, 1, d), lambda b, t: (b, 0, 0))


def _mod_kernel(c_ref, w_ref, b_ref, o_ref):
    c = c_ref[...]
    act = c * jax.nn.sigmoid(c)
    o_ref[0] = jnp.dot(act, w_ref[0], preferred_element_type=F32,
                       precision=lax.Precision.HIGHEST) + b_ref[0]


def _modulation(cc, mod_w, mod_b):
    rows, d = cc.shape
    depth, _, n = mod_w.shape
    tn = 2048
    return pl.pallas_call(
        _mod_kernel,
        out_shape=jax.ShapeDtypeStruct((depth, rows, n), F32),
        grid=(depth, n // tn),
        in_specs=[
            pl.BlockSpec((rows, d), lambda i, j: (0, 0)),
            pl.BlockSpec((1, d, tn), lambda i, j: (i, 0, j)),
            pl.BlockSpec((1, 1, tn), lambda i, j: (i, 0, j)),
        ],
        out_specs=pl.BlockSpec((1, rows, tn), lambda i, j: (i, 0, j)),
        compiler_params=pltpu.CompilerParams(
            dimension_semantics=("arbitrary", "arbitrary"),
            vmem_limit_bytes=_vmem_limit(2 * d * tn * 4)),
        name="modulation",
    )(cc, mod_w, mod_b.reshape(depth, 1, n))


def _ab_in_kernel(*refs, use_rope, rows):
    (x_ref, sh_ref, sc_ref, w_ref, lng_ref, lnb_ref, ws_ref, bias_ref, qg_ref, kg_ref) = refs[:10]
    if use_rope:
        cos_ref, sin_ref = refs[10:12]
    a_ref, q_ref, k_ref, v_ref = refs[-4:]

    gd = GM_WIDTH // GM_GROUPS
    scale = HEAD_DIM ** -0.5 * math.log2(math.e)

    def slab(i, carry):
        r0 = pl.multiple_of(i * AB_SUB, AB_SUB)
        rs = pl.ds(r0, AB_SUB)
        h = (x_ref[0, rs, :] * (1.0 + sc_ref[0]) + sh_ref[0]).astype(BF16)

        u = _gelu_tanh(_dot(h, w_ref[:, 0:GM_WIDTH]))
        vn = _layer_norm(_gelu_tanh(_dot(h, w_ref[:, GM_WIDTH:2 * GM_WIDTH])), lng_ref[...], lnb_ref[...])
        vb = vn.astype(BF16)
        for c in range(AB_SUB // CHUNK):
            r = slice(c * CHUNK, (c + 1) * CHUNK)
            for g in range(GM_GROUPS):
                l = slice(g * gd, (g + 1) * gd)
                mixed = _dot(ws_ref[g], vb[r, l]) + bias_ref[:, l]
                a_ref[0, pl.ds(r0 + c * CHUNK, CHUNK), l] = (u[r, l] * mixed).astype(BF16)

        def rope(z):
            if not use_rope:
                return z
            return z * cos_ref[rs, :] + pltpu.roll(z, HEAD_DIM // 2, 1) * sin_ref[rs, :]

        pq = _dot(h, w_ref[:, 2 * GM_WIDTH:KV_OFF])
        for hh in range(N_Q_HEADS):
            l = slice(hh * HEAD_DIM, (hh + 1) * HEAD_DIM)
            q_ref[0, rs, l] = (rope(_rms_norm(pq[:, l], qg_ref[...])) * scale).astype(BF16)
        pk = _dot(h, w_ref[:, KV_OFF:KV_OFF + KV_W])
        for hh in range(N_KV_HEADS):
            l = slice(hh * HEAD_DIM, (hh + 1) * HEAD_DIM)
            k_ref[0, rs, l] = rope(_rms_norm(pk[:, l], kg_ref[...])).astype(BF16)
        v_ref[0, rs, :] = _dot(h, w_ref[:, KV_OFF + KV_W:KV_OFF + 2 * KV_W]).astype(BF16)
        return carry

    lax.fori_loop(0, rows // AB_SUB, slab, 0)


def _ab_in(x, sh, sc, w, lng, lnb, ws, bias2d, qg, kg, rope_tabs, rows, kv_len, kv_row0, kv_into=None):
    bsz, seq, d = x.shape
    assert kv_row0 % rows == 0
    kv_blk0 = kv_row0 // rows
    n = w.shape[1]
    use_rope = rope_tabs is not None
    in_specs = [
        pl.BlockSpec((1, rows, d), lambda b, t: (b, t, 0)),
        _row_spec(d), _row_spec(d),
        _resident((d, n)),
        _resident((1, GM_WIDTH)), _resident((1, GM_WIDTH)),
        _resident((GM_GROUPS, CHUNK, CHUNK)),
        _resident((CHUNK, GM_WIDTH)),
        _resident((1, HEAD_DIM)), _resident((1, HEAD_DIM)),
    ]
    args = [x, sh, sc, w, lng, lnb, ws, bias2d, qg, kg]
    if use_rope:
        in_specs += [pl.BlockSpec((rows, HEAD_DIM), lambda b, t: (t, 0))] * 2
        args += list(rope_tabs)
    aliases = {}
    if kv_into is not None:
        aliases = {len(args): 2, len(args) + 1: 3}
        in_specs += [pl.BlockSpec(memory_space=pl.ANY)] * 2
        args += list(kv_into)
    out_shape = [
        jax.ShapeDtypeStruct((bsz, seq, GM_WIDTH), BF16),
        jax.ShapeDtypeStruct((bsz, seq, ATTN_WIDTH), BF16),
        jax.ShapeDtypeStruct((bsz, kv_len, KV_W), BF16),
        jax.ShapeDtypeStruct((bsz, kv_len, KV_W), BF16),
    ]
    out_specs = [
        pl.BlockSpec((1, rows, GM_WIDTH), lambda b, t: (b, t, 0)),
        pl.BlockSpec((1, rows, ATTN_WIDTH), lambda b, t: (b, t, 0)),
        pl.BlockSpec((1, rows, KV_W), lambda b, t: (b, t + kv_blk0, 0)),
        pl.BlockSpec((1, rows, KV_W), lambda b, t: (b, t + kv_blk0, 0)),
    ]
    est = d * n * 2 + 2 * rows * d * 4 + 6 * rows * n * 4
    return pl.pallas_call(
        functools.partial(_ab_in_kernel, use_rope=use_rope, rows=rows),
        out_shape=out_shape,
        grid=(bsz, seq // rows),
        in_specs=in_specs,
        out_specs=out_specs,
        input_output_aliases=aliases,
        compiler_params=pltpu.CompilerParams(
            dimension_semantics=("arbitrary", "arbitrary"),
            vmem_limit_bytes=_vmem_limit(est)),
        name="ab_in_rope" if use_rope else "ab_in_ctx",
    )(*args)


def _attn_kernel(q_ref, k_ref, v_ref, o_ref, vt_ref, *, rows, blk):
    @pl.when(pl.program_id(2) == 0)
    def _():
        vt_ref[0:HEAD_DIM, :] = v_ref[0].astype(F32).T
        tail = lax.broadcasted_iota(jnp.int32, (ATTN_PAD, vt_ref.shape[1]), 0)
        vt_ref[HEAD_DIM:HEAD_DIM + ATTN_PAD, :] = jnp.where(tail == 0, 1.0, 0.0)

    k = k_ref[0]
    vt = vt_ref[...].astype(BF16)
    for h in range(N_Q_HEADS // N_KV_HEADS):
        l_ = slice(h * HEAD_DIM, (h + 1) * HEAD_DIM)
        for r0 in range(0, rows, blk):
            qt = q_ref[0, r0:r0 + blk, l_].astype(F32).T.astype(BF16)
            st = _dot(k, qt)
            p = jnp.exp2(st - jnp.max(st, axis=0, keepdims=True))
            ot = _dot(vt, p.astype(BF16))
            o = ot[0:HEAD_DIM] / ot[HEAD_DIM:HEAD_DIM + 1]
            o_ref[0, r0:r0 + blk, l_] = o.T.astype(BF16)


def _attention(q, k, v, k_row0, sk, rows, blk=512):
    bsz, sq, _ = q.shape
    assert k_row0 % sk == 0
    kb = k_row0 // sk
    gw = (N_Q_HEADS // N_KV_HEADS) * HEAD_DIM
    est = 4 * blk * sk * 12 + 4 * sk * HEAD_DIM * 2 + 4 * rows * gw * 2 + sk * HEAD_DIM * 4
    return pl.pallas_call(
        functools.partial(_attn_kernel, rows=rows, blk=min(blk, rows)),
        out_shape=jax.ShapeDtypeStruct((bsz, sq, ATTN_WIDTH), BF16),
        grid=(bsz, N_KV_HEADS, sq // rows),
        in_specs=[
            pl.BlockSpec((1, rows, gw), lambda b, h, t: (b, t, h)),
            pl.BlockSpec((1, sk, HEAD_DIM), lambda b, h, t: (b, kb, h)),
            pl.BlockSpec((1, sk, HEAD_DIM), lambda b, h, t: (b, kb, h)),
        ],
        out_specs=pl.BlockSpec((1, rows, gw), lambda b, h, t: (b, t, h)),
        scratch_shapes=[pltpu.VMEM((HEAD_DIM + ATTN_PAD, sk), F32)],
        compiler_params=pltpu.CompilerParams(
            dimension_semantics=("arbitrary", "arbitrary", "arbitrary"),
            vmem_limit_bytes=_vmem_limit(est)),
        name="attention",
    )(q, k, v)


def _out_ln_kernel(a1_ref, a2_ref, w_ref, x_ref, g_ref, lng_ref, lnb_ref, o_ref):
    half = a1_ref.shape[2]
    y = _dot(a1_ref[0], w_ref[0:half, :]) + _dot(a2_ref[0], w_ref[half:2 * half, :])
    z = DN_ALPHA * x_ref[0] + g_ref[0] * y
    o_ref[0] = _layer_norm(z, lng_ref[...], lnb_ref[...])


def _out_ln(a1, a2, w, x, gate, lng, lnb, rows):
    bsz, seq, d = x.shape
    half = a1.shape[2]
    est = 2 * half * d * 2 + 4 * rows * d * 4 + 4 * rows * half * 2 + 2 * rows * d * 4
    return pl.pallas_call(
        _out_ln_kernel,
        out_shape=jax.ShapeDtypeStruct((bsz, seq, d), F32),
        grid=(bsz, seq // rows),
        in_specs=[
            pl.BlockSpec((1, rows, half), lambda b, t: (b, t, 0)),
            pl.BlockSpec((1, rows, half), lambda b, t: (b, t, 0)),
            _resident((2 * half, d)),
            pl.BlockSpec((1, rows, d), lambda b, t: (b, t, 0)),
            _row_spec(d),
            _resident((1, d)), _resident((1, d)),
        ],
        out_specs=pl.BlockSpec((1, rows, d), lambda b, t: (b, t, 0)),
        compiler_params=pltpu.CompilerParams(
            dimension_semantics=("arbitrary", "arbitrary"),
            vmem_limit_bytes=_vmem_limit(est)),
        name="out_ln",
    )(a1, a2, w, x, gate, lng, lnb)


def _ffn_kernel(x_ref, xp_ref, xn_ref, sh_ref, sc_ref, g_ref, wu1_ref, wu2_ref, cw1_ref, cw2_ref,
                cb1_ref, cb2_ref, wd_ref, lng_ref, lnb_ref, o_ref, hm_ref, u_ref, gt_ref, *, rows):
    _fill_halo_tile(hm_ref, x_ref, xp_ref, xn_ref, sh_ref[0], sc_ref[0], rows)
    n_chunks = wu1_ref.shape[0]
    tiles = rows // V7X_SUBLANES
    sub = lax.broadcasted_iota(jnp.int32, (tiles, V7X_SUBLANES, FF_CHUNK), 1)

    def up_proj(c, slot):
        hm = hm_ref[...]
        u_ref[slot, 0] = _dot(hm, wu1_ref[c]).reshape(tiles + 2, V7X_SUBLANES, FF_CHUNK)
        u_ref[slot, 1] = _dot(hm, wu2_ref[c]).reshape(tiles + 2, V7X_SUBLANES, FF_CHUNK)

    def conv_gate(c, slot):
        def conv(half, cw_ref, cb_ref):
            u = u_ref[slot, half]
            dn = pltpu.roll(u, 1, 1)
            up = pltpu.roll(u, V7X_SUBLANES - 1, 1)
            prev = jnp.where(sub == 0, dn[0:tiles], dn[1:tiles + 1])
            nxt = jnp.where(sub == V7X_SUBLANES - 1, up[2:tiles + 2], up[1:tiles + 1])
            w = cw_ref[c]
            return cb_ref[c] + w[0:1] * prev + w[1:2] * u[1:tiles + 1] + w[2:3] * nxt

        a1 = conv(0, cw1_ref, cb1_ref)
        a2 = conv(1, cw2_ref, cb2_ref)
        gt_ref[c] = (a1 * _silu(a2)).reshape(rows, FF_CHUNK)

    up_proj(0, 0)

    def pair(k, carry):
        c = 2 * k
        up_proj(c + 1, 1)
        conv_gate(c, 0)
        up_proj(c + 2, 0)
        conv_gate(c + 1, 1)
        return carry

    lax.fori_loop(0, (n_chunks - 1) // 2, pair, 0)
    conv_gate(n_chunks - 1, 0)

    f = _dot(gt_ref[0].astype(BF16), wd_ref[0])
    for c in range(1, n_chunks):
        f = f + _dot(gt_ref[c].astype(BF16), wd_ref[c])
    z = DN_ALPHA * x_ref[0] + g_ref[0] * f
    o_ref[0] = _layer_norm(z, lng_ref[...], lnb_ref[...])


def _ffn(x, sh, sc, gate, wu1, wu2, cw1, cw2, cb1, cb2, wd, lng, lnb, rows):
    bsz, seq, d = x.shape
    nc = wu1.shape[0]
    assert nc % 2 == 1, "the pipeline peels one chunk and pairs the rest"
    est = (3 * nc * d * FF_CHUNK * 2 + 4 * rows * d * 4 + (rows + 2 * HALO) * d * 2
           + nc * rows * FF_CHUNK * 4 + 8 * (rows + 2 * HALO) * FF_CHUNK * 4 + 2 * rows * d * 4)
    return pl.pallas_call(
        functools.partial(_ffn_kernel, rows=rows),
        out_shape=jax.ShapeDtypeStruct((bsz, seq, d), F32),
        grid=(bsz, seq // rows),
        in_specs=_halo_specs(rows, seq, d) + [
            _row_spec(d), _row_spec(d), _row_spec(d),
            _resident((nc, d, FF_CHUNK)), _resident((nc, d, FF_CHUNK)),
            _resident((nc, CONV_W, FF_CHUNK)), _resident((nc, CONV_W, FF_CHUNK)),
            _resident((nc, 1, FF_CHUNK)), _resident((nc, 1, FF_CHUNK)),
            _resident((nc, FF_CHUNK, d)),
            _resident((1, d)), _resident((1, d)),
        ],
        out_specs=pl.BlockSpec((1, rows, d), lambda b, t: (b, t, 0)),
        scratch_shapes=[
            pltpu.VMEM((rows + 2 * HALO, d), BF16),
            pltpu.VMEM((2, 2, rows // HALO + 2, HALO, FF_CHUNK), F32),
            pltpu.VMEM((nc, rows, FF_CHUNK), F32),
        ],
        compiler_params=pltpu.CompilerParams(
            dimension_semantics=("arbitrary", "arbitrary"),
            vmem_limit_bytes=_vmem_limit(est)),
        name="conv_ffn",
    )(x, x, x, sh, sc, gate, wu1, wu2, cw1, cw2, cb1, cb2, wd, lng, lnb)


def _cd_in_kernel(*refs, with_fz, rows):
    x_ref, xp_ref, xn_ref, sh_ref, sc_ref = refs[:5]
    if with_fz:
        wfz_ref, wx_ref, wdt_ref, cw_ref, cb_ref, dtb_ref = refs[5:11]
        f_ref, z_ref, xs_ref, b_ref, c_ref, dt_ref, hm_ref = refs[11:]
    else:
        wx_ref, wdt_ref, cw_ref, cb_ref, dtb_ref = refs[5:10]
        xs_ref, b_ref, c_ref, dt_ref, hm_ref = refs[10:]
    _fill_halo_tile(hm_ref, x_ref, xp_ref, xn_ref, sh_ref[0], sc_ref[0], rows)
    hc = hm_ref[HALO:HALO + rows, :]
    if with_fz:
        pfz = _dot(hc, wfz_ref[...])
        f_ref[0] = pfz[:, 0:FN_WIDTH].astype(BF16)
        z_ref[0] = pfz[:, FN_WIDTH:SSM_OFF]
    a = _conv3(_dot(hm_ref[...], wx_ref[...]), cw_ref[...], cb_ref[...], rows)
    s = _silu(a)
    xs_ref[0] = s[:, 0:SSD_WIDTH]
    b_ref[0] = s[:, SSD_WIDTH:SSD_WIDTH + SSD_GN].astype(BF16)
    c_ref[0] = s[:, SSD_WIDTH + SSD_GN:SSD_CONV_DIM].astype(BF16)
    dt_ref[0] = _softplus(_dot(hc, wdt_ref[...]) + dtb_ref[...])


def _cd_in(x, sh, sc, wfz, wx, wdt, cw, cb, dtb, rows):
    bsz, seq, d = x.shape
    with_fz = wfz is not None
    in_specs = _halo_specs(rows, seq, d) + [_row_spec(d), _row_spec(d)]
    args = [x, x, x, sh, sc]
    out_shape, out_specs = [], []
    if with_fz:
        in_specs.append(_resident((d, SSM_OFF)))
        args.append(wfz)
        out_shape += [jax.ShapeDtypeStruct((bsz, seq, FN_WIDTH), BF16),
                      jax.ShapeDtypeStruct((bsz, seq, SSD_WIDTH), F32)]
        out_specs += [pl.BlockSpec((1, rows, FN_WIDTH), lambda b, t: (b, t, 0)),
                      pl.BlockSpec((1, rows, SSD_WIDTH), lambda b, t: (b, t, 0))]
    in_specs += [_resident((d, SSD_CONV_DIM)), _resident((d, V7X_LANES)),
                 _resident((CONV_W, SSD_CONV_DIM)), _resident((1, SSD_CONV_DIM)),
                 _resident((1, V7X_LANES))]
    args += [wx, wdt, cw, cb, dtb]
    out_shape += [jax.ShapeDtypeStruct((bsz, seq, SSD_WIDTH), F32),
                  jax.ShapeDtypeStruct((bsz, seq, SSD_GN), BF16),
                  jax.ShapeDtypeStruct((bsz, seq, SSD_GN), BF16),
                  jax.ShapeDtypeStruct((bsz, seq, V7X_LANES), F32)]
    out_specs += [pl.BlockSpec((1, rows, SSD_WIDTH), lambda b, t: (b, t, 0)),
                  pl.BlockSpec((1, rows, SSD_GN), lambda b, t: (b, t, 0)),
                  pl.BlockSpec((1, rows, SSD_GN), lambda b, t: (b, t, 0)),
                  pl.BlockSpec((1, rows, V7X_LANES), lambda b, t: (b, t, 0))]
    est = d * 2200 * 2 + 2 * rows * d * 4 + 8 * (rows + 2 * HALO) * SSD_CONV_DIM * 4
    return pl.pallas_call(
        functools.partial(_cd_in_kernel, with_fz=with_fz, rows=rows),
        out_shape=out_shape,
        grid=(bsz, seq // rows),
        in_specs=in_specs,
        out_specs=out_specs,
        scratch_shapes=[pltpu.VMEM((rows + 2 * HALO, d), BF16)],
        compiler_params=pltpu.CompilerParams(
            dimension_semantics=("arbitrary", "arbitrary"),
            vmem_limit_bytes=_vmem_limit(est)),
        name="cd_in" if with_fz else "cd_in_ctx",
    )(*args)


def _cumsum_rows(z, reverse):
    n = z.shape[0]
    idx = lax.broadcasted_iota(jnp.int32, z.shape, 0)
    k = 1
    while k < n:
        if reverse:
            z = z + jnp.where(idx < n - k, pltpu.roll(z, n - k, 0), 0.0)
        else:
            z = z + jnp.where(idx >= k, pltpu.roll(z, k, 0), 0.0)
        k *= 2
    return z


def _lane_expand(row, lane0, width):
    lane = lax.broadcasted_iota(jnp.int32, (1, SSD_HPG * width), 1)
    out = jnp.zeros((1, SSD_HPG * width), F32)
    for j in range(SSD_HPG):
        val = jnp.broadcast_to(row[:, lane0 + j:lane0 + j + 1], (1, SSD_HPG * width))
        out = jnp.where(lane // width == j, val, out)
    return out


def _ssd_chunk(x, bm, cm, dt, a_row, h_ref, lane0, reverse, need_y):
    q = x.shape[0]
    gw = SSD_HPG * SSD_HEADDIM
    acum = _cumsum_rows(dt * a_row, reverse)
    total = acum[0:1, :] if reverse else acum[q - 1:q, :]
    acum_t = acum.T
    dt_t = dt.T
    total_t = jnp.broadcast_to(acum_t[:, 0:1] if reverse else acum_t[:, q - 1:q], acum_t.shape)
    wgt_t = dt_t * jnp.exp(total_t - acum_t)
    lane_g = lax.broadcasted_iota(jnp.int32, (q, gw), 1) // SSD_HEADDIM
    lane_gh = lax.broadcasted_iota(jnp.int32, (SSD_STATE, gw), 1) // SSD_HEADDIM
    if need_y:
        srow_t = acum_t - jnp.log(dt_t)
        ti = lax.broadcasted_iota(jnp.int32, (q, q), 0)
        si = lax.broadcasted_iota(jnp.int32, (q, q), 1)
        mask = (si >= ti) if reverse else (si <= ti)
    ys = []
    for g in range(SSD_GROUPS):
        bg = bm[:, g * SSD_STATE:(g + 1) * SSD_STATE]
        xgb = x[:, g * gw:(g + 1) * gw].astype(BF16)
        x_heads = [jnp.where(lane_g == j, xgb, jnp.zeros_like(xgb)) for j in range(SSD_HPG)]
        hg = h_ref[g]
        if need_y:
            cg = cm[:, g * SSD_STATE:(g + 1) * SSD_STATE]
            cb = _dot_nt(cg, bg)
            cgf = cg.astype(F32)
            hgb = hg.astype(BF16)
            lhs, rhs = [], []
            for j in range(SSD_HPG):
                jl = lane0 + g * SSD_HPG + j
                col = jnp.broadcast_to(acum[:, jl:jl + 1], (q, q))
                row = jnp.broadcast_to(srow_t[jl:jl + 1, :], (q, q))
                lhs.append((cb * jnp.exp(jnp.where(mask, col - row, -jnp.inf))).astype(BF16))
                lhs.append((cgf * jnp.exp(col[:, 0:SSD_STATE])).astype(BF16))
                rhs.append(x_heads[j])
                rhs.append(jnp.where(lane_gh == j, hgb, jnp.zeros_like(hgb)))
            ys.append(_dot(jnp.concatenate(lhs, axis=1), jnp.concatenate(rhs, axis=0)))
        bgt = bg.astype(F32).T
        wb = [(bgt * jnp.broadcast_to(wgt_t[lane0 + g * SSD_HPG + j:lane0 + g * SSD_HPG + j + 1, :],
                                      (SSD_STATE, q))).astype(BF16) for j in range(SSD_HPG)]
        st = _dot(jnp.concatenate(wb, axis=1), jnp.concatenate(x_heads, axis=0))
        h_ref[g] = hg * jnp.exp(_lane_expand(total, lane0 + g * SSD_HPG, SSD_HEADDIM)) + st
    if need_y:
        return jnp.concatenate(ys, axis=1)
    return None


def _ssd_kernel(xc_ref, bc_ref, dtc_ref, xl_ref, bl_ref, cl_ref, dtl_ref, z_ref, alog_ref,
                dskip_ref, ng_ref, o_ref, y_ref, h_ref):
    a_row = -jnp.exp(alog_ref[...])
    nc_ctx = xc_ref.shape[1] // SSD_CHUNK
    nc_lat = xl_ref.shape[1] // SSD_CHUNK

    def rows_of(c):
        return pl.ds(pl.multiple_of(c * SSD_CHUNK, SSD_CHUNK), SSD_CHUNK)

    def ctx_step(c, direction):
        r = rows_of(c)
        _ssd_chunk(xc_ref[0, r, :], bc_ref[0, r, :], None, dtc_ref[0, r, :], a_row,
                   h_ref.at[direction], direction * SSD_HEADS, direction == 1, False)

    def lat_step(c, direction):
        r = rows_of(c)
        y_ref[direction, r, :] = _ssd_chunk(
            xl_ref[0, r, :], bl_ref[0, r, :], cl_ref[0, r, :], dtl_ref[0, r, :], a_row,
            h_ref.at[direction], direction * SSD_HEADS, direction == 1, True)

    h_ref[...] = jnp.zeros_like(h_ref)

    def ctx_pair(i, carry):
        ctx_step(i, 0)
        ctx_step(nc_ctx - 1 - i, 1)
        return carry

    lax.fori_loop(0, nc_ctx, ctx_pair, 0)

    def lat_pair(i, carry):
        lat_step(i, 0)
        lat_step(nc_lat - 1 - i, 1)
        return carry

    lax.fori_loop(0, nc_lat, lat_pair, 0)

    def finish(c, carry):
        r = rows_of(c)
        zz = z_ref[0, r, :]
        yt = (y_ref[0, r, :] + y_ref[1, r, :] + xl_ref[0, r, :] * dskip_ref[...]) * _silu(zz)
        o_ref[0, r, :] = _rms_norm(yt, ng_ref[...]).astype(BF16)
        return carry

    lax.fori_loop(0, nc_lat, finish, 0)


def _ssd(xs_c, b_c, dt_c, xs_l, b_l, c_l, dt_l, z, alog_row, dskip_row, ng_row):
    bsz, sl, w = xs_l.shape
    sc = xs_c.shape[1]
    gw = SSD_HPG * SSD_HEADDIM

    def full(s, n):
        return pl.BlockSpec((1, s, n), lambda b: (b, 0, 0))

    est = (2 * (sl + sc) * (w * 4 + 2 * SSD_GN * 2 + V7X_LANES * 4) + 2 * sl * w * 4
           + 2 * sl * w * 2 + 2 * sl * w * 4)
    return pl.pallas_call(
        _ssd_kernel,
        out_shape=jax.ShapeDtypeStruct((bsz, sl, w), BF16),
        grid=(bsz,),
        in_specs=[
            full(sc, w), full(sc, SSD_GN), full(sc, V7X_LANES),
            full(sl, w), full(sl, SSD_GN), full(sl, SSD_GN), full(sl, V7X_LANES),
            full(sl, w),
            _resident((1, V7X_LANES)), _resident((1, w)), _resident((1, w)),
        ],
        out_specs=full(sl, w),
        scratch_shapes=[
            pltpu.VMEM((2, sl, w), F32),
            pltpu.VMEM((2, SSD_GROUPS, SSD_STATE, gw), F32),
        ],
        compiler_params=pltpu.CompilerParams(
            dimension_semantics=("arbitrary",),
            vmem_limit_bytes=_vmem_limit(est)),
        name="ssd_scan",
    )(xs_c, b_c, dt_c, xs_l, b_l, c_l, dt_l, z, alog_row, dskip_row, ng_row)


def _fourier_kernel(f_ref, cc_ref, sc_ref, w2_ref, o_ref, y_ref, *, rows):
    seq = f_ref.shape[1]
    for g in range(FN_GROUPS):
        l = slice(g * FN_GDIM, (g + 1) * FN_GDIM)
        fg = f_ref[0, :, l]
        y_ref[0:seq, l] = _dot(fg, cc_ref[...]).astype(BF16)
        y_ref[seq:2 * seq, l] = _dot(fg, sc_ref[...]).astype(BF16)
    norm = 1.0 / math.sqrt(seq * FN_GDIM)

    def tile(i, carry):
        r = pl.ds(pl.multiple_of(i * rows, rows), rows)
        o_ref[0, r, :] = (_dot(w2_ref[r, :], y_ref[...]) * norm).astype(BF16)
        return carry

    lax.fori_loop(0, seq // rows, tile, 0)


def _fourier(f, cc, sc, w2, rows):
    bsz, seq, w = f.shape
    est = seq * 2 * seq * 2 + 4 * seq * w * 2 + 2 * seq * w * 2 + 4 * rows * 2 * seq * 2
    return pl.pallas_call(
        functools.partial(_fourier_kernel, rows=rows),
        out_shape=jax.ShapeDtypeStruct((bsz, seq, w), BF16),
        grid=(bsz,),
        in_specs=[
            pl.BlockSpec((1, seq, w), lambda b: (b, 0, 0)),
            _resident((FN_GDIM, FN_GDIM)), _resident((FN_GDIM, FN_GDIM)),
            _resident((seq, 2 * seq)),
        ],
        out_specs=pl.BlockSpec((1, seq, w), lambda b: (b, 0, 0)),
        scratch_shapes=[pltpu.VMEM((2 * seq, w), BF16)],
        compiler_params=pltpu.CompilerParams(
            dimension_semantics=("arbitrary",),
            vmem_limit_bytes=_vmem_limit(est)),
        name="fourier_mix",
    )(f, cc, sc, w2)


def _rope_tables(seq):
    rows = seq // GRID_W
    row = jnp.repeat(jnp.arange(rows, dtype=F32), GRID_W)
    col = jnp.broadcast_to(jnp.arange(GRID_W, dtype=F32), (rows, GRID_W)).reshape(-1)
    n_freq = HEAD_DIM // 4
    inv = ROPE_THETA ** (-jnp.arange(n_freq, dtype=F32) / n_freq)
    ang = jnp.stack([row, col], axis=-1)[:, :, None] * inv
    cos, sin = jnp.cos(ang), jnp.sin(ang)
    cos_t = jnp.concatenate([cos[:, 0], cos[:, 1], cos[:, 0], cos[:, 1]], axis=-1)
    sin_t = jnp.concatenate([-sin[:, 0], -sin[:, 1], sin[:, 0], sin[:, 1]], axis=-1)
    return cos_t, sin_t


def _rope_head_order(m):
    lead = m.shape[:-1]
    nh = m.shape[-1] // HEAD_DIM
    m = m.reshape(lead + (nh, 2, 2, HEAD_DIM // 4))
    return jnp.swapaxes(m, -3, -2).reshape(lead + (nh * HEAD_DIM,))


def _dft_tables(seq):
    def tw(n):
        i = np.arange(n, dtype=np.int64)
        ang = ((i[:, None] * i[None, :]) % n).astype(np.float64) * (2.0 * math.pi / n)
        return np.cos(ang), np.sin(ang)

    cc, sc = tw(FN_GDIM)
    cl, sl = tw(seq)
    w2 = np.concatenate([cl, -sl], axis=1).astype(np.float32)
    return (jnp.asarray(cc.astype(np.float32), BF16), jnp.asarray(sc.astype(np.float32), BF16),
            jnp.asarray(w2, BF16))


def _ffn_params(w_up, conv_w, conv_b, w_down):
    d = w_up.shape[0]
    nc = D_FF // FF_CHUNK

    def cols(m, lo):
        r = m.shape[0]
        return m[:, lo:lo + D_FF].reshape(r, nc, FF_CHUNK).transpose(1, 0, 2)

    wu = w_up.astype(BF16)
    cb = conv_b.reshape(1, 2 * D_FF)
    return (cols(wu, 0), cols(wu, D_FF), cols(conv_w, 0), cols(conv_w, D_FF),
            cols(cb, 0), cols(cb, D_FF), w_down.astype(BF16).reshape(nc, FF_CHUNK, d))


def kernel(x, c, ctx, c_ctx, mod_w, mod_b, ln1_g, ln1_b, ln2_g, ln2_b, ffn_up, ffn_conv_w, ffn_conv_b,
           ffn_down, ab_w_in, ab_w_out, gm_ln_g, gm_ln_b, gm_ws, gm_bs, q_norm_g, k_norm_g, cd_w_in,
           cd_w_out, ssd_conv_w, ssd_conv_b, ssd_dt_bias, ssd_a_log, ssd_d, ssd_norm_g):
    bsz, seq, d = x.shape
    ctx_len = ctx.shape[1]
    lat_rows = min(1024, seq)
    out_rows = min(2048, seq)
    dft_rows = min(512, seq)
    ctx_rows = min(256, ctx_len)

    pad = (-(bsz + 1)) % V7X_SUBLANES
    cc = jnp.concatenate([c, c_ctx[None], jnp.zeros((pad, d), F32)], axis=0)
    mod = _modulation(cc, mod_w, mod_b)
    rope_tabs = _rope_tables(seq)

    def row(v):
        return v.reshape(1, -1)

    xl, xc = x, ctx
    for i in range(DEPTH):
        last = i == DEPTH - 1
        j = i // 2
        ml = mod[i, :bsz].reshape(bsz, 6, 1, d)
        mc = jnp.broadcast_to(mod[i, bsz].reshape(1, 6, 1, d), (bsz, 6, 1, d))
        sh1, sc1, g1, sh2, sc2, g2 = (ml[:, k] for k in range(6))
        csh1, csc1, cg1, csh2, csc2, cg2 = (mc[:, k] for k in range(6))
        ln1 = (row(ln1_g[i]), row(ln1_b[i]))
        ln2 = (row(ln2_g[i]), row(ln2_b[i]))
        ffn_p = _ffn_params(ffn_up[i], ffn_conv_w[i], ffn_conv_b[i], ffn_down[i])

        if i % 2 == 0:
            w_in = ab_w_in[j].astype(BF16)
            qk_lo, qk_hi = 2 * GM_WIDTH, KV_OFF + KV_W
            w_in = jnp.concatenate([w_in[:, :qk_lo], _rope_head_order(w_in[:, qk_lo:qk_hi]), w_in[:, qk_hi:]],
                                   axis=1)
            w_out = ab_w_out[j].astype(BF16)
            gm = (row(gm_ln_g[j]), row(gm_ln_b[j]), gm_ws[j].astype(BF16),
                  jnp.repeat(gm_bs[j].T, GM_WIDTH // GM_GROUPS, axis=1),
                  row(_rope_head_order(q_norm_g[j])), row(_rope_head_order(k_norm_g[j])))
            kv_len = seq + ctx_len
            a_c, q_c, k_all, v_all = _ab_in(xc, csh1, csc1, w_in, *gm, None, ctx_rows, kv_len, seq)
            a_l, q_l, k_all, v_all = _ab_in(xl, sh1, sc1, w_in, *gm, rope_tabs, lat_rows, kv_len, 0,
                                            kv_into=(k_all, v_all))
            mix_l = (a_l, _attention(q_l, k_all, v_all, 0, kv_len, lat_rows))
            if not last:
                mix_c = (a_c, _attention(q_c, k_all, v_all, seq, ctx_len, ctx_rows))
        else:
            w_in = cd_w_in[j].astype(BF16)
            w_out = cd_w_out[j].astype(BF16)
            wfz = w_in[:, :SSM_OFF]
            wx = w_in[:, SSM_OFF:SSM_OFF + SSD_CONV_DIM]
            ndt = 2 * SSD_HEADS
            wdt = jnp.pad(w_in[:, SSM_OFF + SSD_CONV_DIM:], ((0, 0), (0, V7X_LANES - ndt)))
            dtb = jnp.pad(ssd_dt_bias[j].reshape(1, ndt), ((0, 0), (0, V7X_LANES - ndt)))
            alog = jnp.pad(ssd_a_log[j].reshape(1, ndt), ((0, 0), (0, V7X_LANES - ndt)))
            cw, cb = ssd_conv_w[j], row(ssd_conv_b[j])
            dskip = row(jnp.repeat(ssd_d[j], SSD_HEADDIM))
            f_l, z_l, xs_l, b_l, c_l, dt_l = _cd_in(xl, sh1, sc1, wfz, wx, wdt, cw, cb, dtb, lat_rows)
            if last:
                xs_c, b_c, _, dt_c = _cd_in(xc, csh1, csc1, None, wx, wdt, cw, cb, dtb, ctx_rows)
            else:
                raise NotImplementedError("odd layers with a context output are not part of this block")
            s_l = _ssd(xs_c, b_c, dt_c, xs_l, b_l, c_l, dt_l, z_l, alog, dskip, row(ssd_norm_g[j]))
            mix_l = (_fourier(f_l, *_dft_tables(seq), dft_rows), s_l)

        xl = _out_ln(*mix_l, w_out, xl, g1, *ln1, out_rows)
        xl = _ffn(xl, sh2, sc2, g2, *ffn_p, *ln2, lat_rows)
        if not last:
            xc = _out_ln(*mix_c, w_out, xc, cg1, *ln1, ctx_rows)
            xc = _ffn(xc, csh2, csc2, cg2, *ffn_p, *ln2, ctx_rows)
    return xl
```

```python
import functools
import math

import jax
import jax.numpy as jnp
import numpy as np
from jax import lax
from jax.experimental import pallas as pl
from jax.experimental.pallas import tpu as pltpu

D_MODEL = 1024
DEPTH = 2
GRID_W = 64
CHUNK = 128
GM_GROUPS = 4
GM_WIDTH = 512
HEAD_DIM = 128
N_Q_HEADS = 4
N_KV_HEADS = 2
ATTN_WIDTH = 512
KV_W = 256
ROPE_THETA = 10000.0
FN_GROUPS = 4
FN_GDIM = 128
FN_WIDTH = 512
SSD_HEADDIM = 64
SSD_HEADS = 8
SSD_GROUPS = 2
SSD_HPG = 4
SSD_STATE = 128
SSD_WIDTH = 512
SSD_GN = 256
SSD_CONV_DIM = 1024
D_FF = 2816
CONV_W = 3
KV_OFF = 2 * GM_WIDTH + ATTN_WIDTH
SSM_OFF = FN_WIDTH + SSD_WIDTH
DN_ALPHA = (2 * DEPTH) ** 0.25
EPS = 1e-6

V7X_LANES = 128
V7X_SUBLANES = 8
V7X_VMEM_BYTES = 64 * 1024 * 1024
AB_SUB = 256
FF_CHUNK = 256
SSD_CHUNK = 256

BF16 = jnp.bfloat16
F32 = jnp.float32
HALO = V7X_SUBLANES


def _vmem_limit(nbytes):
    return int(min(nbytes * 3 // 2 + (8 << 20), V7X_VMEM_BYTES - (3 << 20)))


def _resident(shape):
    nd = len(shape)
    return pl.BlockSpec(shape, lambda *_: (0,) * nd, pipeline_mode=pl.Buffered(1))


def _dot(a, b):
    return jnp.dot(a, b, preferred_element_type=F32)


def _dot_nt(a, b):
    return lax.dot_general(a, b, (((1,), (1,)), ((), ())), preferred_element_type=F32)


def _layer_norm(z, g, b):
    mu = jnp.mean(z, axis=-1, keepdims=True)
    zc = z - mu
    var = jnp.mean(zc * zc, axis=-1, keepdims=True)
    return zc * lax.rsqrt(var + EPS) * g + b


def _rms_norm(z, g):
    return z * lax.rsqrt(jnp.mean(z * z, axis=-1, keepdims=True) + EPS) * g


def _softplus(z):
    return jnp.maximum(z, 0.0) + jnp.log1p(jnp.exp(-jnp.abs(z)))


def _gelu_tanh(z):
    c1 = math.sqrt(2.0 / math.pi)
    h = 0.5 * z
    return h + h * jnp.tanh(z * (c1 + (c1 * 0.044715) * (z * z)))


def _silu(z):
    h = 0.5 * z
    return h + h * jnp.tanh(h)


def _conv3(u, w, b, rows):
    tiles, c = rows // V7X_SUBLANES, u.shape[1]
    u3 = u.reshape(tiles + 2, V7X_SUBLANES, c)
    sub = lax.broadcasted_iota(jnp.int32, (tiles, V7X_SUBLANES, c), 1)
    dn = pltpu.roll(u3, 1, 1)
    up = pltpu.roll(u3, V7X_SUBLANES - 1, 1)
    prev = jnp.where(sub == 0, dn[0:tiles], dn[1:tiles + 1])
    nxt = jnp.where(sub == V7X_SUBLANES - 1, up[2:tiles + 2], up[1:tiles + 1])
    return (b + w[0:1] * prev + w[1:2] * u3[1:tiles + 1] + w[2:3] * nxt).reshape(rows, c)


def _fill_halo_tile(hm_ref, x_ref, xp_ref, xn_ref, sh, sc, rows):
    t = pl.program_id(1)
    nt = pl.num_programs(1)
    keep_prev = jnp.where(t > 0, 1.0, 0.0)
    keep_next = jnp.where(t < nt - 1, 1.0, 0.0)
    hm_ref[HALO:HALO + rows, :] = (x_ref[0] * (1.0 + sc) + sh).astype(BF16)
    hm_ref[0:HALO, :] = ((xp_ref[0] * (1.0 + sc) + sh) * keep_prev).astype(BF16)
    hm_ref[HALO + rows:2 * HALO + rows, :] = ((xn_ref[0] * (1.0 + sc) + sh) * keep_next).astype(BF16)


def _halo_specs(rows, seq, d):
    nb = rows // HALO
    last = seq // HALO - 1
    return [
        pl.BlockSpec((1, rows, d), lambda b, t: (b, t, 0)),
        pl.BlockSpec((1, HALO, d), lambda b, t: (b, jnp.maximum(t * nb - 1, 0), 0)),
        pl.BlockSpec((1, HALO, d), lambda b, t: (b, jnp.minimum((t + 1) * nb, last), 0)),
    ]


def _row_spec(d):
    return pl.BlockSpec((1, 1, d), lambda b, t: (b, 0, 0))


def _mod_kernel(c_ref, w_ref, b_ref, o_ref):
    c = c_ref[...]
    act = c * jax.nn.sigmoid(c)
    o_ref[0] = jnp.dot(act, w_ref[0], preferred_element_type=F32,
                       precision=lax.Precision.HIGHEST) + b_ref[0]


def _modulation(cc, mod_w, mod_b):
    rows, d = cc.shape
    depth, _, n = mod_w.shape
    tn = 2048
    return pl.pallas_call(
        _mod_kernel,
        out_shape=jax.ShapeDtypeStruct((depth, rows, n), F32),
        grid=(depth, n // tn),
        in_specs=[
            pl.BlockSpec((rows, d), lambda i, j: (0, 0)),
            pl.BlockSpec((1, d, tn), lambda i, j: (i, 0, j)),
            pl.BlockSpec((1, 1, tn), lambda i, j: (i, 0, j)),
        ],
        out_specs=pl.BlockSpec((1, rows, tn), lambda i, j: (i, 0, j)),
        compiler_params=pltpu.CompilerParams(
            dimension_semantics=("arbitrary", "arbitrary"),
            vmem_limit_bytes=_vmem_limit(2 * d * tn * 4)),
        name="modulation",
    )(cc, mod_w, mod_b.reshape(depth, 1, n))


def _ab_in_kernel(*refs, use_rope, rows):
    (x_ref, sh_ref, sc_ref, w_ref, lng_ref, lnb_ref, ws_ref, bias_ref, qg_ref, kg_ref) = refs[:10]
    if use_rope:
        cos_ref, sin_ref = refs[10:12]
    a_ref, q_ref, k_ref, v_ref = refs[-4:]

    gd = GM_WIDTH // GM_GROUPS
    scale = HEAD_DIM ** -0.5 * math.log2(math.e)

    def slab(i, carry):
        r0 = pl.multiple_of(i * AB_SUB, AB_SUB)
        rs = pl.ds(r0, AB_SUB)
        h = (x_ref[0, rs, :] * (1.0 + sc_ref[0]) + sh_ref[0]).astype(BF16)

        u = _gelu_tanh(_dot(h, w_ref[:, 0:GM_WIDTH]))
        vn = _layer_norm(_gelu_tanh(_dot(h, w_ref[:, GM_WIDTH:2 * GM_WIDTH])), lng_ref[...], lnb_ref[...])
        vb = vn.astype(BF16)
        for c in range(AB_SUB // CHUNK):
            r = slice(c * CHUNK, (c + 1) * CHUNK)
            for g in range(GM_GROUPS):
                l = slice(g * gd, (g + 1) * gd)
                mixed = _dot(ws_ref[g], vb[r, l]) + bias_ref[:, l]
                a_ref[0, pl.ds(r0 + c * CHUNK, CHUNK), l] = (u[r, l] * mixed).astype(BF16)

        def rope(z):
            if not use_rope:
                return z
            return z * cos_ref[rs, :] + pltpu.roll(z, HEAD_DIM // 2, 1) * sin_ref[rs, :]

        pq = _dot(h, w_ref[:, 2 * GM_WIDTH:KV_OFF])
        for hh in range(N_Q_HEADS):
            l = slice(hh * HEAD_DIM, (hh + 1) * HEAD_DIM)
            q_ref[0, rs, l] = (rope(_rms_norm(pq[:, l], qg_ref[...])) * scale).astype(BF16)
        pk = _dot(h, w_ref[:, KV_OFF:KV_OFF + KV_W])
        for hh in range(N_KV_HEADS):
            l = slice(hh * HEAD_DIM, (hh + 1) * HEAD_DIM)
            k_ref[0, rs, l] = rope(_rms_norm(pk[:, l], kg_ref[...])).astype(BF16)
        v_ref[0, rs, :] = _dot(h, w_ref[:, KV_OFF + KV_W:KV_OFF + 2 * KV_W]).astype(BF16)
        return carry

    lax.fori_loop(0, rows // AB_SUB, slab, 0)


def _ab_in(x, sh, sc, w, lng, lnb, ws, bias2d, qg, kg, rope_tabs, rows, kv_len, kv_row0, kv_into=None):
    bsz, seq, d = x.shape
    assert kv_row0 % rows == 0
    kv_blk0 = kv_row0 // rows
    n = w.shape[1]
    use_rope = rope_tabs is not None
    in_specs = [
        pl.BlockSpec((1, rows, d), lambda b, t: (b, t, 0)),
        _row_spec(d), _row_spec(d),
        _resident((d, n)),
        _resident((1, GM_WIDTH)), _resident((1, GM_WIDTH)),
        _resident((GM_GROUPS, CHUNK, CHUNK)),
        _resident((CHUNK, GM_WIDTH)),
        _resident((1, HEAD_DIM)), _resident((1, HEAD_DIM)),
    ]
    args = [x, sh, sc, w, lng, lnb, ws, bias2d, qg, kg]
    if use_rope:
        in_specs += [pl.BlockSpec((rows, HEAD_DIM), lambda b, t: (t, 0))] * 2
        args += list(rope_tabs)
    aliases = {}
    if kv_into is not None:
        aliases = {len(args): 2, len(args) + 1: 3}
        in_specs += [pl.BlockSpec(memory_space=pl.ANY)] * 2
        args += list(kv_into)
    out_shape = [
        jax.ShapeDtypeStruct((bsz, seq, GM_WIDTH), BF16),
        jax.ShapeDtypeStruct((bsz, seq, ATTN_WIDTH), BF16),
        jax.ShapeDtypeStruct((bsz, kv_len, KV_W), BF16),
        jax.ShapeDtypeStruct((bsz, kv_len, KV_W), BF16),
    ]
    out_specs = [
        pl.BlockSpec((1, rows, GM_WIDTH), lambda b, t: (b, t, 0)),
        pl.BlockSpec((1, rows, ATTN_WIDTH), lambda b, t: (b, t, 0)),
        pl.BlockSpec((1, rows, KV_W), lambda b, t: (b, t + kv_blk0, 0)),
        pl.BlockSpec((1, rows, KV_W), lambda b, t: (b, t + kv_blk0, 0)),
    ]
    est = d * n * 2 + 2 * rows * d * 4 + 6 * rows * n * 4
    return pl.pallas_call(
        functools.partial(_ab_in_kernel, use_rope=use_rope, rows=rows),
        out_shape=out_shape,
        grid=(bsz, seq // rows),
        in_specs=in_specs,
        out_specs=out_specs,
        input_output_aliases=aliases,
        compiler_params=pltpu.CompilerParams(
            dimension_semantics=("arbitrary", "arbitrary"),
            vmem_limit_bytes=_vmem_limit(est)),
        name="ab_in_rope" if use_rope else "ab_in_ctx",
    )(*args)


def _attn_kernel(q_ref, k_ref, v_ref, o_ref, vt_ref, *, rows, blk):
    @pl.when(pl.program_id(2) == 0)
    def _():
        vt_ref[...] = v_ref[0].astype(F32).T

    k = k_ref[0]
    vt = vt_ref[...].astype(BF16)
    for h in range(N_Q_HEADS // N_KV_HEADS):
        l_ = slice(h * HEAD_DIM, (h + 1) * HEAD_DIM)
        for r0 in range(0, rows, blk):
            qt = q_ref[0, r0:r0 + blk, l_].astype(F32).T.astype(BF16)
            st = _dot(k, qt)
            m = jnp.max(st, axis=0, keepdims=True)
            p = jnp.exp2(st - m)
            l = jnp.sum(p, axis=0, keepdims=True)
            ot = _dot(vt, p.astype(BF16)) / l
            o_ref[0, r0:r0 + blk, l_] = ot.T.astype(BF16)


def _attention(q, k, v, k_row0, sk, rows, blk=512):
    bsz, sq, _ = q.shape
    assert k_row0 % sk == 0
    kb = k_row0 // sk
    gw = (N_Q_HEADS // N_KV_HEADS) * HEAD_DIM
    est = 4 * blk * sk * 12 + 4 * sk * HEAD_DIM * 2 + 4 * rows * gw * 2 + sk * HEAD_DIM * 4
    return pl.pallas_call(
        functools.partial(_attn_kernel, rows=rows, blk=min(blk, rows)),
        out_shape=jax.ShapeDtypeStruct((bsz, sq, ATTN_WIDTH), BF16),
        grid=(bsz, N_KV_HEADS, sq // rows),
        in_specs=[
            pl.BlockSpec((1, rows, gw), lambda b, h, t: (b, t, h)),
            pl.BlockSpec((1, sk, HEAD_DIM), lambda b, h, t: (b, kb, h)),
            pl.BlockSpec((1, sk, HEAD_DIM), lambda b, h, t: (b, kb, h)),
        ],
        out_specs=pl.BlockSpec((1, rows, gw), lambda b, h, t: (b, t, h)),
        scratch_shapes=[pltpu.VMEM((HEAD_DIM, sk), F32)],
        compiler_params=pltpu.CompilerParams(
            dimension_semantics=("arbitrary", "arbitrary", "arbitrary"),
            vmem_limit_bytes=_vmem_limit(est)),
        name="attention",
    )(q, k, v)


def _out_ln_kernel(a1_ref, a2_ref, w_ref, x_ref, g_ref, lng_ref, lnb_ref, o_ref):
    half = a1_ref.shape[2]
    y = _dot(a1_ref[0], w_ref[0:half, :]) + _dot(a2_ref[0], w_ref[half:2 * half, :])
    z = DN_ALPHA * x_ref[0] + g_ref[0] * y
    o_ref[0] = _layer_norm(z, lng_ref[...], lnb_ref[...])


def _out_ln(a1, a2, w, x, gate, lng, lnb, rows):
    bsz, seq, d = x.shape
    half = a1.shape[2]
    est = 2 * half * d * 2 + 4 * rows * d * 4 + 4 * rows * half * 2 + 2 * rows * d * 4
    return pl.pallas_call(
        _out_ln_kernel,
        out_shape=jax.ShapeDtypeStruct((bsz, seq, d), F32),
        grid=(bsz, seq // rows),
        in_specs=[
            pl.BlockSpec((1, rows, half), lambda b, t: (b, t, 0)),
            pl.BlockSpec((1, rows, half), lambda b, t: (b, t, 0)),
            _resident((2 * half, d)),
            pl.BlockSpec((1, rows, d), lambda b, t: (b, t, 0)),
            _row_spec(d),
            _resident((1, d)), _resident((1, d)),
        ],
        out_specs=pl.BlockSpec((1, rows, d), lambda b, t: (b, t, 0)),
        compiler_params=pltpu.CompilerParams(
            dimension_semantics=("arbitrary", "arbitrary"),
            vmem_limit_bytes=_vmem_limit(est)),
        name="out_ln",
    )(a1, a2, w, x, gate, lng, lnb)


def _ffn_kernel(x_ref, xp_ref, xn_ref, sh_ref, sc_ref, g_ref, wu1_ref, wu2_ref, cw1_ref, cw2_ref,
                cb1_ref, cb2_ref, wd_ref, lng_ref, lnb_ref, o_ref, hm_ref, u_ref, gt_ref, *, rows):
    _fill_halo_tile(hm_ref, x_ref, xp_ref, xn_ref, sh_ref[0], sc_ref[0], rows)
    n_chunks = wu1_ref.shape[0]
    tiles = rows // V7X_SUBLANES
    sub = lax.broadcasted_iota(jnp.int32, (tiles, V7X_SUBLANES, FF_CHUNK), 1)

    def up_proj(c, slot):
        hm = hm_ref[...]
        u_ref[slot, 0] = _dot(hm, wu1_ref[c]).reshape(tiles + 2, V7X_SUBLANES, FF_CHUNK)
        u_ref[slot, 1] = _dot(hm, wu2_ref[c]).reshape(tiles + 2, V7X_SUBLANES, FF_CHUNK)

    def conv_gate(c, slot):
        def conv(half, cw_ref, cb_ref):
            u = u_ref[slot, half]
            dn = pltpu.roll(u, 1, 1)
            up = pltpu.roll(u, V7X_SUBLANES - 1, 1)
            prev = jnp.where(sub == 0, dn[0:tiles], dn[1:tiles + 1])
            nxt = jnp.where(sub == V7X_SUBLANES - 1, up[2:tiles + 2], up[1:tiles + 1])
            w = cw_ref[c]
            return cb_ref[c] + w[0:1] * prev + w[1:2] * u[1:tiles + 1] + w[2:3] * nxt

        a1 = conv(0, cw1_ref, cb1_ref)
        a2 = conv(1, cw2_ref, cb2_ref)
        gt_ref[c] = (a1 * _silu(a2)).reshape(rows, FF_CHUNK)

    up_proj(0, 0)

    def pair(k, carry):
        c = 2 * k
        up_proj(c + 1, 1)
        conv_gate(c, 0)
        up_proj(c + 2, 0)
        conv_gate(c + 1, 1)
        return carry

    lax.fori_loop(0, (n_chunks - 1) // 2, pair, 0)
    conv_gate(n_chunks - 1, 0)

    f = _dot(gt_ref[0].astype(BF16), wd_ref[0])
    for c in range(1, n_chunks):
        f = f + _dot(gt_ref[c].astype(BF16), wd_ref[c])
    z = DN_ALPHA * x_ref[0] + g_ref[0] * f
    o_ref[0] = _layer_norm(z, lng_ref[...], lnb_ref[...])


def _ffn(x, sh, sc, gate, wu1, wu2, cw1, cw2, cb1, cb2, wd, lng, lnb, rows):
    bsz, seq, d = x.shape
    nc = wu1.shape[0]
    assert nc % 2 == 1, "the pipeline peels one chunk and pairs the rest"
    est = (3 * nc * d * FF_CHUNK * 2 + 4 * rows * d * 4 + (rows + 2 * HALO) * d * 2
           + nc * rows * FF_CHUNK * 4 + 8 * (rows + 2 * HALO) * FF_CHUNK * 4 + 2 * rows * d * 4)
    return pl.pallas_call(
        functools.partial(_ffn_kernel, rows=rows),
        out_shape=jax.ShapeDtypeStruct((bsz, seq, d), F32),
        grid=(bsz, seq // rows),
        in_specs=_halo_specs(rows, seq, d) + [
            _row_spec(d), _row_spec(d), _row_spec(d),
            _resident((nc, d, FF_CHUNK)), _resident((nc, d, FF_CHUNK)),
            _resident((nc, CONV_W, FF_CHUNK)), _resident((nc, CONV_W, FF_CHUNK)),
            _resident((nc, 1, FF_CHUNK)), _resident((nc, 1, FF_CHUNK)),
            _resident((nc, FF_CHUNK, d)),
            _resident((1, d)), _resident((1, d)),
        ],
        out_specs=pl.BlockSpec((1, rows, d), lambda b, t: (b, t, 0)),
        scratch_shapes=[
            pltpu.VMEM((rows + 2 * HALO, d), BF16),
            pltpu.VMEM((2, 2, rows // HALO + 2, HALO, FF_CHUNK), F32),
            pltpu.VMEM((nc, rows, FF_CHUNK), F32),
        ],
        compiler_params=pltpu.CompilerParams(
            dimension_semantics=("arbitrary", "arbitrary"),
            vmem_limit_bytes=_vmem_limit(est)),
        name="conv_ffn",
    )(x, x, x, sh, sc, gate, wu1, wu2, cw1, cw2, cb1, cb2, wd, lng, lnb)


def _cd_in_kernel(*refs, with_fz, rows):
    x_ref, xp_ref, xn_ref, sh_ref, sc_ref = refs[:5]
    if with_fz:
        wfz_ref, wx_ref, wdt_ref, cw_ref, cb_ref, dtb_ref = refs[5:11]
        f_ref, z_ref, xs_ref, b_ref, c_ref, dt_ref, hm_ref = refs[11:]
    else:
        wx_ref, wdt_ref, cw_ref, cb_ref, dtb_ref = refs[5:10]
        xs_ref, b_ref, c_ref, dt_ref, hm_ref = refs[10:]
    _fill_halo_tile(hm_ref, x_ref, xp_ref, xn_ref, sh_ref[0], sc_ref[0], rows)
    hc = hm_ref[HALO:HALO + rows, :]
    if with_fz:
        pfz = _dot(hc, wfz_ref[...])
        f_ref[0] = pfz[:, 0:FN_WIDTH].astype(BF16)
        z_ref[0] = pfz[:, FN_WIDTH:SSM_OFF]
    a = _conv3(_dot(hm_ref[...], wx_ref[...]), cw_ref[...], cb_ref[...], rows)
    s = _silu(a)
    xs_ref[0] = s[:, 0:SSD_WIDTH]
    b_ref[0] = s[:, SSD_WIDTH:SSD_WIDTH + SSD_GN].astype(BF16)
    c_ref[0] = s[:, SSD_WIDTH + SSD_GN:SSD_CONV_DIM].astype(BF16)
    dt_ref[0] = _softplus(_dot(hc, wdt_ref[...]) + dtb_ref[...])


def _cd_in(x, sh, sc, wfz, wx, wdt, cw, cb, dtb, rows):
    bsz, seq, d = x.shape
    with_fz = wfz is not None
    in_specs = _halo_specs(rows, seq, d) + [_row_spec(d), _row_spec(d)]
    args = [x, x, x, sh, sc]
    out_shape, out_specs = [], []
    if with_fz:
        in_specs.append(_resident((d, SSM_OFF)))
        args.append(wfz)
        out_shape += [jax.ShapeDtypeStruct((bsz, seq, FN_WIDTH), BF16),
                      jax.ShapeDtypeStruct((bsz, seq, SSD_WIDTH), F32)]
        out_specs += [pl.BlockSpec((1, rows, FN_WIDTH), lambda b, t: (b, t, 0)),
                      pl.BlockSpec((1, rows, SSD_WIDTH), lambda b, t: (b, t, 0))]
    in_specs += [_resident((d, SSD_CONV_DIM)), _resident((d, V7X_LANES)),
                 _resident((CONV_W, SSD_CONV_DIM)), _resident((1, SSD_CONV_DIM)),
                 _resident((1, V7X_LANES))]
    args += [wx, wdt, cw, cb, dtb]
    out_shape += [jax.ShapeDtypeStruct((bsz, seq, SSD_WIDTH), F32),
                  jax.ShapeDtypeStruct((bsz, seq, SSD_GN), BF16),
                  jax.ShapeDtypeStruct((bsz, seq, SSD_GN), BF16),
                  jax.ShapeDtypeStruct((bsz, seq, V7X_LANES), F32)]
    out_specs += [pl.BlockSpec((1, rows, SSD_WIDTH), lambda b, t: (b, t, 0)),
                  pl.BlockSpec((1, rows, SSD_GN), lambda b, t: (b, t, 0)),
                  pl.BlockSpec((1, rows, SSD_GN), lambda b, t: (b, t, 0)),
                  pl.BlockSpec((1, rows, V7X_LANES), lambda b, t: (b, t, 0))]
    est = d * 2200 * 2 + 2 * rows * d * 4 + 8 * (rows + 2 * HALO) * SSD_CONV_DIM * 4
    return pl.pallas_call(
        functools.partial(_cd_in_kernel, with_fz=with_fz, rows=rows),
        out_shape=out_shape,
        grid=(bsz, seq // rows),
        in_specs=in_specs,
        out_specs=out_specs,
        scratch_shapes=[pltpu.VMEM((rows + 2 * HALO, d), BF16)],
        compiler_params=pltpu.CompilerParams(
            dimension_semantics=("arbitrary", "arbitrary"),
            vmem_limit_bytes=_vmem_limit(est)),
        name="cd_in" if with_fz else "cd_in_ctx",
    )(*args)


def _cumsum_rows(z, reverse):
    n = z.shape[0]
    idx = lax.broadcasted_iota(jnp.int32, z.shape, 0)
    k = 1
    while k < n:
        if reverse:
            z = z + jnp.where(idx < n - k, pltpu.roll(z, n - k, 0), 0.0)
        else:
            z = z + jnp.where(idx >= k, pltpu.roll(z, k, 0), 0.0)
        k *= 2
    return z


def _lane_expand(row, lane0, width):
    lane = lax.broadcasted_iota(jnp.int32, (1, SSD_HPG * width), 1)
    out = jnp.zeros((1, SSD_HPG * width), F32)
    for j in range(SSD_HPG):
        val = jnp.broadcast_to(row[:, lane0 + j:lane0 + j + 1], (1, SSD_HPG * width))
        out = jnp.where(lane // width == j, val, out)
    return out


def _ssd_chunk(x, bm, cm, dt, a_row, h_ref, lane0, reverse, need_y):
    q = x.shape[0]
    gw = SSD_HPG * SSD_HEADDIM
    acum = _cumsum_rows(dt * a_row, reverse)
    total = acum[0:1, :] if reverse else acum[q - 1:q, :]
    acum_t = acum.T
    dt_t = dt.T
    total_t = jnp.broadcast_to(acum_t[:, 0:1] if reverse else acum_t[:, q - 1:q], acum_t.shape)
    wgt_t = dt_t * jnp.exp(total_t - acum_t)
    lane_g = lax.broadcasted_iota(jnp.int32, (q, gw), 1) // SSD_HEADDIM
    lane_gh = lax.broadcasted_iota(jnp.int32, (SSD_STATE, gw), 1) // SSD_HEADDIM
    if need_y:
        srow_t = acum_t - jnp.log(dt_t)
        ti = lax.broadcasted_iota(jnp.int32, (q, q), 0)
        si = lax.broadcasted_iota(jnp.int32, (q, q), 1)
        mask = (si >= ti) if reverse else (si <= ti)
    ys = []
    for g in range(SSD_GROUPS):
        bg = bm[:, g * SSD_STATE:(g + 1) * SSD_STATE]
        xgb = x[:, g * gw:(g + 1) * gw].astype(BF16)
        x_heads = [jnp.where(lane_g == j, xgb, jnp.zeros_like(xgb)) for j in range(SSD_HPG)]
        hg = h_ref[g]
        if need_y:
            cg = cm[:, g * SSD_STATE:(g + 1) * SSD_STATE]
            cb = _dot_nt(cg, bg)
            cgf = cg.astype(F32)
            hgb = hg.astype(BF16)
            lhs, rhs = [], []
            for j in range(SSD_HPG):
                jl = lane0 + g * SSD_HPG + j
                col = jnp.broadcast_to(acum[:, jl:jl + 1], (q, q))
                row = jnp.broadcast_to(srow_t[jl:jl + 1, :], (q, q))
                lhs.append((cb * jnp.exp(jnp.where(mask, col - row, -jnp.inf))).astype(BF16))
                lhs.append((cgf * jnp.exp(col[:, 0:SSD_STATE])).astype(BF16))
                rhs.append(x_heads[j])
                rhs.append(jnp.where(lane_gh == j, hgb, jnp.zeros_like(hgb)))
            ys.append(_dot(jnp.concatenate(lhs, axis=1), jnp.concatenate(rhs, axis=0)))
        bgt = bg.astype(F32).T
        wb = [(bgt * jnp.broadcast_to(wgt_t[lane0 + g * SSD_HPG + j:lane0 + g * SSD_HPG + j + 1, :],
                                      (SSD_STATE, q))).astype(BF16) for j in range(SSD_HPG)]
        st = _dot(jnp.concatenate(wb, axis=1), jnp.concatenate(x_heads, axis=0))
        h_ref[g] = hg * jnp.exp(_lane_expand(total, lane0 + g * SSD_HPG, SSD_HEADDIM)) + st
    if need_y:
        return jnp.concatenate(ys, axis=1)
    return None


def _ssd_kernel(xc_ref, bc_ref, dtc_ref, xl_ref, bl_ref, cl_ref, dtl_ref, z_ref, alog_ref,
                dskip_ref, ng_ref, o_ref, y_ref, h_ref):
    a_row = -jnp.exp(alog_ref[...])
    nc_ctx = xc_ref.shape[1] // SSD_CHUNK
    nc_lat = xl_ref.shape[1] // SSD_CHUNK

    def rows_of(c):
        return pl.ds(pl.multiple_of(c * SSD_CHUNK, SSD_CHUNK), SSD_CHUNK)

    def ctx_step(c, direction):
        r = rows_of(c)
        _ssd_chunk(xc_ref[0, r, :], bc_ref[0, r, :], None, dtc_ref[0, r, :], a_row,
                   h_ref.at[direction], direction * SSD_HEADS, direction == 1, False)

    def lat_step(c, direction):
        r = rows_of(c)
        y_ref[direction, r, :] = _ssd_chunk(
            xl_ref[0, r, :], bl_ref[0, r, :], cl_ref[0, r, :], dtl_ref[0, r, :], a_row,
            h_ref.at[direction], direction * SSD_HEADS, direction == 1, True)

    h_ref[...] = jnp.zeros_like(h_ref)

    def ctx_pair(i, carry):
        ctx_step(i, 0)
        ctx_step(nc_ctx - 1 - i, 1)
        return carry

    lax.fori_loop(0, nc_ctx, ctx_pair, 0)

    def lat_pair(i, carry):
        lat_step(i, 0)
        lat_step(nc_lat - 1 - i, 1)
        return carry

    lax.fori_loop(0, nc_lat, lat_pair, 0)

    def finish(c, carry):
        r = rows_of(c)
        zz = z_ref[0, r, :]
        yt = (y_ref[0, r, :] + y_ref[1, r, :] + xl_ref[0, r, :] * dskip_ref[...]) * _silu(zz)
        o_ref[0, r, :] = _rms_norm(yt, ng_ref[...]).astype(BF16)
        return carry

    lax.fori_loop(0, nc_lat, finish, 0)


def _ssd(xs_c, b_c, dt_c, xs_l, b_l, c_l, dt_l, z, alog_row, dskip_row, ng_row):
    bsz, sl, w = xs_l.shape
    sc = xs_c.shape[1]
    gw = SSD_HPG * SSD_HEADDIM

    def full(s, n):
        return pl.BlockSpec((1, s, n), lambda b: (b, 0, 0))

    est = (2 * (sl + sc) * (w * 4 + 2 * SSD_GN * 2 + V7X_LANES * 4) + 2 * sl * w * 4
           + 2 * sl * w * 2 + 2 * sl * w * 4)
    return pl.pallas_call(
        _ssd_kernel,
        out_shape=jax.ShapeDtypeStruct((bsz, sl, w), BF16),
        grid=(bsz,),
        in_specs=[
            full(sc, w), full(sc, SSD_GN), full(sc, V7X_LANES),
            full(sl, w), full(sl, SSD_GN), full(sl, SSD_GN), full(sl, V7X_LANES),
            full(sl, w),
            _resident((1, V7X_LANES)), _resident((1, w)), _resident((1, w)),
        ],
        out_specs=full(sl, w),
        scratch_shapes=[
            pltpu.VMEM((2, sl, w), F32),
            pltpu.VMEM((2, SSD_GROUPS, SSD_STATE, gw), F32),
        ],
        compiler_params=pltpu.CompilerParams(
            dimension_semantics=("arbitrary",),
            vmem_limit_bytes=_vmem_limit(est)),
        name="ssd_scan",
    )(xs_c, b_c, dt_c, xs_l, b_l, c_l, dt_l, z, alog_row, dskip_row, ng_row)


def _fourier_kernel(f_ref, cc_ref, sc_ref, w2_ref, o_ref, y_ref, *, rows):
    seq = f_ref.shape[1]
    for g in range(FN_GROUPS):
        l = slice(g * FN_GDIM, (g + 1) * FN_GDIM)
        fg = f_ref[0, :, l]
        y_ref[0:seq, l] = _dot(fg, cc_ref[...]).astype(BF16)
        y_ref[seq:2 * seq, l] = _dot(fg, sc_ref[...]).astype(BF16)
    norm = 1.0 / math.sqrt(seq * FN_GDIM)

    def tile(i, carry):
        r = pl.ds(pl.multiple_of(i * rows, rows), rows)
        o_ref[0, r, :] = (_dot(w2_ref[r, :], y_ref[...]) * norm).astype(BF16)
        return carry

    lax.fori_loop(0, seq // rows, tile, 0)


def _fourier(f, cc, sc, w2, rows):
    bsz, seq, w = f.shape
    est = seq * 2 * seq * 2 + 4 * seq * w * 2 + 2 * seq * w * 2 + 4 * rows * 2 * seq * 2
    return pl.pallas_call(
        functools.partial(_fourier_kernel, rows=rows),
        out_shape=jax.ShapeDtypeStruct((bsz, seq, w), BF16),
        grid=(bsz,),
        in_specs=[
            pl.BlockSpec((1, seq, w), lambda b: (b, 0, 0)),
            _resident((FN_GDIM, FN_GDIM)), _resident((FN_GDIM, FN_GDIM)),
            _resident((seq, 2 * seq)),
        ],
        out_specs=pl.BlockSpec((1, seq, w), lambda b: (b, 0, 0)),
        scratch_shapes=[pltpu.VMEM((2 * seq, w), BF16)],
        compiler_params=pltpu.CompilerParams(
            dimension_semantics=("arbitrary",),
            vmem_limit_bytes=_vmem_limit(est)),
        name="fourier_mix",
    )(f, cc, sc, w2)


def _rope_tables(seq):
    rows = seq // GRID_W
    row = jnp.repeat(jnp.arange(rows, dtype=F32), GRID_W)
    col = jnp.broadcast_to(jnp.arange(GRID_W, dtype=F32), (rows, GRID_W)).reshape(-1)
    n_freq = HEAD_DIM // 4
    inv = ROPE_THETA ** (-jnp.arange(n_freq, dtype=F32) / n_freq)
    ang = jnp.stack([row, col], axis=-1)[:, :, None] * inv
    cos, sin = jnp.cos(ang), jnp.sin(ang)
    cos_t = jnp.concatenate([cos[:, 0], cos[:, 1], cos[:, 0], cos[:, 1]], axis=-1)
    sin_t = jnp.concatenate([-sin[:, 0], -sin[:, 1], sin[:, 0], sin[:, 1]], axis=-1)
    return cos_t, sin_t


def _rope_head_order(m):
    lead = m.shape[:-1]
    nh = m.shape[-1] // HEAD_DIM
    m = m.reshape(lead + (nh, 2, 2, HEAD_DIM // 4))
    return jnp.swapaxes(m, -3, -2).reshape(lead + (nh * HEAD_DIM,))


def _dft_tables(seq):
    def tw(n):
        i = np.arange(n, dtype=np.int64)
        ang = ((i[:, None] * i[None, :]) % n).astype(np.float64) * (2.0 * math.pi / n)
        return np.cos(ang), np.sin(ang)

    cc, sc = tw(FN_GDIM)
    cl, sl = tw(seq)
    w2 = np.concatenate([cl, -sl], axis=1).astype(np.float32)
    return (jnp.asarray(cc.astype(np.float32), BF16), jnp.asarray(sc.astype(np.float32), BF16),
            jnp.asarray(w2, BF16))


def _ffn_params(w_up, conv_w, conv_b, w_down):
    d = w_up.shape[0]
    nc = D_FF // FF_CHUNK

    def cols(m, lo):
        r = m.shape[0]
        return m[:, lo:lo + D_FF].reshape(r, nc, FF_CHUNK).transpose(1, 0, 2)

    wu = w_up.astype(BF16)
    cb = conv_b.reshape(1, 2 * D_FF)
    return (cols(wu, 0), cols(wu, D_FF), cols(conv_w, 0), cols(conv_w, D_FF),
            cols(cb, 0), cols(cb, D_FF), w_down.astype(BF16).reshape(nc, FF_CHUNK, d))


def kernel(x, c, ctx, c_ctx, mod_w, mod_b, ln1_g, ln1_b, ln2_g, ln2_b, ffn_up, ffn_conv_w, ffn_conv_b,
           ffn_down, ab_w_in, ab_w_out, gm_ln_g, gm_ln_b, gm_ws, gm_bs, q_norm_g, k_norm_g, cd_w_in,
           cd_w_out, ssd_conv_w, ssd_conv_b, ssd_dt_bias, ssd_a_log, ssd_d, ssd_norm_g):
    bsz, seq, d = x.shape
    ctx_len = ctx.shape[1]
    lat_rows = min(1024, seq)
    out_rows = min(2048, seq)
    dft_rows = min(512, seq)
    ctx_rows = min(256, ctx_len)

    pad = (-(bsz + 1)) % V7X_SUBLANES
    cc = jnp.concatenate([c, c_ctx[None], jnp.zeros((pad, d), F32)], axis=0)
    mod = _modulation(cc, mod_w, mod_b)
    rope_tabs = _rope_tables(seq)

    def row(v):
        return v.reshape(1, -1)

    xl, xc = x, ctx
    for i in range(DEPTH):
        last = i == DEPTH - 1
        j = i // 2
        ml = mod[i, :bsz].reshape(bsz, 6, 1, d)
        mc = jnp.broadcast_to(mod[i, bsz].reshape(1, 6, 1, d), (bsz, 6, 1, d))
        sh1, sc1, g1, sh2, sc2, g2 = (ml[:, k] for k in range(6))
        csh1, csc1, cg1, csh2, csc2, cg2 = (mc[:, k] for k in range(6))
        ln1 = (row(ln1_g[i]), row(ln1_b[i]))
        ln2 = (row(ln2_g[i]), row(ln2_b[i]))
        ffn_p = _ffn_params(ffn_up[i], ffn_conv_w[i], ffn_conv_b[i], ffn_down[i])

        if i % 2 == 0:
            w_in = ab_w_in[j].astype(BF16)
            qk_lo, qk_hi = 2 * GM_WIDTH, KV_OFF + KV_W
            w_in = jnp.concatenate([w_in[:, :qk_lo], _rope_head_order(w_in[:, qk_lo:qk_hi]), w_in[:, qk_hi:]],
                                   axis=1)
            w_out = ab_w_out[j].astype(BF16)
            gm = (row(gm_ln_g[j]), row(gm_ln_b[j]), gm_ws[j].astype(BF16),
                  jnp.repeat(gm_bs[j].T, GM_WIDTH // GM_GROUPS, axis=1),
                  row(_rope_head_order(q_norm_g[j])), row(_rope_head_order(k_norm_g[j])))
            kv_len = seq + ctx_len
            a_c, q_c, k_all, v_all = _ab_in(xc, csh1, csc1, w_in, *gm, None, ctx_rows, kv_len, seq)
            a_l, q_l, k_all, v_all = _ab_in(xl, sh1, sc1, w_in, *gm, rope_tabs, lat_rows, kv_len, 0,
                                            kv_into=(k_all, v_all))
            mix_l = (a_l, _attention(q_l, k_all, v_all, 0, kv_len, lat_rows))
            if not last:
                mix_c = (a_c, _attention(q_c, k_all, v_all, seq, ctx_len, ctx_rows))
        else:
            w_in = cd_w_in[j].astype(BF16)
            w_out = cd_w_out[j].astype(BF16)
            wfz = w_in[:, :SSM_OFF]
            wx = w_in[:, SSM_OFF:SSM_OFF + SSD_CONV_DIM]
            ndt = 2 * SSD_HEADS
            wdt = jnp.pad(w_in[:, SSM_OFF + SSD_CONV_DIM:], ((0, 0), (0, V7X_LANES - ndt)))
            dtb = jnp.pad(ssd_dt_bias[j].reshape(1, ndt), ((0, 0), (0, V7X_LANES - ndt)))
            alog = jnp.pad(ssd_a_log[j].reshape(1, ndt), ((0, 0), (0, V7X_LANES - ndt)))
            cw, cb = ssd_conv_w[j], row(ssd_conv_b[j])
            dskip = row(jnp.repeat(ssd_d[j], SSD_HEADDIM))
            f_l, z_l, xs_l, b_l, c_l, dt_l = _cd_in(xl, sh1, sc1, wfz, wx, wdt, cw, cb, dtb, lat_rows)
            if last:
                xs_c, b_c, _, dt_c = _cd_in(xc, csh1, csc1, None, wx, wdt, cw, cb, dtb, ctx_rows)
            else:
                raise NotImplementedError("odd layers with a context output are not part of this block")
            s_l = _ssd(xs_c, b_c, dt_c, xs_l, b_l, c_l, dt_l, z_l, alog, dskip, row(ssd_norm_g[j]))
            mix_l = (_fourier(f_l, *_dft_tables(seq), dft_rows), s_l)

        xl = _out_ln(*mix_l, w_out, xl, g1, *ln1, out_rows)
        xl = _ffn(xl, sh2, sc2, g2, *ffn_p, *ln2, lat_rows)
        if not last:
            xc = _out_ln(*mix_c, w_out, xc, cg1, *ln1, ctx_rows)
            xc = _ffn(xc, csh2, csc2, cg2, *ffn_p, *ln2, ctx_rows)
    return xl
```

```python
import functools
import math

import jax
import jax.numpy as jnp
import numpy as np
from jax import lax
from jax.experimental import pallas as pl
from jax.experimental.pallas import tpu as pltpu

D_MODEL = 1024
DEPTH = 2
GRID_W = 64
CHUNK = 128
GM_GROUPS = 4
GM_WIDTH = 512
HEAD_DIM = 128
N_Q_HEADS = 4
N_KV_HEADS = 2
ATTN_WIDTH = 512
KV_W = 256
ROPE_THETA = 10000.0
FN_GROUPS = 4
FN_GDIM = 128
FN_WIDTH = 512
SSD_HEADDIM = 64
SSD_HEADS = 8
SSD_GROUPS = 2
SSD_HPG = 4
SSD_STATE = 128
SSD_WIDTH = 512
SSD_GN = 256
SSD_CONV_DIM = 1024
D_FF = 2816
CONV_W = 3
KV_OFF = 2 * GM_WIDTH + ATTN_WIDTH
SSM_OFF = FN_WIDTH + SSD_WIDTH
DN_ALPHA = (2 * DEPTH) ** 0.25
EPS = 1e-6

V7X_LANES = 128
V7X_SUBLANES = 8
V7X_VMEM_BYTES = 64 * 1024 * 1024
AB_SUB = 256
FF_CHUNK = 256
SSD_CHUNK = 256

BF16 = jnp.bfloat16
F32 = jnp.float32
HALO = V7X_SUBLANES


def _vmem_limit(nbytes):
    return int(min(nbytes * 3 // 2 + (8 << 20), V7X_VMEM_BYTES - (3 << 20)))


def _resident(shape):
    nd = len(shape)
    return pl.BlockSpec(shape, lambda *_: (0,) * nd, pipeline_mode=pl.Buffered(1))


def _dot(a, b):
    return jnp.dot(a, b, preferred_element_type=F32)


def _dot_nt(a, b):
    return lax.dot_general(a, b, (((1,), (1,)), ((), ())), preferred_element_type=F32)


def _layer_norm(z, g, b):
    mu = jnp.mean(z, axis=-1, keepdims=True)
    zc = z - mu
    var = jnp.mean(zc * zc, axis=-1, keepdims=True)
    return zc * lax.rsqrt(var + EPS) * g + b


def _rms_norm(z, g):
    return z * lax.rsqrt(jnp.mean(z * z, axis=-1, keepdims=True) + EPS) * g


def _softplus(z):
    return jnp.maximum(z, 0.0) + jnp.log1p(jnp.exp(-jnp.abs(z)))


def _gelu_tanh(z):
    c1 = math.sqrt(2.0 / math.pi)
    h = 0.5 * z
    return h + h * jnp.tanh(z * (c1 + (c1 * 0.044715) * (z * z)))


def _silu(z):
    h = 0.5 * z
    return h + h * jnp.tanh(h)


def _conv3(u, w, b, rows):
    tiles, c = rows // V7X_SUBLANES, u.shape[1]
    u3 = u.reshape(tiles + 2, V7X_SUBLANES, c)
    sub = lax.broadcasted_iota(jnp.int32, (tiles, V7X_SUBLANES, c), 1)
    dn = pltpu.roll(u3, 1, 1)
    up = pltpu.roll(u3, V7X_SUBLANES - 1, 1)
    prev = jnp.where(sub == 0, dn[0:tiles], dn[1:tiles + 1])
    nxt = jnp.where(sub == V7X_SUBLANES - 1, up[2:tiles + 2], up[1:tiles + 1])
    return (b + w[0:1] * prev + w[1:2] * u3[1:tiles + 1] + w[2:3] * nxt).reshape(rows, c)


def _fill_halo_tile(hm_ref, x_ref, xp_ref, xn_ref, sh, sc, rows):
    t = pl.program_id(1)
    nt = pl.num_programs(1)
    keep_prev = jnp.where(t > 0, 1.0, 0.0)
    keep_next = jnp.where(t < nt - 1, 1.0, 0.0)
    hm_ref[HALO:HALO + rows, :] = (x_ref[0] * (1.0 + sc) + sh).astype(BF16)
    hm_ref[0:HALO, :] = ((xp_ref[0] * (1.0 + sc) + sh) * keep_prev).astype(BF16)
    hm_ref[HALO + rows:2 * HALO + rows, :] = ((xn_ref[0] * (1.0 + sc) + sh) * keep_next).astype(BF16)


def _halo_specs(rows, seq, d):
    nb = rows // HALO
    last = seq // HALO - 1
    return [
        pl.BlockSpec((1, rows, d), lambda b, t: (b, t, 0)),
        pl.BlockSpec((1, HALO, d), lambda b, t: (b, jnp.maximum(t * nb - 1, 0), 0)),
        pl.BlockSpec((1, HALO, d), lambda b, t: (b, jnp.minimum((t + 1) * nb, last), 0)),
    ]


def _row_spec(d):
    return pl.BlockSpec((1, 1, d), lambda b, t: (b, 0, 0))


def _mod_kernel(c_ref, w_ref, b_ref, o_ref):
    c = c_ref[...]
    act = c * jax.nn.sigmoid(c)
    o_ref[0] = jnp.dot(act, w_ref[0], preferred_element_type=F32,
                       precision=lax.Precision.HIGHEST) + b_ref[0]


def _modulation(cc, mod_w, mod_b):
    rows, d = cc.shape
    depth, _, n = mod_w.shape
    tn = 2048
    return pl.pallas_call(
        _mod_kernel,
        out_shape=jax.ShapeDtypeStruct((depth, rows, n), F32),
        grid=(depth, n // tn),
        in_specs=[
            pl.BlockSpec((rows, d), lambda i, j: (0, 0)),
            pl.BlockSpec((1, d, tn), lambda i, j: (i, 0, j)),
            pl.BlockSpec((1, 1, tn), lambda i, j: (i, 0, j)),
        ],
        out_specs=pl.BlockSpec((1, rows, tn), lambda i, j: (i, 0, j)),
        compiler_params=pltpu.CompilerParams(
            dimension_semantics=("arbitrary", "arbitrary"),
            vmem_limit_bytes=_vmem_limit(2 * d * tn * 4)),
        name="modulation",
    )(cc, mod_w, mod_b.reshape(depth, 1, n))


def _ab_in_kernel(*refs, use_rope, rows):
    (x_ref, sh_ref, sc_ref, w_ref, lng_ref, lnb_ref, ws_ref, bias_ref, qg_ref, kg_ref) = refs[:10]
    if use_rope:
        cos_ref, sin_ref = refs[10:12]
    a_ref, q_ref, k_ref, v_ref = refs[-4:]

    gd = GM_WIDTH // GM_GROUPS

    def slab(i, carry):
        r0 = pl.multiple_of(i * AB_SUB, AB_SUB)
        rs = pl.ds(r0, AB_SUB)
        h = (x_ref[0, rs, :] * (1.0 + sc_ref[0]) + sh_ref[0]).astype(BF16)

        u = _gelu_tanh(_dot(h, w_ref[:, 0:GM_WIDTH]))
        vn = _layer_norm(_gelu_tanh(_dot(h, w_ref[:, GM_WIDTH:2 * GM_WIDTH])), lng_ref[...], lnb_ref[...])
        vb = vn.astype(BF16)
        for c in range(AB_SUB // CHUNK):
            r = slice(c * CHUNK, (c + 1) * CHUNK)
            for g in range(GM_GROUPS):
                l = slice(g * gd, (g + 1) * gd)
                mixed = _dot(ws_ref[g], vb[r, l]) + bias_ref[:, l]
                a_ref[0, pl.ds(r0 + c * CHUNK, CHUNK), l] = (u[r, l] * mixed).astype(BF16)

        def rope(z):
            if not use_rope:
                return z
            return z * cos_ref[rs, :] + pltpu.roll(z, HEAD_DIM // 2, 1) * sin_ref[rs, :]

        pq = _dot(h, w_ref[:, 2 * GM_WIDTH:KV_OFF])
        for hh in range(N_Q_HEADS):
            l = slice(hh * HEAD_DIM, (hh + 1) * HEAD_DIM)
            q_ref[0, rs, l] = rope(_rms_norm(pq[:, l], qg_ref[...])).astype(BF16)
        pk = _dot(h, w_ref[:, KV_OFF:KV_OFF + KV_W])
        for hh in range(N_KV_HEADS):
            l = slice(hh * HEAD_DIM, (hh + 1) * HEAD_DIM)
            k_ref[0, rs, l] = rope(_rms_norm(pk[:, l], kg_ref[...])).astype(BF16)
        v_ref[0, rs, :] = _dot(h, w_ref[:, KV_OFF + KV_W:KV_OFF + 2 * KV_W]).astype(BF16)
        return carry

    lax.fori_loop(0, rows // AB_SUB, slab, 0)


def _ab_in(x, sh, sc, w, lng, lnb, ws, bias2d, qg, kg, rope_tabs, rows, kv_len, kv_row0, kv_into=None):
    bsz, seq, d = x.shape
    assert kv_row0 % rows == 0
    kv_blk0 = kv_row0 // rows
    n = w.shape[1]
    use_rope = rope_tabs is not None
    in_specs = [
        pl.BlockSpec((1, rows, d), lambda b, t: (b, t, 0)),
        _row_spec(d), _row_spec(d),
        _resident((d, n)),
        _resident((1, GM_WIDTH)), _resident((1, GM_WIDTH)),
        _resident((GM_GROUPS, CHUNK, CHUNK)),
        _resident((CHUNK, GM_WIDTH)),
        _resident((1, HEAD_DIM)), _resident((1, HEAD_DIM)),
    ]
    args = [x, sh, sc, w, lng, lnb, ws, bias2d, qg, kg]
    if use_rope:
        in_specs += [pl.BlockSpec((rows, HEAD_DIM), lambda b, t: (t, 0))] * 2
        args += list(rope_tabs)
    aliases = {}
    if kv_into is not None:
        aliases = {len(args): 2, len(args) + 1: 3}
        in_specs += [pl.BlockSpec(memory_space=pl.ANY)] * 2
        args += list(kv_into)
    out_shape = [
        jax.ShapeDtypeStruct((bsz, seq, GM_WIDTH), BF16),
        jax.ShapeDtypeStruct((bsz, seq, ATTN_WIDTH), BF16),
        jax.ShapeDtypeStruct((bsz, kv_len, KV_W), BF16),
        jax.ShapeDtypeStruct((bsz, kv_len, KV_W), BF16),
    ]
    out_specs = [
        pl.BlockSpec((1, rows, GM_WIDTH), lambda b, t: (b, t, 0)),
        pl.BlockSpec((1, rows, ATTN_WIDTH), lambda b, t: (b, t, 0)),
        pl.BlockSpec((1, rows, KV_W), lambda b, t: (b, t + kv_blk0, 0)),
        pl.BlockSpec((1, rows, KV_W), lambda b, t: (b, t + kv_blk0, 0)),
    ]
    est = d * n * 2 + 2 * rows * d * 4 + 6 * rows * n * 4
    return pl.pallas_call(
        functools.partial(_ab_in_kernel, use_rope=use_rope, rows=rows),
        out_shape=out_shape,
        grid=(bsz, seq // rows),
        in_specs=in_specs,
        out_specs=out_specs,
        input_output_aliases=aliases,
        compiler_params=pltpu.CompilerParams(
            dimension_semantics=("arbitrary", "arbitrary"),
            vmem_limit_bytes=_vmem_limit(est)),
        name="ab_in_rope" if use_rope else "ab_in_ctx",
    )(*args)


def _attn_kernel(q_ref, k_ref, v_ref, o_ref, vt_ref, *, rows, blk):
    @pl.when(pl.program_id(2) == 0)
    def _():
        vt_ref[...] = v_ref[0].astype(F32).T

    k = k_ref[0]
    vt = vt_ref[...].astype(BF16)
    for h in range(N_Q_HEADS // N_KV_HEADS):
        l_ = slice(h * HEAD_DIM, (h + 1) * HEAD_DIM)
        for r0 in range(0, rows, blk):
            qt = q_ref[0, r0:r0 + blk, l_].astype(F32).T.astype(BF16)
            st = _dot(k, qt)
            m = jnp.max(st, axis=0, keepdims=True)
            p = jnp.exp2(st - m)
            l = jnp.sum(p, axis=0, keepdims=True)
            ot = _dot(vt, p.astype(BF16)) / l
            o_ref[0, r0:r0 + blk, l_] = ot.T.astype(BF16)


def _attention(q, k, v, k_row0, sk, rows, blk=512):
    bsz, sq, _ = q.shape
    assert k_row0 % sk == 0
    kb = k_row0 // sk
    gw = (N_Q_HEADS // N_KV_HEADS) * HEAD_DIM
    est = 4 * blk * sk * 12 + 4 * sk * HEAD_DIM * 2 + 4 * rows * gw * 2 + sk * HEAD_DIM * 4
    return pl.pallas_call(
        functools.partial(_attn_kernel, rows=rows, blk=min(blk, rows)),
        out_shape=jax.ShapeDtypeStruct((bsz, sq, ATTN_WIDTH), BF16),
        grid=(bsz, N_KV_HEADS, sq // rows),
        in_specs=[
            pl.BlockSpec((1, rows, gw), lambda b, h, t: (b, t, h)),
            pl.BlockSpec((1, sk, HEAD_DIM), lambda b, h, t: (b, kb, h)),
            pl.BlockSpec((1, sk, HEAD_DIM), lambda b, h, t: (b, kb, h)),
        ],
        out_specs=pl.BlockSpec((1, rows, gw), lambda b, h, t: (b, t, h)),
        scratch_shapes=[pltpu.VMEM((HEAD_DIM, sk), F32)],
        compiler_params=pltpu.CompilerParams(
            dimension_semantics=("arbitrary", "arbitrary", "arbitrary"),
            vmem_limit_bytes=_vmem_limit(est)),
        name="attention",
    )(q, k, v)


def _out_ln_kernel(a1_ref, a2_ref, w_ref, x_ref, g_ref, lng_ref, lnb_ref, o_ref):
    half = a1_ref.shape[2]
    y = _dot(a1_ref[0], w_ref[0:half, :]) + _dot(a2_ref[0], w_ref[half:2 * half, :])
    z = DN_ALPHA * x_ref[0] + g_ref[0] * y
    o_ref[0] = _layer_norm(z, lng_ref[...], lnb_ref[...])


def _out_ln(a1, a2, w, x, gate, lng, lnb, rows):
    bsz, seq, d = x.shape
    half = a1.shape[2]
    est = 2 * half * d * 2 + 4 * rows * d * 4 + 4 * rows * half * 2 + 2 * rows * d * 4
    return pl.pallas_call(
        _out_ln_kernel,
        out_shape=jax.ShapeDtypeStruct((bsz, seq, d), F32),
        grid=(bsz, seq // rows),
        in_specs=[
            pl.BlockSpec((1, rows, half), lambda b, t: (b, t, 0)),
            pl.BlockSpec((1, rows, half), lambda b, t: (b, t, 0)),
            _resident((2 * half, d)),
            pl.BlockSpec((1, rows, d), lambda b, t: (b, t, 0)),
            _row_spec(d),
            _resident((1, d)), _resident((1, d)),
        ],
        out_specs=pl.BlockSpec((1, rows, d), lambda b, t: (b, t, 0)),
        compiler_params=pltpu.CompilerParams(
            dimension_semantics=("arbitrary", "arbitrary"),
            vmem_limit_bytes=_vmem_limit(est)),
        name="out_ln",
    )(a1, a2, w, x, gate, lng, lnb)


def _ffn_kernel(x_ref, xp_ref, xn_ref, sh_ref, sc_ref, g_ref, wu1_ref, wu2_ref, cw1_ref, cw2_ref,
                cb1_ref, cb2_ref, wd_ref, lng_ref, lnb_ref, o_ref, hm_ref, u_ref, gt_ref, *, rows):
    _fill_halo_tile(hm_ref, x_ref, xp_ref, xn_ref, sh_ref[0], sc_ref[0], rows)
    n_chunks = wu1_ref.shape[0]
    tiles = rows // V7X_SUBLANES
    sub = lax.broadcasted_iota(jnp.int32, (tiles, V7X_SUBLANES, FF_CHUNK), 1)

    def up_proj(c, slot):
        hm = hm_ref[...]
        u_ref[slot, 0] = _dot(hm, wu1_ref[c]).reshape(tiles + 2, V7X_SUBLANES, FF_CHUNK)
        u_ref[slot, 1] = _dot(hm, wu2_ref[c]).reshape(tiles + 2, V7X_SUBLANES, FF_CHUNK)

    def conv_gate(c, slot):
        def conv(half, cw_ref, cb_ref):
            u = u_ref[slot, half]
            dn = pltpu.roll(u, 1, 1)
            up = pltpu.roll(u, V7X_SUBLANES - 1, 1)
            prev = jnp.where(sub == 0, dn[0:tiles], dn[1:tiles + 1])
            nxt = jnp.where(sub == V7X_SUBLANES - 1, up[2:tiles + 2], up[1:tiles + 1])
            w = cw_ref[c]
            return cb_ref[c] + w[0:1] * prev + w[1:2] * u[1:tiles + 1] + w[2:3] * nxt

        a1 = conv(0, cw1_ref, cb1_ref)
        a2 = conv(1, cw2_ref, cb2_ref)
        gt_ref[c] = (a1 * _silu(a2)).reshape(rows, FF_CHUNK)

    up_proj(0, 0)

    def pair(k, carry):
        c = 2 * k
        up_proj(c + 1, 1)
        conv_gate(c, 0)
        up_proj(c + 2, 0)
        conv_gate(c + 1, 1)
        return carry

    lax.fori_loop(0, (n_chunks - 1) // 2, pair, 0)
    conv_gate(n_chunks - 1, 0)

    f = _dot(gt_ref[0].astype(BF16), wd_ref[0])
    for c in range(1, n_chunks):
        f = f + _dot(gt_ref[c].astype(BF16), wd_ref[c])
    z = DN_ALPHA * x_ref[0] + g_ref[0] * f
    o_ref[0] = _layer_norm(z, lng_ref[...], lnb_ref[...])


def _ffn(x, sh, sc, gate, wu1, wu2, cw1, cw2, cb1, cb2, wd, lng, lnb, rows):
    bsz, seq, d = x.shape
    nc = wu1.shape[0]
    assert nc % 2 == 1, "the pipeline peels one chunk and pairs the rest"
    est = (3 * nc * d * FF_CHUNK * 2 + 4 * rows * d * 4 + (rows + 2 * HALO) * d * 2
           + nc * rows * FF_CHUNK * 4 + 8 * (rows + 2 * HALO) * FF_CHUNK * 4 + 2 * rows * d * 4)
    return pl.pallas_call(
        functools.partial(_ffn_kernel, rows=rows),
        out_shape=jax.ShapeDtypeStruct((bsz, seq, d), F32),
        grid=(bsz, seq // rows),
        in_specs=_halo_specs(rows, seq, d) + [
            _row_spec(d), _row_spec(d), _row_spec(d),
            _resident((nc, d, FF_CHUNK)), _resident((nc, d, FF_CHUNK)),
            _resident((nc, CONV_W, FF_CHUNK)), _resident((nc, CONV_W, FF_CHUNK)),
            _resident((nc, 1, FF_CHUNK)), _resident((nc, 1, FF_CHUNK)),
            _resident((nc, FF_CHUNK, d)),
            _resident((1, d)), _resident((1, d)),
        ],
        out_specs=pl.BlockSpec((1, rows, d), lambda b, t: (b, t, 0)),
        scratch_shapes=[
            pltpu.VMEM((rows + 2 * HALO, d), BF16),
            pltpu.VMEM((2, 2, rows // HALO + 2, HALO, FF_CHUNK), F32),
            pltpu.VMEM((nc, rows, FF_CHUNK), F32),
        ],
        compiler_params=pltpu.CompilerParams(
            dimension_semantics=("arbitrary", "arbitrary"),
            vmem_limit_bytes=_vmem_limit(est)),
        name="conv_ffn",
    )(x, x, x, sh, sc, gate, wu1, wu2, cw1, cw2, cb1, cb2, wd, lng, lnb)


def _cd_in_kernel(*refs, with_fz, rows):
    x_ref, xp_ref, xn_ref, sh_ref, sc_ref = refs[:5]
    if with_fz:
        wfz_ref, wx_ref, wdt_ref, cw_ref, cb_ref, dtb_ref = refs[5:11]
        f_ref, z_ref, xs_ref, b_ref, c_ref, dt_ref, hm_ref = refs[11:]
    else:
        wx_ref, wdt_ref, cw_ref, cb_ref, dtb_ref = refs[5:10]
        xs_ref, b_ref, c_ref, dt_ref, hm_ref = refs[10:]
    _fill_halo_tile(hm_ref, x_ref, xp_ref, xn_ref, sh_ref[0], sc_ref[0], rows)
    hc = hm_ref[HALO:HALO + rows, :]
    if with_fz:
        pfz = _dot(hc, wfz_ref[...])
        f_ref[0] = pfz[:, 0:FN_WIDTH].astype(BF16)
        z_ref[0] = pfz[:, FN_WIDTH:SSM_OFF]
    a = _conv3(_dot(hm_ref[...], wx_ref[...]), cw_ref[...], cb_ref[...], rows)
    s = _silu(a)
    xs_ref[0] = s[:, 0:SSD_WIDTH]
    b_ref[0] = s[:, SSD_WIDTH:SSD_WIDTH + SSD_GN].astype(BF16)
    c_ref[0] = s[:, SSD_WIDTH + SSD_GN:SSD_CONV_DIM].astype(BF16)
    dt_ref[0] = _softplus(_dot(hc, wdt_ref[...]) + dtb_ref[...])


def _cd_in(x, sh, sc, wfz, wx, wdt, cw, cb, dtb, rows):
    bsz, seq, d = x.shape
    with_fz = wfz is not None
    in_specs = _halo_specs(rows, seq, d) + [_row_spec(d), _row_spec(d)]
    args = [x, x, x, sh, sc]
    out_shape, out_specs = [], []
    if with_fz:
        in_specs.append(_resident((d, SSM_OFF)))
        args.append(wfz)
        out_shape += [jax.ShapeDtypeStruct((bsz, seq, FN_WIDTH), BF16),
                      jax.ShapeDtypeStruct((bsz, seq, SSD_WIDTH), F32)]
        out_specs += [pl.BlockSpec((1, rows, FN_WIDTH), lambda b, t: (b, t, 0)),
                      pl.BlockSpec((1, rows, SSD_WIDTH), lambda b, t: (b, t, 0))]
    in_specs += [_resident((d, SSD_CONV_DIM)), _resident((d, V7X_LANES)),
                 _resident((CONV_W, SSD_CONV_DIM)), _resident((1, SSD_CONV_DIM)),
                 _resident((1, V7X_LANES))]
    args += [wx, wdt, cw, cb, dtb]
    out_shape += [jax.ShapeDtypeStruct((bsz, seq, SSD_WIDTH), F32),
                  jax.ShapeDtypeStruct((bsz, seq, SSD_GN), BF16),
                  jax.ShapeDtypeStruct((bsz, seq, SSD_GN), BF16),
                  jax.ShapeDtypeStruct((bsz, seq, V7X_LANES), F32)]
    out_specs += [pl.BlockSpec((1, rows, SSD_WIDTH), lambda b, t: (b, t, 0)),
                  pl.BlockSpec((1, rows, SSD_GN), lambda b, t: (b, t, 0)),
                  pl.BlockSpec((1, rows, SSD_GN), lambda b, t: (b, t, 0)),
                  pl.BlockSpec((1, rows, V7X_LANES), lambda b, t: (b, t, 0))]
    est = d * 2200 * 2 + 2 * rows * d * 4 + 8 * (rows + 2 * HALO) * SSD_CONV_DIM * 4
    return pl.pallas_call(
        functools.partial(_cd_in_kernel, with_fz=with_fz, rows=rows),
        out_shape=out_shape,
        grid=(bsz, seq // rows),
        in_specs=in_specs,
        out_specs=out_specs,
        scratch_shapes=[pltpu.VMEM((rows + 2 * HALO, d), BF16)],
        compiler_params=pltpu.CompilerParams(
            dimension_semantics=("arbitrary", "arbitrary"),
            vmem_limit_bytes=_vmem_limit(est)),
        name="cd_in" if with_fz else "cd_in_ctx",
    )(*args)


def _cumsum_rows(z, reverse):
    n = z.shape[0]
    idx = lax.broadcasted_iota(jnp.int32, z.shape, 0)
    k = 1
    while k < n:
        if reverse:
            z = z + jnp.where(idx < n - k, pltpu.roll(z, n - k, 0), 0.0)
        else:
            z = z + jnp.where(idx >= k, pltpu.roll(z, k, 0), 0.0)
        k *= 2
    return z


def _lane_expand(row, lane0, width):
    lane = lax.broadcasted_iota(jnp.int32, (1, SSD_HPG * width), 1)
    out = jnp.zeros((1, SSD_HPG * width), F32)
    for j in range(SSD_HPG):
        val = jnp.broadcast_to(row[:, lane0 + j:lane0 + j + 1], (1, SSD_HPG * width))
        out = jnp.where(lane // width == j, val, out)
    return out


def _ssd_chunk(x, bm, cm, dt, a_row, h_ref, lane0, reverse, need_y):
    q = x.shape[0]
    gw = SSD_HPG * SSD_HEADDIM
    acum = _cumsum_rows(dt * a_row, reverse)
    total = acum[0:1, :] if reverse else acum[q - 1:q, :]
    acum_t = acum.T
    dt_t = dt.T
    total_t = jnp.broadcast_to(acum_t[:, 0:1] if reverse else acum_t[:, q - 1:q], acum_t.shape)
    wgt_t = dt_t * jnp.exp2(total_t - acum_t)
    lane_g = lax.broadcasted_iota(jnp.int32, (q, gw), 1) // SSD_HEADDIM
    lane_gh = lax.broadcasted_iota(jnp.int32, (SSD_STATE, gw), 1) // SSD_HEADDIM
    if need_y:
        srow_t = acum_t - jnp.log2(dt_t)
        ti = lax.broadcasted_iota(jnp.int32, (q, q), 0)
        si = lax.broadcasted_iota(jnp.int32, (q, q), 1)
        mask = (si >= ti) if reverse else (si <= ti)
    ys = []
    for g in range(SSD_GROUPS):
        bg = bm[:, g * SSD_STATE:(g + 1) * SSD_STATE]
        xgb = x[:, g * gw:(g + 1) * gw].astype(BF16)
        x_heads = [jnp.where(lane_g == j, xgb, jnp.zeros_like(xgb)) for j in range(SSD_HPG)]
        hg = h_ref[g]
        if need_y:
            cg = cm[:, g * SSD_STATE:(g + 1) * SSD_STATE]
            cb = _dot_nt(cg, bg)
            cgf = cg.astype(F32)
            hgb = hg.astype(BF16)
            lhs, rhs = [], []
            for j in range(SSD_HPG):
                jl = lane0 + g * SSD_HPG + j
                col = jnp.broadcast_to(acum[:, jl:jl + 1], (q, q))
                row = jnp.broadcast_to(srow_t[jl:jl + 1, :], (q, q))
                lhs.append((cb * jnp.exp2(jnp.where(mask, col - row, -jnp.inf))).astype(BF16))
                lhs.append((cgf * jnp.exp2(col[:, 0:SSD_STATE])).astype(BF16))
                rhs.append(x_heads[j])
                rhs.append(jnp.where(lane_gh == j, hgb, jnp.zeros_like(hgb)))
            ys.append(_dot(jnp.concatenate(lhs, axis=1), jnp.concatenate(rhs, axis=0)))
        bgt = bg.astype(F32).T
        wb = [(bgt * jnp.broadcast_to(wgt_t[lane0 + g * SSD_HPG + j:lane0 + g * SSD_HPG + j + 1, :],
                                      (SSD_STATE, q))).astype(BF16) for j in range(SSD_HPG)]
        st = _dot(jnp.concatenate(wb, axis=1), jnp.concatenate(x_heads, axis=0))
        h_ref[g] = hg * jnp.exp2(_lane_expand(total, lane0 + g * SSD_HPG, SSD_HEADDIM)) + st
    if need_y:
        return jnp.concatenate(ys, axis=1)
    return None


def _ssd_kernel(xc_ref, bc_ref, dtc_ref, xl_ref, bl_ref, cl_ref, dtl_ref, z_ref, alog_ref,
                dskip_ref, ng_ref, o_ref, y_ref, h_ref):
    a_row = -jnp.exp(alog_ref[...]) * math.log2(math.e)
    nc_ctx = xc_ref.shape[1] // SSD_CHUNK
    nc_lat = xl_ref.shape[1] // SSD_CHUNK

    def rows_of(c):
        return pl.ds(pl.multiple_of(c * SSD_CHUNK, SSD_CHUNK), SSD_CHUNK)

    def ctx_step(c, direction):
        r = rows_of(c)
        _ssd_chunk(xc_ref[0, r, :], bc_ref[0, r, :], None, dtc_ref[0, r, :], a_row,
                   h_ref.at[direction], direction * SSD_HEADS, direction == 1, False)

    def lat_step(c, direction):
        r = rows_of(c)
        y_ref[direction, r, :] = _ssd_chunk(
            xl_ref[0, r, :], bl_ref[0, r, :], cl_ref[0, r, :], dtl_ref[0, r, :], a_row,
            h_ref.at[direction], direction * SSD_HEADS, direction == 1, True)

    h_ref[...] = jnp.zeros_like(h_ref)

    def ctx_pair(i, carry):
        ctx_step(i, 0)
        ctx_step(nc_ctx - 1 - i, 1)
        return carry

    lax.fori_loop(0, nc_ctx, ctx_pair, 0)

    def lat_pair(i, carry):
        lat_step(i, 0)
        lat_step(nc_lat - 1 - i, 1)
        return carry

    lax.fori_loop(0, nc_lat, lat_pair, 0)

    def finish(c, carry):
        r = rows_of(c)
        zz = z_ref[0, r, :]
        yt = (y_ref[0, r, :] + y_ref[1, r, :] + xl_ref[0, r, :] * dskip_ref[...]) * _silu(zz)
        o_ref[0, r, :] = _rms_norm(yt, ng_ref[...]).astype(BF16)
        return carry

    lax.fori_loop(0, nc_lat, finish, 0)


def _ssd(xs_c, b_c, dt_c, xs_l, b_l, c_l, dt_l, z, alog_row, dskip_row, ng_row):
    bsz, sl, w = xs_l.shape
    sc = xs_c.shape[1]
    gw = SSD_HPG * SSD_HEADDIM

    def full(s, n):
        return pl.BlockSpec((1, s, n), lambda b: (b, 0, 0))

    est = (2 * (sl + sc) * (w * 4 + 2 * SSD_GN * 2 + V7X_LANES * 4) + 2 * sl * w * 4
           + 2 * sl * w * 2 + 2 * sl * w * 4)
    return pl.pallas_call(
        _ssd_kernel,
        out_shape=jax.ShapeDtypeStruct((bsz, sl, w), BF16),
        grid=(bsz,),
        in_specs=[
            full(sc, w), full(sc, SSD_GN), full(sc, V7X_LANES),
            full(sl, w), full(sl, SSD_GN), full(sl, SSD_GN), full(sl, V7X_LANES),
            full(sl, w),
            _resident((1, V7X_LANES)), _resident((1, w)), _resident((1, w)),
        ],
        out_specs=full(sl, w),
        scratch_shapes=[
            pltpu.VMEM((2, sl, w), F32),
            pltpu.VMEM((2, SSD_GROUPS, SSD_STATE, gw), F32),
        ],
        compiler_params=pltpu.CompilerParams(
            dimension_semantics=("arbitrary",),
            vmem_limit_bytes=_vmem_limit(est)),
        name="ssd_scan",
    )(xs_c, b_c, dt_c, xs_l, b_l, c_l, dt_l, z, alog_row, dskip_row, ng_row)


def _fourier_kernel(f_ref, cc_ref, sc_ref, w2_ref, o_ref, y_ref, *, rows):
    seq = f_ref.shape[1]
    for g in range(FN_GROUPS):
        l = slice(g * FN_GDIM, (g + 1) * FN_GDIM)
        fg = f_ref[0, :, l]
        y_ref[0:seq, l] = _dot(fg, cc_ref[...]).astype(BF16)
        y_ref[seq:2 * seq, l] = _dot(fg, sc_ref[...]).astype(BF16)
    norm = 1.0 / math.sqrt(seq * FN_GDIM)

    def tile(i, carry):
        r = pl.ds(pl.multiple_of(i * rows, rows), rows)
        o_ref[0, r, :] = (_dot(w2_ref[r, :], y_ref[...]) * norm).astype(BF16)
        return carry

    lax.fori_loop(0, seq // rows, tile, 0)


def _fourier(f, cc, sc, w2, rows):
    bsz, seq, w = f.shape
    est = seq * 2 * seq * 2 + 4 * seq * w * 2 + 2 * seq * w * 2 + 4 * rows * 2 * seq * 2
    return pl.pallas_call(
        functools.partial(_fourier_kernel, rows=rows),
        out_shape=jax.ShapeDtypeStruct((bsz, seq, w), BF16),
        grid=(bsz,),
        in_specs=[
            pl.BlockSpec((1, seq, w), lambda b: (b, 0, 0)),
            _resident((FN_GDIM, FN_GDIM)), _resident((FN_GDIM, FN_GDIM)),
            _resident((seq, 2 * seq)),
        ],
        out_specs=pl.BlockSpec((1, seq, w), lambda b: (b, 0, 0)),
        scratch_shapes=[pltpu.VMEM((2 * seq, w), BF16)],
        compiler_params=pltpu.CompilerParams(
            dimension_semantics=("arbitrary",),
            vmem_limit_bytes=_vmem_limit(est)),
        name="fourier_mix",
    )(f, cc, sc, w2)


def _rope_tables(seq):
    rows = seq // GRID_W
    row = jnp.repeat(jnp.arange(rows, dtype=F32), GRID_W)
    col = jnp.broadcast_to(jnp.arange(GRID_W, dtype=F32), (rows, GRID_W)).reshape(-1)
    n_freq = HEAD_DIM // 4
    inv = ROPE_THETA ** (-jnp.arange(n_freq, dtype=F32) / n_freq)
    ang = jnp.stack([row, col], axis=-1)[:, :, None] * inv
    cos, sin = jnp.cos(ang), jnp.sin(ang)
    cos_t = jnp.concatenate([cos[:, 0], cos[:, 1], cos[:, 0], cos[:, 1]], axis=-1)
    sin_t = jnp.concatenate([-sin[:, 0], -sin[:, 1], sin[:, 0], sin[:, 1]], axis=-1)
    return cos_t, sin_t


def _rope_head_order(m):
    lead = m.shape[:-1]
    nh = m.shape[-1] // HEAD_DIM
    m = m.reshape(lead + (nh, 2, 2, HEAD_DIM // 4))
    return jnp.swapaxes(m, -3, -2).reshape(lead + (nh * HEAD_DIM,))


def _dft_tables(seq):
    def tw(n):
        i = np.arange(n, dtype=np.int64)
        ang = ((i[:, None] * i[None, :]) % n).astype(np.float64) * (2.0 * math.pi / n)
        return np.cos(ang), np.sin(ang)

    cc, sc = tw(FN_GDIM)
    cl, sl = tw(seq)
    w2 = np.concatenate([cl, -sl], axis=1).astype(np.float32)
    return (jnp.asarray(cc.astype(np.float32), BF16), jnp.asarray(sc.astype(np.float32), BF16),
            jnp.asarray(w2, BF16))


def _ffn_params(w_up, conv_w, conv_b, w_down):
    d = w_up.shape[0]
    nc = D_FF // FF_CHUNK

    def cols(m, lo):
        r = m.shape[0]
        return m[:, lo:lo + D_FF].reshape(r, nc, FF_CHUNK).transpose(1, 0, 2)

    wu = w_up.astype(BF16)
    cb = conv_b.reshape(1, 2 * D_FF)
    return (cols(wu, 0), cols(wu, D_FF), cols(conv_w, 0), cols(conv_w, D_FF),
            cols(cb, 0), cols(cb, D_FF), w_down.astype(BF16).reshape(nc, FF_CHUNK, d))


def kernel(x, c, ctx, c_ctx, mod_w, mod_b, ln1_g, ln1_b, ln2_g, ln2_b, ffn_up, ffn_conv_w, ffn_conv_b,
           ffn_down, ab_w_in, ab_w_out, gm_ln_g, gm_ln_b, gm_ws, gm_bs, q_norm_g, k_norm_g, cd_w_in,
           cd_w_out, ssd_conv_w, ssd_conv_b, ssd_dt_bias, ssd_a_log, ssd_d, ssd_norm_g):
    bsz, seq, d = x.shape
    ctx_len = ctx.shape[1]
    lat_rows = min(1024, seq)
    out_rows = min(2048, seq)
    dft_rows = min(512, seq)
    ctx_rows = min(256, ctx_len)

    pad = (-(bsz + 1)) % V7X_SUBLANES
    cc = jnp.concatenate([c, c_ctx[None], jnp.zeros((pad, d), F32)], axis=0)
    mod = _modulation(cc, mod_w, mod_b)
    rope_tabs = _rope_tables(seq)

    def row(v):
        return v.reshape(1, -1)

    xl, xc = x, ctx
    for i in range(DEPTH):
        last = i == DEPTH - 1
        j = i // 2
        ml = mod[i, :bsz].reshape(bsz, 6, 1, d)
        mc = jnp.broadcast_to(mod[i, bsz].reshape(1, 6, 1, d), (bsz, 6, 1, d))
        sh1, sc1, g1, sh2, sc2, g2 = (ml[:, k] for k in range(6))
        csh1, csc1, cg1, csh2, csc2, cg2 = (mc[:, k] for k in range(6))
        ln1 = (row(ln1_g[i]), row(ln1_b[i]))
        ln2 = (row(ln2_g[i]), row(ln2_b[i]))
        ffn_p = _ffn_params(ffn_up[i], ffn_conv_w[i], ffn_conv_b[i], ffn_down[i])

        if i % 2 == 0:
            w_in = ab_w_in[j].astype(BF16)
            qk_lo, qk_hi = 2 * GM_WIDTH, KV_OFF + KV_W
            q_scale = HEAD_DIM ** -0.5 * math.log2(math.e)
            w_in = jnp.concatenate([w_in[:, :qk_lo], _rope_head_order(w_in[:, qk_lo:qk_hi]), w_in[:, qk_hi:]],
                                   axis=1)
            w_out = ab_w_out[j].astype(BF16)
            gm = (row(gm_ln_g[j]), row(gm_ln_b[j]), gm_ws[j].astype(BF16),
                  jnp.repeat(gm_bs[j].T, GM_WIDTH // GM_GROUPS, axis=1),
                  row(_rope_head_order(q_norm_g[j]) * q_scale), row(_rope_head_order(k_norm_g[j])))
            kv_len = seq + ctx_len
            a_c, q_c, k_all, v_all = _ab_in(xc, csh1, csc1, w_in, *gm, None, ctx_rows, kv_len, seq)
            a_l, q_l, k_all, v_all = _ab_in(xl, sh1, sc1, w_in, *gm, rope_tabs, lat_rows, kv_len, 0,
                                            kv_into=(k_all, v_all))
            mix_l = (a_l, _attention(q_l, k_all, v_all, 0, kv_len, lat_rows))
            if not last:
                mix_c = (a_c, _attention(q_c, k_all, v_all, seq, ctx_len, ctx_rows))
        else:
            w_in = cd_w_in[j].astype(BF16)
            w_out = cd_w_out[j].astype(BF16)
            wfz = w_in[:, :SSM_OFF]
            wx = w_in[:, SSM_OFF:SSM_OFF + SSD_CONV_DIM]
            ndt = 2 * SSD_HEADS
            wdt = jnp.pad(w_in[:, SSM_OFF + SSD_CONV_DIM:], ((0, 0), (0, V7X_LANES - ndt)))
            dtb = jnp.pad(ssd_dt_bias[j].reshape(1, ndt), ((0, 0), (0, V7X_LANES - ndt)))
            alog = jnp.pad(ssd_a_log[j].reshape(1, ndt), ((0, 0), (0, V7X_LANES - ndt)))
            cw, cb = ssd_conv_w[j], row(ssd_conv_b[j])
            dskip = row(jnp.repeat(ssd_d[j], SSD_HEADDIM))
            f_l, z_l, xs_l, b_l, c_l, dt_l = _cd_in(xl, sh1, sc1, wfz, wx, wdt, cw, cb, dtb, lat_rows)
            if last:
                xs_c, b_c, _, dt_c = _cd_in(xc, csh1, csc1, None, wx, wdt, cw, cb, dtb, ctx_rows)
            else:
                raise NotImplementedError("odd layers with a context output are not part of this block")
            s_l = _ssd(xs_c, b_c, dt_c, xs_l, b_l, c_l, dt_l, z_l, alog, dskip, row(ssd_norm_g[j]))
            mix_l = (_fourier(f_l, *_dft_tables(seq), dft_rows), s_l)

        xl = _out_ln(*mix_l, w_out, xl, g1, *ln1, out_rows)
        xl = _ffn(xl, sh2, sc2, g2, *ffn_p, *ln2, lat_rows)
        if not last:
            xc = _out_ln(*mix_c, w_out, xc, cg1, *ln1, ctx_rows)
            xc = _ffn(xc, csh2, csc2, cg2, *ffn_p, *ln2, ctx_rows)
    return xl
```

```python
import functools
import math

import jax
import jax.numpy as jnp
import numpy as np
from jax import lax
from jax.experimental import pallas as pl
from jax.experimental.pallas import tpu as pltpu

D_MODEL = 1024
DEPTH = 2
GRID_W = 64
CHUNK = 128
GM_GROUPS = 4
GM_WIDTH = 512
HEAD_DIM = 128
N_Q_HEADS = 4
N_KV_HEADS = 2
ATTN_WIDTH = 512
KV_W = 256
ROPE_THETA = 10000.0
FN_GROUPS = 4
FN_GDIM = 128
FN_WIDTH = 512
SSD_HEADDIM = 64
SSD_HEADS = 8
SSD_GROUPS = 2
SSD_HPG = 4
SSD_STATE = 128
SSD_WIDTH = 512
SSD_GN = 256
SSD_CONV_DIM = 1024
D_FF = 2816
CONV_W = 3
KV_OFF = 2 * GM_WIDTH + ATTN_WIDTH
SSM_OFF = FN_WIDTH + SSD_WIDTH
DN_ALPHA = (2 * DEPTH) ** 0.25
EPS = 1e-6

V7X_LANES = 128
V7X_SUBLANES = 8
V7X_VMEM_BYTES = 64 * 1024 * 1024
VMEM_CHIP_HEADROOM = 3 << 20
VMEM_TEMP_ALLOWANCE = 8 << 20
MOD_COLS = 2048
AB_SUB = 256
FF_CHUNK = 256
SSD_CHUNK = 256

BF16 = jnp.bfloat16
F32 = jnp.float32
HALO = V7X_SUBLANES


def _vmem_limit(nbytes):
    return int(min(nbytes * 3 // 2 + VMEM_TEMP_ALLOWANCE, V7X_VMEM_BYTES - VMEM_CHIP_HEADROOM))


def _resident(shape):
    nd = len(shape)
    return pl.BlockSpec(shape, lambda *_: (0,) * nd, pipeline_mode=pl.Buffered(1))


def _dot(a, b):
    return jnp.dot(a, b, preferred_element_type=F32)


def _dot_nt(a, b):
    return lax.dot_general(a, b, (((1,), (1,)), ((), ())), preferred_element_type=F32)


def _layer_norm(z, g, b):
    mu = jnp.mean(z, axis=-1, keepdims=True)
    zc = z - mu
    var = jnp.mean(zc * zc, axis=-1, keepdims=True)
    return zc * lax.rsqrt(var + EPS) * g + b


def _rms_norm(z, g):
    return z * lax.rsqrt(jnp.mean(z * z, axis=-1, keepdims=True) + EPS) * g


def _softplus(z):
    return jnp.maximum(z, 0.0) + jnp.log1p(jnp.exp(-jnp.abs(z)))


def _gelu_tanh(z):
    c1 = math.sqrt(2.0 / math.pi)
    h = 0.5 * z
    return h + h * jnp.tanh(z * (c1 + (c1 * 0.044715) * (z * z)))


def _silu(z):
    h = 0.5 * z
    return h + h * jnp.tanh(h)


def _conv3(u, w, b, rows):
    tiles, c = rows // V7X_SUBLANES, u.shape[1]
    u3 = u.reshape(tiles + 2, V7X_SUBLANES, c)
    sub = lax.broadcasted_iota(jnp.int32, (tiles, V7X_SUBLANES, c), 1)
    dn = pltpu.roll(u3, 1, 1)
    up = pltpu.roll(u3, V7X_SUBLANES - 1, 1)
    prev = jnp.where(sub == 0, dn[0:tiles], dn[1:tiles + 1])
    nxt = jnp.where(sub == V7X_SUBLANES - 1, up[2:tiles + 2], up[1:tiles + 1])
    return (b + w[0:1] * prev + w[1:2] * u3[1:tiles + 1] + w[2:3] * nxt).reshape(rows, c)


def _fill_halo_tile(hm_ref, x_ref, xp_ref, xn_ref, sh, sc, rows):
    t = pl.program_id(1)
    nt = pl.num_programs(1)
    keep_prev = jnp.where(t > 0, 1.0, 0.0)
    keep_next = jnp.where(t < nt - 1, 1.0, 0.0)
    hm_ref[HALO:HALO + rows, :] = (x_ref[0] * (1.0 + sc) + sh).astype(BF16)
    hm_ref[0:HALO, :] = ((xp_ref[0] * (1.0 + sc) + sh) * keep_prev).astype(BF16)
    hm_ref[HALO + rows:2 * HALO + rows, :] = ((xn_ref[0] * (1.0 + sc) + sh) * keep_next).astype(BF16)


def _halo_specs(rows, seq, d):
    nb = rows // HALO
    last = seq // HALO - 1
    return [
        pl.BlockSpec((1, rows, d), lambda b, t: (b, t, 0)),
        pl.BlockSpec((1, HALO, d), lambda b, t: (b, jnp.maximum(t * nb - 1, 0), 0)),
        pl.BlockSpec((1, HALO, d), lambda b, t: (b, jnp.minimum((t + 1) * nb, last), 0)),
    ]


def _row_spec(d):
    return pl.BlockSpec((1, 1, d), lambda b, t: (b, 0, 0))


def _mod_kernel(c_ref, w_ref, b_ref, o_ref):
    c = c_ref[...]
    act = c * jax.nn.sigmoid(c)
    o_ref[0] = jnp.dot(act, w_ref[0], preferred_element_type=F32,
                       precision=lax.Precision.HIGHEST) + b_ref[0]


def _modulation(cc, mod_w, mod_b):
    rows, d = cc.shape
    depth, _, n = mod_w.shape
    tn = MOD_COLS
    return pl.pallas_call(
        _mod_kernel,
        out_shape=jax.ShapeDtypeStruct((depth, rows, n), F32),
        grid=(depth, n // tn),
        in_specs=[
            pl.BlockSpec((rows, d), lambda i, j: (0, 0)),
            pl.BlockSpec((1, d, tn), lambda i, j: (i, 0, j)),
            pl.BlockSpec((1, 1, tn), lambda i, j: (i, 0, j)),
        ],
        out_specs=pl.BlockSpec((1, rows, tn), lambda i, j: (i, 0, j)),
        compiler_params=pltpu.CompilerParams(
            dimension_semantics=("arbitrary", "arbitrary"),
            vmem_limit_bytes=_vmem_limit(2 * d * tn * 4)),
        name="modulation",
    )(cc, mod_w, mod_b.reshape(depth, 1, n))


def _ab_in_kernel(*refs, use_rope, rows):
    (x_ref, sh_ref, sc_ref, w_ref, lng_ref, lnb_ref, ws_ref, bias_ref, qg_ref, kg_ref) = refs[:10]
    if use_rope:
        cos_ref, sin_ref = refs[10:12]
    a_ref, q_ref, k_ref, v_ref = refs[-4:]

    gd = GM_WIDTH // GM_GROUPS

    def slab(i, carry):
        r0 = pl.multiple_of(i * AB_SUB, AB_SUB)
        rs = pl.ds(r0, AB_SUB)
        h = (x_ref[0, rs, :] * (1.0 + sc_ref[0]) + sh_ref[0]).astype(BF16)

        u = _gelu_tanh(_dot(h, w_ref[:, 0:GM_WIDTH]))
        vn = _layer_norm(_gelu_tanh(_dot(h, w_ref[:, GM_WIDTH:2 * GM_WIDTH])), lng_ref[...], lnb_ref[...])
        vb = vn.astype(BF16)
        for c in range(AB_SUB // CHUNK):
            r = slice(c * CHUNK, (c + 1) * CHUNK)
            for g in range(GM_GROUPS):
                l = slice(g * gd, (g + 1) * gd)
                mixed = _dot(ws_ref[g], vb[r, l]) + bias_ref[:, l]
                a_ref[0, pl.ds(r0 + c * CHUNK, CHUNK), l] = (u[r, l] * mixed).astype(BF16)

        def rope(z):
            if not use_rope:
                return z
            return z * cos_ref[rs, :] + pltpu.roll(z, HEAD_DIM // 2, 1) * sin_ref[rs, :]

        pq = _dot(h, w_ref[:, 2 * GM_WIDTH:KV_OFF])
        for hh in range(N_Q_HEADS):
            l = slice(hh * HEAD_DIM, (hh + 1) * HEAD_DIM)
            q_ref[0, rs, l] = rope(_rms_norm(pq[:, l], qg_ref[...])).astype(BF16)
        pk = _dot(h, w_ref[:, KV_OFF:KV_OFF + KV_W])
        for hh in range(N_KV_HEADS):
            l = slice(hh * HEAD_DIM, (hh + 1) * HEAD_DIM)
            k_ref[0, rs, l] = rope(_rms_norm(pk[:, l], kg_ref[...])).astype(BF16)
        v_ref[0, rs, :] = _dot(h, w_ref[:, KV_OFF + KV_W:KV_OFF + 2 * KV_W]).astype(BF16)
        return carry

    lax.fori_loop(0, rows // AB_SUB, slab, 0)


def _ab_in(x, sh, sc, w, lng, lnb, ws, bias2d, qg, kg, rope_tabs, rows, kv_len, kv_row0, kv_into=None):
    bsz, seq, d = x.shape
    assert kv_row0 % rows == 0
    kv_blk0 = kv_row0 // rows
    n = w.shape[1]
    use_rope = rope_tabs is not None
    in_specs = [
        pl.BlockSpec((1, rows, d), lambda b, t: (b, t, 0)),
        _row_spec(d), _row_spec(d),
        _resident((d, n)),
        _resident((1, GM_WIDTH)), _resident((1, GM_WIDTH)),
        _resident((GM_GROUPS, CHUNK, CHUNK)),
        _resident((CHUNK, GM_WIDTH)),
        _resident((1, HEAD_DIM)), _resident((1, HEAD_DIM)),
    ]
    args = [x, sh, sc, w, lng, lnb, ws, bias2d, qg, kg]
    if use_rope:
        in_specs += [pl.BlockSpec((rows, HEAD_DIM), lambda b, t: (t, 0))] * 2
        args += list(rope_tabs)
    aliases = {}
    if kv_into is not None:
        aliases = {len(args): 2, len(args) + 1: 3}
        in_specs += [pl.BlockSpec(memory_space=pl.ANY)] * 2
        args += list(kv_into)
    out_shape = [
        jax.ShapeDtypeStruct((bsz, seq, GM_WIDTH), BF16),
        jax.ShapeDtypeStruct((bsz, seq, ATTN_WIDTH), BF16),
        jax.ShapeDtypeStruct((bsz, kv_len, KV_W), BF16),
        jax.ShapeDtypeStruct((bsz, kv_len, KV_W), BF16),
    ]
    out_specs = [
        pl.BlockSpec((1, rows, GM_WIDTH), lambda b, t: (b, t, 0)),
        pl.BlockSpec((1, rows, ATTN_WIDTH), lambda b, t: (b, t, 0)),
        pl.BlockSpec((1, rows, KV_W), lambda b, t: (b, t + kv_blk0, 0)),
        pl.BlockSpec((1, rows, KV_W), lambda b, t: (b, t + kv_blk0, 0)),
    ]
    est = d * n * 2 + 2 * rows * d * 4 + 6 * rows * n * 4
    return pl.pallas_call(
        functools.partial(_ab_in_kernel, use_rope=use_rope, rows=rows),
        out_shape=out_shape,
        grid=(bsz, seq // rows),
        in_specs=in_specs,
        out_specs=out_specs,
        input_output_aliases=aliases,
        compiler_params=pltpu.CompilerParams(
            dimension_semantics=("arbitrary", "arbitrary"),
            vmem_limit_bytes=_vmem_limit(est)),
        name="ab_in_rope" if use_rope else "ab_in_ctx",
    )(*args)


def _attn_kernel(q_ref, k_ref, v_ref, o_ref, vt_ref, *, rows, blk):
    @pl.when(pl.program_id(2) == 0)
    def _():
        vt_ref[...] = v_ref[0].astype(F32).T

    k = k_ref[0]
    vt = vt_ref[...].astype(BF16)
    for h in range(N_Q_HEADS // N_KV_HEADS):
        l_ = slice(h * HEAD_DIM, (h + 1) * HEAD_DIM)
        for r0 in range(0, rows, blk):
            qt = q_ref[0, r0:r0 + blk, l_].astype(F32).T.astype(BF16)
            st = _dot(k, qt)
            m = jnp.max(st, axis=0, keepdims=True)
            p = jnp.exp2(st - m)
            l = jnp.sum(p, axis=0, keepdims=True)
            ot = _dot(vt, p.astype(BF16)) / l
            o_ref[0, r0:r0 + blk, l_] = ot.T.astype(BF16)


def _attention(q, k, v, k_row0, sk, rows, blk=512):
    bsz, sq, _ = q.shape
    assert k_row0 % sk == 0
    kb = k_row0 // sk
    gw = (N_Q_HEADS // N_KV_HEADS) * HEAD_DIM
    est = 4 * blk * sk * 12 + 4 * sk * HEAD_DIM * 2 + 4 * rows * gw * 2 + sk * HEAD_DIM * 4
    return pl.pallas_call(
        functools.partial(_attn_kernel, rows=rows, blk=min(blk, rows)),
        out_shape=jax.ShapeDtypeStruct((bsz, sq, ATTN_WIDTH), BF16),
        grid=(bsz, N_KV_HEADS, sq // rows),
        in_specs=[
            pl.BlockSpec((1, rows, gw), lambda b, h, t: (b, t, h)),
            pl.BlockSpec((1, sk, HEAD_DIM), lambda b, h, t: (b, kb, h)),
            pl.BlockSpec((1, sk, HEAD_DIM), lambda b, h, t: (b, kb, h)),
        ],
        out_specs=pl.BlockSpec((1, rows, gw), lambda b, h, t: (b, t, h)),
        scratch_shapes=[pltpu.VMEM((HEAD_DIM, sk), F32)],
        compiler_params=pltpu.CompilerParams(
            dimension_semantics=("arbitrary", "arbitrary", "arbitrary"),
            vmem_limit_bytes=_vmem_limit(est)),
        name="attention",
    )(q, k, v)


def _out_ln_kernel(a1_ref, a2_ref, w_ref, x_ref, g_ref, lng_ref, lnb_ref, o_ref):
    half = a1_ref.shape[2]
    y = _dot(a1_ref[0], w_ref[0:half, :]) + _dot(a2_ref[0], w_ref[half:2 * half, :])
    z = DN_ALPHA * x_ref[0] + g_ref[0] * y
    o_ref[0] = _layer_norm(z, lng_ref[...], lnb_ref[...])


def _out_ln(a1, a2, w, x, gate, lng, lnb, rows):
    bsz, seq, d = x.shape
    half = a1.shape[2]
    est = 2 * half * d * 2 + 4 * rows * d * 4 + 4 * rows * half * 2 + 2 * rows * d * 4
    return pl.pallas_call(
        _out_ln_kernel,
        out_shape=jax.ShapeDtypeStruct((bsz, seq, d), F32),
        grid=(bsz, seq // rows),
        in_specs=[
            pl.BlockSpec((1, rows, half), lambda b, t: (b, t, 0)),
            pl.BlockSpec((1, rows, half), lambda b, t: (b, t, 0)),
            _resident((2 * half, d)),
            pl.BlockSpec((1, rows, d), lambda b, t: (b, t, 0)),
            _row_spec(d),
            _resident((1, d)), _resident((1, d)),
        ],
        out_specs=pl.BlockSpec((1, rows, d), lambda b, t: (b, t, 0)),
        compiler_params=pltpu.CompilerParams(
            dimension_semantics=("arbitrary", "arbitrary"),
            vmem_limit_bytes=_vmem_limit(est)),
        name="out_ln",
    )(a1, a2, w, x, gate, lng, lnb)


def _ffn_kernel(x_ref, xp_ref, xn_ref, sh_ref, sc_ref, g_ref, wu1_ref, wu2_ref, cw1_ref, cw2_ref,
                cb1_ref, cb2_ref, wd_ref, lng_ref, lnb_ref, o_ref, hm_ref, u_ref, gt_ref, *, rows):
    _fill_halo_tile(hm_ref, x_ref, xp_ref, xn_ref, sh_ref[0], sc_ref[0], rows)
    n_chunks = wu1_ref.shape[0]
    tiles = rows // V7X_SUBLANES
    sub = lax.broadcasted_iota(jnp.int32, (tiles, V7X_SUBLANES, FF_CHUNK), 1)

    def up_proj(c, slot):
        hm = hm_ref[...]
        u_ref[slot, 0] = _dot(hm, wu1_ref[c]).reshape(tiles + 2, V7X_SUBLANES, FF_CHUNK)
        u_ref[slot, 1] = _dot(hm, wu2_ref[c]).reshape(tiles + 2, V7X_SUBLANES, FF_CHUNK)

    def conv_gate(c, slot):
        def conv(half, cw_ref, cb_ref):
            u = u_ref[slot, half]
            dn = pltpu.roll(u, 1, 1)
            up = pltpu.roll(u, V7X_SUBLANES - 1, 1)
            prev = jnp.where(sub == 0, dn[0:tiles], dn[1:tiles + 1])
            nxt = jnp.where(sub == V7X_SUBLANES - 1, up[2:tiles + 2], up[1:tiles + 1])
            w = cw_ref[c]
            return cb_ref[c] + w[0:1] * prev + w[1:2] * u[1:tiles + 1] + w[2:3] * nxt

        a1 = conv(0, cw1_ref, cb1_ref)
        a2 = conv(1, cw2_ref, cb2_ref)
        gt_ref[c] = (a1 * (a2 + a2 * jnp.tanh(a2))).reshape(rows, FF_CHUNK)

    up_proj(0, 0)

    def pair(k, carry):
        c = 2 * k
        up_proj(c + 1, 1)
        conv_gate(c, 0)
        up_proj(c + 2, 0)
        conv_gate(c + 1, 1)
        return carry

    lax.fori_loop(0, (n_chunks - 1) // 2, pair, 0)
    conv_gate(n_chunks - 1, 0)

    f = _dot(gt_ref[0].astype(BF16), wd_ref[0])
    for c in range(1, n_chunks):
        f = f + _dot(gt_ref[c].astype(BF16), wd_ref[c])
    z = DN_ALPHA * x_ref[0] + g_ref[0] * f
    o_ref[0] = _layer_norm(z, lng_ref[...], lnb_ref[...])


def _ffn(x, sh, sc, gate, wu1, wu2, cw1, cw2, cb1, cb2, wd, lng, lnb, rows):
    bsz, seq, d = x.shape
    nc = wu1.shape[0]
    assert nc % 2 == 1, "the pipeline peels one chunk and pairs the rest"
    est = (3 * nc * d * FF_CHUNK * 2 + 4 * rows * d * 4 + (rows + 2 * HALO) * d * 2
           + nc * rows * FF_CHUNK * 4 + 8 * (rows + 2 * HALO) * FF_CHUNK * 4 + 2 * rows * d * 4)
    return pl.pallas_call(
        functools.partial(_ffn_kernel, rows=rows),
        out_shape=jax.ShapeDtypeStruct((bsz, seq, d), F32),
        grid=(bsz, seq // rows),
        in_specs=_halo_specs(rows, seq, d) + [
            _row_spec(d), _row_spec(d), _row_spec(d),
            _resident((nc, d, FF_CHUNK)), _resident((nc, d, FF_CHUNK)),
            _resident((nc, CONV_W, FF_CHUNK)), _resident((nc, CONV_W, FF_CHUNK)),
            _resident((nc, 1, FF_CHUNK)), _resident((nc, 1, FF_CHUNK)),
            _resident((nc, FF_CHUNK, d)),
            _resident((1, d)), _resident((1, d)),
        ],
        out_specs=pl.BlockSpec((1, rows, d), lambda b, t: (b, t, 0)),
        scratch_shapes=[
            pltpu.VMEM((rows + 2 * HALO, d), BF16),
            pltpu.VMEM((2, 2, rows // HALO + 2, HALO, FF_CHUNK), F32),
            pltpu.VMEM((nc, rows, FF_CHUNK), F32),
        ],
        compiler_params=pltpu.CompilerParams(
            dimension_semantics=("arbitrary", "arbitrary"),
            vmem_limit_bytes=_vmem_limit(est)),
        name="conv_ffn",
    )(x, x, x, sh, sc, gate, wu1, wu2, cw1, cw2, cb1, cb2, wd, lng, lnb)


def _cd_in_kernel(*refs, with_fz, rows):
    x_ref, xp_ref, xn_ref, sh_ref, sc_ref = refs[:5]
    if with_fz:
        wfz_ref, wx_ref, wdt_ref, cw_ref, cb_ref, dtb_ref = refs[5:11]
        f_ref, z_ref, xs_ref, b_ref, c_ref, dt_ref, hm_ref = refs[11:]
    else:
        wx_ref, wdt_ref, cw_ref, cb_ref, dtb_ref = refs[5:10]
        xs_ref, b_ref, c_ref, dt_ref, hm_ref = refs[10:]
    _fill_halo_tile(hm_ref, x_ref, xp_ref, xn_ref, sh_ref[0], sc_ref[0], rows)
    hc = hm_ref[HALO:HALO + rows, :]
    if with_fz:
        pfz = _dot(hc, wfz_ref[...])
        f_ref[0] = pfz[:, 0:FN_WIDTH].astype(BF16)
        z_ref[0] = pfz[:, FN_WIDTH:SSM_OFF]
    a = _conv3(_dot(hm_ref[...], wx_ref[...]), cw_ref[...], cb_ref[...], rows)
    s = a + a * jnp.tanh(a)
    xs_ref[0] = s[:, 0:SSD_WIDTH]
    b_ref[0] = s[:, SSD_WIDTH:SSD_WIDTH + SSD_GN].astype(BF16)
    c_ref[0] = s[:, SSD_WIDTH + SSD_GN:SSD_CONV_DIM].astype(BF16)
    dt_ref[0] = _softplus(_dot(hc, wdt_ref[...]) + dtb_ref[...])


def _cd_in(x, sh, sc, wfz, wx, wdt, cw, cb, dtb, rows):
    bsz, seq, d = x.shape
    with_fz = wfz is not None
    in_specs = _halo_specs(rows, seq, d) + [_row_spec(d), _row_spec(d)]
    args = [x, x, x, sh, sc]
    out_shape, out_specs = [], []
    if with_fz:
        in_specs.append(_resident((d, SSM_OFF)))
        args.append(wfz)
        out_shape += [jax.ShapeDtypeStruct((bsz, seq, FN_WIDTH), BF16),
                      jax.ShapeDtypeStruct((bsz, seq, SSD_WIDTH), F32)]
        out_specs += [pl.BlockSpec((1, rows, FN_WIDTH), lambda b, t: (b, t, 0)),
                      pl.BlockSpec((1, rows, SSD_WIDTH), lambda b, t: (b, t, 0))]
    in_specs += [_resident((d, SSD_CONV_DIM)), _resident((d, V7X_LANES)),
                 _resident((CONV_W, SSD_CONV_DIM)), _resident((1, SSD_CONV_DIM)),
                 _resident((1, V7X_LANES))]
    args += [wx, wdt, cw, cb, dtb]
    out_shape += [jax.ShapeDtypeStruct((bsz, seq, SSD_WIDTH), F32),
                  jax.ShapeDtypeStruct((bsz, seq, SSD_GN), BF16),
                  jax.ShapeDtypeStruct((bsz, seq, SSD_GN), BF16),
                  jax.ShapeDtypeStruct((bsz, seq, V7X_LANES), F32)]
    out_specs += [pl.BlockSpec((1, rows, SSD_WIDTH), lambda b, t: (b, t, 0)),
                  pl.BlockSpec((1, rows, SSD_GN), lambda b, t: (b, t, 0)),
                  pl.BlockSpec((1, rows, SSD_GN), lambda b, t: (b, t, 0)),
                  pl.BlockSpec((1, rows, V7X_LANES), lambda b, t: (b, t, 0))]
    est = d * 2200 * 2 + 2 * rows * d * 4 + 8 * (rows + 2 * HALO) * SSD_CONV_DIM * 4
    return pl.pallas_call(
        functools.partial(_cd_in_kernel, with_fz=with_fz, rows=rows),
        out_shape=out_shape,
        grid=(bsz, seq // rows),
        in_specs=in_specs,
        out_specs=out_specs,
        scratch_shapes=[pltpu.VMEM((rows + 2 * HALO, d), BF16)],
        compiler_params=pltpu.CompilerParams(
            dimension_semantics=("arbitrary", "arbitrary"),
            vmem_limit_bytes=_vmem_limit(est)),
        name="cd_in" if with_fz else "cd_in_ctx",
    )(*args)


def _cumsum_rows(z, reverse):
    n = z.shape[0]
    idx = lax.broadcasted_iota(jnp.int32, z.shape, 0)
    k = 1
    while k < n:
        if reverse:
            z = z + jnp.where(idx < n - k, pltpu.roll(z, n - k, 0), 0.0)
        else:
            z = z + jnp.where(idx >= k, pltpu.roll(z, k, 0), 0.0)
        k *= 2
    return z


def _lane_expand(row, lane0, width):
    lane = lax.broadcasted_iota(jnp.int32, (1, SSD_HPG * width), 1)
    out = jnp.zeros((1, SSD_HPG * width), F32)
    for j in range(SSD_HPG):
        val = jnp.broadcast_to(row[:, lane0 + j:lane0 + j + 1], (1, SSD_HPG * width))
        out = jnp.where(lane // width == j, val, out)
    return out


def _ssd_chunk(x, bm, cm, dt, a_row, h_ref, lane0, reverse, need_y):
    q = x.shape[0]
    gw = SSD_HPG * SSD_HEADDIM
    acum = _cumsum_rows(dt * a_row, reverse)
    total = acum[0:1, :] if reverse else acum[q - 1:q, :]
    acum_t = acum.T
    dt_t = dt.T
    total_t = jnp.broadcast_to(acum_t[:, 0:1] if reverse else acum_t[:, q - 1:q], acum_t.shape)
    wgt_t = dt_t * jnp.exp2(total_t - acum_t)
    lane_g = lax.broadcasted_iota(jnp.int32, (q, gw), 1) // SSD_HEADDIM
    lane_gh = lax.broadcasted_iota(jnp.int32, (SSD_STATE, gw), 1) // SSD_HEADDIM
    if need_y:
        srow_t = acum_t - jnp.log2(dt_t)
        ti = lax.broadcasted_iota(jnp.int32, (q, q), 0)
        si = lax.broadcasted_iota(jnp.int32, (q, q), 1)
        mask = (si >= ti) if reverse else (si <= ti)
    ys = []
    for g in range(SSD_GROUPS):
        bg = bm[:, g * SSD_STATE:(g + 1) * SSD_STATE]
        xgb = x[:, g * gw:(g + 1) * gw].astype(BF16)
        x_heads = [jnp.where(lane_g == j, xgb, jnp.zeros_like(xgb)) for j in range(SSD_HPG)]
        hg = h_ref[g]
        if need_y:
            cg = cm[:, g * SSD_STATE:(g + 1) * SSD_STATE]
            cb = _dot_nt(cg, bg)
            cgf = cg.astype(F32)
            hgb = hg.astype(BF16)
            lhs, rhs = [], []
            for j in range(SSD_HPG):
                jl = lane0 + g * SSD_HPG + j
                col = jnp.broadcast_to(acum[:, jl:jl + 1], (q, q))
                row = jnp.broadcast_to(srow_t[jl:jl + 1, :], (q, q))
                lhs.append((cb * jnp.exp2(jnp.where(mask, col - row, -jnp.inf))).astype(BF16))
                lhs.append((cgf * jnp.exp2(col[:, 0:SSD_STATE])).astype(BF16))
                rhs.append(x_heads[j])
                rhs.append(jnp.where(lane_gh == j, hgb, jnp.zeros_like(hgb)))
            ys.append(_dot(jnp.concatenate(lhs, axis=1), jnp.concatenate(rhs, axis=0)))
        bgt = bg.astype(F32).T
        wb = [(bgt * jnp.broadcast_to(wgt_t[lane0 + g * SSD_HPG + j:lane0 + g * SSD_HPG + j + 1, :],
                                      (SSD_STATE, q))).astype(BF16) for j in range(SSD_HPG)]
        st = _dot(jnp.concatenate(wb, axis=1), jnp.concatenate(x_heads, axis=0))
        h_ref[g] = hg * jnp.exp2(_lane_expand(total, lane0 + g * SSD_HPG, SSD_HEADDIM)) + st
    if need_y:
        return jnp.concatenate(ys, axis=1)
    return None


def _ssd_kernel(xc_ref, bc_ref, dtc_ref, xl_ref, bl_ref, cl_ref, dtl_ref, z_ref, alog_ref,
                dskip_ref, ng_ref, o_ref, y_ref, h_ref):
    a_row = -jnp.exp(alog_ref[...]) * math.log2(math.e)
    nc_ctx = xc_ref.shape[1] // SSD_CHUNK
    nc_lat = xl_ref.shape[1] // SSD_CHUNK

    def rows_of(c):
        return pl.ds(pl.multiple_of(c * SSD_CHUNK, SSD_CHUNK), SSD_CHUNK)

    def ctx_step(c, direction):
        r = rows_of(c)
        _ssd_chunk(xc_ref[0, r, :], bc_ref[0, r, :], None, dtc_ref[0, r, :], a_row,
                   h_ref.at[direction], direction * SSD_HEADS, direction == 1, False)

    def lat_step(c, direction):
        r = rows_of(c)
        y_ref[direction, r, :] = _ssd_chunk(
            xl_ref[0, r, :], bl_ref[0, r, :], cl_ref[0, r, :], dtl_ref[0, r, :], a_row,
            h_ref.at[direction], direction * SSD_HEADS, direction == 1, True)

    h_ref[...] = jnp.zeros_like(h_ref)

    def ctx_pair(i, carry):
        ctx_step(i, 0)
        ctx_step(nc_ctx - 1 - i, 1)
        return carry

    lax.fori_loop(0, nc_ctx, ctx_pair, 0)

    def lat_pair(i, carry):
        lat_step(i, 0)
        lat_step(nc_lat - 1 - i, 1)
        return carry

    lax.fori_loop(0, nc_lat, lat_pair, 0)

    def finish(c, carry):
        r = rows_of(c)
        zz = z_ref[0, r, :]
        yt = (y_ref[0, r, :] + y_ref[1, r, :] + xl_ref[0, r, :] * dskip_ref[...]) * _silu(zz)
        o_ref[0, r, :] = _rms_norm(yt, ng_ref[...]).astype(BF16)
        return carry

    lax.fori_loop(0, nc_lat, finish, 0)


def _ssd(xs_c, b_c, dt_c, xs_l, b_l, c_l, dt_l, z, alog_row, dskip_row, ng_row):
    bsz, sl, w = xs_l.shape
    sc = xs_c.shape[1]
    assert sl % SSD_CHUNK == 0 and sc % SSD_CHUNK == 0
    gw = SSD_HPG * SSD_HEADDIM

    def full(s, n):
        return pl.BlockSpec((1, s, n), lambda b: (b, 0, 0))

    est = (2 * (sl + sc) * (w * 4 + 2 * SSD_GN * 2 + V7X_LANES * 4) + 2 * sl * w * 4
           + 2 * sl * w * 2 + 2 * sl * w * 4)
    return pl.pallas_call(
        _ssd_kernel,
        out_shape=jax.ShapeDtypeStruct((bsz, sl, w), BF16),
        grid=(bsz,),
        in_specs=[
            full(sc, w), full(sc, SSD_GN), full(sc, V7X_LANES),
            full(sl, w), full(sl, SSD_GN), full(sl, SSD_GN), full(sl, V7X_LANES),
            full(sl, w),
            _resident((1, V7X_LANES)), _resident((1, w)), _resident((1, w)),
        ],
        out_specs=full(sl, w),
        scratch_shapes=[
            pltpu.VMEM((2, sl, w), F32),
            pltpu.VMEM((2, SSD_GROUPS, SSD_STATE, gw), F32),
        ],
        compiler_params=pltpu.CompilerParams(
            dimension_semantics=("arbitrary",),
            vmem_limit_bytes=_vmem_limit(est)),
        name="ssd_scan",
    )(xs_c, b_c, dt_c, xs_l, b_l, c_l, dt_l, z, alog_row, dskip_row, ng_row)


def _fourier_kernel(f_ref, cc_ref, sc_ref, w2_ref, o_ref, y_ref, *, rows):
    seq = f_ref.shape[1]
    for g in range(FN_GROUPS):
        l = slice(g * FN_GDIM, (g + 1) * FN_GDIM)
        fg = f_ref[0, :, l]
        y_ref[0:seq, l] = _dot(fg, cc_ref[...]).astype(BF16)
        y_ref[seq:2 * seq, l] = _dot(fg, sc_ref[...]).astype(BF16)
    norm = 1.0 / math.sqrt(seq * FN_GDIM)

    def tile(i, carry):
        r = pl.ds(pl.multiple_of(i * rows, rows), rows)
        o_ref[0, r, :] = (_dot(w2_ref[r, :], y_ref[...]) * norm).astype(BF16)
        return carry

    lax.fori_loop(0, seq // rows, tile, 0)


def _fourier(f, cc, sc, w2, rows):
    bsz, seq, w = f.shape
    est = seq * 2 * seq * 2 + 4 * seq * w * 2 + 2 * seq * w * 2 + 4 * rows * 2 * seq * 2
    return pl.pallas_call(
        functools.partial(_fourier_kernel, rows=rows),
        out_shape=jax.ShapeDtypeStruct((bsz, seq, w), BF16),
        grid=(bsz,),
        in_specs=[
            pl.BlockSpec((1, seq, w), lambda b: (b, 0, 0)),
            _resident((FN_GDIM, FN_GDIM)), _resident((FN_GDIM, FN_GDIM)),
            _resident((seq, 2 * seq)),
        ],
        out_specs=pl.BlockSpec((1, seq, w), lambda b: (b, 0, 0)),
        scratch_shapes=[pltpu.VMEM((2 * seq, w), BF16)],
        compiler_params=pltpu.CompilerParams(
            dimension_semantics=("arbitrary",),
            vmem_limit_bytes=_vmem_limit(est)),
        name="fourier_mix",
    )(f, cc, sc, w2)


def _rope_tables(seq):
    rows = seq // GRID_W
    row = jnp.repeat(jnp.arange(rows, dtype=F32), GRID_W)
    col = jnp.broadcast_to(jnp.arange(GRID_W, dtype=F32), (rows, GRID_W)).reshape(-1)
    n_freq = HEAD_DIM // 4
    inv = ROPE_THETA ** (-jnp.arange(n_freq, dtype=F32) / n_freq)
    ang = jnp.stack([row, col], axis=-1)[:, :, None] * inv
    cos, sin = jnp.cos(ang), jnp.sin(ang)
    cos_t = jnp.concatenate([cos[:, 0], cos[:, 1], cos[:, 0], cos[:, 1]], axis=-1)
    sin_t = jnp.concatenate([-sin[:, 0], -sin[:, 1], sin[:, 0], sin[:, 1]], axis=-1)
    return cos_t, sin_t


def _rope_head_order(m):
    lead = m.shape[:-1]
    nh = m.shape[-1] // HEAD_DIM
    m = m.reshape(lead + (nh, 2, 2, HEAD_DIM // 4))
    return jnp.swapaxes(m, -3, -2).reshape(lead + (nh * HEAD_DIM,))


def _dft_tables(seq):
    def tw(n):
        i = np.arange(n, dtype=np.int64)
        ang = ((i[:, None] * i[None, :]) % n).astype(np.float64) * (2.0 * math.pi / n)
        return np.cos(ang), np.sin(ang)

    cc, sc = tw(FN_GDIM)
    cl, sl = tw(seq)
    w2 = np.concatenate([cl, -sl], axis=1).astype(np.float32)
    return (jnp.asarray(cc.astype(np.float32), BF16), jnp.asarray(sc.astype(np.float32), BF16),
            jnp.asarray(w2, BF16))


def _ffn_params(w_up, conv_w, conv_b, w_down):
    d = w_up.shape[0]
    nc = D_FF // FF_CHUNK

    def cols(m, lo):
        r = m.shape[0]
        return m[:, lo:lo + D_FF].reshape(r, nc, FF_CHUNK).transpose(1, 0, 2)

    wu = w_up.astype(BF16)
    cb = conv_b.reshape(1, 2 * D_FF)
    return (cols(wu, 0), cols(wu, D_FF), cols(conv_w, 0), 0.5 * cols(conv_w, D_FF),
            cols(cb, 0), 0.5 * cols(cb, D_FF), w_down.astype(BF16).reshape(nc, FF_CHUNK, d))


def kernel(x, c, ctx, c_ctx, mod_w, mod_b, ln1_g, ln1_b, ln2_g, ln2_b, ffn_up, ffn_conv_w, ffn_conv_b,
           ffn_down, ab_w_in, ab_w_out, gm_ln_g, gm_ln_b, gm_ws, gm_bs, q_norm_g, k_norm_g, cd_w_in,
           cd_w_out, ssd_conv_w, ssd_conv_b, ssd_dt_bias, ssd_a_log, ssd_d, ssd_norm_g):
    bsz, seq, d = x.shape
    ctx_len = ctx.shape[1]
    lat_rows = min(1024, seq)
    out_rows = min(2048, seq)
    dft_rows = min(512, seq)
    ctx_rows = min(256, ctx_len)

    pad = (-(bsz + 1)) % V7X_SUBLANES
    cc = jnp.concatenate([c, c_ctx[None], jnp.zeros((pad, d), F32)], axis=0)
    mod = _modulation(cc, mod_w, mod_b)
    rope_tabs = _rope_tables(seq)

    def row(v):
        return v.reshape(1, -1)

    xl, xc = x, ctx
    for i in range(DEPTH):
        last = i == DEPTH - 1
        j = i // 2
        ml = mod[i, :bsz].reshape(bsz, 6, 1, d)
        mc = jnp.broadcast_to(mod[i, bsz].reshape(1, 6, 1, d), (bsz, 6, 1, d))
        sh1, sc1, g1, sh2, sc2, g2 = (ml[:, k] for k in range(6))
        csh1, csc1, cg1, csh2, csc2, cg2 = (mc[:, k] for k in range(6))
        ln1 = (row(ln1_g[i]), row(ln1_b[i]))
        ln2 = (row(ln2_g[i]), row(ln2_b[i]))
        ffn_p = _ffn_params(ffn_up[i], ffn_conv_w[i], ffn_conv_b[i], ffn_down[i])

        if i % 2 == 0:
            w_in = ab_w_in[j].astype(BF16)
            qk_lo, qk_hi = 2 * GM_WIDTH, KV_OFF + KV_W
            q_scale = HEAD_DIM ** -0.5 * math.log2(math.e)
            w_in = jnp.concatenate([w_in[:, :qk_lo], _rope_head_order(w_in[:, qk_lo:qk_hi]), w_in[:, qk_hi:]],
                                   axis=1)
            w_out = ab_w_out[j].astype(BF16)
            gm = (row(gm_ln_g[j]), row(gm_ln_b[j]), gm_ws[j].astype(BF16),
                  jnp.repeat(gm_bs[j].T, GM_WIDTH // GM_GROUPS, axis=1),
                  row(_rope_head_order(q_norm_g[j]) * q_scale), row(_rope_head_order(k_norm_g[j])))
            kv_len = seq + ctx_len
            a_c, q_c, k_all, v_all = _ab_in(xc, csh1, csc1, w_in, *gm, None, ctx_rows, kv_len, seq)
            a_l, q_l, k_all, v_all = _ab_in(xl, sh1, sc1, w_in, *gm, rope_tabs, lat_rows, kv_len, 0,
                                            kv_into=(k_all, v_all))
            mix_l = (a_l, _attention(q_l, k_all, v_all, 0, kv_len, lat_rows))
            if not last:
                mix_c = (a_c, _attention(q_c, k_all, v_all, seq, ctx_len, ctx_rows))
        else:
            w_in = cd_w_in[j].astype(BF16)
            w_out = cd_w_out[j].astype(BF16)
            wfz = w_in[:, :SSM_OFF]
            wx = w_in[:, SSM_OFF:SSM_OFF + SSD_CONV_DIM]
            ndt = 2 * SSD_HEADS
            wdt = jnp.pad(w_in[:, SSM_OFF + SSD_CONV_DIM:], ((0, 0), (0, V7X_LANES - ndt)))
            dtb = jnp.pad(ssd_dt_bias[j].reshape(1, ndt), ((0, 0), (0, V7X_LANES - ndt)))
            alog = jnp.pad(ssd_a_log[j].reshape(1, ndt), ((0, 0), (0, V7X_LANES - ndt)))
            cw, cb = 0.5 * ssd_conv_w[j], 0.5 * row(ssd_conv_b[j])
            dskip = row(jnp.repeat(ssd_d[j], SSD_HEADDIM))
            f_l, z_l, xs_l, b_l, c_l, dt_l = _cd_in(xl, sh1, sc1, wfz, wx, wdt, cw, cb, dtb, lat_rows)
            if last:
                xs_c, b_c, _, dt_c = _cd_in(xc, csh1, csc1, None, wx, wdt, cw, cb, dtb, ctx_rows)
            else:
                raise NotImplementedError("odd layers with a context output are not part of this block")
            s_l = _ssd(xs_c, b_c, dt_c, xs_l, b_l, c_l, dt_l, z_l, alog, dskip, row(ssd_norm_g[j]))
            mix_l = (_fourier(f_l, *_dft_tables(seq), dft_rows), s_l)

        xl = _out_ln(*mix_l, w_out, xl, g1, *ln1, out_rows)
        xl = _ffn(xl, sh2, sc2, g2, *ffn_p, *ln2, lat_rows)
        if not last:
            xc = _out_ln(*mix_c, w_out, xc, cg1, *ln1, ctx_rows)
            xc = _ffn(xc, csh2, csc2, cg2, *ffn_p, *ln2, ctx_rows)
    return xl
```

```python
import functools
import math

import jax
import jax.numpy as jnp
import numpy as np
from jax import lax
from jax.experimental import pallas as pl
from jax.experimental.pallas import tpu as pltpu

D_MODEL = 1024
DEPTH = 2
GRID_W = 64
CHUNK = 128
GM_GROUPS = 4
GM_WIDTH = 512
HEAD_DIM = 128
N_Q_HEADS = 4
N_KV_HEADS = 2
ATTN_WIDTH = 512
KV_W = 256
ROPE_THETA = 10000.0
FN_GROUPS = 4
FN_GDIM = 128
FN_WIDTH = 512
SSD_HEADDIM = 64
SSD_HEADS = 8
SSD_GROUPS = 2
SSD_HPG = 4
SSD_STATE = 128
SSD_WIDTH = 512
SSD_GN = 256
SSD_CONV_DIM = 1024
D_FF = 2816
CONV_W = 3
KV_OFF = 2 * GM_WIDTH + ATTN_WIDTH
SSM_OFF = FN_WIDTH + SSD_WIDTH
DN_ALPHA = (2 * DEPTH) ** 0.25
EPS = 1e-6

V7X_LANES = 128
V7X_SUBLANES = 8
V7X_VMEM_BYTES = 64 * 1024 * 1024
VMEM_CHIP_HEADROOM = 3 << 20
VMEM_TEMP_ALLOWANCE = 8 << 20
MOD_COLS = 2048
AB_SUB = 512
FF_CHUNK = 256
SSD_CHUNK = 256

BF16 = jnp.bfloat16
F32 = jnp.float32
HALO = V7X_SUBLANES


def _vmem_limit(nbytes):
    return int(min(nbytes * 3 // 2 + VMEM_TEMP_ALLOWANCE, V7X_VMEM_BYTES - VMEM_CHIP_HEADROOM))


def _resident(shape):
    nd = len(shape)
    return pl.BlockSpec(shape, lambda *_: (0,) * nd, pipeline_mode=pl.Buffered(1))


def _dot(a, b):
    return jnp.dot(a, b, preferred_element_type=F32)


def _dot_nt(a, b):
    return lax.dot_general(a, b, (((1,), (1,)), ((), ())), preferred_element_type=F32)


def _layer_norm(z, g, b, eps=EPS):
    mu = jnp.mean(z, axis=-1, keepdims=True)
    zc = z - mu
    var = jnp.mean(zc * zc, axis=-1, keepdims=True)
    return zc * lax.rsqrt(var + eps) * g + b


def _deepnorm(x, gate, y, g, b):
    return _layer_norm(x + (gate * (1.0 / DN_ALPHA)) * y, g, b, eps=EPS / DN_ALPHA ** 2)


def _rms_norm(z, g):
    return z * lax.rsqrt(jnp.mean(z * z, axis=-1, keepdims=True) + EPS) * g


def _softplus(z):
    return jnp.maximum(z, 0.0) + jnp.log1p(jnp.exp(-jnp.abs(z)))


def _gelu_tanh(z):
    c1 = math.sqrt(2.0 / math.pi)
    h = 0.5 * z
    return h + h * jnp.tanh(z * (c1 + (c1 * 0.044715) * (z * z)))


def _silu(z):
    h = 0.5 * z
    return h + h * jnp.tanh(h)


def _conv3(u, w, b, rows):
    tiles, c = rows // V7X_SUBLANES, u.shape[1]
    u3 = u.reshape(tiles + 2, V7X_SUBLANES, c)
    sub = lax.broadcasted_iota(jnp.int32, (tiles, V7X_SUBLANES, c), 1)
    dn = pltpu.roll(u3, 1, 1)
    up = pltpu.roll(u3, V7X_SUBLANES - 1, 1)
    prev = jnp.where(sub == 0, dn[0:tiles], dn[1:tiles + 1])
    nxt = jnp.where(sub == V7X_SUBLANES - 1, up[2:tiles + 2], up[1:tiles + 1])
    return (b + w[0:1] * prev + w[1:2] * u3[1:tiles + 1] + w[2:3] * nxt).reshape(rows, c)


def _fill_halo_tile(hm_ref, x_ref, xp_ref, xn_ref, sh, sc, rows):
    t = pl.program_id(1)
    nt = pl.num_programs(1)
    keep_prev = jnp.where(t > 0, 1.0, 0.0)
    keep_next = jnp.where(t < nt - 1, 1.0, 0.0)
    hm_ref[HALO:HALO + rows, :] = (x_ref[0] * (1.0 + sc) + sh).astype(BF16)
    hm_ref[0:HALO, :] = ((xp_ref[0] * (1.0 + sc) + sh) * keep_prev).astype(BF16)
    hm_ref[HALO + rows:2 * HALO + rows, :] = ((xn_ref[0] * (1.0 + sc) + sh) * keep_next).astype(BF16)


def _halo_specs(rows, seq, d):
    nb = rows // HALO
    last = seq // HALO - 1
    return [
        pl.BlockSpec((1, rows, d), lambda b, t: (b, t, 0)),
        pl.BlockSpec((1, HALO, d), lambda b, t: (b, jnp.maximum(t * nb - 1, 0), 0)),
        pl.BlockSpec((1, HALO, d), lambda b, t: (b, jnp.minimum((t + 1) * nb, last), 0)),
    ]


def _row_spec(d):
    return pl.BlockSpec((1, 1, d), lambda b, t: (b, 0, 0))


def _mod_kernel(c_ref, w_ref, b_ref, o_ref):
    c = c_ref[...]
    act = c * jax.nn.sigmoid(c)
    o_ref[0] = jnp.dot(act, w_ref[0], preferred_element_type=F32,
                       precision=lax.Precision.HIGHEST) + b_ref[0]


def _modulation(cc, mod_w, mod_b):
    rows, d = cc.shape
    depth, _, n = mod_w.shape
    tn = MOD_COLS
    return pl.pallas_call(
        _mod_kernel,
        out_shape=jax.ShapeDtypeStruct((depth, rows, n), F32),
        grid=(depth, n // tn),
        in_specs=[
            pl.BlockSpec((rows, d), lambda i, j: (0, 0)),
            pl.BlockSpec((1, d, tn), lambda i, j: (i, 0, j)),
            pl.BlockSpec((1, 1, tn), lambda i, j: (i, 0, j)),
        ],
        out_specs=pl.BlockSpec((1, rows, tn), lambda i, j: (i, 0, j)),
        compiler_params=pltpu.CompilerParams(
            dimension_semantics=("arbitrary", "arbitrary"),
            vmem_limit_bytes=_vmem_limit(2 * d * tn * 4)),
        name="modulation",
    )(cc, mod_w, mod_b.reshape(depth, 1, n))


def _ab_in_kernel(*refs, use_rope, rows):
    (x_ref, sh_ref, sc_ref, w_ref, lng_ref, lnb_ref, ws_ref, bias_ref, qg_ref, kg_ref) = refs[:10]
    if use_rope:
        cos_ref, sin_ref = refs[10:12]
    a_ref, q_ref, k_ref, v_ref = refs[-4:]

    gd = GM_WIDTH // GM_GROUPS
    slab_rows = min(AB_SUB, rows)

    def slab(i, carry):
        r0 = pl.multiple_of(i * slab_rows, slab_rows)
        rs = pl.ds(r0, slab_rows)
        h = (x_ref[0, rs, :] * (1.0 + sc_ref[0]) + sh_ref[0]).astype(BF16)

        u = _gelu_tanh(_dot(h, w_ref[:, 0:GM_WIDTH]))
        vn = _layer_norm(_gelu_tanh(_dot(h, w_ref[:, GM_WIDTH:2 * GM_WIDTH])), lng_ref[...], lnb_ref[...])
        vb = vn.astype(BF16)
        for c in range(slab_rows // CHUNK):
            r = slice(c * CHUNK, (c + 1) * CHUNK)
            for g in range(GM_GROUPS):
                l = slice(g * gd, (g + 1) * gd)
                mixed = _dot(ws_ref[g], vb[r, l]) + bias_ref[:, l]
                a_ref[0, pl.ds(r0 + c * CHUNK, CHUNK), l] = (u[r, l] * mixed).astype(BF16)

        def rope(z):
            if not use_rope:
                return z
            return z * cos_ref[rs, :] + pltpu.roll(z, HEAD_DIM // 2, 1) * sin_ref[rs, :]

        pq = _dot(h, w_ref[:, 2 * GM_WIDTH:KV_OFF])
        for hh in range(N_Q_HEADS):
            l = slice(hh * HEAD_DIM, (hh + 1) * HEAD_DIM)
            q_ref[0, rs, l] = rope(_rms_norm(pq[:, l], qg_ref[...])).astype(BF16)
        pk = _dot(h, w_ref[:, KV_OFF:KV_OFF + KV_W])
        for hh in range(N_KV_HEADS):
            l = slice(hh * HEAD_DIM, (hh + 1) * HEAD_DIM)
            k_ref[0, rs, l] = rope(_rms_norm(pk[:, l], kg_ref[...])).astype(BF16)
        v_ref[0, rs, :] = _dot(h, w_ref[:, KV_OFF + KV_W:KV_OFF + 2 * KV_W]).astype(BF16)
        return carry

    lax.fori_loop(0, rows // slab_rows, slab, 0)


def _ab_in(x, sh, sc, w, lng, lnb, ws, bias2d, qg, kg, rope_tabs, rows, kv_len, kv_row0, kv_into=None):
    bsz, seq, d = x.shape
    assert kv_row0 % rows == 0
    kv_blk0 = kv_row0 // rows
    n = w.shape[1]
    use_rope = rope_tabs is not None
    in_specs = [
        pl.BlockSpec((1, rows, d), lambda b, t: (b, t, 0)),
        _row_spec(d), _row_spec(d),
        _resident((d, n)),
        _resident((1, GM_WIDTH)), _resident((1, GM_WIDTH)),
        _resident((GM_GROUPS, CHUNK, CHUNK)),
        _resident((CHUNK, GM_WIDTH)),
        _resident((1, HEAD_DIM)), _resident((1, HEAD_DIM)),
    ]
    args = [x, sh, sc, w, lng, lnb, ws, bias2d, qg, kg]
    if use_rope:
        in_specs += [pl.BlockSpec((rows, HEAD_DIM), lambda b, t: (t, 0))] * 2
        args += list(rope_tabs)
    aliases = {}
    if kv_into is not None:
        aliases = {len(args): 2, len(args) + 1: 3}
        in_specs += [pl.BlockSpec(memory_space=pl.ANY)] * 2
        args += list(kv_into)
    out_shape = [
        jax.ShapeDtypeStruct((bsz, seq, GM_WIDTH), BF16),
        jax.ShapeDtypeStruct((bsz, seq, ATTN_WIDTH), BF16),
        jax.ShapeDtypeStruct((bsz, kv_len, KV_W), BF16),
        jax.ShapeDtypeStruct((bsz, kv_len, KV_W), BF16),
    ]
    out_specs = [
        pl.BlockSpec((1, rows, GM_WIDTH), lambda b, t: (b, t, 0)),
        pl.BlockSpec((1, rows, ATTN_WIDTH), lambda b, t: (b, t, 0)),
        pl.BlockSpec((1, rows, KV_W), lambda b, t: (b, t + kv_blk0, 0)),
        pl.BlockSpec((1, rows, KV_W), lambda b, t: (b, t + kv_blk0, 0)),
    ]
    est = d * n * 2 + 2 * rows * d * 4 + 6 * rows * n * 4
    return pl.pallas_call(
        functools.partial(_ab_in_kernel, use_rope=use_rope, rows=rows),
        out_shape=out_shape,
        grid=(bsz, seq // rows),
        in_specs=in_specs,
        out_specs=out_specs,
        input_output_aliases=aliases,
        compiler_params=pltpu.CompilerParams(
            dimension_semantics=("arbitrary", "arbitrary"),
            vmem_limit_bytes=_vmem_limit(est)),
        name="ab_in_rope" if use_rope else "ab_in_ctx",
    )(*args)


def _attn_kernel(q_ref, k_ref, v_ref, o_ref, vt_ref, *, rows, blk):
    @pl.when(pl.program_id(2) == 0)
    def _():
        vt_ref[...] = v_ref[0].astype(F32).T

    k = k_ref[0]
    vt = vt_ref[...].astype(BF16)
    for h in range(N_Q_HEADS // N_KV_HEADS):
        l_ = slice(h * HEAD_DIM, (h + 1) * HEAD_DIM)
        for r0 in range(0, rows, blk):
            qt = q_ref[0, r0:r0 + blk, l_].astype(F32).T.astype(BF16)
            st = _dot(k, qt)
            m = jnp.max(st, axis=0, keepdims=True)
            p = jnp.exp2(st - m)
            l = jnp.sum(p, axis=0, keepdims=True)
            ot = _dot(vt, p.astype(BF16)) / l
            o_ref[0, r0:r0 + blk, l_] = ot.T.astype(BF16)


def _attention(q, k, v, k_row0, sk, rows, blk=512):
    bsz, sq, _ = q.shape
    assert k_row0 % sk == 0
    kb = k_row0 // sk
    gw = (N_Q_HEADS // N_KV_HEADS) * HEAD_DIM
    est = 4 * blk * sk * 12 + 4 * sk * HEAD_DIM * 2 + 4 * rows * gw * 2 + sk * HEAD_DIM * 4
    return pl.pallas_call(
        functools.partial(_attn_kernel, rows=rows, blk=min(blk, rows)),
        out_shape=jax.ShapeDtypeStruct((bsz, sq, ATTN_WIDTH), BF16),
        grid=(bsz, N_KV_HEADS, sq // rows),
        in_specs=[
            pl.BlockSpec((1, rows, gw), lambda b, h, t: (b, t, h)),
            pl.BlockSpec((1, sk, HEAD_DIM), lambda b, h, t: (b, kb, h)),
            pl.BlockSpec((1, sk, HEAD_DIM), lambda b, h, t: (b, kb, h)),
        ],
        out_specs=pl.BlockSpec((1, rows, gw), lambda b, h, t: (b, t, h)),
        scratch_shapes=[pltpu.VMEM((HEAD_DIM, sk), F32)],
        compiler_params=pltpu.CompilerParams(
            dimension_semantics=("arbitrary", "arbitrary", "arbitrary"),
            vmem_limit_bytes=_vmem_limit(est)),
        name="attention",
    )(q, k, v)


def _out_ln_kernel(a1_ref, a2_ref, w_ref, x_ref, g_ref, lng_ref, lnb_ref, o_ref):
    half = a1_ref.shape[2]
    y = _dot(a1_ref[0], w_ref[0:half, :]) + _dot(a2_ref[0], w_ref[half:2 * half, :])
    o_ref[0] = _deepnorm(x_ref[0], g_ref[0], y, lng_ref[...], lnb_ref[...])


def _out_ln(a1, a2, w, x, gate, lng, lnb, rows):
    bsz, seq, d = x.shape
    half = a1.shape[2]
    est = 2 * half * d * 2 + 4 * rows * d * 4 + 4 * rows * half * 2 + 2 * rows * d * 4
    return pl.pallas_call(
        _out_ln_kernel,
        out_shape=jax.ShapeDtypeStruct((bsz, seq, d), F32),
        grid=(bsz, seq // rows),
        in_specs=[
            pl.BlockSpec((1, rows, half), lambda b, t: (b, t, 0)),
            pl.BlockSpec((1, rows, half), lambda b, t: (b, t, 0)),
            _resident((2 * half, d)),
            pl.BlockSpec((1, rows, d), lambda b, t: (b, t, 0)),
            _row_spec(d),
            _resident((1, d)), _resident((1, d)),
        ],
        out_specs=pl.BlockSpec((1, rows, d), lambda b, t: (b, t, 0)),
        compiler_params=pltpu.CompilerParams(
            dimension_semantics=("arbitrary", "arbitrary"),
            vmem_limit_bytes=_vmem_limit(est)),
        name="out_ln",
    )(a1, a2, w, x, gate, lng, lnb)


def _ffn_kernel(x_ref, xp_ref, xn_ref, sh_ref, sc_ref, g_ref, wu1_ref, wu2_ref, cw1_ref, cw2_ref,
                cb1_ref, cb2_ref, wd_ref, lng_ref, lnb_ref, o_ref, hm_ref, u_ref, gt_ref, *, rows):
    _fill_halo_tile(hm_ref, x_ref, xp_ref, xn_ref, sh_ref[0], sc_ref[0], rows)
    n_chunks = wu1_ref.shape[0]
    tiles = rows // V7X_SUBLANES
    sub = lax.broadcasted_iota(jnp.int32, (tiles, V7X_SUBLANES, FF_CHUNK), 1)

    def up_proj(c, slot):
        hm = hm_ref[...]
        u_ref[slot, 0] = _dot(hm, wu1_ref[c]).reshape(tiles + 2, V7X_SUBLANES, FF_CHUNK)
        u_ref[slot, 1] = _dot(hm, wu2_ref[c]).reshape(tiles + 2, V7X_SUBLANES, FF_CHUNK)

    def conv_gate(c, slot):
        def conv(half, cw_ref, cb_ref):
            u = u_ref[slot, half]
            dn = pltpu.roll(u, 1, 1)
            up = pltpu.roll(u, V7X_SUBLANES - 1, 1)
            prev = jnp.where(sub == 0, dn[0:tiles], dn[1:tiles + 1])
            nxt = jnp.where(sub == V7X_SUBLANES - 1, up[2:tiles + 2], up[1:tiles + 1])
            w = cw_ref[c]
            return cb_ref[c] + w[0:1] * prev + w[1:2] * u[1:tiles + 1] + w[2:3] * nxt

        a1 = conv(0, cw1_ref, cb1_ref)
        a2 = conv(1, cw2_ref, cb2_ref)
        gt_ref[c] = (a1 * (a2 + a2 * jnp.tanh(a2))).reshape(rows, FF_CHUNK)

    up_proj(0, 0)

    def pair(k, carry):
        c = 2 * k
        up_proj(c + 1, 1)
        conv_gate(c, 0)
        up_proj(c + 2, 0)
        conv_gate(c + 1, 1)
        return carry

    lax.fori_loop(0, (n_chunks - 1) // 2, pair, 0)
    conv_gate(n_chunks - 1, 0)

    f = _dot(gt_ref[0].astype(BF16), wd_ref[0])
    for c in range(1, n_chunks):
        f = f + _dot(gt_ref[c].astype(BF16), wd_ref[c])
    o_ref[0] = _deepnorm(x_ref[0], g_ref[0], f, lng_ref[...], lnb_ref[...])


def _ffn(x, sh, sc, gate, wu1, wu2, cw1, cw2, cb1, cb2, wd, lng, lnb, rows):
    bsz, seq, d = x.shape
    nc = wu1.shape[0]
    assert nc % 2 == 1, "the pipeline peels one chunk and pairs the rest"
    est = (3 * nc * d * FF_CHUNK * 2 + 4 * rows * d * 4 + (rows + 2 * HALO) * d * 2
           + nc * rows * FF_CHUNK * 4 + 8 * (rows + 2 * HALO) * FF_CHUNK * 4 + 2 * rows * d * 4)
    return pl.pallas_call(
        functools.partial(_ffn_kernel, rows=rows),
        out_shape=jax.ShapeDtypeStruct((bsz, seq, d), F32),
        grid=(bsz, seq // rows),
        in_specs=_halo_specs(rows, seq, d) + [
            _row_spec(d), _row_spec(d), _row_spec(d),
            _resident((nc, d, FF_CHUNK)), _resident((nc, d, FF_CHUNK)),
            _resident((nc, CONV_W, FF_CHUNK)), _resident((nc, CONV_W, FF_CHUNK)),
            _resident((nc, 1, FF_CHUNK)), _resident((nc, 1, FF_CHUNK)),
            _resident((nc, FF_CHUNK, d)),
            _resident((1, d)), _resident((1, d)),
        ],
        out_specs=pl.BlockSpec((1, rows, d), lambda b, t: (b, t, 0)),
        scratch_shapes=[
            pltpu.VMEM((rows + 2 * HALO, d), BF16),
            pltpu.VMEM((2, 2, rows // HALO + 2, HALO, FF_CHUNK), F32),
            pltpu.VMEM((nc, rows, FF_CHUNK), F32),
        ],
        compiler_params=pltpu.CompilerParams(
            dimension_semantics=("arbitrary", "arbitrary"),
            vmem_limit_bytes=_vmem_limit(est)),
        name="conv_ffn",
    )(x, x, x, sh, sc, gate, wu1, wu2, cw1, cw2, cb1, cb2, wd, lng, lnb)


def _cd_in_kernel(*refs, with_fz, rows):
    x_ref, xp_ref, xn_ref, sh_ref, sc_ref = refs[:5]
    if with_fz:
        wfz_ref, wx_ref, wdt_ref, cw_ref, cb_ref, dtb_ref = refs[5:11]
        f_ref, z_ref, xs_ref, b_ref, c_ref, dt_ref, hm_ref = refs[11:]
    else:
        wx_ref, wdt_ref, cw_ref, cb_ref, dtb_ref = refs[5:10]
        xs_ref, b_ref, c_ref, dt_ref, hm_ref = refs[10:]
    _fill_halo_tile(hm_ref, x_ref, xp_ref, xn_ref, sh_ref[0], sc_ref[0], rows)
    hc = hm_ref[HALO:HALO + rows, :]
    if with_fz:
        pfz = _dot(hc, wfz_ref[...])
        f_ref[0] = pfz[:, 0:FN_WIDTH].astype(BF16)
        z_ref[0] = pfz[:, FN_WIDTH:SSM_OFF]
    a = _conv3(_dot(hm_ref[...], wx_ref[...]), cw_ref[...], cb_ref[...], rows)
    s = a + a * jnp.tanh(a)
    xs_ref[0] = s[:, 0:SSD_WIDTH]
    b_ref[0] = s[:, SSD_WIDTH:SSD_WIDTH + SSD_GN].astype(BF16)
    c_ref[0] = s[:, SSD_WIDTH + SSD_GN:SSD_CONV_DIM].astype(BF16)
    dt_ref[0] = _softplus(_dot(hc, wdt_ref[...]) + dtb_ref[...])


def _cd_in(x, sh, sc, wfz, wx, wdt, cw, cb, dtb, rows):
    bsz, seq, d = x.shape
    with_fz = wfz is not None
    in_specs = _halo_specs(rows, seq, d) + [_row_spec(d), _row_spec(d)]
    args = [x, x, x, sh, sc]
    out_shape, out_specs = [], []
    if with_fz:
        in_specs.append(_resident((d, SSM_OFF)))
        args.append(wfz)
        out_shape += [jax.ShapeDtypeStruct((bsz, seq, FN_WIDTH), BF16),
                      jax.ShapeDtypeStruct((bsz, seq, SSD_WIDTH), F32)]
        out_specs += [pl.BlockSpec((1, rows, FN_WIDTH), lambda b, t: (b, t, 0)),
                      pl.BlockSpec((1, rows, SSD_WIDTH), lambda b, t: (b, t, 0))]
    in_specs += [_resident((d, SSD_CONV_DIM)), _resident((d, V7X_LANES)),
                 _resident((CONV_W, SSD_CONV_DIM)), _resident((1, SSD_CONV_DIM)),
                 _resident((1, V7X_LANES))]
    args += [wx, wdt, cw, cb, dtb]
    out_shape += [jax.ShapeDtypeStruct((bsz, seq, SSD_WIDTH), F32),
                  jax.ShapeDtypeStruct((bsz, seq, SSD_GN), BF16),
                  jax.ShapeDtypeStruct((bsz, seq, SSD_GN), BF16),
                  jax.ShapeDtypeStruct((bsz, seq, V7X_LANES), F32)]
    out_specs += [pl.BlockSpec((1, rows, SSD_WIDTH), lambda b, t: (b, t, 0)),
                  pl.BlockSpec((1, rows, SSD_GN), lambda b, t: (b, t, 0)),
                  pl.BlockSpec((1, rows, SSD_GN), lambda b, t: (b, t, 0)),
                  pl.BlockSpec((1, rows, V7X_LANES), lambda b, t: (b, t, 0))]
    est = d * 2200 * 2 + 2 * rows * d * 4 + 8 * (rows + 2 * HALO) * SSD_CONV_DIM * 4
    return pl.pallas_call(
        functools.partial(_cd_in_kernel, with_fz=with_fz, rows=rows),
        out_shape=out_shape,
        grid=(bsz, seq // rows),
        in_specs=in_specs,
        out_specs=out_specs,
        scratch_shapes=[pltpu.VMEM((rows + 2 * HALO, d), BF16)],
        compiler_params=pltpu.CompilerParams(
            dimension_semantics=("arbitrary", "arbitrary"),
            vmem_limit_bytes=_vmem_limit(est)),
        name="cd_in" if with_fz else "cd_in_ctx",
    )(*args)


def _cumsum_rows(z, reverse):
    n = z.shape[0]
    idx = lax.broadcasted_iota(jnp.int32, z.shape, 0)
    k = 1
    while k < n:
        if reverse:
            z = z + jnp.where(idx < n - k, pltpu.roll(z, n - k, 0), 0.0)
        else:
            z = z + jnp.where(idx >= k, pltpu.roll(z, k, 0), 0.0)
        k *= 2
    return z


def _lane_expand(row, lane0, width):
    lane = lax.broadcasted_iota(jnp.int32, (1, SSD_HPG * width), 1)
    out = jnp.zeros((1, SSD_HPG * width), F32)
    for j in range(SSD_HPG):
        val = jnp.broadcast_to(row[:, lane0 + j:lane0 + j + 1], (1, SSD_HPG * width))
        out = jnp.where(lane // width == j, val, out)
    return out


def _ssd_chunk(x, bm, cm, dt, a_row, h_ref, lane0, reverse, need_y):
    q = x.shape[0]
    gw = SSD_HPG * SSD_HEADDIM
    acum = _cumsum_rows(dt * a_row, reverse)
    total = acum[0:1, :] if reverse else acum[q - 1:q, :]
    acum_t = acum.T
    dt_t = dt.T
    total_t = jnp.broadcast_to(acum_t[:, 0:1] if reverse else acum_t[:, q - 1:q], acum_t.shape)
    wgt_t = dt_t * jnp.exp2(total_t - acum_t)
    lane_g = lax.broadcasted_iota(jnp.int32, (q, gw), 1) // SSD_HEADDIM
    lane_gh = lax.broadcasted_iota(jnp.int32, (SSD_STATE, gw), 1) // SSD_HEADDIM
    if need_y:
        srow_t = acum_t - jnp.log2(dt_t)
        ti = lax.broadcasted_iota(jnp.int32, (q, q), 0)
        si = lax.broadcasted_iota(jnp.int32, (q, q), 1)
        mask = (si >= ti) if reverse else (si <= ti)
    ys = []
    for g in range(SSD_GROUPS):
        bg = bm[:, g * SSD_STATE:(g + 1) * SSD_STATE]
        xgb = x[:, g * gw:(g + 1) * gw].astype(BF16)
        x_heads = [jnp.where(lane_g == j, xgb, jnp.zeros_like(xgb)) for j in range(SSD_HPG)]
        hg = h_ref[g]
        if need_y:
            cg = cm[:, g * SSD_STATE:(g + 1) * SSD_STATE]
            cb = _dot_nt(cg, bg)
            cgf = cg.astype(F32)
            hgb = hg.astype(BF16)
            lhs, rhs = [], []
            for j in range(SSD_HPG):
                jl = lane0 + g * SSD_HPG + j
                col = jnp.broadcast_to(acum[:, jl:jl + 1], (q, q))
                row = jnp.broadcast_to(srow_t[jl:jl + 1, :], (q, q))
                lhs.append((cb * jnp.exp2(jnp.where(mask, col - row, -jnp.inf))).astype(BF16))
                lhs.append((cgf * jnp.exp2(col[:, 0:SSD_STATE])).astype(BF16))
                rhs.append(x_heads[j])
                rhs.append(jnp.where(lane_gh == j, hgb, jnp.zeros_like(hgb)))
            ys.append(_dot(jnp.concatenate(lhs, axis=1), jnp.concatenate(rhs, axis=0)))
        bgt = bg.astype(F32).T
        wb = [(bgt * jnp.broadcast_to(wgt_t[lane0 + g * SSD_HPG + j:lane0 + g * SSD_HPG + j + 1, :],
                                      (SSD_STATE, q))).astype(BF16) for j in range(SSD_HPG)]
        st = _dot(jnp.concatenate(wb, axis=1), jnp.concatenate(x_heads, axis=0))
        h_ref[g] = hg * jnp.exp2(_lane_expand(total, lane0 + g * SSD_HPG, SSD_HEADDIM)) + st
    if need_y:
        return jnp.concatenate(ys, axis=1)
    return None


def _ssd_kernel(xc_ref, bc_ref, dtc_ref, xl_ref, bl_ref, cl_ref, dtl_ref, z_ref, alog_ref,
                dskip_ref, ng_ref, o_ref, y_ref, h_ref):
    a_row = -jnp.exp(alog_ref[...]) * math.log2(math.e)
    nc_ctx = xc_ref.shape[1] // SSD_CHUNK
    nc_lat = xl_ref.shape[1] // SSD_CHUNK

    def rows_of(c):
        return pl.ds(pl.multiple_of(c * SSD_CHUNK, SSD_CHUNK), SSD_CHUNK)

    def ctx_step(c, direction):
        r = rows_of(c)
        _ssd_chunk(xc_ref[0, r, :], bc_ref[0, r, :], None, dtc_ref[0, r, :], a_row,
                   h_ref.at[direction], direction * SSD_HEADS, direction == 1, False)

    def lat_step(c, direction):
        r = rows_of(c)
        y_ref[direction, r, :] = _ssd_chunk(
            xl_ref[0, r, :], bl_ref[0, r, :], cl_ref[0, r, :], dtl_ref[0, r, :], a_row,
            h_ref.at[direction], direction * SSD_HEADS, direction == 1, True)

    h_ref[...] = jnp.zeros_like(h_ref)

    def ctx_pair(i, carry):
        ctx_step(i, 0)
        ctx_step(nc_ctx - 1 - i, 1)
        return carry

    lax.fori_loop(0, nc_ctx, ctx_pair, 0)

    def lat_pair(i, carry):
        lat_step(i, 0)
        lat_step(nc_lat - 1 - i, 1)
        return carry

    lax.fori_loop(0, nc_lat, lat_pair, 0)

    def finish(c, carry):
        r = rows_of(c)
        zz = z_ref[0, r, :]
        yt = (y_ref[0, r, :] + y_ref[1, r, :] + xl_ref[0, r, :] * dskip_ref[...]) * _silu(zz)
        o_ref[0, r, :] = _rms_norm(yt, ng_ref[...]).astype(BF16)
        return carry

    lax.fori_loop(0, nc_lat, finish, 0)


def _ssd(xs_c, b_c, dt_c, xs_l, b_l, c_l, dt_l, z, alog_row, dskip_row, ng_row):
    bsz, sl, w = xs_l.shape
    sc = xs_c.shape[1]
    assert sl % SSD_CHUNK == 0 and sc % SSD_CHUNK == 0
    gw = SSD_HPG * SSD_HEADDIM

    def full(s, n):
        return pl.BlockSpec((1, s, n), lambda b: (b, 0, 0))

    est = (2 * (sl + sc) * (w * 4 + 2 * SSD_GN * 2 + V7X_LANES * 4) + 2 * sl * w * 4
           + 2 * sl * w * 2 + 2 * sl * w * 4)
    return pl.pallas_call(
        _ssd_kernel,
        out_shape=jax.ShapeDtypeStruct((bsz, sl, w), BF16),
        grid=(bsz,),
        in_specs=[
            full(sc, w), full(sc, SSD_GN), full(sc, V7X_LANES),
            full(sl, w), full(sl, SSD_GN), full(sl, SSD_GN), full(sl, V7X_LANES),
            full(sl, w),
            _resident((1, V7X_LANES)), _resident((1, w)), _resident((1, w)),
        ],
        out_specs=full(sl, w),
        scratch_shapes=[
            pltpu.VMEM((2, sl, w), F32),
            pltpu.VMEM((2, SSD_GROUPS, SSD_STATE, gw), F32),
        ],
        compiler_params=pltpu.CompilerParams(
            dimension_semantics=("arbitrary",),
            vmem_limit_bytes=_vmem_limit(est)),
        name="ssd_scan",
    )(xs_c, b_c, dt_c, xs_l, b_l, c_l, dt_l, z, alog_row, dskip_row, ng_row)


def _fourier_kernel(f_ref, cc_ref, sc_ref, w2_ref, o_ref, y_ref, *, rows):
    seq = f_ref.shape[1]
    for g in range(FN_GROUPS):
        l = slice(g * FN_GDIM, (g + 1) * FN_GDIM)
        fg = f_ref[0, :, l]
        y_ref[0:seq, l] = _dot(fg, cc_ref[...]).astype(BF16)
        y_ref[seq:2 * seq, l] = _dot(fg, sc_ref[...]).astype(BF16)
    norm = 1.0 / math.sqrt(seq * FN_GDIM)

    def tile(i, carry):
        r = pl.ds(pl.multiple_of(i * rows, rows), rows)
        o_ref[0, r, :] = (_dot(w2_ref[r, :], y_ref[...]) * norm).astype(BF16)
        return carry

    lax.fori_loop(0, seq // rows, tile, 0)


def _fourier(f, cc, sc, w2, rows):
    bsz, seq, w = f.shape
    est = seq * 2 * seq * 2 + 4 * seq * w * 2 + 2 * seq * w * 2 + 4 * rows * 2 * seq * 2
    return pl.pallas_call(
        functools.partial(_fourier_kernel, rows=rows),
        out_shape=jax.ShapeDtypeStruct((bsz, seq, w), BF16),
        grid=(bsz,),
        in_specs=[
            pl.BlockSpec((1, seq, w), lambda b: (b, 0, 0)),
            _resident((FN_GDIM, FN_GDIM)), _resident((FN_GDIM, FN_GDIM)),
            _resident((seq, 2 * seq)),
        ],
        out_specs=pl.BlockSpec((1, seq, w), lambda b: (b, 0, 0)),
        scratch_shapes=[pltpu.VMEM((2 * seq, w), BF16)],
        compiler_params=pltpu.CompilerParams(
            dimension_semantics=("arbitrary",),
            vmem_limit_bytes=_vmem_limit(est)),
        name="fourier_mix",
    )(f, cc, sc, w2)


def _rope_tables(seq):
    rows = seq // GRID_W
    row = jnp.repeat(jnp.arange(rows, dtype=F32), GRID_W)
    col = jnp.broadcast_to(jnp.arange(GRID_W, dtype=F32), (rows, GRID_W)).reshape(-1)
    n_freq = HEAD_DIM // 4
    inv = ROPE_THETA ** (-jnp.arange(n_freq, dtype=F32) / n_freq)
    ang = jnp.stack([row, col], axis=-1)[:, :, None] * inv
    cos, sin = jnp.cos(ang), jnp.sin(ang)
    cos_t = jnp.concatenate([cos[:, 0], cos[:, 1], cos[:, 0], cos[:, 1]], axis=-1)
    sin_t = jnp.concatenate([-sin[:, 0], -sin[:, 1], sin[:, 0], sin[:, 1]], axis=-1)
    return cos_t, sin_t


def _rope_head_order(m):
    lead = m.shape[:-1]
    nh = m.shape[-1] // HEAD_DIM
    m = m.reshape(lead + (nh, 2, 2, HEAD_DIM // 4))
    return jnp.swapaxes(m, -3, -2).reshape(lead + (nh * HEAD_DIM,))


def _dft_tables(seq):
    def tw(n):
        i = np.arange(n, dtype=np.int64)
        ang = ((i[:, None] * i[None, :]) % n).astype(np.float64) * (2.0 * math.pi / n)
        return np.cos(ang), np.sin(ang)

    cc, sc = tw(FN_GDIM)
    cl, sl = tw(seq)
    w2 = np.concatenate([cl, -sl], axis=1).astype(np.float32)
    return (jnp.asarray(cc.astype(np.float32), BF16), jnp.asarray(sc.astype(np.float32), BF16),
            jnp.asarray(w2, BF16))


def _ffn_params(w_up, conv_w, conv_b, w_down):
    d = w_up.shape[0]
    nc = D_FF // FF_CHUNK

    def cols(m, lo):
        r = m.shape[0]
        return m[:, lo:lo + D_FF].reshape(r, nc, FF_CHUNK).transpose(1, 0, 2)

    wu = w_up.astype(BF16)
    cb = conv_b.reshape(1, 2 * D_FF)
    return (cols(wu, 0), cols(wu, D_FF), cols(conv_w, 0), 0.5 * cols(conv_w, D_FF),
            cols(cb, 0), 0.5 * cols(cb, D_FF), w_down.astype(BF16).reshape(nc, FF_CHUNK, d))


def kernel(x, c, ctx, c_ctx, mod_w, mod_b, ln1_g, ln1_b, ln2_g, ln2_b, ffn_up, ffn_conv_w, ffn_conv_b,
           ffn_down, ab_w_in, ab_w_out, gm_ln_g, gm_ln_b, gm_ws, gm_bs, q_norm_g, k_norm_g, cd_w_in,
           cd_w_out, ssd_conv_w, ssd_conv_b, ssd_dt_bias, ssd_a_log, ssd_d, ssd_norm_g):
    bsz, seq, d = x.shape
    ctx_len = ctx.shape[1]
    lat_rows = min(1024, seq)
    out_rows = min(2048, seq)
    dft_rows = min(512, seq)
    ctx_rows = min(256, ctx_len)

    pad = (-(bsz + 1)) % V7X_SUBLANES
    cc = jnp.concatenate([c, c_ctx[None], jnp.zeros((pad, d), F32)], axis=0)
    mod = _modulation(cc, mod_w, mod_b)
    rope_tabs = _rope_tables(seq)

    def row(v):
        return v.reshape(1, -1)

    xl, xc = x, ctx
    for i in range(DEPTH):
        last = i == DEPTH - 1
        j = i // 2
        ml = mod[i, :bsz].reshape(bsz, 6, 1, d)
        mc = jnp.broadcast_to(mod[i, bsz].reshape(1, 6, 1, d), (bsz, 6, 1, d))
        sh1, sc1, g1, sh2, sc2, g2 = (ml[:, k] for k in range(6))
        csh1, csc1, cg1, csh2, csc2, cg2 = (mc[:, k] for k in range(6))
        ln1 = (row(ln1_g[i]), row(ln1_b[i]))
        ln2 = (row(ln2_g[i]), row(ln2_b[i]))
        ffn_p = _ffn_params(ffn_up[i], ffn_conv_w[i], ffn_conv_b[i], ffn_down[i])

        if i % 2 == 0:
            w_in = ab_w_in[j].astype(BF16)
            qk_lo, qk_hi = 2 * GM_WIDTH, KV_OFF + KV_W
            q_scale = HEAD_DIM ** -0.5 * math.log2(math.e)
            w_in = jnp.concatenate([w_in[:, :qk_lo], _rope_head_order(w_in[:, qk_lo:qk_hi]), w_in[:, qk_hi:]],
                                   axis=1)
            w_out = ab_w_out[j].astype(BF16)
            gm = (row(gm_ln_g[j]), row(gm_ln_b[j]), gm_ws[j].astype(BF16),
                  jnp.repeat(gm_bs[j].T, GM_WIDTH // GM_GROUPS, axis=1),
                  row(_rope_head_order(q_norm_g[j]) * q_scale), row(_rope_head_order(k_norm_g[j])))
            kv_len = seq + ctx_len
            a_c, q_c, k_all, v_all = _ab_in(xc, csh1, csc1, w_in, *gm, None, ctx_rows, kv_len, seq)
            a_l, q_l, k_all, v_all = _ab_in(xl, sh1, sc1, w_in, *gm, rope_tabs, lat_rows, kv_len, 0,
                                            kv_into=(k_all, v_all))
            mix_l = (a_l, _attention(q_l, k_all, v_all, 0, kv_len, lat_rows))
            if not last:
                mix_c = (a_c, _attention(q_c, k_all, v_all, seq, ctx_len, ctx_rows))
        else:
            w_in = cd_w_in[j].astype(BF16)
            w_out = cd_w_out[j].astype(BF16)
            wfz = w_in[:, :SSM_OFF]
            wx = w_in[:, SSM_OFF:SSM_OFF + SSD_CONV_DIM]
            ndt = 2 * SSD_HEADS
            wdt = jnp.pad(w_in[:, SSM_OFF + SSD_CONV_DIM:], ((0, 0), (0, V7X_LANES - ndt)))
            dtb = jnp.pad(ssd_dt_bias[j].reshape(1, ndt), ((0, 0), (0, V7X_LANES - ndt)))
            alog = jnp.pad(ssd_a_log[j].reshape(1, ndt), ((0, 0), (0, V7X_LANES - ndt)))
            cw, cb = 0.5 * ssd_conv_w[j], 0.5 * row(ssd_conv_b[j])
            dskip = row(jnp.repeat(ssd_d[j], SSD_HEADDIM))
            f_l, z_l, xs_l, b_l, c_l, dt_l = _cd_in(xl, sh1, sc1, wfz, wx, wdt, cw, cb, dtb, lat_rows)
            if last:
                xs_c, b_c, _, dt_c = _cd_in(xc, csh1, csc1, None, wx, wdt, cw, cb, dtb, ctx_rows)
            else:
                raise NotImplementedError("odd layers with a context output are not part of this block")
            s_l = _ssd(xs_c, b_c, dt_c, xs_l, b_l, c_l, dt_l, z_l, alog, dskip, row(ssd_norm_g[j]))
            mix_l = (_fourier(f_l, *_dft_tables(seq), dft_rows), s_l)

        xl = _out_ln(*mix_l, w_out, xl, g1, *ln1, out_rows)
        xl = _ffn(xl, sh2, sc2, g2, *ffn_p, *ln2, lat_rows)
        if not last:
            xc = _out_ln(*mix_c, w_out, xc, cg1, *ln1, ctx_rows)
            xc = _ffn(xc, csh2, csc2, cg2, *ffn_p, *ln2, ctx_rows)
    return xl
```

```python
import functools
import math

import jax
import jax.numpy as jnp
import numpy as np
from jax import lax
from jax.experimental import pallas as pl
from jax.experimental.pallas import tpu as pltpu

D_MODEL = 1024
DEPTH = 2
GRID_W = 64
CHUNK = 128
GM_GROUPS = 4
GM_WIDTH = 512
HEAD_DIM = 128
N_Q_HEADS = 4
N_KV_HEADS = 2
ATTN_WIDTH = 512
KV_W = 256
ROPE_THETA = 10000.0
FN_GROUPS = 4
FN_GDIM = 128
FN_WIDTH = 512
SSD_HEADDIM = 64
SSD_HEADS = 8
SSD_GROUPS = 2
SSD_HPG = 4
SSD_STATE = 128
SSD_WIDTH = 512
SSD_GN = 256
SSD_CONV_DIM = 1024
D_FF = 2816
CONV_W = 3
KV_OFF = 2 * GM_WIDTH + ATTN_WIDTH
SSM_OFF = FN_WIDTH + SSD_WIDTH
DN_ALPHA = (2 * DEPTH) ** 0.25
EPS = 1e-6

V7X_LANES = 128
V7X_SUBLANES = 8
V7X_VMEM_BYTES = 64 * 1024 * 1024
VMEM_CHIP_HEADROOM = 3 << 20
VMEM_TEMP_ALLOWANCE = 8 << 20
MOD_COLS = 2048
AB_SUB = 256
FF_CHUNK = 256
SSD_CHUNK = 256

BF16 = jnp.bfloat16
F32 = jnp.float32
HALO = V7X_SUBLANES


def _vmem_limit(nbytes):
    return int(min(nbytes * 3 // 2 + VMEM_TEMP_ALLOWANCE, V7X_VMEM_BYTES - VMEM_CHIP_HEADROOM))


def _resident(shape):
    nd = len(shape)
    return pl.BlockSpec(shape, lambda *_: (0,) * nd, pipeline_mode=pl.Buffered(1))


def _dot(a, b):
    return jnp.dot(a, b, preferred_element_type=F32)


def _dot_nt(a, b):
    return lax.dot_general(a, b, (((1,), (1,)), ((), ())), preferred_element_type=F32)


def _layer_norm(z, g, b, eps=EPS):
    mu = jnp.mean(z, axis=-1, keepdims=True)
    zc = z - mu
    var = jnp.mean(zc * zc, axis=-1, keepdims=True)
    return zc * lax.rsqrt(var + eps) * g + b


def _deepnorm(x, gate, y, g, b):
    return _layer_norm(x + (gate * (1.0 / DN_ALPHA)) * y, g, b, eps=EPS / DN_ALPHA ** 2)


def _rms_norm(z, g):
    return z * lax.rsqrt(jnp.mean(z * z, axis=-1, keepdims=True) + EPS) * g


def _softplus(z):
    return jnp.maximum(z, 0.0) + jnp.log1p(jnp.exp(-jnp.abs(z)))


def _gelu_tanh_of_half(h):
    c1 = math.sqrt(2.0 / math.pi)
    return h + h * jnp.tanh(h * (2.0 * c1 + (8.0 * c1 * 0.044715) * (h * h)))


def _silu(z):
    h = 0.5 * z
    return h + h * jnp.tanh(h)


def _conv3(u, w, b, rows):
    tiles, c = rows // V7X_SUBLANES, u.shape[1]
    u3 = u.reshape(tiles + 2, V7X_SUBLANES, c)
    sub = lax.broadcasted_iota(jnp.int32, (tiles, V7X_SUBLANES, c), 1)
    dn = pltpu.roll(u3, 1, 1)
    up = pltpu.roll(u3, V7X_SUBLANES - 1, 1)
    prev = jnp.where(sub == 0, dn[0:tiles], dn[1:tiles + 1])
    nxt = jnp.where(sub == V7X_SUBLANES - 1, up[2:tiles + 2], up[1:tiles + 1])
    return (b + w[0:1] * prev + w[1:2] * u3[1:tiles + 1] + w[2:3] * nxt).reshape(rows, c)


def _fill_halo_tile(hm_ref, x_ref, xp_ref, xn_ref, sh, sc, rows):
    t = pl.program_id(1)
    nt = pl.num_programs(1)
    keep_prev = jnp.where(t > 0, 1.0, 0.0)
    keep_next = jnp.where(t < nt - 1, 1.0, 0.0)
    hm_ref[HALO:HALO + rows, :] = (x_ref[0] * (1.0 + sc) + sh).astype(BF16)
    hm_ref[0:HALO, :] = ((xp_ref[0] * (1.0 + sc) + sh) * keep_prev).astype(BF16)
    hm_ref[HALO + rows:2 * HALO + rows, :] = ((xn_ref[0] * (1.0 + sc) + sh) * keep_next).astype(BF16)


def _halo_specs(rows, seq, d):
    nb = rows // HALO
    last = seq // HALO - 1
    return [
        pl.BlockSpec((1, rows, d), lambda b, t: (b, t, 0)),
        pl.BlockSpec((1, HALO, d), lambda b, t: (b, jnp.maximum(t * nb - 1, 0), 0)),
        pl.BlockSpec((1, HALO, d), lambda b, t: (b, jnp.minimum((t + 1) * nb, last), 0)),
    ]


def _row_spec(d):
    return pl.BlockSpec((1, 1, d), lambda b, t: (b, 0, 0))


def _mod_kernel(c_ref, w_ref, b_ref, o_ref):
    c = c_ref[...]
    act = c * jax.nn.sigmoid(c)
    o_ref[0] = jnp.dot(act, w_ref[0], preferred_element_type=F32,
                       precision=lax.Precision.HIGHEST) + b_ref[0]


def _modulation(cc, mod_w, mod_b):
    rows, d = cc.shape
    depth, _, n = mod_w.shape
    tn = MOD_COLS
    return pl.pallas_call(
        _mod_kernel,
        out_shape=jax.ShapeDtypeStruct((depth, rows, n), F32),
        grid=(depth, n // tn),
        in_specs=[
            pl.BlockSpec((rows, d), lambda i, j: (0, 0)),
            pl.BlockSpec((1, d, tn), lambda i, j: (i, 0, j)),
            pl.BlockSpec((1, 1, tn), lambda i, j: (i, 0, j)),
        ],
        out_specs=pl.BlockSpec((1, rows, tn), lambda i, j: (i, 0, j)),
        compiler_params=pltpu.CompilerParams(
            dimension_semantics=("arbitrary", "arbitrary"),
            vmem_limit_bytes=_vmem_limit(2 * d * tn * 4)),
        name="modulation",
    )(cc, mod_w, mod_b.reshape(depth, 1, n))


def _ab_in_kernel(*refs, use_rope, rows):
    (x_ref, sh_ref, sc_ref, w_ref, lng_ref, lnb_ref, ws_ref, bias_ref, qg_ref, kg_ref) = refs[:10]
    if use_rope:
        cos_ref, sin_ref = refs[10:12]
    a_ref, q_ref, k_ref, v_ref = refs[-4:]

    gd = GM_WIDTH // GM_GROUPS
    slab_rows = min(AB_SUB, rows)

    def slab(i, carry):
        r0 = pl.multiple_of(i * slab_rows, slab_rows)
        rs = pl.ds(r0, slab_rows)
        h = (x_ref[0, rs, :] * (1.0 + sc_ref[0]) + sh_ref[0]).astype(BF16)

        u = _gelu_tanh_of_half(_dot(h, w_ref[:, 0:GM_WIDTH]))
        vn = _layer_norm(_gelu_tanh_of_half(_dot(h, w_ref[:, GM_WIDTH:2 * GM_WIDTH])),
                         lng_ref[...], lnb_ref[...])
        vb = vn.astype(BF16)
        for c in range(slab_rows // CHUNK):
            r = slice(c * CHUNK, (c + 1) * CHUNK)
            for g in range(GM_GROUPS):
                l = slice(g * gd, (g + 1) * gd)
                mixed = _dot(ws_ref[g], vb[r, l]) + bias_ref[:, l]
                a_ref[0, pl.ds(r0 + c * CHUNK, CHUNK), l] = (u[r, l] * mixed).astype(BF16)

        def rope(z):
            if not use_rope:
                return z
            return z * cos_ref[rs, :] + pltpu.roll(z, HEAD_DIM // 2, 1) * sin_ref[rs, :]

        pq = _dot(h, w_ref[:, 2 * GM_WIDTH:KV_OFF])
        for hh in range(N_Q_HEADS):
            l = slice(hh * HEAD_DIM, (hh + 1) * HEAD_DIM)
            q_ref[0, rs, l] = rope(_rms_norm(pq[:, l], qg_ref[...])).astype(BF16)
        pk = _dot(h, w_ref[:, KV_OFF:KV_OFF + KV_W])
        for hh in range(N_KV_HEADS):
            l = slice(hh * HEAD_DIM, (hh + 1) * HEAD_DIM)
            k_ref[0, rs, l] = rope(_rms_norm(pk[:, l], kg_ref[...])).astype(BF16)
        v_ref[0, rs, :] = _dot(h, w_ref[:, KV_OFF + KV_W:KV_OFF + 2 * KV_W]).astype(BF16)
        return carry

    lax.fori_loop(0, rows // slab_rows, slab, 0)


def _ab_in(x, sh, sc, w, lng, lnb, ws, bias2d, qg, kg, rope_tabs, rows, kv_len, kv_row0, kv_into=None):
    bsz, seq, d = x.shape
    assert kv_row0 % rows == 0
    kv_blk0 = kv_row0 // rows
    n = w.shape[1]
    use_rope = rope_tabs is not None
    in_specs = [
        pl.BlockSpec((1, rows, d), lambda b, t: (b, t, 0)),
        _row_spec(d), _row_spec(d),
        _resident((d, n)),
        _resident((1, GM_WIDTH)), _resident((1, GM_WIDTH)),
        _resident((GM_GROUPS, CHUNK, CHUNK)),
        _resident((CHUNK, GM_WIDTH)),
        _resident((1, HEAD_DIM)), _resident((1, HEAD_DIM)),
    ]
    args = [x, sh, sc, w, lng, lnb, ws, bias2d, qg, kg]
    if use_rope:
        in_specs += [pl.BlockSpec((rows, HEAD_DIM), lambda b, t: (t, 0))] * 2
        args += list(rope_tabs)
    aliases = {}
    if kv_into is not None:
        aliases = {len(args): 2, len(args) + 1: 3}
        in_specs += [pl.BlockSpec(memory_space=pl.ANY)] * 2
        args += list(kv_into)
    out_shape = [
        jax.ShapeDtypeStruct((bsz, seq, GM_WIDTH), BF16),
        jax.ShapeDtypeStruct((bsz, seq, ATTN_WIDTH), BF16),
        jax.ShapeDtypeStruct((bsz, kv_len, KV_W), BF16),
        jax.ShapeDtypeStruct((bsz, kv_len, KV_W), BF16),
    ]
    out_specs = [
        pl.BlockSpec((1, rows, GM_WIDTH), lambda b, t: (b, t, 0)),
        pl.BlockSpec((1, rows, ATTN_WIDTH), lambda b, t: (b, t, 0)),
        pl.BlockSpec((1, rows, KV_W), lambda b, t: (b, t + kv_blk0, 0)),
        pl.BlockSpec((1, rows, KV_W), lambda b, t: (b, t + kv_blk0, 0)),
    ]
    est = d * n * 2 + 2 * rows * d * 4 + 6 * rows * n * 4
    return pl.pallas_call(
        functools.partial(_ab_in_kernel, use_rope=use_rope, rows=rows),
        out_shape=out_shape,
        grid=(bsz, seq // rows),
        in_specs=in_specs,
        out_specs=out_specs,
        input_output_aliases=aliases,
        compiler_params=pltpu.CompilerParams(
            dimension_semantics=("arbitrary", "arbitrary"),
            vmem_limit_bytes=_vmem_limit(est)),
        name="ab_in_rope" if use_rope else "ab_in_ctx",
    )(*args)


def _attn_kernel(q_ref, k_ref, v_ref, o_ref, vt_ref, *, rows, blk):
    @pl.when(pl.program_id(2) == 0)
    def _():
        vt_ref[...] = v_ref[0].astype(F32).T

    k = k_ref[0]
    vt = vt_ref[...].astype(BF16)
    for h in range(N_Q_HEADS // N_KV_HEADS):
        l_ = slice(h * HEAD_DIM, (h + 1) * HEAD_DIM)
        for r0 in range(0, rows, blk):
            qt = q_ref[0, r0:r0 + blk, l_].astype(F32).T.astype(BF16)
            st = _dot(k, qt)
            m = jnp.max(st, axis=0, keepdims=True)
            p = jnp.exp2(st - m)
            l = jnp.sum(p, axis=0, keepdims=True)
            ot = _dot(vt, p.astype(BF16)) / l
            o_ref[0, r0:r0 + blk, l_] = ot.T.astype(BF16)


def _attention(q, k, v, k_row0, sk, rows, blk=512):
    bsz, sq, _ = q.shape
    assert k_row0 % sk == 0
    kb = k_row0 // sk
    gw = (N_Q_HEADS // N_KV_HEADS) * HEAD_DIM
    est = 4 * blk * sk * 12 + 4 * sk * HEAD_DIM * 2 + 4 * rows * gw * 2 + sk * HEAD_DIM * 4
    return pl.pallas_call(
        functools.partial(_attn_kernel, rows=rows, blk=min(blk, rows)),
        out_shape=jax.ShapeDtypeStruct((bsz, sq, ATTN_WIDTH), BF16),
        grid=(bsz, N_KV_HEADS, sq // rows),
        in_specs=[
            pl.BlockSpec((1, rows, gw), lambda b, h, t: (b, t, h)),
            pl.BlockSpec((1, sk, HEAD_DIM), lambda b, h, t: (b, kb, h)),
            pl.BlockSpec((1, sk, HEAD_DIM), lambda b, h, t: (b, kb, h)),
        ],
        out_specs=pl.BlockSpec((1, rows, gw), lambda b, h, t: (b, t, h)),
        scratch_shapes=[pltpu.VMEM((HEAD_DIM, sk), F32)],
        compiler_params=pltpu.CompilerParams(
            dimension_semantics=("arbitrary", "arbitrary", "arbitrary"),
            vmem_limit_bytes=_vmem_limit(est)),
        name="attention",
    )(q, k, v)


def _out_ln_kernel(a1_ref, a2_ref, w_ref, x_ref, g_ref, lng_ref, lnb_ref, o_ref):
    half = a1_ref.shape[2]
    y = _dot(a1_ref[0], w_ref[0:half, :]) + _dot(a2_ref[0], w_ref[half:2 * half, :])
    o_ref[0] = _deepnorm(x_ref[0], g_ref[0], y, lng_ref[...], lnb_ref[...])


def _out_ln(a1, a2, w, x, gate, lng, lnb, rows):
    bsz, seq, d = x.shape
    half = a1.shape[2]
    est = 2 * half * d * 2 + 4 * rows * d * 4 + 4 * rows * half * 2 + 2 * rows * d * 4
    return pl.pallas_call(
        _out_ln_kernel,
        out_shape=jax.ShapeDtypeStruct((bsz, seq, d), F32),
        grid=(bsz, seq // rows),
        in_specs=[
            pl.BlockSpec((1, rows, half), lambda b, t: (b, t, 0)),
            pl.BlockSpec((1, rows, half), lambda b, t: (b, t, 0)),
            _resident((2 * half, d)),
            pl.BlockSpec((1, rows, d), lambda b, t: (b, t, 0)),
            _row_spec(d),
            _resident((1, d)), _resident((1, d)),
        ],
        out_specs=pl.BlockSpec((1, rows, d), lambda b, t: (b, t, 0)),
        compiler_params=pltpu.CompilerParams(
            dimension_semantics=("arbitrary", "arbitrary"),
            vmem_limit_bytes=_vmem_limit(est)),
        name="out_ln",
    )(a1, a2, w, x, gate, lng, lnb)


def _ffn_kernel(x_ref, xp_ref, xn_ref, sh_ref, sc_ref, g_ref, wu1_ref, wu2_ref, cw1_ref, cw2_ref,
                cb1_ref, cb2_ref, wd_ref, lng_ref, lnb_ref, o_ref, hm_ref, u_ref, gt_ref, *, rows):
    _fill_halo_tile(hm_ref, x_ref, xp_ref, xn_ref, sh_ref[0], sc_ref[0], rows)
    n_chunks = wu1_ref.shape[0]
    tiles = rows // V7X_SUBLANES
    sub = lax.broadcasted_iota(jnp.int32, (tiles, V7X_SUBLANES, FF_CHUNK), 1)

    def up_proj(c, slot):
        hm = hm_ref[...]
        u_ref[slot, 0] = _dot(hm, wu1_ref[c]).reshape(tiles + 2, V7X_SUBLANES, FF_CHUNK)
        u_ref[slot, 1] = _dot(hm, wu2_ref[c]).reshape(tiles + 2, V7X_SUBLANES, FF_CHUNK)

    def conv_gate(c, slot):
        def conv(half, cw_ref, cb_ref):
            u = u_ref[slot, half]
            dn = pltpu.roll(u, 1, 1)
            up = pltpu.roll(u, V7X_SUBLANES - 1, 1)
            prev = jnp.where(sub == 0, dn[0:tiles], dn[1:tiles + 1])
            nxt = jnp.where(sub == V7X_SUBLANES - 1, up[2:tiles + 2], up[1:tiles + 1])
            w = cw_ref[c]
            return cb_ref[c] + w[0:1] * prev + w[1:2] * u[1:tiles + 1] + w[2:3] * nxt

        a1 = conv(0, cw1_ref, cb1_ref)
        a2 = conv(1, cw2_ref, cb2_ref)
        gt_ref[c] = (a1 * (a2 + a2 * jnp.tanh(a2))).reshape(rows, FF_CHUNK)

    up_proj(0, 0)

    def pair(k, carry):
        c = 2 * k
        up_proj(c + 1, 1)
        conv_gate(c, 0)
        up_proj(c + 2, 0)
        conv_gate(c + 1, 1)
        return carry

    lax.fori_loop(0, (n_chunks - 1) // 2, pair, 0)
    conv_gate(n_chunks - 1, 0)

    f = _dot(gt_ref[0].astype(BF16), wd_ref[0])
    for c in range(1, n_chunks):
        f = f + _dot(gt_ref[c].astype(BF16), wd_ref[c])
    o_ref[0] = _deepnorm(x_ref[0], g_ref[0], f, lng_ref[...], lnb_ref[...])


def _ffn(x, sh, sc, gate, wu1, wu2, cw1, cw2, cb1, cb2, wd, lng, lnb, rows):
    bsz, seq, d = x.shape
    nc = wu1.shape[0]
    assert nc % 2 == 1, "the pipeline peels one chunk and pairs the rest"
    est = (3 * nc * d * FF_CHUNK * 2 + 4 * rows * d * 4 + (rows + 2 * HALO) * d * 2
           + nc * rows * FF_CHUNK * 4 + 8 * (rows + 2 * HALO) * FF_CHUNK * 4 + 2 * rows * d * 4)
    return pl.pallas_call(
        functools.partial(_ffn_kernel, rows=rows),
        out_shape=jax.ShapeDtypeStruct((bsz, seq, d), F32),
        grid=(bsz, seq // rows),
        in_specs=_halo_specs(rows, seq, d) + [
            _row_spec(d), _row_spec(d), _row_spec(d),
            _resident((nc, d, FF_CHUNK)), _resident((nc, d, FF_CHUNK)),
            _resident((nc, CONV_W, FF_CHUNK)), _resident((nc, CONV_W, FF_CHUNK)),
            _resident((nc, 1, FF_CHUNK)), _resident((nc, 1, FF_CHUNK)),
            _resident((nc, FF_CHUNK, d)),
            _resident((1, d)), _resident((1, d)),
        ],
        out_specs=pl.BlockSpec((1, rows, d), lambda b, t: (b, t, 0)),
        scratch_shapes=[
            pltpu.VMEM((rows + 2 * HALO, d), BF16),
            pltpu.VMEM((2, 2, rows // HALO + 2, HALO, FF_CHUNK), F32),
            pltpu.VMEM((nc, rows, FF_CHUNK), F32),
        ],
        compiler_params=pltpu.CompilerParams(
            dimension_semantics=("arbitrary", "arbitrary"),
            vmem_limit_bytes=_vmem_limit(est)),
        name="conv_ffn",
    )(x, x, x, sh, sc, gate, wu1, wu2, cw1, cw2, cb1, cb2, wd, lng, lnb)


def _cd_in_kernel(*refs, with_fz, rows):
    x_ref, xp_ref, xn_ref, sh_ref, sc_ref = refs[:5]
    if with_fz:
        wfz_ref, wx_ref, wdt_ref, cw_ref, cb_ref, dtb_ref = refs[5:11]
        f_ref, z_ref, xs_ref, b_ref, c_ref, dt_ref, hm_ref = refs[11:]
    else:
        wx_ref, wdt_ref, cw_ref, cb_ref, dtb_ref = refs[5:10]
        xs_ref, b_ref, c_ref, dt_ref, hm_ref = refs[10:]
    _fill_halo_tile(hm_ref, x_ref, xp_ref, xn_ref, sh_ref[0], sc_ref[0], rows)
    hc = hm_ref[HALO:HALO + rows, :]
    if with_fz:
        pfz = _dot(hc, wfz_ref[...])
        f_ref[0] = pfz[:, 0:FN_WIDTH].astype(BF16)
        z_ref[0] = pfz[:, FN_WIDTH:SSM_OFF]
    a = _conv3(_dot(hm_ref[...], wx_ref[...]), cw_ref[...], cb_ref[...], rows)
    s = a + a * jnp.tanh(a)
    xs_ref[0] = s[:, 0:SSD_WIDTH]
    b_ref[0] = s[:, SSD_WIDTH:SSD_WIDTH + SSD_GN].astype(BF16)
    c_ref[0] = s[:, SSD_WIDTH + SSD_GN:SSD_CONV_DIM].astype(BF16)
    dt_ref[0] = _softplus(_dot(hc, wdt_ref[...]) + dtb_ref[...])


def _cd_in(x, sh, sc, wfz, wx, wdt, cw, cb, dtb, rows):
    bsz, seq, d = x.shape
    with_fz = wfz is not None
    in_specs = _halo_specs(rows, seq, d) + [_row_spec(d), _row_spec(d)]
    args = [x, x, x, sh, sc]
    out_shape, out_specs = [], []
    if with_fz:
        in_specs.append(_resident((d, SSM_OFF)))
        args.append(wfz)
        out_shape += [jax.ShapeDtypeStruct((bsz, seq, FN_WIDTH), BF16),
                      jax.ShapeDtypeStruct((bsz, seq, SSD_WIDTH), F32)]
        out_specs += [pl.BlockSpec((1, rows, FN_WIDTH), lambda b, t: (b, t, 0)),
                      pl.BlockSpec((1, rows, SSD_WIDTH), lambda b, t: (b, t, 0))]
    in_specs += [_resident((d, SSD_CONV_DIM)), _resident((d, V7X_LANES)),
                 _resident((CONV_W, SSD_CONV_DIM)), _resident((1, SSD_CONV_DIM)),
                 _resident((1, V7X_LANES))]
    args += [wx, wdt, cw, cb, dtb]
    out_shape += [jax.ShapeDtypeStruct((bsz, seq, SSD_WIDTH), F32),
                  jax.ShapeDtypeStruct((bsz, seq, SSD_GN), BF16),
                  jax.ShapeDtypeStruct((bsz, seq, SSD_GN), BF16),
                  jax.ShapeDtypeStruct((bsz, seq, V7X_LANES), F32)]
    out_specs += [pl.BlockSpec((1, rows, SSD_WIDTH), lambda b, t: (b, t, 0)),
                  pl.BlockSpec((1, rows, SSD_GN), lambda b, t: (b, t, 0)),
                  pl.BlockSpec((1, rows, SSD_GN), lambda b, t: (b, t, 0)),
                  pl.BlockSpec((1, rows, V7X_LANES), lambda b, t: (b, t, 0))]
    est = d * 2200 * 2 + 2 * rows * d * 4 + 8 * (rows + 2 * HALO) * SSD_CONV_DIM * 4
    return pl.pallas_call(
        functools.partial(_cd_in_kernel, with_fz=with_fz, rows=rows),
        out_shape=out_shape,
        grid=(bsz, seq // rows),
        in_specs=in_specs,
        out_specs=out_specs,
        scratch_shapes=[pltpu.VMEM((rows + 2 * HALO, d), BF16)],
        compiler_params=pltpu.CompilerParams(
            dimension_semantics=("arbitrary", "arbitrary"),
            vmem_limit_bytes=_vmem_limit(est)),
        name="cd_in" if with_fz else "cd_in_ctx",
    )(*args)


def _cumsum_rows(z, reverse):
    n = z.shape[0]
    idx = lax.broadcasted_iota(jnp.int32, z.shape, 0)
    k = 1
    while k < n:
        if reverse:
            z = z + jnp.where(idx < n - k, pltpu.roll(z, n - k, 0), 0.0)
        else:
            z = z + jnp.where(idx >= k, pltpu.roll(z, k, 0), 0.0)
        k *= 2
    return z


def _lane_expand(row, lane0, width):
    lane = lax.broadcasted_iota(jnp.int32, (1, SSD_HPG * width), 1)
    out = jnp.zeros((1, SSD_HPG * width), F32)
    for j in range(SSD_HPG):
        val = jnp.broadcast_to(row[:, lane0 + j:lane0 + j + 1], (1, SSD_HPG * width))
        out = jnp.where(lane // width == j, val, out)
    return out


def _ssd_chunk(x, bm, cm, dt, a_row, h_ref, lane0, reverse, need_y):
    q = x.shape[0]
    gw = SSD_HPG * SSD_HEADDIM
    acum = _cumsum_rows(dt * a_row, reverse)
    total = acum[0:1, :] if reverse else acum[q - 1:q, :]
    acum_t = acum.T
    dt_t = dt.T
    total_t = jnp.broadcast_to(acum_t[:, 0:1] if reverse else acum_t[:, q - 1:q], acum_t.shape)
    wgt_t = dt_t * jnp.exp2(total_t - acum_t)
    lane_g = lax.broadcasted_iota(jnp.int32, (q, gw), 1) // SSD_HEADDIM
    lane_gh = lax.broadcasted_iota(jnp.int32, (SSD_STATE, gw), 1) // SSD_HEADDIM
    if need_y:
        srow_t = acum_t - jnp.log2(dt_t)
        ti = lax.broadcasted_iota(jnp.int32, (q, q), 0)
        si = lax.broadcasted_iota(jnp.int32, (q, q), 1)
        mask = (si >= ti) if reverse else (si <= ti)
    ys = []
    for g in range(SSD_GROUPS):
        bg = bm[:, g * SSD_STATE:(g + 1) * SSD_STATE]
        xgb = x[:, g * gw:(g + 1) * gw].astype(BF16)
        x_heads = [jnp.where(lane_g == j, xgb, jnp.zeros_like(xgb)) for j in range(SSD_HPG)]
        hg = h_ref[g]
        if need_y:
            cg = cm[:, g * SSD_STATE:(g + 1) * SSD_STATE]
            cb = _dot_nt(cg, bg)
            cgf = cg.astype(F32)
            hgb = hg.astype(BF16)
            lhs, rhs = [], []
            for j in range(SSD_HPG):
                jl = lane0 + g * SSD_HPG + j
                col = jnp.broadcast_to(acum[:, jl:jl + 1], (q, q))
                row = jnp.broadcast_to(srow_t[jl:jl + 1, :], (q, q))
                lhs.append((cb * jnp.exp2(jnp.where(mask, col - row, -jnp.inf))).astype(BF16))
                lhs.append((cgf * jnp.exp2(col[:, 0:SSD_STATE])).astype(BF16))
                rhs.append(x_heads[j])
                rhs.append(jnp.where(lane_gh == j, hgb, jnp.zeros_like(hgb)))
            ys.append(_dot(jnp.concatenate(lhs, axis=1), jnp.concatenate(rhs, axis=0)))
        bgt = bg.astype(F32).T
        wb = [(bgt * jnp.broadcast_to(wgt_t[lane0 + g * SSD_HPG + j:lane0 + g * SSD_HPG + j + 1, :],
                                      (SSD_STATE, q))).astype(BF16) for j in range(SSD_HPG)]
        st = _dot(jnp.concatenate(wb, axis=1), jnp.concatenate(x_heads, axis=0))
        h_ref[g] = hg * jnp.exp2(_lane_expand(total, lane0 + g * SSD_HPG, SSD_HEADDIM)) + st
    if need_y:
        return jnp.concatenate(ys, axis=1)
    return None


def _ssd_kernel(xc_ref, bc_ref, dtc_ref, xl_ref, bl_ref, cl_ref, dtl_ref, z_ref, alog_ref,
                dskip_ref, ng_ref, o_ref, y_ref, h_ref):
    a_row = -jnp.exp(alog_ref[...]) * math.log2(math.e)
    nc_ctx = xc_ref.shape[1] // SSD_CHUNK
    nc_lat = xl_ref.shape[1] // SSD_CHUNK

    def rows_of(c):
        return pl.ds(pl.multiple_of(c * SSD_CHUNK, SSD_CHUNK), SSD_CHUNK)

    def ctx_step(c, direction):
        r = rows_of(c)
        _ssd_chunk(xc_ref[0, r, :], bc_ref[0, r, :], None, dtc_ref[0, r, :], a_row,
                   h_ref.at[direction], direction * SSD_HEADS, direction == 1, False)

    def lat_step(c, direction):
        r = rows_of(c)
        y_ref[direction, r, :] = _ssd_chunk(
            xl_ref[0, r, :], bl_ref[0, r, :], cl_ref[0, r, :], dtl_ref[0, r, :], a_row,
            h_ref.at[direction], direction * SSD_HEADS, direction == 1, True)

    h_ref[...] = jnp.zeros_like(h_ref)

    def ctx_pair(i, carry):
        ctx_step(i, 0)
        ctx_step(nc_ctx - 1 - i, 1)
        return carry

    lax.fori_loop(0, nc_ctx, ctx_pair, 0)

    def lat_pair(i, carry):
        lat_step(i, 0)
        lat_step(nc_lat - 1 - i, 1)
        return carry

    lax.fori_loop(0, nc_lat, lat_pair, 0)

    def finish(c, carry):
        r = rows_of(c)
        zz = z_ref[0, r, :]
        yt = (y_ref[0, r, :] + y_ref[1, r, :] + xl_ref[0, r, :] * dskip_ref[...]) * _silu(zz)
        o_ref[0, r, :] = _rms_norm(yt, ng_ref[...]).astype(BF16)
        return carry

    lax.fori_loop(0, nc_lat, finish, 0)


def _ssd(xs_c, b_c, dt_c, xs_l, b_l, c_l, dt_l, z, alog_row, dskip_row, ng_row):
    bsz, sl, w = xs_l.shape
    sc = xs_c.shape[1]
    assert sl % SSD_CHUNK == 0 and sc % SSD_CHUNK == 0
    gw = SSD_HPG * SSD_HEADDIM

    def full(s, n):
        return pl.BlockSpec((1, s, n), lambda b: (b, 0, 0))

    est = (2 * (sl + sc) * (w * 4 + 2 * SSD_GN * 2 + V7X_LANES * 4) + 2 * sl * w * 4
           + 2 * sl * w * 2 + 2 * sl * w * 4)
    return pl.pallas_call(
        _ssd_kernel,
        out_shape=jax.ShapeDtypeStruct((bsz, sl, w), BF16),
        grid=(bsz,),
        in_specs=[
            full(sc, w), full(sc, SSD_GN), full(sc, V7X_LANES),
            full(sl, w), full(sl, SSD_GN), full(sl, SSD_GN), full(sl, V7X_LANES),
            full(sl, w),
            _resident((1, V7X_LANES)), _resident((1, w)), _resident((1, w)),
        ],
        out_specs=full(sl, w),
        scratch_shapes=[
            pltpu.VMEM((2, sl, w), F32),
            pltpu.VMEM((2, SSD_GROUPS, SSD_STATE, gw), F32),
        ],
        compiler_params=pltpu.CompilerParams(
            dimension_semantics=("arbitrary",),
            vmem_limit_bytes=_vmem_limit(est)),
        name="ssd_scan",
    )(xs_c, b_c, dt_c, xs_l, b_l, c_l, dt_l, z, alog_row, dskip_row, ng_row)


def _fourier_kernel(f_ref, cc_ref, sc_ref, w2_ref, o_ref, y_ref, *, rows):
    seq = f_ref.shape[1]
    for g in range(FN_GROUPS):
        l = slice(g * FN_GDIM, (g + 1) * FN_GDIM)
        fg = f_ref[0, :, l]
        y_ref[0:seq, l] = _dot(fg, cc_ref[...]).astype(BF16)
        y_ref[seq:2 * seq, l] = _dot(fg, sc_ref[...]).astype(BF16)
    norm = 1.0 / math.sqrt(seq * FN_GDIM)

    def tile(i, carry):
        r = pl.ds(pl.multiple_of(i * rows, rows), rows)
        o_ref[0, r, :] = (_dot(w2_ref[r, :], y_ref[...]) * norm).astype(BF16)
        return carry

    lax.fori_loop(0, seq // rows, tile, 0)


def _fourier(f, cc, sc, w2, rows):
    bsz, seq, w = f.shape
    est = seq * 2 * seq * 2 + 4 * seq * w * 2 + 2 * seq * w * 2 + 4 * rows * 2 * seq * 2
    return pl.pallas_call(
        functools.partial(_fourier_kernel, rows=rows),
        out_shape=jax.ShapeDtypeStruct((bsz, seq, w), BF16),
        grid=(bsz,),
        in_specs=[
            pl.BlockSpec((1, seq, w), lambda b: (b, 0, 0)),
            _resident((FN_GDIM, FN_GDIM)), _resident((FN_GDIM, FN_GDIM)),
            _resident((seq, 2 * seq)),
        ],
        out_specs=pl.BlockSpec((1, seq, w), lambda b: (b, 0, 0)),
        scratch_shapes=[pltpu.VMEM((2 * seq, w), BF16)],
        compiler_params=pltpu.CompilerParams(
            dimension_semantics=("arbitrary",),
            vmem_limit_bytes=_vmem_limit(est)),
        name="fourier_mix",
    )(f, cc, sc, w2)


def _rope_tables(seq):
    rows = seq // GRID_W
    row = jnp.repeat(jnp.arange(rows, dtype=F32), GRID_W)
    col = jnp.broadcast_to(jnp.arange(GRID_W, dtype=F32), (rows, GRID_W)).reshape(-1)
    n_freq = HEAD_DIM // 4
    inv = ROPE_THETA ** (-jnp.arange(n_freq, dtype=F32) / n_freq)
    ang = jnp.stack([row, col], axis=-1)[:, :, None] * inv
    cos, sin = jnp.cos(ang), jnp.sin(ang)
    cos_t = jnp.concatenate([cos[:, 0], cos[:, 1], cos[:, 0], cos[:, 1]], axis=-1)
    sin_t = jnp.concatenate([-sin[:, 0], -sin[:, 1], sin[:, 0], sin[:, 1]], axis=-1)
    return cos_t, sin_t


def _rope_head_order(m):
    lead = m.shape[:-1]
    nh = m.shape[-1] // HEAD_DIM
    m = m.reshape(lead + (nh, 2, 2, HEAD_DIM // 4))
    return jnp.swapaxes(m, -3, -2).reshape(lead + (nh * HEAD_DIM,))


def _dft_tables(seq):
    def tw(n):
        i = np.arange(n, dtype=np.int64)
        ang = ((i[:, None] * i[None, :]) % n).astype(np.float64) * (2.0 * math.pi / n)
        return np.cos(ang), np.sin(ang)

    cc, sc = tw(FN_GDIM)
    cl, sl = tw(seq)
    w2 = np.concatenate([cl, -sl], axis=1).astype(np.float32)
    return (jnp.asarray(cc.astype(np.float32), BF16), jnp.asarray(sc.astype(np.float32), BF16),
            jnp.asarray(w2, BF16))


def _ffn_params(w_up, conv_w, conv_b, w_down):
    d = w_up.shape[0]
    nc = D_FF // FF_CHUNK

    def cols(m, lo):
        r = m.shape[0]
        return m[:, lo:lo + D_FF].reshape(r, nc, FF_CHUNK).transpose(1, 0, 2)

    wu = w_up.astype(BF16)
    cb = conv_b.reshape(1, 2 * D_FF)
    return (cols(wu, 0), cols(wu, D_FF), cols(conv_w, 0), 0.5 * cols(conv_w, D_FF),
            cols(cb, 0), 0.5 * cols(cb, D_FF), w_down.astype(BF16).reshape(nc, FF_CHUNK, d))


def kernel(x, c, ctx, c_ctx, mod_w, mod_b, ln1_g, ln1_b, ln2_g, ln2_b, ffn_up, ffn_conv_w, ffn_conv_b,
           ffn_down, ab_w_in, ab_w_out, gm_ln_g, gm_ln_b, gm_ws, gm_bs, q_norm_g, k_norm_g, cd_w_in,
           cd_w_out, ssd_conv_w, ssd_conv_b, ssd_dt_bias, ssd_a_log, ssd_d, ssd_norm_g):
    bsz, seq, d = x.shape
    ctx_len = ctx.shape[1]
    lat_rows = min(1024, seq)
    out_rows = min(2048, seq)
    dft_rows = min(512, seq)
    ctx_rows = min(256, ctx_len)

    pad = (-(bsz + 1)) % V7X_SUBLANES
    cc = jnp.concatenate([c, c_ctx[None], jnp.zeros((pad, d), F32)], axis=0)
    mod = _modulation(cc, mod_w, mod_b)
    rope_tabs = _rope_tables(seq)

    def row(v):
        return v.reshape(1, -1)

    xl, xc = x, ctx
    for i in range(DEPTH):
        last = i == DEPTH - 1
        j = i // 2
        ml = mod[i, :bsz].reshape(bsz, 6, 1, d)
        mc = jnp.broadcast_to(mod[i, bsz].reshape(1, 6, 1, d), (bsz, 6, 1, d))
        sh1, sc1, g1, sh2, sc2, g2 = (ml[:, k] for k in range(6))
        csh1, csc1, cg1, csh2, csc2, cg2 = (mc[:, k] for k in range(6))
        ln1 = (row(ln1_g[i]), row(ln1_b[i]))
        ln2 = (row(ln2_g[i]), row(ln2_b[i]))
        ffn_p = _ffn_params(ffn_up[i], ffn_conv_w[i], ffn_conv_b[i], ffn_down[i])

        if i % 2 == 0:
            w_in = ab_w_in[j].astype(BF16)
            qk_lo, qk_hi = 2 * GM_WIDTH, KV_OFF + KV_W
            q_scale = HEAD_DIM ** -0.5 * math.log2(math.e)
            w_in = jnp.concatenate([0.5 * w_in[:, :qk_lo], _rope_head_order(w_in[:, qk_lo:qk_hi]),
                                    w_in[:, qk_hi:]], axis=1)
            w_out = ab_w_out[j].astype(BF16)
            gm = (row(gm_ln_g[j]), row(gm_ln_b[j]), gm_ws[j].astype(BF16),
                  jnp.repeat(gm_bs[j].T, GM_WIDTH // GM_GROUPS, axis=1),
                  row(_rope_head_order(q_norm_g[j]) * q_scale), row(_rope_head_order(k_norm_g[j])))
            kv_len = seq + ctx_len
            a_c, q_c, k_all, v_all = _ab_in(xc, csh1, csc1, w_in, *gm, None, ctx_rows, kv_len, seq)
            a_l, q_l, k_all, v_all = _ab_in(xl, sh1, sc1, w_in, *gm, rope_tabs, lat_rows, kv_len, 0,
                                            kv_into=(k_all, v_all))
            mix_l = (a_l, _attention(q_l, k_all, v_all, 0, kv_len, lat_rows))
            if not last:
                mix_c = (a_c, _attention(q_c, k_all, v_all, seq, ctx_len, ctx_rows))
        else:
            w_in = cd_w_in[j].astype(BF16)
            w_out = cd_w_out[j].astype(BF16)
            wfz = w_in[:, :SSM_OFF]
            wx = w_in[:, SSM_OFF:SSM_OFF + SSD_CONV_DIM]
            ndt = 2 * SSD_HEADS
            wdt = jnp.pad(w_in[:, SSM_OFF + SSD_CONV_DIM:], ((0, 0), (0, V7X_LANES - ndt)))
            dtb = jnp.pad(ssd_dt_bias[j].reshape(1, ndt), ((0, 0), (0, V7X_LANES - ndt)))
            alog = jnp.pad(ssd_a_log[j].reshape(1, ndt), ((0, 0), (0, V7X_LANES - ndt)))
            cw, cb = 0.5 * ssd_conv_w[j], 0.5 * row(ssd_conv_b[j])
            dskip = row(jnp.repeat(ssd_d[j], SSD_HEADDIM))
            f_l, z_l, xs_l, b_l, c_l, dt_l = _cd_in(xl, sh1, sc1, wfz, wx, wdt, cw, cb, dtb, lat_rows)
            if last:
                xs_c, b_c, _, dt_c = _cd_in(xc, csh1, csc1, None, wx, wdt, cw, cb, dtb, ctx_rows)
            else:
                raise NotImplementedError("odd layers with a context output are not part of this block")
            s_l = _ssd(xs_c, b_c, dt_c, xs_l, b_l, c_l, dt_l, z_l, alog, dskip, row(ssd_norm_g[j]))
            mix_l = (_fourier(f_l, *_dft_tables(seq), dft_rows), s_l)

        xl = _out_ln(*mix_l, w_out, xl, g1, *ln1, out_rows)
        xl = _ffn(xl, sh2, sc2, g2, *ffn_p, *ln2, lat_rows)
        if not last:
            xc = _out_ln(*mix_c, w_out, xc, cg1, *ln1, ctx_rows)
            xc = _ffn(xc, csh2, csc2, cg2, *ffn_p, *ln2, ctx_rows)
    return xl
```

```python
import functools
import math

import jax
import jax.numpy as jnp
import numpy as np
from jax import lax
from jax.experimental import pallas as pl
from jax.experimental.pallas import tpu as pltpu

D_MODEL = 1024
DEPTH = 2
GRID_W = 64
CHUNK = 128
GM_GROUPS = 4
GM_WIDTH = 512
HEAD_DIM = 128
N_Q_HEADS = 4
N_KV_HEADS = 2
ATTN_WIDTH = 512
KV_W = 256
ROPE_THETA = 10000.0
FN_GROUPS = 4
FN_GDIM = 128
FN_WIDTH = 512
SSD_HEADDIM = 64
SSD_HEADS = 8
SSD_GROUPS = 2
SSD_HPG = 4
SSD_STATE = 128
SSD_WIDTH = 512
SSD_GN = 256
SSD_CONV_DIM = 1024
D_FF = 2816
CONV_W = 3
KV_OFF = 2 * GM_WIDTH + ATTN_WIDTH
SSM_OFF = FN_WIDTH + SSD_WIDTH
DN_ALPHA = (2 * DEPTH) ** 0.25
EPS = 1e-6

V7X_LANES = 128
V7X_SUBLANES = 8
V7X_VMEM_BYTES = 64 * 1024 * 1024
VMEM_CHIP_HEADROOM = 3 << 20
VMEM_TEMP_ALLOWANCE = 8 << 20
MOD_COLS = 2048
AB_SUB = 256
FF_CHUNK = 256
SSD_CHUNK = 256

BF16 = jnp.bfloat16
F32 = jnp.float32
HALO = V7X_SUBLANES


def _vmem_limit(nbytes):
    return int(min(nbytes * 3 // 2 + VMEM_TEMP_ALLOWANCE, V7X_VMEM_BYTES - VMEM_CHIP_HEADROOM))


def _resident(shape):
    nd = len(shape)
    return pl.BlockSpec(shape, lambda *_: (0,) * nd, pipeline_mode=pl.Buffered(1))


def _dot(a, b):
    return jnp.dot(a, b, preferred_element_type=F32)


def _dot_nt(a, b):
    return lax.dot_general(a, b, (((1,), (1,)), ((), ())), preferred_element_type=F32)


def _layer_norm(z, g, b, eps=EPS):
    mu = jnp.mean(z, axis=-1, keepdims=True)
    zc = z - mu
    var = jnp.mean(zc * zc, axis=-1, keepdims=True)
    return zc * lax.rsqrt(var + eps) * g + b


def _deepnorm(x, gate, y, g, b):
    return _layer_norm(x + (gate * (1.0 / DN_ALPHA)) * y, g, b, eps=EPS / DN_ALPHA ** 2)


def _rms_norm(z, g):
    return z * lax.rsqrt(jnp.mean(z * z, axis=-1, keepdims=True) + EPS) * g


def _softplus(z):
    return jnp.maximum(z, 0.0) + jnp.log1p(jnp.exp(-jnp.abs(z)))


def _gelu_tanh_of_half(h):
    c1 = math.sqrt(2.0 / math.pi)
    return h + h * jnp.tanh(h * (2.0 * c1 + (8.0 * c1 * 0.044715) * (h * h)))


def _silu(z):
    h = 0.5 * z
    return h + h * jnp.tanh(h)


def _conv3(u, w, b, rows):
    tiles, c = rows // V7X_SUBLANES, u.shape[1]
    u3 = u.reshape(tiles + 2, V7X_SUBLANES, c)
    sub = lax.broadcasted_iota(jnp.int32, (tiles, V7X_SUBLANES, c), 1)
    dn = pltpu.roll(u3, 1, 1)
    up = pltpu.roll(u3, V7X_SUBLANES - 1, 1)
    prev = jnp.where(sub == 0, dn[0:tiles], dn[1:tiles + 1])
    nxt = jnp.where(sub == V7X_SUBLANES - 1, up[2:tiles + 2], up[1:tiles + 1])
    return (b + w[0:1] * prev + w[1:2] * u3[1:tiles + 1] + w[2:3] * nxt).reshape(rows, c)


def _fill_halo_tile(hm_ref, x_ref, xp_ref, xn_ref, sh, sc, rows):
    t = pl.program_id(1)
    nt = pl.num_programs(1)
    keep_prev = jnp.where(t > 0, 1.0, 0.0)
    keep_next = jnp.where(t < nt - 1, 1.0, 0.0)
    hm_ref[HALO:HALO + rows, :] = (x_ref[0] * (1.0 + sc) + sh).astype(BF16)
    hm_ref[0:HALO, :] = ((xp_ref[0] * (1.0 + sc) + sh) * keep_prev).astype(BF16)
    hm_ref[HALO + rows:2 * HALO + rows, :] = ((xn_ref[0] * (1.0 + sc) + sh) * keep_next).astype(BF16)


def _halo_specs(rows, seq, d):
    nb = rows // HALO
    last = seq // HALO - 1
    return [
        pl.BlockSpec((1, rows, d), lambda b, t: (b, t, 0)),
        pl.BlockSpec((1, HALO, d), lambda b, t: (b, jnp.maximum(t * nb - 1, 0), 0)),
        pl.BlockSpec((1, HALO, d), lambda b, t: (b, jnp.minimum((t + 1) * nb, last), 0)),
    ]


def _row_spec(d):
    return pl.BlockSpec((1, 1, d), lambda b, t: (b, 0, 0))


def _mod_kernel(c_ref, w_ref, b_ref, o_ref):
    c = c_ref[...]
    act = c * jax.nn.sigmoid(c)
    o_ref[0] = jnp.dot(act, w_ref[0], preferred_element_type=F32,
                       precision=lax.Precision.HIGHEST) + b_ref[0]


def _modulation(cc, mod_w, mod_b):
    rows, d = cc.shape
    depth, _, n = mod_w.shape
    tn = MOD_COLS
    return pl.pallas_call(
        _mod_kernel,
        out_shape=jax.ShapeDtypeStruct((depth, rows, n), F32),
        grid=(depth, n // tn),
        in_specs=[
            pl.BlockSpec((rows, d), lambda i, j: (0, 0)),
            pl.BlockSpec((1, d, tn), lambda i, j: (i, 0, j)),
            pl.BlockSpec((1, 1, tn), lambda i, j: (i, 0, j)),
        ],
        out_specs=pl.BlockSpec((1, rows, tn), lambda i, j: (i, 0, j)),
        compiler_params=pltpu.CompilerParams(
            dimension_semantics=("arbitrary", "arbitrary"),
            vmem_limit_bytes=_vmem_limit(2 * d * tn * 4)),
        name="modulation",
    )(cc, mod_w, mod_b.reshape(depth, 1, n))


def _ab_in_kernel(*refs, use_rope, rows):
    (x_ref, sh_ref, sc_ref, w_ref, lng_ref, lnb_ref, ws_ref, bias_ref, qg_ref, kg_ref) = refs[:10]
    if use_rope:
        cos_ref, sin_ref = refs[10:12]
    a_ref, q_ref, k_ref, v_ref = refs[-4:]

    gd = GM_WIDTH // GM_GROUPS
    slab_rows = min(AB_SUB, rows)

    def slab(i, carry):
        r0 = pl.multiple_of(i * slab_rows, slab_rows)
        rs = pl.ds(r0, slab_rows)
        h = (x_ref[0, rs, :] * (1.0 + sc_ref[0]) + sh_ref[0]).astype(BF16)

        u = _gelu_tanh_of_half(_dot(h, w_ref[:, 0:GM_WIDTH]))
        vn = _layer_norm(_gelu_tanh_of_half(_dot(h, w_ref[:, GM_WIDTH:2 * GM_WIDTH])),
                         lng_ref[...], lnb_ref[...])
        vb = vn.astype(BF16)
        for c in range(slab_rows // CHUNK):
            r = slice(c * CHUNK, (c + 1) * CHUNK)
            for g in range(GM_GROUPS):
                l = slice(g * gd, (g + 1) * gd)
                mixed = _dot(ws_ref[g], vb[r, l]) + bias_ref[:, l]
                a_ref[0, pl.ds(r0 + c * CHUNK, CHUNK), l] = (u[r, l] * mixed).astype(BF16)

        def rope(z):
            if not use_rope:
                return z
            return z * cos_ref[rs, :] + pltpu.roll(z, HEAD_DIM // 2, 1) * sin_ref[rs, :]

        pq = _dot(h, w_ref[:, 2 * GM_WIDTH:KV_OFF])
        for hh in range(N_Q_HEADS):
            l = slice(hh * HEAD_DIM, (hh + 1) * HEAD_DIM)
            q_ref[0, rs, l] = rope(_rms_norm(pq[:, l], qg_ref[...])).astype(BF16)
        pk = _dot(h, w_ref[:, KV_OFF:KV_OFF + KV_W])
        for hh in range(N_KV_HEADS):
            l = slice(hh * HEAD_DIM, (hh + 1) * HEAD_DIM)
            k_ref[0, rs, l] = rope(_rms_norm(pk[:, l], kg_ref[...])).astype(BF16)
        v_ref[0, rs, :] = _dot(h, w_ref[:, KV_OFF + KV_W:KV_OFF + 2 * KV_W]).astype(BF16)
        return carry

    lax.fori_loop(0, rows // slab_rows, slab, 0)


def _ab_in(x, sh, sc, w, lng, lnb, ws, bias2d, qg, kg, rope_tabs, rows, kv_len, kv_row0, kv_into=None):
    bsz, seq, d = x.shape
    assert kv_row0 % rows == 0
    kv_blk0 = kv_row0 // rows
    n = w.shape[1]
    use_rope = rope_tabs is not None
    in_specs = [
        pl.BlockSpec((1, rows, d), lambda b, t: (b, t, 0)),
        _row_spec(d), _row_spec(d),
        _resident((d, n)),
        _resident((1, GM_WIDTH)), _resident((1, GM_WIDTH)),
        _resident((GM_GROUPS, CHUNK, CHUNK)),
        _resident((CHUNK, GM_WIDTH)),
        _resident((1, HEAD_DIM)), _resident((1, HEAD_DIM)),
    ]
    args = [x, sh, sc, w, lng, lnb, ws, bias2d, qg, kg]
    if use_rope:
        in_specs += [pl.BlockSpec((rows, HEAD_DIM), lambda b, t: (t, 0))] * 2
        args += list(rope_tabs)
    aliases = {}
    if kv_into is not None:
        aliases = {len(args): 2, len(args) + 1: 3}
        in_specs += [pl.BlockSpec(memory_space=pl.ANY)] * 2
        args += list(kv_into)
    out_shape = [
        jax.ShapeDtypeStruct((bsz, seq, GM_WIDTH), BF16),
        jax.ShapeDtypeStruct((bsz, seq, ATTN_WIDTH), BF16),
        jax.ShapeDtypeStruct((bsz, kv_len, KV_W), BF16),
        jax.ShapeDtypeStruct((bsz, kv_len, KV_W), BF16),
    ]
    out_specs = [
        pl.BlockSpec((1, rows, GM_WIDTH), lambda b, t: (b, t, 0)),
        pl.BlockSpec((1, rows, ATTN_WIDTH), lambda b, t: (b, t, 0)),
        pl.BlockSpec((1, rows, KV_W), lambda b, t: (b, t + kv_blk0, 0)),
        pl.BlockSpec((1, rows, KV_W), lambda b, t: (b, t + kv_blk0, 0)),
    ]
    est = d * n * 2 + 2 * rows * d * 4 + 6 * rows * n * 4
    return pl.pallas_call(
        functools.partial(_ab_in_kernel, use_rope=use_rope, rows=rows),
        out_shape=out_shape,
        grid=(bsz, seq // rows),
        in_specs=in_specs,
        out_specs=out_specs,
        input_output_aliases=aliases,
        compiler_params=pltpu.CompilerParams(
            dimension_semantics=("arbitrary", "arbitrary"),
            vmem_limit_bytes=_vmem_limit(est)),
        name="ab_in_rope" if use_rope else "ab_in_ctx",
    )(*args)


def _attn_kernel(q_ref, k_ref, v_ref, o_ref, vt_ref, *, rows, blk):
    @pl.when(pl.program_id(2) == 0)
    def _():
        vt_ref[...] = v_ref[0].astype(F32).T

    k = k_ref[0]
    vt = vt_ref[...].astype(BF16)
    for h in range(N_Q_HEADS // N_KV_HEADS):
        l_ = slice(h * HEAD_DIM, (h + 1) * HEAD_DIM)
        for r0 in range(0, rows, blk):
            qt = q_ref[0, r0:r0 + blk, l_].astype(F32).T.astype(BF16)
            st = _dot(k, qt)
            m = jnp.max(st, axis=0, keepdims=True)
            p = jnp.exp2(st - m)
            l = jnp.sum(p, axis=0, keepdims=True)
            ot = _dot(vt, p.astype(BF16)) / l
            o_ref[0, r0:r0 + blk, l_] = ot.T.astype(BF16)


def _attention(q, k, v, k_row0, sk, rows, blk=512):
    bsz, sq, _ = q.shape
    assert k_row0 % sk == 0
    kb = k_row0 // sk
    gw = (N_Q_HEADS // N_KV_HEADS) * HEAD_DIM
    est = 4 * blk * sk * 12 + 4 * sk * HEAD_DIM * 2 + 4 * rows * gw * 2 + sk * HEAD_DIM * 4
    return pl.pallas_call(
        functools.partial(_attn_kernel, rows=rows, blk=min(blk, rows)),
        out_shape=jax.ShapeDtypeStruct((bsz, sq, ATTN_WIDTH), BF16),
        grid=(bsz, N_KV_HEADS, sq // rows),
        in_specs=[
            pl.BlockSpec((1, rows, gw), lambda b, h, t: (b, t, h)),
            pl.BlockSpec((1, sk, HEAD_DIM), lambda b, h, t: (b, kb, h)),
            pl.BlockSpec((1, sk, HEAD_DIM), lambda b, h, t: (b, kb, h)),
        ],
        out_specs=pl.BlockSpec((1, rows, gw), lambda b, h, t: (b, t, h)),
        scratch_shapes=[pltpu.VMEM((HEAD_DIM, sk), F32)],
        compiler_params=pltpu.CompilerParams(
            dimension_semantics=("arbitrary", "arbitrary", "arbitrary"),
            vmem_limit_bytes=_vmem_limit(est)),
        name="attention",
    )(q, k, v)


def _out_ln_kernel(a1_ref, a2_ref, w_ref, x_ref, g_ref, lng_ref, lnb_ref, o_ref):
    half = a1_ref.shape[2]
    y = _dot(a1_ref[0], w_ref[0:half, :]) + _dot(a2_ref[0], w_ref[half:2 * half, :])
    o_ref[0] = _deepnorm(x_ref[0], g_ref[0], y, lng_ref[...], lnb_ref[...])


def _out_ln(a1, a2, w, x, gate, lng, lnb, rows):
    bsz, seq, d = x.shape
    half = a1.shape[2]
    est = 2 * half * d * 2 + 4 * rows * d * 4 + 4 * rows * half * 2 + 2 * rows * d * 4
    return pl.pallas_call(
        _out_ln_kernel,
        out_shape=jax.ShapeDtypeStruct((bsz, seq, d), F32),
        grid=(bsz, seq // rows),
        in_specs=[
            pl.BlockSpec((1, rows, half), lambda b, t: (b, t, 0)),
            pl.BlockSpec((1, rows, half), lambda b, t: (b, t, 0)),
            _resident((2 * half, d)),
            pl.BlockSpec((1, rows, d), lambda b, t: (b, t, 0)),
            _row_spec(d),
            _resident((1, d)), _resident((1, d)),
        ],
        out_specs=pl.BlockSpec((1, rows, d), lambda b, t: (b, t, 0)),
        compiler_params=pltpu.CompilerParams(
            dimension_semantics=("arbitrary", "arbitrary"),
            vmem_limit_bytes=_vmem_limit(est)),
        name="out_ln",
    )(a1, a2, w, x, gate, lng, lnb)


def _ffn_kernel(x_ref, xp_ref, xn_ref, sh_ref, sc_ref, g_ref, wu1_ref, wu2_ref, cw1_ref, cw2_ref,
                cb1_ref, cb2_ref, wd_ref, lng_ref, lnb_ref, o_ref, hm_ref, u_ref, gt_ref, *, rows):
    _fill_halo_tile(hm_ref, x_ref, xp_ref, xn_ref, sh_ref[0], sc_ref[0], rows)
    n_chunks = wu1_ref.shape[1] // FF_CHUNK

    def cols(ref, c):
        off = c * FF_CHUNK
        return ref[:, pl.ds(off if isinstance(off, int) else pl.multiple_of(off, FF_CHUNK), FF_CHUNK)]
    tiles = rows // V7X_SUBLANES
    sub = lax.broadcasted_iota(jnp.int32, (tiles, V7X_SUBLANES, FF_CHUNK), 1)

    def up_proj(c, slot):
        hm = hm_ref[...]
        u_ref[slot, 0] = _dot(hm, cols(wu1_ref, c)).reshape(tiles + 2, V7X_SUBLANES, FF_CHUNK)
        u_ref[slot, 1] = _dot(hm, cols(wu2_ref, c)).reshape(tiles + 2, V7X_SUBLANES, FF_CHUNK)

    def conv_gate(c, slot):
        def conv(half, cw_ref, cb_ref):
            u = u_ref[slot, half]
            dn = pltpu.roll(u, 1, 1)
            up = pltpu.roll(u, V7X_SUBLANES - 1, 1)
            prev = jnp.where(sub == 0, dn[0:tiles], dn[1:tiles + 1])
            nxt = jnp.where(sub == V7X_SUBLANES - 1, up[2:tiles + 2], up[1:tiles + 1])
            w = cols(cw_ref, c)
            return cols(cb_ref, c) + w[0:1] * prev + w[1:2] * u[1:tiles + 1] + w[2:3] * nxt

        a1 = conv(0, cw1_ref, cb1_ref)
        a2 = conv(1, cw2_ref, cb2_ref)
        gt_ref[c] = (a1 * (a2 + a2 * jnp.tanh(a2))).reshape(rows, FF_CHUNK)

    up_proj(0, 0)

    def pair(k, carry):
        c = 2 * k
        up_proj(c + 1, 1)
        conv_gate(c, 0)
        up_proj(c + 2, 0)
        conv_gate(c + 1, 1)
        return carry

    lax.fori_loop(0, (n_chunks - 1) // 2, pair, 0)
    conv_gate(n_chunks - 1, 0)

    f = _dot(gt_ref[0].astype(BF16), wd_ref[0])
    for c in range(1, n_chunks):
        f = f + _dot(gt_ref[c].astype(BF16), wd_ref[c])
    o_ref[0] = _deepnorm(x_ref[0], g_ref[0], f, lng_ref[...], lnb_ref[...])


def _ffn(x, sh, sc, gate, wu1, wu2, cw1, cw2, cb1, cb2, wd, lng, lnb, rows):
    bsz, seq, d = x.shape
    dff = wu1.shape[1]
    nc = dff // FF_CHUNK
    assert dff % FF_CHUNK == 0 and nc % 2 == 1, "the pipeline peels one chunk and pairs the rest"
    est = (3 * nc * d * FF_CHUNK * 2 + 4 * rows * d * 4 + (rows + 2 * HALO) * d * 2
           + nc * rows * FF_CHUNK * 4 + 8 * (rows + 2 * HALO) * FF_CHUNK * 4 + 2 * rows * d * 4)
    return pl.pallas_call(
        functools.partial(_ffn_kernel, rows=rows),
        out_shape=jax.ShapeDtypeStruct((bsz, seq, d), F32),
        grid=(bsz, seq // rows),
        in_specs=_halo_specs(rows, seq, d) + [
            _row_spec(d), _row_spec(d), _row_spec(d),
            _resident((d, dff)), _resident((d, dff)),
            _resident((CONV_W, dff)), _resident((CONV_W, dff)),
            _resident((1, dff)), _resident((1, dff)),
            _resident((nc, FF_CHUNK, d)),
            _resident((1, d)), _resident((1, d)),
        ],
        out_specs=pl.BlockSpec((1, rows, d), lambda b, t: (b, t, 0)),
        scratch_shapes=[
            pltpu.VMEM((rows + 2 * HALO, d), BF16),
            pltpu.VMEM((2, 2, rows // HALO + 2, HALO, FF_CHUNK), F32),
            pltpu.VMEM((nc, rows, FF_CHUNK), F32),
        ],
        compiler_params=pltpu.CompilerParams(
            dimension_semantics=("arbitrary", "arbitrary"),
            vmem_limit_bytes=_vmem_limit(est)),
        name="conv_ffn",
    )(x, x, x, sh, sc, gate, wu1, wu2, cw1, cw2, cb1, cb2, wd, lng, lnb)


def _cd_in_kernel(*refs, with_fz, rows):
    x_ref, xp_ref, xn_ref, sh_ref, sc_ref = refs[:5]
    if with_fz:
        wfz_ref, wx_ref, wdt_ref, cw_ref, cb_ref, dtb_ref = refs[5:11]
        f_ref, z_ref, xs_ref, b_ref, c_ref, dt_ref, hm_ref = refs[11:]
    else:
        wx_ref, wdt_ref, cw_ref, cb_ref, dtb_ref = refs[5:10]
        xs_ref, b_ref, c_ref, dt_ref, hm_ref = refs[10:]
    _fill_halo_tile(hm_ref, x_ref, xp_ref, xn_ref, sh_ref[0], sc_ref[0], rows)
    hc = hm_ref[HALO:HALO + rows, :]
    if with_fz:
        pfz = _dot(hc, wfz_ref[...])
        f_ref[0] = pfz[:, 0:FN_WIDTH].astype(BF16)
        z_ref[0] = pfz[:, FN_WIDTH:SSM_OFF]
    a = _conv3(_dot(hm_ref[...], wx_ref[...]), cw_ref[...], cb_ref[...], rows)
    s = a + a * jnp.tanh(a)
    xs_ref[0] = s[:, 0:SSD_WIDTH]
    b_ref[0] = s[:, SSD_WIDTH:SSD_WIDTH + SSD_GN].astype(BF16)
    c_ref[0] = s[:, SSD_WIDTH + SSD_GN:SSD_CONV_DIM].astype(BF16)
    dt_ref[0] = _softplus(_dot(hc, wdt_ref[...]) + dtb_ref[...])


def _cd_in(x, sh, sc, wfz, wx, wdt, cw, cb, dtb, rows):
    bsz, seq, d = x.shape
    with_fz = wfz is not None
    in_specs = _halo_specs(rows, seq, d) + [_row_spec(d), _row_spec(d)]
    args = [x, x, x, sh, sc]
    out_shape, out_specs = [], []
    if with_fz:
        in_specs.append(_resident((d, SSM_OFF)))
        args.append(wfz)
        out_shape += [jax.ShapeDtypeStruct((bsz, seq, FN_WIDTH), BF16),
                      jax.ShapeDtypeStruct((bsz, seq, SSD_WIDTH), F32)]
        out_specs += [pl.BlockSpec((1, rows, FN_WIDTH), lambda b, t: (b, t, 0)),
                      pl.BlockSpec((1, rows, SSD_WIDTH), lambda b, t: (b, t, 0))]
    in_specs += [_resident((d, SSD_CONV_DIM)), _resident((d, V7X_LANES)),
                 _resident((CONV_W, SSD_CONV_DIM)), _resident((1, SSD_CONV_DIM)),
                 _resident((1, V7X_LANES))]
    args += [wx, wdt, cw, cb, dtb]
    out_shape += [jax.ShapeDtypeStruct((bsz, seq, SSD_WIDTH), F32),
                  jax.ShapeDtypeStruct((bsz, seq, SSD_GN), BF16),
                  jax.ShapeDtypeStruct((bsz, seq, SSD_GN), BF16),
                  jax.ShapeDtypeStruct((bsz, seq, V7X_LANES), F32)]
    out_specs += [pl.BlockSpec((1, rows, SSD_WIDTH), lambda b, t: (b, t, 0)),
                  pl.BlockSpec((1, rows, SSD_GN), lambda b, t: (b, t, 0)),
                  pl.BlockSpec((1, rows, SSD_GN), lambda b, t: (b, t, 0)),
                  pl.BlockSpec((1, rows, V7X_LANES), lambda b, t: (b, t, 0))]
    est = d * 2200 * 2 + 2 * rows * d * 4 + 8 * (rows + 2 * HALO) * SSD_CONV_DIM * 4
    return pl.pallas_call(
        functools.partial(_cd_in_kernel, with_fz=with_fz, rows=rows),
        out_shape=out_shape,
        grid=(bsz, seq // rows),
        in_specs=in_specs,
        out_specs=out_specs,
        scratch_shapes=[pltpu.VMEM((rows + 2 * HALO, d), BF16)],
        compiler_params=pltpu.CompilerParams(
            dimension_semantics=("arbitrary", "arbitrary"),
            vmem_limit_bytes=_vmem_limit(est)),
        name="cd_in" if with_fz else "cd_in_ctx",
    )(*args)


def _cumsum_rows(z, reverse):
    n = z.shape[0]
    idx = lax.broadcasted_iota(jnp.int32, z.shape, 0)
    k = 1
    while k < n:
        if reverse:
            z = z + jnp.where(idx < n - k, pltpu.roll(z, n - k, 0), 0.0)
        else:
            z = z + jnp.where(idx >= k, pltpu.roll(z, k, 0), 0.0)
        k *= 2
    return z


def _lane_expand(row, lane0, width):
    lane = lax.broadcasted_iota(jnp.int32, (1, SSD_HPG * width), 1)
    out = jnp.zeros((1, SSD_HPG * width), F32)
    for j in range(SSD_HPG):
        val = jnp.broadcast_to(row[:, lane0 + j:lane0 + j + 1], (1, SSD_HPG * width))
        out = jnp.where(lane // width == j, val, out)
    return out


def _ssd_chunk(x, bm, cm, dt, a_row, h_ref, lane0, reverse, need_y):
    q = x.shape[0]
    gw = SSD_HPG * SSD_HEADDIM
    acum = _cumsum_rows(dt * a_row, reverse)
    total = acum[0:1, :] if reverse else acum[q - 1:q, :]
    acum_t = acum.T
    dt_t = dt.T
    total_t = jnp.broadcast_to(acum_t[:, 0:1] if reverse else acum_t[:, q - 1:q], acum_t.shape)
    wgt_t = dt_t * jnp.exp2(total_t - acum_t)
    lane_g = lax.broadcasted_iota(jnp.int32, (q, gw), 1) // SSD_HEADDIM
    lane_gh = lax.broadcasted_iota(jnp.int32, (SSD_STATE, gw), 1) // SSD_HEADDIM
    if need_y:
        srow_t = acum_t - jnp.log2(dt_t)
        ti = lax.broadcasted_iota(jnp.int32, (q, q), 0)
        si = lax.broadcasted_iota(jnp.int32, (q, q), 1)
        mask = (si >= ti) if reverse else (si <= ti)
    ys = []
    for g in range(SSD_GROUPS):
        bg = bm[:, g * SSD_STATE:(g + 1) * SSD_STATE]
        xgb = x[:, g * gw:(g + 1) * gw].astype(BF16)
        x_heads = [jnp.where(lane_g == j, xgb, jnp.zeros_like(xgb)) for j in range(SSD_HPG)]
        hg = h_ref[g]
        if need_y:
            cg = cm[:, g * SSD_STATE:(g + 1) * SSD_STATE]
            cb = _dot_nt(cg, bg)
            cgf = cg.astype(F32)
            hgb = hg.astype(BF16)
            lhs, rhs = [], []
            for j in range(SSD_HPG):
                jl = lane0 + g * SSD_HPG + j
                col = jnp.broadcast_to(acum[:, jl:jl + 1], (q, q))
                row = jnp.broadcast_to(srow_t[jl:jl + 1, :], (q, q))
                lhs.append((cb * jnp.exp2(jnp.where(mask, col - row, -jnp.inf))).astype(BF16))
                lhs.append((cgf * jnp.exp2(col[:, 0:SSD_STATE])).astype(BF16))
                rhs.append(x_heads[j])
                rhs.append(jnp.where(lane_gh == j, hgb, jnp.zeros_like(hgb)))
            ys.append(_dot(jnp.concatenate(lhs, axis=1), jnp.concatenate(rhs, axis=0)))
        bgt = bg.astype(F32).T
        wb = [(bgt * jnp.broadcast_to(wgt_t[lane0 + g * SSD_HPG + j:lane0 + g * SSD_HPG + j + 1, :],
                                      (SSD_STATE, q))).astype(BF16) for j in range(SSD_HPG)]
        st = _dot(jnp.concatenate(wb, axis=1), jnp.concatenate(x_heads, axis=0))
        h_ref[g] = hg * jnp.exp2(_lane_expand(total, lane0 + g * SSD_HPG, SSD_HEADDIM)) + st
    if need_y:
        return jnp.concatenate(ys, axis=1)
    return None


def _ssd_kernel(xc_ref, bc_ref, dtc_ref, xl_ref, bl_ref, cl_ref, dtl_ref, z_ref, alog_ref,
                dskip_ref, ng_ref, o_ref, y_ref, h_ref):
    a_row = -jnp.exp(alog_ref[...]) * math.log2(math.e)
    nc_ctx = xc_ref.shape[1] // SSD_CHUNK
    nc_lat = xl_ref.shape[1] // SSD_CHUNK

    def rows_of(c):
        return pl.ds(pl.multiple_of(c * SSD_CHUNK, SSD_CHUNK), SSD_CHUNK)

    def ctx_step(c, direction):
        r = rows_of(c)
        _ssd_chunk(xc_ref[0, r, :], bc_ref[0, r, :], None, dtc_ref[0, r, :], a_row,
                   h_ref.at[direction], direction * SSD_HEADS, direction == 1, False)

    def lat_step(c, direction):
        r = rows_of(c)
        y_ref[direction, r, :] = _ssd_chunk(
            xl_ref[0, r, :], bl_ref[0, r, :], cl_ref[0, r, :], dtl_ref[0, r, :], a_row,
            h_ref.at[direction], direction * SSD_HEADS, direction == 1, True)

    h_ref[...] = jnp.zeros_like(h_ref)

    def ctx_pair(i, carry):
        ctx_step(i, 0)
        ctx_step(nc_ctx - 1 - i, 1)
        return carry

    lax.fori_loop(0, nc_ctx, ctx_pair, 0)

    def lat_pair(i, carry):
        lat_step(i, 0)
        lat_step(nc_lat - 1 - i, 1)
        return carry

    lax.fori_loop(0, nc_lat, lat_pair, 0)

    def finish(c, carry):
        r = rows_of(c)
        zz = z_ref[0, r, :]
        yt = (y_ref[0, r, :] + y_ref[1, r, :] + xl_ref[0, r, :] * dskip_ref[...]) * _silu(zz)
        o_ref[0, r, :] = _rms_norm(yt, ng_ref[...]).astype(BF16)
        return carry

    lax.fori_loop(0, nc_lat, finish, 0)


def _ssd(xs_c, b_c, dt_c, xs_l, b_l, c_l, dt_l, z, alog_row, dskip_row, ng_row):
    bsz, sl, w = xs_l.shape
    sc = xs_c.shape[1]
    assert sl % SSD_CHUNK == 0 and sc % SSD_CHUNK == 0
    gw = SSD_HPG * SSD_HEADDIM

    def full(s, n):
        return pl.BlockSpec((1, s, n), lambda b: (b, 0, 0))

    est = (2 * (sl + sc) * (w * 4 + 2 * SSD_GN * 2 + V7X_LANES * 4) + 2 * sl * w * 4
           + 2 * sl * w * 2 + 2 * sl * w * 4)
    return pl.pallas_call(
        _ssd_kernel,
        out_shape=jax.ShapeDtypeStruct((bsz, sl, w), BF16),
        grid=(bsz,),
        in_specs=[
            full(sc, w), full(sc, SSD_GN), full(sc, V7X_LANES),
            full(sl, w), full(sl, SSD_GN), full(sl, SSD_GN), full(sl, V7X_LANES),
            full(sl, w),
            _resident((1, V7X_LANES)), _resident((1, w)), _resident((1, w)),
        ],
        out_specs=full(sl, w),
        scratch_shapes=[
            pltpu.VMEM((2, sl, w), F32),
            pltpu.VMEM((2, SSD_GROUPS, SSD_STATE, gw), F32),
        ],
        compiler_params=pltpu.CompilerParams(
            dimension_semantics=("arbitrary",),
            vmem_limit_bytes=_vmem_limit(est)),
        name="ssd_scan",
    )(xs_c, b_c, dt_c, xs_l, b_l, c_l, dt_l, z, alog_row, dskip_row, ng_row)


def _fourier_kernel(f_ref, cc_ref, sc_ref, w2_ref, o_ref, y_ref, *, rows):
    seq = f_ref.shape[1]
    for g in range(FN_GROUPS):
        l = slice(g * FN_GDIM, (g + 1) * FN_GDIM)
        fg = f_ref[0, :, l]
        y_ref[0:seq, l] = _dot(fg, cc_ref[...]).astype(BF16)
        y_ref[seq:2 * seq, l] = _dot(fg, sc_ref[...]).astype(BF16)
    norm = 1.0 / math.sqrt(seq * FN_GDIM)

    def tile(i, carry):
        r = pl.ds(pl.multiple_of(i * rows, rows), rows)
        o_ref[0, r, :] = (_dot(w2_ref[r, :], y_ref[...]) * norm).astype(BF16)
        return carry

    lax.fori_loop(0, seq // rows, tile, 0)


def _fourier(f, cc, sc, w2, rows):
    bsz, seq, w = f.shape
    est = seq * 2 * seq * 2 + 4 * seq * w * 2 + 2 * seq * w * 2 + 4 * rows * 2 * seq * 2
    return pl.pallas_call(
        functools.partial(_fourier_kernel, rows=rows),
        out_shape=jax.ShapeDtypeStruct((bsz, seq, w), BF16),
        grid=(bsz,),
        in_specs=[
            pl.BlockSpec((1, seq, w), lambda b: (b, 0, 0)),
            _resident((FN_GDIM, FN_GDIM)), _resident((FN_GDIM, FN_GDIM)),
            _resident((seq, 2 * seq)),
        ],
        out_specs=pl.BlockSpec((1, seq, w), lambda b: (b, 0, 0)),
        scratch_shapes=[pltpu.VMEM((2 * seq, w), BF16)],
        compiler_params=pltpu.CompilerParams(
            dimension_semantics=("arbitrary",),
            vmem_limit_bytes=_vmem_limit(est)),
        name="fourier_mix",
    )(f, cc, sc, w2)


def _rope_tables(seq):
    rows = seq // GRID_W
    row = jnp.repeat(jnp.arange(rows, dtype=F32), GRID_W)
    col = jnp.broadcast_to(jnp.arange(GRID_W, dtype=F32), (rows, GRID_W)).reshape(-1)
    n_freq = HEAD_DIM // 4
    inv = ROPE_THETA ** (-jnp.arange(n_freq, dtype=F32) / n_freq)
    ang = jnp.stack([row, col], axis=-1)[:, :, None] * inv
    cos, sin = jnp.cos(ang), jnp.sin(ang)
    cos_t = jnp.concatenate([cos[:, 0], cos[:, 1], cos[:, 0], cos[:, 1]], axis=-1)
    sin_t = jnp.concatenate([-sin[:, 0], -sin[:, 1], sin[:, 0], sin[:, 1]], axis=-1)
    return cos_t, sin_t


def _rope_head_order(m):
    lead = m.shape[:-1]
    nh = m.shape[-1] // HEAD_DIM
    m = m.reshape(lead + (nh, 2, 2, HEAD_DIM // 4))
    return jnp.swapaxes(m, -3, -2).reshape(lead + (nh * HEAD_DIM,))


def _dft_tables(seq):
    def tw(n):
        i = np.arange(n, dtype=np.int64)
        ang = ((i[:, None] * i[None, :]) % n).astype(np.float64) * (2.0 * math.pi / n)
        return np.cos(ang), np.sin(ang)

    cc, sc = tw(FN_GDIM)
    cl, sl = tw(seq)
    w2 = np.concatenate([cl, -sl], axis=1).astype(np.float32)
    return (jnp.asarray(cc.astype(np.float32), BF16), jnp.asarray(sc.astype(np.float32), BF16),
            jnp.asarray(w2, BF16))


def _ffn_params(w_up, conv_w, conv_b, w_down):
    d = w_up.shape[0]
    wu = w_up.astype(BF16)
    cb = conv_b.reshape(1, 2 * D_FF)
    return (wu[:, :D_FF], wu[:, D_FF:], conv_w[:, :D_FF], 0.5 * conv_w[:, D_FF:],
            cb[:, :D_FF], 0.5 * cb[:, D_FF:], w_down.astype(BF16).reshape(D_FF // FF_CHUNK, FF_CHUNK, d))


def kernel(x, c, ctx, c_ctx, mod_w, mod_b, ln1_g, ln1_b, ln2_g, ln2_b, ffn_up, ffn_conv_w, ffn_conv_b,
           ffn_down, ab_w_in, ab_w_out, gm_ln_g, gm_ln_b, gm_ws, gm_bs, q_norm_g, k_norm_g, cd_w_in,
           cd_w_out, ssd_conv_w, ssd_conv_b, ssd_dt_bias, ssd_a_log, ssd_d, ssd_norm_g):
    bsz, seq, d = x.shape
    ctx_len = ctx.shape[1]
    lat_rows = min(1024, seq)
    out_rows = min(2048, seq)
    dft_rows = min(512, seq)
    ctx_rows = min(256, ctx_len)

    pad = (-(bsz + 1)) % V7X_SUBLANES
    cc = jnp.concatenate([c, c_ctx[None], jnp.zeros((pad, d), F32)], axis=0)
    mod = _modulation(cc, mod_w, mod_b)
    rope_tabs = _rope_tables(seq)

    def row(v):
        return v.reshape(1, -1)

    xl, xc = x, ctx
    for i in range(DEPTH):
        last = i == DEPTH - 1
        j = i // 2
        ml = mod[i, :bsz].reshape(bsz, 6, 1, d)
        mc = jnp.broadcast_to(mod[i, bsz].reshape(1, 6, 1, d), (bsz, 6, 1, d))
        sh1, sc1, g1, sh2, sc2, g2 = (ml[:, k] for k in range(6))
        csh1, csc1, cg1, csh2, csc2, cg2 = (mc[:, k] for k in range(6))
        ln1 = (row(ln1_g[i]), row(ln1_b[i]))
        ln2 = (row(ln2_g[i]), row(ln2_b[i]))
        ffn_p = _ffn_params(ffn_up[i], ffn_conv_w[i], ffn_conv_b[i], ffn_down[i])

        if i % 2 == 0:
            w_in = ab_w_in[j].astype(BF16)
            qk_lo, qk_hi = 2 * GM_WIDTH, KV_OFF + KV_W
            q_scale = HEAD_DIM ** -0.5 * math.log2(math.e)
            w_in = jnp.concatenate([0.5 * w_in[:, :qk_lo], _rope_head_order(w_in[:, qk_lo:qk_hi]),
                                    w_in[:, qk_hi:]], axis=1)
            w_out = ab_w_out[j].astype(BF16)
            gm = (row(gm_ln_g[j]), row(gm_ln_b[j]), gm_ws[j].astype(BF16),
                  jnp.repeat(gm_bs[j].T, GM_WIDTH // GM_GROUPS, axis=1),
                  row(_rope_head_order(q_norm_g[j]) * q_scale), row(_rope_head_order(k_norm_g[j])))
            kv_len = seq + ctx_len
            a_c, q_c, k_all, v_all = _ab_in(xc, csh1, csc1, w_in, *gm, None, ctx_rows, kv_len, seq)
            a_l, q_l, k_all, v_all = _ab_in(xl, sh1, sc1, w_in, *gm, rope_tabs, lat_rows, kv_len, 0,
                                            kv_into=(k_all, v_all))
            mix_l = (a_l, _attention(q_l, k_all, v_all, 0, kv_len, lat_rows))
            if not last:
                mix_c = (a_c, _attention(q_c, k_all, v_all, seq, ctx_len, ctx_rows))
        else:
            w_in = cd_w_in[j].astype(BF16)
            w_out = cd_w_out[j].astype(BF16)
            wfz = w_in[:, :SSM_OFF]
            wx = w_in[:, SSM_OFF:SSM_OFF + SSD_CONV_DIM]
            ndt = 2 * SSD_HEADS
            wdt = jnp.pad(w_in[:, SSM_OFF + SSD_CONV_DIM:], ((0, 0), (0, V7X_LANES - ndt)))
            dtb = jnp.pad(ssd_dt_bias[j].reshape(1, ndt), ((0, 0), (0, V7X_LANES - ndt)))
            alog = jnp.pad(ssd_a_log[j].reshape(1, ndt), ((0, 0), (0, V7X_LANES - ndt)))
            cw, cb = 0.5 * ssd_conv_w[j], 0.5 * row(ssd_conv_b[j])
            dskip = row(jnp.repeat(ssd_d[j], SSD_HEADDIM))
            f_l, z_l, xs_l, b_l, c_l, dt_l = _cd_in(xl, sh1, sc1, wfz, wx, wdt, cw, cb, dtb, lat_rows)
            if last:
                xs_c, b_c, _, dt_c = _cd_in(xc, csh1, csc1, None, wx, wdt, cw, cb, dtb, ctx_rows)
            else:
                raise NotImplementedError("odd layers with a context output are not part of this block")
            s_l = _ssd(xs_c, b_c, dt_c, xs_l, b_l, c_l, dt_l, z_l, alog, dskip, row(ssd_norm_g[j]))
            mix_l = (_fourier(f_l, *_dft_tables(seq), dft_rows), s_l)

        xl = _out_ln(*mix_l, w_out, xl, g1, *ln1, out_rows)
        xl = _ffn(xl, sh2, sc2, g2, *ffn_p, *ln2, lat_rows)
        if not last:
            xc = _out_ln(*mix_c, w_out, xc, cg1, *ln1, ctx_rows)
            xc = _ffn(xc, csh2, csc2, cg2, *ffn_p, *ln2, ctx_rows)
    return xl
```

```python
import functools
import math

import jax
import jax.numpy as jnp
import numpy as np
from jax import lax
from jax.experimental import pallas as pl
from jax.experimental.pallas import tpu as pltpu

D_MODEL = 1024
DEPTH = 2
GRID_W = 64
CHUNK = 128
GM_GROUPS = 4
GM_WIDTH = 512
HEAD_DIM = 128
N_Q_HEADS = 4
N_KV_HEADS = 2
ATTN_WIDTH = 512
KV_W = 256
ROPE_THETA = 10000.0
FN_GROUPS = 4
FN_GDIM = 128
FN_WIDTH = 512
SSD_HEADDIM = 64
SSD_HEADS = 8
SSD_GROUPS = 2
SSD_HPG = 4
SSD_STATE = 128
SSD_WIDTH = 512
SSD_GN = 256
SSD_CONV_DIM = 1024
D_FF = 2816
CONV_W = 3
KV_OFF = 2 * GM_WIDTH + ATTN_WIDTH
SSM_OFF = FN_WIDTH + SSD_WIDTH
DN_ALPHA = (2 * DEPTH) ** 0.25
EPS = 1e-6

V7X_LANES = 128
V7X_SUBLANES = 8
V7X_VMEM_BYTES = 64 * 1024 * 1024
VMEM_CHIP_HEADROOM = 3 << 20
VMEM_TEMP_ALLOWANCE = 8 << 20
MOD_COLS = 2048
AB_SUB = 256
FF_CHUNK = 256
SSD_CHUNK = 256

BF16 = jnp.bfloat16
F32 = jnp.float32
HALO = V7X_SUBLANES


def _vmem_limit(nbytes):
    return int(min(nbytes * 3 // 2 + VMEM_TEMP_ALLOWANCE, V7X_VMEM_BYTES - VMEM_CHIP_HEADROOM))


def _resident(shape):
    nd = len(shape)
    return pl.BlockSpec(shape, lambda *_: (0,) * nd, pipeline_mode=pl.Buffered(1))


def _dot(a, b):
    return jnp.dot(a, b, preferred_element_type=F32)


def _dot_nt(a, b):
    return lax.dot_general(a, b, (((1,), (1,)), ((), ())), preferred_element_type=F32)


def _layer_norm(z, g, b, eps=EPS):
    mu = jnp.mean(z, axis=-1, keepdims=True)
    zc = z - mu
    var = jnp.mean(zc * zc, axis=-1, keepdims=True)
    return zc * lax.rsqrt(var + eps) * g + b


def _deepnorm(x, gate, y, g, b):
    return _layer_norm(x + (gate * (1.0 / DN_ALPHA)) * y, g, b, eps=EPS / DN_ALPHA ** 2)


def _rms_norm(z, g):
    return z * lax.rsqrt(jnp.mean(z * z, axis=-1, keepdims=True) + EPS) * g


def _softplus(z):
    return jnp.maximum(z, 0.0) + jnp.log1p(jnp.exp(-jnp.abs(z)))


def _gelu_tanh_of_half(h):
    c1 = math.sqrt(2.0 / math.pi)
    return h + h * jnp.tanh(h * (2.0 * c1 + (8.0 * c1 * 0.044715) * (h * h)))


def _silu(z):
    h = 0.5 * z
    return h + h * jnp.tanh(h)


def _conv3(u, w, b, rows):
    tiles, c = rows // V7X_SUBLANES, u.shape[1]
    u3 = u.reshape(tiles + 2, V7X_SUBLANES, c)
    sub = lax.broadcasted_iota(jnp.int32, (tiles, V7X_SUBLANES, c), 1)
    dn = pltpu.roll(u3, 1, 1)
    up = pltpu.roll(u3, V7X_SUBLANES - 1, 1)
    prev = jnp.where(sub == 0, dn[0:tiles], dn[1:tiles + 1])
    nxt = jnp.where(sub == V7X_SUBLANES - 1, up[2:tiles + 2], up[1:tiles + 1])
    return (b + w[0:1] * prev + w[1:2] * u3[1:tiles + 1] + w[2:3] * nxt).reshape(rows, c)


def _fill_halo_tile(hm_ref, x_ref, xp_ref, xn_ref, sh, sc, rows):
    t = pl.program_id(1)
    nt = pl.num_programs(1)
    keep_prev = jnp.where(t > 0, 1.0, 0.0)
    keep_next = jnp.where(t < nt - 1, 1.0, 0.0)
    hm_ref[HALO:HALO + rows, :] = (x_ref[0] * (1.0 + sc) + sh).astype(BF16)
    hm_ref[0:HALO, :] = ((xp_ref[0] * (1.0 + sc) + sh) * keep_prev).astype(BF16)
    hm_ref[HALO + rows:2 * HALO + rows, :] = ((xn_ref[0] * (1.0 + sc) + sh) * keep_next).astype(BF16)


def _halo_specs(rows, seq, d):
    nb = rows // HALO
    last = seq // HALO - 1
    return [
        pl.BlockSpec((1, rows, d), lambda b, t: (b, t, 0)),
        pl.BlockSpec((1, HALO, d), lambda b, t: (b, jnp.maximum(t * nb - 1, 0), 0)),
        pl.BlockSpec((1, HALO, d), lambda b, t: (b, jnp.minimum((t + 1) * nb, last), 0)),
    ]


def _row_spec(d):
    return pl.BlockSpec((1, 1, d), lambda b, t: (b, 0, 0))


def _mod_kernel(c_ref, w_ref, b_ref, o_ref):
    c = c_ref[...]
    act = c * jax.nn.sigmoid(c)
    o_ref[0] = jnp.dot(act, w_ref[0], preferred_element_type=F32,
                       precision=lax.Precision.HIGHEST) + b_ref[0]


def _modulation(cc, mod_w, mod_b):
    rows, d = cc.shape
    depth, _, n = mod_w.shape
    tn = MOD_COLS
    return pl.pallas_call(
        _mod_kernel,
        out_shape=jax.ShapeDtypeStruct((depth, rows, n), F32),
        grid=(depth, n // tn),
        in_specs=[
            pl.BlockSpec((rows, d), lambda i, j: (0, 0)),
            pl.BlockSpec((1, d, tn), lambda i, j: (i, 0, j)),
            pl.BlockSpec((1, 1, tn), lambda i, j: (i, 0, j)),
        ],
        out_specs=pl.BlockSpec((1, rows, tn), lambda i, j: (i, 0, j)),
        compiler_params=pltpu.CompilerParams(
            dimension_semantics=("arbitrary", "arbitrary"),
            vmem_limit_bytes=_vmem_limit(2 * d * tn * 4)),
        name="modulation",
    )(cc, mod_w, mod_b.reshape(depth, 1, n))


def _ab_in_kernel(*refs, use_rope, rows):
    (x_ref, sh_ref, sc_ref, w_ref, lng_ref, lnb_ref, ws_ref, bias_ref, qg_ref, kg_ref) = refs[:10]
    if use_rope:
        cos_ref, sin_ref = refs[10:12]
    a_ref, q_ref, k_ref, v_ref = refs[-4:]

    gd = GM_WIDTH // GM_GROUPS
    slab_rows = min(AB_SUB, rows)

    def slab(i, carry):
        r0 = pl.multiple_of(i * slab_rows, slab_rows)
        rs = pl.ds(r0, slab_rows)
        h = (x_ref[0, rs, :] * (1.0 + sc_ref[0]) + sh_ref[0]).astype(BF16)

        u = _gelu_tanh_of_half(_dot(h, w_ref[:, 0:GM_WIDTH]))
        vn = _layer_norm(_gelu_tanh_of_half(_dot(h, w_ref[:, GM_WIDTH:2 * GM_WIDTH])),
                         lng_ref[...], lnb_ref[...])
        vb = vn.astype(BF16)
        for c in range(slab_rows // CHUNK):
            r = slice(c * CHUNK, (c + 1) * CHUNK)
            for g in range(GM_GROUPS):
                l = slice(g * gd, (g + 1) * gd)
                mixed = _dot(ws_ref[g], vb[r, l]) + bias_ref[:, l]
                a_ref[0, pl.ds(r0 + c * CHUNK, CHUNK), l] = (u[r, l] * mixed).astype(BF16)

        def rope(z):
            if not use_rope:
                return z
            return z * cos_ref[rs, :] + pltpu.roll(z, HEAD_DIM // 2, 1) * sin_ref[rs, :]

        pq = _dot(h, w_ref[:, 2 * GM_WIDTH:KV_OFF])
        for hh in range(N_Q_HEADS):
            l = slice(hh * HEAD_DIM, (hh + 1) * HEAD_DIM)
            q_ref[0, rs, l] = rope(_rms_norm(pq[:, l], qg_ref[...])).astype(BF16)
        pk = _dot(h, w_ref[:, KV_OFF:KV_OFF + KV_W])
        for hh in range(N_KV_HEADS):
            l = slice(hh * HEAD_DIM, (hh + 1) * HEAD_DIM)
            k_ref[0, rs, l] = rope(_rms_norm(pk[:, l], kg_ref[...])).astype(BF16)
        v_ref[0, rs, :] = _dot(h, w_ref[:, KV_OFF + KV_W:KV_OFF + 2 * KV_W]).astype(BF16)
        return carry

    lax.fori_loop(0, rows // slab_rows, slab, 0)


def _ab_in(x, sh, sc, w, lng, lnb, ws, bias2d, qg, kg, rope_tabs, rows, kv_len, kv_row0, kv_into=None):
    bsz, seq, d = x.shape
    assert kv_row0 % rows == 0
    kv_blk0 = kv_row0 // rows
    n = w.shape[1]
    use_rope = rope_tabs is not None
    in_specs = [
        pl.BlockSpec((1, rows, d), lambda b, t: (b, t, 0)),
        _row_spec(d), _row_spec(d),
        _resident((d, n)),
        _resident((1, GM_WIDTH)), _resident((1, GM_WIDTH)),
        _resident((GM_GROUPS, CHUNK, CHUNK)),
        _resident((CHUNK, GM_WIDTH)),
        _resident((1, HEAD_DIM)), _resident((1, HEAD_DIM)),
    ]
    args = [x, sh, sc, w, lng, lnb, ws, bias2d, qg, kg]
    if use_rope:
        in_specs += [pl.BlockSpec((rows, HEAD_DIM), lambda b, t: (t, 0))] * 2
        args += list(rope_tabs)
    aliases = {}
    if kv_into is not None:
        aliases = {len(args): 2, len(args) + 1: 3}
        in_specs += [pl.BlockSpec(memory_space=pl.ANY)] * 2
        args += list(kv_into)
    out_shape = [
        jax.ShapeDtypeStruct((bsz, seq, GM_WIDTH), BF16),
        jax.ShapeDtypeStruct((bsz, seq, ATTN_WIDTH), BF16),
        jax.ShapeDtypeStruct((bsz, kv_len, KV_W), BF16),
        jax.ShapeDtypeStruct((bsz, kv_len, KV_W), BF16),
    ]
    out_specs = [
        pl.BlockSpec((1, rows, GM_WIDTH), lambda b, t: (b, t, 0)),
        pl.BlockSpec((1, rows, ATTN_WIDTH), lambda b, t: (b, t, 0)),
        pl.BlockSpec((1, rows, KV_W), lambda b, t: (b, t + kv_blk0, 0)),
        pl.BlockSpec((1, rows, KV_W), lambda b, t: (b, t + kv_blk0, 0)),
    ]
    est = d * n * 2 + 2 * rows * d * 4 + 6 * rows * n * 4
    return pl.pallas_call(
        functools.partial(_ab_in_kernel, use_rope=use_rope, rows=rows),
        out_shape=out_shape,
        grid=(bsz, seq // rows),
        in_specs=in_specs,
        out_specs=out_specs,
        input_output_aliases=aliases,
        compiler_params=pltpu.CompilerParams(
            dimension_semantics=("arbitrary", "arbitrary"),
            vmem_limit_bytes=_vmem_limit(est)),
        name="ab_in_rope" if use_rope else "ab_in_ctx",
    )(*args)


def _attn_kernel(q_ref, k_ref, v_ref, o_ref, vt_ref, *, rows, blk):
    @pl.when(pl.program_id(2) == 0)
    def _():
        vt_ref[...] = v_ref[0].astype(F32).T

    k = k_ref[0]
    vt = vt_ref[...].astype(BF16)
    for h in range(N_Q_HEADS // N_KV_HEADS):
        l_ = slice(h * HEAD_DIM, (h + 1) * HEAD_DIM)
        for r0 in range(0, rows, blk):
            st = _dot_nt(k, q_ref[0, r0:r0 + blk, l_])
            m = jnp.max(st, axis=0, keepdims=True)
            p = jnp.exp2(st - m)
            l = jnp.sum(p, axis=0, keepdims=True)
            ot = _dot(vt, p.astype(BF16)) / l
            o_ref[0, r0:r0 + blk, l_] = ot.T.astype(BF16)


def _attention(q, k, v, k_row0, sk, rows, blk=512):
    bsz, sq, _ = q.shape
    assert k_row0 % sk == 0
    kb = k_row0 // sk
    gw = (N_Q_HEADS // N_KV_HEADS) * HEAD_DIM
    est = 4 * blk * sk * 12 + 4 * sk * HEAD_DIM * 2 + 4 * rows * gw * 2 + sk * HEAD_DIM * 4
    return pl.pallas_call(
        functools.partial(_attn_kernel, rows=rows, blk=min(blk, rows)),
        out_shape=jax.ShapeDtypeStruct((bsz, sq, ATTN_WIDTH), BF16),
        grid=(bsz, N_KV_HEADS, sq // rows),
        in_specs=[
            pl.BlockSpec((1, rows, gw), lambda b, h, t: (b, t, h)),
            pl.BlockSpec((1, sk, HEAD_DIM), lambda b, h, t: (b, kb, h)),
            pl.BlockSpec((1, sk, HEAD_DIM), lambda b, h, t: (b, kb, h)),
        ],
        out_specs=pl.BlockSpec((1, rows, gw), lambda b, h, t: (b, t, h)),
        scratch_shapes=[pltpu.VMEM((HEAD_DIM, sk), F32)],
        compiler_params=pltpu.CompilerParams(
            dimension_semantics=("arbitrary", "arbitrary", "arbitrary"),
            vmem_limit_bytes=_vmem_limit(est)),
        name="attention",
    )(q, k, v)


def _out_ln_kernel(a1_ref, a2_ref, w_ref, x_ref, g_ref, lng_ref, lnb_ref, o_ref):
    half = a1_ref.shape[2]
    y = _dot(a1_ref[0], w_ref[0:half, :]) + _dot(a2_ref[0], w_ref[half:2 * half, :])
    o_ref[0] = _deepnorm(x_ref[0], g_ref[0], y, lng_ref[...], lnb_ref[...])


def _out_ln(a1, a2, w, x, gate, lng, lnb, rows):
    bsz, seq, d = x.shape
    half = a1.shape[2]
    est = 2 * half * d * 2 + 4 * rows * d * 4 + 4 * rows * half * 2 + 2 * rows * d * 4
    return pl.pallas_call(
        _out_ln_kernel,
        out_shape=jax.ShapeDtypeStruct((bsz, seq, d), F32),
        grid=(bsz, seq // rows),
        in_specs=[
            pl.BlockSpec((1, rows, half), lambda b, t: (b, t, 0)),
            pl.BlockSpec((1, rows, half), lambda b, t: (b, t, 0)),
            _resident((2 * half, d)),
            pl.BlockSpec((1, rows, d), lambda b, t: (b, t, 0)),
            _row_spec(d),
            _resident((1, d)), _resident((1, d)),
        ],
        out_specs=pl.BlockSpec((1, rows, d), lambda b, t: (b, t, 0)),
        compiler_params=pltpu.CompilerParams(
            dimension_semantics=("arbitrary", "arbitrary"),
            vmem_limit_bytes=_vmem_limit(est)),
        name="out_ln",
    )(a1, a2, w, x, gate, lng, lnb)


def _ffn_kernel(x_ref, xp_ref, xn_ref, sh_ref, sc_ref, g_ref, wu1_ref, wu2_ref, cw1_ref, cw2_ref,
                cb1_ref, cb2_ref, wd_ref, lng_ref, lnb_ref, o_ref, hm_ref, u_ref, gt_ref, *, rows):
    _fill_halo_tile(hm_ref, x_ref, xp_ref, xn_ref, sh_ref[0], sc_ref[0], rows)
    n_chunks = wu1_ref.shape[1] // FF_CHUNK

    def cols(ref, c):
        off = c * FF_CHUNK
        return ref[:, pl.ds(off if isinstance(off, int) else pl.multiple_of(off, FF_CHUNK), FF_CHUNK)]
    tiles = rows // V7X_SUBLANES
    sub = lax.broadcasted_iota(jnp.int32, (tiles, V7X_SUBLANES, FF_CHUNK), 1)

    def up_proj(c, slot):
        hm = hm_ref[...]
        u_ref[slot, 0] = _dot(hm, cols(wu1_ref, c)).reshape(tiles + 2, V7X_SUBLANES, FF_CHUNK)
        u_ref[slot, 1] = _dot(hm, cols(wu2_ref, c)).reshape(tiles + 2, V7X_SUBLANES, FF_CHUNK)

    def conv_gate(c, slot):
        def conv(half, cw_ref, cb_ref):
            u = u_ref[slot, half]
            dn = pltpu.roll(u, 1, 1)
            up = pltpu.roll(u, V7X_SUBLANES - 1, 1)
            prev = jnp.where(sub == 0, dn[0:tiles], dn[1:tiles + 1])
            nxt = jnp.where(sub == V7X_SUBLANES - 1, up[2:tiles + 2], up[1:tiles + 1])
            w = cols(cw_ref, c)
            return cols(cb_ref, c) + w[0:1] * prev + w[1:2] * u[1:tiles + 1] + w[2:3] * nxt

        a1 = conv(0, cw1_ref, cb1_ref)
        a2 = conv(1, cw2_ref, cb2_ref)
        gt_ref[c] = (a1 * (a2 + a2 * jnp.tanh(a2))).reshape(rows, FF_CHUNK)

    up_proj(0, 0)

    def pair(k, carry):
        c = 2 * k
        up_proj(c + 1, 1)
        conv_gate(c, 0)
        up_proj(c + 2, 0)
        conv_gate(c + 1, 1)
        return carry

    lax.fori_loop(0, (n_chunks - 1) // 2, pair, 0)
    conv_gate(n_chunks - 1, 0)

    f = _dot(gt_ref[0].astype(BF16), wd_ref[0])
    for c in range(1, n_chunks):
        f = f + _dot(gt_ref[c].astype(BF16), wd_ref[c])
    o_ref[0] = _deepnorm(x_ref[0], g_ref[0], f, lng_ref[...], lnb_ref[...])


def _ffn(x, sh, sc, gate, wu1, wu2, cw1, cw2, cb1, cb2, wd, lng, lnb, rows):
    bsz, seq, d = x.shape
    dff = wu1.shape[1]
    nc = dff // FF_CHUNK
    assert dff % FF_CHUNK == 0 and nc % 2 == 1, "the pipeline peels one chunk and pairs the rest"
    est = (3 * nc * d * FF_CHUNK * 2 + 4 * rows * d * 4 + (rows + 2 * HALO) * d * 2
           + nc * rows * FF_CHUNK * 4 + 8 * (rows + 2 * HALO) * FF_CHUNK * 4 + 2 * rows * d * 4)
    return pl.pallas_call(
        functools.partial(_ffn_kernel, rows=rows),
        out_shape=jax.ShapeDtypeStruct((bsz, seq, d), F32),
        grid=(bsz, seq // rows),
        in_specs=_halo_specs(rows, seq, d) + [
            _row_spec(d), _row_spec(d), _row_spec(d),
            _resident((d, dff)), _resident((d, dff)),
            _resident((CONV_W, dff)), _resident((CONV_W, dff)),
            _resident((1, dff)), _resident((1, dff)),
            _resident((nc, FF_CHUNK, d)),
            _resident((1, d)), _resident((1, d)),
        ],
        out_specs=pl.BlockSpec((1, rows, d), lambda b, t: (b, t, 0)),
        scratch_shapes=[
            pltpu.VMEM((rows + 2 * HALO, d), BF16),
            pltpu.VMEM((2, 2, rows // HALO + 2, HALO, FF_CHUNK), F32),
            pltpu.VMEM((nc, rows, FF_CHUNK), F32),
        ],
        compiler_params=pltpu.CompilerParams(
            dimension_semantics=("arbitrary", "arbitrary"),
            vmem_limit_bytes=_vmem_limit(est)),
        name="conv_ffn",
    )(x, x, x, sh, sc, gate, wu1, wu2, cw1, cw2, cb1, cb2, wd, lng, lnb)


def _cd_in_kernel(*refs, with_fz, rows):
    x_ref, xp_ref, xn_ref, sh_ref, sc_ref = refs[:5]
    if with_fz:
        wfz_ref, wx_ref, wdt_ref, cw_ref, cb_ref, dtb_ref = refs[5:11]
        f_ref, z_ref, xs_ref, b_ref, c_ref, dt_ref, hm_ref = refs[11:]
    else:
        wx_ref, wdt_ref, cw_ref, cb_ref, dtb_ref = refs[5:10]
        xs_ref, b_ref, c_ref, dt_ref, hm_ref = refs[10:]
    _fill_halo_tile(hm_ref, x_ref, xp_ref, xn_ref, sh_ref[0], sc_ref[0], rows)
    hc = hm_ref[HALO:HALO + rows, :]
    if with_fz:
        pfz = _dot(hc, wfz_ref[...])
        f_ref[0] = pfz[:, 0:FN_WIDTH].astype(BF16)
        z_ref[0] = pfz[:, FN_WIDTH:SSM_OFF]
    a = _conv3(_dot(hm_ref[...], wx_ref[...]), cw_ref[...], cb_ref[...], rows)
    s = a + a * jnp.tanh(a)
    xs_ref[0] = s[:, 0:SSD_WIDTH]
    b_ref[0] = s[:, SSD_WIDTH:SSD_WIDTH + SSD_GN].astype(BF16)
    c_ref[0] = s[:, SSD_WIDTH + SSD_GN:SSD_CONV_DIM].astype(BF16)
    dt_ref[0] = _softplus(_dot(hc, wdt_ref[...]) + dtb_ref[...])


def _cd_in(x, sh, sc, wfz, wx, wdt, cw, cb, dtb, rows):
    bsz, seq, d = x.shape
    with_fz = wfz is not None
    in_specs = _halo_specs(rows, seq, d) + [_row_spec(d), _row_spec(d)]
    args = [x, x, x, sh, sc]
    out_shape, out_specs = [], []
    if with_fz:
        in_specs.append(_resident((d, SSM_OFF)))
        args.append(wfz)
        out_shape += [jax.ShapeDtypeStruct((bsz, seq, FN_WIDTH), BF16),
                      jax.ShapeDtypeStruct((bsz, seq, SSD_WIDTH), F32)]
        out_specs += [pl.BlockSpec((1, rows, FN_WIDTH), lambda b, t: (b, t, 0)),
                      pl.BlockSpec((1, rows, SSD_WIDTH), lambda b, t: (b, t, 0))]
    in_specs += [_resident((d, SSD_CONV_DIM)), _resident((d, V7X_LANES)),
                 _resident((CONV_W, SSD_CONV_DIM)), _resident((1, SSD_CONV_DIM)),
                 _resident((1, V7X_LANES))]
    args += [wx, wdt, cw, cb, dtb]
    out_shape += [jax.ShapeDtypeStruct((bsz, seq, SSD_WIDTH), F32),
                  jax.ShapeDtypeStruct((bsz, seq, SSD_GN), BF16),
                  jax.ShapeDtypeStruct((bsz, seq, SSD_GN), BF16),
                  jax.ShapeDtypeStruct((bsz, seq, V7X_LANES), F32)]
    out_specs += [pl.BlockSpec((1, rows, SSD_WIDTH), lambda b, t: (b, t, 0)),
                  pl.BlockSpec((1, rows, SSD_GN), lambda b, t: (b, t, 0)),
                  pl.BlockSpec((1, rows, SSD_GN), lambda b, t: (b, t, 0)),
                  pl.BlockSpec((1, rows, V7X_LANES), lambda b, t: (b, t, 0))]
    est = d * 2200 * 2 + 2 * rows * d * 4 + 8 * (rows + 2 * HALO) * SSD_CONV_DIM * 4
    return pl.pallas_call(
        functools.partial(_cd_in_kernel, with_fz=with_fz, rows=rows),
        out_shape=out_shape,
        grid=(bsz, seq // rows),
        in_specs=in_specs,
        out_specs=out_specs,
        scratch_shapes=[pltpu.VMEM((rows + 2 * HALO, d), BF16)],
        compiler_params=pltpu.CompilerParams(
            dimension_semantics=("arbitrary", "arbitrary"),
            vmem_limit_bytes=_vmem_limit(est)),
        name="cd_in" if with_fz else "cd_in_ctx",
    )(*args)


def _cumsum_rows(z, reverse):
    n = z.shape[0]
    idx = lax.broadcasted_iota(jnp.int32, z.shape, 0)
    k = 1
    while k < n:
        if reverse:
            z = z + jnp.where(idx < n - k, pltpu.roll(z, n - k, 0), 0.0)
        else:
            z = z + jnp.where(idx >= k, pltpu.roll(z, k, 0), 0.0)
        k *= 2
    return z


def _lane_expand(row, lane0, width):
    lane = lax.broadcasted_iota(jnp.int32, (1, SSD_HPG * width), 1)
    out = jnp.zeros((1, SSD_HPG * width), F32)
    for j in range(SSD_HPG):
        val = jnp.broadcast_to(row[:, lane0 + j:lane0 + j + 1], (1, SSD_HPG * width))
        out = jnp.where(lane // width == j, val, out)
    return out


def _ssd_chunk(x, bm, cm, dt, a_row, h_ref, lane0, reverse, need_y):
    q = x.shape[0]
    gw = SSD_HPG * SSD_HEADDIM
    acum = _cumsum_rows(dt * a_row, reverse)
    total = acum[0:1, :] if reverse else acum[q - 1:q, :]
    acum_t = acum.T
    dt_t = dt.T
    total_t = jnp.broadcast_to(acum_t[:, 0:1] if reverse else acum_t[:, q - 1:q], acum_t.shape)
    wgt_t = dt_t * jnp.exp2(total_t - acum_t)
    lane_g = lax.broadcasted_iota(jnp.int32, (q, gw), 1) // SSD_HEADDIM
    lane_gh = lax.broadcasted_iota(jnp.int32, (SSD_STATE, gw), 1) // SSD_HEADDIM
    if need_y:
        srow_t = acum_t - jnp.log2(dt_t)
        ti = lax.broadcasted_iota(jnp.int32, (q, q), 0)
        si = lax.broadcasted_iota(jnp.int32, (q, q), 1)
        mask = (si >= ti) if reverse else (si <= ti)
    ys = []
    for g in range(SSD_GROUPS):
        bg = bm[:, g * SSD_STATE:(g + 1) * SSD_STATE]
        xgb = x[:, g * gw:(g + 1) * gw].astype(BF16)
        x_heads = [jnp.where(lane_g == j, xgb, jnp.zeros_like(xgb)) for j in range(SSD_HPG)]
        hg = h_ref[g]
        if need_y:
            cg = cm[:, g * SSD_STATE:(g + 1) * SSD_STATE]
            cb = _dot_nt(cg, bg)
            cgf = cg.astype(F32)
            hgb = hg.astype(BF16)
            lhs, rhs = [], []
            for j in range(SSD_HPG):
                jl = lane0 + g * SSD_HPG + j
                col = jnp.broadcast_to(acum[:, jl:jl + 1], (q, q))
                row = jnp.broadcast_to(srow_t[jl:jl + 1, :], (q, q))
                lhs.append((cb * jnp.exp2(jnp.where(mask, col - row, -jnp.inf))).astype(BF16))
                lhs.append((cgf * jnp.exp2(col[:, 0:SSD_STATE])).astype(BF16))
                rhs.append(x_heads[j])
                rhs.append(jnp.where(lane_gh == j, hgb, jnp.zeros_like(hgb)))
            ys.append(_dot(jnp.concatenate(lhs, axis=1), jnp.concatenate(rhs, axis=0)))
        bgt = bg.astype(F32).T
        wb = [(bgt * jnp.broadcast_to(wgt_t[lane0 + g * SSD_HPG + j:lane0 + g * SSD_HPG + j + 1, :],
                                      (SSD_STATE, q))).astype(BF16) for j in range(SSD_HPG)]
        st = _dot(jnp.concatenate(wb, axis=1), jnp.concatenate(x_heads, axis=0))
        h_ref[g] = hg * jnp.exp2(_lane_expand(total, lane0 + g * SSD_HPG, SSD_HEADDIM)) + st
    if need_y:
        return jnp.concatenate(ys, axis=1)
    return None


def _ssd_kernel(xc_ref, bc_ref, dtc_ref, xl_ref, bl_ref, cl_ref, dtl_ref, z_ref, alog_ref,
                dskip_ref, ng_ref, o_ref, y_ref, h_ref):
    a_row = -jnp.exp(alog_ref[...]) * math.log2(math.e)
    nc_ctx = xc_ref.shape[1] // SSD_CHUNK
    nc_lat = xl_ref.shape[1] // SSD_CHUNK

    def rows_of(c):
        return pl.ds(pl.multiple_of(c * SSD_CHUNK, SSD_CHUNK), SSD_CHUNK)

    def ctx_step(c, direction):
        r = rows_of(c)
        _ssd_chunk(xc_ref[0, r, :], bc_ref[0, r, :], None, dtc_ref[0, r, :], a_row,
                   h_ref.at[direction], direction * SSD_HEADS, direction == 1, False)

    def lat_step(c, direction):
        r = rows_of(c)
        y_ref[direction, r, :] = _ssd_chunk(
            xl_ref[0, r, :], bl_ref[0, r, :], cl_ref[0, r, :], dtl_ref[0, r, :], a_row,
            h_ref.at[direction], direction * SSD_HEADS, direction == 1, True)

    h_ref[...] = jnp.zeros_like(h_ref)

    def ctx_pair(i, carry):
        ctx_step(i, 0)
        ctx_step(nc_ctx - 1 - i, 1)
        return carry

    lax.fori_loop(0, nc_ctx, ctx_pair, 0)

    def lat_pair(i, carry):
        lat_step(i, 0)
        lat_step(nc_lat - 1 - i, 1)
        return carry

    lax.fori_loop(0, nc_lat, lat_pair, 0)

    def finish(c, carry):
        r = rows_of(c)
        zz = z_ref[0, r, :]
        yt = (y_ref[0, r, :] + y_ref[1, r, :] + xl_ref[0, r, :] * dskip_ref[...]) * _silu(zz)
        o_ref[0, r, :] = _rms_norm(yt, ng_ref[...]).astype(BF16)
        return carry

    lax.fori_loop(0, nc_lat, finish, 0)


def _ssd(xs_c, b_c, dt_c, xs_l, b_l, c_l, dt_l, z, alog_row, dskip_row, ng_row):
    bsz, sl, w = xs_l.shape
    sc = xs_c.shape[1]
    assert sl % SSD_CHUNK == 0 and sc % SSD_CHUNK == 0
    gw = SSD_HPG * SSD_HEADDIM

    def full(s, n):
        return pl.BlockSpec((1, s, n), lambda b: (b, 0, 0))

    est = (2 * (sl + sc) * (w * 4 + 2 * SSD_GN * 2 + V7X_LANES * 4) + 2 * sl * w * 4
           + 2 * sl * w * 2 + 2 * sl * w * 4)
    return pl.pallas_call(
        _ssd_kernel,
        out_shape=jax.ShapeDtypeStruct((bsz, sl, w), BF16),
        grid=(bsz,),
        in_specs=[
            full(sc, w), full(sc, SSD_GN), full(sc, V7X_LANES),
            full(sl, w), full(sl, SSD_GN), full(sl, SSD_GN), full(sl, V7X_LANES),
            full(sl, w),
            _resident((1, V7X_LANES)), _resident((1, w)), _resident((1, w)),
        ],
        out_specs=full(sl, w),
        scratch_shapes=[
            pltpu.VMEM((2, sl, w), F32),
            pltpu.VMEM((2, SSD_GROUPS, SSD_STATE, gw), F32),
        ],
        compiler_params=pltpu.CompilerParams(
            dimension_semantics=("arbitrary",),
            vmem_limit_bytes=_vmem_limit(est)),
        name="ssd_scan",
    )(xs_c, b_c, dt_c, xs_l, b_l, c_l, dt_l, z, alog_row, dskip_row, ng_row)


def _fourier_kernel(f_ref, cc_ref, sc_ref, w2_ref, o_ref, y_ref, *, rows):
    seq = f_ref.shape[1]
    for g in range(FN_GROUPS):
        l = slice(g * FN_GDIM, (g + 1) * FN_GDIM)
        fg = f_ref[0, :, l]
        y_ref[0:seq, l] = _dot(fg, cc_ref[...]).astype(BF16)
        y_ref[seq:2 * seq, l] = _dot(fg, sc_ref[...]).astype(BF16)
    norm = 1.0 / math.sqrt(seq * FN_GDIM)

    def tile(i, carry):
        r = pl.ds(pl.multiple_of(i * rows, rows), rows)
        o_ref[0, r, :] = (_dot(w2_ref[r, :], y_ref[...]) * norm).astype(BF16)
        return carry

    lax.fori_loop(0, seq // rows, tile, 0)


def _fourier(f, cc, sc, w2, rows):
    bsz, seq, w = f.shape
    est = seq * 2 * seq * 2 + 4 * seq * w * 2 + 2 * seq * w * 2 + 4 * rows * 2 * seq * 2
    return pl.pallas_call(
        functools.partial(_fourier_kernel, rows=rows),
        out_shape=jax.ShapeDtypeStruct((bsz, seq, w), BF16),
        grid=(bsz,),
        in_specs=[
            pl.BlockSpec((1, seq, w), lambda b: (b, 0, 0)),
            _resident((FN_GDIM, FN_GDIM)), _resident((FN_GDIM, FN_GDIM)),
            _resident((seq, 2 * seq)),
        ],
        out_specs=pl.BlockSpec((1, seq, w), lambda b: (b, 0, 0)),
        scratch_shapes=[pltpu.VMEM((2 * seq, w), BF16)],
        compiler_params=pltpu.CompilerParams(
            dimension_semantics=("arbitrary",),
            vmem_limit_bytes=_vmem_limit(est)),
        name="fourier_mix",
    )(f, cc, sc, w2)


def _rope_tables(seq):
    rows = seq // GRID_W
    row = jnp.repeat(jnp.arange(rows, dtype=F32), GRID_W)
    col = jnp.broadcast_to(jnp.arange(GRID_W, dtype=F32), (rows, GRID_W)).reshape(-1)
    n_freq = HEAD_DIM // 4
    inv = ROPE_THETA ** (-jnp.arange(n_freq, dtype=F32) / n_freq)
    ang = jnp.stack([row, col], axis=-1)[:, :, None] * inv
    cos, sin = jnp.cos(ang), jnp.sin(ang)
    cos_t = jnp.concatenate([cos[:, 0], cos[:, 1], cos[:, 0], cos[:, 1]], axis=-1)
    sin_t = jnp.concatenate([-sin[:, 0], -sin[:, 1], sin[:, 0], sin[:, 1]], axis=-1)
    return cos_t, sin_t


def _rope_head_order(m):
    lead = m.shape[:-1]
    nh = m.shape[-1] // HEAD_DIM
    m = m.reshape(lead + (nh, 2, 2, HEAD_DIM // 4))
    return jnp.swapaxes(m, -3, -2).reshape(lead + (nh * HEAD_DIM,))


def _dft_tables(seq):
    def tw(n):
        i = np.arange(n, dtype=np.int64)
        ang = ((i[:, None] * i[None, :]) % n).astype(np.float64) * (2.0 * math.pi / n)
        return np.cos(ang), np.sin(ang)

    cc, sc = tw(FN_GDIM)
    cl, sl = tw(seq)
    w2 = np.concatenate([cl, -sl], axis=1).astype(np.float32)
    return (jnp.asarray(cc.astype(np.float32), BF16), jnp.asarray(sc.astype(np.float32), BF16),
            jnp.asarray(w2, BF16))


def _ffn_params(w_up, conv_w, conv_b, w_down):
    d = w_up.shape[0]
    wu = w_up.astype(BF16)
    cb = conv_b.reshape(1, 2 * D_FF)
    return (wu[:, :D_FF], wu[:, D_FF:], conv_w[:, :D_FF], 0.5 * conv_w[:, D_FF:],
            cb[:, :D_FF], 0.5 * cb[:, D_FF:], w_down.astype(BF16).reshape(D_FF // FF_CHUNK, FF_CHUNK, d))


def kernel(x, c, ctx, c_ctx, mod_w, mod_b, ln1_g, ln1_b, ln2_g, ln2_b, ffn_up, ffn_conv_w, ffn_conv_b,
           ffn_down, ab_w_in, ab_w_out, gm_ln_g, gm_ln_b, gm_ws, gm_bs, q_norm_g, k_norm_g, cd_w_in,
           cd_w_out, ssd_conv_w, ssd_conv_b, ssd_dt_bias, ssd_a_log, ssd_d, ssd_norm_g):
    bsz, seq, d = x.shape
    ctx_len = ctx.shape[1]
    lat_rows = min(1024, seq)
    out_rows = min(2048, seq)
    dft_rows = min(512, seq)
    ctx_rows = min(256, ctx_len)

    pad = (-(bsz + 1)) % V7X_SUBLANES
    cc = jnp.concatenate([c, c_ctx[None], jnp.zeros((pad, d), F32)], axis=0)
    mod = _modulation(cc, mod_w, mod_b)
    rope_tabs = _rope_tables(seq)

    def row(v):
        return v.reshape(1, -1)

    xl, xc = x, ctx
    for i in range(DEPTH):
        last = i == DEPTH - 1
        j = i // 2
        ml = mod[i, :bsz].reshape(bsz, 6, 1, d)
        mc = jnp.broadcast_to(mod[i, bsz].reshape(1, 6, 1, d), (bsz, 6, 1, d))
        sh1, sc1, g1, sh2, sc2, g2 = (ml[:, k] for k in range(6))
        csh1, csc1, cg1, csh2, csc2, cg2 = (mc[:, k] for k in range(6))
        ln1 = (row(ln1_g[i]), row(ln1_b[i]))
        ln2 = (row(ln2_g[i]), row(ln2_b[i]))
        ffn_p = _ffn_params(ffn_up[i], ffn_conv_w[i], ffn_conv_b[i], ffn_down[i])

        if i % 2 == 0:
            w_in = ab_w_in[j].astype(BF16)
            qk_lo, qk_hi = 2 * GM_WIDTH, KV_OFF + KV_W
            q_scale = HEAD_DIM ** -0.5 * math.log2(math.e)
            w_in = jnp.concatenate([0.5 * w_in[:, :qk_lo], _rope_head_order(w_in[:, qk_lo:qk_hi]),
                                    w_in[:, qk_hi:]], axis=1)
            w_out = ab_w_out[j].astype(BF16)
            gm = (row(gm_ln_g[j]), row(gm_ln_b[j]), gm_ws[j].astype(BF16),
                  jnp.repeat(gm_bs[j].T, GM_WIDTH // GM_GROUPS, axis=1),
                  row(_rope_head_order(q_norm_g[j]) * q_scale), row(_rope_head_order(k_norm_g[j])))
            kv_len = seq + ctx_len
            a_c, q_c, k_all, v_all = _ab_in(xc, csh1, csc1, w_in, *gm, None, ctx_rows, kv_len, seq)
            a_l, q_l, k_all, v_all = _ab_in(xl, sh1, sc1, w_in, *gm, rope_tabs, lat_rows, kv_len, 0,
                                            kv_into=(k_all, v_all))
            mix_l = (a_l, _attention(q_l, k_all, v_all, 0, kv_len, lat_rows))
            if not last:
                mix_c = (a_c, _attention(q_c, k_all, v_all, seq, ctx_len, ctx_rows))
        else:
            w_in = cd_w_in[j].astype(BF16)
            w_out = cd_w_out[j].astype(BF16)
            wfz = w_in[:, :SSM_OFF]
            wx = w_in[:, SSM_OFF:SSM_OFF + SSD_CONV_DIM]
            ndt = 2 * SSD_HEADS
            wdt = jnp.pad(w_in[:, SSM_OFF + SSD_CONV_DIM:], ((0, 0), (0, V7X_LANES - ndt)))
            dtb = jnp.pad(ssd_dt_bias[j].reshape(1, ndt), ((0, 0), (0, V7X_LANES - ndt)))
            alog = jnp.pad(ssd_a_log[j].reshape(1, ndt), ((0, 0), (0, V7X_LANES - ndt)))
            cw, cb = 0.5 * ssd_conv_w[j], 0.5 * row(ssd_conv_b[j])
            dskip = row(jnp.repeat(ssd_d[j], SSD_HEADDIM))
            f_l, z_l, xs_l, b_l, c_l, dt_l = _cd_in(xl, sh1, sc1, wfz, wx, wdt, cw, cb, dtb, lat_rows)
            if last:
                xs_c, b_c, _, dt_c = _cd_in(xc, csh1, csc1, None, wx, wdt, cw, cb, dtb, ctx_rows)
            else:
                raise NotImplementedError("odd layers with a context output are not part of this block")
            s_l = _ssd(xs_c, b_c, dt_c, xs_l, b_l, c_l, dt_l, z_l, alog, dskip, row(ssd_norm_g[j]))
            mix_l = (_fourier(f_l, *_dft_tables(seq), dft_rows), s_l)

        xl = _out_ln(*mix_l, w_out, xl, g1, *ln1, out_rows)
        xl = _ffn(xl, sh2, sc2, g2, *ffn_p, *ln2, lat_rows)
        if not last:
            xc = _out_ln(*mix_c, w_out, xc, cg1, *ln1, ctx_rows)
            xc = _ffn(xc, csh2, csc2, cg2, *ffn_p, *ln2, ctx_rows)
    return xl
```

```python
import functools
import math

import jax
import jax.numpy as jnp
import numpy as np
from jax import lax
from jax.experimental import pallas as pl
from jax.experimental.pallas import tpu as pltpu

D_MODEL = 1024
DEPTH = 2
GRID_W = 64
CHUNK = 128
GM_GROUPS = 4
GM_WIDTH = 512
HEAD_DIM = 128
N_Q_HEADS = 4
N_KV_HEADS = 2
ATTN_WIDTH = 512
KV_W = 256
ROPE_THETA = 10000.0
FN_GROUPS = 4
FN_GDIM = 128
FN_WIDTH = 512
SSD_HEADDIM = 64
SSD_HEADS = 8
SSD_GROUPS = 2
SSD_HPG = 4
SSD_STATE = 128
SSD_WIDTH = 512
SSD_GN = 256
SSD_CONV_DIM = 1024
D_FF = 2816
CONV_W = 3
KV_OFF = 2 * GM_WIDTH + ATTN_WIDTH
SSM_OFF = FN_WIDTH + SSD_WIDTH
DN_ALPHA = (2 * DEPTH) ** 0.25
EPS = 1e-6

V7X_LANES = 128
V7X_SUBLANES = 8
V7X_VMEM_BYTES = 64 * 1024 * 1024
VMEM_CHIP_HEADROOM = 3 << 20
VMEM_TEMP_ALLOWANCE = 8 << 20
MOD_COLS = 2048
AB_SUB = 256
FF_CHUNK = 256
SSD_CHUNK = 256

BF16 = jnp.bfloat16
F32 = jnp.float32
HALO = V7X_SUBLANES


def _vmem_limit(nbytes):
    return int(min(nbytes * 3 // 2 + VMEM_TEMP_ALLOWANCE, V7X_VMEM_BYTES - VMEM_CHIP_HEADROOM))


def _resident(shape):
    nd = len(shape)
    return pl.BlockSpec(shape, lambda *_: (0,) * nd, pipeline_mode=pl.Buffered(1))


def _dot(a, b):
    return jnp.dot(a, b, preferred_element_type=F32)


def _dot_nt(a, b):
    return lax.dot_general(a, b, (((1,), (1,)), ((), ())), preferred_element_type=F32)


def _layer_norm(z, g, b, eps=EPS):
    mu = jnp.mean(z, axis=-1, keepdims=True)
    zc = z - mu
    var = jnp.mean(zc * zc, axis=-1, keepdims=True)
    return zc * lax.rsqrt(var + eps) * g + b


def _deepnorm(x, gate, y, g, b):
    return _layer_norm(x + (gate * (1.0 / DN_ALPHA)) * y, g, b, eps=EPS / DN_ALPHA ** 2)


def _rms_norm(z, g):
    return z * lax.rsqrt(jnp.mean(z * z, axis=-1, keepdims=True) + EPS) * g


def _softplus(z):
    return jnp.maximum(z, 0.0) + jnp.log1p(jnp.exp(-jnp.abs(z)))


def _gelu_tanh_of_half(h):
    c1 = math.sqrt(2.0 / math.pi)
    return h + h * jnp.tanh(h * (2.0 * c1 + (8.0 * c1 * 0.044715) * (h * h)))


def _silu(z):
    h = 0.5 * z
    return h + h * jnp.tanh(h)


def _conv3(u, w, b, rows):
    tiles, c = rows // V7X_SUBLANES, u.shape[1]
    u3 = u.reshape(tiles + 2, V7X_SUBLANES, c)
    sub = lax.broadcasted_iota(jnp.int32, (tiles, V7X_SUBLANES, c), 1)
    dn = pltpu.roll(u3, 1, 1)
    up = pltpu.roll(u3, V7X_SUBLANES - 1, 1)
    prev = jnp.where(sub == 0, dn[0:tiles], dn[1:tiles + 1])
    nxt = jnp.where(sub == V7X_SUBLANES - 1, up[2:tiles + 2], up[1:tiles + 1])
    return (b + w[0:1] * prev + w[1:2] * u3[1:tiles + 1] + w[2:3] * nxt).reshape(rows, c)


def _fill_halo_tile(hm_ref, x_ref, xp_ref, xn_ref, sh, sc, rows):
    t = pl.program_id(1)
    nt = pl.num_programs(1)
    keep_prev = jnp.where(t > 0, 1.0, 0.0)
    keep_next = jnp.where(t < nt - 1, 1.0, 0.0)
    hm_ref[HALO:HALO + rows, :] = (x_ref[0] * (1.0 + sc) + sh).astype(BF16)
    hm_ref[0:HALO, :] = ((xp_ref[0] * (1.0 + sc) + sh) * keep_prev).astype(BF16)
    hm_ref[HALO + rows:2 * HALO + rows, :] = ((xn_ref[0] * (1.0 + sc) + sh) * keep_next).astype(BF16)


def _halo_specs(rows, seq, d):
    nb = rows // HALO
    last = seq // HALO - 1
    return [
        pl.BlockSpec((1, rows, d), lambda b, t: (b, t, 0)),
        pl.BlockSpec((1, HALO, d), lambda b, t: (b, jnp.maximum(t * nb - 1, 0), 0)),
        pl.BlockSpec((1, HALO, d), lambda b, t: (b, jnp.minimum((t + 1) * nb, last), 0)),
    ]


def _row_spec(d):
    return pl.BlockSpec((1, 1, d), lambda b, t: (b, 0, 0))


def _mod_kernel(c_ref, w_ref, b_ref, o_ref):
    c = c_ref[...]
    act = c * jax.nn.sigmoid(c)
    o_ref[0] = jnp.dot(act, w_ref[0], preferred_element_type=F32,
                       precision=lax.Precision.HIGHEST) + b_ref[0]


def _modulation(cc, mod_w, mod_b):
    rows, d = cc.shape
    depth, _, n = mod_w.shape
    tn = MOD_COLS
    return pl.pallas_call(
        _mod_kernel,
        out_shape=jax.ShapeDtypeStruct((depth, rows, n), F32),
        grid=(depth, n // tn),
        in_specs=[
            pl.BlockSpec((rows, d), lambda i, j: (0, 0)),
            pl.BlockSpec((1, d, tn), lambda i, j: (i, 0, j)),
            pl.BlockSpec((1, 1, tn), lambda i, j: (i, 0, j)),
        ],
        out_specs=pl.BlockSpec((1, rows, tn), lambda i, j: (i, 0, j)),
        compiler_params=pltpu.CompilerParams(
            dimension_semantics=("arbitrary", "arbitrary"),
            vmem_limit_bytes=_vmem_limit(2 * d * tn * 4)),
        name="modulation",
    )(cc, mod_w, mod_b.reshape(depth, 1, n))


def _ab_in_kernel(*refs, use_rope, rows):
    (x_ref, sh_ref, sc_ref, w_ref, lng_ref, lnb_ref, ws_ref, bias_ref, qg_ref, kg_ref) = refs[:10]
    if use_rope:
        cos_ref, sin_ref = refs[10:12]
    a_ref, q_ref, k_ref, v_ref = refs[-4:]

    gd = GM_WIDTH // GM_GROUPS
    slab_rows = min(AB_SUB, rows)

    def slab(i, carry):
        r0 = pl.multiple_of(i * slab_rows, slab_rows)
        rs = pl.ds(r0, slab_rows)
        h = (x_ref[0, rs, :] * (1.0 + sc_ref[0]) + sh_ref[0]).astype(BF16)

        u = _gelu_tanh_of_half(_dot(h, w_ref[:, 0:GM_WIDTH]))
        vn = _layer_norm(_gelu_tanh_of_half(_dot(h, w_ref[:, GM_WIDTH:2 * GM_WIDTH])),
                         lng_ref[...], lnb_ref[...])
        vb = vn.astype(BF16)
        for c in range(slab_rows // CHUNK):
            r = slice(c * CHUNK, (c + 1) * CHUNK)
            for g in range(GM_GROUPS):
                l = slice(g * gd, (g + 1) * gd)
                mixed = _dot(ws_ref[g], vb[r, l]) + bias_ref[:, l]
                a_ref[0, pl.ds(r0 + c * CHUNK, CHUNK), l] = (u[r, l] * mixed).astype(BF16)

        def rope(z):
            if not use_rope:
                return z
            return z * cos_ref[rs, :] + pltpu.roll(z, HEAD_DIM // 2, 1) * sin_ref[rs, :]

        pq = _dot(h, w_ref[:, 2 * GM_WIDTH:KV_OFF])
        for hh in range(N_Q_HEADS):
            l = slice(hh * HEAD_DIM, (hh + 1) * HEAD_DIM)
            q_ref[0, rs, l] = rope(_rms_norm(pq[:, l], qg_ref[...])).astype(BF16)
        pk = _dot(h, w_ref[:, KV_OFF:KV_OFF + KV_W])
        for hh in range(N_KV_HEADS):
            l = slice(hh * HEAD_DIM, (hh + 1) * HEAD_DIM)
            k_ref[0, rs, l] = rope(_rms_norm(pk[:, l], kg_ref[...])).astype(BF16)
        v_ref[0, rs, :] = _dot(h, w_ref[:, KV_OFF + KV_W:KV_OFF + 2 * KV_W]).astype(BF16)
        return carry

    lax.fori_loop(0, rows // slab_rows, slab, 0)


def _ab_in(x, sh, sc, w, lng, lnb, ws, bias2d, qg, kg, rope_tabs, rows, kv_len, kv_row0, kv_into=None):
    bsz, seq, d = x.shape
    assert kv_row0 % rows == 0
    kv_blk0 = kv_row0 // rows
    n = w.shape[1]
    use_rope = rope_tabs is not None
    in_specs = [
        pl.BlockSpec((1, rows, d), lambda b, t: (b, t, 0)),
        _row_spec(d), _row_spec(d),
        _resident((d, n)),
        _resident((1, GM_WIDTH)), _resident((1, GM_WIDTH)),
        _resident((GM_GROUPS, CHUNK, CHUNK)),
        _resident((CHUNK, GM_WIDTH)),
        _resident((1, HEAD_DIM)), _resident((1, HEAD_DIM)),
    ]
    args = [x, sh, sc, w, lng, lnb, ws, bias2d, qg, kg]
    if use_rope:
        in_specs += [pl.BlockSpec((rows, HEAD_DIM), lambda b, t: (t, 0))] * 2
        args += list(rope_tabs)
    aliases = {}
    if kv_into is not None:
        aliases = {len(args): 2, len(args) + 1: 3}
        in_specs += [pl.BlockSpec(memory_space=pl.ANY)] * 2
        args += list(kv_into)
    out_shape = [
        jax.ShapeDtypeStruct((bsz, seq, GM_WIDTH), BF16),
        jax.ShapeDtypeStruct((bsz, seq, ATTN_WIDTH), BF16),
        jax.ShapeDtypeStruct((bsz, kv_len, KV_W), BF16),
        jax.ShapeDtypeStruct((bsz, kv_len, KV_W), BF16),
    ]
    out_specs = [
        pl.BlockSpec((1, rows, GM_WIDTH), lambda b, t: (b, t, 0)),
        pl.BlockSpec((1, rows, ATTN_WIDTH), lambda b, t: (b, t, 0)),
        pl.BlockSpec((1, rows, KV_W), lambda b, t: (b, t + kv_blk0, 0)),
        pl.BlockSpec((1, rows, KV_W), lambda b, t: (b, t + kv_blk0, 0)),
    ]
    est = d * n * 2 + 2 * rows * d * 4 + 6 * rows * n * 4
    return pl.pallas_call(
        functools.partial(_ab_in_kernel, use_rope=use_rope, rows=rows),
        out_shape=out_shape,
        grid=(bsz, seq // rows),
        in_specs=in_specs,
        out_specs=out_specs,
        input_output_aliases=aliases,
        compiler_params=pltpu.CompilerParams(
            dimension_semantics=("arbitrary", "arbitrary"),
            vmem_limit_bytes=_vmem_limit(est)),
        name="ab_in_rope" if use_rope else "ab_in_ctx",
    )(*args)


def _attn_kernel(q_ref, k_ref, v_ref, o_ref, vt_ref, *, rows, blk):
    @pl.when(pl.program_id(2) == 0)
    def _():
        vt_ref[...] = v_ref[0].astype(F32).T

    k = k_ref[0]
    vt = vt_ref[...].astype(BF16)
    for h in range(N_Q_HEADS // N_KV_HEADS):
        l_ = slice(h * HEAD_DIM, (h + 1) * HEAD_DIM)
        for r0 in range(0, rows, blk):
            st = _dot_nt(k, q_ref[0, r0:r0 + blk, l_])
            m = jnp.max(st, axis=0, keepdims=True)
            p = jnp.exp2(st - m)
            l = jnp.sum(p, axis=0, keepdims=True)
            ot = _dot(vt, p.astype(BF16)) / l
            o_ref[0, r0:r0 + blk, l_] = ot.T.astype(BF16)


def _attention(q, k, v, k_row0, sk, rows, blk=512):
    bsz, sq, _ = q.shape
    assert k_row0 % sk == 0
    kb = k_row0 // sk
    gw = (N_Q_HEADS // N_KV_HEADS) * HEAD_DIM
    est = 4 * blk * sk * 12 + 4 * sk * HEAD_DIM * 2 + 4 * rows * gw * 2 + sk * HEAD_DIM * 4
    return pl.pallas_call(
        functools.partial(_attn_kernel, rows=rows, blk=min(blk, rows)),
        out_shape=jax.ShapeDtypeStruct((bsz, sq, ATTN_WIDTH), BF16),
        grid=(bsz, N_KV_HEADS, sq // rows),
        in_specs=[
            pl.BlockSpec((1, rows, gw), lambda b, h, t: (b, t, h)),
            pl.BlockSpec((1, sk, HEAD_DIM), lambda b, h, t: (b, kb, h)),
            pl.BlockSpec((1, sk, HEAD_DIM), lambda b, h, t: (b, kb, h)),
        ],
        out_specs=pl.BlockSpec((1, rows, gw), lambda b, h, t: (b, t, h)),
        scratch_shapes=[pltpu.VMEM((HEAD_DIM, sk), F32)],
        compiler_params=pltpu.CompilerParams(
            dimension_semantics=("arbitrary", "arbitrary", "arbitrary"),
            vmem_limit_bytes=_vmem_limit(est)),
        name="attention",
    )(q, k, v)


def _out_ln_kernel(a1_ref, a2_ref, w_ref, x_ref, g_ref, lng_ref, lnb_ref, o_ref):
    half = a1_ref.shape[2]
    y = _dot(a1_ref[0], w_ref[0:half, :]) + _dot(a2_ref[0], w_ref[half:2 * half, :])
    o_ref[0] = _deepnorm(x_ref[0], g_ref[0], y, lng_ref[...], lnb_ref[...])


def _out_ln(a1, a2, w, x, gate, lng, lnb, rows):
    bsz, seq, d = x.shape
    half = a1.shape[2]
    est = 2 * half * d * 2 + 4 * rows * d * 4 + 4 * rows * half * 2 + 2 * rows * d * 4
    return pl.pallas_call(
        _out_ln_kernel,
        out_shape=jax.ShapeDtypeStruct((bsz, seq, d), F32),
        grid=(bsz, seq // rows),
        in_specs=[
            pl.BlockSpec((1, rows, half), lambda b, t: (b, t, 0)),
            pl.BlockSpec((1, rows, half), lambda b, t: (b, t, 0)),
            _resident((2 * half, d)),
            pl.BlockSpec((1, rows, d), lambda b, t: (b, t, 0)),
            _row_spec(d),
            _resident((1, d)), _resident((1, d)),
        ],
        out_specs=pl.BlockSpec((1, rows, d), lambda b, t: (b, t, 0)),
        compiler_params=pltpu.CompilerParams(
            dimension_semantics=("arbitrary", "arbitrary"),
            vmem_limit_bytes=_vmem_limit(est)),
        name="out_ln",
    )(a1, a2, w, x, gate, lng, lnb)


def _ffn_kernel(x_ref, xp_ref, xn_ref, sh_ref, sc_ref, g_ref, wu1_ref, wu2_ref, cw1_ref, cw2_ref,
                cb1_ref, cb2_ref, wd_ref, lng_ref, lnb_ref, o_ref, hm_ref, u_ref, gt_ref, *, rows):
    _fill_halo_tile(hm_ref, x_ref, xp_ref, xn_ref, sh_ref[0], sc_ref[0], rows)
    n_chunks = wu1_ref.shape[1] // FF_CHUNK

    def cols(ref, c):
        off = c * FF_CHUNK
        return ref[:, pl.ds(off if isinstance(off, int) else pl.multiple_of(off, FF_CHUNK), FF_CHUNK)]
    tiles = rows // V7X_SUBLANES
    sub = lax.broadcasted_iota(jnp.int32, (tiles, V7X_SUBLANES, FF_CHUNK), 1)

    def up_proj(c, slot):
        hm = hm_ref[...]
        u_ref[slot, 0] = _dot(hm, cols(wu1_ref, c)).reshape(tiles + 2, V7X_SUBLANES, FF_CHUNK)
        u_ref[slot, 1] = _dot(hm, cols(wu2_ref, c)).reshape(tiles + 2, V7X_SUBLANES, FF_CHUNK)

    def conv_gate(c, slot):
        def conv(half, cw_ref, cb_ref):
            u = u_ref[slot, half]
            dn = pltpu.roll(u, 1, 1)
            up = pltpu.roll(u, V7X_SUBLANES - 1, 1)
            prev = jnp.where(sub == 0, dn[0:tiles], dn[1:tiles + 1])
            nxt = jnp.where(sub == V7X_SUBLANES - 1, up[2:tiles + 2], up[1:tiles + 1])
            w = cols(cw_ref, c)
            return cols(cb_ref, c) + w[0:1] * prev + w[1:2] * u[1:tiles + 1] + w[2:3] * nxt

        a1 = conv(0, cw1_ref, cb1_ref)
        a2 = conv(1, cw2_ref, cb2_ref)
        gt_ref[c] = (a1 * (a2 + a2 * jnp.tanh(a2))).reshape(rows, FF_CHUNK)

    up_proj(0, 0)

    def pair(k, carry):
        c = 2 * k
        up_proj(c + 1, 1)
        conv_gate(c, 0)
        up_proj(c + 2, 0)
        conv_gate(c + 1, 1)
        return carry

    lax.fori_loop(0, (n_chunks - 1) // 2, pair, 0)
    conv_gate(n_chunks - 1, 0)

    f = _dot(gt_ref[0].astype(BF16), wd_ref[0])
    for c in range(1, n_chunks):
        f = f + _dot(gt_ref[c].astype(BF16), wd_ref[c])
    o_ref[0] = _deepnorm(x_ref[0], g_ref[0], f, lng_ref[...], lnb_ref[...])


def _ffn(x, sh, sc, gate, wu1, wu2, cw1, cw2, cb1, cb2, wd, lng, lnb, rows):
    bsz, seq, d = x.shape
    dff = wu1.shape[1]
    nc = dff // FF_CHUNK
    assert dff % FF_CHUNK == 0 and nc % 2 == 1, "the pipeline peels one chunk and pairs the rest"
    est = (3 * nc * d * FF_CHUNK * 2 + 4 * rows * d * 4 + (rows + 2 * HALO) * d * 2
           + nc * rows * FF_CHUNK * 4 + 8 * (rows + 2 * HALO) * FF_CHUNK * 4 + 2 * rows * d * 4)
    return pl.pallas_call(
        functools.partial(_ffn_kernel, rows=rows),
        out_shape=jax.ShapeDtypeStruct((bsz, seq, d), F32),
        grid=(bsz, seq // rows),
        in_specs=_halo_specs(rows, seq, d) + [
            _row_spec(d), _row_spec(d), _row_spec(d),
            _resident((d, dff)), _resident((d, dff)),
            _resident((CONV_W, dff)), _resident((CONV_W, dff)),
            _resident((1, dff)), _resident((1, dff)),
            _resident((nc, FF_CHUNK, d)),
            _resident((1, d)), _resident((1, d)),
        ],
        out_specs=pl.BlockSpec((1, rows, d), lambda b, t: (b, t, 0)),
        scratch_shapes=[
            pltpu.VMEM((rows + 2 * HALO, d), BF16),
            pltpu.VMEM((2, 2, rows // HALO + 2, HALO, FF_CHUNK), F32),
            pltpu.VMEM((nc, rows, FF_CHUNK), F32),
        ],
        compiler_params=pltpu.CompilerParams(
            dimension_semantics=("arbitrary", "arbitrary"),
            vmem_limit_bytes=_vmem_limit(est)),
        name="conv_ffn",
    )(x, x, x, sh, sc, gate, wu1, wu2, cw1, cw2, cb1, cb2, wd, lng, lnb)


def _cd_in_kernel(*refs, with_fz, rows):
    x_ref, xp_ref, xn_ref, sh_ref, sc_ref = refs[:5]
    if with_fz:
        wfz_ref, wx_ref, wdt_ref, cw_ref, cb_ref, dtb_ref = refs[5:11]
        f_ref, z_ref, xs_ref, b_ref, c_ref, dt_ref, hm_ref = refs[11:]
    else:
        wx_ref, wdt_ref, cw_ref, cb_ref, dtb_ref = refs[5:10]
        xs_ref, b_ref, c_ref, dt_ref, hm_ref = refs[10:]
    _fill_halo_tile(hm_ref, x_ref, xp_ref, xn_ref, sh_ref[0], sc_ref[0], rows)
    hc = hm_ref[HALO:HALO + rows, :]
    if with_fz:
        pfz = _dot(hc, wfz_ref[...])
        f_ref[0] = pfz[:, 0:FN_WIDTH].astype(BF16)
        z_ref[0] = pfz[:, FN_WIDTH:SSM_OFF]
    a = _conv3(_dot(hm_ref[...], wx_ref[...]), cw_ref[...], cb_ref[...], rows)
    s = a + a * jnp.tanh(a)
    xs_ref[0] = s[:, 0:SSD_WIDTH]
    b_ref[0] = s[:, SSD_WIDTH:SSD_WIDTH + SSD_GN].astype(BF16)
    c_ref[0] = s[:, SSD_WIDTH + SSD_GN:SSD_CONV_DIM].astype(BF16)
    dt_ref[0] = _softplus(_dot(hc, wdt_ref[...]) + dtb_ref[...])


def _cd_in(x, sh, sc, wfz, wx, wdt, cw, cb, dtb, rows):
    bsz, seq, d = x.shape
    with_fz = wfz is not None
    in_specs = _halo_specs(rows, seq, d) + [_row_spec(d), _row_spec(d)]
    args = [x, x, x, sh, sc]
    out_shape, out_specs = [], []
    if with_fz:
        in_specs.append(_resident((d, SSM_OFF)))
        args.append(wfz)
        out_shape += [jax.ShapeDtypeStruct((bsz, seq, FN_WIDTH), BF16),
                      jax.ShapeDtypeStruct((bsz, seq, SSD_WIDTH), F32)]
        out_specs += [pl.BlockSpec((1, rows, FN_WIDTH), lambda b, t: (b, t, 0)),
                      pl.BlockSpec((1, rows, SSD_WIDTH), lambda b, t: (b, t, 0))]
    in_specs += [_resident((d, SSD_CONV_DIM)), _resident((d, V7X_LANES)),
                 _resident((CONV_W, SSD_CONV_DIM)), _resident((1, SSD_CONV_DIM)),
                 _resident((1, V7X_LANES))]
    args += [wx, wdt, cw, cb, dtb]
    out_shape += [jax.ShapeDtypeStruct((bsz, seq, SSD_WIDTH), F32),
                  jax.ShapeDtypeStruct((bsz, seq, SSD_GN), BF16),
                  jax.ShapeDtypeStruct((bsz, seq, SSD_GN), BF16),
                  jax.ShapeDtypeStruct((bsz, seq, V7X_LANES), F32)]
    out_specs += [pl.BlockSpec((1, rows, SSD_WIDTH), lambda b, t: (b, t, 0)),
                  pl.BlockSpec((1, rows, SSD_GN), lambda b, t: (b, t, 0)),
                  pl.BlockSpec((1, rows, SSD_GN), lambda b, t: (b, t, 0)),
                  pl.BlockSpec((1, rows, V7X_LANES), lambda b, t: (b, t, 0))]
    est = d * 2200 * 2 + 2 * rows * d * 4 + 8 * (rows + 2 * HALO) * SSD_CONV_DIM * 4
    return pl.pallas_call(
        functools.partial(_cd_in_kernel, with_fz=with_fz, rows=rows),
        out_shape=out_shape,
        grid=(bsz, seq // rows),
        in_specs=in_specs,
        out_specs=out_specs,
        scratch_shapes=[pltpu.VMEM((rows + 2 * HALO, d), BF16)],
        compiler_params=pltpu.CompilerParams(
            dimension_semantics=("arbitrary", "arbitrary"),
            vmem_limit_bytes=_vmem_limit(est)),
        name="cd_in" if with_fz else "cd_in_ctx",
    )(*args)


def _cumsum_rows(z, reverse):
    n = z.shape[0]
    idx = lax.broadcasted_iota(jnp.int32, z.shape, 0)
    k = 1
    while k < n:
        if reverse:
            z = z + jnp.where(idx < n - k, pltpu.roll(z, n - k, 0), 0.0)
        else:
            z = z + jnp.where(idx >= k, pltpu.roll(z, k, 0), 0.0)
        k *= 2
    return z


def _lane_expand(row, lane0, width):
    lane = lax.broadcasted_iota(jnp.int32, (1, SSD_HPG * width), 1)
    out = jnp.zeros((1, SSD_HPG * width), F32)
    for j in range(SSD_HPG):
        val = jnp.broadcast_to(row[:, lane0 + j:lane0 + j + 1], (1, SSD_HPG * width))
        out = jnp.where(lane // width == j, val, out)
    return out


def _ssd_chunk(x, bm, cm, dt, a_row, h_ref, lane0, reverse, need_y):
    q = x.shape[0]
    gw = SSD_HPG * SSD_HEADDIM
    acum = _cumsum_rows(dt * a_row, reverse)
    total = acum[0:1, :] if reverse else acum[q - 1:q, :]
    acum_t = acum.T
    dt_t = dt.T
    total_t = jnp.broadcast_to(acum_t[:, 0:1] if reverse else acum_t[:, q - 1:q], acum_t.shape)
    wgt_t = dt_t * jnp.exp2(total_t - acum_t)
    lane_g = lax.broadcasted_iota(jnp.int32, (q, gw), 1) // SSD_HEADDIM
    lane_gh = lax.broadcasted_iota(jnp.int32, (SSD_STATE, gw), 1) // SSD_HEADDIM
    if need_y:
        srow_t = acum_t - jnp.log2(dt_t)
        ti = lax.broadcasted_iota(jnp.int32, (q, q), 0)
        si = lax.broadcasted_iota(jnp.int32, (q, q), 1)
        mask = (si >= ti) if reverse else (si <= ti)
    ys = []
    for g in range(SSD_GROUPS):
        bg = bm[:, g * SSD_STATE:(g + 1) * SSD_STATE]
        xgb = x[:, g * gw:(g + 1) * gw].astype(BF16)
        x_heads = [jnp.where(lane_g == j, xgb, jnp.zeros_like(xgb)) for j in range(SSD_HPG)]
        hg = h_ref[g]
        if need_y:
            cg = cm[:, g * SSD_STATE:(g + 1) * SSD_STATE]
            cb = _dot_nt(cg, bg)
            cgf = cg.astype(F32)
            hgb = hg.astype(BF16)
            lhs, rhs = [], []
            for j in range(SSD_HPG):
                jl = lane0 + g * SSD_HPG + j
                col = jnp.broadcast_to(acum[:, jl:jl + 1], (q, q))
                row = jnp.broadcast_to(srow_t[jl:jl + 1, :], (q, q))
                lhs.append((cb * jnp.exp2(jnp.where(mask, col - row, -jnp.inf))).astype(BF16))
                lhs.append((cgf * jnp.exp2(col[:, 0:SSD_STATE])).astype(BF16))
                rhs.append(x_heads[j])
                rhs.append(jnp.where(lane_gh == j, hgb, jnp.zeros_like(hgb)))
            ys.append(_dot(jnp.concatenate(lhs, axis=1), jnp.concatenate(rhs, axis=0)))
        bgt = bg.astype(F32).T
        wb = [(bgt * jnp.broadcast_to(wgt_t[lane0 + g * SSD_HPG + j:lane0 + g * SSD_HPG + j + 1, :],
                                      (SSD_STATE, q))).astype(BF16) for j in range(SSD_HPG)]
        st = _dot(jnp.concatenate(wb, axis=1), jnp.concatenate(x_heads, axis=0))
        h_ref[g] = hg * jnp.exp2(_lane_expand(total, lane0 + g * SSD_HPG, SSD_HEADDIM)) + st
    if need_y:
        return jnp.concatenate(ys, axis=1)
    return None


def _ssd_kernel(xc_ref, bc_ref, dtc_ref, xl_ref, bl_ref, cl_ref, dtl_ref, z_ref, alog_ref,
                dskip_ref, ng_ref, o_ref, y_ref, h_ref):
    a_row = -jnp.exp(alog_ref[...]) * math.log2(math.e)
    nc_ctx = xc_ref.shape[1] // SSD_CHUNK
    nc_lat = xl_ref.shape[1] // SSD_CHUNK

    def rows_of(c):
        return pl.ds(pl.multiple_of(c * SSD_CHUNK, SSD_CHUNK), SSD_CHUNK)

    def ctx_step(c, direction):
        r = rows_of(c)
        _ssd_chunk(xc_ref[0, r, :], bc_ref[0, r, :], None, dtc_ref[0, r, :], a_row,
                   h_ref.at[direction], direction * SSD_HEADS, direction == 1, False)

    def lat_step(c, direction):
        r = rows_of(c)
        y_ref[direction, r, :] = _ssd_chunk(
            xl_ref[0, r, :], bl_ref[0, r, :], cl_ref[0, r, :], dtl_ref[0, r, :], a_row,
            h_ref.at[direction], direction * SSD_HEADS, direction == 1, True)

    h_ref[...] = jnp.zeros_like(h_ref)

    def ctx_pair(i, carry):
        ctx_step(i, 0)
        ctx_step(nc_ctx - 1 - i, 1)
        return carry

    lax.fori_loop(0, nc_ctx, ctx_pair, 0)

    def lat_pair(i, carry):
        lat_step(i, 0)
        lat_step(nc_lat - 1 - i, 1)
        return carry

    lax.fori_loop(0, nc_lat, lat_pair, 0)

    def finish(c, carry):
        r = rows_of(c)
        zz = z_ref[0, r, :]
        yt = (y_ref[0, r, :] + y_ref[1, r, :] + xl_ref[0, r, :] * dskip_ref[...]) * _silu(zz)
        o_ref[0, r, :] = _rms_norm(yt, ng_ref[...]).astype(BF16)
        return carry

    lax.fori_loop(0, nc_lat, finish, 0)


def _ssd(xs_c, b_c, dt_c, xs_l, b_l, c_l, dt_l, z, alog_row, dskip_row, ng_row):
    bsz, sl, w = xs_l.shape
    sc = xs_c.shape[1]
    assert sl % SSD_CHUNK == 0 and sc % SSD_CHUNK == 0
    gw = SSD_HPG * SSD_HEADDIM

    def full(s, n):
        return pl.BlockSpec((1, s, n), lambda b: (b, 0, 0))

    est = (2 * (sl + sc) * (w * 4 + 2 * SSD_GN * 2 + V7X_LANES * 4) + 2 * sl * w * 4
           + 2 * sl * w * 2 + 2 * sl * w * 4)
    return pl.pallas_call(
        _ssd_kernel,
        out_shape=jax.ShapeDtypeStruct((bsz, sl, w), BF16),
        grid=(bsz,),
        in_specs=[
            full(sc, w), full(sc, SSD_GN), full(sc, V7X_LANES),
            full(sl, w), full(sl, SSD_GN), full(sl, SSD_GN), full(sl, V7X_LANES),
            full(sl, w),
            _resident((1, V7X_LANES)), _resident((1, w)), _resident((1, w)),
        ],
        out_specs=full(sl, w),
        scratch_shapes=[
            pltpu.VMEM((2, sl, w), F32),
            pltpu.VMEM((2, SSD_GROUPS, SSD_STATE, gw), F32),
        ],
        compiler_params=pltpu.CompilerParams(
            dimension_semantics=("arbitrary",),
            vmem_limit_bytes=_vmem_limit(est)),
        name="ssd_scan",
    )(xs_c, b_c, dt_c, xs_l, b_l, c_l, dt_l, z, alog_row, dskip_row, ng_row)


def _fourier_kernel(f_ref, cc_ref, sc_ref, w2_ref, o_ref, y_ref, *, rows):
    seq = f_ref.shape[1]
    for g in range(FN_GROUPS):
        l = slice(g * FN_GDIM, (g + 1) * FN_GDIM)
        fg = f_ref[0, :, l]
        y_ref[0:seq, l] = _dot(fg, cc_ref[...]).astype(BF16)
        y_ref[seq:2 * seq, l] = _dot(fg, sc_ref[...]).astype(BF16)
    norm = 1.0 / math.sqrt(seq * FN_GDIM)

    def tile(i, carry):
        r = pl.ds(pl.multiple_of(i * rows, rows), rows)
        o_ref[0, r, :] = (_dot(w2_ref[r, :], y_ref[...]) * norm).astype(BF16)
        return carry

    lax.fori_loop(0, seq // rows, tile, 0)


def _fourier(f, cc, sc, w2, rows):
    bsz, seq, w = f.shape
    est = seq * 2 * seq * 2 + 4 * seq * w * 2 + 2 * seq * w * 2 + 4 * rows * 2 * seq * 2
    return pl.pallas_call(
        functools.partial(_fourier_kernel, rows=rows),
        out_shape=jax.ShapeDtypeStruct((bsz, seq, w), BF16),
        grid=(bsz,),
        in_specs=[
            pl.BlockSpec((1, seq, w), lambda b: (b, 0, 0)),
            _resident((FN_GDIM, FN_GDIM)), _resident((FN_GDIM, FN_GDIM)),
            _resident((seq, 2 * seq)),
        ],
        out_specs=pl.BlockSpec((1, seq, w), lambda b: (b, 0, 0)),
        scratch_shapes=[pltpu.VMEM((2 * seq, w), BF16)],
        compiler_params=pltpu.CompilerParams(
            dimension_semantics=("arbitrary",),
            vmem_limit_bytes=_vmem_limit(est)),
        name="fourier_mix",
    )(f, cc, sc, w2)


def _rope_tables(seq):
    rows = seq // GRID_W
    row = jnp.repeat(jnp.arange(rows, dtype=F32), GRID_W)
    col = jnp.broadcast_to(jnp.arange(GRID_W, dtype=F32), (rows, GRID_W)).reshape(-1)
    n_freq = HEAD_DIM // 4
    inv = ROPE_THETA ** (-jnp.arange(n_freq, dtype=F32) / n_freq)
    ang = jnp.stack([row, col], axis=-1)[:, :, None] * inv
    cos, sin = jnp.cos(ang), jnp.sin(ang)
    cos_t = jnp.concatenate([cos[:, 0], cos[:, 1], cos[:, 0], cos[:, 1]], axis=-1)
    sin_t = jnp.concatenate([-sin[:, 0], -sin[:, 1], sin[:, 0], sin[:, 1]], axis=-1)
    return cos_t, sin_t


def _rope_head_order(m):
    lead = m.shape[:-1]
    nh = m.shape[-1] // HEAD_DIM
    m = m.reshape(lead + (nh, 2, 2, HEAD_DIM // 4))
    return jnp.swapaxes(m, -3, -2).reshape(lead + (nh * HEAD_DIM,))


def _dft_tables(seq):
    def tw(n):
        i = np.arange(n, dtype=np.int64)
        ang = ((i[:, None] * i[None, :]) % n).astype(np.float64) * (2.0 * math.pi / n)
        return np.cos(ang), np.sin(ang)

    cc, sc = tw(FN_GDIM)
    cl, sl = tw(seq)
    w2 = np.concatenate([cl, -sl], axis=1).astype(np.float32)
    return (jnp.asarray(cc.astype(np.float32), BF16), jnp.asarray(sc.astype(np.float32), BF16),
            jnp.asarray(w2, BF16))


def _ffn_params(w_up, conv_w, conv_b, w_down):
    d = w_up.shape[0]
    wu = w_up.astype(BF16)
    cb = conv_b.reshape(1, 2 * D_FF)
    return (wu[:, :D_FF], wu[:, D_FF:], conv_w[:, :D_FF], 0.5 * conv_w[:, D_FF:],
            cb[:, :D_FF], 0.5 * cb[:, D_FF:], w_down.astype(BF16).reshape(D_FF // FF_CHUNK, FF_CHUNK, d))


def kernel(x, c, ctx, c_ctx, mod_w, mod_b, ln1_g, ln1_b, ln2_g, ln2_b, ffn_up, ffn_conv_w, ffn_conv_b,
           ffn_down, ab_w_in, ab_w_out, gm_ln_g, gm_ln_b, gm_ws, gm_bs, q_norm_g, k_norm_g, cd_w_in,
           cd_w_out, ssd_conv_w, ssd_conv_b, ssd_dt_bias, ssd_a_log, ssd_d, ssd_norm_g):
    bsz, seq, d = x.shape
    ctx_len = ctx.shape[1]
    lat_rows = min(1024, seq)
    out_rows = min(2048, seq)
    dft_rows = min(2048, seq)
    ctx_rows = min(256, ctx_len)

    pad = (-(bsz + 1)) % V7X_SUBLANES
    cc = jnp.concatenate([c, c_ctx[None], jnp.zeros((pad, d), F32)], axis=0)
    mod = _modulation(cc, mod_w, mod_b)
    rope_tabs = _rope_tables(seq)

    def row(v):
        return v.reshape(1, -1)

    xl, xc = x, ctx
    for i in range(DEPTH):
        last = i == DEPTH - 1
        j = i // 2
        ml = mod[i, :bsz].reshape(bsz, 6, 1, d)
        mc = jnp.broadcast_to(mod[i, bsz].reshape(1, 6, 1, d), (bsz, 6, 1, d))
        sh1, sc1, g1, sh2, sc2, g2 = (ml[:, k] for k in range(6))
        csh1, csc1, cg1, csh2, csc2, cg2 = (mc[:, k] for k in range(6))
        ln1 = (row(ln1_g[i]), row(ln1_b[i]))
        ln2 = (row(ln2_g[i]), row(ln2_b[i]))
        ffn_p = _ffn_params(ffn_up[i], ffn_conv_w[i], ffn_conv_b[i], ffn_down[i])

        if i % 2 == 0:
            w_in = ab_w_in[j].astype(BF16)
            qk_lo, qk_hi = 2 * GM_WIDTH, KV_OFF + KV_W
            q_scale = HEAD_DIM ** -0.5 * math.log2(math.e)
            w_in = jnp.concatenate([0.5 * w_in[:, :qk_lo], _rope_head_order(w_in[:, qk_lo:qk_hi]),
                                    w_in[:, qk_hi:]], axis=1)
            w_out = ab_w_out[j].astype(BF16)
            gm = (row(gm_ln_g[j]), row(gm_ln_b[j]), gm_ws[j].astype(BF16),
                  jnp.repeat(gm_bs[j].T, GM_WIDTH // GM_GROUPS, axis=1),
                  row(_rope_head_order(q_norm_g[j]) * q_scale), row(_rope_head_order(k_norm_g[j])))
            kv_len = seq + ctx_len
            a_c, q_c, k_all, v_all = _ab_in(xc, csh1, csc1, w_in, *gm, None, ctx_rows, kv_len, seq)
            a_l, q_l, k_all, v_all = _ab_in(xl, sh1, sc1, w_in, *gm, rope_tabs, lat_rows, kv_len, 0,
                                            kv_into=(k_all, v_all))
            mix_l = (a_l, _attention(q_l, k_all, v_all, 0, kv_len, lat_rows))
            if not last:
                mix_c = (a_c, _attention(q_c, k_all, v_all, seq, ctx_len, ctx_rows))
        else:
            w_in = cd_w_in[j].astype(BF16)
            w_out = cd_w_out[j].astype(BF16)
            wfz = w_in[:, :SSM_OFF]
            wx = w_in[:, SSM_OFF:SSM_OFF + SSD_CONV_DIM]
            ndt = 2 * SSD_HEADS
            wdt = jnp.pad(w_in[:, SSM_OFF + SSD_CONV_DIM:], ((0, 0), (0, V7X_LANES - ndt)))
            dtb = jnp.pad(ssd_dt_bias[j].reshape(1, ndt), ((0, 0), (0, V7X_LANES - ndt)))
            alog = jnp.pad(ssd_a_log[j].reshape(1, ndt), ((0, 0), (0, V7X_LANES - ndt)))
            cw, cb = 0.5 * ssd_conv_w[j], 0.5 * row(ssd_conv_b[j])
            dskip = row(jnp.repeat(ssd_d[j], SSD_HEADDIM))
            f_l, z_l, xs_l, b_l, c_l, dt_l = _cd_in(xl, sh1, sc1, wfz, wx, wdt, cw, cb, dtb, lat_rows)
            if last:
                xs_c, b_c, _, dt_c = _cd_in(xc, csh1, csc1, None, wx, wdt, cw, cb, dtb, ctx_rows)
            else:
                raise NotImplementedError("odd layers with a context output are not part of this block")
            s_l = _ssd(xs_c, b_c, dt_c, xs_l, b_l, c_l, dt_l, z_l, alog, dskip, row(ssd_norm_g[j]))
            mix_l = (_fourier(f_l, *_dft_tables(seq), dft_rows), s_l)

        xl = _out_ln(*mix_l, w_out, xl, g1, *ln1, out_rows)
        xl = _ffn(xl, sh2, sc2, g2, *ffn_p, *ln2, lat_rows)
        if not last:
            xc = _out_ln(*mix_c, w_out, xc, cg1, *ln1, ctx_rows)
            xc = _ffn(xc, csh2, csc2, cg2, *ffn_p, *ln2, ctx_rows)
    return xl
```

```python
import functools
import math

import jax
import jax.numpy as jnp
import numpy as np
from jax import lax
from jax.experimental import pallas as pl
from jax.experimental.pallas import tpu as pltpu

D_MODEL = 1024
DEPTH = 2
GRID_W = 64
CHUNK = 128
GM_GROUPS = 4
GM_WIDTH = 512
HEAD_DIM = 128
N_Q_HEADS = 4
N_KV_HEADS = 2
ATTN_WIDTH = 512
KV_W = 256
ROPE_THETA = 10000.0
FN_GROUPS = 4
FN_GDIM = 128
FN_WIDTH = 512
SSD_HEADDIM = 64
SSD_HEADS = 8
SSD_GROUPS = 2
SSD_HPG = 4
SSD_STATE = 128
SSD_WIDTH = 512
SSD_GN = 256
SSD_CONV_DIM = 1024
D_FF = 2816
CONV_W = 3
KV_OFF = 2 * GM_WIDTH + ATTN_WIDTH
SSM_OFF = FN_WIDTH + SSD_WIDTH
DN_ALPHA = (2 * DEPTH) ** 0.25
EPS = 1e-6

V7X_LANES = 128
V7X_SUBLANES = 8
V7X_VMEM_BYTES = 64 * 1024 * 1024
VMEM_CHIP_HEADROOM = 3 << 20
VMEM_TEMP_ALLOWANCE = 8 << 20
MOD_COLS = 2048
AB_SUB = 256
FF_CHUNK = 256
SSD_CHUNK = 256

BF16 = jnp.bfloat16
F32 = jnp.float32
HALO = V7X_SUBLANES


def _vmem_limit(nbytes):
    return int(min(nbytes * 3 // 2 + VMEM_TEMP_ALLOWANCE, V7X_VMEM_BYTES - VMEM_CHIP_HEADROOM))


def _resident(shape):
    nd = len(shape)
    return pl.BlockSpec(shape, lambda *_: (0,) * nd, pipeline_mode=pl.Buffered(1))


def _dot(a, b):
    return jnp.dot(a, b, preferred_element_type=F32)


def _dot_nt(a, b):
    return lax.dot_general(a, b, (((1,), (1,)), ((), ())), preferred_element_type=F32)


def _layer_norm(z, g, b, eps=EPS):
    mu = jnp.mean(z, axis=-1, keepdims=True)
    zc = z - mu
    var = jnp.mean(zc * zc, axis=-1, keepdims=True)
    return zc * lax.rsqrt(var + eps) * g + b


def _deepnorm(x, gate, y, g, b):
    return _layer_norm(x + (gate * (1.0 / DN_ALPHA)) * y, g, b, eps=EPS / DN_ALPHA ** 2)


def _rms_norm(z, g):
    return z * lax.rsqrt(jnp.mean(z * z, axis=-1, keepdims=True) + EPS) * g


def _softplus(z):
    return jnp.maximum(z, 0.0) + jnp.log1p(jnp.exp(-jnp.abs(z)))


def _gelu_tanh_of_half(h):
    c1 = math.sqrt(2.0 / math.pi)
    return h + h * jnp.tanh(h * (2.0 * c1 + (8.0 * c1 * 0.044715) * (h * h)))


def _silu(z):
    h = 0.5 * z
    return h + h * jnp.tanh(h)


def _conv3(u, w, b, rows):
    tiles, c = rows // V7X_SUBLANES, u.shape[1]
    u3 = u.reshape(tiles + 2, V7X_SUBLANES, c)
    sub = lax.broadcasted_iota(jnp.int32, (tiles, V7X_SUBLANES, c), 1)
    dn = pltpu.roll(u3, 1, 1)
    up = pltpu.roll(u3, V7X_SUBLANES - 1, 1)
    prev = jnp.where(sub == 0, dn[0:tiles], dn[1:tiles + 1])
    nxt = jnp.where(sub == V7X_SUBLANES - 1, up[2:tiles + 2], up[1:tiles + 1])
    return (b + w[0:1] * prev + w[1:2] * u3[1:tiles + 1] + w[2:3] * nxt).reshape(rows, c)


def _fill_halo_tile(hm_ref, x_ref, xp_ref, xn_ref, sh, sc, rows):
    t = pl.program_id(1)
    nt = pl.num_programs(1)
    keep_prev = jnp.where(t > 0, 1.0, 0.0)
    keep_next = jnp.where(t < nt - 1, 1.0, 0.0)
    hm_ref[HALO:HALO + rows, :] = (x_ref[0] * (1.0 + sc) + sh).astype(BF16)
    hm_ref[0:HALO, :] = ((xp_ref[0] * (1.0 + sc) + sh) * keep_prev).astype(BF16)
    hm_ref[HALO + rows:2 * HALO + rows, :] = ((xn_ref[0] * (1.0 + sc) + sh) * keep_next).astype(BF16)


def _halo_specs(rows, seq, d):
    nb = rows // HALO
    last = seq // HALO - 1
    return [
        pl.BlockSpec((1, rows, d), lambda b, t: (b, t, 0)),
        pl.BlockSpec((1, HALO, d), lambda b, t: (b, jnp.maximum(t * nb - 1, 0), 0)),
        pl.BlockSpec((1, HALO, d), lambda b, t: (b, jnp.minimum((t + 1) * nb, last), 0)),
    ]


def _row_spec(d):
    return pl.BlockSpec((1, 1, d), lambda b, t: (b, 0, 0))


def _mod_kernel(c_ref, w_ref, b_ref, o_ref):
    c = c_ref[...]
    act = c * jax.nn.sigmoid(c)
    o_ref[0] = jnp.dot(act, w_ref[0], preferred_element_type=F32,
                       precision=lax.Precision.HIGHEST) + b_ref[0]


def _modulation(cc, mod_w, mod_b):
    rows, d = cc.shape
    depth, _, n = mod_w.shape
    tn = MOD_COLS
    return pl.pallas_call(
        _mod_kernel,
        out_shape=jax.ShapeDtypeStruct((depth, rows, n), F32),
        grid=(depth, n // tn),
        in_specs=[
            pl.BlockSpec((rows, d), lambda i, j: (0, 0)),
            pl.BlockSpec((1, d, tn), lambda i, j: (i, 0, j)),
            pl.BlockSpec((1, 1, tn), lambda i, j: (i, 0, j)),
        ],
        out_specs=pl.BlockSpec((1, rows, tn), lambda i, j: (i, 0, j)),
        compiler_params=pltpu.CompilerParams(
            dimension_semantics=("arbitrary", "arbitrary"),
            vmem_limit_bytes=_vmem_limit(2 * d * tn * 4)),
        name="modulation",
    )(cc, mod_w, mod_b.reshape(depth, 1, n))


def _ab_in_kernel(*refs, use_rope, rows):
    (x_ref, sh_ref, sc_ref, w_ref, lng_ref, lnb_ref, ws_ref, bias_ref, qg_ref, kg_ref) = refs[:10]
    if use_rope:
        cos_ref, sin_ref = refs[10:12]
    a_ref, q_ref, k_ref, v_ref = refs[-4:]

    gd = GM_WIDTH // GM_GROUPS
    slab_rows = min(AB_SUB, rows)

    def slab(i, carry):
        r0 = pl.multiple_of(i * slab_rows, slab_rows)
        rs = pl.ds(r0, slab_rows)
        h = (x_ref[0, rs, :] * (1.0 + sc_ref[0]) + sh_ref[0]).astype(BF16)

        u = _gelu_tanh_of_half(_dot(h, w_ref[:, 0:GM_WIDTH]))
        vn = _layer_norm(_gelu_tanh_of_half(_dot(h, w_ref[:, GM_WIDTH:2 * GM_WIDTH])),
                         lng_ref[...], lnb_ref[...])
        vb = vn.astype(BF16)
        for c in range(slab_rows // CHUNK):
            r = slice(c * CHUNK, (c + 1) * CHUNK)
            for g in range(GM_GROUPS):
                l = slice(g * gd, (g + 1) * gd)
                mixed = _dot(ws_ref[g], vb[r, l]) + bias_ref[:, l]
                a_ref[0, pl.ds(r0 + c * CHUNK, CHUNK), l] = (u[r, l] * mixed).astype(BF16)

        def rope(z):
            if not use_rope:
                return z
            return z * cos_ref[rs, :] + pltpu.roll(z, HEAD_DIM // 2, 1) * sin_ref[rs, :]

        pq = _dot(h, w_ref[:, 2 * GM_WIDTH:KV_OFF])
        for hh in range(N_Q_HEADS):
            l = slice(hh * HEAD_DIM, (hh + 1) * HEAD_DIM)
            q_ref[0, rs, l] = rope(_rms_norm(pq[:, l], qg_ref[...])).astype(BF16)
        pk = _dot(h, w_ref[:, KV_OFF:KV_OFF + KV_W])
        for hh in range(N_KV_HEADS):
            l = slice(hh * HEAD_DIM, (hh + 1) * HEAD_DIM)
            k_ref[0, rs, l] = rope(_rms_norm(pk[:, l], kg_ref[...])).astype(BF16)
        v_ref[0, rs, :] = _dot(h, w_ref[:, KV_OFF + KV_W:KV_OFF + 2 * KV_W]).astype(BF16)
        return carry

    lax.fori_loop(0, rows // slab_rows, slab, 0)


def _ab_in(x, sh, sc, w, lng, lnb, ws, bias2d, qg, kg, rope_tabs, rows, kv_len, kv_row0, kv_into=None):
    bsz, seq, d = x.shape
    assert kv_row0 % rows == 0
    kv_blk0 = kv_row0 // rows
    n = w.shape[1]
    use_rope = rope_tabs is not None
    in_specs = [
        pl.BlockSpec((1, rows, d), lambda b, t: (b, t, 0)),
        _row_spec(d), _row_spec(d),
        _resident((d, n)),
        _resident((1, GM_WIDTH)), _resident((1, GM_WIDTH)),
        _resident((GM_GROUPS, CHUNK, CHUNK)),
        _resident((CHUNK, GM_WIDTH)),
        _resident((1, HEAD_DIM)), _resident((1, HEAD_DIM)),
    ]
    args = [x, sh, sc, w, lng, lnb, ws, bias2d, qg, kg]
    if use_rope:
        in_specs += [pl.BlockSpec((rows, HEAD_DIM), lambda b, t: (t, 0))] * 2
        args += list(rope_tabs)
    aliases = {}
    if kv_into is not None:
        aliases = {len(args): 2, len(args) + 1: 3}
        in_specs += [pl.BlockSpec(memory_space=pl.ANY)] * 2
        args += list(kv_into)
    out_shape = [
        jax.ShapeDtypeStruct((bsz, seq, GM_WIDTH), BF16),
        jax.ShapeDtypeStruct((bsz, seq, ATTN_WIDTH), BF16),
        jax.ShapeDtypeStruct((bsz, kv_len, KV_W), BF16),
        jax.ShapeDtypeStruct((bsz, kv_len, KV_W), BF16),
    ]
    out_specs = [
        pl.BlockSpec((1, rows, GM_WIDTH), lambda b, t: (b, t, 0)),
        pl.BlockSpec((1, rows, ATTN_WIDTH), lambda b, t: (b, t, 0)),
        pl.BlockSpec((1, rows, KV_W), lambda b, t: (b, t + kv_blk0, 0)),
        pl.BlockSpec((1, rows, KV_W), lambda b, t: (b, t + kv_blk0, 0)),
    ]
    est = d * n * 2 + 2 * rows * d * 4 + 6 * rows * n * 4
    return pl.pallas_call(
        functools.partial(_ab_in_kernel, use_rope=use_rope, rows=rows),
        out_shape=out_shape,
        grid=(bsz, seq // rows),
        in_specs=in_specs,
        out_specs=out_specs,
        input_output_aliases=aliases,
        compiler_params=pltpu.CompilerParams(
            dimension_semantics=("arbitrary", "arbitrary"),
            vmem_limit_bytes=_vmem_limit(est)),
        name="ab_in_rope" if use_rope else "ab_in_ctx",
    )(*args)


def _attn_kernel(q_ref, k_ref, v_ref, o_ref, vt_ref, *, rows, blk):
    @pl.when(pl.program_id(2) == 0)
    def _():
        vt_ref[...] = v_ref[0].astype(F32).T

    k = k_ref[0]
    vt = vt_ref[...].astype(BF16)
    for h in range(N_Q_HEADS // N_KV_HEADS):
        l_ = slice(h * HEAD_DIM, (h + 1) * HEAD_DIM)
        for r0 in range(0, rows, blk):
            st = _dot_nt(k, q_ref[0, r0:r0 + blk, l_])
            m = jnp.max(st, axis=0, keepdims=True)
            p = jnp.exp2(st - m)
            l = jnp.sum(p, axis=0, keepdims=True)
            ot = _dot(vt, p.astype(BF16)) / l
            o_ref[0, r0:r0 + blk, l_] = ot.T.astype(BF16)


def _attention(q, k, v, k_row0, sk, rows, blk=512):
    bsz, sq, _ = q.shape
    assert k_row0 % sk == 0
    kb = k_row0 // sk
    gw = (N_Q_HEADS // N_KV_HEADS) * HEAD_DIM
    est = 4 * blk * sk * 12 + 4 * sk * HEAD_DIM * 2 + 4 * rows * gw * 2 + sk * HEAD_DIM * 4
    return pl.pallas_call(
        functools.partial(_attn_kernel, rows=rows, blk=min(blk, rows)),
        out_shape=jax.ShapeDtypeStruct((bsz, sq, ATTN_WIDTH), BF16),
        grid=(bsz, N_KV_HEADS, sq // rows),
        in_specs=[
            pl.BlockSpec((1, rows, gw), lambda b, h, t: (b, t, h)),
            pl.BlockSpec((1, sk, HEAD_DIM), lambda b, h, t: (b, kb, h)),
            pl.BlockSpec((1, sk, HEAD_DIM), lambda b, h, t: (b, kb, h)),
        ],
        out_specs=pl.BlockSpec((1, rows, gw), lambda b, h, t: (b, t, h)),
        scratch_shapes=[pltpu.VMEM((HEAD_DIM, sk), F32)],
        compiler_params=pltpu.CompilerParams(
            dimension_semantics=("arbitrary", "arbitrary", "arbitrary"),
            vmem_limit_bytes=_vmem_limit(est)),
        name="attention",
    )(q, k, v)


def _out_ln_kernel(a1_ref, a2_ref, w_ref, x_ref, g_ref, lng_ref, lnb_ref, o_ref):
    half = a1_ref.shape[2]
    y = _dot(a1_ref[0], w_ref[0:half, :]) + _dot(a2_ref[0], w_ref[half:2 * half, :])
    o_ref[0] = _deepnorm(x_ref[0], g_ref[0], y, lng_ref[...], lnb_ref[...])


def _out_ln(a1, a2, w, x, gate, lng, lnb, rows):
    bsz, seq, d = x.shape
    half = a1.shape[2]
    est = 2 * half * d * 2 + 4 * rows * d * 4 + 4 * rows * half * 2 + 2 * rows * d * 4
    return pl.pallas_call(
        _out_ln_kernel,
        out_shape=jax.ShapeDtypeStruct((bsz, seq, d), F32),
        grid=(bsz, seq // rows),
        in_specs=[
            pl.BlockSpec((1, rows, half), lambda b, t: (b, t, 0)),
            pl.BlockSpec((1, rows, half), lambda b, t: (b, t, 0)),
            _resident((2 * half, d)),
            pl.BlockSpec((1, rows, d), lambda b, t: (b, t, 0)),
            _row_spec(d),
            _resident((1, d)), _resident((1, d)),
        ],
        out_specs=pl.BlockSpec((1, rows, d), lambda b, t: (b, t, 0)),
        compiler_params=pltpu.CompilerParams(
            dimension_semantics=("arbitrary", "arbitrary"),
            vmem_limit_bytes=_vmem_limit(est)),
        name="out_ln",
    )(a1, a2, w, x, gate, lng, lnb)


def _ffn_kernel(x_ref, xp_ref, xn_ref, sh_ref, sc_ref, g_ref, wu_ref, cw_ref, cb_ref, wd_ref, lng_ref, lnb_ref,
                o_ref, hm_ref, u_ref, gt_ref, *, rows):
    _fill_halo_tile(hm_ref, x_ref, xp_ref, xn_ref, sh_ref[0], sc_ref[0], rows)
    n_chunks = wd_ref.shape[0]

    def cols(ref, c):
        off = c * FF_CHUNK
        return ref[:, pl.ds(off if isinstance(off, int) else pl.multiple_of(off, FF_CHUNK), FF_CHUNK)]
    tiles = rows // V7X_SUBLANES
    sub = lax.broadcasted_iota(jnp.int32, (tiles, V7X_SUBLANES, FF_CHUNK), 1)

    def up_proj(c, slot):
        hm = hm_ref[...]
        u_ref[slot, 0] = _dot(hm, cols(wu_ref, c)).reshape(tiles + 2, V7X_SUBLANES, FF_CHUNK)
        u_ref[slot, 1] = _dot(hm, cols(wu_ref, c + n_chunks)).reshape(tiles + 2, V7X_SUBLANES, FF_CHUNK)

    def conv_gate(c, slot):
        def conv(half):
            u = u_ref[slot, half]
            dn = pltpu.roll(u, 1, 1)
            up = pltpu.roll(u, V7X_SUBLANES - 1, 1)
            prev = jnp.where(sub == 0, dn[0:tiles], dn[1:tiles + 1])
            nxt = jnp.where(sub == V7X_SUBLANES - 1, up[2:tiles + 2], up[1:tiles + 1])
            w = cols(cw_ref, c + half * n_chunks)
            return cols(cb_ref, c + half * n_chunks) + w[0:1] * prev + w[1:2] * u[1:tiles + 1] + w[2:3] * nxt

        a1 = conv(0)
        a2 = conv(1)
        gt_ref[c] = (a1 * (a2 + a2 * jnp.tanh(a2))).reshape(rows, FF_CHUNK)

    up_proj(0, 0)

    def pair(k, carry):
        c = 2 * k
        up_proj(c + 1, 1)
        conv_gate(c, 0)
        up_proj(c + 2, 0)
        conv_gate(c + 1, 1)
        return carry

    lax.fori_loop(0, (n_chunks - 1) // 2, pair, 0)
    conv_gate(n_chunks - 1, 0)

    f = _dot(gt_ref[0].astype(BF16), wd_ref[0])
    for c in range(1, n_chunks):
        f = f + _dot(gt_ref[c].astype(BF16), wd_ref[c])
    o_ref[0] = _deepnorm(x_ref[0], g_ref[0], f, lng_ref[...], lnb_ref[...])


def _ffn(x, sh, sc, gate, wu, cw, cb, wd, lng, lnb, rows):
    bsz, seq, d = x.shape
    nc = wd.shape[0]
    dff = nc * FF_CHUNK
    assert wu.shape[1] == 2 * dff and nc % 2 == 1, "the pipeline peels one chunk and pairs the rest"
    est = (3 * nc * d * FF_CHUNK * 2 + 4 * rows * d * 4 + (rows + 2 * HALO) * d * 2
           + nc * rows * FF_CHUNK * 4 + 8 * (rows + 2 * HALO) * FF_CHUNK * 4 + 2 * rows * d * 4)
    return pl.pallas_call(
        functools.partial(_ffn_kernel, rows=rows),
        out_shape=jax.ShapeDtypeStruct((bsz, seq, d), F32),
        grid=(bsz, seq // rows),
        in_specs=_halo_specs(rows, seq, d) + [
            _row_spec(d), _row_spec(d), _row_spec(d),
            _resident((d, 2 * dff)), _resident((CONV_W, 2 * dff)), _resident((1, 2 * dff)),
            _resident((nc, FF_CHUNK, d)),
            _resident((1, d)), _resident((1, d)),
        ],
        out_specs=pl.BlockSpec((1, rows, d), lambda b, t: (b, t, 0)),
        scratch_shapes=[
            pltpu.VMEM((rows + 2 * HALO, d), BF16),
            pltpu.VMEM((2, 2, rows // HALO + 2, HALO, FF_CHUNK), F32),
            pltpu.VMEM((nc, rows, FF_CHUNK), F32),
        ],
        compiler_params=pltpu.CompilerParams(
            dimension_semantics=("arbitrary", "arbitrary"),
            vmem_limit_bytes=_vmem_limit(est)),
        name="conv_ffn",
    )(x, x, x, sh, sc, gate, wu, cw, cb, wd, lng, lnb)


def _cd_in_kernel(*refs, with_fz, rows):
    x_ref, xp_ref, xn_ref, sh_ref, sc_ref = refs[:5]
    if with_fz:
        wfz_ref, wx_ref, wdt_ref, cw_ref, cb_ref, dtb_ref = refs[5:11]
        f_ref, z_ref, xs_ref, b_ref, c_ref, dt_ref, hm_ref = refs[11:]
    else:
        wx_ref, wdt_ref, cw_ref, cb_ref, dtb_ref = refs[5:10]
        xs_ref, b_ref, c_ref, dt_ref, hm_ref = refs[10:]
    _fill_halo_tile(hm_ref, x_ref, xp_ref, xn_ref, sh_ref[0], sc_ref[0], rows)
    hc = hm_ref[HALO:HALO + rows, :]
    if with_fz:
        pfz = _dot(hc, wfz_ref[...])
        f_ref[0] = pfz[:, 0:FN_WIDTH].astype(BF16)
        z_ref[0] = pfz[:, FN_WIDTH:SSM_OFF]
    a = _conv3(_dot(hm_ref[...], wx_ref[...]), cw_ref[...], cb_ref[...], rows)
    s = a + a * jnp.tanh(a)
    xs_ref[0] = s[:, 0:SSD_WIDTH]
    b_ref[0] = s[:, SSD_WIDTH:SSD_WIDTH + SSD_GN].astype(BF16)
    c_ref[0] = s[:, SSD_WIDTH + SSD_GN:SSD_CONV_DIM].astype(BF16)
    dt_ref[0] = _softplus(_dot(hc, wdt_ref[...]) + dtb_ref[...])


def _cd_in(x, sh, sc, wfz, wx, wdt, cw, cb, dtb, rows):
    bsz, seq, d = x.shape
    with_fz = wfz is not None
    in_specs = _halo_specs(rows, seq, d) + [_row_spec(d), _row_spec(d)]
    args = [x, x, x, sh, sc]
    out_shape, out_specs = [], []
    if with_fz:
        in_specs.append(_resident((d, SSM_OFF)))
        args.append(wfz)
        out_shape += [jax.ShapeDtypeStruct((bsz, seq, FN_WIDTH), BF16),
                      jax.ShapeDtypeStruct((bsz, seq, SSD_WIDTH), F32)]
        out_specs += [pl.BlockSpec((1, rows, FN_WIDTH), lambda b, t: (b, t, 0)),
                      pl.BlockSpec((1, rows, SSD_WIDTH), lambda b, t: (b, t, 0))]
    in_specs += [_resident((d, SSD_CONV_DIM)), _resident((d, V7X_LANES)),
                 _resident((CONV_W, SSD_CONV_DIM)), _resident((1, SSD_CONV_DIM)),
                 _resident((1, V7X_LANES))]
    args += [wx, wdt, cw, cb, dtb]
    out_shape += [jax.ShapeDtypeStruct((bsz, seq, SSD_WIDTH), F32),
                  jax.ShapeDtypeStruct((bsz, seq, SSD_GN), BF16),
                  jax.ShapeDtypeStruct((bsz, seq, SSD_GN), BF16),
                  jax.ShapeDtypeStruct((bsz, seq, V7X_LANES), F32)]
    out_specs += [pl.BlockSpec((1, rows, SSD_WIDTH), lambda b, t: (b, t, 0)),
                  pl.BlockSpec((1, rows, SSD_GN), lambda b, t: (b, t, 0)),
                  pl.BlockSpec((1, rows, SSD_GN), lambda b, t: (b, t, 0)),
                  pl.BlockSpec((1, rows, V7X_LANES), lambda b, t: (b, t, 0))]
    est = d * 2200 * 2 + 2 * rows * d * 4 + 8 * (rows + 2 * HALO) * SSD_CONV_DIM * 4
    return pl.pallas_call(
        functools.partial(_cd_in_kernel, with_fz=with_fz, rows=rows),
        out_shape=out_shape,
        grid=(bsz, seq // rows),
        in_specs=in_specs,
        out_specs=out_specs,
        scratch_shapes=[pltpu.VMEM((rows + 2 * HALO, d), BF16)],
        compiler_params=pltpu.CompilerParams(
            dimension_semantics=("arbitrary", "arbitrary"),
            vmem_limit_bytes=_vmem_limit(est)),
        name="cd_in" if with_fz else "cd_in_ctx",
    )(*args)


def _cumsum_rows(z, reverse):
    n = z.shape[0]
    idx = lax.broadcasted_iota(jnp.int32, z.shape, 0)
    k = 1
    while k < n:
        if reverse:
            z = z + jnp.where(idx < n - k, pltpu.roll(z, n - k, 0), 0.0)
        else:
            z = z + jnp.where(idx >= k, pltpu.roll(z, k, 0), 0.0)
        k *= 2
    return z


def _lane_expand(row, lane0, width):
    lane = lax.broadcasted_iota(jnp.int32, (1, SSD_HPG * width), 1)
    out = jnp.zeros((1, SSD_HPG * width), F32)
    for j in range(SSD_HPG):
        val = jnp.broadcast_to(row[:, lane0 + j:lane0 + j + 1], (1, SSD_HPG * width))
        out = jnp.where(lane // width == j, val, out)
    return out


def _ssd_chunk(x, bm, cm, dt, a_row, h_ref, lane0, reverse, need_y):
    q = x.shape[0]
    gw = SSD_HPG * SSD_HEADDIM
    acum = _cumsum_rows(dt * a_row, reverse)
    total = acum[0:1, :] if reverse else acum[q - 1:q, :]
    acum_t = acum.T
    dt_t = dt.T
    total_t = jnp.broadcast_to(acum_t[:, 0:1] if reverse else acum_t[:, q - 1:q], acum_t.shape)
    wgt_t = dt_t * jnp.exp2(total_t - acum_t)
    lane_g = lax.broadcasted_iota(jnp.int32, (q, gw), 1) // SSD_HEADDIM
    lane_gh = lax.broadcasted_iota(jnp.int32, (SSD_STATE, gw), 1) // SSD_HEADDIM
    if need_y:
        srow_t = acum_t - jnp.log2(dt_t)
        ti = lax.broadcasted_iota(jnp.int32, (q, q), 0)
        si = lax.broadcasted_iota(jnp.int32, (q, q), 1)
        mask = (si >= ti) if reverse else (si <= ti)
    ys = []
    for g in range(SSD_GROUPS):
        bg = bm[:, g * SSD_STATE:(g + 1) * SSD_STATE]
        xgb = x[:, g * gw:(g + 1) * gw].astype(BF16)
        x_heads = [jnp.where(lane_g == j, xgb, jnp.zeros_like(xgb)) for j in range(SSD_HPG)]
        hg = h_ref[g]
        if need_y:
            cg = cm[:, g * SSD_STATE:(g + 1) * SSD_STATE]
            cb = _dot_nt(cg, bg)
            cgf = cg.astype(F32)
            hgb = hg.astype(BF16)
            lhs, rhs = [], []
            for j in range(SSD_HPG):
                jl = lane0 + g * SSD_HPG + j
                col = jnp.broadcast_to(acum[:, jl:jl + 1], (q, q))
                row = jnp.broadcast_to(srow_t[jl:jl + 1, :], (q, q))
                lhs.append((cb * jnp.exp2(jnp.where(mask, col - row, -jnp.inf))).astype(BF16))
                lhs.append((cgf * jnp.exp2(col[:, 0:SSD_STATE])).astype(BF16))
                rhs.append(x_heads[j])
                rhs.append(jnp.where(lane_gh == j, hgb, jnp.zeros_like(hgb)))
            ys.append(_dot(jnp.concatenate(lhs, axis=1), jnp.concatenate(rhs, axis=0)))
        bgt = bg.astype(F32).T
        wb = [(bgt * jnp.broadcast_to(wgt_t[lane0 + g * SSD_HPG + j:lane0 + g * SSD_HPG + j + 1, :],
                                      (SSD_STATE, q))).astype(BF16) for j in range(SSD_HPG)]
        st = _dot(jnp.concatenate(wb, axis=1), jnp.concatenate(x_heads, axis=0))
        h_ref[g] = hg * jnp.exp2(_lane_expand(total, lane0 + g * SSD_HPG, SSD_HEADDIM)) + st
    if need_y:
        return jnp.concatenate(ys, axis=1)
    return None


def _ssd_kernel(xc_ref, bc_ref, dtc_ref, xl_ref, bl_ref, cl_ref, dtl_ref, z_ref, alog_ref,
                dskip_ref, ng_ref, o_ref, y_ref, h_ref):
    a_row = -jnp.exp(alog_ref[...]) * math.log2(math.e)
    nc_ctx = xc_ref.shape[1] // SSD_CHUNK
    nc_lat = xl_ref.shape[1] // SSD_CHUNK

    def rows_of(c):
        return pl.ds(pl.multiple_of(c * SSD_CHUNK, SSD_CHUNK), SSD_CHUNK)

    def ctx_step(c, direction):
        r = rows_of(c)
        _ssd_chunk(xc_ref[0, r, :], bc_ref[0, r, :], None, dtc_ref[0, r, :], a_row,
                   h_ref.at[direction], direction * SSD_HEADS, direction == 1, False)

    def lat_step(c, direction):
        r = rows_of(c)
        y_ref[direction, r, :] = _ssd_chunk(
            xl_ref[0, r, :], bl_ref[0, r, :], cl_ref[0, r, :], dtl_ref[0, r, :], a_row,
            h_ref.at[direction], direction * SSD_HEADS, direction == 1, True)

    h_ref[...] = jnp.zeros_like(h_ref)

    def ctx_pair(i, carry):
        ctx_step(i, 0)
        ctx_step(nc_ctx - 1 - i, 1)
        return carry

    lax.fori_loop(0, nc_ctx, ctx_pair, 0)

    def lat_pair(i, carry):
        lat_step(i, 0)
        lat_step(nc_lat - 1 - i, 1)
        return carry

    lax.fori_loop(0, nc_lat, lat_pair, 0)

    def finish(c, carry):
        r = rows_of(c)
        zz = z_ref[0, r, :]
        yt = (y_ref[0, r, :] + y_ref[1, r, :] + xl_ref[0, r, :] * dskip_ref[...]) * _silu(zz)
        o_ref[0, r, :] = _rms_norm(yt, ng_ref[...]).astype(BF16)
        return carry

    lax.fori_loop(0, nc_lat, finish, 0)


def _ssd(xs_c, b_c, dt_c, xs_l, b_l, c_l, dt_l, z, alog_row, dskip_row, ng_row):
    bsz, sl, w = xs_l.shape
    sc = xs_c.shape[1]
    assert sl % SSD_CHUNK == 0 and sc % SSD_CHUNK == 0
    gw = SSD_HPG * SSD_HEADDIM

    def full(s, n):
        return pl.BlockSpec((1, s, n), lambda b: (b, 0, 0))

    est = (2 * (sl + sc) * (w * 4 + 2 * SSD_GN * 2 + V7X_LANES * 4) + 2 * sl * w * 4
           + 2 * sl * w * 2 + 2 * sl * w * 4)
    return pl.pallas_call(
        _ssd_kernel,
        out_shape=jax.ShapeDtypeStruct((bsz, sl, w), BF16),
        grid=(bsz,),
        in_specs=[
            full(sc, w), full(sc, SSD_GN), full(sc, V7X_LANES),
            full(sl, w), full(sl, SSD_GN), full(sl, SSD_GN), full(sl, V7X_LANES),
            full(sl, w),
            _resident((1, V7X_LANES)), _resident((1, w)), _resident((1, w)),
        ],
        out_specs=full(sl, w),
        scratch_shapes=[
            pltpu.VMEM((2, sl, w), F32),
            pltpu.VMEM((2, SSD_GROUPS, SSD_STATE, gw), F32),
        ],
        compiler_params=pltpu.CompilerParams(
            dimension_semantics=("arbitrary",),
            vmem_limit_bytes=_vmem_limit(est)),
        name="ssd_scan",
    )(xs_c, b_c, dt_c, xs_l, b_l, c_l, dt_l, z, alog_row, dskip_row, ng_row)


def _fourier_kernel(f_ref, cc_ref, sc_ref, w2_ref, o_ref, y_ref, *, rows):
    seq = f_ref.shape[1]
    for g in range(FN_GROUPS):
        l = slice(g * FN_GDIM, (g + 1) * FN_GDIM)
        fg = f_ref[0, :, l]
        y_ref[0:seq, l] = _dot(fg, cc_ref[...]).astype(BF16)
        y_ref[seq:2 * seq, l] = _dot(fg, sc_ref[...]).astype(BF16)
    norm = 1.0 / math.sqrt(seq * FN_GDIM)

    def tile(i, carry):
        r = pl.ds(pl.multiple_of(i * rows, rows), rows)
        o_ref[0, r, :] = (_dot(w2_ref[r, :], y_ref[...]) * norm).astype(BF16)
        return carry

    lax.fori_loop(0, seq // rows, tile, 0)


def _fourier(f, cc, sc, w2, rows):
    bsz, seq, w = f.shape
    est = seq * 2 * seq * 2 + 4 * seq * w * 2 + 2 * seq * w * 2 + 4 * rows * 2 * seq * 2
    return pl.pallas_call(
        functools.partial(_fourier_kernel, rows=rows),
        out_shape=jax.ShapeDtypeStruct((bsz, seq, w), BF16),
        grid=(bsz,),
        in_specs=[
            pl.BlockSpec((1, seq, w), lambda b: (b, 0, 0)),
            _resident((FN_GDIM, FN_GDIM)), _resident((FN_GDIM, FN_GDIM)),
            _resident((seq, 2 * seq)),
        ],
        out_specs=pl.BlockSpec((1, seq, w), lambda b: (b, 0, 0)),
        scratch_shapes=[pltpu.VMEM((2 * seq, w), BF16)],
        compiler_params=pltpu.CompilerParams(
            dimension_semantics=("arbitrary",),
            vmem_limit_bytes=_vmem_limit(est)),
        name="fourier_mix",
    )(f, cc, sc, w2)


def _rope_tables(seq):
    rows = seq // GRID_W
    row = jnp.repeat(jnp.arange(rows, dtype=F32), GRID_W)
    col = jnp.broadcast_to(jnp.arange(GRID_W, dtype=F32), (rows, GRID_W)).reshape(-1)
    n_freq = HEAD_DIM // 4
    inv = ROPE_THETA ** (-jnp.arange(n_freq, dtype=F32) / n_freq)
    ang = jnp.stack([row, col], axis=-1)[:, :, None] * inv
    cos, sin = jnp.cos(ang), jnp.sin(ang)
    cos_t = jnp.concatenate([cos[:, 0], cos[:, 1], cos[:, 0], cos[:, 1]], axis=-1)
    sin_t = jnp.concatenate([-sin[:, 0], -sin[:, 1], sin[:, 0], sin[:, 1]], axis=-1)
    return cos_t, sin_t


def _rope_head_order(m):
    lead = m.shape[:-1]
    nh = m.shape[-1] // HEAD_DIM
    m = m.reshape(lead + (nh, 2, 2, HEAD_DIM // 4))
    return jnp.swapaxes(m, -3, -2).reshape(lead + (nh * HEAD_DIM,))


def _dft_tables(seq):
    def tw(n):
        i = np.arange(n, dtype=np.int64)
        ang = ((i[:, None] * i[None, :]) % n).astype(np.float64) * (2.0 * math.pi / n)
        return np.cos(ang), np.sin(ang)

    cc, sc = tw(FN_GDIM)
    cl, sl = tw(seq)
    w2 = np.concatenate([cl, -sl], axis=1).astype(np.float32)
    return (jnp.asarray(cc.astype(np.float32), BF16), jnp.asarray(sc.astype(np.float32), BF16),
            jnp.asarray(w2, BF16))


def _ffn_params(w_up, conv_w, conv_b, w_down):
    d = w_up.shape[0]
    half_gate = jnp.concatenate([jnp.ones((D_FF,), F32), jnp.full((D_FF,), 0.5, F32)])
    return (w_up.astype(BF16), conv_w * half_gate, (conv_b * half_gate).reshape(1, 2 * D_FF),
            w_down.astype(BF16).reshape(D_FF // FF_CHUNK, FF_CHUNK, d))


def kernel(x, c, ctx, c_ctx, mod_w, mod_b, ln1_g, ln1_b, ln2_g, ln2_b, ffn_up, ffn_conv_w, ffn_conv_b,
           ffn_down, ab_w_in, ab_w_out, gm_ln_g, gm_ln_b, gm_ws, gm_bs, q_norm_g, k_norm_g, cd_w_in,
           cd_w_out, ssd_conv_w, ssd_conv_b, ssd_dt_bias, ssd_a_log, ssd_d, ssd_norm_g):
    bsz, seq, d = x.shape
    ctx_len = ctx.shape[1]
    lat_rows = min(1024, seq)
    out_rows = min(2048, seq)
    dft_rows = min(2048, seq)
    ctx_rows = min(256, ctx_len)

    pad = (-(bsz + 1)) % V7X_SUBLANES
    cc = jnp.concatenate([c, c_ctx[None], jnp.zeros((pad, d), F32)], axis=0)
    mod = _modulation(cc, mod_w, mod_b)
    rope_tabs = _rope_tables(seq)

    def row(v):
        return v.reshape(1, -1)

    xl, xc = x, ctx
    for i in range(DEPTH):
        last = i == DEPTH - 1
        j = i // 2
        ml = mod[i, :bsz].reshape(bsz, 6, 1, d)
        mc = jnp.broadcast_to(mod[i, bsz].reshape(1, 6, 1, d), (bsz, 6, 1, d))
        sh1, sc1, g1, sh2, sc2, g2 = (ml[:, k] for k in range(6))
        csh1, csc1, cg1, csh2, csc2, cg2 = (mc[:, k] for k in range(6))
        ln1 = (row(ln1_g[i]), row(ln1_b[i]))
        ln2 = (row(ln2_g[i]), row(ln2_b[i]))
        ffn_p = _ffn_params(ffn_up[i], ffn_conv_w[i], ffn_conv_b[i], ffn_down[i])

        if i % 2 == 0:
            w_in = ab_w_in[j].astype(BF16)
            qk_lo, qk_hi = 2 * GM_WIDTH, KV_OFF + KV_W
            q_scale = HEAD_DIM ** -0.5 * math.log2(math.e)
            w_in = jnp.concatenate([0.5 * w_in[:, :qk_lo], _rope_head_order(w_in[:, qk_lo:qk_hi]),
                                    w_in[:, qk_hi:]], axis=1)
            w_out = ab_w_out[j].astype(BF16)
            gm = (row(gm_ln_g[j]), row(gm_ln_b[j]), gm_ws[j].astype(BF16),
                  jnp.repeat(gm_bs[j].T, GM_WIDTH // GM_GROUPS, axis=1),
                  row(_rope_head_order(q_norm_g[j]) * q_scale), row(_rope_head_order(k_norm_g[j])))
            kv_len = seq + ctx_len
            a_c, q_c, k_all, v_all = _ab_in(xc, csh1, csc1, w_in, *gm, None, ctx_rows, kv_len, seq)
            a_l, q_l, k_all, v_all = _ab_in(xl, sh1, sc1, w_in, *gm, rope_tabs, lat_rows, kv_len, 0,
                                            kv_into=(k_all, v_all))
            mix_l = (a_l, _attention(q_l, k_all, v_all, 0, kv_len, lat_rows))
            if not last:
                mix_c = (a_c, _attention(q_c, k_all, v_all, seq, ctx_len, ctx_rows))
        else:
            w_in = cd_w_in[j].astype(BF16)
            w_out = cd_w_out[j].astype(BF16)
            wfz = w_in[:, :SSM_OFF]
            wx = w_in[:, SSM_OFF:SSM_OFF + SSD_CONV_DIM]
            ndt = 2 * SSD_HEADS
            wdt = jnp.pad(w_in[:, SSM_OFF + SSD_CONV_DIM:], ((0, 0), (0, V7X_LANES - ndt)))
            dtb = jnp.pad(ssd_dt_bias[j].reshape(1, ndt), ((0, 0), (0, V7X_LANES - ndt)))
            alog = jnp.pad(ssd_a_log[j].reshape(1, ndt), ((0, 0), (0, V7X_LANES - ndt)))
            cw, cb = 0.5 * ssd_conv_w[j], 0.5 * row(ssd_conv_b[j])
            dskip = row(jnp.repeat(ssd_d[j], SSD_HEADDIM))
            f_l, z_l, xs_l, b_l, c_l, dt_l = _cd_in(xl, sh1, sc1, wfz, wx, wdt, cw, cb, dtb, lat_rows)
            if last:
                xs_c, b_c, _, dt_c = _cd_in(xc, csh1, csc1, None, wx, wdt, cw, cb, dtb, ctx_rows)
            else:
                raise NotImplementedError("odd layers with a context output are not part of this block")
            s_l = _ssd(xs_c, b_c, dt_c, xs_l, b_l, c_l, dt_l, z_l, alog, dskip, row(ssd_norm_g[j]))
            mix_l = (_fourier(f_l, *_dft_tables(seq), dft_rows), s_l)

        xl = _out_ln(*mix_l, w_out, xl, g1, *ln1, out_rows)
        xl = _ffn(xl, sh2, sc2, g2, *ffn_p, *ln2, lat_rows)
        if not last:
            xc = _out_ln(*mix_c, w_out, xc, cg1, *ln1, ctx_rows)
            xc = _ffn(xc, csh2, csc2, cg2, *ffn_p, *ln2, ctx_rows)
    return xl
```

```python
import functools
import math

import jax
import jax.numpy as jnp
import numpy as np
from jax import lax
from jax.experimental import pallas as pl
from jax.experimental.pallas import tpu as pltpu

D_MODEL = 1024
DEPTH = 2
GRID_W = 64
CHUNK = 128
GM_GROUPS = 4
GM_WIDTH = 512
HEAD_DIM = 128
N_Q_HEADS = 4
N_KV_HEADS = 2
ATTN_WIDTH = 512
KV_W = 256
ROPE_THETA = 10000.0
FN_GROUPS = 4
FN_GDIM = 128
FN_WIDTH = 512
SSD_HEADDIM = 64
SSD_HEADS = 8
SSD_GROUPS = 2
SSD_HPG = 4
SSD_STATE = 128
SSD_WIDTH = 512
SSD_GN = 256
SSD_CONV_DIM = 1024
D_FF = 2816
CONV_W = 3
KV_OFF = 2 * GM_WIDTH + ATTN_WIDTH
SSM_OFF = FN_WIDTH + SSD_WIDTH
DN_ALPHA = (2 * DEPTH) ** 0.25
EPS = 1e-6

V7X_LANES = 128
V7X_SUBLANES = 8
V7X_VMEM_BYTES = 64 * 1024 * 1024
VMEM_CHIP_HEADROOM = 3 << 20
VMEM_TEMP_ALLOWANCE = 8 << 20
MOD_COLS = 2048
AB_SUB = 256
FF_CHUNK = 256
SSD_CHUNK = 256

BF16 = jnp.bfloat16
F32 = jnp.float32
HALO = V7X_SUBLANES


def _vmem_limit(nbytes):
    return int(min(nbytes * 3 // 2 + VMEM_TEMP_ALLOWANCE, V7X_VMEM_BYTES - VMEM_CHIP_HEADROOM))


def _resident(shape):
    nd = len(shape)
    return pl.BlockSpec(shape, lambda *_: (0,) * nd, pipeline_mode=pl.Buffered(1))


def _dot(a, b):
    return jnp.dot(a, b, preferred_element_type=F32)


def _dot_nt(a, b):
    return lax.dot_general(a, b, (((1,), (1,)), ((), ())), preferred_element_type=F32)


def _layer_norm(z, g, b, eps=EPS):
    mu = jnp.mean(z, axis=-1, keepdims=True)
    zc = z - mu
    var = jnp.mean(zc * zc, axis=-1, keepdims=True)
    return zc * lax.rsqrt(var + eps) * g + b


def _deepnorm(x, gate, y, g, b):
    return _layer_norm(x + (gate * (1.0 / DN_ALPHA)) * y, g, b, eps=EPS / DN_ALPHA ** 2)


def _rms_norm(z, g):
    return z * lax.rsqrt(jnp.mean(z * z, axis=-1, keepdims=True) + EPS) * g


def _softplus(z):
    return jnp.maximum(z, 0.0) + jnp.log1p(jnp.exp(-jnp.abs(z)))


def _gelu_tanh_of_half(h):
    c1 = math.sqrt(2.0 / math.pi)
    return h + h * jnp.tanh(h * (2.0 * c1 + (8.0 * c1 * 0.044715) * (h * h)))


def _silu(z):
    h = 0.5 * z
    return h + h * jnp.tanh(h)


def _conv3(u, w, b, rows):
    tiles, c = rows // V7X_SUBLANES, u.shape[1]
    u3 = u.reshape(tiles + 2, V7X_SUBLANES, c)
    sub = lax.broadcasted_iota(jnp.int32, (tiles, V7X_SUBLANES, c), 1)
    dn = pltpu.roll(u3, 1, 1)
    up = pltpu.roll(u3, V7X_SUBLANES - 1, 1)
    prev = jnp.where(sub == 0, dn[0:tiles], dn[1:tiles + 1])
    nxt = jnp.where(sub == V7X_SUBLANES - 1, up[2:tiles + 2], up[1:tiles + 1])
    return (b + w[0:1] * prev + w[1:2] * u3[1:tiles + 1] + w[2:3] * nxt).reshape(rows, c)


def _fill_halo_tile(hm_ref, x_ref, xp_ref, xn_ref, sh, sc, rows):
    t = pl.program_id(1)
    nt = pl.num_programs(1)
    keep_prev = jnp.where(t > 0, 1.0, 0.0)
    keep_next = jnp.where(t < nt - 1, 1.0, 0.0)
    hm_ref[HALO:HALO + rows, :] = (x_ref[0] * (1.0 + sc) + sh).astype(BF16)
    hm_ref[0:HALO, :] = ((xp_ref[0] * (1.0 + sc) + sh) * keep_prev).astype(BF16)
    hm_ref[HALO + rows:2 * HALO + rows, :] = ((xn_ref[0] * (1.0 + sc) + sh) * keep_next).astype(BF16)


def _halo_specs(rows, seq, d):
    nb = rows // HALO
    last = seq // HALO - 1
    return [
        pl.BlockSpec((1, rows, d), lambda b, t: (b, t, 0)),
        pl.BlockSpec((1, HALO, d), lambda b, t: (b, jnp.maximum(t * nb - 1, 0), 0)),
        pl.BlockSpec((1, HALO, d), lambda b, t: (b, jnp.minimum((t + 1) * nb, last), 0)),
    ]


def _row_spec(d):
    return pl.BlockSpec((1, 1, d), lambda b, t: (b, 0, 0))


def _mod_kernel(c_ref, w_ref, b_ref, o_ref):
    c = c_ref[...]
    act = c * jax.nn.sigmoid(c)
    o_ref[0] = jnp.dot(act, w_ref[0], preferred_element_type=F32,
                       precision=lax.Precision.HIGHEST) + b_ref[0]


def _modulation(cc, mod_w, mod_b):
    rows, d = cc.shape
    depth, _, n = mod_w.shape
    tn = MOD_COLS
    return pl.pallas_call(
        _mod_kernel,
        out_shape=jax.ShapeDtypeStruct((depth, rows, n), F32),
        grid=(depth, n // tn),
        in_specs=[
            pl.BlockSpec((rows, d), lambda i, j: (0, 0)),
            pl.BlockSpec((1, d, tn), lambda i, j: (i, 0, j)),
            pl.BlockSpec((1, 1, tn), lambda i, j: (i, 0, j)),
        ],
        out_specs=pl.BlockSpec((1, rows, tn), lambda i, j: (i, 0, j)),
        compiler_params=pltpu.CompilerParams(
            dimension_semantics=("arbitrary", "arbitrary"),
            vmem_limit_bytes=_vmem_limit(2 * d * tn * 4)),
        name="modulation",
    )(cc, mod_w, mod_b.reshape(depth, 1, n))


def _ab_in_kernel(*refs, use_rope, rows):
    (x_ref, sh_ref, sc_ref, w_ref, lng_ref, lnb_ref, ws_ref, bias_ref, qg_ref, kg_ref) = refs[:10]
    if use_rope:
        cos_ref, sin_ref = refs[10:12]
    a_ref, q_ref, k_ref, v_ref = refs[-4:]

    gd = GM_WIDTH // GM_GROUPS
    slab_rows = min(AB_SUB, rows)

    def slab(i, carry):
        r0 = pl.multiple_of(i * slab_rows, slab_rows)
        rs = pl.ds(r0, slab_rows)
        h = (x_ref[0, rs, :] * (1.0 + sc_ref[0]) + sh_ref[0]).astype(BF16)

        u = _gelu_tanh_of_half(_dot(h, w_ref[:, 0:GM_WIDTH]))
        vn = _layer_norm(_gelu_tanh_of_half(_dot(h, w_ref[:, GM_WIDTH:2 * GM_WIDTH])),
                         lng_ref[...], lnb_ref[...])
        vb = vn.astype(BF16)
        for c in range(slab_rows // CHUNK):
            r = slice(c * CHUNK, (c + 1) * CHUNK)
            for g in range(GM_GROUPS):
                l = slice(g * gd, (g + 1) * gd)
                mixed = _dot(ws_ref[g], vb[r, l]) + bias_ref[:, l]
                a_ref[0, pl.ds(r0 + c * CHUNK, CHUNK), l] = (u[r, l] * mixed).astype(BF16)

        def rope(z):
            if not use_rope:
                return z
            return z * cos_ref[rs, :] + pltpu.roll(z, HEAD_DIM // 2, 1) * sin_ref[rs, :]

        pq = _dot(h, w_ref[:, 2 * GM_WIDTH:KV_OFF])
        for hh in range(N_Q_HEADS):
            l = slice(hh * HEAD_DIM, (hh + 1) * HEAD_DIM)
            q_ref[0, rs, l] = rope(_rms_norm(pq[:, l], qg_ref[...])).astype(BF16)
        pk = _dot(h, w_ref[:, KV_OFF:KV_OFF + KV_W])
        for hh in range(N_KV_HEADS):
            l = slice(hh * HEAD_DIM, (hh + 1) * HEAD_DIM)
            k_ref[0, rs, l] = rope(_rms_norm(pk[:, l], kg_ref[...])).astype(BF16)
        v_ref[0, rs, :] = _dot(h, w_ref[:, KV_OFF + KV_W:KV_OFF + 2 * KV_W]).astype(BF16)
        return carry

    lax.fori_loop(0, rows // slab_rows, slab, 0)


def _ab_in(x, sh, sc, w, lng, lnb, ws, bias2d, qg, kg, rope_tabs, rows, kv_len, kv_row0, kv_into=None):
    bsz, seq, d = x.shape
    assert kv_row0 % rows == 0
    kv_blk0 = kv_row0 // rows
    n = w.shape[1]
    use_rope = rope_tabs is not None
    in_specs = [
        pl.BlockSpec((1, rows, d), lambda b, t: (b, t, 0)),
        _row_spec(d), _row_spec(d),
        _resident((d, n)),
        _resident((1, GM_WIDTH)), _resident((1, GM_WIDTH)),
        _resident((GM_GROUPS, CHUNK, CHUNK)),
        _resident((CHUNK, GM_WIDTH)),
        _resident((1, HEAD_DIM)), _resident((1, HEAD_DIM)),
    ]
    args = [x, sh, sc, w, lng, lnb, ws, bias2d, qg, kg]
    if use_rope:
        in_specs += [pl.BlockSpec((rows, HEAD_DIM), lambda b, t: (t, 0))] * 2
        args += list(rope_tabs)
    aliases = {}
    if kv_into is not None:
        aliases = {len(args): 2, len(args) + 1: 3}
        in_specs += [pl.BlockSpec(memory_space=pl.ANY)] * 2
        args += list(kv_into)
    out_shape = [
        jax.ShapeDtypeStruct((bsz, seq, GM_WIDTH), BF16),
        jax.ShapeDtypeStruct((bsz, seq, ATTN_WIDTH), BF16),
        jax.ShapeDtypeStruct((bsz, kv_len, KV_W), BF16),
        jax.ShapeDtypeStruct((bsz, kv_len, KV_W), BF16),
    ]
    out_specs = [
        pl.BlockSpec((1, rows, GM_WIDTH), lambda b, t: (b, t, 0)),
        pl.BlockSpec((1, rows, ATTN_WIDTH), lambda b, t: (b, t, 0)),
        pl.BlockSpec((1, rows, KV_W), lambda b, t: (b, t + kv_blk0, 0)),
        pl.BlockSpec((1, rows, KV_W), lambda b, t: (b, t + kv_blk0, 0)),
    ]
    est = d * n * 2 + 2 * rows * d * 4 + 6 * rows * n * 4
    return pl.pallas_call(
        functools.partial(_ab_in_kernel, use_rope=use_rope, rows=rows),
        out_shape=out_shape,
        grid=(bsz, seq // rows),
        in_specs=in_specs,
        out_specs=out_specs,
        input_output_aliases=aliases,
        compiler_params=pltpu.CompilerParams(
            dimension_semantics=("arbitrary", "arbitrary"),
            vmem_limit_bytes=_vmem_limit(est)),
        name="ab_in_rope" if use_rope else "ab_in_ctx",
    )(*args)


def _attn_kernel(q_ref, k_ref, v_ref, o_ref, vt_ref, *, rows, blk):
    @pl.when(pl.program_id(2) == 0)
    def _():
        vt_ref[...] = v_ref[0].astype(F32).T

    k = k_ref[0]
    vt = vt_ref[...].astype(BF16)
    for h in range(N_Q_HEADS // N_KV_HEADS):
        l_ = slice(h * HEAD_DIM, (h + 1) * HEAD_DIM)
        for r0 in range(0, rows, blk):
            st = _dot_nt(k, q_ref[0, r0:r0 + blk, l_])
            m = jnp.max(st, axis=0, keepdims=True)
            p = jnp.exp2(st - m)
            l = jnp.sum(p, axis=0, keepdims=True)
            ot = _dot(vt, p.astype(BF16)) / l
            o_ref[0, r0:r0 + blk, l_] = ot.T.astype(BF16)


def _attention(q, k, v, k_row0, sk, rows, blk=512):
    bsz, sq, _ = q.shape
    assert k_row0 % sk == 0
    kb = k_row0 // sk
    gw = (N_Q_HEADS // N_KV_HEADS) * HEAD_DIM
    est = 4 * blk * sk * 12 + 4 * sk * HEAD_DIM * 2 + 4 * rows * gw * 2 + sk * HEAD_DIM * 4
    return pl.pallas_call(
        functools.partial(_attn_kernel, rows=rows, blk=min(blk, rows)),
        out_shape=jax.ShapeDtypeStruct((bsz, sq, ATTN_WIDTH), BF16),
        grid=(bsz, N_KV_HEADS, sq // rows),
        in_specs=[
            pl.BlockSpec((1, rows, gw), lambda b, h, t: (b, t, h)),
            pl.BlockSpec((1, sk, HEAD_DIM), lambda b, h, t: (b, kb, h)),
            pl.BlockSpec((1, sk, HEAD_DIM), lambda b, h, t: (b, kb, h)),
        ],
        out_specs=pl.BlockSpec((1, rows, gw), lambda b, h, t: (b, t, h)),
        scratch_shapes=[pltpu.VMEM((HEAD_DIM, sk), F32)],
        compiler_params=pltpu.CompilerParams(
            dimension_semantics=("arbitrary", "arbitrary", "arbitrary"),
            vmem_limit_bytes=_vmem_limit(est)),
        name="attention",
    )(q, k, v)


def _out_ln_kernel(a1_ref, a2_ref, w_ref, x_ref, g_ref, lng_ref, lnb_ref, o_ref):
    half = a1_ref.shape[2]
    y = _dot(a1_ref[0], w_ref[0:half, :]) + _dot(a2_ref[0], w_ref[half:2 * half, :])
    o_ref[0] = _deepnorm(x_ref[0], g_ref[0], y, lng_ref[...], lnb_ref[...])


def _out_ln(a1, a2, w, x, gate, lng, lnb, rows):
    bsz, seq, d = x.shape
    half = a1.shape[2]
    est = 2 * half * d * 2 + 4 * rows * d * 4 + 4 * rows * half * 2 + 2 * rows * d * 4
    return pl.pallas_call(
        _out_ln_kernel,
        out_shape=jax.ShapeDtypeStruct((bsz, seq, d), F32),
        grid=(bsz, seq // rows),
        in_specs=[
            pl.BlockSpec((1, rows, half), lambda b, t: (b, t, 0)),
            pl.BlockSpec((1, rows, half), lambda b, t: (b, t, 0)),
            _resident((2 * half, d)),
            pl.BlockSpec((1, rows, d), lambda b, t: (b, t, 0)),
            _row_spec(d),
            _resident((1, d)), _resident((1, d)),
        ],
        out_specs=pl.BlockSpec((1, rows, d), lambda b, t: (b, t, 0)),
        compiler_params=pltpu.CompilerParams(
            dimension_semantics=("arbitrary", "arbitrary"),
            vmem_limit_bytes=_vmem_limit(est)),
        name="out_ln",
    )(a1, a2, w, x, gate, lng, lnb)


def _ffn_kernel(x_ref, xp_ref, xn_ref, sh_ref, sc_ref, g_ref, wu_ref, cw_ref, cb_ref, wd_ref, lng_ref, lnb_ref,
                o_ref, hm_ref, u_ref, gt_ref, *, rows):
    _fill_halo_tile(hm_ref, x_ref, xp_ref, xn_ref, sh_ref[0], sc_ref[0], rows)
    n_chunks = wd_ref.shape[0]

    def cols(ref, c):
        off = c * FF_CHUNK
        return ref[:, pl.ds(off if isinstance(off, int) else pl.multiple_of(off, FF_CHUNK), FF_CHUNK)]
    tiles = rows // V7X_SUBLANES
    sub = lax.broadcasted_iota(jnp.int32, (tiles, V7X_SUBLANES, FF_CHUNK), 1)

    def up_proj(c, slot):
        hm = hm_ref[...]
        u_ref[slot, 0] = _dot(hm, cols(wu_ref, c)).reshape(tiles + 2, V7X_SUBLANES, FF_CHUNK)
        u_ref[slot, 1] = _dot(hm, cols(wu_ref, c + n_chunks)).reshape(tiles + 2, V7X_SUBLANES, FF_CHUNK)

    def conv_gate(c, slot):
        def conv(half):
            u = u_ref[slot, half]
            dn = pltpu.roll(u, 1, 1)
            up = pltpu.roll(u, V7X_SUBLANES - 1, 1)
            prev = jnp.where(sub == 0, dn[0:tiles], dn[1:tiles + 1])
            nxt = jnp.where(sub == V7X_SUBLANES - 1, up[2:tiles + 2], up[1:tiles + 1])
            w = cols(cw_ref, c + half * n_chunks)
            return cols(cb_ref, c + half * n_chunks) + w[0:1] * prev + w[1:2] * u[1:tiles + 1] + w[2:3] * nxt

        a1 = conv(0)
        a2 = conv(1)
        gt_ref[c] = (a1 * (a2 + a2 * jnp.tanh(a2))).reshape(rows, FF_CHUNK)

    up_proj(0, 0)

    def pair(k, carry):
        c = 2 * k
        up_proj(c + 1, 1)
        conv_gate(c, 0)
        up_proj(c + 2, 0)
        conv_gate(c + 1, 1)
        return carry

    lax.fori_loop(0, (n_chunks - 1) // 2, pair, 0)
    conv_gate(n_chunks - 1, 0)

    f = _dot(gt_ref[0].astype(BF16), wd_ref[0])
    for c in range(1, n_chunks):
        f = f + _dot(gt_ref[c].astype(BF16), wd_ref[c])
    o_ref[0] = _deepnorm(x_ref[0], g_ref[0], f, lng_ref[...], lnb_ref[...])


def _ffn(x, sh, sc, gate, wu, cw, cb, wd, layer, lng, lnb, rows):
    bsz, seq, d = x.shape
    nc = wd.shape[1]
    dff = nc * FF_CHUNK
    assert wu.shape[2] == 2 * dff and nc % 2 == 1, "the pipeline peels one chunk and pairs the rest"

    def of_layer(*shape):
        nd = len(shape)
        return pl.BlockSpec((None,) + shape, lambda b, t: (layer,) + (0,) * nd, pipeline_mode=pl.Buffered(1))

    est = (3 * nc * d * FF_CHUNK * 2 + 4 * rows * d * 4 + (rows + 2 * HALO) * d * 2
           + nc * rows * FF_CHUNK * 4 + 8 * (rows + 2 * HALO) * FF_CHUNK * 4 + 2 * rows * d * 4)
    return pl.pallas_call(
        functools.partial(_ffn_kernel, rows=rows),
        out_shape=jax.ShapeDtypeStruct((bsz, seq, d), F32),
        grid=(bsz, seq // rows),
        in_specs=_halo_specs(rows, seq, d) + [
            _row_spec(d), _row_spec(d), _row_spec(d),
            of_layer(d, 2 * dff), of_layer(CONV_W, 2 * dff), of_layer(1, 2 * dff),
            of_layer(nc, FF_CHUNK, d),
            _resident((1, d)), _resident((1, d)),
        ],
        out_specs=pl.BlockSpec((1, rows, d), lambda b, t: (b, t, 0)),
        scratch_shapes=[
            pltpu.VMEM((rows + 2 * HALO, d), BF16),
            pltpu.VMEM((2, 2, rows // HALO + 2, HALO, FF_CHUNK), F32),
            pltpu.VMEM((nc, rows, FF_CHUNK), F32),
        ],
        compiler_params=pltpu.CompilerParams(
            dimension_semantics=("arbitrary", "arbitrary"),
            vmem_limit_bytes=_vmem_limit(est)),
        name="conv_ffn",
    )(x, x, x, sh, sc, gate, wu, cw, cb, wd, lng, lnb)


def _cd_in_kernel(*refs, with_fz, rows):
    x_ref, xp_ref, xn_ref, sh_ref, sc_ref = refs[:5]
    if with_fz:
        wfz_ref, wx_ref, wdt_ref, cw_ref, cb_ref, dtb_ref = refs[5:11]
        f_ref, z_ref, xs_ref, b_ref, c_ref, dt_ref, hm_ref = refs[11:]
    else:
        wx_ref, wdt_ref, cw_ref, cb_ref, dtb_ref = refs[5:10]
        xs_ref, b_ref, c_ref, dt_ref, hm_ref = refs[10:]
    _fill_halo_tile(hm_ref, x_ref, xp_ref, xn_ref, sh_ref[0], sc_ref[0], rows)
    hc = hm_ref[HALO:HALO + rows, :]
    if with_fz:
        pfz = _dot(hc, wfz_ref[...])
        f_ref[0] = pfz[:, 0:FN_WIDTH].astype(BF16)
        z_ref[0] = pfz[:, FN_WIDTH:SSM_OFF]
    a = _conv3(_dot(hm_ref[...], wx_ref[...]), cw_ref[...], cb_ref[...], rows)
    s = a + a * jnp.tanh(a)
    xs_ref[0] = s[:, 0:SSD_WIDTH]
    b_ref[0] = s[:, SSD_WIDTH:SSD_WIDTH + SSD_GN].astype(BF16)
    c_ref[0] = s[:, SSD_WIDTH + SSD_GN:SSD_CONV_DIM].astype(BF16)
    dt_ref[0] = _softplus(_dot(hc, wdt_ref[...]) + dtb_ref[...])


def _cd_in(x, sh, sc, wfz, wx, wdt, cw, cb, dtb, rows):
    bsz, seq, d = x.shape
    with_fz = wfz is not None
    in_specs = _halo_specs(rows, seq, d) + [_row_spec(d), _row_spec(d)]
    args = [x, x, x, sh, sc]
    out_shape, out_specs = [], []
    if with_fz:
        in_specs.append(_resident((d, SSM_OFF)))
        args.append(wfz)
        out_shape += [jax.ShapeDtypeStruct((bsz, seq, FN_WIDTH), BF16),
                      jax.ShapeDtypeStruct((bsz, seq, SSD_WIDTH), F32)]
        out_specs += [pl.BlockSpec((1, rows, FN_WIDTH), lambda b, t: (b, t, 0)),
                      pl.BlockSpec((1, rows, SSD_WIDTH), lambda b, t: (b, t, 0))]
    in_specs += [_resident((d, SSD_CONV_DIM)), _resident((d, V7X_LANES)),
                 _resident((CONV_W, SSD_CONV_DIM)), _resident((1, SSD_CONV_DIM)),
                 _resident((1, V7X_LANES))]
    args += [wx, wdt, cw, cb, dtb]
    out_shape += [jax.ShapeDtypeStruct((bsz, seq, SSD_WIDTH), F32),
                  jax.ShapeDtypeStruct((bsz, seq, SSD_GN), BF16),
                  jax.ShapeDtypeStruct((bsz, seq, SSD_GN), BF16),
                  jax.ShapeDtypeStruct((bsz, seq, V7X_LANES), F32)]
    out_specs += [pl.BlockSpec((1, rows, SSD_WIDTH), lambda b, t: (b, t, 0)),
                  pl.BlockSpec((1, rows, SSD_GN), lambda b, t: (b, t, 0)),
                  pl.BlockSpec((1, rows, SSD_GN), lambda b, t: (b, t, 0)),
                  pl.BlockSpec((1, rows, V7X_LANES), lambda b, t: (b, t, 0))]
    est = d * 2200 * 2 + 2 * rows * d * 4 + 8 * (rows + 2 * HALO) * SSD_CONV_DIM * 4
    return pl.pallas_call(
        functools.partial(_cd_in_kernel, with_fz=with_fz, rows=rows),
        out_shape=out_shape,
        grid=(bsz, seq // rows),
        in_specs=in_specs,
        out_specs=out_specs,
        scratch_shapes=[pltpu.VMEM((rows + 2 * HALO, d), BF16)],
        compiler_params=pltpu.CompilerParams(
            dimension_semantics=("arbitrary", "arbitrary"),
            vmem_limit_bytes=_vmem_limit(est)),
        name="cd_in" if with_fz else "cd_in_ctx",
    )(*args)


def _cumsum_rows(z, reverse):
    n = z.shape[0]
    idx = lax.broadcasted_iota(jnp.int32, z.shape, 0)
    k = 1
    while k < n:
        if reverse:
            z = z + jnp.where(idx < n - k, pltpu.roll(z, n - k, 0), 0.0)
        else:
            z = z + jnp.where(idx >= k, pltpu.roll(z, k, 0), 0.0)
        k *= 2
    return z


def _lane_expand(row, lane0, width):
    lane = lax.broadcasted_iota(jnp.int32, (1, SSD_HPG * width), 1)
    out = jnp.zeros((1, SSD_HPG * width), F32)
    for j in range(SSD_HPG):
        val = jnp.broadcast_to(row[:, lane0 + j:lane0 + j + 1], (1, SSD_HPG * width))
        out = jnp.where(lane // width == j, val, out)
    return out


def _ssd_chunk(x, bm, cm, dt, a_row, h_ref, lane0, reverse, need_y):
    q = x.shape[0]
    gw = SSD_HPG * SSD_HEADDIM
    acum = _cumsum_rows(dt * a_row, reverse)
    total = acum[0:1, :] if reverse else acum[q - 1:q, :]
    acum_t = acum.T
    dt_t = dt.T
    total_t = jnp.broadcast_to(acum_t[:, 0:1] if reverse else acum_t[:, q - 1:q], acum_t.shape)
    wgt_t = dt_t * jnp.exp2(total_t - acum_t)
    lane_g = lax.broadcasted_iota(jnp.int32, (q, gw), 1) // SSD_HEADDIM
    lane_gh = lax.broadcasted_iota(jnp.int32, (SSD_STATE, gw), 1) // SSD_HEADDIM
    if need_y:
        srow_t = acum_t - jnp.log2(dt_t)
        ti = lax.broadcasted_iota(jnp.int32, (q, q), 0)
        si = lax.broadcasted_iota(jnp.int32, (q, q), 1)
        mask = (si >= ti) if reverse else (si <= ti)
    ys = []
    for g in range(SSD_GROUPS):
        bg = bm[:, g * SSD_STATE:(g + 1) * SSD_STATE]
        xgb = x[:, g * gw:(g + 1) * gw].astype(BF16)
        x_heads = [jnp.where(lane_g == j, xgb, jnp.zeros_like(xgb)) for j in range(SSD_HPG)]
        hg = h_ref[g]
        if need_y:
            cg = cm[:, g * SSD_STATE:(g + 1) * SSD_STATE]
            cb = _dot_nt(cg, bg)
            cgf = cg.astype(F32)
            hgb = hg.astype(BF16)
            lhs, rhs = [], []
            for j in range(SSD_HPG):
                jl = lane0 + g * SSD_HPG + j
                col = jnp.broadcast_to(acum[:, jl:jl + 1], (q, q))
                row = jnp.broadcast_to(srow_t[jl:jl + 1, :], (q, q))
                lhs.append((cb * jnp.exp2(jnp.where(mask, col - row, -jnp.inf))).astype(BF16))
                lhs.append((cgf * jnp.exp2(col[:, 0:SSD_STATE])).astype(BF16))
                rhs.append(x_heads[j])
                rhs.append(jnp.where(lane_gh == j, hgb, jnp.zeros_like(hgb)))
            ys.append(_dot(jnp.concatenate(lhs, axis=1), jnp.concatenate(rhs, axis=0)))
        bgt = bg.astype(F32).T
        wb = [(bgt * jnp.broadcast_to(wgt_t[lane0 + g * SSD_HPG + j:lane0 + g * SSD_HPG + j + 1, :],
                                      (SSD_STATE, q))).astype(BF16) for j in range(SSD_HPG)]
        st = _dot(jnp.concatenate(wb, axis=1), jnp.concatenate(x_heads, axis=0))
        h_ref[g] = hg * jnp.exp2(_lane_expand(total, lane0 + g * SSD_HPG, SSD_HEADDIM)) + st
    if need_y:
        return jnp.concatenate(ys, axis=1)
    return None


def _ssd_kernel(xc_ref, bc_ref, dtc_ref, xl_ref, bl_ref, cl_ref, dtl_ref, z_ref, alog_ref,
                dskip_ref, ng_ref, o_ref, y_ref, h_ref):
    a_row = -jnp.exp(alog_ref[...]) * math.log2(math.e)
    nc_ctx = xc_ref.shape[1] // SSD_CHUNK
    nc_lat = xl_ref.shape[1] // SSD_CHUNK

    def rows_of(c):
        return pl.ds(pl.multiple_of(c * SSD_CHUNK, SSD_CHUNK), SSD_CHUNK)

    def ctx_step(c, direction):
        r = rows_of(c)
        _ssd_chunk(xc_ref[0, r, :], bc_ref[0, r, :], None, dtc_ref[0, r, :], a_row,
                   h_ref.at[direction], direction * SSD_HEADS, direction == 1, False)

    def lat_step(c, direction):
        r = rows_of(c)
        y_ref[direction, r, :] = _ssd_chunk(
            xl_ref[0, r, :], bl_ref[0, r, :], cl_ref[0, r, :], dtl_ref[0, r, :], a_row,
            h_ref.at[direction], direction * SSD_HEADS, direction == 1, True)

    h_ref[...] = jnp.zeros_like(h_ref)

    def ctx_pair(i, carry):
        ctx_step(i, 0)
        ctx_step(nc_ctx - 1 - i, 1)
        return carry

    lax.fori_loop(0, nc_ctx, ctx_pair, 0)

    def lat_pair(i, carry):
        lat_step(i, 0)
        lat_step(nc_lat - 1 - i, 1)
        return carry

    lax.fori_loop(0, nc_lat, lat_pair, 0)

    def finish(c, carry):
        r = rows_of(c)
        zz = z_ref[0, r, :]
        yt = (y_ref[0, r, :] + y_ref[1, r, :] + xl_ref[0, r, :] * dskip_ref[...]) * _silu(zz)
        o_ref[0, r, :] = _rms_norm(yt, ng_ref[...]).astype(BF16)
        return carry

    lax.fori_loop(0, nc_lat, finish, 0)


def _ssd(xs_c, b_c, dt_c, xs_l, b_l, c_l, dt_l, z, alog_row, dskip_row, ng_row):
    bsz, sl, w = xs_l.shape
    sc = xs_c.shape[1]
    assert sl % SSD_CHUNK == 0 and sc % SSD_CHUNK == 0
    gw = SSD_HPG * SSD_HEADDIM

    def full(s, n):
        return pl.BlockSpec((1, s, n), lambda b: (b, 0, 0))

    est = (2 * (sl + sc) * (w * 4 + 2 * SSD_GN * 2 + V7X_LANES * 4) + 2 * sl * w * 4
           + 2 * sl * w * 2 + 2 * sl * w * 4)
    return pl.pallas_call(
        _ssd_kernel,
        out_shape=jax.ShapeDtypeStruct((bsz, sl, w), BF16),
        grid=(bsz,),
        in_specs=[
            full(sc, w), full(sc, SSD_GN), full(sc, V7X_LANES),
            full(sl, w), full(sl, SSD_GN), full(sl, SSD_GN), full(sl, V7X_LANES),
            full(sl, w),
            _resident((1, V7X_LANES)), _resident((1, w)), _resident((1, w)),
        ],
        out_specs=full(sl, w),
        scratch_shapes=[
            pltpu.VMEM((2, sl, w), F32),
            pltpu.VMEM((2, SSD_GROUPS, SSD_STATE, gw), F32),
        ],
        compiler_params=pltpu.CompilerParams(
            dimension_semantics=("arbitrary",),
            vmem_limit_bytes=_vmem_limit(est)),
        name="ssd_scan",
    )(xs_c, b_c, dt_c, xs_l, b_l, c_l, dt_l, z, alog_row, dskip_row, ng_row)


def _fourier_kernel(f_ref, cc_ref, sc_ref, w2_ref, o_ref, y_ref, *, rows):
    seq = f_ref.shape[1]
    for g in range(FN_GROUPS):
        l = slice(g * FN_GDIM, (g + 1) * FN_GDIM)
        fg = f_ref[0, :, l]
        y_ref[0:seq, l] = _dot(fg, cc_ref[...]).astype(BF16)
        y_ref[seq:2 * seq, l] = _dot(fg, sc_ref[...]).astype(BF16)
    norm = 1.0 / math.sqrt(seq * FN_GDIM)

    def tile(i, carry):
        r = pl.ds(pl.multiple_of(i * rows, rows), rows)
        o_ref[0, r, :] = (_dot(w2_ref[r, :], y_ref[...]) * norm).astype(BF16)
        return carry

    lax.fori_loop(0, seq // rows, tile, 0)


def _fourier(f, cc, sc, w2, rows):
    bsz, seq, w = f.shape
    est = seq * 2 * seq * 2 + 4 * seq * w * 2 + 2 * seq * w * 2 + 4 * rows * 2 * seq * 2
    return pl.pallas_call(
        functools.partial(_fourier_kernel, rows=rows),
        out_shape=jax.ShapeDtypeStruct((bsz, seq, w), BF16),
        grid=(bsz,),
        in_specs=[
            pl.BlockSpec((1, seq, w), lambda b: (b, 0, 0)),
            _resident((FN_GDIM, FN_GDIM)), _resident((FN_GDIM, FN_GDIM)),
            _resident((seq, 2 * seq)),
        ],
        out_specs=pl.BlockSpec((1, seq, w), lambda b: (b, 0, 0)),
        scratch_shapes=[pltpu.VMEM((2 * seq, w), BF16)],
        compiler_params=pltpu.CompilerParams(
            dimension_semantics=("arbitrary",),
            vmem_limit_bytes=_vmem_limit(est)),
        name="fourier_mix",
    )(f, cc, sc, w2)


def _rope_tables(seq):
    rows = seq // GRID_W
    row = jnp.repeat(jnp.arange(rows, dtype=F32), GRID_W)
    col = jnp.broadcast_to(jnp.arange(GRID_W, dtype=F32), (rows, GRID_W)).reshape(-1)
    n_freq = HEAD_DIM // 4
    inv = ROPE_THETA ** (-jnp.arange(n_freq, dtype=F32) / n_freq)
    ang = jnp.stack([row, col], axis=-1)[:, :, None] * inv
    cos, sin = jnp.cos(ang), jnp.sin(ang)
    cos_t = jnp.concatenate([cos[:, 0], cos[:, 1], cos[:, 0], cos[:, 1]], axis=-1)
    sin_t = jnp.concatenate([-sin[:, 0], -sin[:, 1], sin[:, 0], sin[:, 1]], axis=-1)
    return cos_t, sin_t


def _rope_head_order(m):
    lead = m.shape[:-1]
    nh = m.shape[-1] // HEAD_DIM
    m = m.reshape(lead + (nh, 2, 2, HEAD_DIM // 4))
    return jnp.swapaxes(m, -3, -2).reshape(lead + (nh * HEAD_DIM,))


def _dft_tables(seq):
    def tw(n):
        i = np.arange(n, dtype=np.int64)
        ang = ((i[:, None] * i[None, :]) % n).astype(np.float64) * (2.0 * math.pi / n)
        return np.cos(ang), np.sin(ang)

    cc, sc = tw(FN_GDIM)
    cl, sl = tw(seq)
    w2 = np.concatenate([cl, -sl], axis=1).astype(np.float32)
    return (jnp.asarray(cc.astype(np.float32), BF16), jnp.asarray(sc.astype(np.float32), BF16),
            jnp.asarray(w2, BF16))


def _ffn_params(w_up, conv_w, conv_b, w_down):
    depth, d, _ = w_up.shape
    half_gate = jnp.concatenate([jnp.ones((D_FF,), F32), jnp.full((D_FF,), 0.5, F32)])
    return (w_up.astype(BF16), conv_w * half_gate, (conv_b * half_gate).reshape(depth, 1, 2 * D_FF),
            w_down.astype(BF16).reshape(depth, D_FF // FF_CHUNK, FF_CHUNK, d))


def kernel(x, c, ctx, c_ctx, mod_w, mod_b, ln1_g, ln1_b, ln2_g, ln2_b, ffn_up, ffn_conv_w, ffn_conv_b,
           ffn_down, ab_w_in, ab_w_out, gm_ln_g, gm_ln_b, gm_ws, gm_bs, q_norm_g, k_norm_g, cd_w_in,
           cd_w_out, ssd_conv_w, ssd_conv_b, ssd_dt_bias, ssd_a_log, ssd_d, ssd_norm_g):
    bsz, seq, d = x.shape
    ctx_len = ctx.shape[1]
    lat_rows = min(1024, seq)
    out_rows = min(2048, seq)
    dft_rows = min(2048, seq)
    ctx_rows = min(256, ctx_len)

    pad = (-(bsz + 1)) % V7X_SUBLANES
    cc = jnp.concatenate([c, c_ctx[None], jnp.zeros((pad, d), F32)], axis=0)
    mod = _modulation(cc, mod_w, mod_b)
    rope_tabs = _rope_tables(seq)

    def row(v):
        return v.reshape(1, -1)

    ffn_p = _ffn_params(ffn_up, ffn_conv_w, ffn_conv_b, ffn_down)
    xl, xc = x, ctx
    for i in range(DEPTH):
        last = i == DEPTH - 1
        j = i // 2
        ml = mod[i, :bsz].reshape(bsz, 6, 1, d)
        mc = jnp.broadcast_to(mod[i, bsz].reshape(1, 6, 1, d), (bsz, 6, 1, d))
        sh1, sc1, g1, sh2, sc2, g2 = (ml[:, k] for k in range(6))
        csh1, csc1, cg1, csh2, csc2, cg2 = (mc[:, k] for k in range(6))
        ln1 = (row(ln1_g[i]), row(ln1_b[i]))
        ln2 = (row(ln2_g[i]), row(ln2_b[i]))

        if i % 2 == 0:
            w_in = ab_w_in[j].astype(BF16)
            qk_lo, qk_hi = 2 * GM_WIDTH, KV_OFF + KV_W
            q_scale = HEAD_DIM ** -0.5 * math.log2(math.e)
            w_in = jnp.concatenate([0.5 * w_in[:, :qk_lo], _rope_head_order(w_in[:, qk_lo:qk_hi]),
                                    w_in[:, qk_hi:]], axis=1)
            w_out = ab_w_out[j].astype(BF16)
            gm = (row(gm_ln_g[j]), row(gm_ln_b[j]), gm_ws[j].astype(BF16),
                  jnp.repeat(gm_bs[j].T, GM_WIDTH // GM_GROUPS, axis=1),
                  row(_rope_head_order(q_norm_g[j]) * q_scale), row(_rope_head_order(k_norm_g[j])))
            kv_len = seq + ctx_len
            a_c, q_c, k_all, v_all = _ab_in(xc, csh1, csc1, w_in, *gm, None, ctx_rows, kv_len, seq)
            a_l, q_l, k_all, v_all = _ab_in(xl, sh1, sc1, w_in, *gm, rope_tabs, lat_rows, kv_len, 0,
                                            kv_into=(k_all, v_all))
            mix_l = (a_l, _attention(q_l, k_all, v_all, 0, kv_len, lat_rows))
            if not last:
                mix_c = (a_c, _attention(q_c, k_all, v_all, seq, ctx_len, ctx_rows))
        else:
            w_in = cd_w_in[j].astype(BF16)
            w_out = cd_w_out[j].astype(BF16)
            wfz = w_in[:, :SSM_OFF]
            wx = w_in[:, SSM_OFF:SSM_OFF + SSD_CONV_DIM]
            ndt = 2 * SSD_HEADS
            wdt = jnp.pad(w_in[:, SSM_OFF + SSD_CONV_DIM:], ((0, 0), (0, V7X_LANES - ndt)))
            dtb = jnp.pad(ssd_dt_bias[j].reshape(1, ndt), ((0, 0), (0, V7X_LANES - ndt)))
            alog = jnp.pad(ssd_a_log[j].reshape(1, ndt), ((0, 0), (0, V7X_LANES - ndt)))
            cw, cb = 0.5 * ssd_conv_w[j], 0.5 * row(ssd_conv_b[j])
            dskip = row(jnp.repeat(ssd_d[j], SSD_HEADDIM))
            f_l, z_l, xs_l, b_l, c_l, dt_l = _cd_in(xl, sh1, sc1, wfz, wx, wdt, cw, cb, dtb, lat_rows)
            if last:
                xs_c, b_c, _, dt_c = _cd_in(xc, csh1, csc1, None, wx, wdt, cw, cb, dtb, ctx_rows)
            else:
                raise NotImplementedError("odd layers with a context output are not part of this block")
            s_l = _ssd(xs_c, b_c, dt_c, xs_l, b_l, c_l, dt_l, z_l, alog, dskip, row(ssd_norm_g[j]))
            mix_l = (_fourier(f_l, *_dft_tables(seq), dft_rows), s_l)

        xl = _out_ln(*mix_l, w_out, xl, g1, *ln1, out_rows)
        xl = _ffn(xl, sh2, sc2, g2, *ffn_p, i, *ln2, lat_rows)
        if not last:
            xc = _out_ln(*mix_c, w_out, xc, cg1, *ln1, ctx_rows)
            xc = _ffn(xc, csh2, csc2, cg2, *ffn_p, i, *ln2, ctx_rows)
    return xl
```

```python
import functools
import math

import jax
import jax.numpy as jnp
import numpy as np
from jax import lax
from jax.experimental import pallas as pl
from jax.experimental.pallas import tpu as pltpu

D_MODEL = 1024
DEPTH = 2
GRID_W = 64
CHUNK = 128
GM_GROUPS = 4
GM_WIDTH = 512
HEAD_DIM = 128
N_Q_HEADS = 4
N_KV_HEADS = 2
ATTN_WIDTH = 512
KV_W = 256
ROPE_THETA = 10000.0
FN_GROUPS = 4
FN_GDIM = 128
FN_WIDTH = 512
SSD_HEADDIM = 64
SSD_HEADS = 8
SSD_GROUPS = 2
SSD_HPG = 4
SSD_STATE = 128
SSD_WIDTH = 512
SSD_GN = 256
SSD_CONV_DIM = 1024
D_FF = 2816
CONV_W = 3
KV_OFF = 2 * GM_WIDTH + ATTN_WIDTH
SSM_OFF = FN_WIDTH + SSD_WIDTH
DN_ALPHA = (2 * DEPTH) ** 0.25
EPS = 1e-6

V7X_LANES = 128
V7X_SUBLANES = 8
V7X_VMEM_BYTES = 64 * 1024 * 1024
VMEM_CHIP_HEADROOM = 3 << 20
VMEM_TEMP_ALLOWANCE = 8 << 20
MOD_COLS = 2048
AB_SUB = 256
FF_CHUNK = 256
SSD_CHUNK = 256

BF16 = jnp.bfloat16
F32 = jnp.float32
HALO = V7X_SUBLANES


def _vmem_limit(nbytes):
    return int(min(nbytes * 3 // 2 + VMEM_TEMP_ALLOWANCE, V7X_VMEM_BYTES - VMEM_CHIP_HEADROOM))


def _resident(shape):
    nd = len(shape)
    return pl.BlockSpec(shape, lambda *_: (0,) * nd, pipeline_mode=pl.Buffered(1))


def _dot(a, b):
    return jnp.dot(a, b, preferred_element_type=F32)


def _dot_nt(a, b):
    return lax.dot_general(a, b, (((1,), (1,)), ((), ())), preferred_element_type=F32)


def _layer_norm(z, g, b, eps=EPS):
    mu = jnp.mean(z, axis=-1, keepdims=True)
    zc = z - mu
    var = jnp.mean(zc * zc, axis=-1, keepdims=True)
    return zc * lax.rsqrt(var + eps) * g + b


def _deepnorm(x, gate, y, g, b):
    return _layer_norm(x + (gate * (1.0 / DN_ALPHA)) * y, g, b, eps=EPS / DN_ALPHA ** 2)


def _rms_norm(z, g):
    return z * lax.rsqrt(jnp.mean(z * z, axis=-1, keepdims=True) + EPS) * g


def _softplus(z):
    return jnp.maximum(z, 0.0) + jnp.log1p(jnp.exp(-jnp.abs(z)))


def _gelu_tanh_of_half(h):
    c1 = math.sqrt(2.0 / math.pi)
    return h + h * jnp.tanh(h * (2.0 * c1 + (8.0 * c1 * 0.044715) * (h * h)))


def _silu(z):
    h = 0.5 * z
    return h + h * jnp.tanh(h)


def _conv3(u, w, b, rows):
    tiles, c = rows // V7X_SUBLANES, u.shape[1]
    u3 = u.reshape(tiles + 2, V7X_SUBLANES, c)
    sub = lax.broadcasted_iota(jnp.int32, (tiles, V7X_SUBLANES, c), 1)
    dn = pltpu.roll(u3, 1, 1)
    up = pltpu.roll(u3, V7X_SUBLANES - 1, 1)
    prev = jnp.where(sub == 0, dn[0:tiles], dn[1:tiles + 1])
    nxt = jnp.where(sub == V7X_SUBLANES - 1, up[2:tiles + 2], up[1:tiles + 1])
    return (b + w[0:1] * prev + w[1:2] * u3[1:tiles + 1] + w[2:3] * nxt).reshape(rows, c)


def _fill_halo_tile(hm_ref, x_ref, xp_ref, xn_ref, sh, sc, rows):
    t = pl.program_id(1)
    nt = pl.num_programs(1)
    keep_prev = jnp.where(t > 0, 1.0, 0.0)
    keep_next = jnp.where(t < nt - 1, 1.0, 0.0)
    hm_ref[HALO:HALO + rows, :] = (x_ref[0] * (1.0 + sc) + sh).astype(BF16)
    hm_ref[0:HALO, :] = ((xp_ref[0] * (1.0 + sc) + sh) * keep_prev).astype(BF16)
    hm_ref[HALO + rows:2 * HALO + rows, :] = ((xn_ref[0] * (1.0 + sc) + sh) * keep_next).astype(BF16)


def _halo_specs(rows, seq, d):
    nb = rows // HALO
    last = seq // HALO - 1
    return [
        pl.BlockSpec((1, rows, d), lambda b, t: (b, t, 0)),
        pl.BlockSpec((1, HALO, d), lambda b, t: (b, jnp.maximum(t * nb - 1, 0), 0)),
        pl.BlockSpec((1, HALO, d), lambda b, t: (b, jnp.minimum((t + 1) * nb, last), 0)),
    ]


def _row_spec(d):
    return pl.BlockSpec((1, 1, d), lambda b, t: (b, 0, 0))


def _mod_kernel(c_ref, w_ref, b_ref, o_ref):
    c = c_ref[...]
    act = c * jax.nn.sigmoid(c)
    o_ref[0] = jnp.dot(act, w_ref[0], preferred_element_type=F32,
                       precision=lax.Precision.HIGHEST) + b_ref[0]


def _modulation(cc, mod_w, mod_b):
    rows, d = cc.shape
    depth, _, n = mod_w.shape
    tn = MOD_COLS
    return pl.pallas_call(
        _mod_kernel,
        out_shape=jax.ShapeDtypeStruct((depth, rows, n), F32),
        grid=(depth, n // tn),
        in_specs=[
            pl.BlockSpec((rows, d), lambda i, j: (0, 0)),
            pl.BlockSpec((1, d, tn), lambda i, j: (i, 0, j)),
            pl.BlockSpec((1, 1, tn), lambda i, j: (i, 0, j)),
        ],
        out_specs=pl.BlockSpec((1, rows, tn), lambda i, j: (i, 0, j)),
        compiler_params=pltpu.CompilerParams(
            dimension_semantics=("arbitrary", "arbitrary"),
            vmem_limit_bytes=_vmem_limit(2 * d * tn * 4)),
        name="modulation",
    )(cc, mod_w, mod_b.reshape(depth, 1, n))


def _ab_in_kernel(*refs, use_rope, rows):
    (x_ref, sh_ref, sc_ref, w_ref, lng_ref, lnb_ref, ws_ref, bias_ref, qg_ref, kg_ref) = refs[:10]
    if use_rope:
        cos_ref, sin_ref = refs[10:12]
    a_ref, q_ref, k_ref, v_ref = refs[-4:]

    gd = GM_WIDTH // GM_GROUPS
    slab_rows = min(AB_SUB, rows)

    def slab(i, carry):
        r0 = pl.multiple_of(i * slab_rows, slab_rows)
        rs = pl.ds(r0, slab_rows)
        h = (x_ref[0, rs, :] * (1.0 + sc_ref[0]) + sh_ref[0]).astype(BF16)

        u = _gelu_tanh_of_half(_dot(h, w_ref[:, 0:GM_WIDTH]))
        vn = _layer_norm(_gelu_tanh_of_half(_dot(h, w_ref[:, GM_WIDTH:2 * GM_WIDTH])),
                         lng_ref[...], lnb_ref[...])
        vb = vn.astype(BF16)
        for c in range(slab_rows // CHUNK):
            r = slice(c * CHUNK, (c + 1) * CHUNK)
            for g in range(GM_GROUPS):
                l = slice(g * gd, (g + 1) * gd)
                mixed = _dot(ws_ref[g], vb[r, l]) + bias_ref[:, l]
                a_ref[0, pl.ds(r0 + c * CHUNK, CHUNK), l] = (u[r, l] * mixed).astype(BF16)

        def rope(z):
            if not use_rope:
                return z
            return z * cos_ref[rs, :] + pltpu.roll(z, HEAD_DIM // 2, 1) * sin_ref[rs, :]

        pq = _dot(h, w_ref[:, 2 * GM_WIDTH:KV_OFF])
        for hh in range(N_Q_HEADS):
            l = slice(hh * HEAD_DIM, (hh + 1) * HEAD_DIM)
            q_ref[0, rs, l] = rope(_rms_norm(pq[:, l], qg_ref[...])).astype(BF16)
        pk = _dot(h, w_ref[:, KV_OFF:KV_OFF + KV_W])
        for hh in range(N_KV_HEADS):
            l = slice(hh * HEAD_DIM, (hh + 1) * HEAD_DIM)
            k_ref[0, rs, l] = rope(_rms_norm(pk[:, l], kg_ref[...])).astype(BF16)
        v_ref[0, rs, :] = _dot(h, w_ref[:, KV_OFF + KV_W:KV_OFF + 2 * KV_W]).astype(BF16)
        return carry

    lax.fori_loop(0, rows // slab_rows, slab, 0)


def _ab_in(x, sh, sc, w, lng, lnb, ws, bias2d, qg, kg, rope_tabs, rows, kv_len, kv_row0, kv_into=None):
    bsz, seq, d = x.shape
    assert kv_row0 % rows == 0
    kv_blk0 = kv_row0 // rows
    n = w.shape[1]
    use_rope = rope_tabs is not None
    in_specs = [
        pl.BlockSpec((1, rows, d), lambda b, t: (b, t, 0)),
        _row_spec(d), _row_spec(d),
        _resident((d, n)),
        _resident((1, GM_WIDTH)), _resident((1, GM_WIDTH)),
        _resident((GM_GROUPS, CHUNK, CHUNK)),
        _resident((CHUNK, GM_WIDTH)),
        _resident((1, HEAD_DIM)), _resident((1, HEAD_DIM)),
    ]
    args = [x, sh, sc, w, lng, lnb, ws, bias2d, qg, kg]
    if use_rope:
        in_specs += [pl.BlockSpec((rows, HEAD_DIM), lambda b, t: (t, 0))] * 2
        args += list(rope_tabs)
    aliases = {}
    if kv_into is not None:
        aliases = {len(args): 2, len(args) + 1: 3}
        in_specs += [pl.BlockSpec(memory_space=pl.ANY)] * 2
        args += list(kv_into)
    out_shape = [
        jax.ShapeDtypeStruct((bsz, seq, GM_WIDTH), BF16),
        jax.ShapeDtypeStruct((bsz, seq, ATTN_WIDTH), BF16),
        jax.ShapeDtypeStruct((bsz, kv_len, KV_W), BF16),
        jax.ShapeDtypeStruct((bsz, kv_len, KV_W), BF16),
    ]
    out_specs = [
        pl.BlockSpec((1, rows, GM_WIDTH), lambda b, t: (b, t, 0)),
        pl.BlockSpec((1, rows, ATTN_WIDTH), lambda b, t: (b, t, 0)),
        pl.BlockSpec((1, rows, KV_W), lambda b, t: (b, t + kv_blk0, 0)),
        pl.BlockSpec((1, rows, KV_W), lambda b, t: (b, t + kv_blk0, 0)),
    ]
    est = d * n * 2 + 2 * rows * d * 4 + 6 * rows * n * 4
    return pl.pallas_call(
        functools.partial(_ab_in_kernel, use_rope=use_rope, rows=rows),
        out_shape=out_shape,
        grid=(bsz, seq // rows),
        in_specs=in_specs,
        out_specs=out_specs,
        input_output_aliases=aliases,
        compiler_params=pltpu.CompilerParams(
            dimension_semantics=("arbitrary", "arbitrary"),
            vmem_limit_bytes=_vmem_limit(est)),
        name="ab_in_rope" if use_rope else "ab_in_ctx",
    )(*args)


def _attn_kernel(q_ref, k_ref, v_ref, o_ref, vt_ref, *, rows, blk):
    @pl.when(pl.program_id(2) == 0)
    def _():
        vt_ref[...] = v_ref[0].astype(F32).T

    k = k_ref[0]
    vt = vt_ref[...].astype(BF16)
    for h in range(N_Q_HEADS // N_KV_HEADS):
        l_ = slice(h * HEAD_DIM, (h + 1) * HEAD_DIM)
        for r0 in range(0, rows, blk):
            st = _dot_nt(k, q_ref[0, r0:r0 + blk, l_])
            m = jnp.max(st, axis=0, keepdims=True)
            p = jnp.exp2(st - m)
            l = jnp.sum(p, axis=0, keepdims=True)
            ot = _dot(vt, p.astype(BF16)) / l
            o_ref[0, r0:r0 + blk, l_] = ot.T.astype(BF16)


def _attention(q, k, v, k_row0, sk, rows, blk=512):
    bsz, sq, _ = q.shape
    assert k_row0 % sk == 0
    kb = k_row0 // sk
    gw = (N_Q_HEADS // N_KV_HEADS) * HEAD_DIM
    est = 4 * blk * sk * 12 + 4 * sk * HEAD_DIM * 2 + 4 * rows * gw * 2 + sk * HEAD_DIM * 4
    return pl.pallas_call(
        functools.partial(_attn_kernel, rows=rows, blk=min(blk, rows)),
        out_shape=jax.ShapeDtypeStruct((bsz, sq, ATTN_WIDTH), BF16),
        grid=(bsz, N_KV_HEADS, sq // rows),
        in_specs=[
            pl.BlockSpec((1, rows, gw), lambda b, h, t: (b, t, h)),
            pl.BlockSpec((1, sk, HEAD_DIM), lambda b, h, t: (b, kb, h)),
            pl.BlockSpec((1, sk, HEAD_DIM), lambda b, h, t: (b, kb, h)),
        ],
        out_specs=pl.BlockSpec((1, rows, gw), lambda b, h, t: (b, t, h)),
        scratch_shapes=[pltpu.VMEM((HEAD_DIM, sk), F32)],
        compiler_params=pltpu.CompilerParams(
            dimension_semantics=("arbitrary", "arbitrary", "arbitrary"),
            vmem_limit_bytes=_vmem_limit(est)),
        name="attention",
    )(q, k, v)


def _out_ln_kernel(a1_ref, a2_ref, w_ref, x_ref, g_ref, lng_ref, lnb_ref, o_ref):
    half = a1_ref.shape[2]
    y = _dot(a1_ref[0], w_ref[0:half, :]) + _dot(a2_ref[0], w_ref[half:2 * half, :])
    o_ref[0] = _deepnorm(x_ref[0], g_ref[0], y, lng_ref[...], lnb_ref[...])


def _out_ln(a1, a2, w, x, gate, lng, lnb, rows):
    bsz, seq, d = x.shape
    half = a1.shape[2]
    est = 2 * half * d * 2 + 4 * rows * d * 4 + 4 * rows * half * 2 + 2 * rows * d * 4
    return pl.pallas_call(
        _out_ln_kernel,
        out_shape=jax.ShapeDtypeStruct((bsz, seq, d), F32),
        grid=(bsz, seq // rows),
        in_specs=[
            pl.BlockSpec((1, rows, half), lambda b, t: (b, t, 0)),
            pl.BlockSpec((1, rows, half), lambda b, t: (b, t, 0)),
            _resident((2 * half, d)),
            pl.BlockSpec((1, rows, d), lambda b, t: (b, t, 0)),
            _row_spec(d),
            _resident((1, d)), _resident((1, d)),
        ],
        out_specs=pl.BlockSpec((1, rows, d), lambda b, t: (b, t, 0)),
        compiler_params=pltpu.CompilerParams(
            dimension_semantics=("arbitrary", "arbitrary"),
            vmem_limit_bytes=_vmem_limit(est)),
        name="out_ln",
    )(a1, a2, w, x, gate, lng, lnb)


def _ffn_kernel(x_ref, xp_ref, xn_ref, sh_ref, sc_ref, g_ref, wu_ref, cw_ref, cb_ref, wd_ref, lng_ref, lnb_ref,
                o_ref, hm_ref, u_ref, gt_ref, *, rows):
    _fill_halo_tile(hm_ref, x_ref, xp_ref, xn_ref, sh_ref[0], sc_ref[0], rows)
    n_chunks = wd_ref.shape[0]

    def cols(ref, c):
        off = c * FF_CHUNK
        return ref[:, pl.ds(off if isinstance(off, int) else pl.multiple_of(off, FF_CHUNK), FF_CHUNK)]
    tiles = rows // V7X_SUBLANES
    sub = lax.broadcasted_iota(jnp.int32, (tiles, V7X_SUBLANES, FF_CHUNK), 1)

    def up_proj(c, slot):
        hm = hm_ref[...]
        u_ref[slot, 0] = _dot(hm, cols(wu_ref, c)).reshape(tiles + 2, V7X_SUBLANES, FF_CHUNK)
        u_ref[slot, 1] = _dot(hm, cols(wu_ref, c + n_chunks)).reshape(tiles + 2, V7X_SUBLANES, FF_CHUNK)

    def conv_gate(c, slot):
        def conv(half):
            u = u_ref[slot, half]
            dn = pltpu.roll(u, 1, 1)
            up = pltpu.roll(u, V7X_SUBLANES - 1, 1)
            prev = jnp.where(sub == 0, dn[0:tiles], dn[1:tiles + 1])
            nxt = jnp.where(sub == V7X_SUBLANES - 1, up[2:tiles + 2], up[1:tiles + 1])
            w = cols(cw_ref, c + half * n_chunks)
            return cols(cb_ref, c + half * n_chunks) + w[0:1] * prev + w[1:2] * u[1:tiles + 1] + w[2:3] * nxt

        a1 = conv(0)
        a2 = conv(1)
        gt_ref[c] = (a1 * (a2 + a2 * jnp.tanh(a2))).reshape(rows, FF_CHUNK)

    up_proj(0, 0)

    def pair(k, carry):
        c = 2 * k
        up_proj(c + 1, 1)
        conv_gate(c, 0)
        up_proj(c + 2, 0)
        conv_gate(c + 1, 1)
        return carry

    lax.fori_loop(0, (n_chunks - 1) // 2, pair, 0)
    conv_gate(n_chunks - 1, 0)

    f = _dot(gt_ref[0].astype(BF16), wd_ref[0])
    for c in range(1, n_chunks):
        f = f + _dot(gt_ref[c].astype(BF16), wd_ref[c])
    o_ref[0] = _deepnorm(x_ref[0], g_ref[0], f, lng_ref[...], lnb_ref[...])


def _ffn(x, sh, sc, gate, wu, cw, cb, wd, layer, lng, lnb, rows):
    bsz, seq, d = x.shape
    nc = wd.shape[1]
    dff = nc * FF_CHUNK
    assert wu.shape[2] == 2 * dff and nc % 2 == 1, "the pipeline peels one chunk and pairs the rest"

    def of_layer(*shape):
        nd = len(shape)
        return pl.BlockSpec((None,) + shape, lambda b, t: (layer,) + (0,) * nd, pipeline_mode=pl.Buffered(1))

    est = (3 * nc * d * FF_CHUNK * 2 + 4 * rows * d * 4 + (rows + 2 * HALO) * d * 2
           + nc * rows * FF_CHUNK * 4 + 8 * (rows + 2 * HALO) * FF_CHUNK * 4 + 2 * rows * d * 4)
    return pl.pallas_call(
        functools.partial(_ffn_kernel, rows=rows),
        out_shape=jax.ShapeDtypeStruct((bsz, seq, d), F32),
        grid=(bsz, seq // rows),
        in_specs=_halo_specs(rows, seq, d) + [
            _row_spec(d), _row_spec(d), _row_spec(d),
            of_layer(d, 2 * dff), of_layer(CONV_W, 2 * dff), of_layer(1, 2 * dff),
            of_layer(nc, FF_CHUNK, d),
            _resident((1, d)), _resident((1, d)),
        ],
        out_specs=pl.BlockSpec((1, rows, d), lambda b, t: (b, t, 0)),
        scratch_shapes=[
            pltpu.VMEM((rows + 2 * HALO, d), BF16),
            pltpu.VMEM((2, 2, rows // HALO + 2, HALO, FF_CHUNK), F32),
            pltpu.VMEM((nc, rows, FF_CHUNK), F32),
        ],
        compiler_params=pltpu.CompilerParams(
            dimension_semantics=("arbitrary", "arbitrary"),
            vmem_limit_bytes=_vmem_limit(est)),
        name="conv_ffn",
    )(x, x, x, sh, sc, gate, wu, cw, cb, wd, lng, lnb)


def _cd_in_kernel(*refs, with_fz, rows):
    x_ref, xp_ref, xn_ref, sh_ref, sc_ref = refs[:5]
    if with_fz:
        wfz_ref, wx_ref, wdt_ref, cw_ref, cb_ref, dtb_ref = refs[5:11]
        f_ref, z_ref, xs_ref, b_ref, c_ref, dt_ref, hm_ref = refs[11:]
    else:
        wx_ref, wdt_ref, cw_ref, cb_ref, dtb_ref = refs[5:10]
        xs_ref, b_ref, c_ref, dt_ref, hm_ref = refs[10:]
    _fill_halo_tile(hm_ref, x_ref, xp_ref, xn_ref, sh_ref[0], sc_ref[0], rows)
    hc = hm_ref[HALO:HALO + rows, :]
    if with_fz:
        pfz = _dot(hc, wfz_ref[...])
        f_ref[0] = pfz[:, 0:FN_WIDTH].astype(BF16)
        z_ref[0] = pfz[:, FN_WIDTH:SSM_OFF]
    a = _conv3(_dot(hm_ref[...], wx_ref[...]), cw_ref[...], cb_ref[...], rows)
    s = a + a * jnp.tanh(a)
    xs_ref[0] = s[:, 0:SSD_WIDTH]
    b_ref[0] = s[:, SSD_WIDTH:SSD_WIDTH + SSD_GN].astype(BF16)
    c_ref[0] = s[:, SSD_WIDTH + SSD_GN:SSD_CONV_DIM].astype(BF16)
    dt_ref[0] = _softplus(_dot(hc, wdt_ref[...]) + dtb_ref[...])


def _cd_in(x, sh, sc, wfz, wx, wdt, cw, cb, dtb, rows):
    bsz, seq, d = x.shape
    with_fz = wfz is not None
    in_specs = _halo_specs(rows, seq, d) + [_row_spec(d), _row_spec(d)]
    args = [x, x, x, sh, sc]
    out_shape, out_specs = [], []
    if with_fz:
        in_specs.append(_resident((d, SSM_OFF)))
        args.append(wfz)
        out_shape += [jax.ShapeDtypeStruct((bsz, seq, FN_WIDTH), BF16),
                      jax.ShapeDtypeStruct((bsz, seq, SSD_WIDTH), F32)]
        out_specs += [pl.BlockSpec((1, rows, FN_WIDTH), lambda b, t: (b, t, 0)),
                      pl.BlockSpec((1, rows, SSD_WIDTH), lambda b, t: (b, t, 0))]
    in_specs += [_resident((d, SSD_CONV_DIM)), _resident((d, V7X_LANES)),
                 _resident((CONV_W, SSD_CONV_DIM)), _resident((1, SSD_CONV_DIM)),
                 _resident((1, V7X_LANES))]
    args += [wx, wdt, cw, cb, dtb]
    out_shape += [jax.ShapeDtypeStruct((bsz, seq, SSD_WIDTH), F32),
                  jax.ShapeDtypeStruct((bsz, seq, SSD_GN), BF16),
                  jax.ShapeDtypeStruct((bsz, seq, SSD_GN), BF16),
                  jax.ShapeDtypeStruct((bsz, seq, V7X_LANES), F32)]
    out_specs += [pl.BlockSpec((1, rows, SSD_WIDTH), lambda b, t: (b, t, 0)),
                  pl.BlockSpec((1, rows, SSD_GN), lambda b, t: (b, t, 0)),
                  pl.BlockSpec((1, rows, SSD_GN), lambda b, t: (b, t, 0)),
                  pl.BlockSpec((1, rows, V7X_LANES), lambda b, t: (b, t, 0))]
    est = d * 2200 * 2 + 2 * rows * d * 4 + 8 * (rows + 2 * HALO) * SSD_CONV_DIM * 4
    return pl.pallas_call(
        functools.partial(_cd_in_kernel, with_fz=with_fz, rows=rows),
        out_shape=out_shape,
        grid=(bsz, seq // rows),
        in_specs=in_specs,
        out_specs=out_specs,
        scratch_shapes=[pltpu.VMEM((rows + 2 * HALO, d), BF16)],
        compiler_params=pltpu.CompilerParams(
            dimension_semantics=("arbitrary", "arbitrary"),
            vmem_limit_bytes=_vmem_limit(est)),
        name="cd_in" if with_fz else "cd_in_ctx",
    )(*args)


def _cumsum_rows(z, reverse):
    n = z.shape[0]
    idx = lax.broadcasted_iota(jnp.int32, z.shape, 0)
    k = 1
    while k < n:
        if reverse:
            z = z + jnp.where(idx < n - k, pltpu.roll(z, n - k, 0), 0.0)
        else:
            z = z + jnp.where(idx >= k, pltpu.roll(z, k, 0), 0.0)
        k *= 2
    return z


def _lane_expand(row, lane0, width):
    lane = lax.broadcasted_iota(jnp.int32, (1, SSD_HPG * width), 1)
    out = jnp.zeros((1, SSD_HPG * width), F32)
    for j in range(SSD_HPG):
        val = jnp.broadcast_to(row[:, lane0 + j:lane0 + j + 1], (1, SSD_HPG * width))
        out = jnp.where(lane // width == j, val, out)
    return out


def _ssd_chunk(x, bm, cm, dt, a_row, h_ref, lane0, reverse, need_y):
    q = x.shape[0]
    gw = SSD_HPG * SSD_HEADDIM
    acum = _cumsum_rows(dt * a_row, reverse)
    total = acum[0:1, :] if reverse else acum[q - 1:q, :]
    acum_t = acum.T
    dt_t = dt.T
    total_t = jnp.broadcast_to(acum_t[:, 0:1] if reverse else acum_t[:, q - 1:q], acum_t.shape)
    wgt_t = dt_t * jnp.exp2(total_t - acum_t)
    lane_g = lax.broadcasted_iota(jnp.int32, (q, gw), 1) // SSD_HEADDIM
    lane_gh = lax.broadcasted_iota(jnp.int32, (SSD_STATE, gw), 1) // SSD_HEADDIM
    if need_y:
        srow_t = acum_t - jnp.log2(dt_t)
        ti = lax.broadcasted_iota(jnp.int32, (q, q), 0)
        si = lax.broadcasted_iota(jnp.int32, (q, q), 1)
        mask = (si >= ti) if reverse else (si <= ti)
    ys = []
    for g in range(SSD_GROUPS):
        bg = bm[:, g * SSD_STATE:(g + 1) * SSD_STATE]
        xgb = x[:, g * gw:(g + 1) * gw].astype(BF16)
        x_heads = [jnp.where(lane_g == j, xgb, jnp.zeros_like(xgb)) for j in range(SSD_HPG)]
        hg = h_ref[g]
        if need_y:
            cg = cm[:, g * SSD_STATE:(g + 1) * SSD_STATE]
            cb = _dot_nt(cg, bg)
            cgf = cg.astype(F32)
            hgb = hg.astype(BF16)
            lhs, rhs = [], []
            for j in range(SSD_HPG):
                jl = lane0 + g * SSD_HPG + j
                col = jnp.broadcast_to(acum[:, jl:jl + 1], (q, q))
                row = jnp.broadcast_to(srow_t[jl:jl + 1, :], (q, q))
                lhs.append((cb * jnp.exp2(jnp.where(mask, col - row, -jnp.inf))).astype(BF16))
                lhs.append((cgf * jnp.exp2(col[:, 0:SSD_STATE])).astype(BF16))
                rhs.append(x_heads[j])
                rhs.append(jnp.where(lane_gh == j, hgb, jnp.zeros_like(hgb)))
            ys.append(_dot(jnp.concatenate(lhs, axis=1), jnp.concatenate(rhs, axis=0)))
        bgt = bg.astype(F32).T
        wb = [(bgt * jnp.broadcast_to(wgt_t[lane0 + g * SSD_HPG + j:lane0 + g * SSD_HPG + j + 1, :],
                                      (SSD_STATE, q))).astype(BF16) for j in range(SSD_HPG)]
        st = _dot(jnp.concatenate(wb, axis=1), jnp.concatenate(x_heads, axis=0))
        h_ref[g] = hg * jnp.exp2(_lane_expand(total, lane0 + g * SSD_HPG, SSD_HEADDIM)) + st
    if need_y:
        return jnp.concatenate(ys, axis=1)
    return None


def _ssd_kernel(xc_ref, bc_ref, dtc_ref, xl_ref, bl_ref, cl_ref, dtl_ref, z_ref, alog_ref,
                dskip_ref, ng_ref, o_ref, y_ref, h_ref):
    a_row = -jnp.exp(alog_ref[...]) * math.log2(math.e)
    nc_ctx = xc_ref.shape[1] // SSD_CHUNK
    nc_lat = xl_ref.shape[1] // SSD_CHUNK

    def rows_of(c):
        return pl.ds(pl.multiple_of(c * SSD_CHUNK, SSD_CHUNK), SSD_CHUNK)

    def ctx_step(c, direction):
        r = rows_of(c)
        _ssd_chunk(xc_ref[0, r, :], bc_ref[0, r, :], None, dtc_ref[0, r, :], a_row,
                   h_ref.at[direction], direction * SSD_HEADS, direction == 1, False)

    def lat_step(c, direction):
        r = rows_of(c)
        y_ref[direction, r, :] = _ssd_chunk(
            xl_ref[0, r, :], bl_ref[0, r, :], cl_ref[0, r, :], dtl_ref[0, r, :], a_row,
            h_ref.at[direction], direction * SSD_HEADS, direction == 1, True)

    h_ref[...] = jnp.zeros_like(h_ref)

    def ctx_pair(i, carry):
        ctx_step(i, 0)
        ctx_step(nc_ctx - 1 - i, 1)
        return carry

    lax.fori_loop(0, nc_ctx, ctx_pair, 0)

    def lat_pair(i, carry):
        lat_step(i, 0)
        lat_step(nc_lat - 1 - i, 1)
        return carry

    lax.fori_loop(0, nc_lat, lat_pair, 0)

    def finish(c, carry):
        r = rows_of(c)
        zz = z_ref[0, r, :]
        yt = (y_ref[0, r, :] + y_ref[1, r, :] + xl_ref[0, r, :] * dskip_ref[...]) * _silu(zz)
        o_ref[0, r, :] = _rms_norm(yt, ng_ref[...]).astype(BF16)
        return carry

    lax.fori_loop(0, nc_lat, finish, 0)


def _ssd(xs_c, b_c, dt_c, xs_l, b_l, c_l, dt_l, z, alog_row, dskip_row, ng_row):
    bsz, sl, w = xs_l.shape
    sc = xs_c.shape[1]
    assert sl % SSD_CHUNK == 0 and sc % SSD_CHUNK == 0
    gw = SSD_HPG * SSD_HEADDIM

    def full(s, n):
        return pl.BlockSpec((1, s, n), lambda b: (b, 0, 0))

    est = (2 * (sl + sc) * (w * 4 + 2 * SSD_GN * 2 + V7X_LANES * 4) + 2 * sl * w * 4
           + 2 * sl * w * 2 + 2 * sl * w * 4)
    return pl.pallas_call(
        _ssd_kernel,
        out_shape=jax.ShapeDtypeStruct((bsz, sl, w), BF16),
        grid=(bsz,),
        in_specs=[
            full(sc, w), full(sc, SSD_GN), full(sc, V7X_LANES),
            full(sl, w), full(sl, SSD_GN), full(sl, SSD_GN), full(sl, V7X_LANES),
            full(sl, w),
            _resident((1, V7X_LANES)), _resident((1, w)), _resident((1, w)),
        ],
        out_specs=full(sl, w),
        scratch_shapes=[
            pltpu.VMEM((2, sl, w), F32),
            pltpu.VMEM((2, SSD_GROUPS, SSD_STATE, gw), F32),
        ],
        compiler_params=pltpu.CompilerParams(
            dimension_semantics=("arbitrary",),
            vmem_limit_bytes=_vmem_limit(est)),
        name="ssd_scan",
    )(xs_c, b_c, dt_c, xs_l, b_l, c_l, dt_l, z, alog_row, dskip_row, ng_row)


def _fourier_kernel(f_ref, cc_ref, sc_ref, w2_ref, o_ref, y_ref, *, rows):
    seq = f_ref.shape[1]
    for g in range(FN_GROUPS):
        l = slice(g * FN_GDIM, (g + 1) * FN_GDIM)
        fg = f_ref[0, :, l]
        y_ref[0:seq, l] = _dot(fg, cc_ref[...]).astype(BF16)
        y_ref[seq:2 * seq, l] = _dot(fg, sc_ref[...]).astype(BF16)
    norm = 1.0 / math.sqrt(seq * FN_GDIM)

    def tile(i, carry):
        r = pl.ds(pl.multiple_of(i * rows, rows), rows)
        o_ref[0, r, :] = (_dot(w2_ref[r, :], y_ref[...]) * norm).astype(BF16)
        return carry

    lax.fori_loop(0, seq // rows, tile, 0)


def _fourier(f, cc, sc, w2, rows):
    bsz, seq, w = f.shape
    est = seq * 2 * seq * 2 + 4 * seq * w * 2 + 2 * seq * w * 2 + 4 * rows * 2 * seq * 2
    return pl.pallas_call(
        functools.partial(_fourier_kernel, rows=rows),
        out_shape=jax.ShapeDtypeStruct((bsz, seq, w), BF16),
        grid=(bsz,),
        in_specs=[
            pl.BlockSpec((1, seq, w), lambda b: (b, 0, 0)),
            _resident((FN_GDIM, FN_GDIM)), _resident((FN_GDIM, FN_GDIM)),
            _resident((seq, 2 * seq)),
        ],
        out_specs=pl.BlockSpec((1, seq, w), lambda b: (b, 0, 0)),
        scratch_shapes=[pltpu.VMEM((2 * seq, w), BF16)],
        compiler_params=pltpu.CompilerParams(
            dimension_semantics=("arbitrary",),
            vmem_limit_bytes=_vmem_limit(est)),
        name="fourier_mix",
    )(f, cc, sc, w2)


def _rope_tables(seq):
    rows = seq // GRID_W
    row = np.repeat(np.arange(rows, dtype=np.float32), GRID_W)
    col = np.tile(np.arange(GRID_W, dtype=np.float32), rows)
    n_freq = HEAD_DIM // 4
    inv = np.power(np.float32(ROPE_THETA), -np.arange(n_freq, dtype=np.float32) / np.float32(n_freq))
    ang = (np.stack([row, col], axis=-1)[:, :, None] * inv.astype(np.float32)).astype(np.float64)
    cos, sin = np.cos(ang).astype(np.float32), np.sin(ang).astype(np.float32)
    cos_t = np.concatenate([cos[:, 0], cos[:, 1], cos[:, 0], cos[:, 1]], axis=-1)
    sin_t = np.concatenate([-sin[:, 0], -sin[:, 1], sin[:, 0], sin[:, 1]], axis=-1)
    return jnp.asarray(cos_t), jnp.asarray(sin_t)


def _rope_head_order(m):
    lead = m.shape[:-1]
    nh = m.shape[-1] // HEAD_DIM
    m = m.reshape(lead + (nh, 2, 2, HEAD_DIM // 4))
    return jnp.swapaxes(m, -3, -2).reshape(lead + (nh * HEAD_DIM,))


def _dft_tables(seq):
    def tw(n):
        i = np.arange(n, dtype=np.int64)
        ang = ((i[:, None] * i[None, :]) % n).astype(np.float64) * (2.0 * math.pi / n)
        return np.cos(ang), np.sin(ang)

    cc, sc = tw(FN_GDIM)
    cl, sl = tw(seq)
    w2 = np.concatenate([cl, -sl], axis=1).astype(np.float32)
    return (jnp.asarray(cc.astype(np.float32), BF16), jnp.asarray(sc.astype(np.float32), BF16),
            jnp.asarray(w2, BF16))


def _ffn_params(w_up, conv_w, conv_b, w_down):
    depth, d, _ = w_up.shape
    half_gate = jnp.concatenate([jnp.ones((D_FF,), F32), jnp.full((D_FF,), 0.5, F32)])
    return (w_up.astype(BF16), conv_w * half_gate, (conv_b * half_gate).reshape(depth, 1, 2 * D_FF),
            w_down.astype(BF16).reshape(depth, D_FF // FF_CHUNK, FF_CHUNK, d))


def kernel(x, c, ctx, c_ctx, mod_w, mod_b, ln1_g, ln1_b, ln2_g, ln2_b, ffn_up, ffn_conv_w, ffn_conv_b,
           ffn_down, ab_w_in, ab_w_out, gm_ln_g, gm_ln_b, gm_ws, gm_bs, q_norm_g, k_norm_g, cd_w_in,
           cd_w_out, ssd_conv_w, ssd_conv_b, ssd_dt_bias, ssd_a_log, ssd_d, ssd_norm_g):
    bsz, seq, d = x.shape
    ctx_len = ctx.shape[1]
    lat_rows = min(1024, seq)
    out_rows = min(2048, seq)
    dft_rows = min(2048, seq)
    ctx_rows = min(256, ctx_len)

    pad = (-(bsz + 1)) % V7X_SUBLANES
    cc = jnp.concatenate([c, c_ctx[None], jnp.zeros((pad, d), F32)], axis=0)
    mod = _modulation(cc, mod_w, mod_b)
    rope_tabs = _rope_tables(seq)

    def row(v):
        return v.reshape(1, -1)

    ffn_p = _ffn_params(ffn_up, ffn_conv_w, ffn_conv_b, ffn_down)
    xl, xc = x, ctx
    for i in range(DEPTH):
        last = i == DEPTH - 1
        j = i // 2
        ml = mod[i, :bsz].reshape(bsz, 6, 1, d)
        mc = jnp.broadcast_to(mod[i, bsz].reshape(1, 6, 1, d), (bsz, 6, 1, d))
        sh1, sc1, g1, sh2, sc2, g2 = (ml[:, k] for k in range(6))
        csh1, csc1, cg1, csh2, csc2, cg2 = (mc[:, k] for k in range(6))
        ln1 = (row(ln1_g[i]), row(ln1_b[i]))
        ln2 = (row(ln2_g[i]), row(ln2_b[i]))

        if i % 2 == 0:
            w_in = ab_w_in[j].astype(BF16)
            qk_lo, qk_hi = 2 * GM_WIDTH, KV_OFF + KV_W
            q_scale = HEAD_DIM ** -0.5 * math.log2(math.e)
            w_in = jnp.concatenate([0.5 * w_in[:, :qk_lo], _rope_head_order(w_in[:, qk_lo:qk_hi]),
                                    w_in[:, qk_hi:]], axis=1)
            w_out = ab_w_out[j].astype(BF16)
            gm = (row(gm_ln_g[j]), row(gm_ln_b[j]), gm_ws[j].astype(BF16),
                  jnp.repeat(gm_bs[j].T, GM_WIDTH // GM_GROUPS, axis=1),
                  row(_rope_head_order(q_norm_g[j]) * q_scale), row(_rope_head_order(k_norm_g[j])))
            kv_len = seq + ctx_len
            a_c, q_c, k_all, v_all = _ab_in(xc, csh1, csc1, w_in, *gm, None, ctx_rows, kv_len, seq)
            a_l, q_l, k_all, v_all = _ab_in(xl, sh1, sc1, w_in, *gm, rope_tabs, lat_rows, kv_len, 0,
                                            kv_into=(k_all, v_all))
            mix_l = (a_l, _attention(q_l, k_all, v_all, 0, kv_len, lat_rows))
            if not last:
                mix_c = (a_c, _attention(q_c, k_all, v_all, seq, ctx_len, ctx_rows))
        else:
            w_in = cd_w_in[j].astype(BF16)
            w_out = cd_w_out[j].astype(BF16)
            wfz = w_in[:, :SSM_OFF]
            wx = w_in[:, SSM_OFF:SSM_OFF + SSD_CONV_DIM]
            ndt = 2 * SSD_HEADS
            wdt = jnp.pad(w_in[:, SSM_OFF + SSD_CONV_DIM:], ((0, 0), (0, V7X_LANES - ndt)))
            dtb = jnp.pad(ssd_dt_bias[j].reshape(1, ndt), ((0, 0), (0, V7X_LANES - ndt)))
            alog = jnp.pad(ssd_a_log[j].reshape(1, ndt), ((0, 0), (0, V7X_LANES - ndt)))
            cw, cb = 0.5 * ssd_conv_w[j], 0.5 * row(ssd_conv_b[j])
            dskip = row(jnp.repeat(ssd_d[j], SSD_HEADDIM))
            f_l, z_l, xs_l, b_l, c_l, dt_l = _cd_in(xl, sh1, sc1, wfz, wx, wdt, cw, cb, dtb, lat_rows)
            if last:
                xs_c, b_c, _, dt_c = _cd_in(xc, csh1, csc1, None, wx, wdt, cw, cb, dtb, ctx_rows)
            else:
                raise NotImplementedError("odd layers with a context output are not part of this block")
            s_l = _ssd(xs_c, b_c, dt_c, xs_l, b_l, c_l, dt_l, z_l, alog, dskip, row(ssd_norm_g[j]))
            mix_l = (_fourier(f_l, *_dft_tables(seq), dft_rows), s_l)

        xl = _out_ln(*mix_l, w_out, xl, g1, *ln1, out_rows)
        xl = _ffn(xl, sh2, sc2, g2, *ffn_p, i, *ln2, lat_rows)
        if not last:
            xc = _out_ln(*mix_c, w_out, xc, cg1, *ln1, ctx_rows)
            xc = _ffn(xc, csh2, csc2, cg2, *ffn_p, i, *ln2, ctx_rows)
    return xl
```

```python
import functools
import math

import jax
import jax.numpy as jnp
import numpy as np
from jax import lax
from jax.experimental import pallas as pl
from jax.experimental.pallas import tpu as pltpu

D_MODEL = 1024
DEPTH = 2
GRID_W = 64
CHUNK = 128
GM_GROUPS = 4
GM_WIDTH = 512
HEAD_DIM = 128
N_Q_HEADS = 4
N_KV_HEADS = 2
ATTN_WIDTH = 512
KV_W = 256
ROPE_THETA = 10000.0
FN_GROUPS = 4
FN_GDIM = 128
FN_WIDTH = 512
SSD_HEADDIM = 64
SSD_HEADS = 8
SSD_GROUPS = 2
SSD_HPG = 4
SSD_STATE = 128
SSD_WIDTH = 512
SSD_GN = 256
SSD_CONV_DIM = 1024
D_FF = 2816
CONV_W = 3
KV_OFF = 2 * GM_WIDTH + ATTN_WIDTH
SSM_OFF = FN_WIDTH + SSD_WIDTH
DN_ALPHA = (2 * DEPTH) ** 0.25
EPS = 1e-6

V7X_LANES = 128
V7X_SUBLANES = 8
V7X_VMEM_BYTES = 64 * 1024 * 1024
VMEM_CHIP_HEADROOM = 3 << 20
VMEM_TEMP_ALLOWANCE = 8 << 20
MOD_COLS = 2048
AB_SUB = 256
FF_CHUNK = 256
SSD_CHUNK = 256

BF16 = jnp.bfloat16
F32 = jnp.float32
HALO = V7X_SUBLANES


def _vmem_limit(nbytes):
    return int(min(nbytes * 3 // 2 + VMEM_TEMP_ALLOWANCE, V7X_VMEM_BYTES - VMEM_CHIP_HEADROOM))


def _resident(shape):
    nd = len(shape)
    return pl.BlockSpec(shape, lambda *_: (0,) * nd, pipeline_mode=pl.Buffered(1))


def _dot(a, b):
    return jnp.dot(a, b, preferred_element_type=F32)


def _dot_nt(a, b):
    return lax.dot_general(a, b, (((1,), (1,)), ((), ())), preferred_element_type=F32)


def _layer_norm(z, g, b, eps=EPS):
    mu = jnp.mean(z, axis=-1, keepdims=True)
    zc = z - mu
    var = jnp.mean(zc * zc, axis=-1, keepdims=True)
    return zc * lax.rsqrt(var + eps) * g + b


def _deepnorm(x, gate, y, g, b):
    return _layer_norm(x + (gate * (1.0 / DN_ALPHA)) * y, g, b, eps=EPS / DN_ALPHA ** 2)


def _rms_norm(z, g):
    return z * lax.rsqrt(jnp.mean(z * z, axis=-1, keepdims=True) + EPS) * g


def _softplus(z):
    return jnp.maximum(z, 0.0) + jnp.log1p(jnp.exp(-jnp.abs(z)))


def _gelu_tanh_of_half(h):
    c1 = math.sqrt(2.0 / math.pi)
    return h + h * jnp.tanh(h * (2.0 * c1 + (8.0 * c1 * 0.044715) * (h * h)))


def _silu(z):
    h = 0.5 * z
    return h + h * jnp.tanh(h)


def _conv3(u, w, b, rows):
    tiles, c = rows // V7X_SUBLANES, u.shape[1]
    u3 = u.reshape(tiles + 2, V7X_SUBLANES, c)
    sub = lax.broadcasted_iota(jnp.int32, (tiles, V7X_SUBLANES, c), 1)
    dn = pltpu.roll(u3, 1, 1)
    up = pltpu.roll(u3, V7X_SUBLANES - 1, 1)
    prev = jnp.where(sub == 0, dn[0:tiles], dn[1:tiles + 1])
    nxt = jnp.where(sub == V7X_SUBLANES - 1, up[2:tiles + 2], up[1:tiles + 1])
    return (b + w[0:1] * prev + w[1:2] * u3[1:tiles + 1] + w[2:3] * nxt).reshape(rows, c)


def _fill_halo_tile(hm_ref, x_ref, xp_ref, xn_ref, sh, sc, rows):
    t = pl.program_id(1)
    nt = pl.num_programs(1)
    keep_prev = jnp.where(t > 0, 1.0, 0.0)
    keep_next = jnp.where(t < nt - 1, 1.0, 0.0)
    hm_ref[HALO:HALO + rows, :] = (x_ref[0] * (1.0 + sc) + sh).astype(BF16)
    hm_ref[0:HALO, :] = ((xp_ref[0] * (1.0 + sc) + sh) * keep_prev).astype(BF16)
    hm_ref[HALO + rows:2 * HALO + rows, :] = ((xn_ref[0] * (1.0 + sc) + sh) * keep_next).astype(BF16)


def _halo_specs(rows, seq, d):
    nb = rows // HALO
    last = seq // HALO - 1
    return [
        pl.BlockSpec((1, rows, d), lambda b, t: (b, t, 0)),
        pl.BlockSpec((1, HALO, d), lambda b, t: (b, jnp.maximum(t * nb - 1, 0), 0)),
        pl.BlockSpec((1, HALO, d), lambda b, t: (b, jnp.minimum((t + 1) * nb, last), 0)),
    ]


def _row_spec(d):
    return pl.BlockSpec((1, 1, d), lambda b, t: (b, 0, 0))


def _mod_kernel(c_ref, w_ref, b_ref, o_ref):
    c = c_ref[...]
    act = c * jax.nn.sigmoid(c)
    o_ref[0] = jnp.dot(act, w_ref[0], preferred_element_type=F32,
                       precision=lax.Precision.HIGHEST) + b_ref[0]


def _modulation(cc, mod_w, mod_b):
    rows, d = cc.shape
    depth, _, n = mod_w.shape
    tn = MOD_COLS
    return pl.pallas_call(
        _mod_kernel,
        out_shape=jax.ShapeDtypeStruct((depth, rows, n), F32),
        grid=(depth, n // tn),
        in_specs=[
            pl.BlockSpec((rows, d), lambda i, j: (0, 0)),
            pl.BlockSpec((1, d, tn), lambda i, j: (i, 0, j)),
            pl.BlockSpec((1, 1, tn), lambda i, j: (i, 0, j)),
        ],
        out_specs=pl.BlockSpec((1, rows, tn), lambda i, j: (i, 0, j)),
        compiler_params=pltpu.CompilerParams(
            dimension_semantics=("arbitrary", "arbitrary"),
            vmem_limit_bytes=_vmem_limit(2 * d * tn * 4)),
        name="modulation",
    )(cc, mod_w, mod_b.reshape(depth, 1, n))


def _ab_in_kernel(*refs, use_rope, rows):
    (x_ref, sh_ref, sc_ref, w_ref, lng_ref, lnb_ref, ws_ref, bias_ref, qg_ref, kg_ref) = refs[:10]
    if use_rope:
        cos_ref, sin_ref = refs[10:12]
    a_ref, q_ref, k_ref, v_ref = refs[-4:]

    gd = GM_WIDTH // GM_GROUPS
    slab_rows = min(AB_SUB, rows)

    def slab(i, carry):
        r0 = pl.multiple_of(i * slab_rows, slab_rows)
        rs = pl.ds(r0, slab_rows)
        h = (x_ref[0, rs, :] * (1.0 + sc_ref[0]) + sh_ref[0]).astype(BF16)

        u = _gelu_tanh_of_half(_dot(h, w_ref[:, 0:GM_WIDTH]))
        vn = _layer_norm(_gelu_tanh_of_half(_dot(h, w_ref[:, GM_WIDTH:2 * GM_WIDTH])),
                         lng_ref[...], lnb_ref[...])
        vb = vn.astype(BF16)
        for c in range(slab_rows // CHUNK):
            r = slice(c * CHUNK, (c + 1) * CHUNK)
            for g in range(GM_GROUPS):
                l = slice(g * gd, (g + 1) * gd)
                mixed = _dot(ws_ref[g], vb[r, l]) + bias_ref[:, l]
                a_ref[0, pl.ds(r0 + c * CHUNK, CHUNK), l] = (u[r, l] * mixed).astype(BF16)

        def rope(z):
            if not use_rope:
                return z
            return z * cos_ref[rs, :] + pltpu.roll(z, HEAD_DIM // 2, 1) * sin_ref[rs, :]

        pq = _dot(h, w_ref[:, 2 * GM_WIDTH:KV_OFF])
        for hh in range(N_Q_HEADS):
            l = slice(hh * HEAD_DIM, (hh + 1) * HEAD_DIM)
            q_ref[0, rs, l] = rope(_rms_norm(pq[:, l], qg_ref[...])).astype(BF16)
        pk = _dot(h, w_ref[:, KV_OFF:KV_OFF + KV_W])
        for hh in range(N_KV_HEADS):
            l = slice(hh * HEAD_DIM, (hh + 1) * HEAD_DIM)
            k_ref[0, rs, l] = rope(_rms_norm(pk[:, l], kg_ref[...])).astype(BF16)
        v_ref[0, rs, :] = _dot(h, w_ref[:, KV_OFF + KV_W:KV_OFF + 2 * KV_W]).astype(BF16)
        return carry

    lax.fori_loop(0, rows // slab_rows, slab, 0)


def _ab_in(x, sh, sc, w, lng, lnb, ws, bias2d, qg, kg, rope_tabs, rows, kv_len, kv_row0, kv_into=None):
    bsz, seq, d = x.shape
    assert kv_row0 % rows == 0
    kv_blk0 = kv_row0 // rows
    n = w.shape[1]
    use_rope = rope_tabs is not None
    in_specs = [
        pl.BlockSpec((1, rows, d), lambda b, t: (b, t, 0)),
        _row_spec(d), _row_spec(d),
        _resident((d, n)),
        _resident((1, GM_WIDTH)), _resident((1, GM_WIDTH)),
        _resident((GM_GROUPS, CHUNK, CHUNK)),
        _resident((CHUNK, GM_WIDTH)),
        _resident((1, HEAD_DIM)), _resident((1, HEAD_DIM)),
    ]
    args = [x, sh, sc, w, lng, lnb, ws, bias2d, qg, kg]
    if use_rope:
        in_specs += [pl.BlockSpec((rows, HEAD_DIM), lambda b, t: (t, 0))] * 2
        args += list(rope_tabs)
    aliases = {}
    if kv_into is not None:
        aliases = {len(args): 2, len(args) + 1: 3}
        in_specs += [pl.BlockSpec(memory_space=pl.ANY)] * 2
        args += list(kv_into)
    out_shape = [
        jax.ShapeDtypeStruct((bsz, seq, GM_WIDTH), BF16),
        jax.ShapeDtypeStruct((bsz, seq, ATTN_WIDTH), BF16),
        jax.ShapeDtypeStruct((bsz, kv_len, KV_W), BF16),
        jax.ShapeDtypeStruct((bsz, kv_len, KV_W), BF16),
    ]
    out_specs = [
        pl.BlockSpec((1, rows, GM_WIDTH), lambda b, t: (b, t, 0)),
        pl.BlockSpec((1, rows, ATTN_WIDTH), lambda b, t: (b, t, 0)),
        pl.BlockSpec((1, rows, KV_W), lambda b, t: (b, t + kv_blk0, 0)),
        pl.BlockSpec((1, rows, KV_W), lambda b, t: (b, t + kv_blk0, 0)),
    ]
    est = d * n * 2 + 2 * rows * d * 4 + 6 * rows * n * 4
    return pl.pallas_call(
        functools.partial(_ab_in_kernel, use_rope=use_rope, rows=rows),
        out_shape=out_shape,
        grid=(bsz, seq // rows),
        in_specs=in_specs,
        out_specs=out_specs,
        input_output_aliases=aliases,
        compiler_params=pltpu.CompilerParams(
            dimension_semantics=("arbitrary", "arbitrary"),
            vmem_limit_bytes=_vmem_limit(est)),
        name="ab_in_rope" if use_rope else "ab_in_ctx",
    )(*args)


def _attn_kernel(q_ref, k_ref, v_ref, o_ref, vt_ref, *, rows, blk):
    @pl.when(pl.program_id(2) == 0)
    def _():
        vt_ref[...] = v_ref[0].astype(F32).T

    k = k_ref[0]
    vt = vt_ref[...].astype(BF16)
    for h in range(N_Q_HEADS // N_KV_HEADS):
        l_ = slice(h * HEAD_DIM, (h + 1) * HEAD_DIM)
        for r0 in range(0, rows, blk):
            st = _dot_nt(k, q_ref[0, r0:r0 + blk, l_])
            m = jnp.max(st, axis=0, keepdims=True)
            p = jnp.exp2(st - m)
            l = jnp.sum(p, axis=0, keepdims=True)
            ot = _dot(vt, p.astype(BF16)) / l
            o_ref[0, r0:r0 + blk, l_] = ot.T.astype(BF16)


def _attention(q, k, v, k_row0, sk, rows, blk=512):
    bsz, sq, _ = q.shape
    assert k_row0 % sk == 0
    kb = k_row0 // sk
    gw = (N_Q_HEADS // N_KV_HEADS) * HEAD_DIM
    est = 4 * blk * sk * 12 + 4 * sk * HEAD_DIM * 2 + 4 * rows * gw * 2 + sk * HEAD_DIM * 4
    return pl.pallas_call(
        functools.partial(_attn_kernel, rows=rows, blk=min(blk, rows)),
        out_shape=jax.ShapeDtypeStruct((bsz, sq, ATTN_WIDTH), BF16),
        grid=(bsz, N_KV_HEADS, sq // rows),
        in_specs=[
            pl.BlockSpec((1, rows, gw), lambda b, h, t: (b, t, h)),
            pl.BlockSpec((1, sk, HEAD_DIM), lambda b, h, t: (b, kb, h)),
            pl.BlockSpec((1, sk, HEAD_DIM), lambda b, h, t: (b, kb, h)),
        ],
        out_specs=pl.BlockSpec((1, rows, gw), lambda b, h, t: (b, t, h)),
        scratch_shapes=[pltpu.VMEM((HEAD_DIM, sk), F32)],
        compiler_params=pltpu.CompilerParams(
            dimension_semantics=("arbitrary", "arbitrary", "arbitrary"),
            vmem_limit_bytes=_vmem_limit(est)),
        name="attention",
    )(q, k, v)


def _out_ln_kernel(a1_ref, a2_ref, w_ref, x_ref, g_ref, lng_ref, lnb_ref, o_ref):
    half = a1_ref.shape[2]
    y = _dot(a1_ref[0], w_ref[0:half, :]) + _dot(a2_ref[0], w_ref[half:2 * half, :])
    o_ref[0] = _deepnorm(x_ref[0], g_ref[0], y, lng_ref[...], lnb_ref[...])


def _out_ln(a1, a2, w, x, gate, lng, lnb, rows):
    bsz, seq, d = x.shape
    half = a1.shape[2]
    est = 2 * half * d * 2 + 4 * rows * d * 4 + 4 * rows * half * 2 + 2 * rows * d * 4
    return pl.pallas_call(
        _out_ln_kernel,
        out_shape=jax.ShapeDtypeStruct((bsz, seq, d), F32),
        grid=(bsz, seq // rows),
        in_specs=[
            pl.BlockSpec((1, rows, half), lambda b, t: (b, t, 0)),
            pl.BlockSpec((1, rows, half), lambda b, t: (b, t, 0)),
            _resident((2 * half, d)),
            pl.BlockSpec((1, rows, d), lambda b, t: (b, t, 0)),
            _row_spec(d),
            _resident((1, d)), _resident((1, d)),
        ],
        out_specs=pl.BlockSpec((1, rows, d), lambda b, t: (b, t, 0)),
        compiler_params=pltpu.CompilerParams(
            dimension_semantics=("arbitrary", "arbitrary"),
            vmem_limit_bytes=_vmem_limit(est)),
        name="out_ln",
    )(a1, a2, w, x, gate, lng, lnb)


def _ffn_kernel(x_ref, xp_ref, xn_ref, sh_ref, sc_ref, g_ref, wu_ref, cw_ref, cb_ref, wd_ref, lng_ref, lnb_ref,
                o_ref, hm_ref, u_ref, gt_ref, *, rows):
    _fill_halo_tile(hm_ref, x_ref, xp_ref, xn_ref, sh_ref[0], sc_ref[0], rows)
    n_chunks = wd_ref.shape[0]

    def cols(ref, c):
        off = c * FF_CHUNK
        return ref[:, pl.ds(off if isinstance(off, int) else pl.multiple_of(off, FF_CHUNK), FF_CHUNK)]
    tiles = rows // V7X_SUBLANES
    sub = lax.broadcasted_iota(jnp.int32, (tiles, V7X_SUBLANES, FF_CHUNK), 1)

    def up_proj(c, slot):
        hm = hm_ref[...]
        u_ref[slot, 0] = _dot(hm, cols(wu_ref, c)).reshape(tiles + 2, V7X_SUBLANES, FF_CHUNK)
        u_ref[slot, 1] = _dot(hm, cols(wu_ref, c + n_chunks)).reshape(tiles + 2, V7X_SUBLANES, FF_CHUNK)

    def conv_gate(c, slot):
        def conv(half):
            u = u_ref[slot, half]
            dn = pltpu.roll(u, 1, 1)
            up = pltpu.roll(u, V7X_SUBLANES - 1, 1)
            prev = jnp.where(sub == 0, dn[0:tiles], dn[1:tiles + 1])
            nxt = jnp.where(sub == V7X_SUBLANES - 1, up[2:tiles + 2], up[1:tiles + 1])
            w = cols(cw_ref, c + half * n_chunks)
            return cols(cb_ref, c + half * n_chunks) + w[0:1] * prev + w[1:2] * u[1:tiles + 1] + w[2:3] * nxt

        a1 = conv(0)
        a2 = conv(1)
        gt_ref[c] = (a1 * (a2 + a2 * jnp.tanh(a2))).reshape(rows, FF_CHUNK)

    up_proj(0, 0)

    def pair(k, carry):
        c = 2 * k
        up_proj(c + 1, 1)
        conv_gate(c, 0)
        up_proj(c + 2, 0)
        conv_gate(c + 1, 1)
        return carry

    lax.fori_loop(0, (n_chunks - 1) // 2, pair, 0)
    conv_gate(n_chunks - 1, 0)

    f = _dot(gt_ref[0].astype(BF16), wd_ref[0])
    for c in range(1, n_chunks):
        f = f + _dot(gt_ref[c].astype(BF16), wd_ref[c])
    o_ref[0] = _deepnorm(x_ref[0], g_ref[0], f, lng_ref[...], lnb_ref[...])


def _ffn(x, sh, sc, gate, wu, cw, cb, wd, layer, lng, lnb, rows):
    bsz, seq, d = x.shape
    nc = wd.shape[1]
    dff = nc * FF_CHUNK
    assert wu.shape[2] == 2 * dff and nc % 2 == 1, "the pipeline peels one chunk and pairs the rest"

    def of_layer(*shape):
        nd = len(shape)
        return pl.BlockSpec((None,) + shape, lambda b, t: (layer,) + (0,) * nd, pipeline_mode=pl.Buffered(1))

    est = (3 * nc * d * FF_CHUNK * 2 + 4 * rows * d * 4 + (rows + 2 * HALO) * d * 2
           + nc * rows * FF_CHUNK * 4 + 8 * (rows + 2 * HALO) * FF_CHUNK * 4 + 2 * rows * d * 4)
    return pl.pallas_call(
        functools.partial(_ffn_kernel, rows=rows),
        out_shape=jax.ShapeDtypeStruct((bsz, seq, d), F32),
        grid=(bsz, seq // rows),
        in_specs=_halo_specs(rows, seq, d) + [
            _row_spec(d), _row_spec(d), _row_spec(d),
            of_layer(d, 2 * dff), of_layer(CONV_W, 2 * dff), of_layer(1, 2 * dff),
            of_layer(nc, FF_CHUNK, d),
            _resident((1, d)), _resident((1, d)),
        ],
        out_specs=pl.BlockSpec((1, rows, d), lambda b, t: (b, t, 0)),
        scratch_shapes=[
            pltpu.VMEM((rows + 2 * HALO, d), BF16),
            pltpu.VMEM((2, 2, rows // HALO + 2, HALO, FF_CHUNK), F32),
            pltpu.VMEM((nc, rows, FF_CHUNK), F32),
        ],
        compiler_params=pltpu.CompilerParams(
            dimension_semantics=("arbitrary", "arbitrary"),
            vmem_limit_bytes=_vmem_limit(est)),
        name="conv_ffn",
    )(x, x, x, sh, sc, gate, wu, cw, cb, wd, lng, lnb)


def _cd_in_kernel(*refs, with_fz, rows):
    x_ref, xp_ref, xn_ref, sh_ref, sc_ref = refs[:5]
    if with_fz:
        wfz_ref, wx_ref, wdt_ref, cw_ref, cb_ref, dtb_ref = refs[5:11]
        f_ref, z_ref, xs_ref, b_ref, c_ref, dt_ref, hm_ref = refs[11:]
    else:
        wx_ref, wdt_ref, cw_ref, cb_ref, dtb_ref = refs[5:10]
        xs_ref, b_ref, c_ref, dt_ref, hm_ref = refs[10:]
    _fill_halo_tile(hm_ref, x_ref, xp_ref, xn_ref, sh_ref[0], sc_ref[0], rows)
    hc = hm_ref[HALO:HALO + rows, :]
    if with_fz:
        pfz = _dot(hc, wfz_ref[...])
        f_ref[0] = pfz[:, 0:FN_WIDTH].astype(BF16)
        z_ref[0] = pfz[:, FN_WIDTH:SSM_OFF]
    a = _conv3(_dot(hm_ref[...], wx_ref[...]), cw_ref[...], cb_ref[...], rows)
    s = a + a * jnp.tanh(a)
    xs_ref[0] = s[:, 0:SSD_WIDTH]
    b_ref[0] = s[:, SSD_WIDTH:SSD_WIDTH + SSD_GN].astype(BF16)
    c_ref[0] = s[:, SSD_WIDTH + SSD_GN:SSD_CONV_DIM].astype(BF16)
    dt_ref[0] = _softplus(_dot(hc, wdt_ref[...]) + dtb_ref[...])


def _cd_in(x, sh, sc, wfz, wx, wdt, cw, cb, dtb, rows):
    bsz, seq, d = x.shape
    with_fz = wfz is not None
    in_specs = _halo_specs(rows, seq, d) + [_row_spec(d), _row_spec(d)]
    args = [x, x, x, sh, sc]
    out_shape, out_specs = [], []
    if with_fz:
        in_specs.append(_resident((d, SSM_OFF)))
        args.append(wfz)
        out_shape += [jax.ShapeDtypeStruct((bsz, seq, FN_WIDTH), BF16),
                      jax.ShapeDtypeStruct((bsz, seq, SSD_WIDTH), F32)]
        out_specs += [pl.BlockSpec((1, rows, FN_WIDTH), lambda b, t: (b, t, 0)),
                      pl.BlockSpec((1, rows, SSD_WIDTH), lambda b, t: (b, t, 0))]
    in_specs += [_resident((d, SSD_CONV_DIM)), _resident((d, V7X_LANES)),
                 _resident((CONV_W, SSD_CONV_DIM)), _resident((1, SSD_CONV_DIM)),
                 _resident((1, V7X_LANES))]
    args += [wx, wdt, cw, cb, dtb]
    out_shape += [jax.ShapeDtypeStruct((bsz, seq, SSD_WIDTH), F32),
                  jax.ShapeDtypeStruct((bsz, seq, SSD_GN), BF16),
                  jax.ShapeDtypeStruct((bsz, seq, SSD_GN), BF16),
                  jax.ShapeDtypeStruct((bsz, seq, V7X_LANES), F32)]
    out_specs += [pl.BlockSpec((1, rows, SSD_WIDTH), lambda b, t: (b, t, 0)),
                  pl.BlockSpec((1, rows, SSD_GN), lambda b, t: (b, t, 0)),
                  pl.BlockSpec((1, rows, SSD_GN), lambda b, t: (b, t, 0)),
                  pl.BlockSpec((1, rows, V7X_LANES), lambda b, t: (b, t, 0))]
    est = d * 2200 * 2 + 2 * rows * d * 4 + 8 * (rows + 2 * HALO) * SSD_CONV_DIM * 4
    return pl.pallas_call(
        functools.partial(_cd_in_kernel, with_fz=with_fz, rows=rows),
        out_shape=out_shape,
        grid=(bsz, seq // rows),
        in_specs=in_specs,
        out_specs=out_specs,
        scratch_shapes=[pltpu.VMEM((rows + 2 * HALO, d), BF16)],
        compiler_params=pltpu.CompilerParams(
            dimension_semantics=("arbitrary", "arbitrary"),
            vmem_limit_bytes=_vmem_limit(est)),
        name="cd_in" if with_fz else "cd_in_ctx",
    )(*args)


def _cumsum_rows(z, reverse):
    n = z.shape[0]
    idx = lax.broadcasted_iota(jnp.int32, z.shape, 0)
    k = 1
    while k < n:
        if reverse:
            z = z + jnp.where(idx < n - k, pltpu.roll(z, n - k, 0), 0.0)
        else:
            z = z + jnp.where(idx >= k, pltpu.roll(z, k, 0), 0.0)
        k *= 2
    return z


def _lane_expand(row, lane0, width):
    lane = lax.broadcasted_iota(jnp.int32, (1, SSD_HPG * width), 1)
    out = jnp.zeros((1, SSD_HPG * width), F32)
    for j in range(SSD_HPG):
        val = jnp.broadcast_to(row[:, lane0 + j:lane0 + j + 1], (1, SSD_HPG * width))
        out = jnp.where(lane // width == j, val, out)
    return out


def _ssd_chunk(x, bm, cm, dt, a_row, h_ref, lane0, reverse, need_y):
    q = x.shape[0]
    gw = SSD_HPG * SSD_HEADDIM
    acum = _cumsum_rows(dt * a_row, reverse)
    total = acum[0:1, :] if reverse else acum[q - 1:q, :]
    acum_t = acum.T
    dt_t = dt.T
    total_t = jnp.broadcast_to(acum_t[:, 0:1] if reverse else acum_t[:, q - 1:q], acum_t.shape)
    wgt_t = dt_t * jnp.exp2(total_t - acum_t)
    lane_g = lax.broadcasted_iota(jnp.int32, (q, gw), 1) // SSD_HEADDIM
    lane_gh = lax.broadcasted_iota(jnp.int32, (SSD_STATE, gw), 1) // SSD_HEADDIM
    if need_y:
        srow_t = acum_t - jnp.log2(dt_t)
        ti = lax.broadcasted_iota(jnp.int32, (q, q), 0)
        si = lax.broadcasted_iota(jnp.int32, (q, q), 1)
        mask = (si >= ti) if reverse else (si <= ti)
    ys = []
    for g in range(SSD_GROUPS):
        bg = bm[:, g * SSD_STATE:(g + 1) * SSD_STATE]
        xgb = x[:, g * gw:(g + 1) * gw].astype(BF16)
        x_heads = [jnp.where(lane_g == j, xgb, jnp.zeros_like(xgb)) for j in range(SSD_HPG)]
        hg = h_ref[g]
        if need_y:
            cg = cm[:, g * SSD_STATE:(g + 1) * SSD_STATE]
            cb = _dot_nt(cg, bg)
            cgf = cg.astype(F32)
            hgb = hg.astype(BF16)
            lhs, rhs = [], []
            for j in range(SSD_HPG):
                jl = lane0 + g * SSD_HPG + j
                col = jnp.broadcast_to(acum[:, jl:jl + 1], (q, q))
                row = jnp.broadcast_to(srow_t[jl:jl + 1, :], (q, q))
                lhs.append((cb * jnp.exp2(jnp.where(mask, col - row, -jnp.inf))).astype(BF16))
                lhs.append((cgf * jnp.exp2(col[:, 0:SSD_STATE])).astype(BF16))
                rhs.append(x_heads[j])
                rhs.append(jnp.where(lane_gh == j, hgb, jnp.zeros_like(hgb)))
            ys.append(_dot(jnp.concatenate(lhs, axis=1), jnp.concatenate(rhs, axis=0)))
        bgt = bg.astype(F32).T
        wb = [(bgt * jnp.broadcast_to(wgt_t[lane0 + g * SSD_HPG + j:lane0 + g * SSD_HPG + j + 1, :],
                                      (SSD_STATE, q))).astype(BF16) for j in range(SSD_HPG)]
        st = _dot(jnp.concatenate(wb, axis=1), jnp.concatenate(x_heads, axis=0))
        h_ref[g] = hg * jnp.exp2(_lane_expand(total, lane0 + g * SSD_HPG, SSD_HEADDIM)) + st
    if need_y:
        return jnp.concatenate(ys, axis=1)
    return None


def _ssd_kernel(xc_ref, bc_ref, dtc_ref, xl_ref, bl_ref, cl_ref, dtl_ref, z_ref, alog_ref,
                dskip_ref, ng_ref, o_ref, y_ref, h_ref):
    a_row = -jnp.exp(alog_ref[...]) * math.log2(math.e)
    nc_ctx = xc_ref.shape[1] // SSD_CHUNK
    nc_lat = xl_ref.shape[1] // SSD_CHUNK

    def rows_of(c):
        return pl.ds(pl.multiple_of(c * SSD_CHUNK, SSD_CHUNK), SSD_CHUNK)

    def ctx_step(c, direction):
        r = rows_of(c)
        _ssd_chunk(xc_ref[0, r, :], bc_ref[0, r, :], None, dtc_ref[0, r, :], a_row,
                   h_ref.at[direction], direction * SSD_HEADS, direction == 1, False)

    def lat_step(c, direction):
        r = rows_of(c)
        y_ref[direction, r, :] = _ssd_chunk(
            xl_ref[0, r, :], bl_ref[0, r, :], cl_ref[0, r, :], dtl_ref[0, r, :], a_row,
            h_ref.at[direction], direction * SSD_HEADS, direction == 1, True)

    h_ref[...] = jnp.zeros_like(h_ref)

    def ctx_pair(i, carry):
        ctx_step(i, 0)
        ctx_step(nc_ctx - 1 - i, 1)
        return carry

    lax.fori_loop(0, nc_ctx, ctx_pair, 0)

    def lat_pair(i, carry):
        lat_step(i, 0)
        lat_step(nc_lat - 1 - i, 1)
        return carry

    lax.fori_loop(0, nc_lat, lat_pair, 0)

    def finish(c, carry):
        r = rows_of(c)
        zz = z_ref[0, r, :]
        yt = (y_ref[0, r, :] + y_ref[1, r, :] + xl_ref[0, r, :] * dskip_ref[...]) * _silu(zz)
        o_ref[0, r, :] = _rms_norm(yt, ng_ref[...]).astype(BF16)
        return carry

    lax.fori_loop(0, nc_lat, finish, 0)


def _ssd(xs_c, b_c, dt_c, xs_l, b_l, c_l, dt_l, z, alog_row, dskip_row, ng_row):
    bsz, sl, w = xs_l.shape
    sc = xs_c.shape[1]
    assert sl % SSD_CHUNK == 0 and sc % SSD_CHUNK == 0
    gw = SSD_HPG * SSD_HEADDIM

    def full(s, n):
        return pl.BlockSpec((1, s, n), lambda b: (b, 0, 0))

    est = (2 * (sl + sc) * (w * 4 + 2 * SSD_GN * 2 + V7X_LANES * 4) + 2 * sl * w * 4
           + 2 * sl * w * 2 + 2 * sl * w * 4)
    return pl.pallas_call(
        _ssd_kernel,
        out_shape=jax.ShapeDtypeStruct((bsz, sl, w), BF16),
        grid=(bsz,),
        in_specs=[
            full(sc, w), full(sc, SSD_GN), full(sc, V7X_LANES),
            full(sl, w), full(sl, SSD_GN), full(sl, SSD_GN), full(sl, V7X_LANES),
            full(sl, w),
            _resident((1, V7X_LANES)), _resident((1, w)), _resident((1, w)),
        ],
        out_specs=full(sl, w),
        scratch_shapes=[
            pltpu.VMEM((2, sl, w), F32),
            pltpu.VMEM((2, SSD_GROUPS, SSD_STATE, gw), F32),
        ],
        compiler_params=pltpu.CompilerParams(
            dimension_semantics=("arbitrary",),
            vmem_limit_bytes=_vmem_limit(est)),
        name="ssd_scan",
    )(xs_c, b_c, dt_c, xs_l, b_l, c_l, dt_l, z, alog_row, dskip_row, ng_row)


def _fourier_kernel(f_ref, cc_ref, sc_ref, w2_ref, o_ref, y_ref, *, rows):
    seq = f_ref.shape[1]
    for g in range(FN_GROUPS):
        l = slice(g * FN_GDIM, (g + 1) * FN_GDIM)
        fg = f_ref[0, :, l]
        y_ref[0:seq, l] = _dot(fg, cc_ref[...]).astype(BF16)
        y_ref[seq:2 * seq, l] = _dot(fg, sc_ref[...]).astype(BF16)
    norm = 1.0 / math.sqrt(seq * FN_GDIM)

    def tile(i, carry):
        r = pl.ds(pl.multiple_of(i * rows, rows), rows)
        o_ref[0, r, :] = (_dot(w2_ref[r, :], y_ref[...]) * norm).astype(BF16)
        return carry

    lax.fori_loop(0, seq // rows, tile, 0)


def _fourier(f, cc, sc, w2, rows):
    bsz, seq, w = f.shape
    est = seq * 2 * seq * 2 + 4 * seq * w * 2 + 2 * seq * w * 2 + 4 * rows * 2 * seq * 2
    return pl.pallas_call(
        functools.partial(_fourier_kernel, rows=rows),
        out_shape=jax.ShapeDtypeStruct((bsz, seq, w), BF16),
        grid=(bsz,),
        in_specs=[
            pl.BlockSpec((1, seq, w), lambda b: (b, 0, 0)),
            _resident((FN_GDIM, FN_GDIM)), _resident((FN_GDIM, FN_GDIM)),
            _resident((seq, 2 * seq)),
        ],
        out_specs=pl.BlockSpec((1, seq, w), lambda b: (b, 0, 0)),
        scratch_shapes=[pltpu.VMEM((2 * seq, w), BF16)],
        compiler_params=pltpu.CompilerParams(
            dimension_semantics=("arbitrary",),
            vmem_limit_bytes=_vmem_limit(est)),
        name="fourier_mix",
    )(f, cc, sc, w2)


def _rope_tables(seq):
    rows = seq // GRID_W
    row = jnp.repeat(jnp.arange(rows, dtype=F32), GRID_W)
    col = jnp.broadcast_to(jnp.arange(GRID_W, dtype=F32), (rows, GRID_W)).reshape(-1)
    n_freq = HEAD_DIM // 4
    inv = ROPE_THETA ** (-jnp.arange(n_freq, dtype=F32) / n_freq)
    ang = jnp.stack([row, col], axis=-1)[:, :, None] * inv
    cos, sin = jnp.cos(ang), jnp.sin(ang)
    cos_t = jnp.concatenate([cos[:, 0], cos[:, 1], cos[:, 0], cos[:, 1]], axis=-1)
    sin_t = jnp.concatenate([-sin[:, 0], -sin[:, 1], sin[:, 0], sin[:, 1]], axis=-1)
    return cos_t, sin_t


def _rope_head_order(m):
    lead = m.shape[:-1]
    nh = m.shape[-1] // HEAD_DIM
    m = m.reshape(lead + (nh, 2, 2, HEAD_DIM // 4))
    return jnp.swapaxes(m, -3, -2).reshape(lead + (nh * HEAD_DIM,))


def _dft_tables(seq):
    def tw(n):
        i = np.arange(n, dtype=np.int64)
        ang = ((i[:, None] * i[None, :]) % n).astype(np.float64) * (2.0 * math.pi / n)
        return np.cos(ang), np.sin(ang)

    cc, sc = tw(FN_GDIM)
    cl, sl = tw(seq)
    w2 = np.concatenate([cl, -sl], axis=1).astype(np.float32)
    return (jnp.asarray(cc.astype(np.float32), BF16), jnp.asarray(sc.astype(np.float32), BF16),
            jnp.asarray(w2, BF16))


def _ffn_params(w_up, conv_w, conv_b, w_down):
    depth, d, _ = w_up.shape
    half_gate = jnp.concatenate([jnp.ones((D_FF,), F32), jnp.full((D_FF,), 0.5, F32)])
    return (w_up.astype(BF16), conv_w * half_gate, (conv_b * half_gate).reshape(depth, 1, 2 * D_FF),
            w_down.astype(BF16).reshape(depth, D_FF // FF_CHUNK, FF_CHUNK, d))


def kernel(x, c, ctx, c_ctx, mod_w, mod_b, ln1_g, ln1_b, ln2_g, ln2_b, ffn_up, ffn_conv_w, ffn_conv_b,
           ffn_down, ab_w_in, ab_w_out, gm_ln_g, gm_ln_b, gm_ws, gm_bs, q_norm_g, k_norm_g, cd_w_in,
           cd_w_out, ssd_conv_w, ssd_conv_b, ssd_dt_bias, ssd_a_log, ssd_d, ssd_norm_g):
    bsz, seq, d = x.shape
    ctx_len = ctx.shape[1]
    lat_rows = min(1024, seq)
    out_rows = min(2048, seq)
    dft_rows = min(2048, seq)
    ctx_rows = min(256, ctx_len)

    pad = (-(bsz + 1)) % V7X_SUBLANES
    cc = jnp.concatenate([c, c_ctx[None], jnp.zeros((pad, d), F32)], axis=0)
    mod = _modulation(cc, mod_w, mod_b)
    rope_tabs = _rope_tables(seq)

    def row(v):
        return v.reshape(1, -1)

    ffn_p = _ffn_params(ffn_up, ffn_conv_w, ffn_conv_b, ffn_down)
    xl, xc = x, ctx
    for i in range(DEPTH):
        last = i == DEPTH - 1
        j = i // 2
        ml = mod[i, :bsz].reshape(bsz, 6, 1, d)
        mc = jnp.broadcast_to(mod[i, bsz].reshape(1, 6, 1, d), (bsz, 6, 1, d))
        sh1, sc1, g1, sh2, sc2, g2 = (ml[:, k] for k in range(6))
        csh1, csc1, cg1, csh2, csc2, cg2 = (mc[:, k] for k in range(6))
        ln1 = (row(ln1_g[i]), row(ln1_b[i]))
        ln2 = (row(ln2_g[i]), row(ln2_b[i]))

        if i % 2 == 0:
            w_in = ab_w_in[j].astype(BF16)
            qk_lo, qk_hi = 2 * GM_WIDTH, KV_OFF + KV_W
            q_scale = HEAD_DIM ** -0.5 * math.log2(math.e)
            w_in = jnp.concatenate([0.5 * w_in[:, :qk_lo], _rope_head_order(w_in[:, qk_lo:qk_hi]),
                                    w_in[:, qk_hi:]], axis=1)
            w_out = ab_w_out[j].astype(BF16)
            gm = (row(gm_ln_g[j]), row(gm_ln_b[j]), gm_ws[j].astype(BF16),
                  jnp.repeat(gm_bs[j].T, GM_WIDTH // GM_GROUPS, axis=1),
                  row(_rope_head_order(q_norm_g[j]) * q_scale), row(_rope_head_order(k_norm_g[j])))
            kv_len = seq + ctx_len
            a_c, q_c, k_all, v_all = _ab_in(xc, csh1, csc1, w_in, *gm, None, ctx_rows, kv_len, seq)
            a_l, q_l, k_all, v_all = _ab_in(xl, sh1, sc1, w_in, *gm, rope_tabs, lat_rows, kv_len, 0,
                                            kv_into=(k_all, v_all))
            mix_l = (a_l, _attention(q_l, k_all, v_all, 0, kv_len, out_rows))
            if not last:
                mix_c = (a_c, _attention(q_c, k_all, v_all, seq, ctx_len, ctx_rows))
        else:
            w_in = cd_w_in[j].astype(BF16)
            w_out = cd_w_out[j].astype(BF16)
            wfz = w_in[:, :SSM_OFF]
            wx = w_in[:, SSM_OFF:SSM_OFF + SSD_CONV_DIM]
            ndt = 2 * SSD_HEADS
            wdt = jnp.pad(w_in[:, SSM_OFF + SSD_CONV_DIM:], ((0, 0), (0, V7X_LANES - ndt)))
            dtb = jnp.pad(ssd_dt_bias[j].reshape(1, ndt), ((0, 0), (0, V7X_LANES - ndt)))
            alog = jnp.pad(ssd_a_log[j].reshape(1, ndt), ((0, 0), (0, V7X_LANES - ndt)))
            cw, cb = 0.5 * ssd_conv_w[j], 0.5 * row(ssd_conv_b[j])
            dskip = row(jnp.repeat(ssd_d[j], SSD_HEADDIM))
            f_l, z_l, xs_l, b_l, c_l, dt_l = _cd_in(xl, sh1, sc1, wfz, wx, wdt, cw, cb, dtb, lat_rows)
            if last:
                xs_c, b_c, _, dt_c = _cd_in(xc, csh1, csc1, None, wx, wdt, cw, cb, dtb, ctx_rows)
            else:
                raise NotImplementedError("odd layers with a context output are not part of this block")
            s_l = _ssd(xs_c, b_c, dt_c, xs_l, b_l, c_l, dt_l, z_l, alog, dskip, row(ssd_norm_g[j]))
            mix_l = (_fourier(f_l, *_dft_tables(seq), dft_rows), s_l)

        xl = _out_ln(*mix_l, w_out, xl, g1, *ln1, out_rows)
        xl = _ffn(xl, sh2, sc2, g2, *ffn_p, i, *ln2, lat_rows)
        if not last:
            xc = _out_ln(*mix_c, w_out, xc, cg1, *ln1, ctx_rows)
            xc = _ffn(xc, csh2, csc2, cg2, *ffn_p, i, *ln2, ctx_rows)
    return xl
```

```python
import functools
import math

import jax
import jax.numpy as jnp
import numpy as np
from jax import lax
from jax.experimental import pallas as pl
from jax.experimental.pallas import tpu as pltpu

D_MODEL = 1024
DEPTH = 2
GRID_W = 64
CHUNK = 128
GM_GROUPS = 4
GM_WIDTH = 512
HEAD_DIM = 128
N_Q_HEADS = 4
N_KV_HEADS = 2
ATTN_WIDTH = 512
KV_W = 256
ROPE_THETA = 10000.0
FN_GROUPS = 4
FN_GDIM = 128
FN_WIDTH = 512
SSD_HEADDIM = 64
SSD_HEADS = 8
SSD_GROUPS = 2
SSD_HPG = 4
SSD_STATE = 128
SSD_WIDTH = 512
SSD_GN = 256
SSD_CONV_DIM = 1024
D_FF = 2816
CONV_W = 3
KV_OFF = 2 * GM_WIDTH + ATTN_WIDTH
SSM_OFF = FN_WIDTH + SSD_WIDTH
DN_ALPHA = (2 * DEPTH) ** 0.25
EPS = 1e-6

V7X_LANES = 128
V7X_SUBLANES = 8
V7X_VMEM_BYTES = 64 * 1024 * 1024
VMEM_CHIP_HEADROOM = 3 << 20
VMEM_TEMP_ALLOWANCE = 8 << 20
MOD_COLS = 2048
AB_SUB = 256
FF_CHUNK = 256
SSD_CHUNK = 256

BF16 = jnp.bfloat16
F32 = jnp.float32
HALO = V7X_SUBLANES


def _vmem_limit(nbytes):
    return int(min(nbytes * 3 // 2 + VMEM_TEMP_ALLOWANCE, V7X_VMEM_BYTES - VMEM_CHIP_HEADROOM))


def _resident(shape):
    nd = len(shape)
    return pl.BlockSpec(shape, lambda *_: (0,) * nd, pipeline_mode=pl.Buffered(1))


def _dot(a, b):
    return jnp.dot(a, b, preferred_element_type=F32)


def _dot_nt(a, b):
    return lax.dot_general(a, b, (((1,), (1,)), ((), ())), preferred_element_type=F32)


def _layer_norm(z, g, b, eps=EPS):
    mu = jnp.mean(z, axis=-1, keepdims=True)
    zc = z - mu
    var = jnp.mean(zc * zc, axis=-1, keepdims=True)
    return zc * lax.rsqrt(var + eps) * g + b


def _deepnorm(x, gate, y, g, b):
    return _layer_norm(x + (gate * (1.0 / DN_ALPHA)) * y, g, b, eps=EPS / DN_ALPHA ** 2)


def _rms_norm(z, g):
    return z * lax.rsqrt(jnp.mean(z * z, axis=-1, keepdims=True) + EPS) * g


def _softplus(z):
    return jnp.maximum(z, 0.0) + jnp.log1p(jnp.exp(-jnp.abs(z)))


def _gelu_tanh_of_half(h):
    c1 = math.sqrt(2.0 / math.pi)
    return h + h * jnp.tanh(h * (2.0 * c1 + (8.0 * c1 * 0.044715) * (h * h)))


def _silu(z):
    h = 0.5 * z
    return h + h * jnp.tanh(h)


def _conv3(u, w, b, rows):
    tiles, c = rows // V7X_SUBLANES, u.shape[1]
    u3 = u.reshape(tiles + 2, V7X_SUBLANES, c)
    sub = lax.broadcasted_iota(jnp.int32, (tiles, V7X_SUBLANES, c), 1)
    dn = pltpu.roll(u3, 1, 1)
    up = pltpu.roll(u3, V7X_SUBLANES - 1, 1)
    prev = jnp.where(sub == 0, dn[0:tiles], dn[1:tiles + 1])
    nxt = jnp.where(sub == V7X_SUBLANES - 1, up[2:tiles + 2], up[1:tiles + 1])
    return (b + w[0:1] * prev + w[1:2] * u3[1:tiles + 1] + w[2:3] * nxt).reshape(rows, c)


def _fill_halo_tile(hm_ref, x_ref, xp_ref, xn_ref, sh, sc, rows):
    t = pl.program_id(1)
    nt = pl.num_programs(1)
    keep_prev = jnp.where(t > 0, 1.0, 0.0)
    keep_next = jnp.where(t < nt - 1, 1.0, 0.0)
    hm_ref[HALO:HALO + rows, :] = (x_ref[0] * (1.0 + sc) + sh).astype(BF16)
    hm_ref[0:HALO, :] = ((xp_ref[0] * (1.0 + sc) + sh) * keep_prev).astype(BF16)
    hm_ref[HALO + rows:2 * HALO + rows, :] = ((xn_ref[0] * (1.0 + sc) + sh) * keep_next).astype(BF16)


def _halo_specs(rows, seq, d):
    nb = rows // HALO
    last = seq // HALO - 1
    return [
        pl.BlockSpec((1, rows, d), lambda b, t: (b, t, 0)),
        pl.BlockSpec((1, HALO, d), lambda b, t: (b, jnp.maximum(t * nb - 1, 0), 0)),
        pl.BlockSpec((1, HALO, d), lambda b, t: (b, jnp.minimum((t + 1) * nb, last), 0)),
    ]


def _row_spec(d):
    return pl.BlockSpec((1, 1, d), lambda b, t: (b, 0, 0))


def _mod_kernel(c_ref, w_ref, b_ref, o_ref):
    c = c_ref[...]
    act = c * jax.nn.sigmoid(c)
    o_ref[0] = jnp.dot(act, w_ref[0], preferred_element_type=F32,
                       precision=lax.Precision.HIGHEST) + b_ref[0]


def _modulation(cc, mod_w, mod_b):
    rows, d = cc.shape
    depth, _, n = mod_w.shape
    tn = MOD_COLS
    return pl.pallas_call(
        _mod_kernel,
        out_shape=jax.ShapeDtypeStruct((depth, rows, n), F32),
        grid=(depth, n // tn),
        in_specs=[
            pl.BlockSpec((rows, d), lambda i, j: (0, 0)),
            pl.BlockSpec((1, d, tn), lambda i, j: (i, 0, j)),
            pl.BlockSpec((1, 1, tn), lambda i, j: (i, 0, j)),
        ],
        out_specs=pl.BlockSpec((1, rows, tn), lambda i, j: (i, 0, j)),
        compiler_params=pltpu.CompilerParams(
            dimension_semantics=("arbitrary", "arbitrary"),
            vmem_limit_bytes=_vmem_limit(2 * d * tn * 4)),
        name="modulation",
    )(cc, mod_w, mod_b.reshape(depth, 1, n))


def _ab_in_kernel(*refs, use_rope, rows):
    (x_ref, sh_ref, sc_ref, w_ref, lng_ref, lnb_ref, ws_ref, bias_ref, qg_ref, kg_ref) = refs[:10]
    if use_rope:
        cos_ref, sin_ref = refs[10:12]
    a_ref, q_ref, k_ref, v_ref = refs[-4:]

    gd = GM_WIDTH // GM_GROUPS
    slab_rows = min(AB_SUB, rows)

    def slab(i, carry):
        r0 = pl.multiple_of(i * slab_rows, slab_rows)
        rs = pl.ds(r0, slab_rows)
        h = (x_ref[0, rs, :] * (1.0 + sc_ref[0]) + sh_ref[0]).astype(BF16)

        u = _gelu_tanh_of_half(_dot(h, w_ref[:, 0:GM_WIDTH]))
        vn = _layer_norm(_gelu_tanh_of_half(_dot(h, w_ref[:, GM_WIDTH:2 * GM_WIDTH])),
                         lng_ref[...], lnb_ref[...])
        vb = vn.astype(BF16)
        for c in range(slab_rows // CHUNK):
            r = slice(c * CHUNK, (c + 1) * CHUNK)
            for g in range(GM_GROUPS):
                l = slice(g * gd, (g + 1) * gd)
                mixed = _dot(ws_ref[g], vb[r, l]) + bias_ref[:, l]
                a_ref[0, pl.ds(r0 + c * CHUNK, CHUNK), l] = (u[r, l] * mixed).astype(BF16)

        def rope(z):
            if not use_rope:
                return z
            return z * cos_ref[rs, :] + pltpu.roll(z, HEAD_DIM // 2, 1) * sin_ref[rs, :]

        pq = _dot(h, w_ref[:, 2 * GM_WIDTH:KV_OFF])
        for hh in range(N_Q_HEADS):
            l = slice(hh * HEAD_DIM, (hh + 1) * HEAD_DIM)
            q_ref[0, rs, l] = rope(_rms_norm(pq[:, l], qg_ref[...])).astype(BF16)
        pk = _dot(h, w_ref[:, KV_OFF:KV_OFF + KV_W])
        for hh in range(N_KV_HEADS):
            l = slice(hh * HEAD_DIM, (hh + 1) * HEAD_DIM)
            k_ref[0, rs, l] = rope(_rms_norm(pk[:, l], kg_ref[...])).astype(BF16)
        v_ref[0, rs, :] = _dot(h, w_ref[:, KV_OFF + KV_W:KV_OFF + 2 * KV_W]).astype(BF16)
        return carry

    lax.fori_loop(0, rows // slab_rows, slab, 0)


def _ab_in(x, sh, sc, w, lng, lnb, ws, bias2d, qg, kg, rope_tabs, rows, kv_len, kv_row0, kv_into=None):
    bsz, seq, d = x.shape
    assert kv_row0 % rows == 0
    kv_blk0 = kv_row0 // rows
    n = w.shape[1]
    use_rope = rope_tabs is not None
    in_specs = [
        pl.BlockSpec((1, rows, d), lambda b, t: (b, t, 0)),
        _row_spec(d), _row_spec(d),
        _resident((d, n)),
        _resident((1, GM_WIDTH)), _resident((1, GM_WIDTH)),
        _resident((GM_GROUPS, CHUNK, CHUNK)),
        _resident((CHUNK, GM_WIDTH)),
        _resident((1, HEAD_DIM)), _resident((1, HEAD_DIM)),
    ]
    args = [x, sh, sc, w, lng, lnb, ws, bias2d, qg, kg]
    if use_rope:
        in_specs += [pl.BlockSpec((rows, HEAD_DIM), lambda b, t: (t, 0))] * 2
        args += list(rope_tabs)
    aliases = {}
    if kv_into is not None:
        aliases = {len(args): 2, len(args) + 1: 3}
        in_specs += [pl.BlockSpec(memory_space=pl.ANY)] * 2
        args += list(kv_into)
    out_shape = [
        jax.ShapeDtypeStruct((bsz, seq, GM_WIDTH), BF16),
        jax.ShapeDtypeStruct((bsz, seq, ATTN_WIDTH), BF16),
        jax.ShapeDtypeStruct((bsz, kv_len, KV_W), BF16),
        jax.ShapeDtypeStruct((bsz, kv_len, KV_W), BF16),
    ]
    out_specs = [
        pl.BlockSpec((1, rows, GM_WIDTH), lambda b, t: (b, t, 0)),
        pl.BlockSpec((1, rows, ATTN_WIDTH), lambda b, t: (b, t, 0)),
        pl.BlockSpec((1, rows, KV_W), lambda b, t: (b, t + kv_blk0, 0)),
        pl.BlockSpec((1, rows, KV_W), lambda b, t: (b, t + kv_blk0, 0)),
    ]
    est = d * n * 2 + 2 * rows * d * 4 + 6 * rows * n * 4
    return pl.pallas_call(
        functools.partial(_ab_in_kernel, use_rope=use_rope, rows=rows),
        out_shape=out_shape,
        grid=(bsz, seq // rows),
        in_specs=in_specs,
        out_specs=out_specs,
        input_output_aliases=aliases,
        compiler_params=pltpu.CompilerParams(
            dimension_semantics=("arbitrary", "arbitrary"),
            vmem_limit_bytes=_vmem_limit(est)),
        name="ab_in_rope" if use_rope else "ab_in_ctx",
    )(*args)


def _attn_kernel(q_ref, k_ref, v_ref, o_ref, vt_ref, *, rows, blk):
    @pl.when(pl.program_id(2) == 0)
    def _():
        vt_ref[...] = v_ref[0].astype(F32).T

    k = k_ref[0]
    vt = vt_ref[...].astype(BF16)
    for h in range(N_Q_HEADS // N_KV_HEADS):
        l_ = slice(h * HEAD_DIM, (h + 1) * HEAD_DIM)
        for r0 in range(0, rows, blk):
            st = _dot_nt(k, q_ref[0, r0:r0 + blk, l_])
            m = jnp.max(st, axis=0, keepdims=True)
            p = jnp.exp2(st - m)
            l = jnp.sum(p, axis=0, keepdims=True)
            ot = _dot(vt, p.astype(BF16)) / l
            o_ref[0, r0:r0 + blk, l_] = ot.T.astype(BF16)


def _attention(q, k, v, k_row0, sk, rows, blk=512):
    bsz, sq, _ = q.shape
    assert k_row0 % sk == 0
    kb = k_row0 // sk
    gw = (N_Q_HEADS // N_KV_HEADS) * HEAD_DIM
    est = 4 * blk * sk * 12 + 4 * sk * HEAD_DIM * 2 + 4 * rows * gw * 2 + sk * HEAD_DIM * 4
    return pl.pallas_call(
        functools.partial(_attn_kernel, rows=rows, blk=min(blk, rows)),
        out_shape=jax.ShapeDtypeStruct((bsz, sq, ATTN_WIDTH), BF16),
        grid=(bsz, N_KV_HEADS, sq // rows),
        in_specs=[
            pl.BlockSpec((1, rows, gw), lambda b, h, t: (b, t, h)),
            pl.BlockSpec((1, sk, HEAD_DIM), lambda b, h, t: (b, kb, h)),
            pl.BlockSpec((1, sk, HEAD_DIM), lambda b, h, t: (b, kb, h)),
        ],
        out_specs=pl.BlockSpec((1, rows, gw), lambda b, h, t: (b, t, h)),
        scratch_shapes=[pltpu.VMEM((HEAD_DIM, sk), F32)],
        compiler_params=pltpu.CompilerParams(
            dimension_semantics=("arbitrary", "arbitrary", "arbitrary"),
            vmem_limit_bytes=_vmem_limit(est)),
        name="attention",
    )(q, k, v)


def _out_ln_kernel(a1_ref, a2_ref, w_ref, x_ref, g_ref, lng_ref, lnb_ref, o_ref):
    half = a1_ref.shape[2]
    y = _dot(a1_ref[0], w_ref[0:half, :]) + _dot(a2_ref[0], w_ref[half:2 * half, :])
    o_ref[0] = _deepnorm(x_ref[0], g_ref[0], y, lng_ref[...], lnb_ref[...])


def _out_ln(a1, a2, w, x, gate, lng, lnb, rows):
    bsz, seq, d = x.shape
    half = a1.shape[2]
    est = 2 * half * d * 2 + 4 * rows * d * 4 + 4 * rows * half * 2 + 2 * rows * d * 4
    return pl.pallas_call(
        _out_ln_kernel,
        out_shape=jax.ShapeDtypeStruct((bsz, seq, d), F32),
        grid=(bsz, seq // rows),
        in_specs=[
            pl.BlockSpec((1, rows, half), lambda b, t: (b, t, 0)),
            pl.BlockSpec((1, rows, half), lambda b, t: (b, t, 0)),
            _resident((2 * half, d)),
            pl.BlockSpec((1, rows, d), lambda b, t: (b, t, 0)),
            _row_spec(d),
            _resident((1, d)), _resident((1, d)),
        ],
        out_specs=pl.BlockSpec((1, rows, d), lambda b, t: (b, t, 0)),
        compiler_params=pltpu.CompilerParams(
            dimension_semantics=("arbitrary", "arbitrary"),
            vmem_limit_bytes=_vmem_limit(est)),
        name="out_ln",
    )(a1, a2, w, x, gate, lng, lnb)


def _ffn_kernel(x_ref, xp_ref, xn_ref, sh_ref, sc_ref, g_ref, wu_ref, cw_ref, cb_ref, wd_ref, lng_ref, lnb_ref,
                o_ref, hm_ref, u_ref, gt_ref, *, rows):
    _fill_halo_tile(hm_ref, x_ref, xp_ref, xn_ref, sh_ref[0], sc_ref[0], rows)
    n_chunks = wd_ref.shape[0]

    def cols(ref, c):
        off = c * FF_CHUNK
        return ref[:, pl.ds(off if isinstance(off, int) else pl.multiple_of(off, FF_CHUNK), FF_CHUNK)]
    tiles = rows // V7X_SUBLANES
    sub = lax.broadcasted_iota(jnp.int32, (tiles, V7X_SUBLANES, FF_CHUNK), 1)

    def up_proj(c, slot):
        hm = hm_ref[...]
        u_ref[slot, 0] = _dot(hm, cols(wu_ref, c)).reshape(tiles + 2, V7X_SUBLANES, FF_CHUNK)
        u_ref[slot, 1] = _dot(hm, cols(wu_ref, c + n_chunks)).reshape(tiles + 2, V7X_SUBLANES, FF_CHUNK)

    def conv_gate(c, slot):
        def conv(half):
            u = u_ref[slot, half]
            dn = pltpu.roll(u, 1, 1)
            up = pltpu.roll(u, V7X_SUBLANES - 1, 1)
            prev = jnp.where(sub == 0, dn[0:tiles], dn[1:tiles + 1])
            nxt = jnp.where(sub == V7X_SUBLANES - 1, up[2:tiles + 2], up[1:tiles + 1])
            w = cols(cw_ref, c + half * n_chunks)
            return cols(cb_ref, c + half * n_chunks) + w[0:1] * prev + w[1:2] * u[1:tiles + 1] + w[2:3] * nxt

        a1 = conv(0)
        a2 = conv(1)
        gt_ref[c] = (a1 * (a2 + a2 * jnp.tanh(a2))).reshape(rows, FF_CHUNK)

    up_proj(0, 0)

    def pair(k, carry):
        c = 2 * k
        up_proj(c + 1, 1)
        conv_gate(c, 0)
        up_proj(c + 2, 0)
        conv_gate(c + 1, 1)
        return carry

    lax.fori_loop(0, (n_chunks - 1) // 2, pair, 0)
    conv_gate(n_chunks - 1, 0)

    f = _dot(gt_ref[0].astype(BF16), wd_ref[0])
    for c in range(1, n_chunks):
        f = f + _dot(gt_ref[c].astype(BF16), wd_ref[c])
    o_ref[0] = _deepnorm(x_ref[0], g_ref[0], f, lng_ref[...], lnb_ref[...])


def _ffn(x, sh, sc, gate, wu, cw, cb, wd, layer, lng, lnb, rows):
    bsz, seq, d = x.shape
    nc = wd.shape[1]
    dff = nc * FF_CHUNK
    assert wu.shape[2] == 2 * dff and nc % 2 == 1, "the pipeline peels one chunk and pairs the rest"

    def of_layer(*shape):
        nd = len(shape)
        return pl.BlockSpec((None,) + shape, lambda b, t: (layer,) + (0,) * nd, pipeline_mode=pl.Buffered(1))

    est = (3 * nc * d * FF_CHUNK * 2 + 4 * rows * d * 4 + (rows + 2 * HALO) * d * 2
           + nc * rows * FF_CHUNK * 4 + 8 * (rows + 2 * HALO) * FF_CHUNK * 4 + 2 * rows * d * 4)
    return pl.pallas_call(
        functools.partial(_ffn_kernel, rows=rows),
        out_shape=jax.ShapeDtypeStruct((bsz, seq, d), F32),
        grid=(bsz, seq // rows),
        in_specs=_halo_specs(rows, seq, d) + [
            _row_spec(d), _row_spec(d), _row_spec(d),
            of_layer(d, 2 * dff), of_layer(CONV_W, 2 * dff), of_layer(1, 2 * dff),
            of_layer(nc, FF_CHUNK, d),
            _resident((1, d)), _resident((1, d)),
        ],
        out_specs=pl.BlockSpec((1, rows, d), lambda b, t: (b, t, 0)),
        scratch_shapes=[
            pltpu.VMEM((rows + 2 * HALO, d), BF16),
            pltpu.VMEM((2, 2, rows // HALO + 2, HALO, FF_CHUNK), F32),
            pltpu.VMEM((nc, rows, FF_CHUNK), F32),
        ],
        compiler_params=pltpu.CompilerParams(
            dimension_semantics=("arbitrary", "arbitrary"),
            vmem_limit_bytes=_vmem_limit(est)),
        name="conv_ffn",
    )(x, x, x, sh, sc, gate, wu, cw, cb, wd, lng, lnb)


def _cd_in_kernel(*refs, with_fz, rows):
    x_ref, xp_ref, xn_ref, sh_ref, sc_ref = refs[:5]
    if with_fz:
        wfz_ref, wx_ref, wdt_ref, cw_ref, cb_ref, dtb_ref = refs[5:11]
        f_ref, z_ref, xs_ref, b_ref, c_ref, dt_ref, hm_ref = refs[11:]
    else:
        wx_ref, wdt_ref, cw_ref, cb_ref, dtb_ref = refs[5:10]
        xs_ref, b_ref, c_ref, dt_ref, hm_ref = refs[10:]
    _fill_halo_tile(hm_ref, x_ref, xp_ref, xn_ref, sh_ref[0], sc_ref[0], rows)
    hc = hm_ref[HALO:HALO + rows, :]
    if with_fz:
        pfz = _dot(hc, wfz_ref[...])
        f_ref[0] = pfz[:, 0:FN_WIDTH].astype(BF16)
        z_ref[0] = pfz[:, FN_WIDTH:SSM_OFF]
    a = _conv3(_dot(hm_ref[...], wx_ref[...]), cw_ref[...], cb_ref[...], rows)
    s = a + a * jnp.tanh(a)
    xs_ref[0] = s[:, 0:SSD_WIDTH]
    b_ref[0] = s[:, SSD_WIDTH:SSD_WIDTH + SSD_GN].astype(BF16)
    c_ref[0] = s[:, SSD_WIDTH + SSD_GN:SSD_CONV_DIM].astype(BF16)
    dt_ref[0] = _softplus(_dot(hc, wdt_ref[...]) + dtb_ref[...])


def _cd_in(x, sh, sc, wfz, wx, wdt, cw, cb, dtb, rows):
    bsz, seq, d = x.shape
    with_fz = wfz is not None
    in_specs = _halo_specs(rows, seq, d) + [_row_spec(d), _row_spec(d)]
    args = [x, x, x, sh, sc]
    out_shape, out_specs = [], []
    if with_fz:
        in_specs.append(_resident((d, SSM_OFF)))
        args.append(wfz)
        out_shape += [jax.ShapeDtypeStruct((bsz, seq, FN_WIDTH), BF16),
                      jax.ShapeDtypeStruct((bsz, seq, SSD_WIDTH), F32)]
        out_specs += [pl.BlockSpec((1, rows, FN_WIDTH), lambda b, t: (b, t, 0)),
                      pl.BlockSpec((1, rows, SSD_WIDTH), lambda b, t: (b, t, 0))]
    in_specs += [_resident((d, SSD_CONV_DIM)), _resident((d, V7X_LANES)),
                 _resident((CONV_W, SSD_CONV_DIM)), _resident((1, SSD_CONV_DIM)),
                 _resident((1, V7X_LANES))]
    args += [wx, wdt, cw, cb, dtb]
    out_shape += [jax.ShapeDtypeStruct((bsz, seq, SSD_WIDTH), F32),
                  jax.ShapeDtypeStruct((bsz, seq, SSD_GN), BF16),
                  jax.ShapeDtypeStruct((bsz, seq, SSD_GN), BF16),
                  jax.ShapeDtypeStruct((bsz, seq, V7X_LANES), F32)]
    out_specs += [pl.BlockSpec((1, rows, SSD_WIDTH), lambda b, t: (b, t, 0)),
                  pl.BlockSpec((1, rows, SSD_GN), lambda b, t: (b, t, 0)),
                  pl.BlockSpec((1, rows, SSD_GN), lambda b, t: (b, t, 0)),
                  pl.BlockSpec((1, rows, V7X_LANES), lambda b, t: (b, t, 0))]
    est = d * 2200 * 2 + 2 * rows * d * 4 + 8 * (rows + 2 * HALO) * SSD_CONV_DIM * 4
    return pl.pallas_call(
        functools.partial(_cd_in_kernel, with_fz=with_fz, rows=rows),
        out_shape=out_shape,
        grid=(bsz, seq // rows),
        in_specs=in_specs,
        out_specs=out_specs,
        scratch_shapes=[pltpu.VMEM((rows + 2 * HALO, d), BF16)],
        compiler_params=pltpu.CompilerParams(
            dimension_semantics=("arbitrary", "arbitrary"),
            vmem_limit_bytes=_vmem_limit(est)),
        name="cd_in" if with_fz else "cd_in_ctx",
    )(*args)


def _cumsum_rows(z, reverse):
    n = z.shape[0]
    idx = lax.broadcasted_iota(jnp.int32, z.shape, 0)
    k = 1
    while k < n:
        if reverse:
            z = z + jnp.where(idx < n - k, pltpu.roll(z, n - k, 0), 0.0)
        else:
            z = z + jnp.where(idx >= k, pltpu.roll(z, k, 0), 0.0)
        k *= 2
    return z


def _lane_expand(row, lane0, width):
    lane = lax.broadcasted_iota(jnp.int32, (1, SSD_HPG * width), 1)
    out = jnp.zeros((1, SSD_HPG * width), F32)
    for j in range(SSD_HPG):
        val = jnp.broadcast_to(row[:, lane0 + j:lane0 + j + 1], (1, SSD_HPG * width))
        out = jnp.where(lane // width == j, val, out)
    return out


def _ssd_chunk(x, bm, cm, dt, a_row, h_ref, lane0, reverse, need_y):
    q = x.shape[0]
    gw = SSD_HPG * SSD_HEADDIM
    acum = _cumsum_rows(dt * a_row, reverse)
    total = acum[0:1, :] if reverse else acum[q - 1:q, :]
    acum_t = acum.T
    dt_t = dt.T
    total_t = jnp.broadcast_to(acum_t[:, 0:1] if reverse else acum_t[:, q - 1:q], acum_t.shape)
    wgt_t = dt_t * jnp.exp2(total_t - acum_t)
    lane_g = lax.broadcasted_iota(jnp.int32, (q, gw), 1) // SSD_HEADDIM
    lane_gh = lax.broadcasted_iota(jnp.int32, (SSD_STATE, gw), 1) // SSD_HEADDIM
    if need_y:
        srow_t = acum_t - jnp.log2(dt_t)
        ti = lax.broadcasted_iota(jnp.int32, (q, q), 0)
        si = lax.broadcasted_iota(jnp.int32, (q, q), 1)
        mask = (si >= ti) if reverse else (si <= ti)
    ys = []
    for g in range(SSD_GROUPS):
        bg = bm[:, g * SSD_STATE:(g + 1) * SSD_STATE]
        xgb = x[:, g * gw:(g + 1) * gw].astype(BF16)
        x_heads = [jnp.where(lane_g == j, xgb, jnp.zeros_like(xgb)) for j in range(SSD_HPG)]
        hg = h_ref[g]
        if need_y:
            cg = cm[:, g * SSD_STATE:(g + 1) * SSD_STATE]
            cb = _dot_nt(cg, bg)
            cgf = cg.astype(F32)
            hgb = hg.astype(BF16)
            lhs, rhs = [], []
            for j in range(SSD_HPG):
                jl = lane0 + g * SSD_HPG + j
                col = jnp.broadcast_to(acum[:, jl:jl + 1], (q, q))
                row = jnp.broadcast_to(srow_t[jl:jl + 1, :], (q, q))
                lhs.append((cb * jnp.exp2(jnp.where(mask, col - row, -jnp.inf))).astype(BF16))
                lhs.append((cgf * jnp.exp2(col[:, 0:SSD_STATE])).astype(BF16))
                rhs.append(x_heads[j])
                rhs.append(jnp.where(lane_gh == j, hgb, jnp.zeros_like(hgb)))
            ys.append(_dot(jnp.concatenate(lhs, axis=1), jnp.concatenate(rhs, axis=0)))
        bgt = bg.astype(F32).T
        wb = [(bgt * jnp.broadcast_to(wgt_t[lane0 + g * SSD_HPG + j:lane0 + g * SSD_HPG + j + 1, :],
                                      (SSD_STATE, q))).astype(BF16) for j in range(SSD_HPG)]
        st = _dot(jnp.concatenate(wb, axis=1), jnp.concatenate(x_heads, axis=0))
        h_ref[g] = hg * jnp.exp2(_lane_expand(total, lane0 + g * SSD_HPG, SSD_HEADDIM)) + st
    if need_y:
        return jnp.concatenate(ys, axis=1)
    return None


def _ssd_kernel(xc_ref, bc_ref, dtc_ref, xl_ref, bl_ref, cl_ref, dtl_ref, z_ref, alog_ref,
                dskip_ref, ng_ref, o_ref, y_ref, h_ref):
    a_row = -jnp.exp(alog_ref[...]) * math.log2(math.e)
    nc_ctx = xc_ref.shape[1] // SSD_CHUNK
    nc_lat = xl_ref.shape[1] // SSD_CHUNK

    def rows_of(c):
        return pl.ds(pl.multiple_of(c * SSD_CHUNK, SSD_CHUNK), SSD_CHUNK)

    def ctx_step(c, direction):
        r = rows_of(c)
        _ssd_chunk(xc_ref[0, r, :], bc_ref[0, r, :], None, dtc_ref[0, r, :], a_row,
                   h_ref.at[direction], direction * SSD_HEADS, direction == 1, False)

    def lat_step(c, direction):
        r = rows_of(c)
        y_ref[direction, r, :] = _ssd_chunk(
            xl_ref[0, r, :], bl_ref[0, r, :], cl_ref[0, r, :], dtl_ref[0, r, :], a_row,
            h_ref.at[direction], direction * SSD_HEADS, direction == 1, True)

    h_ref[...] = jnp.zeros_like(h_ref)

    def ctx_pair(i, carry):
        ctx_step(i, 0)
        ctx_step(nc_ctx - 1 - i, 1)
        return carry

    lax.fori_loop(0, nc_ctx, ctx_pair, 0)

    def lat_pair(i, carry):
        lat_step(i, 0)
        lat_step(nc_lat - 1 - i, 1)
        return carry

    lax.fori_loop(0, nc_lat, lat_pair, 0)

    def finish(c, carry):
        r = rows_of(c)
        zz = z_ref[0, r, :]
        yt = (y_ref[0, r, :] + y_ref[1, r, :] + xl_ref[0, r, :] * dskip_ref[...]) * _silu(zz)
        o_ref[0, r, :] = _rms_norm(yt, ng_ref[...]).astype(BF16)
        return carry

    lax.fori_loop(0, nc_lat, finish, 0)


def _ssd(xs_c, b_c, dt_c, xs_l, b_l, c_l, dt_l, z, alog_row, dskip_row, ng_row):
    bsz, sl, w = xs_l.shape
    sc = xs_c.shape[1]
    assert sl % SSD_CHUNK == 0 and sc % SSD_CHUNK == 0
    gw = SSD_HPG * SSD_HEADDIM

    def full(s, n):
        return pl.BlockSpec((1, s, n), lambda b: (b, 0, 0))

    est = (2 * (sl + sc) * (w * 4 + 2 * SSD_GN * 2 + V7X_LANES * 4) + 2 * sl * w * 4
           + 2 * sl * w * 2 + 2 * sl * w * 4)
    return pl.pallas_call(
        _ssd_kernel,
        out_shape=jax.ShapeDtypeStruct((bsz, sl, w), BF16),
        grid=(bsz,),
        in_specs=[
            full(sc, w), full(sc, SSD_GN), full(sc, V7X_LANES),
            full(sl, w), full(sl, SSD_GN), full(sl, SSD_GN), full(sl, V7X_LANES),
            full(sl, w),
            _resident((1, V7X_LANES)), _resident((1, w)), _resident((1, w)),
        ],
        out_specs=full(sl, w),
        scratch_shapes=[
            pltpu.VMEM((2, sl, w), F32),
            pltpu.VMEM((2, SSD_GROUPS, SSD_STATE, gw), F32),
        ],
        compiler_params=pltpu.CompilerParams(
            dimension_semantics=("arbitrary",),
            vmem_limit_bytes=_vmem_limit(est)),
        name="ssd_scan",
    )(xs_c, b_c, dt_c, xs_l, b_l, c_l, dt_l, z, alog_row, dskip_row, ng_row)


def _fourier_kernel(f_ref, cc_ref, sc_ref, w2_ref, o_ref, y_ref, *, rows):
    seq = f_ref.shape[1]
    for g in range(FN_GROUPS):
        l = slice(g * FN_GDIM, (g + 1) * FN_GDIM)
        fg = f_ref[0, :, l]
        y_ref[0:seq, l] = _dot(fg, cc_ref[...]).astype(BF16)
        y_ref[seq:2 * seq, l] = _dot(fg, sc_ref[...]).astype(BF16)
    norm = 1.0 / math.sqrt(seq * FN_GDIM)

    def tile(i, carry):
        r = pl.ds(pl.multiple_of(i * rows, rows), rows)
        o_ref[0, r, :] = (_dot(w2_ref[r, :], y_ref[...]) * norm).astype(BF16)
        return carry

    lax.fori_loop(0, seq // rows, tile, 0)


def _fourier(f, cc, sc, w2, rows):
    bsz, seq, w = f.shape
    est = seq * 2 * seq * 2 + 4 * seq * w * 2 + 2 * seq * w * 2 + 4 * rows * 2 * seq * 2
    return pl.pallas_call(
        functools.partial(_fourier_kernel, rows=rows),
        out_shape=jax.ShapeDtypeStruct((bsz, seq, w), BF16),
        grid=(bsz,),
        in_specs=[
            pl.BlockSpec((1, seq, w), lambda b: (b, 0, 0)),
            _resident((FN_GDIM, FN_GDIM)), _resident((FN_GDIM, FN_GDIM)),
            _resident((seq, 2 * seq)),
        ],
        out_specs=pl.BlockSpec((1, seq, w), lambda b: (b, 0, 0)),
        scratch_shapes=[pltpu.VMEM((2 * seq, w), BF16)],
        compiler_params=pltpu.CompilerParams(
            dimension_semantics=("arbitrary",),
            vmem_limit_bytes=_vmem_limit(est)),
        name="fourier_mix",
    )(f, cc, sc, w2)


def _rope_tables(seq):
    rows = seq // GRID_W
    row = jnp.repeat(jnp.arange(rows, dtype=F32), GRID_W)
    col = jnp.broadcast_to(jnp.arange(GRID_W, dtype=F32), (rows, GRID_W)).reshape(-1)
    n_freq = HEAD_DIM // 4
    inv = ROPE_THETA ** (-jnp.arange(n_freq, dtype=F32) / n_freq)
    ang = jnp.stack([row, col], axis=-1)[:, :, None] * inv
    cos, sin = jnp.cos(ang), jnp.sin(ang)
    cos_t = jnp.concatenate([cos[:, 0], cos[:, 1], cos[:, 0], cos[:, 1]], axis=-1)
    sin_t = jnp.concatenate([-sin[:, 0], -sin[:, 1], sin[:, 0], sin[:, 1]], axis=-1)
    return cos_t, sin_t


def _rope_head_order(m):
    lead = m.shape[:-1]
    nh = m.shape[-1] // HEAD_DIM
    m = m.reshape(lead + (nh, 2, 2, HEAD_DIM // 4))
    return jnp.swapaxes(m, -3, -2).reshape(lead + (nh * HEAD_DIM,))


def _dft_tables(seq):
    def tw(n):
        i = np.arange(n, dtype=np.int64)
        ang = ((i[:, None] * i[None, :]) % n).astype(np.float64) * (2.0 * math.pi / n)
        return np.cos(ang), np.sin(ang)

    cc, sc = tw(FN_GDIM)
    cl, sl = tw(seq)
    w2 = np.concatenate([cl, -sl], axis=1).astype(np.float32)
    return (jnp.asarray(cc.astype(np.float32), BF16), jnp.asarray(sc.astype(np.float32), BF16),
            jnp.asarray(w2, BF16))


def _ffn_params(w_up, conv_w, conv_b, w_down):
    depth, d, _ = w_up.shape
    half_gate = jnp.concatenate([jnp.ones((D_FF,), F32), jnp.full((D_FF,), 0.5, F32)])
    return (w_up.astype(BF16), conv_w * half_gate, (conv_b * half_gate).reshape(depth, 1, 2 * D_FF),
            w_down.astype(BF16).reshape(depth, D_FF // FF_CHUNK, FF_CHUNK, d))


def kernel(x, c, ctx, c_ctx, mod_w, mod_b, ln1_g, ln1_b, ln2_g, ln2_b, ffn_up, ffn_conv_w, ffn_conv_b,
           ffn_down, ab_w_in, ab_w_out, gm_ln_g, gm_ln_b, gm_ws, gm_bs, q_norm_g, k_norm_g, cd_w_in,
           cd_w_out, ssd_conv_w, ssd_conv_b, ssd_dt_bias, ssd_a_log, ssd_d, ssd_norm_g):
    bsz, seq, d = x.shape
    ctx_len = ctx.shape[1]
    lat_rows = min(1024, seq)
    out_rows = min(2048, seq)
    dft_rows = min(2048, seq)
    ctx_rows = min(256, ctx_len)

    pad = (-(bsz + 1)) % V7X_SUBLANES
    cc = jnp.concatenate([c, c_ctx[None], jnp.zeros((pad, d), F32)], axis=0)
    mod = _modulation(cc, mod_w, mod_b)
    rope_tabs = _rope_tables(seq)

    def row(v):
        return v.reshape(1, -1)

    ffn_p = _ffn_params(ffn_up, ffn_conv_w, ffn_conv_b, ffn_down)
    xl, xc = x, ctx
    for i in range(DEPTH):
        last = i == DEPTH - 1
        j = i // 2
        ml = mod[i, :bsz].reshape(bsz, 6, 1, d)
        mc = jnp.broadcast_to(mod[i, bsz].reshape(1, 6, 1, d), (bsz, 6, 1, d))
        sh1, sc1, g1, sh2, sc2, g2 = (ml[:, k] for k in range(6))
        csh1, csc1, cg1, csh2, csc2, cg2 = (mc[:, k] for k in range(6))
        ln1 = (row(ln1_g[i]), row(ln1_b[i]))
        ln2 = (row(ln2_g[i]), row(ln2_b[i]))

        if i % 2 == 0:
            w_in = ab_w_in[j].astype(BF16)
            qk_lo, qk_hi = 2 * GM_WIDTH, KV_OFF + KV_W
            q_scale = HEAD_DIM ** -0.5 * math.log2(math.e)
            w_in = jnp.concatenate([0.5 * w_in[:, :qk_lo], _rope_head_order(w_in[:, qk_lo:qk_hi]),
                                    w_in[:, qk_hi:]], axis=1)
            w_out = ab_w_out[j].astype(BF16)
            gm = (row(gm_ln_g[j]), row(gm_ln_b[j]), gm_ws[j].astype(BF16),
                  jnp.repeat(gm_bs[j].T, GM_WIDTH // GM_GROUPS, axis=1),
                  row(_rope_head_order(q_norm_g[j]) * q_scale), row(_rope_head_order(k_norm_g[j])))
            kv_len = seq + ctx_len
            a_c, q_c, k_all, v_all = _ab_in(xc, csh1, csc1, w_in, *gm, None, ctx_rows, kv_len, seq)
            a_l, q_l, k_all, v_all = _ab_in(xl, sh1, sc1, w_in, *gm, rope_tabs, out_rows, kv_len, 0,
                                            kv_into=(k_all, v_all))
            mix_l = (a_l, _attention(q_l, k_all, v_all, 0, kv_len, out_rows))
            if not last:
                mix_c = (a_c, _attention(q_c, k_all, v_all, seq, ctx_len, ctx_rows))
        else:
            w_in = cd_w_in[j].astype(BF16)
            w_out = cd_w_out[j].astype(BF16)
            wfz = w_in[:, :SSM_OFF]
            wx = w_in[:, SSM_OFF:SSM_OFF + SSD_CONV_DIM]
            ndt = 2 * SSD_HEADS
            wdt = jnp.pad(w_in[:, SSM_OFF + SSD_CONV_DIM:], ((0, 0), (0, V7X_LANES - ndt)))
            dtb = jnp.pad(ssd_dt_bias[j].reshape(1, ndt), ((0, 0), (0, V7X_LANES - ndt)))
            alog = jnp.pad(ssd_a_log[j].reshape(1, ndt), ((0, 0), (0, V7X_LANES - ndt)))
            cw, cb = 0.5 * ssd_conv_w[j], 0.5 * row(ssd_conv_b[j])
            dskip = row(jnp.repeat(ssd_d[j], SSD_HEADDIM))
            f_l, z_l, xs_l, b_l, c_l, dt_l = _cd_in(xl, sh1, sc1, wfz, wx, wdt, cw, cb, dtb, lat_rows)
            if last:
                xs_c, b_c, _, dt_c = _cd_in(xc, csh1, csc1, None, wx, wdt, cw, cb, dtb, ctx_rows)
            else:
                raise NotImplementedError("odd layers with a context output are not part of this block")
            s_l = _ssd(xs_c, b_c, dt_c, xs_l, b_l, c_l, dt_l, z_l, alog, dskip, row(ssd_norm_g[j]))
            mix_l = (_fourier(f_l, *_dft_tables(seq), dft_rows), s_l)

        xl = _out_ln(*mix_l, w_out, xl, g1, *ln1, out_rows)
        xl = _ffn(xl, sh2, sc2, g2, *ffn_p, i, *ln2, lat_rows)
        if not last:
            xc = _out_ln(*mix_c, w_out, xc, cg1, *ln1, ctx_rows)
            xc = _ffn(xc, csh2, csc2, cg2, *ffn_p, i, *ln2, ctx_rows)
    return xl
```

```python
import functools
import math

import jax
import jax.numpy as jnp
import numpy as np
from jax import lax
from jax.experimental import pallas as pl
from jax.experimental.pallas import tpu as pltpu

D_MODEL = 1024
DEPTH = 2
GRID_W = 64
CHUNK = 128
GM_GROUPS = 4
GM_WIDTH = 512
HEAD_DIM = 128
N_Q_HEADS = 4
N_KV_HEADS = 2
ATTN_WIDTH = 512
KV_W = 256
ROPE_THETA = 10000.0
FN_GROUPS = 4
FN_GDIM = 128
FN_WIDTH = 512
SSD_HEADDIM = 64
SSD_HEADS = 8
SSD_GROUPS = 2
SSD_HPG = 4
SSD_STATE = 128
SSD_WIDTH = 512
SSD_GN = 256
SSD_CONV_DIM = 1024
D_FF = 2816
CONV_W = 3
KV_OFF = 2 * GM_WIDTH + ATTN_WIDTH
SSM_OFF = FN_WIDTH + SSD_WIDTH
DN_ALPHA = (2 * DEPTH) ** 0.25
EPS = 1e-6

V7X_LANES = 128
V7X_SUBLANES = 8
V7X_VMEM_BYTES = 64 * 1024 * 1024
VMEM_CHIP_HEADROOM = 3 << 20
VMEM_TEMP_ALLOWANCE = 8 << 20
MOD_COLS = 2048
AB_SUB = 256
FF_CHUNK = 256
SSD_CHUNK = 256

BF16 = jnp.bfloat16
F32 = jnp.float32
HALO = V7X_SUBLANES


def _vmem_limit(nbytes):
    return int(min(nbytes * 3 // 2 + VMEM_TEMP_ALLOWANCE, V7X_VMEM_BYTES - VMEM_CHIP_HEADROOM))


def _resident(shape):
    nd = len(shape)
    return pl.BlockSpec(shape, lambda *_: (0,) * nd, pipeline_mode=pl.Buffered(1))


def _dot(a, b):
    return jnp.dot(a, b, preferred_element_type=F32)


def _dot_nt(a, b):
    return lax.dot_general(a, b, (((1,), (1,)), ((), ())), preferred_element_type=F32)


def _layer_norm(z, g, b, eps=EPS):
    mu = jnp.mean(z, axis=-1, keepdims=True)
    zc = z - mu
    var = jnp.mean(zc * zc, axis=-1, keepdims=True)
    return zc * lax.rsqrt(var + eps) * g + b


def _deepnorm(x, gate, y, g, b):
    return _layer_norm(x + (gate * (1.0 / DN_ALPHA)) * y, g, b, eps=EPS / DN_ALPHA ** 2)


def _rms_norm(z, g):
    return z * lax.rsqrt(jnp.mean(z * z, axis=-1, keepdims=True) + EPS) * g


def _softplus(z):
    return jnp.maximum(z, 0.0) + jnp.log1p(jnp.exp(-jnp.abs(z)))


def _gelu_tanh_of_half(h):
    c1 = math.sqrt(2.0 / math.pi)
    return h + h * jnp.tanh(h * (2.0 * c1 + (8.0 * c1 * 0.044715) * (h * h)))


def _silu(z):
    h = 0.5 * z
    return h + h * jnp.tanh(h)


def _conv3(u, w, b, rows):
    tiles, c = rows // V7X_SUBLANES, u.shape[1]
    u3 = u.reshape(tiles + 2, V7X_SUBLANES, c)
    sub = lax.broadcasted_iota(jnp.int32, (tiles, V7X_SUBLANES, c), 1)
    dn = pltpu.roll(u3, 1, 1)
    up = pltpu.roll(u3, V7X_SUBLANES - 1, 1)
    prev = jnp.where(sub == 0, dn[0:tiles], dn[1:tiles + 1])
    nxt = jnp.where(sub == V7X_SUBLANES - 1, up[2:tiles + 2], up[1:tiles + 1])
    return (b + w[0:1] * prev + w[1:2] * u3[1:tiles + 1] + w[2:3] * nxt).reshape(rows, c)


def _fill_halo_tile(hm_ref, x_ref, xp_ref, xn_ref, sh, sc, rows):
    t = pl.program_id(1)
    nt = pl.num_programs(1)
    keep_prev = jnp.where(t > 0, 1.0, 0.0)
    keep_next = jnp.where(t < nt - 1, 1.0, 0.0)
    hm_ref[HALO:HALO + rows, :] = (x_ref[0] * (1.0 + sc) + sh).astype(BF16)
    hm_ref[0:HALO, :] = ((xp_ref[0] * (1.0 + sc) + sh) * keep_prev).astype(BF16)
    hm_ref[HALO + rows:2 * HALO + rows, :] = ((xn_ref[0] * (1.0 + sc) + sh) * keep_next).astype(BF16)


def _halo_specs(rows, seq, d):
    nb = rows // HALO
    last = seq // HALO - 1
    return [
        pl.BlockSpec((1, rows, d), lambda b, t: (b, t, 0)),
        pl.BlockSpec((1, HALO, d), lambda b, t: (b, jnp.maximum(t * nb - 1, 0), 0)),
        pl.BlockSpec((1, HALO, d), lambda b, t: (b, jnp.minimum((t + 1) * nb, last), 0)),
    ]


def _row_spec(d):
    return pl.BlockSpec((1, 1, d), lambda b, t: (b, 0, 0))


def _mod_kernel(c_ref, w_ref, b_ref, o_ref):
    c = c_ref[...]
    act = c * jax.nn.sigmoid(c)
    o_ref[0] = jnp.dot(act, w_ref[0], preferred_element_type=F32,
                       precision=lax.Precision.HIGHEST) + b_ref[0]


def _modulation(cc, mod_w, mod_b):
    rows, d = cc.shape
    depth, _, n = mod_w.shape
    tn = MOD_COLS
    return pl.pallas_call(
        _mod_kernel,
        out_shape=jax.ShapeDtypeStruct((depth, rows, n), F32),
        grid=(depth, n // tn),
        in_specs=[
            pl.BlockSpec((rows, d), lambda i, j: (0, 0)),
            pl.BlockSpec((1, d, tn), lambda i, j: (i, 0, j)),
            pl.BlockSpec((1, 1, tn), lambda i, j: (i, 0, j)),
        ],
        out_specs=pl.BlockSpec((1, rows, tn), lambda i, j: (i, 0, j)),
        compiler_params=pltpu.CompilerParams(
            dimension_semantics=("arbitrary", "arbitrary"),
            vmem_limit_bytes=_vmem_limit(2 * d * tn * 4)),
        name="modulation",
    )(cc, mod_w, mod_b.reshape(depth, 1, n))


def _ab_in_kernel(*refs, use_rope, rows):
    (x_ref, sh_ref, sc_ref, w_ref, lng_ref, lnb_ref, ws_ref, bias_ref, qg_ref, kg_ref) = refs[:10]
    if use_rope:
        cos_ref, sin_ref = refs[10:12]
    a_ref, q_ref, k_ref, v_ref = refs[-4:]

    gd = GM_WIDTH // GM_GROUPS
    slab_rows = min(AB_SUB, rows)

    def slab(i, carry):
        r0 = pl.multiple_of(i * slab_rows, slab_rows)
        rs = pl.ds(r0, slab_rows)
        h = (x_ref[0, rs, :] * (1.0 + sc_ref[0]) + sh_ref[0]).astype(BF16)

        u = _gelu_tanh_of_half(_dot(h, w_ref[:, 0:GM_WIDTH]))
        vn = _layer_norm(_gelu_tanh_of_half(_dot(h, w_ref[:, GM_WIDTH:2 * GM_WIDTH])),
                         lng_ref[...], lnb_ref[...])
        vb = vn.astype(BF16)
        for c in range(slab_rows // CHUNK):
            r = slice(c * CHUNK, (c + 1) * CHUNK)
            for g in range(GM_GROUPS):
                l = slice(g * gd, (g + 1) * gd)
                mixed = _dot(ws_ref[g], vb[r, l]) + bias_ref[:, l]
                a_ref[0, pl.ds(r0 + c * CHUNK, CHUNK), l] = (u[r, l] * mixed).astype(BF16)

        def rope(z):
            if not use_rope:
                return z
            return z * cos_ref[rs, :] + pltpu.roll(z, HEAD_DIM // 2, 1) * sin_ref[rs, :]

        pq = _dot(h, w_ref[:, 2 * GM_WIDTH:KV_OFF])
        for hh in range(N_Q_HEADS):
            l = slice(hh * HEAD_DIM, (hh + 1) * HEAD_DIM)
            q_ref[0, rs, l] = rope(_rms_norm(pq[:, l], qg_ref[...])).astype(BF16)
        pk = _dot(h, w_ref[:, KV_OFF:KV_OFF + KV_W])
        for hh in range(N_KV_HEADS):
            l = slice(hh * HEAD_DIM, (hh + 1) * HEAD_DIM)
            k_ref[0, rs, l] = rope(_rms_norm(pk[:, l], kg_ref[...])).astype(BF16)
        v_ref[0, rs, :] = _dot(h, w_ref[:, KV_OFF + KV_W:KV_OFF + 2 * KV_W]).astype(BF16)
        return carry

    lax.fori_loop(0, rows // slab_rows, slab, 0)


def _ab_in(x, sh, sc, w, lng, lnb, ws, bias2d, qg, kg, rope_tabs, rows, kv_len, kv_row0, kv_into=None):
    bsz, seq, d = x.shape
    assert kv_row0 % rows == 0
    kv_blk0 = kv_row0 // rows
    n = w.shape[1]
    use_rope = rope_tabs is not None
    in_specs = [
        pl.BlockSpec((1, rows, d), lambda b, t: (b, t, 0)),
        _row_spec(d), _row_spec(d),
        _resident((d, n)),
        _resident((1, GM_WIDTH)), _resident((1, GM_WIDTH)),
        _resident((GM_GROUPS, CHUNK, CHUNK)),
        _resident((CHUNK, GM_WIDTH)),
        _resident((1, HEAD_DIM)), _resident((1, HEAD_DIM)),
    ]
    args = [x, sh, sc, w, lng, lnb, ws, bias2d, qg, kg]
    if use_rope:
        in_specs += [pl.BlockSpec((rows, HEAD_DIM), lambda b, t: (t, 0))] * 2
        args += list(rope_tabs)
    aliases = {}
    if kv_into is not None:
        aliases = {len(args): 2, len(args) + 1: 3}
        in_specs += [pl.BlockSpec(memory_space=pl.ANY)] * 2
        args += list(kv_into)
    out_shape = [
        jax.ShapeDtypeStruct((bsz, seq, GM_WIDTH), BF16),
        jax.ShapeDtypeStruct((bsz, seq, ATTN_WIDTH), BF16),
        jax.ShapeDtypeStruct((bsz, kv_len, KV_W), BF16),
        jax.ShapeDtypeStruct((bsz, kv_len, KV_W), BF16),
    ]
    out_specs = [
        pl.BlockSpec((1, rows, GM_WIDTH), lambda b, t: (b, t, 0)),
        pl.BlockSpec((1, rows, ATTN_WIDTH), lambda b, t: (b, t, 0)),
        pl.BlockSpec((1, rows, KV_W), lambda b, t: (b, t + kv_blk0, 0)),
        pl.BlockSpec((1, rows, KV_W), lambda b, t: (b, t + kv_blk0, 0)),
    ]
    est = d * n * 2 + 2 * rows * d * 4 + 6 * rows * n * 4
    return pl.pallas_call(
        functools.partial(_ab_in_kernel, use_rope=use_rope, rows=rows),
        out_shape=out_shape,
        grid=(bsz, seq // rows),
        in_specs=in_specs,
        out_specs=out_specs,
        input_output_aliases=aliases,
        compiler_params=pltpu.CompilerParams(
            dimension_semantics=("arbitrary", "arbitrary"),
            vmem_limit_bytes=_vmem_limit(est)),
        name="ab_in_rope" if use_rope else "ab_in_ctx",
    )(*args)


def _attn_kernel(q_ref, k_ref, v_ref, o_ref, vt_ref, *, rows, blk):
    @pl.when(pl.program_id(2) == 0)
    def _():
        vt_ref[...] = v_ref[0].astype(F32).T

    k = k_ref[0]
    vt = vt_ref[...].astype(BF16)
    for h in range(N_Q_HEADS // N_KV_HEADS):
        l_ = slice(h * HEAD_DIM, (h + 1) * HEAD_DIM)
        for r0 in range(0, rows, blk):
            st = _dot_nt(k, q_ref[0, r0:r0 + blk, l_])
            m = jnp.max(st, axis=0, keepdims=True)
            p = jnp.exp2(st - m)
            l = jnp.sum(p, axis=0, keepdims=True)
            ot = _dot(vt, p.astype(BF16)) / l
            o_ref[0, r0:r0 + blk, l_] = ot.T.astype(BF16)


def _attention(q, k, v, k_row0, sk, rows, blk=1024):
    bsz, sq, _ = q.shape
    assert k_row0 % sk == 0
    kb = k_row0 // sk
    gw = (N_Q_HEADS // N_KV_HEADS) * HEAD_DIM
    est = 4 * blk * sk * 12 + 4 * sk * HEAD_DIM * 2 + 4 * rows * gw * 2 + sk * HEAD_DIM * 4
    return pl.pallas_call(
        functools.partial(_attn_kernel, rows=rows, blk=min(blk, rows)),
        out_shape=jax.ShapeDtypeStruct((bsz, sq, ATTN_WIDTH), BF16),
        grid=(bsz, N_KV_HEADS, sq // rows),
        in_specs=[
            pl.BlockSpec((1, rows, gw), lambda b, h, t: (b, t, h)),
            pl.BlockSpec((1, sk, HEAD_DIM), lambda b, h, t: (b, kb, h)),
            pl.BlockSpec((1, sk, HEAD_DIM), lambda b, h, t: (b, kb, h)),
        ],
        out_specs=pl.BlockSpec((1, rows, gw), lambda b, h, t: (b, t, h)),
        scratch_shapes=[pltpu.VMEM((HEAD_DIM, sk), F32)],
        compiler_params=pltpu.CompilerParams(
            dimension_semantics=("arbitrary", "arbitrary", "arbitrary"),
            vmem_limit_bytes=_vmem_limit(est)),
        name="attention",
    )(q, k, v)


def _out_ln_kernel(a1_ref, a2_ref, w_ref, x_ref, g_ref, lng_ref, lnb_ref, o_ref):
    half = a1_ref.shape[2]
    y = _dot(a1_ref[0], w_ref[0:half, :]) + _dot(a2_ref[0], w_ref[half:2 * half, :])
    o_ref[0] = _deepnorm(x_ref[0], g_ref[0], y, lng_ref[...], lnb_ref[...])


def _out_ln(a1, a2, w, x, gate, lng, lnb, rows):
    bsz, seq, d = x.shape
    half = a1.shape[2]
    est = 2 * half * d * 2 + 4 * rows * d * 4 + 4 * rows * half * 2 + 2 * rows * d * 4
    return pl.pallas_call(
        _out_ln_kernel,
        out_shape=jax.ShapeDtypeStruct((bsz, seq, d), F32),
        grid=(bsz, seq // rows),
        in_specs=[
            pl.BlockSpec((1, rows, half), lambda b, t: (b, t, 0)),
            pl.BlockSpec((1, rows, half), lambda b, t: (b, t, 0)),
            _resident((2 * half, d)),
            pl.BlockSpec((1, rows, d), lambda b, t: (b, t, 0)),
            _row_spec(d),
            _resident((1, d)), _resident((1, d)),
        ],
        out_specs=pl.BlockSpec((1, rows, d), lambda b, t: (b, t, 0)),
        compiler_params=pltpu.CompilerParams(
            dimension_semantics=("arbitrary", "arbitrary"),
            vmem_limit_bytes=_vmem_limit(est)),
        name="out_ln",
    )(a1, a2, w, x, gate, lng, lnb)


def _ffn_kernel(x_ref, xp_ref, xn_ref, sh_ref, sc_ref, g_ref, wu_ref, cw_ref, cb_ref, wd_ref, lng_ref, lnb_ref,
                o_ref, hm_ref, u_ref, gt_ref, *, rows):
    _fill_halo_tile(hm_ref, x_ref, xp_ref, xn_ref, sh_ref[0], sc_ref[0], rows)
    n_chunks = wd_ref.shape[0]

    def cols(ref, c):
        off = c * FF_CHUNK
        return ref[:, pl.ds(off if isinstance(off, int) else pl.multiple_of(off, FF_CHUNK), FF_CHUNK)]
    tiles = rows // V7X_SUBLANES
    sub = lax.broadcasted_iota(jnp.int32, (tiles, V7X_SUBLANES, FF_CHUNK), 1)

    def up_proj(c, slot):
        hm = hm_ref[...]
        u_ref[slot, 0] = _dot(hm, cols(wu_ref, c)).reshape(tiles + 2, V7X_SUBLANES, FF_CHUNK)
        u_ref[slot, 1] = _dot(hm, cols(wu_ref, c + n_chunks)).reshape(tiles + 2, V7X_SUBLANES, FF_CHUNK)

    def conv_gate(c, slot):
        def conv(half):
            u = u_ref[slot, half]
            dn = pltpu.roll(u, 1, 1)
            up = pltpu.roll(u, V7X_SUBLANES - 1, 1)
            prev = jnp.where(sub == 0, dn[0:tiles], dn[1:tiles + 1])
            nxt = jnp.where(sub == V7X_SUBLANES - 1, up[2:tiles + 2], up[1:tiles + 1])
            w = cols(cw_ref, c + half * n_chunks)
            return cols(cb_ref, c + half * n_chunks) + w[0:1] * prev + w[1:2] * u[1:tiles + 1] + w[2:3] * nxt

        a1 = conv(0)
        a2 = conv(1)
        gt_ref[c] = (a1 * (a2 + a2 * jnp.tanh(a2))).reshape(rows, FF_CHUNK)

    up_proj(0, 0)

    def pair(k, carry):
        c = 2 * k
        up_proj(c + 1, 1)
        conv_gate(c, 0)
        up_proj(c + 2, 0)
        conv_gate(c + 1, 1)
        return carry

    lax.fori_loop(0, (n_chunks - 1) // 2, pair, 0)
    conv_gate(n_chunks - 1, 0)

    f = _dot(gt_ref[0].astype(BF16), wd_ref[0])
    for c in range(1, n_chunks):
        f = f + _dot(gt_ref[c].astype(BF16), wd_ref[c])
    o_ref[0] = _deepnorm(x_ref[0], g_ref[0], f, lng_ref[...], lnb_ref[...])


def _ffn(x, sh, sc, gate, wu, cw, cb, wd, layer, lng, lnb, rows):
    bsz, seq, d = x.shape
    nc = wd.shape[1]
    dff = nc * FF_CHUNK
    assert wu.shape[2] == 2 * dff and nc % 2 == 1, "the pipeline peels one chunk and pairs the rest"

    def of_layer(*shape):
        nd = len(shape)
        return pl.BlockSpec((None,) + shape, lambda b, t: (layer,) + (0,) * nd, pipeline_mode=pl.Buffered(1))

    est = (3 * nc * d * FF_CHUNK * 2 + 4 * rows * d * 4 + (rows + 2 * HALO) * d * 2
           + nc * rows * FF_CHUNK * 4 + 8 * (rows + 2 * HALO) * FF_CHUNK * 4 + 2 * rows * d * 4)
    return pl.pallas_call(
        functools.partial(_ffn_kernel, rows=rows),
        out_shape=jax.ShapeDtypeStruct((bsz, seq, d), F32),
        grid=(bsz, seq // rows),
        in_specs=_halo_specs(rows, seq, d) + [
            _row_spec(d), _row_spec(d), _row_spec(d),
            of_layer(d, 2 * dff), of_layer(CONV_W, 2 * dff), of_layer(1, 2 * dff),
            of_layer(nc, FF_CHUNK, d),
            _resident((1, d)), _resident((1, d)),
        ],
        out_specs=pl.BlockSpec((1, rows, d), lambda b, t: (b, t, 0)),
        scratch_shapes=[
            pltpu.VMEM((rows + 2 * HALO, d), BF16),
            pltpu.VMEM((2, 2, rows // HALO + 2, HALO, FF_CHUNK), F32),
            pltpu.VMEM((nc, rows, FF_CHUNK), F32),
        ],
        compiler_params=pltpu.CompilerParams(
            dimension_semantics=("arbitrary", "arbitrary"),
            vmem_limit_bytes=_vmem_limit(est)),
        name="conv_ffn",
    )(x, x, x, sh, sc, gate, wu, cw, cb, wd, lng, lnb)


def _cd_in_kernel(*refs, with_fz, rows):
    x_ref, xp_ref, xn_ref, sh_ref, sc_ref = refs[:5]
    if with_fz:
        wfz_ref, wx_ref, wdt_ref, cw_ref, cb_ref, dtb_ref = refs[5:11]
        f_ref, z_ref, xs_ref, b_ref, c_ref, dt_ref, hm_ref = refs[11:]
    else:
        wx_ref, wdt_ref, cw_ref, cb_ref, dtb_ref = refs[5:10]
        xs_ref, b_ref, c_ref, dt_ref, hm_ref = refs[10:]
    _fill_halo_tile(hm_ref, x_ref, xp_ref, xn_ref, sh_ref[0], sc_ref[0], rows)
    hc = hm_ref[HALO:HALO + rows, :]
    if with_fz:
        pfz = _dot(hc, wfz_ref[...])
        f_ref[0] = pfz[:, 0:FN_WIDTH].astype(BF16)
        z_ref[0] = pfz[:, FN_WIDTH:SSM_OFF]
    a = _conv3(_dot(hm_ref[...], wx_ref[...]), cw_ref[...], cb_ref[...], rows)
    s = a + a * jnp.tanh(a)
    xs_ref[0] = s[:, 0:SSD_WIDTH]
    b_ref[0] = s[:, SSD_WIDTH:SSD_WIDTH + SSD_GN].astype(BF16)
    c_ref[0] = s[:, SSD_WIDTH + SSD_GN:SSD_CONV_DIM].astype(BF16)
    dt_ref[0] = _softplus(_dot(hc, wdt_ref[...]) + dtb_ref[...])


def _cd_in(x, sh, sc, wfz, wx, wdt, cw, cb, dtb, rows):
    bsz, seq, d = x.shape
    with_fz = wfz is not None
    in_specs = _halo_specs(rows, seq, d) + [_row_spec(d), _row_spec(d)]
    args = [x, x, x, sh, sc]
    out_shape, out_specs = [], []
    if with_fz:
        in_specs.append(_resident((d, SSM_OFF)))
        args.append(wfz)
        out_shape += [jax.ShapeDtypeStruct((bsz, seq, FN_WIDTH), BF16),
                      jax.ShapeDtypeStruct((bsz, seq, SSD_WIDTH), F32)]
        out_specs += [pl.BlockSpec((1, rows, FN_WIDTH), lambda b, t: (b, t, 0)),
                      pl.BlockSpec((1, rows, SSD_WIDTH), lambda b, t: (b, t, 0))]
    in_specs += [_resident((d, SSD_CONV_DIM)), _resident((d, V7X_LANES)),
                 _resident((CONV_W, SSD_CONV_DIM)), _resident((1, SSD_CONV_DIM)),
                 _resident((1, V7X_LANES))]
    args += [wx, wdt, cw, cb, dtb]
    out_shape += [jax.ShapeDtypeStruct((bsz, seq, SSD_WIDTH), F32),
                  jax.ShapeDtypeStruct((bsz, seq, SSD_GN), BF16),
                  jax.ShapeDtypeStruct((bsz, seq, SSD_GN), BF16),
                  jax.ShapeDtypeStruct((bsz, seq, V7X_LANES), F32)]
    out_specs += [pl.BlockSpec((1, rows, SSD_WIDTH), lambda b, t: (b, t, 0)),
                  pl.BlockSpec((1, rows, SSD_GN), lambda b, t: (b, t, 0)),
                  pl.BlockSpec((1, rows, SSD_GN), lambda b, t: (b, t, 0)),
                  pl.BlockSpec((1, rows, V7X_LANES), lambda b, t: (b, t, 0))]
    est = d * 2200 * 2 + 2 * rows * d * 4 + 8 * (rows + 2 * HALO) * SSD_CONV_DIM * 4
    return pl.pallas_call(
        functools.partial(_cd_in_kernel, with_fz=with_fz, rows=rows),
        out_shape=out_shape,
        grid=(bsz, seq // rows),
        in_specs=in_specs,
        out_specs=out_specs,
        scratch_shapes=[pltpu.VMEM((rows + 2 * HALO, d), BF16)],
        compiler_params=pltpu.CompilerParams(
            dimension_semantics=("arbitrary", "arbitrary"),
            vmem_limit_bytes=_vmem_limit(est)),
        name="cd_in" if with_fz else "cd_in_ctx",
    )(*args)


def _cumsum_rows(z, reverse):
    n = z.shape[0]
    idx = lax.broadcasted_iota(jnp.int32, z.shape, 0)
    k = 1
    while k < n:
        if reverse:
            z = z + jnp.where(idx < n - k, pltpu.roll(z, n - k, 0), 0.0)
        else:
            z = z + jnp.where(idx >= k, pltpu.roll(z, k, 0), 0.0)
        k *= 2
    return z


def _lane_expand(row, lane0, width):
    lane = lax.broadcasted_iota(jnp.int32, (1, SSD_HPG * width), 1)
    out = jnp.zeros((1, SSD_HPG * width), F32)
    for j in range(SSD_HPG):
        val = jnp.broadcast_to(row[:, lane0 + j:lane0 + j + 1], (1, SSD_HPG * width))
        out = jnp.where(lane // width == j, val, out)
    return out


def _ssd_chunk(x, bm, cm, dt, a_row, h_ref, lane0, reverse, need_y):
    q = x.shape[0]
    gw = SSD_HPG * SSD_HEADDIM
    acum = _cumsum_rows(dt * a_row, reverse)
    total = acum[0:1, :] if reverse else acum[q - 1:q, :]
    acum_t = acum.T
    dt_t = dt.T
    total_t = jnp.broadcast_to(acum_t[:, 0:1] if reverse else acum_t[:, q - 1:q], acum_t.shape)
    wgt_t = dt_t * jnp.exp2(total_t - acum_t)
    lane_g = lax.broadcasted_iota(jnp.int32, (q, gw), 1) // SSD_HEADDIM
    lane_gh = lax.broadcasted_iota(jnp.int32, (SSD_STATE, gw), 1) // SSD_HEADDIM
    if need_y:
        srow_t = acum_t - jnp.log2(dt_t)
        ti = lax.broadcasted_iota(jnp.int32, (q, q), 0)
        si = lax.broadcasted_iota(jnp.int32, (q, q), 1)
        mask = (si >= ti) if reverse else (si <= ti)
    ys = []
    for g in range(SSD_GROUPS):
        bg = bm[:, g * SSD_STATE:(g + 1) * SSD_STATE]
        xgb = x[:, g * gw:(g + 1) * gw].astype(BF16)
        x_heads = [jnp.where(lane_g == j, xgb, jnp.zeros_like(xgb)) for j in range(SSD_HPG)]
        hg = h_ref[g]
        if need_y:
            cg = cm[:, g * SSD_STATE:(g + 1) * SSD_STATE]
            cb = _dot_nt(cg, bg)
            cgf = cg.astype(F32)
            hgb = hg.astype(BF16)
            lhs, rhs = [], []
            for j in range(SSD_HPG):
                jl = lane0 + g * SSD_HPG + j
                col = jnp.broadcast_to(acum[:, jl:jl + 1], (q, q))
                row = jnp.broadcast_to(srow_t[jl:jl + 1, :], (q, q))
                lhs.append((cb * jnp.exp2(jnp.where(mask, col - row, -jnp.inf))).astype(BF16))
                lhs.append((cgf * jnp.exp2(col[:, 0:SSD_STATE])).astype(BF16))
                rhs.append(x_heads[j])
                rhs.append(jnp.where(lane_gh == j, hgb, jnp.zeros_like(hgb)))
            ys.append(_dot(jnp.concatenate(lhs, axis=1), jnp.concatenate(rhs, axis=0)))
        bgt = bg.astype(F32).T
        wb = [(bgt * jnp.broadcast_to(wgt_t[lane0 + g * SSD_HPG + j:lane0 + g * SSD_HPG + j + 1, :],
                                      (SSD_STATE, q))).astype(BF16) for j in range(SSD_HPG)]
        st = _dot(jnp.concatenate(wb, axis=1), jnp.concatenate(x_heads, axis=0))
        h_ref[g] = hg * jnp.exp2(_lane_expand(total, lane0 + g * SSD_HPG, SSD_HEADDIM)) + st
    if need_y:
        return jnp.concatenate(ys, axis=1)
    return None


def _ssd_kernel(xc_ref, bc_ref, dtc_ref, xl_ref, bl_ref, cl_ref, dtl_ref, z_ref, alog_ref,
                dskip_ref, ng_ref, o_ref, y_ref, h_ref):
    a_row = -jnp.exp(alog_ref[...]) * math.log2(math.e)
    nc_ctx = xc_ref.shape[1] // SSD_CHUNK
    nc_lat = xl_ref.shape[1] // SSD_CHUNK

    def rows_of(c):
        return pl.ds(pl.multiple_of(c * SSD_CHUNK, SSD_CHUNK), SSD_CHUNK)

    def ctx_step(c, direction):
        r = rows_of(c)
        _ssd_chunk(xc_ref[0, r, :], bc_ref[0, r, :], None, dtc_ref[0, r, :], a_row,
                   h_ref.at[direction], direction * SSD_HEADS, direction == 1, False)

    def lat_step(c, direction):
        r = rows_of(c)
        y_ref[direction, r, :] = _ssd_chunk(
            xl_ref[0, r, :], bl_ref[0, r, :], cl_ref[0, r, :], dtl_ref[0, r, :], a_row,
            h_ref.at[direction], direction * SSD_HEADS, direction == 1, True)

    h_ref[...] = jnp.zeros_like(h_ref)

    def ctx_pair(i, carry):
        ctx_step(i, 0)
        ctx_step(nc_ctx - 1 - i, 1)
        return carry

    lax.fori_loop(0, nc_ctx, ctx_pair, 0)

    def lat_pair(i, carry):
        lat_step(i, 0)
        lat_step(nc_lat - 1 - i, 1)
        return carry

    lax.fori_loop(0, nc_lat, lat_pair, 0)

    def finish(c, carry):
        r = rows_of(c)
        zz = z_ref[0, r, :]
        yt = (y_ref[0, r, :] + y_ref[1, r, :] + xl_ref[0, r, :] * dskip_ref[...]) * _silu(zz)
        o_ref[0, r, :] = _rms_norm(yt, ng_ref[...]).astype(BF16)
        return carry

    lax.fori_loop(0, nc_lat, finish, 0)


def _ssd(xs_c, b_c, dt_c, xs_l, b_l, c_l, dt_l, z, alog_row, dskip_row, ng_row):
    bsz, sl, w = xs_l.shape
    sc = xs_c.shape[1]
    assert sl % SSD_CHUNK == 0 and sc % SSD_CHUNK == 0
    gw = SSD_HPG * SSD_HEADDIM

    def full(s, n):
        return pl.BlockSpec((1, s, n), lambda b: (b, 0, 0))

    est = (2 * (sl + sc) * (w * 4 + 2 * SSD_GN * 2 + V7X_LANES * 4) + 2 * sl * w * 4
           + 2 * sl * w * 2 + 2 * sl * w * 4)
    return pl.pallas_call(
        _ssd_kernel,
        out_shape=jax.ShapeDtypeStruct((bsz, sl, w), BF16),
        grid=(bsz,),
        in_specs=[
            full(sc, w), full(sc, SSD_GN), full(sc, V7X_LANES),
            full(sl, w), full(sl, SSD_GN), full(sl, SSD_GN), full(sl, V7X_LANES),
            full(sl, w),
            _resident((1, V7X_LANES)), _resident((1, w)), _resident((1, w)),
        ],
        out_specs=full(sl, w),
        scratch_shapes=[
            pltpu.VMEM((2, sl, w), F32),
            pltpu.VMEM((2, SSD_GROUPS, SSD_STATE, gw), F32),
        ],
        compiler_params=pltpu.CompilerParams(
            dimension_semantics=("arbitrary",),
            vmem_limit_bytes=_vmem_limit(est)),
        name="ssd_scan",
    )(xs_c, b_c, dt_c, xs_l, b_l, c_l, dt_l, z, alog_row, dskip_row, ng_row)


def _fourier_kernel(f_ref, cc_ref, sc_ref, w2_ref, o_ref, y_ref, *, rows):
    seq = f_ref.shape[1]
    for g in range(FN_GROUPS):
        l = slice(g * FN_GDIM, (g + 1) * FN_GDIM)
        fg = f_ref[0, :, l]
        y_ref[0:seq, l] = _dot(fg, cc_ref[...]).astype(BF16)
        y_ref[seq:2 * seq, l] = _dot(fg, sc_ref[...]).astype(BF16)
    norm = 1.0 / math.sqrt(seq * FN_GDIM)

    def tile(i, carry):
        r = pl.ds(pl.multiple_of(i * rows, rows), rows)
        o_ref[0, r, :] = (_dot(w2_ref[r, :], y_ref[...]) * norm).astype(BF16)
        return carry

    lax.fori_loop(0, seq // rows, tile, 0)


def _fourier(f, cc, sc, w2, rows):
    bsz, seq, w = f.shape
    est = seq * 2 * seq * 2 + 4 * seq * w * 2 + 2 * seq * w * 2 + 4 * rows * 2 * seq * 2
    return pl.pallas_call(
        functools.partial(_fourier_kernel, rows=rows),
        out_shape=jax.ShapeDtypeStruct((bsz, seq, w), BF16),
        grid=(bsz,),
        in_specs=[
            pl.BlockSpec((1, seq, w), lambda b: (b, 0, 0)),
            _resident((FN_GDIM, FN_GDIM)), _resident((FN_GDIM, FN_GDIM)),
            _resident((seq, 2 * seq)),
        ],
        out_specs=pl.BlockSpec((1, seq, w), lambda b: (b, 0, 0)),
        scratch_shapes=[pltpu.VMEM((2 * seq, w), BF16)],
        compiler_params=pltpu.CompilerParams(
            dimension_semantics=("arbitrary",),
            vmem_limit_bytes=_vmem_limit(est)),
        name="fourier_mix",
    )(f, cc, sc, w2)


def _rope_tables(seq):
    rows = seq // GRID_W
    row = jnp.repeat(jnp.arange(rows, dtype=F32), GRID_W)
    col = jnp.broadcast_to(jnp.arange(GRID_W, dtype=F32), (rows, GRID_W)).reshape(-1)
    n_freq = HEAD_DIM // 4
    inv = ROPE_THETA ** (-jnp.arange(n_freq, dtype=F32) / n_freq)
    ang = jnp.stack([row, col], axis=-1)[:, :, None] * inv
    cos, sin = jnp.cos(ang), jnp.sin(ang)
    cos_t = jnp.concatenate([cos[:, 0], cos[:, 1], cos[:, 0], cos[:, 1]], axis=-1)
    sin_t = jnp.concatenate([-sin[:, 0], -sin[:, 1], sin[:, 0], sin[:, 1]], axis=-1)
    return cos_t, sin_t


def _rope_head_order(m):
    lead = m.shape[:-1]
    nh = m.shape[-1] // HEAD_DIM
    m = m.reshape(lead + (nh, 2, 2, HEAD_DIM // 4))
    return jnp.swapaxes(m, -3, -2).reshape(lead + (nh * HEAD_DIM,))


def _dft_tables(seq):
    def tw(n):
        i = np.arange(n, dtype=np.int64)
        ang = ((i[:, None] * i[None, :]) % n).astype(np.float64) * (2.0 * math.pi / n)
        return np.cos(ang), np.sin(ang)

    cc, sc = tw(FN_GDIM)
    cl, sl = tw(seq)
    w2 = np.concatenate([cl, -sl], axis=1).astype(np.float32)
    return (jnp.asarray(cc.astype(np.float32), BF16), jnp.asarray(sc.astype(np.float32), BF16),
            jnp.asarray(w2, BF16))


def _ffn_params(w_up, conv_w, conv_b, w_down):
    depth, d, _ = w_up.shape
    half_gate = jnp.concatenate([jnp.ones((D_FF,), F32), jnp.full((D_FF,), 0.5, F32)])
    return (w_up.astype(BF16), conv_w * half_gate, (conv_b * half_gate).reshape(depth, 1, 2 * D_FF),
            w_down.astype(BF16).reshape(depth, D_FF // FF_CHUNK, FF_CHUNK, d))


def kernel(x, c, ctx, c_ctx, mod_w, mod_b, ln1_g, ln1_b, ln2_g, ln2_b, ffn_up, ffn_conv_w, ffn_conv_b,
           ffn_down, ab_w_in, ab_w_out, gm_ln_g, gm_ln_b, gm_ws, gm_bs, q_norm_g, k_norm_g, cd_w_in,
           cd_w_out, ssd_conv_w, ssd_conv_b, ssd_dt_bias, ssd_a_log, ssd_d, ssd_norm_g):
    bsz, seq, d = x.shape
    ctx_len = ctx.shape[1]
    lat_rows = min(1024, seq)
    out_rows = min(2048, seq)
    dft_rows = min(2048, seq)
    ctx_rows = min(256, ctx_len)

    pad = (-(bsz + 1)) % V7X_SUBLANES
    cc = jnp.concatenate([c, c_ctx[None], jnp.zeros((pad, d), F32)], axis=0)
    mod = _modulation(cc, mod_w, mod_b)
    rope_tabs = _rope_tables(seq)

    def row(v):
        return v.reshape(1, -1)

    ffn_p = _ffn_params(ffn_up, ffn_conv_w, ffn_conv_b, ffn_down)
    xl, xc = x, ctx
    for i in range(DEPTH):
        last = i == DEPTH - 1
        j = i // 2
        ml = mod[i, :bsz].reshape(bsz, 6, 1, d)
        mc = jnp.broadcast_to(mod[i, bsz].reshape(1, 6, 1, d), (bsz, 6, 1, d))
        sh1, sc1, g1, sh2, sc2, g2 = (ml[:, k] for k in range(6))
        csh1, csc1, cg1, csh2, csc2, cg2 = (mc[:, k] for k in range(6))
        ln1 = (row(ln1_g[i]), row(ln1_b[i]))
        ln2 = (row(ln2_g[i]), row(ln2_b[i]))

        if i % 2 == 0:
            w_in = ab_w_in[j].astype(BF16)
            qk_lo, qk_hi = 2 * GM_WIDTH, KV_OFF + KV_W
            q_scale = HEAD_DIM ** -0.5 * math.log2(math.e)
            w_in = jnp.concatenate([0.5 * w_in[:, :qk_lo], _rope_head_order(w_in[:, qk_lo:qk_hi]),
                                    w_in[:, qk_hi:]], axis=1)
            w_out = ab_w_out[j].astype(BF16)
            gm = (row(gm_ln_g[j]), row(gm_ln_b[j]), gm_ws[j].astype(BF16),
                  jnp.repeat(gm_bs[j].T, GM_WIDTH // GM_GROUPS, axis=1),
                  row(_rope_head_order(q_norm_g[j]) * q_scale), row(_rope_head_order(k_norm_g[j])))
            kv_len = seq + ctx_len
            a_c, q_c, k_all, v_all = _ab_in(xc, csh1, csc1, w_in, *gm, None, ctx_rows, kv_len, seq)
            a_l, q_l, k_all, v_all = _ab_in(xl, sh1, sc1, w_in, *gm, rope_tabs, out_rows, kv_len, 0,
                                            kv_into=(k_all, v_all))
            mix_l = (a_l, _attention(q_l, k_all, v_all, 0, kv_len, out_rows))
            if not last:
                mix_c = (a_c, _attention(q_c, k_all, v_all, seq, ctx_len, ctx_rows))
        else:
            w_in = cd_w_in[j].astype(BF16)
            w_out = cd_w_out[j].astype(BF16)
            wfz = w_in[:, :SSM_OFF]
            wx = w_in[:, SSM_OFF:SSM_OFF + SSD_CONV_DIM]
            ndt = 2 * SSD_HEADS
            wdt = jnp.pad(w_in[:, SSM_OFF + SSD_CONV_DIM:], ((0, 0), (0, V7X_LANES - ndt)))
            dtb = jnp.pad(ssd_dt_bias[j].reshape(1, ndt), ((0, 0), (0, V7X_LANES - ndt)))
            alog = jnp.pad(ssd_a_log[j].reshape(1, ndt), ((0, 0), (0, V7X_LANES - ndt)))
            cw, cb = 0.5 * ssd_conv_w[j], 0.5 * row(ssd_conv_b[j])
            dskip = row(jnp.repeat(ssd_d[j], SSD_HEADDIM))
            f_l, z_l, xs_l, b_l, c_l, dt_l = _cd_in(xl, sh1, sc1, wfz, wx, wdt, cw, cb, dtb, lat_rows)
            if last:
                xs_c, b_c, _, dt_c = _cd_in(xc, csh1, csc1, None, wx, wdt, cw, cb, dtb, ctx_rows)
            else:
                raise NotImplementedError("odd layers with a context output are not part of this block")
            s_l = _ssd(xs_c, b_c, dt_c, xs_l, b_l, c_l, dt_l, z_l, alog, dskip, row(ssd_norm_g[j]))
            mix_l = (_fourier(f_l, *_dft_tables(seq), dft_rows), s_l)

        xl = _out_ln(*mix_l, w_out, xl, g1, *ln1, out_rows)
        xl = _ffn(xl, sh2, sc2, g2, *ffn_p, i, *ln2, lat_rows)
        if not last:
            xc = _out_ln(*mix_c, w_out, xc, cg1, *ln1, ctx_rows)
            xc = _ffn(xc, csh2, csc2, cg2, *ffn_p, i, *ln2, ctx_rows)
    return xl
```

```python
import functools
import math

import jax
import jax.numpy as jnp
import numpy as np
from jax import lax
from jax.experimental import pallas as pl
from jax.experimental.pallas import tpu as pltpu

D_MODEL = 1024
DEPTH = 2
GRID_W = 64
CHUNK = 128
GM_GROUPS = 4
GM_WIDTH = 512
HEAD_DIM = 128
N_Q_HEADS = 4
N_KV_HEADS = 2
ATTN_WIDTH = 512
KV_W = 256
ROPE_THETA = 10000.0
FN_GROUPS = 4
FN_GDIM = 128
FN_WIDTH = 512
SSD_HEADDIM = 64
SSD_HEADS = 8
SSD_GROUPS = 2
SSD_HPG = 4
SSD_STATE = 128
SSD_WIDTH = 512
SSD_GN = 256
SSD_CONV_DIM = 1024
D_FF = 2816
CONV_W = 3
KV_OFF = 2 * GM_WIDTH + ATTN_WIDTH
SSM_OFF = FN_WIDTH + SSD_WIDTH
DN_ALPHA = (2 * DEPTH) ** 0.25
EPS = 1e-6

V7X_LANES = 128
V7X_SUBLANES = 8
V7X_VMEM_BYTES = 64 * 1024 * 1024
VMEM_CHIP_HEADROOM = 3 << 20
VMEM_TEMP_ALLOWANCE = 8 << 20
MOD_COLS = 2048
AB_SUB = 256
FF_CHUNK = 256
SSD_CHUNK = 256

BF16 = jnp.bfloat16
F32 = jnp.float32
HALO = V7X_SUBLANES


def _vmem_limit(nbytes):
    return int(min(nbytes * 3 // 2 + VMEM_TEMP_ALLOWANCE, V7X_VMEM_BYTES - VMEM_CHIP_HEADROOM))


def _resident(shape):
    nd = len(shape)
    return pl.BlockSpec(shape, lambda *_: (0,) * nd, pipeline_mode=pl.Buffered(1))


def _dot(a, b):
    return jnp.dot(a, b, preferred_element_type=F32)


def _dot_nt(a, b):
    return lax.dot_general(a, b, (((1,), (1,)), ((), ())), preferred_element_type=F32)


def _layer_norm(z, g, b, eps=EPS):
    mu = jnp.mean(z, axis=-1, keepdims=True)
    zc = z - mu
    var = jnp.mean(zc * zc, axis=-1, keepdims=True)
    return zc * lax.rsqrt(var + eps) * g + b


def _deepnorm(x, gate, y, g, b):
    return _layer_norm(x + (gate * (1.0 / DN_ALPHA)) * y, g, b, eps=EPS / DN_ALPHA ** 2)


def _rms_norm(z, g):
    return z * lax.rsqrt(jnp.mean(z * z, axis=-1, keepdims=True) + EPS) * g


def _softplus(z):
    return jnp.maximum(z, 0.0) + jnp.log1p(jnp.exp(-jnp.abs(z)))


def _gelu_tanh_of_half(h):
    c1 = math.sqrt(2.0 / math.pi)
    return h + h * jnp.tanh(h * (2.0 * c1 + (8.0 * c1 * 0.044715) * (h * h)))


def _silu(z):
    h = 0.5 * z
    return h + h * jnp.tanh(h)


def _conv3(u, w, b, rows):
    tiles, c = rows // V7X_SUBLANES, u.shape[1]
    u3 = u.reshape(tiles + 2, V7X_SUBLANES, c)
    sub = lax.broadcasted_iota(jnp.int32, (tiles, V7X_SUBLANES, c), 1)
    dn = pltpu.roll(u3, 1, 1)
    up = pltpu.roll(u3, V7X_SUBLANES - 1, 1)
    prev = jnp.where(sub == 0, dn[0:tiles], dn[1:tiles + 1])
    nxt = jnp.where(sub == V7X_SUBLANES - 1, up[2:tiles + 2], up[1:tiles + 1])
    return (b + w[0:1] * prev + w[1:2] * u3[1:tiles + 1] + w[2:3] * nxt).reshape(rows, c)


def _fill_halo_tile(hm_ref, x_ref, xp_ref, xn_ref, sh, sc, rows):
    t = pl.program_id(1)
    nt = pl.num_programs(1)
    keep_prev = jnp.where(t > 0, 1.0, 0.0)
    keep_next = jnp.where(t < nt - 1, 1.0, 0.0)
    hm_ref[HALO:HALO + rows, :] = (x_ref[0] * (1.0 + sc) + sh).astype(BF16)
    hm_ref[0:HALO, :] = ((xp_ref[0] * (1.0 + sc) + sh) * keep_prev).astype(BF16)
    hm_ref[HALO + rows:2 * HALO + rows, :] = ((xn_ref[0] * (1.0 + sc) + sh) * keep_next).astype(BF16)


def _halo_specs(rows, seq, d):
    nb = rows // HALO
    last = seq // HALO - 1
    return [
        pl.BlockSpec((1, rows, d), lambda b, t: (b, t, 0)),
        pl.BlockSpec((1, HALO, d), lambda b, t: (b, jnp.maximum(t * nb - 1, 0), 0)),
        pl.BlockSpec((1, HALO, d), lambda b, t: (b, jnp.minimum((t + 1) * nb, last), 0)),
    ]


def _row_spec(d):
    return pl.BlockSpec((1, 1, d), lambda b, t: (b, 0, 0))


def _mod_kernel(c_ref, w_ref, b_ref, o_ref):
    c = c_ref[...]
    act = c * jax.nn.sigmoid(c)
    o_ref[0] = jnp.dot(act, w_ref[0], preferred_element_type=F32,
                       precision=lax.Precision.HIGHEST) + b_ref[0]


def _modulation(cc, mod_w, mod_b):
    rows, d = cc.shape
    depth, _, n = mod_w.shape
    tn = MOD_COLS
    return pl.pallas_call(
        _mod_kernel,
        out_shape=jax.ShapeDtypeStruct((depth, rows, n), F32),
        grid=(depth, n // tn),
        in_specs=[
            pl.BlockSpec((rows, d), lambda i, j: (0, 0)),
            pl.BlockSpec((1, d, tn), lambda i, j: (i, 0, j)),
            pl.BlockSpec((1, 1, tn), lambda i, j: (i, 0, j)),
        ],
        out_specs=pl.BlockSpec((1, rows, tn), lambda i, j: (i, 0, j)),
        compiler_params=pltpu.CompilerParams(
            dimension_semantics=("arbitrary", "arbitrary"),
            vmem_limit_bytes=_vmem_limit(2 * d * tn * 4)),
        name="modulation",
    )(cc, mod_w, mod_b.reshape(depth, 1, n))


def _ab_in_kernel(*refs, use_rope, rows):
    (x_ref, sh_ref, sc_ref, w_ref, lng_ref, lnb_ref, ws_ref, bias_ref, qg_ref, kg_ref) = refs[:10]
    if use_rope:
        cos_ref, sin_ref = refs[10:12]
    a_ref, q_ref, k_ref, v_ref = refs[-4:]

    gd = GM_WIDTH // GM_GROUPS
    slab_rows = min(AB_SUB, rows)

    def slab(i, carry):
        r0 = pl.multiple_of(i * slab_rows, slab_rows)
        rs = pl.ds(r0, slab_rows)
        h = (x_ref[0, rs, :] * (1.0 + sc_ref[0]) + sh_ref[0]).astype(BF16)

        u = _gelu_tanh_of_half(_dot(h, w_ref[:, 0:GM_WIDTH]))
        vn = _layer_norm(_gelu_tanh_of_half(_dot(h, w_ref[:, GM_WIDTH:2 * GM_WIDTH])),
                         lng_ref[...], lnb_ref[...])
        vb = vn.astype(BF16)
        for c in range(slab_rows // CHUNK):
            r = slice(c * CHUNK, (c + 1) * CHUNK)
            for g in range(GM_GROUPS):
                l = slice(g * gd, (g + 1) * gd)
                mixed = _dot(ws_ref[g], vb[r, l]) + bias_ref[:, l]
                a_ref[0, pl.ds(r0 + c * CHUNK, CHUNK), l] = (u[r, l] * mixed).astype(BF16)

        def rope(z):
            if not use_rope:
                return z
            return z * cos_ref[rs, :] + pltpu.roll(z, HEAD_DIM // 2, 1) * sin_ref[rs, :]

        pq = _dot(h, w_ref[:, 2 * GM_WIDTH:KV_OFF])
        for hh in range(N_Q_HEADS):
            l = slice(hh * HEAD_DIM, (hh + 1) * HEAD_DIM)
            q_ref[0, rs, l] = rope(_rms_norm(pq[:, l], qg_ref[...])).astype(BF16)
        pk = _dot(h, w_ref[:, KV_OFF:KV_OFF + KV_W])
        for hh in range(N_KV_HEADS):
            l = slice(hh * HEAD_DIM, (hh + 1) * HEAD_DIM)
            k_ref[0, rs, l] = rope(_rms_norm(pk[:, l], kg_ref[...])).astype(BF16)
        v_ref[0, rs, :] = _dot(h, w_ref[:, KV_OFF + KV_W:KV_OFF + 2 * KV_W]).astype(BF16)
        return carry

    lax.fori_loop(0, rows // slab_rows, slab, 0)


def _ab_in(x, sh, sc, w, lng, lnb, ws, bias2d, qg, kg, rope_tabs, rows, kv_len, kv_row0, kv_into=None):
    bsz, seq, d = x.shape
    assert kv_row0 % rows == 0
    kv_blk0 = kv_row0 // rows
    n = w.shape[1]
    use_rope = rope_tabs is not None
    in_specs = [
        pl.BlockSpec((1, rows, d), lambda b, t: (b, t, 0)),
        _row_spec(d), _row_spec(d),
        _resident((d, n)),
        _resident((1, GM_WIDTH)), _resident((1, GM_WIDTH)),
        _resident((GM_GROUPS, CHUNK, CHUNK)),
        _resident((CHUNK, GM_WIDTH)),
        _resident((1, HEAD_DIM)), _resident((1, HEAD_DIM)),
    ]
    args = [x, sh, sc, w, lng, lnb, ws, bias2d, qg, kg]
    if use_rope:
        in_specs += [pl.BlockSpec((rows, HEAD_DIM), lambda b, t: (t, 0))] * 2
        args += list(rope_tabs)
    aliases = {}
    if kv_into is not None:
        aliases = {len(args): 2, len(args) + 1: 3}
        in_specs += [pl.BlockSpec(memory_space=pl.ANY)] * 2
        args += list(kv_into)
    out_shape = [
        jax.ShapeDtypeStruct((bsz, seq, GM_WIDTH), BF16),
        jax.ShapeDtypeStruct((bsz, seq, ATTN_WIDTH), BF16),
        jax.ShapeDtypeStruct((bsz, kv_len, KV_W), BF16),
        jax.ShapeDtypeStruct((bsz, kv_len, KV_W), BF16),
    ]
    out_specs = [
        pl.BlockSpec((1, rows, GM_WIDTH), lambda b, t: (b, t, 0)),
        pl.BlockSpec((1, rows, ATTN_WIDTH), lambda b, t: (b, t, 0)),
        pl.BlockSpec((1, rows, KV_W), lambda b, t: (b, t + kv_blk0, 0)),
        pl.BlockSpec((1, rows, KV_W), lambda b, t: (b, t + kv_blk0, 0)),
    ]
    est = d * n * 2 + 2 * rows * d * 4 + 6 * rows * n * 4
    return pl.pallas_call(
        functools.partial(_ab_in_kernel, use_rope=use_rope, rows=rows),
        out_shape=out_shape,
        grid=(bsz, seq // rows),
        in_specs=in_specs,
        out_specs=out_specs,
        input_output_aliases=aliases,
        compiler_params=pltpu.CompilerParams(
            dimension_semantics=("arbitrary", "arbitrary"),
            vmem_limit_bytes=_vmem_limit(est)),
        name="ab_in_rope" if use_rope else "ab_in_ctx",
    )(*args)


def _attn_kernel(q_ref, k_ref, v_ref, o_ref, vt_ref, *, rows, blk):
    @pl.when(pl.program_id(2) == 0)
    def _():
        vt_ref[...] = v_ref[0].astype(F32).T

    k = k_ref[0]
    vt = vt_ref[...].astype(BF16)
    for h in range(N_Q_HEADS // N_KV_HEADS):
        l_ = slice(h * HEAD_DIM, (h + 1) * HEAD_DIM)
        for r0 in range(0, rows, blk):
            st = _dot_nt(k, q_ref[0, r0:r0 + blk, l_])
            m = jnp.max(st, axis=0, keepdims=True)
            p = jnp.exp2(st - m)
            l = jnp.sum(p, axis=0, keepdims=True)
            ot = _dot(vt, p.astype(BF16)) / l
            o_ref[0, r0:r0 + blk, l_] = ot.T.astype(BF16)


def _attention(q, k, v, k_row0, sk, rows, blk=2048):
    bsz, sq, _ = q.shape
    assert k_row0 % sk == 0
    kb = k_row0 // sk
    gw = (N_Q_HEADS // N_KV_HEADS) * HEAD_DIM
    est = 4 * blk * sk * 12 + 4 * sk * HEAD_DIM * 2 + 4 * rows * gw * 2 + sk * HEAD_DIM * 4
    return pl.pallas_call(
        functools.partial(_attn_kernel, rows=rows, blk=min(blk, rows)),
        out_shape=jax.ShapeDtypeStruct((bsz, sq, ATTN_WIDTH), BF16),
        grid=(bsz, N_KV_HEADS, sq // rows),
        in_specs=[
            pl.BlockSpec((1, rows, gw), lambda b, h, t: (b, t, h)),
            pl.BlockSpec((1, sk, HEAD_DIM), lambda b, h, t: (b, kb, h)),
            pl.BlockSpec((1, sk, HEAD_DIM), lambda b, h, t: (b, kb, h)),
        ],
        out_specs=pl.BlockSpec((1, rows, gw), lambda b, h, t: (b, t, h)),
        scratch_shapes=[pltpu.VMEM((HEAD_DIM, sk), F32)],
        compiler_params=pltpu.CompilerParams(
            dimension_semantics=("arbitrary", "arbitrary", "arbitrary"),
            vmem_limit_bytes=_vmem_limit(est)),
        name="attention",
    )(q, k, v)


def _out_ln_kernel(a1_ref, a2_ref, w_ref, x_ref, g_ref, lng_ref, lnb_ref, o_ref):
    half = a1_ref.shape[2]
    y = _dot(a1_ref[0], w_ref[0:half, :]) + _dot(a2_ref[0], w_ref[half:2 * half, :])
    o_ref[0] = _deepnorm(x_ref[0], g_ref[0], y, lng_ref[...], lnb_ref[...])


def _out_ln(a1, a2, w, x, gate, lng, lnb, rows):
    bsz, seq, d = x.shape
    half = a1.shape[2]
    est = 2 * half * d * 2 + 4 * rows * d * 4 + 4 * rows * half * 2 + 2 * rows * d * 4
    return pl.pallas_call(
        _out_ln_kernel,
        out_shape=jax.ShapeDtypeStruct((bsz, seq, d), F32),
        grid=(bsz, seq // rows),
        in_specs=[
            pl.BlockSpec((1, rows, half), lambda b, t: (b, t, 0)),
            pl.BlockSpec((1, rows, half), lambda b, t: (b, t, 0)),
            _resident((2 * half, d)),
            pl.BlockSpec((1, rows, d), lambda b, t: (b, t, 0)),
            _row_spec(d),
            _resident((1, d)), _resident((1, d)),
        ],
        out_specs=pl.BlockSpec((1, rows, d), lambda b, t: (b, t, 0)),
        compiler_params=pltpu.CompilerParams(
            dimension_semantics=("arbitrary", "arbitrary"),
            vmem_limit_bytes=_vmem_limit(est)),
        name="out_ln",
    )(a1, a2, w, x, gate, lng, lnb)


def _ffn_kernel(x_ref, xp_ref, xn_ref, sh_ref, sc_ref, g_ref, wu_ref, cw_ref, cb_ref, wd_ref, lng_ref, lnb_ref,
                o_ref, hm_ref, u_ref, gt_ref, *, rows):
    _fill_halo_tile(hm_ref, x_ref, xp_ref, xn_ref, sh_ref[0], sc_ref[0], rows)
    n_chunks = wd_ref.shape[0]

    def cols(ref, c):
        off = c * FF_CHUNK
        return ref[:, pl.ds(off if isinstance(off, int) else pl.multiple_of(off, FF_CHUNK), FF_CHUNK)]
    tiles = rows // V7X_SUBLANES
    sub = lax.broadcasted_iota(jnp.int32, (tiles, V7X_SUBLANES, FF_CHUNK), 1)

    def up_proj(c, slot):
        hm = hm_ref[...]
        u_ref[slot, 0] = _dot(hm, cols(wu_ref, c)).reshape(tiles + 2, V7X_SUBLANES, FF_CHUNK)
        u_ref[slot, 1] = _dot(hm, cols(wu_ref, c + n_chunks)).reshape(tiles + 2, V7X_SUBLANES, FF_CHUNK)

    def conv_gate(c, slot):
        def conv(half):
            u = u_ref[slot, half]
            dn = pltpu.roll(u, 1, 1)
            up = pltpu.roll(u, V7X_SUBLANES - 1, 1)
            prev = jnp.where(sub == 0, dn[0:tiles], dn[1:tiles + 1])
            nxt = jnp.where(sub == V7X_SUBLANES - 1, up[2:tiles + 2], up[1:tiles + 1])
            w = cols(cw_ref, c + half * n_chunks)
            return cols(cb_ref, c + half * n_chunks) + w[0:1] * prev + w[1:2] * u[1:tiles + 1] + w[2:3] * nxt

        a1 = conv(0)
        a2 = conv(1)
        gt_ref[c] = (a1 * (a2 + a2 * jnp.tanh(a2))).reshape(rows, FF_CHUNK)

    up_proj(0, 0)

    def pair(k, carry):
        c = 2 * k
        up_proj(c + 1, 1)
        conv_gate(c, 0)
        up_proj(c + 2, 0)
        conv_gate(c + 1, 1)
        return carry

    lax.fori_loop(0, (n_chunks - 1) // 2, pair, 0)
    conv_gate(n_chunks - 1, 0)

    f = _dot(gt_ref[0].astype(BF16), wd_ref[0])
    for c in range(1, n_chunks):
        f = f + _dot(gt_ref[c].astype(BF16), wd_ref[c])
    o_ref[0] = _deepnorm(x_ref[0], g_ref[0], f, lng_ref[...], lnb_ref[...])


def _ffn(x, sh, sc, gate, wu, cw, cb, wd, layer, lng, lnb, rows):
    bsz, seq, d = x.shape
    nc = wd.shape[1]
    dff = nc * FF_CHUNK
    assert wu.shape[2] == 2 * dff and nc % 2 == 1, "the pipeline peels one chunk and pairs the rest"

    def of_layer(*shape):
        nd = len(shape)
        return pl.BlockSpec((None,) + shape, lambda b, t: (layer,) + (0,) * nd, pipeline_mode=pl.Buffered(1))

    est = (3 * nc * d * FF_CHUNK * 2 + 4 * rows * d * 4 + (rows + 2 * HALO) * d * 2
           + nc * rows * FF_CHUNK * 4 + 8 * (rows + 2 * HALO) * FF_CHUNK * 4 + 2 * rows * d * 4)
    return pl.pallas_call(
        functools.partial(_ffn_kernel, rows=rows),
        out_shape=jax.ShapeDtypeStruct((bsz, seq, d), F32),
        grid=(bsz, seq // rows),
        in_specs=_halo_specs(rows, seq, d) + [
            _row_spec(d), _row_spec(d), _row_spec(d),
            of_layer(d, 2 * dff), of_layer(CONV_W, 2 * dff), of_layer(1, 2 * dff),
            of_layer(nc, FF_CHUNK, d),
            _resident((1, d)), _resident((1, d)),
        ],
        out_specs=pl.BlockSpec((1, rows, d), lambda b, t: (b, t, 0)),
        scratch_shapes=[
            pltpu.VMEM((rows + 2 * HALO, d), BF16),
            pltpu.VMEM((2, 2, rows // HALO + 2, HALO, FF_CHUNK), F32),
            pltpu.VMEM((nc, rows, FF_CHUNK), F32),
        ],
        compiler_params=pltpu.CompilerParams(
            dimension_semantics=("arbitrary", "arbitrary"),
            vmem_limit_bytes=_vmem_limit(est)),
        name="conv_ffn",
    )(x, x, x, sh, sc, gate, wu, cw, cb, wd, lng, lnb)


def _cd_in_kernel(*refs, with_fz, rows):
    x_ref, xp_ref, xn_ref, sh_ref, sc_ref = refs[:5]
    if with_fz:
        wfz_ref, wx_ref, wdt_ref, cw_ref, cb_ref, dtb_ref = refs[5:11]
        f_ref, z_ref, xs_ref, b_ref, c_ref, dt_ref, hm_ref = refs[11:]
    else:
        wx_ref, wdt_ref, cw_ref, cb_ref, dtb_ref = refs[5:10]
        xs_ref, b_ref, c_ref, dt_ref, hm_ref = refs[10:]
    _fill_halo_tile(hm_ref, x_ref, xp_ref, xn_ref, sh_ref[0], sc_ref[0], rows)
    hc = hm_ref[HALO:HALO + rows, :]
    if with_fz:
        pfz = _dot(hc, wfz_ref[...])
        f_ref[0] = pfz[:, 0:FN_WIDTH].astype(BF16)
        z_ref[0] = pfz[:, FN_WIDTH:SSM_OFF]
    a = _conv3(_dot(hm_ref[...], wx_ref[...]), cw_ref[...], cb_ref[...], rows)
    s = a + a * jnp.tanh(a)
    xs_ref[0] = s[:, 0:SSD_WIDTH]
    b_ref[0] = s[:, SSD_WIDTH:SSD_WIDTH + SSD_GN].astype(BF16)
    c_ref[0] = s[:, SSD_WIDTH + SSD_GN:SSD_CONV_DIM].astype(BF16)
    dt_ref[0] = _softplus(_dot(hc, wdt_ref[...]) + dtb_ref[...])


def _cd_in(x, sh, sc, wfz, wx, wdt, cw, cb, dtb, rows):
    bsz, seq, d = x.shape
    with_fz = wfz is not None
    in_specs = _halo_specs(rows, seq, d) + [_row_spec(d), _row_spec(d)]
    args = [x, x, x, sh, sc]
    out_shape, out_specs = [], []
    if with_fz:
        in_specs.append(_resident((d, SSM_OFF)))
        args.append(wfz)
        out_shape += [jax.ShapeDtypeStruct((bsz, seq, FN_WIDTH), BF16),
                      jax.ShapeDtypeStruct((bsz, seq, SSD_WIDTH), F32)]
        out_specs += [pl.BlockSpec((1, rows, FN_WIDTH), lambda b, t: (b, t, 0)),
                      pl.BlockSpec((1, rows, SSD_WIDTH), lambda b, t: (b, t, 0))]
    in_specs += [_resident((d, SSD_CONV_DIM)), _resident((d, V7X_LANES)),
                 _resident((CONV_W, SSD_CONV_DIM)), _resident((1, SSD_CONV_DIM)),
                 _resident((1, V7X_LANES))]
    args += [wx, wdt, cw, cb, dtb]
    out_shape += [jax.ShapeDtypeStruct((bsz, seq, SSD_WIDTH), F32),
                  jax.ShapeDtypeStruct((bsz, seq, SSD_GN), BF16),
                  jax.ShapeDtypeStruct((bsz, seq, SSD_GN), BF16),
                  jax.ShapeDtypeStruct((bsz, seq, V7X_LANES), F32)]
    out_specs += [pl.BlockSpec((1, rows, SSD_WIDTH), lambda b, t: (b, t, 0)),
                  pl.BlockSpec((1, rows, SSD_GN), lambda b, t: (b, t, 0)),
                  pl.BlockSpec((1, rows, SSD_GN), lambda b, t: (b, t, 0)),
                  pl.BlockSpec((1, rows, V7X_LANES), lambda b, t: (b, t, 0))]
    est = d * 2200 * 2 + 2 * rows * d * 4 + 8 * (rows + 2 * HALO) * SSD_CONV_DIM * 4
    return pl.pallas_call(
        functools.partial(_cd_in_kernel, with_fz=with_fz, rows=rows),
        out_shape=out_shape,
        grid=(bsz, seq // rows),
        in_specs=in_specs,
        out_specs=out_specs,
        scratch_shapes=[pltpu.VMEM((rows + 2 * HALO, d), BF16)],
        compiler_params=pltpu.CompilerParams(
            dimension_semantics=("arbitrary", "arbitrary"),
            vmem_limit_bytes=_vmem_limit(est)),
        name="cd_in" if with_fz else "cd_in_ctx",
    )(*args)


def _cumsum_rows(z, reverse):
    n = z.shape[0]
    idx = lax.broadcasted_iota(jnp.int32, z.shape, 0)
    k = 1
    while k < n:
        if reverse:
            z = z + jnp.where(idx < n - k, pltpu.roll(z, n - k, 0), 0.0)
        else:
            z = z + jnp.where(idx >= k, pltpu.roll(z, k, 0), 0.0)
        k *= 2
    return z


def _lane_expand(row, lane0, width):
    lane = lax.broadcasted_iota(jnp.int32, (1, SSD_HPG * width), 1)
    out = jnp.zeros((1, SSD_HPG * width), F32)
    for j in range(SSD_HPG):
        val = jnp.broadcast_to(row[:, lane0 + j:lane0 + j + 1], (1, SSD_HPG * width))
        out = jnp.where(lane // width == j, val, out)
    return out


def _ssd_chunk(x, bm, cm, dt, a_row, h_ref, lane0, reverse, need_y):
    q = x.shape[0]
    gw = SSD_HPG * SSD_HEADDIM
    acum = _cumsum_rows(dt * a_row, reverse)
    total = acum[0:1, :] if reverse else acum[q - 1:q, :]
    acum_t = acum.T
    dt_t = dt.T
    total_t = jnp.broadcast_to(acum_t[:, 0:1] if reverse else acum_t[:, q - 1:q], acum_t.shape)
    wgt_t = dt_t * jnp.exp2(total_t - acum_t)
    lane_g = lax.broadcasted_iota(jnp.int32, (q, gw), 1) // SSD_HEADDIM
    lane_gh = lax.broadcasted_iota(jnp.int32, (SSD_STATE, gw), 1) // SSD_HEADDIM
    if need_y:
        srow_t = acum_t - jnp.log2(dt_t)
        ti = lax.broadcasted_iota(jnp.int32, (q, q), 0)
        si = lax.broadcasted_iota(jnp.int32, (q, q), 1)
        mask = (si >= ti) if reverse else (si <= ti)
    ys = []
    for g in range(SSD_GROUPS):
        bg = bm[:, g * SSD_STATE:(g + 1) * SSD_STATE]
        xgb = x[:, g * gw:(g + 1) * gw].astype(BF16)
        x_heads = [jnp.where(lane_g == j, xgb, jnp.zeros_like(xgb)) for j in range(SSD_HPG)]
        hg = h_ref[g]
        if need_y:
            cg = cm[:, g * SSD_STATE:(g + 1) * SSD_STATE]
            cb = _dot_nt(cg, bg)
            cgf = cg.astype(F32)
            hgb = hg.astype(BF16)
            lhs, rhs = [], []
            for j in range(SSD_HPG):
                jl = lane0 + g * SSD_HPG + j
                col = jnp.broadcast_to(acum[:, jl:jl + 1], (q, q))
                row = jnp.broadcast_to(srow_t[jl:jl + 1, :], (q, q))
                lhs.append((cb * jnp.exp2(jnp.where(mask, col - row, -jnp.inf))).astype(BF16))
                lhs.append((cgf * jnp.exp2(col[:, 0:SSD_STATE])).astype(BF16))
                rhs.append(x_heads[j])
                rhs.append(jnp.where(lane_gh == j, hgb, jnp.zeros_like(hgb)))
            ys.append(_dot(jnp.concatenate(lhs, axis=1), jnp.concatenate(rhs, axis=0)))
        bgt = bg.astype(F32).T
        wb = [(bgt * jnp.broadcast_to(wgt_t[lane0 + g * SSD_HPG + j:lane0 + g * SSD_HPG + j + 1, :],
                                      (SSD_STATE, q))).astype(BF16) for j in range(SSD_HPG)]
        st = _dot(jnp.concatenate(wb, axis=1), jnp.concatenate(x_heads, axis=0))
        h_ref[g] = hg * jnp.exp2(_lane_expand(total, lane0 + g * SSD_HPG, SSD_HEADDIM)) + st
    if need_y:
        return jnp.concatenate(ys, axis=1)
    return None


def _ssd_kernel(xc_ref, bc_ref, dtc_ref, xl_ref, bl_ref, cl_ref, dtl_ref, z_ref, alog_ref,
                dskip_ref, ng_ref, o_ref, y_ref, h_ref):
    a_row = -jnp.exp(alog_ref[...]) * math.log2(math.e)
    nc_ctx = xc_ref.shape[1] // SSD_CHUNK
    nc_lat = xl_ref.shape[1] // SSD_CHUNK

    def rows_of(c):
        return pl.ds(pl.multiple_of(c * SSD_CHUNK, SSD_CHUNK), SSD_CHUNK)

    def ctx_step(c, direction):
        r = rows_of(c)
        _ssd_chunk(xc_ref[0, r, :], bc_ref[0, r, :], None, dtc_ref[0, r, :], a_row,
                   h_ref.at[direction], direction * SSD_HEADS, direction == 1, False)

    def lat_step(c, direction):
        r = rows_of(c)
        y_ref[direction, r, :] = _ssd_chunk(
            xl_ref[0, r, :], bl_ref[0, r, :], cl_ref[0, r, :], dtl_ref[0, r, :], a_row,
            h_ref.at[direction], direction * SSD_HEADS, direction == 1, True)

    h_ref[...] = jnp.zeros_like(h_ref)

    def ctx_pair(i, carry):
        ctx_step(i, 0)
        ctx_step(nc_ctx - 1 - i, 1)
        return carry

    lax.fori_loop(0, nc_ctx, ctx_pair, 0)

    def lat_pair(i, carry):
        lat_step(i, 0)
        lat_step(nc_lat - 1 - i, 1)
        return carry

    lax.fori_loop(0, nc_lat, lat_pair, 0)

    def finish(c, carry):
        r = rows_of(c)
        zz = z_ref[0, r, :]
        yt = (y_ref[0, r, :] + y_ref[1, r, :] + xl_ref[0, r, :] * dskip_ref[...]) * _silu(zz)
        o_ref[0, r, :] = _rms_norm(yt, ng_ref[...]).astype(BF16)
        return carry

    lax.fori_loop(0, nc_lat, finish, 0)


def _ssd(xs_c, b_c, dt_c, xs_l, b_l, c_l, dt_l, z, alog_row, dskip_row, ng_row):
    bsz, sl, w = xs_l.shape
    sc = xs_c.shape[1]
    assert sl % SSD_CHUNK == 0 and sc % SSD_CHUNK == 0
    gw = SSD_HPG * SSD_HEADDIM

    def full(s, n):
        return pl.BlockSpec((1, s, n), lambda b: (b, 0, 0))

    est = (2 * (sl + sc) * (w * 4 + 2 * SSD_GN * 2 + V7X_LANES * 4) + 2 * sl * w * 4
           + 2 * sl * w * 2 + 2 * sl * w * 4)
    return pl.pallas_call(
        _ssd_kernel,
        out_shape=jax.ShapeDtypeStruct((bsz, sl, w), BF16),
        grid=(bsz,),
        in_specs=[
            full(sc, w), full(sc, SSD_GN), full(sc, V7X_LANES),
            full(sl, w), full(sl, SSD_GN), full(sl, SSD_GN), full(sl, V7X_LANES),
            full(sl, w),
            _resident((1, V7X_LANES)), _resident((1, w)), _resident((1, w)),
        ],
        out_specs=full(sl, w),
        scratch_shapes=[
            pltpu.VMEM((2, sl, w), F32),
            pltpu.VMEM((2, SSD_GROUPS, SSD_STATE, gw), F32),
        ],
        compiler_params=pltpu.CompilerParams(
            dimension_semantics=("arbitrary",),
            vmem_limit_bytes=_vmem_limit(est)),
        name="ssd_scan",
    )(xs_c, b_c, dt_c, xs_l, b_l, c_l, dt_l, z, alog_row, dskip_row, ng_row)


def _fourier_kernel(f_ref, cc_ref, sc_ref, w2_ref, o_ref, y_ref, *, rows):
    seq = f_ref.shape[1]
    for g in range(FN_GROUPS):
        l = slice(g * FN_GDIM, (g + 1) * FN_GDIM)
        fg = f_ref[0, :, l]
        y_ref[0:seq, l] = _dot(fg, cc_ref[...]).astype(BF16)
        y_ref[seq:2 * seq, l] = _dot(fg, sc_ref[...]).astype(BF16)
    norm = 1.0 / math.sqrt(seq * FN_GDIM)

    def tile(i, carry):
        r = pl.ds(pl.multiple_of(i * rows, rows), rows)
        o_ref[0, r, :] = (_dot(w2_ref[r, :], y_ref[...]) * norm).astype(BF16)
        return carry

    lax.fori_loop(0, seq // rows, tile, 0)


def _fourier(f, cc, sc, w2, rows):
    bsz, seq, w = f.shape
    est = seq * 2 * seq * 2 + 4 * seq * w * 2 + 2 * seq * w * 2 + 4 * rows * 2 * seq * 2
    return pl.pallas_call(
        functools.partial(_fourier_kernel, rows=rows),
        out_shape=jax.ShapeDtypeStruct((bsz, seq, w), BF16),
        grid=(bsz,),
        in_specs=[
            pl.BlockSpec((1, seq, w), lambda b: (b, 0, 0)),
            _resident((FN_GDIM, FN_GDIM)), _resident((FN_GDIM, FN_GDIM)),
            _resident((seq, 2 * seq)),
        ],
        out_specs=pl.BlockSpec((1, seq, w), lambda b: (b, 0, 0)),
        scratch_shapes=[pltpu.VMEM((2 * seq, w), BF16)],
        compiler_params=pltpu.CompilerParams(
            dimension_semantics=("arbitrary",),
            vmem_limit_bytes=_vmem_limit(est)),
        name="fourier_mix",
    )(f, cc, sc, w2)


def _rope_tables(seq):
    rows = seq // GRID_W
    row = jnp.repeat(jnp.arange(rows, dtype=F32), GRID_W)
    col = jnp.broadcast_to(jnp.arange(GRID_W, dtype=F32), (rows, GRID_W)).reshape(-1)
    n_freq = HEAD_DIM // 4
    inv = ROPE_THETA ** (-jnp.arange(n_freq, dtype=F32) / n_freq)
    ang = jnp.stack([row, col], axis=-1)[:, :, None] * inv
    cos, sin = jnp.cos(ang), jnp.sin(ang)
    cos_t = jnp.concatenate([cos[:, 0], cos[:, 1], cos[:, 0], cos[:, 1]], axis=-1)
    sin_t = jnp.concatenate([-sin[:, 0], -sin[:, 1], sin[:, 0], sin[:, 1]], axis=-1)
    return cos_t, sin_t


def _rope_head_order(m):
    lead = m.shape[:-1]
    nh = m.shape[-1] // HEAD_DIM
    m = m.reshape(lead + (nh, 2, 2, HEAD_DIM // 4))
    return jnp.swapaxes(m, -3, -2).reshape(lead + (nh * HEAD_DIM,))


def _dft_tables(seq):
    def tw(n):
        i = np.arange(n, dtype=np.int64)
        ang = ((i[:, None] * i[None, :]) % n).astype(np.float64) * (2.0 * math.pi / n)
        return np.cos(ang), np.sin(ang)

    cc, sc = tw(FN_GDIM)
    cl, sl = tw(seq)
    w2 = np.concatenate([cl, -sl], axis=1).astype(np.float32)
    return (jnp.asarray(cc.astype(np.float32), BF16), jnp.asarray(sc.astype(np.float32), BF16),
            jnp.asarray(w2, BF16))


def _ffn_params(w_up, conv_w, conv_b, w_down):
    depth, d, _ = w_up.shape
    half_gate = jnp.concatenate([jnp.ones((D_FF,), F32), jnp.full((D_FF,), 0.5, F32)])
    return (w_up.astype(BF16), conv_w * half_gate, (conv_b * half_gate).reshape(depth, 1, 2 * D_FF),
            w_down.astype(BF16).reshape(depth, D_FF // FF_CHUNK, FF_CHUNK, d))


def kernel(x, c, ctx, c_ctx, mod_w, mod_b, ln1_g, ln1_b, ln2_g, ln2_b, ffn_up, ffn_conv_w, ffn_conv_b,
           ffn_down, ab_w_in, ab_w_out, gm_ln_g, gm_ln_b, gm_ws, gm_bs, q_norm_g, k_norm_g, cd_w_in,
           cd_w_out, ssd_conv_w, ssd_conv_b, ssd_dt_bias, ssd_a_log, ssd_d, ssd_norm_g):
    bsz, seq, d = x.shape
    ctx_len = ctx.shape[1]
    lat_rows = min(1024, seq)
    out_rows = min(2048, seq)
    dft_rows = min(2048, seq)
    ctx_rows = min(256, ctx_len)

    pad = (-(bsz + 1)) % V7X_SUBLANES
    cc = jnp.concatenate([c, c_ctx[None], jnp.zeros((pad, d), F32)], axis=0)
    mod = _modulation(cc, mod_w, mod_b)
    rope_tabs = _rope_tables(seq)

    def row(v):
        return v.reshape(1, -1)

    ffn_p = _ffn_params(ffn_up, ffn_conv_w, ffn_conv_b, ffn_down)
    xl, xc = x, ctx
    for i in range(DEPTH):
        last = i == DEPTH - 1
        j = i // 2
        ml = mod[i, :bsz].reshape(bsz, 6, 1, d)
        mc = jnp.broadcast_to(mod[i, bsz].reshape(1, 6, 1, d), (bsz, 6, 1, d))
        sh1, sc1, g1, sh2, sc2, g2 = (ml[:, k] for k in range(6))
        csh1, csc1, cg1, csh2, csc2, cg2 = (mc[:, k] for k in range(6))
        ln1 = (row(ln1_g[i]), row(ln1_b[i]))
        ln2 = (row(ln2_g[i]), row(ln2_b[i]))

        if i % 2 == 0:
            w_in = ab_w_in[j].astype(BF16)
            qk_lo, qk_hi = 2 * GM_WIDTH, KV_OFF + KV_W
            q_scale = HEAD_DIM ** -0.5 * math.log2(math.e)
            w_in = jnp.concatenate([0.5 * w_in[:, :qk_lo], _rope_head_order(w_in[:, qk_lo:qk_hi]),
                                    w_in[:, qk_hi:]], axis=1)
            w_out = ab_w_out[j].astype(BF16)
            gm = (row(gm_ln_g[j]), row(gm_ln_b[j]), gm_ws[j].astype(BF16),
                  jnp.repeat(gm_bs[j].T, GM_WIDTH // GM_GROUPS, axis=1),
                  row(_rope_head_order(q_norm_g[j]) * q_scale), row(_rope_head_order(k_norm_g[j])))
            kv_len = seq + ctx_len
            a_c, q_c, k_all, v_all = _ab_in(xc, csh1, csc1, w_in, *gm, None, ctx_rows, kv_len, seq)
            a_l, q_l, k_all, v_all = _ab_in(xl, sh1, sc1, w_in, *gm, rope_tabs, out_rows, kv_len, 0,
                                            kv_into=(k_all, v_all))
            mix_l = (a_l, _attention(q_l, k_all, v_all, 0, kv_len, out_rows))
            if not last:
                mix_c = (a_c, _attention(q_c, k_all, v_all, seq, ctx_len, ctx_rows))
        else:
            w_in = cd_w_in[j].astype(BF16)
            w_out = cd_w_out[j].astype(BF16)
            wfz = w_in[:, :SSM_OFF]
            wx = w_in[:, SSM_OFF:SSM_OFF + SSD_CONV_DIM]
            ndt = 2 * SSD_HEADS
            wdt = jnp.pad(w_in[:, SSM_OFF + SSD_CONV_DIM:], ((0, 0), (0, V7X_LANES - ndt)))
            dtb = jnp.pad(ssd_dt_bias[j].reshape(1, ndt), ((0, 0), (0, V7X_LANES - ndt)))
            alog = jnp.pad(ssd_a_log[j].reshape(1, ndt), ((0, 0), (0, V7X_LANES - ndt)))
            cw, cb = 0.5 * ssd_conv_w[j], 0.5 * row(ssd_conv_b[j])
            dskip = row(jnp.repeat(ssd_d[j], SSD_HEADDIM))
            f_l, z_l, xs_l, b_l, c_l, dt_l = _cd_in(xl, sh1, sc1, wfz, wx, wdt, cw, cb, dtb, lat_rows)
            if last:
                xs_c, b_c, _, dt_c = _cd_in(xc, csh1, csc1, None, wx, wdt, cw, cb, dtb, ctx_rows)
            else:
                raise NotImplementedError("odd layers with a context output are not part of this block")
            s_l = _ssd(xs_c, b_c, dt_c, xs_l, b_l, c_l, dt_l, z_l, alog, dskip, row(ssd_norm_g[j]))
            mix_l = (_fourier(f_l, *_dft_tables(seq), dft_rows), s_l)

        xl = _out_ln(*mix_l, w_out, xl, g1, *ln1, out_rows)
        xl = _ffn(xl, sh2, sc2, g2, *ffn_p, i, *ln2, lat_rows)
        if not last:
            xc = _out_ln(*mix_c, w_out, xc, cg1, *ln1, ctx_rows)
            xc = _ffn(xc, csh2, csc2, cg2, *ffn_p, i, *ln2, ctx_rows)
    return xl
```

```python
import functools
import math

import jax
import jax.numpy as jnp
import numpy as np
from jax import lax
from jax.experimental import pallas as pl
from jax.experimental.pallas import tpu as pltpu

D_MODEL = 1024
DEPTH = 2
GRID_W = 64
CHUNK = 128
GM_GROUPS = 4
GM_WIDTH = 512
HEAD_DIM = 128
N_Q_HEADS = 4
N_KV_HEADS = 2
ATTN_WIDTH = 512
KV_W = 256
ROPE_THETA = 10000.0
FN_GROUPS = 4
FN_GDIM = 128
FN_WIDTH = 512
SSD_HEADDIM = 64
SSD_HEADS = 8
SSD_GROUPS = 2
SSD_HPG = 4
SSD_STATE = 128
SSD_WIDTH = 512
SSD_GN = 256
SSD_CONV_DIM = 1024
D_FF = 2816
CONV_W = 3
KV_OFF = 2 * GM_WIDTH + ATTN_WIDTH
SSM_OFF = FN_WIDTH + SSD_WIDTH
DN_ALPHA = (2 * DEPTH) ** 0.25
EPS = 1e-6

V7X_LANES = 128
V7X_SUBLANES = 8
V7X_VMEM_BYTES = 64 * 1024 * 1024
VMEM_CHIP_HEADROOM = 3 << 20
VMEM_TEMP_ALLOWANCE = 8 << 20
MOD_COLS = 2048
AB_SUB = 256
FF_CHUNK = 256
SSD_CHUNK = 256

BF16 = jnp.bfloat16
F32 = jnp.float32
HALO = V7X_SUBLANES


def _vmem_limit(nbytes):
    return int(min(nbytes * 3 // 2 + VMEM_TEMP_ALLOWANCE, V7X_VMEM_BYTES - VMEM_CHIP_HEADROOM))


def _resident(shape):
    nd = len(shape)
    return pl.BlockSpec(shape, lambda *_: (0,) * nd, pipeline_mode=pl.Buffered(1))


def _dot(a, b):
    return jnp.dot(a, b, preferred_element_type=F32)


def _dot_nt(a, b):
    return lax.dot_general(a, b, (((1,), (1,)), ((), ())), preferred_element_type=F32)


def _layer_norm(z, g, b, eps=EPS):
    mu = jnp.mean(z, axis=-1, keepdims=True)
    zc = z - mu
    var = jnp.mean(zc * zc, axis=-1, keepdims=True)
    return zc * lax.rsqrt(var + eps) * g + b


def _deepnorm(x, gate, y, g, b):
    return _layer_norm(x + (gate * (1.0 / DN_ALPHA)) * y, g, b, eps=EPS / DN_ALPHA ** 2)


def _rms_norm(z, g):
    return z * lax.rsqrt(jnp.mean(z * z, axis=-1, keepdims=True) + EPS) * g


def _softplus(z):
    return jnp.maximum(z, 0.0) + jnp.log1p(jnp.exp(-jnp.abs(z)))


def _gelu_tanh_of_half(h):
    c1 = math.sqrt(2.0 / math.pi)
    return h + h * jnp.tanh(h * (2.0 * c1 + (8.0 * c1 * 0.044715) * (h * h)))


def _silu(z):
    h = 0.5 * z
    return h + h * jnp.tanh(h)


def _conv3(u, w, b, rows):
    tiles, c = rows // V7X_SUBLANES, u.shape[1]
    u3 = u.reshape(tiles + 2, V7X_SUBLANES, c)
    sub = lax.broadcasted_iota(jnp.int32, (tiles, V7X_SUBLANES, c), 1)
    dn = pltpu.roll(u3, 1, 1)
    up = pltpu.roll(u3, V7X_SUBLANES - 1, 1)
    prev = jnp.where(sub == 0, dn[0:tiles], dn[1:tiles + 1])
    nxt = jnp.where(sub == V7X_SUBLANES - 1, up[2:tiles + 2], up[1:tiles + 1])
    return (b + w[0:1] * prev + w[1:2] * u3[1:tiles + 1] + w[2:3] * nxt).reshape(rows, c)


def _fill_halo_tile(hm_ref, x_ref, xp_ref, xn_ref, sh, sc, rows):
    t = pl.program_id(1)
    nt = pl.num_programs(1)
    keep_prev = jnp.where(t > 0, 1.0, 0.0)
    keep_next = jnp.where(t < nt - 1, 1.0, 0.0)
    hm_ref[HALO:HALO + rows, :] = (x_ref[0] * (1.0 + sc) + sh).astype(BF16)
    hm_ref[0:HALO, :] = ((xp_ref[0] * (1.0 + sc) + sh) * keep_prev).astype(BF16)
    hm_ref[HALO + rows:2 * HALO + rows, :] = ((xn_ref[0] * (1.0 + sc) + sh) * keep_next).astype(BF16)


def _halo_specs(rows, seq, d):
    nb = rows // HALO
    last = seq // HALO - 1
    return [
        pl.BlockSpec((1, rows, d), lambda b, t: (b, t, 0)),
        pl.BlockSpec((1, HALO, d), lambda b, t: (b, jnp.maximum(t * nb - 1, 0), 0)),
        pl.BlockSpec((1, HALO, d), lambda b, t: (b, jnp.minimum((t + 1) * nb, last), 0)),
    ]


def _row_spec(d):
    return pl.BlockSpec((1, 1, d), lambda b, t: (b, 0, 0))


def _mod_kernel(c_ref, w_ref, b_ref, o_ref):
    c = c_ref[...]
    act = c * jax.nn.sigmoid(c)
    o_ref[0] = jnp.dot(act, w_ref[0], preferred_element_type=F32,
                       precision=lax.Precision.HIGHEST) + b_ref[0]


def _modulation(cc, mod_w, mod_b):
    rows, d = cc.shape
    depth, _, n = mod_w.shape
    tn = MOD_COLS
    return pl.pallas_call(
        _mod_kernel,
        out_shape=jax.ShapeDtypeStruct((depth, rows, n), F32),
        grid=(depth, n // tn),
        in_specs=[
            pl.BlockSpec((rows, d), lambda i, j: (0, 0)),
            pl.BlockSpec((1, d, tn), lambda i, j: (i, 0, j)),
            pl.BlockSpec((1, 1, tn), lambda i, j: (i, 0, j)),
        ],
        out_specs=pl.BlockSpec((1, rows, tn), lambda i, j: (i, 0, j)),
        compiler_params=pltpu.CompilerParams(
            dimension_semantics=("arbitrary", "arbitrary"),
            vmem_limit_bytes=_vmem_limit(2 * d * tn * 4)),
        name="modulation",
    )(cc, mod_w, mod_b.reshape(depth, 1, n))


def _ab_in_kernel(*refs, use_rope, rows):
    (x_ref, sh_ref, sc_ref, w_ref, lng_ref, lnb_ref, ws_ref, bias_ref, qg_ref, kg_ref) = refs[:10]
    if use_rope:
        cos_ref, sin_ref = refs[10:12]
    a_ref, q_ref, k_ref, v_ref = refs[-4:]

    gd = GM_WIDTH // GM_GROUPS
    slab_rows = min(AB_SUB, rows)

    def slab(i, carry):
        r0 = pl.multiple_of(i * slab_rows, slab_rows)
        rs = pl.ds(r0, slab_rows)
        h = (x_ref[0, rs, :] * (1.0 + sc_ref[0]) + sh_ref[0]).astype(BF16)

        u = _gelu_tanh_of_half(_dot(h, w_ref[:, 0:GM_WIDTH]))
        vn = _layer_norm(_gelu_tanh_of_half(_dot(h, w_ref[:, GM_WIDTH:2 * GM_WIDTH])),
                         lng_ref[...], lnb_ref[...])
        vb = vn.astype(BF16)
        for c in range(slab_rows // CHUNK):
            r = slice(c * CHUNK, (c + 1) * CHUNK)
            for g in range(GM_GROUPS):
                l = slice(g * gd, (g + 1) * gd)
                mixed = _dot(ws_ref[g], vb[r, l]) + bias_ref[:, l]
                a_ref[0, pl.ds(r0 + c * CHUNK, CHUNK), l] = (u[r, l] * mixed).astype(BF16)

        def rope(z):
            if not use_rope:
                return z
            return z * cos_ref[rs, :] + pltpu.roll(z, HEAD_DIM // 2, 1) * sin_ref[rs, :]

        pq = _dot(h, w_ref[:, 2 * GM_WIDTH:KV_OFF])
        for hh in range(N_Q_HEADS):
            l = slice(hh * HEAD_DIM, (hh + 1) * HEAD_DIM)
            q_ref[0, rs, l] = rope(_rms_norm(pq[:, l], qg_ref[...])).astype(BF16)
        pk = _dot(h, w_ref[:, KV_OFF:KV_OFF + KV_W])
        for hh in range(N_KV_HEADS):
            l = slice(hh * HEAD_DIM, (hh + 1) * HEAD_DIM)
            k_ref[0, rs, l] = rope(_rms_norm(pk[:, l], kg_ref[...])).astype(BF16)
        v_ref[0, rs, :] = _dot(h, w_ref[:, KV_OFF + KV_W:KV_OFF + 2 * KV_W]).astype(BF16)
        return carry

    lax.fori_loop(0, rows // slab_rows, slab, 0)


def _ab_in(x, sh, sc, w, lng, lnb, ws, bias2d, qg, kg, rope_tabs, rows, kv_len, kv_row0, kv_into=None):
    bsz, seq, d = x.shape
    assert kv_row0 % rows == 0
    kv_blk0 = kv_row0 // rows
    n = w.shape[1]
    use_rope = rope_tabs is not None
    in_specs = [
        pl.BlockSpec((1, rows, d), lambda b, t: (b, t, 0)),
        _row_spec(d), _row_spec(d),
        _resident((d, n)),
        _resident((1, GM_WIDTH)), _resident((1, GM_WIDTH)),
        _resident((GM_GROUPS, CHUNK, CHUNK)),
        _resident((CHUNK, GM_WIDTH)),
        _resident((1, HEAD_DIM)), _resident((1, HEAD_DIM)),
    ]
    args = [x, sh, sc, w, lng, lnb, ws, bias2d, qg, kg]
    if use_rope:
        in_specs += [pl.BlockSpec((rows, HEAD_DIM), lambda b, t: (t, 0))] * 2
        args += list(rope_tabs)
    aliases = {}
    if kv_into is not None:
        aliases = {len(args): 2, len(args) + 1: 3}
        in_specs += [pl.BlockSpec(memory_space=pl.ANY)] * 2
        args += list(kv_into)
    out_shape = [
        jax.ShapeDtypeStruct((bsz, seq, GM_WIDTH), BF16),
        jax.ShapeDtypeStruct((bsz, seq, ATTN_WIDTH), BF16),
        jax.ShapeDtypeStruct((bsz, kv_len, KV_W), BF16),
        jax.ShapeDtypeStruct((bsz, kv_len, KV_W), BF16),
    ]
    out_specs = [
        pl.BlockSpec((1, rows, GM_WIDTH), lambda b, t: (b, t, 0)),
        pl.BlockSpec((1, rows, ATTN_WIDTH), lambda b, t: (b, t, 0)),
        pl.BlockSpec((1, rows, KV_W), lambda b, t: (b, t + kv_blk0, 0)),
        pl.BlockSpec((1, rows, KV_W), lambda b, t: (b, t + kv_blk0, 0)),
    ]
    est = d * n * 2 + 2 * rows * d * 4 + 6 * rows * n * 4
    return pl.pallas_call(
        functools.partial(_ab_in_kernel, use_rope=use_rope, rows=rows),
        out_shape=out_shape,
        grid=(bsz, seq // rows),
        in_specs=in_specs,
        out_specs=out_specs,
        input_output_aliases=aliases,
        compiler_params=pltpu.CompilerParams(
            dimension_semantics=("arbitrary", "arbitrary"),
            vmem_limit_bytes=_vmem_limit(est)),
        name="ab_in_rope" if use_rope else "ab_in_ctx",
    )(*args)


def _attn_kernel(q_ref, k_ref, v_ref, o_ref, vt_ref, *, rows, blk):
    @pl.when(pl.program_id(2) == 0)
    def _():
        vt_ref[...] = v_ref[0].astype(F32).T

    k = k_ref[0]
    vt = vt_ref[...].astype(BF16)
    for h in range(N_Q_HEADS // N_KV_HEADS):
        l_ = slice(h * HEAD_DIM, (h + 1) * HEAD_DIM)
        for r0 in range(0, rows, blk):
            st = _dot_nt(k, q_ref[0, r0:r0 + blk, l_])
            m = jnp.max(st, axis=0, keepdims=True)
            p = jnp.exp2(st - m)
            l = jnp.sum(p, axis=0, keepdims=True)
            ot = _dot(vt, p.astype(BF16)) / l
            o_ref[0, r0:r0 + blk, l_] = ot.T.astype(BF16)


def _attention(q, k, v, k_row0, sk, rows, blk=2048):
    bsz, sq, _ = q.shape
    assert k_row0 % sk == 0
    kb = k_row0 // sk
    gw = (N_Q_HEADS // N_KV_HEADS) * HEAD_DIM
    est = 4 * blk * sk * 12 + 4 * sk * HEAD_DIM * 2 + 4 * rows * gw * 2 + sk * HEAD_DIM * 4
    return pl.pallas_call(
        functools.partial(_attn_kernel, rows=rows, blk=min(blk, rows)),
        out_shape=jax.ShapeDtypeStruct((bsz, sq, ATTN_WIDTH), BF16),
        grid=(bsz, N_KV_HEADS, sq // rows),
        in_specs=[
            pl.BlockSpec((1, rows, gw), lambda b, h, t: (b, t, h)),
            pl.BlockSpec((1, sk, HEAD_DIM), lambda b, h, t: (b, kb, h)),
            pl.BlockSpec((1, sk, HEAD_DIM), lambda b, h, t: (b, kb, h)),
        ],
        out_specs=pl.BlockSpec((1, rows, gw), lambda b, h, t: (b, t, h)),
        scratch_shapes=[pltpu.VMEM((HEAD_DIM, sk), F32)],
        compiler_params=pltpu.CompilerParams(
            dimension_semantics=("arbitrary", "arbitrary", "arbitrary"),
            vmem_limit_bytes=_vmem_limit(est)),
        name="attention",
    )(q, k, v)


def _out_ln_kernel(a1_ref, a2_ref, w_ref, x_ref, g_ref, lng_ref, lnb_ref, o_ref):
    half = a1_ref.shape[2]
    y = _dot(a1_ref[0], w_ref[0:half, :]) + _dot(a2_ref[0], w_ref[half:2 * half, :])
    o_ref[0] = _deepnorm(x_ref[0], g_ref[0], y, lng_ref[...], lnb_ref[...])


def _out_ln(a1, a2, w, x, gate, lng, lnb, rows):
    bsz, seq, d = x.shape
    half = a1.shape[2]
    est = 2 * half * d * 2 + 4 * rows * d * 4 + 4 * rows * half * 2 + 2 * rows * d * 4
    return pl.pallas_call(
        _out_ln_kernel,
        out_shape=jax.ShapeDtypeStruct((bsz, seq, d), F32),
        grid=(bsz, seq // rows),
        in_specs=[
            pl.BlockSpec((1, rows, half), lambda b, t: (b, t, 0)),
            pl.BlockSpec((1, rows, half), lambda b, t: (b, t, 0)),
            _resident((2 * half, d)),
            pl.BlockSpec((1, rows, d), lambda b, t: (b, t, 0)),
            _row_spec(d),
            _resident((1, d)), _resident((1, d)),
        ],
        out_specs=pl.BlockSpec((1, rows, d), lambda b, t: (b, t, 0)),
        compiler_params=pltpu.CompilerParams(
            dimension_semantics=("arbitrary", "arbitrary"),
            vmem_limit_bytes=_vmem_limit(est)),
        name="out_ln",
    )(a1, a2, w, x, gate, lng, lnb)


def _ffn_kernel(x_ref, xp_ref, xn_ref, sh_ref, sc_ref, g_ref, wu_ref, cw_ref, cb_ref, wd_ref, lng_ref, lnb_ref,
                o_ref, hm_ref, u_ref, gt_ref, *, rows, seg):
    _fill_halo_tile(hm_ref, x_ref, xp_ref, xn_ref, sh_ref[0], sc_ref[0], rows)
    n_chunks = wd_ref.shape[0]

    def cols(ref, c):
        off = c * FF_CHUNK
        return ref[:, pl.ds(off if isinstance(off, int) else pl.multiple_of(off, FF_CHUNK), FF_CHUNK)]
    tiles = rows // V7X_SUBLANES
    sub = lax.broadcasted_iota(jnp.int32, (tiles, V7X_SUBLANES, FF_CHUNK), 1)
    if seg < rows:
        seg_tiles = seg // V7X_SUBLANES
        til = lax.broadcasted_iota(jnp.int32, (tiles, V7X_SUBLANES, FF_CHUNK), 0) % seg_tiles
        seq_first = (sub == 0) & (til == 0)
        seq_last = (sub == V7X_SUBLANES - 1) & (til == seg_tiles - 1)

    def up_proj(c, slot):
        hm = hm_ref[...]
        u_ref[slot, 0] = _dot(hm, cols(wu_ref, c)).reshape(tiles + 2, V7X_SUBLANES, FF_CHUNK)
        u_ref[slot, 1] = _dot(hm, cols(wu_ref, c + n_chunks)).reshape(tiles + 2, V7X_SUBLANES, FF_CHUNK)

    def conv_gate(c, slot):
        def conv(half):
            u = u_ref[slot, half]
            dn = pltpu.roll(u, 1, 1)
            up = pltpu.roll(u, V7X_SUBLANES - 1, 1)
            prev = jnp.where(sub == 0, dn[0:tiles], dn[1:tiles + 1])
            nxt = jnp.where(sub == V7X_SUBLANES - 1, up[2:tiles + 2], up[1:tiles + 1])
            if seg < rows:
                prev = jnp.where(seq_first, 0.0, prev)
                nxt = jnp.where(seq_last, 0.0, nxt)
            w = cols(cw_ref, c + half * n_chunks)
            return cols(cb_ref, c + half * n_chunks) + w[0:1] * prev + w[1:2] * u[1:tiles + 1] + w[2:3] * nxt

        a1 = conv(0)
        a2 = conv(1)
        gt_ref[c] = (a1 * (a2 + a2 * jnp.tanh(a2))).reshape(rows, FF_CHUNK)

    up_proj(0, 0)

    def pair(k, carry):
        c = 2 * k
        up_proj(c + 1, 1)
        conv_gate(c, 0)
        up_proj(c + 2, 0)
        conv_gate(c + 1, 1)
        return carry

    lax.fori_loop(0, (n_chunks - 1) // 2, pair, 0)
    conv_gate(n_chunks - 1, 0)

    f = _dot(gt_ref[0].astype(BF16), wd_ref[0])
    for c in range(1, n_chunks):
        f = f + _dot(gt_ref[c].astype(BF16), wd_ref[c])
    o_ref[0] = _deepnorm(x_ref[0], g_ref[0], f, lng_ref[...], lnb_ref[...])


def _ffn(x, sh, sc, gate, wu, cw, cb, wd, layer, lng, lnb, rows, seg=None):
    bsz, seq, d = x.shape
    seg = seq if seg is None else seg
    assert seg == seq or (rows == seq and rows % seg == 0 and seg % V7X_SUBLANES == 0)
    nc = wd.shape[1]
    dff = nc * FF_CHUNK
    assert wu.shape[2] == 2 * dff and nc % 2 == 1, "the pipeline peels one chunk and pairs the rest"

    def of_layer(*shape):
        nd = len(shape)
        return pl.BlockSpec((None,) + shape, lambda b, t: (layer,) + (0,) * nd, pipeline_mode=pl.Buffered(1))

    est = (3 * nc * d * FF_CHUNK * 2 + 4 * rows * d * 4 + (rows + 2 * HALO) * d * 2
           + nc * rows * FF_CHUNK * 4 + 8 * (rows + 2 * HALO) * FF_CHUNK * 4 + 2 * rows * d * 4)
    return pl.pallas_call(
        functools.partial(_ffn_kernel, rows=rows, seg=min(seg, rows)),
        out_shape=jax.ShapeDtypeStruct((bsz, seq, d), F32),
        grid=(bsz, seq // rows),
        in_specs=_halo_specs(rows, seq, d) + [
            _row_spec(d), _row_spec(d), _row_spec(d),
            of_layer(d, 2 * dff), of_layer(CONV_W, 2 * dff), of_layer(1, 2 * dff),
            of_layer(nc, FF_CHUNK, d),
            _resident((1, d)), _resident((1, d)),
        ],
        out_specs=pl.BlockSpec((1, rows, d), lambda b, t: (b, t, 0)),
        scratch_shapes=[
            pltpu.VMEM((rows + 2 * HALO, d), BF16),
            pltpu.VMEM((2, 2, rows // HALO + 2, HALO, FF_CHUNK), F32),
            pltpu.VMEM((nc, rows, FF_CHUNK), F32),
        ],
        compiler_params=pltpu.CompilerParams(
            dimension_semantics=("arbitrary", "arbitrary"),
            vmem_limit_bytes=_vmem_limit(est)),
        name="conv_ffn",
    )(x, x, x, sh, sc, gate, wu, cw, cb, wd, lng, lnb)


def _cd_in_kernel(*refs, with_fz, rows):
    x_ref, xp_ref, xn_ref, sh_ref, sc_ref = refs[:5]
    if with_fz:
        wfz_ref, wx_ref, wdt_ref, cw_ref, cb_ref, dtb_ref = refs[5:11]
        f_ref, z_ref, xs_ref, b_ref, c_ref, dt_ref, hm_ref = refs[11:]
    else:
        wx_ref, wdt_ref, cw_ref, cb_ref, dtb_ref = refs[5:10]
        xs_ref, b_ref, c_ref, dt_ref, hm_ref = refs[10:]
    _fill_halo_tile(hm_ref, x_ref, xp_ref, xn_ref, sh_ref[0], sc_ref[0], rows)
    hc = hm_ref[HALO:HALO + rows, :]
    if with_fz:
        pfz = _dot(hc, wfz_ref[...])
        f_ref[0] = pfz[:, 0:FN_WIDTH].astype(BF16)
        z_ref[0] = pfz[:, FN_WIDTH:SSM_OFF]
    a = _conv3(_dot(hm_ref[...], wx_ref[...]), cw_ref[...], cb_ref[...], rows)
    s = a + a * jnp.tanh(a)
    xs_ref[0] = s[:, 0:SSD_WIDTH]
    b_ref[0] = s[:, SSD_WIDTH:SSD_WIDTH + SSD_GN].astype(BF16)
    c_ref[0] = s[:, SSD_WIDTH + SSD_GN:SSD_CONV_DIM].astype(BF16)
    dt_ref[0] = _softplus(_dot(hc, wdt_ref[...]) + dtb_ref[...])


def _cd_in(x, sh, sc, wfz, wx, wdt, cw, cb, dtb, rows):
    bsz, seq, d = x.shape
    with_fz = wfz is not None
    in_specs = _halo_specs(rows, seq, d) + [_row_spec(d), _row_spec(d)]
    args = [x, x, x, sh, sc]
    out_shape, out_specs = [], []
    if with_fz:
        in_specs.append(_resident((d, SSM_OFF)))
        args.append(wfz)
        out_shape += [jax.ShapeDtypeStruct((bsz, seq, FN_WIDTH), BF16),
                      jax.ShapeDtypeStruct((bsz, seq, SSD_WIDTH), F32)]
        out_specs += [pl.BlockSpec((1, rows, FN_WIDTH), lambda b, t: (b, t, 0)),
                      pl.BlockSpec((1, rows, SSD_WIDTH), lambda b, t: (b, t, 0))]
    in_specs += [_resident((d, SSD_CONV_DIM)), _resident((d, V7X_LANES)),
                 _resident((CONV_W, SSD_CONV_DIM)), _resident((1, SSD_CONV_DIM)),
                 _resident((1, V7X_LANES))]
    args += [wx, wdt, cw, cb, dtb]
    out_shape += [jax.ShapeDtypeStruct((bsz, seq, SSD_WIDTH), F32),
                  jax.ShapeDtypeStruct((bsz, seq, SSD_GN), BF16),
                  jax.ShapeDtypeStruct((bsz, seq, SSD_GN), BF16),
                  jax.ShapeDtypeStruct((bsz, seq, V7X_LANES), F32)]
    out_specs += [pl.BlockSpec((1, rows, SSD_WIDTH), lambda b, t: (b, t, 0)),
                  pl.BlockSpec((1, rows, SSD_GN), lambda b, t: (b, t, 0)),
                  pl.BlockSpec((1, rows, SSD_GN), lambda b, t: (b, t, 0)),
                  pl.BlockSpec((1, rows, V7X_LANES), lambda b, t: (b, t, 0))]
    est = d * 2200 * 2 + 2 * rows * d * 4 + 8 * (rows + 2 * HALO) * SSD_CONV_DIM * 4
    return pl.pallas_call(
        functools.partial(_cd_in_kernel, with_fz=with_fz, rows=rows),
        out_shape=out_shape,
        grid=(bsz, seq // rows),
        in_specs=in_specs,
        out_specs=out_specs,
        scratch_shapes=[pltpu.VMEM((rows + 2 * HALO, d), BF16)],
        compiler_params=pltpu.CompilerParams(
            dimension_semantics=("arbitrary", "arbitrary"),
            vmem_limit_bytes=_vmem_limit(est)),
        name="cd_in" if with_fz else "cd_in_ctx",
    )(*args)


def _cumsum_rows(z, reverse):
    n = z.shape[0]
    idx = lax.broadcasted_iota(jnp.int32, z.shape, 0)
    k = 1
    while k < n:
        if reverse:
            z = z + jnp.where(idx < n - k, pltpu.roll(z, n - k, 0), 0.0)
        else:
            z = z + jnp.where(idx >= k, pltpu.roll(z, k, 0), 0.0)
        k *= 2
    return z


def _lane_expand(row, lane0, width):
    lane = lax.broadcasted_iota(jnp.int32, (1, SSD_HPG * width), 1)
    out = jnp.zeros((1, SSD_HPG * width), F32)
    for j in range(SSD_HPG):
        val = jnp.broadcast_to(row[:, lane0 + j:lane0 + j + 1], (1, SSD_HPG * width))
        out = jnp.where(lane // width == j, val, out)
    return out


def _ssd_chunk(x, bm, cm, dt, a_row, h_ref, lane0, reverse, need_y):
    q = x.shape[0]
    gw = SSD_HPG * SSD_HEADDIM
    acum = _cumsum_rows(dt * a_row, reverse)
    total = acum[0:1, :] if reverse else acum[q - 1:q, :]
    acum_t = acum.T
    dt_t = dt.T
    total_t = jnp.broadcast_to(acum_t[:, 0:1] if reverse else acum_t[:, q - 1:q], acum_t.shape)
    wgt_t = dt_t * jnp.exp2(total_t - acum_t)
    lane_g = lax.broadcasted_iota(jnp.int32, (q, gw), 1) // SSD_HEADDIM
    lane_gh = lax.broadcasted_iota(jnp.int32, (SSD_STATE, gw), 1) // SSD_HEADDIM
    if need_y:
        srow_t = acum_t - jnp.log2(dt_t)
        ti = lax.broadcasted_iota(jnp.int32, (q, q), 0)
        si = lax.broadcasted_iota(jnp.int32, (q, q), 1)
        mask = (si >= ti) if reverse else (si <= ti)
    ys = []
    for g in range(SSD_GROUPS):
        bg = bm[:, g * SSD_STATE:(g + 1) * SSD_STATE]
        xgb = x[:, g * gw:(g + 1) * gw].astype(BF16)
        x_heads = [jnp.where(lane_g == j, xgb, jnp.zeros_like(xgb)) for j in range(SSD_HPG)]
        hg = h_ref[g]
        if need_y:
            cg = cm[:, g * SSD_STATE:(g + 1) * SSD_STATE]
            cb = _dot_nt(cg, bg)
            cgf = cg.astype(F32)
            hgb = hg.astype(BF16)
            lhs, rhs = [], []
            for j in range(SSD_HPG):
                jl = lane0 + g * SSD_HPG + j
                col = jnp.broadcast_to(acum[:, jl:jl + 1], (q, q))
                row = jnp.broadcast_to(srow_t[jl:jl + 1, :], (q, q))
                lhs.append((cb * jnp.exp2(jnp.where(mask, col - row, -jnp.inf))).astype(BF16))
                lhs.append((cgf * jnp.exp2(col[:, 0:SSD_STATE])).astype(BF16))
                rhs.append(x_heads[j])
                rhs.append(jnp.where(lane_gh == j, hgb, jnp.zeros_like(hgb)))
            ys.append(_dot(jnp.concatenate(lhs, axis=1), jnp.concatenate(rhs, axis=0)))
        bgt = bg.astype(F32).T
        wb = [(bgt * jnp.broadcast_to(wgt_t[lane0 + g * SSD_HPG + j:lane0 + g * SSD_HPG + j + 1, :],
                                      (SSD_STATE, q))).astype(BF16) for j in range(SSD_HPG)]
        st = _dot(jnp.concatenate(wb, axis=1), jnp.concatenate(x_heads, axis=0))
        h_ref[g] = hg * jnp.exp2(_lane_expand(total, lane0 + g * SSD_HPG, SSD_HEADDIM)) + st
    if need_y:
        return jnp.concatenate(ys, axis=1)
    return None


def _ssd_kernel(xc_ref, bc_ref, dtc_ref, xl_ref, bl_ref, cl_ref, dtl_ref, z_ref, alog_ref,
                dskip_ref, ng_ref, o_ref, y_ref, h_ref):
    a_row = -jnp.exp(alog_ref[...]) * math.log2(math.e)
    nc_ctx = xc_ref.shape[1] // SSD_CHUNK
    nc_lat = xl_ref.shape[1] // SSD_CHUNK

    def rows_of(c):
        return pl.ds(pl.multiple_of(c * SSD_CHUNK, SSD_CHUNK), SSD_CHUNK)

    def ctx_step(c, direction):
        r = rows_of(c)
        _ssd_chunk(xc_ref[0, r, :], bc_ref[0, r, :], None, dtc_ref[0, r, :], a_row,
                   h_ref.at[direction], direction * SSD_HEADS, direction == 1, False)

    def lat_step(c, direction):
        r = rows_of(c)
        y_ref[direction, r, :] = _ssd_chunk(
            xl_ref[0, r, :], bl_ref[0, r, :], cl_ref[0, r, :], dtl_ref[0, r, :], a_row,
            h_ref.at[direction], direction * SSD_HEADS, direction == 1, True)

    h_ref[...] = jnp.zeros_like(h_ref)

    def ctx_pair(i, carry):
        ctx_step(i, 0)
        ctx_step(nc_ctx - 1 - i, 1)
        return carry

    lax.fori_loop(0, nc_ctx, ctx_pair, 0)

    def lat_pair(i, carry):
        lat_step(i, 0)
        lat_step(nc_lat - 1 - i, 1)
        return carry

    lax.fori_loop(0, nc_lat, lat_pair, 0)

    def finish(c, carry):
        r = rows_of(c)
        zz = z_ref[0, r, :]
        yt = (y_ref[0, r, :] + y_ref[1, r, :] + xl_ref[0, r, :] * dskip_ref[...]) * _silu(zz)
        o_ref[0, r, :] = _rms_norm(yt, ng_ref[...]).astype(BF16)
        return carry

    lax.fori_loop(0, nc_lat, finish, 0)


def _ssd(xs_c, b_c, dt_c, xs_l, b_l, c_l, dt_l, z, alog_row, dskip_row, ng_row):
    bsz, sl, w = xs_l.shape
    sc = xs_c.shape[1]
    assert sl % SSD_CHUNK == 0 and sc % SSD_CHUNK == 0
    gw = SSD_HPG * SSD_HEADDIM

    def full(s, n):
        return pl.BlockSpec((1, s, n), lambda b: (b, 0, 0))

    est = (2 * (sl + sc) * (w * 4 + 2 * SSD_GN * 2 + V7X_LANES * 4) + 2 * sl * w * 4
           + 2 * sl * w * 2 + 2 * sl * w * 4)
    return pl.pallas_call(
        _ssd_kernel,
        out_shape=jax.ShapeDtypeStruct((bsz, sl, w), BF16),
        grid=(bsz,),
        in_specs=[
            full(sc, w), full(sc, SSD_GN), full(sc, V7X_LANES),
            full(sl, w), full(sl, SSD_GN), full(sl, SSD_GN), full(sl, V7X_LANES),
            full(sl, w),
            _resident((1, V7X_LANES)), _resident((1, w)), _resident((1, w)),
        ],
        out_specs=full(sl, w),
        scratch_shapes=[
            pltpu.VMEM((2, sl, w), F32),
            pltpu.VMEM((2, SSD_GROUPS, SSD_STATE, gw), F32),
        ],
        compiler_params=pltpu.CompilerParams(
            dimension_semantics=("arbitrary",),
            vmem_limit_bytes=_vmem_limit(est)),
        name="ssd_scan",
    )(xs_c, b_c, dt_c, xs_l, b_l, c_l, dt_l, z, alog_row, dskip_row, ng_row)


def _fourier_kernel(f_ref, cc_ref, sc_ref, w2_ref, o_ref, y_ref, *, rows):
    seq = f_ref.shape[1]
    for g in range(FN_GROUPS):
        l = slice(g * FN_GDIM, (g + 1) * FN_GDIM)
        fg = f_ref[0, :, l]
        y_ref[0:seq, l] = _dot(fg, cc_ref[...]).astype(BF16)
        y_ref[seq:2 * seq, l] = _dot(fg, sc_ref[...]).astype(BF16)
    norm = 1.0 / math.sqrt(seq * FN_GDIM)

    def tile(i, carry):
        r = pl.ds(pl.multiple_of(i * rows, rows), rows)
        o_ref[0, r, :] = (_dot(w2_ref[r, :], y_ref[...]) * norm).astype(BF16)
        return carry

    lax.fori_loop(0, seq // rows, tile, 0)


def _fourier(f, cc, sc, w2, rows):
    bsz, seq, w = f.shape
    est = seq * 2 * seq * 2 + 4 * seq * w * 2 + 2 * seq * w * 2 + 4 * rows * 2 * seq * 2
    return pl.pallas_call(
        functools.partial(_fourier_kernel, rows=rows),
        out_shape=jax.ShapeDtypeStruct((bsz, seq, w), BF16),
        grid=(bsz,),
        in_specs=[
            pl.BlockSpec((1, seq, w), lambda b: (b, 0, 0)),
            _resident((FN_GDIM, FN_GDIM)), _resident((FN_GDIM, FN_GDIM)),
            _resident((seq, 2 * seq)),
        ],
        out_specs=pl.BlockSpec((1, seq, w), lambda b: (b, 0, 0)),
        scratch_shapes=[pltpu.VMEM((2 * seq, w), BF16)],
        compiler_params=pltpu.CompilerParams(
            dimension_semantics=("arbitrary",),
            vmem_limit_bytes=_vmem_limit(est)),
        name="fourier_mix",
    )(f, cc, sc, w2)


def _rope_tables(seq):
    rows = seq // GRID_W
    row = jnp.repeat(jnp.arange(rows, dtype=F32), GRID_W)
    col = jnp.broadcast_to(jnp.arange(GRID_W, dtype=F32), (rows, GRID_W)).reshape(-1)
    n_freq = HEAD_DIM // 4
    inv = ROPE_THETA ** (-jnp.arange(n_freq, dtype=F32) / n_freq)
    ang = jnp.stack([row, col], axis=-1)[:, :, None] * inv
    cos, sin = jnp.cos(ang), jnp.sin(ang)
    cos_t = jnp.concatenate([cos[:, 0], cos[:, 1], cos[:, 0], cos[:, 1]], axis=-1)
    sin_t = jnp.concatenate([-sin[:, 0], -sin[:, 1], sin[:, 0], sin[:, 1]], axis=-1)
    return cos_t, sin_t


def _rope_head_order(m):
    lead = m.shape[:-1]
    nh = m.shape[-1] // HEAD_DIM
    m = m.reshape(lead + (nh, 2, 2, HEAD_DIM // 4))
    return jnp.swapaxes(m, -3, -2).reshape(lead + (nh * HEAD_DIM,))


def _dft_tables(seq):
    def tw(n):
        i = np.arange(n, dtype=np.int64)
        ang = ((i[:, None] * i[None, :]) % n).astype(np.float64) * (2.0 * math.pi / n)
        return np.cos(ang), np.sin(ang)

    cc, sc = tw(FN_GDIM)
    cl, sl = tw(seq)
    w2 = np.concatenate([cl, -sl], axis=1).astype(np.float32)
    return (jnp.asarray(cc.astype(np.float32), BF16), jnp.asarray(sc.astype(np.float32), BF16),
            jnp.asarray(w2, BF16))


def _ffn_params(w_up, conv_w, conv_b, w_down):
    depth, d, _ = w_up.shape
    half_gate = jnp.concatenate([jnp.ones((D_FF,), F32), jnp.full((D_FF,), 0.5, F32)])
    return (w_up.astype(BF16), conv_w * half_gate, (conv_b * half_gate).reshape(depth, 1, 2 * D_FF),
            w_down.astype(BF16).reshape(depth, D_FF // FF_CHUNK, FF_CHUNK, d))


def kernel(x, c, ctx, c_ctx, mod_w, mod_b, ln1_g, ln1_b, ln2_g, ln2_b, ffn_up, ffn_conv_w, ffn_conv_b,
           ffn_down, ab_w_in, ab_w_out, gm_ln_g, gm_ln_b, gm_ws, gm_bs, q_norm_g, k_norm_g, cd_w_in,
           cd_w_out, ssd_conv_w, ssd_conv_b, ssd_dt_bias, ssd_a_log, ssd_d, ssd_norm_g):
    bsz, seq, d = x.shape
    ctx_len = ctx.shape[1]
    lat_rows = min(1024, seq)
    out_rows = min(2048, seq)
    dft_rows = min(2048, seq)
    ctx_rows = min(256, ctx_len)

    pad = (-(bsz + 1)) % V7X_SUBLANES
    cc = jnp.concatenate([c, c_ctx[None], jnp.zeros((pad, d), F32)], axis=0)
    mod = _modulation(cc, mod_w, mod_b)
    rope_tabs = _rope_tables(seq)

    def row(v):
        return v.reshape(1, -1)

    ffn_p = _ffn_params(ffn_up, ffn_conv_w, ffn_conv_b, ffn_down)
    xl, xc = x, ctx
    for i in range(DEPTH):
        last = i == DEPTH - 1
        j = i // 2
        ml = mod[i, :bsz].reshape(bsz, 6, 1, d)
        mc = jnp.broadcast_to(mod[i, bsz].reshape(1, 6, 1, d), (bsz, 6, 1, d))
        sh1, sc1, g1, sh2, sc2, g2 = (ml[:, k] for k in range(6))
        csh1, csc1, cg1, csh2, csc2, cg2 = (mc[:, k] for k in range(6))
        ln1 = (row(ln1_g[i]), row(ln1_b[i]))
        ln2 = (row(ln2_g[i]), row(ln2_b[i]))

        if i % 2 == 0:
            w_in = ab_w_in[j].astype(BF16)
            qk_lo, qk_hi = 2 * GM_WIDTH, KV_OFF + KV_W
            q_scale = HEAD_DIM ** -0.5 * math.log2(math.e)
            w_in = jnp.concatenate([0.5 * w_in[:, :qk_lo], _rope_head_order(w_in[:, qk_lo:qk_hi]),
                                    w_in[:, qk_hi:]], axis=1)
            w_out = ab_w_out[j].astype(BF16)
            gm = (row(gm_ln_g[j]), row(gm_ln_b[j]), gm_ws[j].astype(BF16),
                  jnp.repeat(gm_bs[j].T, GM_WIDTH // GM_GROUPS, axis=1),
                  row(_rope_head_order(q_norm_g[j]) * q_scale), row(_rope_head_order(k_norm_g[j])))
            kv_len = seq + ctx_len
            a_c, q_c, k_all, v_all = _ab_in(xc, csh1, csc1, w_in, *gm, None, ctx_rows, kv_len, seq)
            a_l, q_l, k_all, v_all = _ab_in(xl, sh1, sc1, w_in, *gm, rope_tabs, out_rows, kv_len, 0,
                                            kv_into=(k_all, v_all))
            mix_l = (a_l, _attention(q_l, k_all, v_all, 0, kv_len, out_rows))
            if not last:
                mix_c = (a_c, _attention(q_c, k_all, v_all, seq, ctx_len, ctx_rows))
        else:
            w_in = cd_w_in[j].astype(BF16)
            w_out = cd_w_out[j].astype(BF16)
            wfz = w_in[:, :SSM_OFF]
            wx = w_in[:, SSM_OFF:SSM_OFF + SSD_CONV_DIM]
            ndt = 2 * SSD_HEADS
            wdt = jnp.pad(w_in[:, SSM_OFF + SSD_CONV_DIM:], ((0, 0), (0, V7X_LANES - ndt)))
            dtb = jnp.pad(ssd_dt_bias[j].reshape(1, ndt), ((0, 0), (0, V7X_LANES - ndt)))
            alog = jnp.pad(ssd_a_log[j].reshape(1, ndt), ((0, 0), (0, V7X_LANES - ndt)))
            cw, cb = 0.5 * ssd_conv_w[j], 0.5 * row(ssd_conv_b[j])
            dskip = row(jnp.repeat(ssd_d[j], SSD_HEADDIM))
            f_l, z_l, xs_l, b_l, c_l, dt_l = _cd_in(xl, sh1, sc1, wfz, wx, wdt, cw, cb, dtb, lat_rows)
            if last:
                xs_c, b_c, _, dt_c = _cd_in(xc, csh1, csc1, None, wx, wdt, cw, cb, dtb, ctx_rows)
            else:
                raise NotImplementedError("odd layers with a context output are not part of this block")
            s_l = _ssd(xs_c, b_c, dt_c, xs_l, b_l, c_l, dt_l, z_l, alog, dskip, row(ssd_norm_g[j]))
            mix_l = (_fourier(f_l, *_dft_tables(seq), dft_rows), s_l)

        xl = _out_ln(*mix_l, w_out, xl, g1, *ln1, out_rows)
        xl = _ffn(xl, sh2, sc2, g2, *ffn_p, i, *ln2, lat_rows)
        if not last:
            def stacked(v, n):
                return v.reshape(bsz // n, n * ctx_len, v.shape[-1])
            n_out = math.gcd(bsz, max(1, out_rows // ctx_len))
            n_ffn = math.gcd(bsz, max(1, lat_rows // ctx_len))
            xc = _out_ln(*(stacked(m, n_out) for m in mix_c), w_out, stacked(xc, n_out), cg1[:bsz // n_out],
                         *ln1, n_out * ctx_len)
            xc = _ffn(stacked(xc.reshape(bsz, ctx_len, d), n_ffn), csh2[:bsz // n_ffn], csc2[:bsz // n_ffn],
                      cg2[:bsz // n_ffn], *ffn_p, i, *ln2, n_ffn * ctx_len, seg=ctx_len)
            xc = xc.reshape(bsz, ctx_len, d)
    return xl
```

```python
import functools
import math

import jax
import jax.numpy as jnp
import numpy as np
from jax import lax
from jax.experimental import pallas as pl
from jax.experimental.pallas import tpu as pltpu

D_MODEL = 1024
DEPTH = 2
GRID_W = 64
CHUNK = 128
GM_GROUPS = 4
GM_WIDTH = 512
HEAD_DIM = 128
N_Q_HEADS = 4
N_KV_HEADS = 2
ATTN_WIDTH = 512
KV_W = 256
ROPE_THETA = 10000.0
FN_GROUPS = 4
FN_GDIM = 128
FN_WIDTH = 512
SSD_HEADDIM = 64
SSD_HEADS = 8
SSD_GROUPS = 2
SSD_HPG = 4
SSD_STATE = 128
SSD_WIDTH = 512
SSD_GN = 256
SSD_CONV_DIM = 1024
D_FF = 2816
CONV_W = 3
KV_OFF = 2 * GM_WIDTH + ATTN_WIDTH
SSM_OFF = FN_WIDTH + SSD_WIDTH
DN_ALPHA = (2 * DEPTH) ** 0.25
EPS = 1e-6

V7X_LANES = 128
V7X_SUBLANES = 8
V7X_VMEM_BYTES = 64 * 1024 * 1024
VMEM_CHIP_HEADROOM = 3 << 20
VMEM_TEMP_ALLOWANCE = 8 << 20
MOD_COLS = 2048
AB_SUB = 256
FF_CHUNK = 256
SSD_CHUNK = 256

BF16 = jnp.bfloat16
F32 = jnp.float32
HALO = V7X_SUBLANES


def _vmem_limit(nbytes):
    return int(min(nbytes * 3 // 2 + VMEM_TEMP_ALLOWANCE, V7X_VMEM_BYTES - VMEM_CHIP_HEADROOM))


def _resident(shape):
    nd = len(shape)
    return pl.BlockSpec(shape, lambda *_: (0,) * nd, pipeline_mode=pl.Buffered(1))


def _dot(a, b):
    return jnp.dot(a, b, preferred_element_type=F32)


def _dot_nt(a, b):
    return lax.dot_general(a, b, (((1,), (1,)), ((), ())), preferred_element_type=F32)


def _layer_norm(z, g, b, eps=EPS):
    mu = jnp.mean(z, axis=-1, keepdims=True)
    zc = z - mu
    var = jnp.mean(zc * zc, axis=-1, keepdims=True)
    return zc * lax.rsqrt(var + eps) * g + b


def _deepnorm(x, gate, y, g, b):
    return _layer_norm(x + (gate * (1.0 / DN_ALPHA)) * y, g, b, eps=EPS / DN_ALPHA ** 2)


def _rms_norm(z, g):
    return z * lax.rsqrt(jnp.mean(z * z, axis=-1, keepdims=True) + EPS) * g


def _softplus(z):
    return jnp.maximum(z, 0.0) + jnp.log1p(jnp.exp(-jnp.abs(z)))


def _gelu_tanh_of_half(h):
    c1 = math.sqrt(2.0 / math.pi)
    return h + h * jnp.tanh(h * (2.0 * c1 + (8.0 * c1 * 0.044715) * (h * h)))


def _silu(z):
    h = 0.5 * z
    return h + h * jnp.tanh(h)


def _conv3(u, w, b, rows, seg):
    tiles, c = rows // V7X_SUBLANES, u.shape[1]
    u3 = u.reshape(tiles + 2, V7X_SUBLANES, c)
    sub = lax.broadcasted_iota(jnp.int32, (tiles, V7X_SUBLANES, c), 1)
    dn = pltpu.roll(u3, 1, 1)
    up = pltpu.roll(u3, V7X_SUBLANES - 1, 1)
    prev = jnp.where(sub == 0, dn[0:tiles], dn[1:tiles + 1])
    nxt = jnp.where(sub == V7X_SUBLANES - 1, up[2:tiles + 2], up[1:tiles + 1])
    if seg < rows:
        seg_tiles = seg // V7X_SUBLANES
        til = lax.broadcasted_iota(jnp.int32, (tiles, V7X_SUBLANES, c), 0) % seg_tiles
        prev = jnp.where((sub == 0) & (til == 0), 0.0, prev)
        nxt = jnp.where((sub == V7X_SUBLANES - 1) & (til == seg_tiles - 1), 0.0, nxt)
    return (b + w[0:1] * prev + w[1:2] * u3[1:tiles + 1] + w[2:3] * nxt).reshape(rows, c)


def _fill_halo_tile(hm_ref, x_ref, xp_ref, xn_ref, sh, sc, rows):
    t = pl.program_id(1)
    nt = pl.num_programs(1)
    keep_prev = jnp.where(t > 0, 1.0, 0.0)
    keep_next = jnp.where(t < nt - 1, 1.0, 0.0)
    hm_ref[HALO:HALO + rows, :] = (x_ref[0] * (1.0 + sc) + sh).astype(BF16)
    hm_ref[0:HALO, :] = ((xp_ref[0] * (1.0 + sc) + sh) * keep_prev).astype(BF16)
    hm_ref[HALO + rows:2 * HALO + rows, :] = ((xn_ref[0] * (1.0 + sc) + sh) * keep_next).astype(BF16)


def _halo_specs(rows, seq, d):
    nb = rows // HALO
    last = seq // HALO - 1
    return [
        pl.BlockSpec((1, rows, d), lambda b, t: (b, t, 0)),
        pl.BlockSpec((1, HALO, d), lambda b, t: (b, jnp.maximum(t * nb - 1, 0), 0)),
        pl.BlockSpec((1, HALO, d), lambda b, t: (b, jnp.minimum((t + 1) * nb, last), 0)),
    ]


def _row_spec(d):
    return pl.BlockSpec((1, 1, d), lambda b, t: (b, 0, 0))


def _mod_kernel(c_ref, w_ref, b_ref, o_ref):
    c = c_ref[...]
    act = c * jax.nn.sigmoid(c)
    o_ref[0] = jnp.dot(act, w_ref[0], preferred_element_type=F32,
                       precision=lax.Precision.HIGHEST) + b_ref[0]


def _modulation(cc, mod_w, mod_b):
    rows, d = cc.shape
    depth, _, n = mod_w.shape
    tn = MOD_COLS
    return pl.pallas_call(
        _mod_kernel,
        out_shape=jax.ShapeDtypeStruct((depth, rows, n), F32),
        grid=(depth, n // tn),
        in_specs=[
            pl.BlockSpec((rows, d), lambda i, j: (0, 0)),
            pl.BlockSpec((1, d, tn), lambda i, j: (i, 0, j)),
            pl.BlockSpec((1, 1, tn), lambda i, j: (i, 0, j)),
        ],
        out_specs=pl.BlockSpec((1, rows, tn), lambda i, j: (i, 0, j)),
        compiler_params=pltpu.CompilerParams(
            dimension_semantics=("arbitrary", "arbitrary"),
            vmem_limit_bytes=_vmem_limit(2 * d * tn * 4)),
        name="modulation",
    )(cc, mod_w, mod_b.reshape(depth, 1, n))


def _ab_in_kernel(*refs, use_rope, rows):
    (x_ref, sh_ref, sc_ref, w_ref, lng_ref, lnb_ref, ws_ref, bias_ref, qg_ref, kg_ref) = refs[:10]
    if use_rope:
        cos_ref, sin_ref = refs[10:12]
    a_ref, q_ref, k_ref, v_ref = refs[-4:]

    gd = GM_WIDTH // GM_GROUPS
    slab_rows = min(AB_SUB, rows)

    def slab(i, carry):
        r0 = pl.multiple_of(i * slab_rows, slab_rows)
        rs = pl.ds(r0, slab_rows)
        h = (x_ref[0, rs, :] * (1.0 + sc_ref[0]) + sh_ref[0]).astype(BF16)

        u = _gelu_tanh_of_half(_dot(h, w_ref[:, 0:GM_WIDTH]))
        vn = _layer_norm(_gelu_tanh_of_half(_dot(h, w_ref[:, GM_WIDTH:2 * GM_WIDTH])),
                         lng_ref[...], lnb_ref[...])
        vb = vn.astype(BF16)
        for c in range(slab_rows // CHUNK):
            r = slice(c * CHUNK, (c + 1) * CHUNK)
            for g in range(GM_GROUPS):
                l = slice(g * gd, (g + 1) * gd)
                mixed = _dot(ws_ref[g], vb[r, l]) + bias_ref[:, l]
                a_ref[0, pl.ds(r0 + c * CHUNK, CHUNK), l] = (u[r, l] * mixed).astype(BF16)

        def rope(z):
            if not use_rope:
                return z
            return z * cos_ref[rs, :] + pltpu.roll(z, HEAD_DIM // 2, 1) * sin_ref[rs, :]

        pq = _dot(h, w_ref[:, 2 * GM_WIDTH:KV_OFF])
        for hh in range(N_Q_HEADS):
            l = slice(hh * HEAD_DIM, (hh + 1) * HEAD_DIM)
            q_ref[0, rs, l] = rope(_rms_norm(pq[:, l], qg_ref[...])).astype(BF16)
        pk = _dot(h, w_ref[:, KV_OFF:KV_OFF + KV_W])
        for hh in range(N_KV_HEADS):
            l = slice(hh * HEAD_DIM, (hh + 1) * HEAD_DIM)
            k_ref[0, rs, l] = rope(_rms_norm(pk[:, l], kg_ref[...])).astype(BF16)
        v_ref[0, rs, :] = _dot(h, w_ref[:, KV_OFF + KV_W:KV_OFF + 2 * KV_W]).astype(BF16)
        return carry

    lax.fori_loop(0, rows // slab_rows, slab, 0)


def _ab_in(x, sh, sc, w, lng, lnb, ws, bias2d, qg, kg, rope_tabs, rows, kv_len, kv_row0, kv_into=None):
    bsz, seq, d = x.shape
    assert kv_row0 % rows == 0
    kv_blk0 = kv_row0 // rows
    n = w.shape[1]
    use_rope = rope_tabs is not None
    in_specs = [
        pl.BlockSpec((1, rows, d), lambda b, t: (b, t, 0)),
        _row_spec(d), _row_spec(d),
        _resident((d, n)),
        _resident((1, GM_WIDTH)), _resident((1, GM_WIDTH)),
        _resident((GM_GROUPS, CHUNK, CHUNK)),
        _resident((CHUNK, GM_WIDTH)),
        _resident((1, HEAD_DIM)), _resident((1, HEAD_DIM)),
    ]
    args = [x, sh, sc, w, lng, lnb, ws, bias2d, qg, kg]
    if use_rope:
        in_specs += [pl.BlockSpec((rows, HEAD_DIM), lambda b, t: (t, 0))] * 2
        args += list(rope_tabs)
    aliases = {}
    if kv_into is not None:
        aliases = {len(args): 2, len(args) + 1: 3}
        in_specs += [pl.BlockSpec(memory_space=pl.ANY)] * 2
        args += list(kv_into)
    out_shape = [
        jax.ShapeDtypeStruct((bsz, seq, GM_WIDTH), BF16),
        jax.ShapeDtypeStruct((bsz, seq, ATTN_WIDTH), BF16),
        jax.ShapeDtypeStruct((bsz, kv_len, KV_W), BF16),
        jax.ShapeDtypeStruct((bsz, kv_len, KV_W), BF16),
    ]
    out_specs = [
        pl.BlockSpec((1, rows, GM_WIDTH), lambda b, t: (b, t, 0)),
        pl.BlockSpec((1, rows, ATTN_WIDTH), lambda b, t: (b, t, 0)),
        pl.BlockSpec((1, rows, KV_W), lambda b, t: (b, t + kv_blk0, 0)),
        pl.BlockSpec((1, rows, KV_W), lambda b, t: (b, t + kv_blk0, 0)),
    ]
    est = d * n * 2 + 2 * rows * d * 4 + 6 * rows * n * 4
    return pl.pallas_call(
        functools.partial(_ab_in_kernel, use_rope=use_rope, rows=rows),
        out_shape=out_shape,
        grid=(bsz, seq // rows),
        in_specs=in_specs,
        out_specs=out_specs,
        input_output_aliases=aliases,
        compiler_params=pltpu.CompilerParams(
            dimension_semantics=("arbitrary", "arbitrary"),
            vmem_limit_bytes=_vmem_limit(est)),
        name="ab_in_rope" if use_rope else "ab_in_ctx",
    )(*args)


def _attn_kernel(q_ref, k_ref, v_ref, o_ref, vt_ref, *, rows, blk):
    @pl.when(pl.program_id(2) == 0)
    def _():
        vt_ref[...] = v_ref[0].astype(F32).T

    k = k_ref[0]
    vt = vt_ref[...].astype(BF16)
    for h in range(N_Q_HEADS // N_KV_HEADS):
        l_ = slice(h * HEAD_DIM, (h + 1) * HEAD_DIM)
        for r0 in range(0, rows, blk):
            st = _dot_nt(k, q_ref[0, r0:r0 + blk, l_])
            m = jnp.max(st, axis=0, keepdims=True)
            p = jnp.exp2(st - m)
            l = jnp.sum(p, axis=0, keepdims=True)
            ot = _dot(vt, p.astype(BF16)) / l
            o_ref[0, r0:r0 + blk, l_] = ot.T.astype(BF16)


def _attention(q, k, v, k_row0, sk, rows, blk=2048):
    bsz, sq, _ = q.shape
    assert k_row0 % sk == 0
    kb = k_row0 // sk
    gw = (N_Q_HEADS // N_KV_HEADS) * HEAD_DIM
    est = 4 * blk * sk * 12 + 4 * sk * HEAD_DIM * 2 + 4 * rows * gw * 2 + sk * HEAD_DIM * 4
    return pl.pallas_call(
        functools.partial(_attn_kernel, rows=rows, blk=min(blk, rows)),
        out_shape=jax.ShapeDtypeStruct((bsz, sq, ATTN_WIDTH), BF16),
        grid=(bsz, N_KV_HEADS, sq // rows),
        in_specs=[
            pl.BlockSpec((1, rows, gw), lambda b, h, t: (b, t, h)),
            pl.BlockSpec((1, sk, HEAD_DIM), lambda b, h, t: (b, kb, h)),
            pl.BlockSpec((1, sk, HEAD_DIM), lambda b, h, t: (b, kb, h)),
        ],
        out_specs=pl.BlockSpec((1, rows, gw), lambda b, h, t: (b, t, h)),
        scratch_shapes=[pltpu.VMEM((HEAD_DIM, sk), F32)],
        compiler_params=pltpu.CompilerParams(
            dimension_semantics=("arbitrary", "arbitrary", "arbitrary"),
            vmem_limit_bytes=_vmem_limit(est)),
        name="attention",
    )(q, k, v)


def _out_ln_kernel(a1_ref, a2_ref, w_ref, x_ref, g_ref, lng_ref, lnb_ref, o_ref):
    half = a1_ref.shape[2]
    y = _dot(a1_ref[0], w_ref[0:half, :]) + _dot(a2_ref[0], w_ref[half:2 * half, :])
    o_ref[0] = _deepnorm(x_ref[0], g_ref[0], y, lng_ref[...], lnb_ref[...])


def _out_ln(a1, a2, w, x, gate, lng, lnb, rows):
    bsz, seq, d = x.shape
    half = a1.shape[2]
    est = 2 * half * d * 2 + 4 * rows * d * 4 + 4 * rows * half * 2 + 2 * rows * d * 4
    return pl.pallas_call(
        _out_ln_kernel,
        out_shape=jax.ShapeDtypeStruct((bsz, seq, d), F32),
        grid=(bsz, seq // rows),
        in_specs=[
            pl.BlockSpec((1, rows, half), lambda b, t: (b, t, 0)),
            pl.BlockSpec((1, rows, half), lambda b, t: (b, t, 0)),
            _resident((2 * half, d)),
            pl.BlockSpec((1, rows, d), lambda b, t: (b, t, 0)),
            _row_spec(d),
            _resident((1, d)), _resident((1, d)),
        ],
        out_specs=pl.BlockSpec((1, rows, d), lambda b, t: (b, t, 0)),
        compiler_params=pltpu.CompilerParams(
            dimension_semantics=("arbitrary", "arbitrary"),
            vmem_limit_bytes=_vmem_limit(est)),
        name="out_ln",
    )(a1, a2, w, x, gate, lng, lnb)


def _ffn_kernel(x_ref, xp_ref, xn_ref, sh_ref, sc_ref, g_ref, wu_ref, cw_ref, cb_ref, wd_ref, lng_ref, lnb_ref,
                o_ref, hm_ref, u_ref, gt_ref, *, rows, seg):
    _fill_halo_tile(hm_ref, x_ref, xp_ref, xn_ref, sh_ref[0], sc_ref[0], rows)
    n_chunks = wd_ref.shape[0]

    def cols(ref, c):
        off = c * FF_CHUNK
        return ref[:, pl.ds(off if isinstance(off, int) else pl.multiple_of(off, FF_CHUNK), FF_CHUNK)]
    tiles = rows // V7X_SUBLANES
    sub = lax.broadcasted_iota(jnp.int32, (tiles, V7X_SUBLANES, FF_CHUNK), 1)
    if seg < rows:
        seg_tiles = seg // V7X_SUBLANES
        til = lax.broadcasted_iota(jnp.int32, (tiles, V7X_SUBLANES, FF_CHUNK), 0) % seg_tiles
        seq_first = (sub == 0) & (til == 0)
        seq_last = (sub == V7X_SUBLANES - 1) & (til == seg_tiles - 1)

    def up_proj(c, slot):
        hm = hm_ref[...]
        u_ref[slot, 0] = _dot(hm, cols(wu_ref, c)).reshape(tiles + 2, V7X_SUBLANES, FF_CHUNK)
        u_ref[slot, 1] = _dot(hm, cols(wu_ref, c + n_chunks)).reshape(tiles + 2, V7X_SUBLANES, FF_CHUNK)

    def conv_gate(c, slot):
        def conv(half):
            u = u_ref[slot, half]
            dn = pltpu.roll(u, 1, 1)
            up = pltpu.roll(u, V7X_SUBLANES - 1, 1)
            prev = jnp.where(sub == 0, dn[0:tiles], dn[1:tiles + 1])
            nxt = jnp.where(sub == V7X_SUBLANES - 1, up[2:tiles + 2], up[1:tiles + 1])
            if seg < rows:
                prev = jnp.where(seq_first, 0.0, prev)
                nxt = jnp.where(seq_last, 0.0, nxt)
            w = cols(cw_ref, c + half * n_chunks)
            return cols(cb_ref, c + half * n_chunks) + w[0:1] * prev + w[1:2] * u[1:tiles + 1] + w[2:3] * nxt

        a1 = conv(0)
        a2 = conv(1)
        gt_ref[c] = (a1 * (a2 + a2 * jnp.tanh(a2))).reshape(rows, FF_CHUNK)

    up_proj(0, 0)

    def pair(k, carry):
        c = 2 * k
        up_proj(c + 1, 1)
        conv_gate(c, 0)
        up_proj(c + 2, 0)
        conv_gate(c + 1, 1)
        return carry

    lax.fori_loop(0, (n_chunks - 1) // 2, pair, 0)
    conv_gate(n_chunks - 1, 0)

    f = _dot(gt_ref[0].astype(BF16), wd_ref[0])
    for c in range(1, n_chunks):
        f = f + _dot(gt_ref[c].astype(BF16), wd_ref[c])
    o_ref[0] = _deepnorm(x_ref[0], g_ref[0], f, lng_ref[...], lnb_ref[...])


def _ffn(x, sh, sc, gate, wu, cw, cb, wd, layer, lng, lnb, rows, seg=None):
    bsz, seq, d = x.shape
    seg = seq if seg is None else seg
    assert seg == seq or (rows == seq and rows % seg == 0 and seg % V7X_SUBLANES == 0)
    nc = wd.shape[1]
    dff = nc * FF_CHUNK
    assert wu.shape[2] == 2 * dff and nc % 2 == 1, "the pipeline peels one chunk and pairs the rest"

    def of_layer(*shape):
        nd = len(shape)
        return pl.BlockSpec((None,) + shape, lambda b, t: (layer,) + (0,) * nd, pipeline_mode=pl.Buffered(1))

    est = (3 * nc * d * FF_CHUNK * 2 + 4 * rows * d * 4 + (rows + 2 * HALO) * d * 2
           + nc * rows * FF_CHUNK * 4 + 8 * (rows + 2 * HALO) * FF_CHUNK * 4 + 2 * rows * d * 4)
    return pl.pallas_call(
        functools.partial(_ffn_kernel, rows=rows, seg=min(seg, rows)),
        out_shape=jax.ShapeDtypeStruct((bsz, seq, d), F32),
        grid=(bsz, seq // rows),
        in_specs=_halo_specs(rows, seq, d) + [
            _row_spec(d), _row_spec(d), _row_spec(d),
            of_layer(d, 2 * dff), of_layer(CONV_W, 2 * dff), of_layer(1, 2 * dff),
            of_layer(nc, FF_CHUNK, d),
            _resident((1, d)), _resident((1, d)),
        ],
        out_specs=pl.BlockSpec((1, rows, d), lambda b, t: (b, t, 0)),
        scratch_shapes=[
            pltpu.VMEM((rows + 2 * HALO, d), BF16),
            pltpu.VMEM((2, 2, rows // HALO + 2, HALO, FF_CHUNK), F32),
            pltpu.VMEM((nc, rows, FF_CHUNK), F32),
        ],
        compiler_params=pltpu.CompilerParams(
            dimension_semantics=("arbitrary", "arbitrary"),
            vmem_limit_bytes=_vmem_limit(est)),
        name="conv_ffn",
    )(x, x, x, sh, sc, gate, wu, cw, cb, wd, lng, lnb)


def _cd_in_kernel(*refs, with_fz, rows, seg):
    x_ref, xp_ref, xn_ref, sh_ref, sc_ref = refs[:5]
    if with_fz:
        wfz_ref, wx_ref, wdt_ref, cw_ref, cb_ref, dtb_ref = refs[5:11]
        f_ref, z_ref, xs_ref, b_ref, c_ref, dt_ref, hm_ref = refs[11:]
    else:
        wx_ref, wdt_ref, cw_ref, cb_ref, dtb_ref = refs[5:10]
        xs_ref, b_ref, c_ref, dt_ref, hm_ref = refs[10:]
    _fill_halo_tile(hm_ref, x_ref, xp_ref, xn_ref, sh_ref[0], sc_ref[0], rows)
    hc = hm_ref[HALO:HALO + rows, :]
    if with_fz:
        pfz = _dot(hc, wfz_ref[...])
        f_ref[0] = pfz[:, 0:FN_WIDTH].astype(BF16)
        z_ref[0] = pfz[:, FN_WIDTH:SSM_OFF]
    a = _conv3(_dot(hm_ref[...], wx_ref[...]), cw_ref[...], cb_ref[...], rows, seg)
    s = a + a * jnp.tanh(a)
    xs_ref[0] = s[:, 0:SSD_WIDTH]
    b_ref[0] = s[:, SSD_WIDTH:SSD_WIDTH + SSD_GN].astype(BF16)
    c_ref[0] = s[:, SSD_WIDTH + SSD_GN:SSD_CONV_DIM].astype(BF16)
    dt_ref[0] = _softplus(_dot(hc, wdt_ref[...]) + dtb_ref[...])


def _cd_in(x, sh, sc, wfz, wx, wdt, cw, cb, dtb, rows, seg=None):
    bsz, seq, d = x.shape
    seg = seq if seg is None else seg
    assert seg == seq or (rows == seq and rows % seg == 0 and seg % V7X_SUBLANES == 0)
    with_fz = wfz is not None
    in_specs = _halo_specs(rows, seq, d) + [_row_spec(d), _row_spec(d)]
    args = [x, x, x, sh, sc]
    out_shape, out_specs = [], []
    if with_fz:
        in_specs.append(_resident((d, SSM_OFF)))
        args.append(wfz)
        out_shape += [jax.ShapeDtypeStruct((bsz, seq, FN_WIDTH), BF16),
                      jax.ShapeDtypeStruct((bsz, seq, SSD_WIDTH), F32)]
        out_specs += [pl.BlockSpec((1, rows, FN_WIDTH), lambda b, t: (b, t, 0)),
                      pl.BlockSpec((1, rows, SSD_WIDTH), lambda b, t: (b, t, 0))]
    in_specs += [_resident((d, SSD_CONV_DIM)), _resident((d, V7X_LANES)),
                 _resident((CONV_W, SSD_CONV_DIM)), _resident((1, SSD_CONV_DIM)),
                 _resident((1, V7X_LANES))]
    args += [wx, wdt, cw, cb, dtb]
    out_shape += [jax.ShapeDtypeStruct((bsz, seq, SSD_WIDTH), F32),
                  jax.ShapeDtypeStruct((bsz, seq, SSD_GN), BF16),
                  jax.ShapeDtypeStruct((bsz, seq, SSD_GN), BF16),
                  jax.ShapeDtypeStruct((bsz, seq, V7X_LANES), F32)]
    out_specs += [pl.BlockSpec((1, rows, SSD_WIDTH), lambda b, t: (b, t, 0)),
                  pl.BlockSpec((1, rows, SSD_GN), lambda b, t: (b, t, 0)),
                  pl.BlockSpec((1, rows, SSD_GN), lambda b, t: (b, t, 0)),
                  pl.BlockSpec((1, rows, V7X_LANES), lambda b, t: (b, t, 0))]
    est = d * 2200 * 2 + 2 * rows * d * 4 + 8 * (rows + 2 * HALO) * SSD_CONV_DIM * 4
    return pl.pallas_call(
        functools.partial(_cd_in_kernel, with_fz=with_fz, rows=rows, seg=min(seg, rows)),
        out_shape=out_shape,
        grid=(bsz, seq // rows),
        in_specs=in_specs,
        out_specs=out_specs,
        scratch_shapes=[pltpu.VMEM((rows + 2 * HALO, d), BF16)],
        compiler_params=pltpu.CompilerParams(
            dimension_semantics=("arbitrary", "arbitrary"),
            vmem_limit_bytes=_vmem_limit(est)),
        name="cd_in" if with_fz else "cd_in_ctx",
    )(*args)


def _cumsum_rows(z, reverse):
    n = z.shape[0]
    idx = lax.broadcasted_iota(jnp.int32, z.shape, 0)
    k = 1
    while k < n:
        if reverse:
            z = z + jnp.where(idx < n - k, pltpu.roll(z, n - k, 0), 0.0)
        else:
            z = z + jnp.where(idx >= k, pltpu.roll(z, k, 0), 0.0)
        k *= 2
    return z


def _lane_expand(row, lane0, width):
    lane = lax.broadcasted_iota(jnp.int32, (1, SSD_HPG * width), 1)
    out = jnp.zeros((1, SSD_HPG * width), F32)
    for j in range(SSD_HPG):
        val = jnp.broadcast_to(row[:, lane0 + j:lane0 + j + 1], (1, SSD_HPG * width))
        out = jnp.where(lane // width == j, val, out)
    return out


def _ssd_chunk(x, bm, cm, dt, a_row, h_ref, lane0, reverse, need_y):
    q = x.shape[0]
    gw = SSD_HPG * SSD_HEADDIM
    acum = _cumsum_rows(dt * a_row, reverse)
    total = acum[0:1, :] if reverse else acum[q - 1:q, :]
    acum_t = acum.T
    dt_t = dt.T
    total_t = jnp.broadcast_to(acum_t[:, 0:1] if reverse else acum_t[:, q - 1:q], acum_t.shape)
    wgt_t = dt_t * jnp.exp2(total_t - acum_t)
    lane_g = lax.broadcasted_iota(jnp.int32, (q, gw), 1) // SSD_HEADDIM
    lane_gh = lax.broadcasted_iota(jnp.int32, (SSD_STATE, gw), 1) // SSD_HEADDIM
    if need_y:
        srow_t = acum_t - jnp.log2(dt_t)
        ti = lax.broadcasted_iota(jnp.int32, (q, q), 0)
        si = lax.broadcasted_iota(jnp.int32, (q, q), 1)
        mask = (si >= ti) if reverse else (si <= ti)
    ys = []
    for g in range(SSD_GROUPS):
        bg = bm[:, g * SSD_STATE:(g + 1) * SSD_STATE]
        xgb = x[:, g * gw:(g + 1) * gw].astype(BF16)
        x_heads = [jnp.where(lane_g == j, xgb, jnp.zeros_like(xgb)) for j in range(SSD_HPG)]
        hg = h_ref[g]
        if need_y:
            cg = cm[:, g * SSD_STATE:(g + 1) * SSD_STATE]
            cb = _dot_nt(cg, bg)
            cgf = cg.astype(F32)
            hgb = hg.astype(BF16)
            lhs, rhs = [], []
            for j in range(SSD_HPG):
                jl = lane0 + g * SSD_HPG + j
                col = jnp.broadcast_to(acum[:, jl:jl + 1], (q, q))
                row = jnp.broadcast_to(srow_t[jl:jl + 1, :], (q, q))
                lhs.append((cb * jnp.exp2(jnp.where(mask, col - row, -jnp.inf))).astype(BF16))
                lhs.append((cgf * jnp.exp2(col[:, 0:SSD_STATE])).astype(BF16))
                rhs.append(x_heads[j])
                rhs.append(jnp.where(lane_gh == j, hgb, jnp.zeros_like(hgb)))
            ys.append(_dot(jnp.concatenate(lhs, axis=1), jnp.concatenate(rhs, axis=0)))
        bgt = bg.astype(F32).T
        wb = [(bgt * jnp.broadcast_to(wgt_t[lane0 + g * SSD_HPG + j:lane0 + g * SSD_HPG + j + 1, :],
                                      (SSD_STATE, q))).astype(BF16) for j in range(SSD_HPG)]
        st = _dot(jnp.concatenate(wb, axis=1), jnp.concatenate(x_heads, axis=0))
        h_ref[g] = hg * jnp.exp2(_lane_expand(total, lane0 + g * SSD_HPG, SSD_HEADDIM)) + st
    if need_y:
        return jnp.concatenate(ys, axis=1)
    return None


def _ssd_kernel(xc_ref, bc_ref, dtc_ref, xl_ref, bl_ref, cl_ref, dtl_ref, z_ref, alog_ref,
                dskip_ref, ng_ref, o_ref, y_ref, h_ref):
    a_row = -jnp.exp(alog_ref[...]) * math.log2(math.e)
    nc_ctx = xc_ref.shape[1] // SSD_CHUNK
    nc_lat = xl_ref.shape[1] // SSD_CHUNK

    def rows_of(c):
        return pl.ds(pl.multiple_of(c * SSD_CHUNK, SSD_CHUNK), SSD_CHUNK)

    def ctx_step(c, direction):
        r = rows_of(c)
        _ssd_chunk(xc_ref[0, r, :], bc_ref[0, r, :], None, dtc_ref[0, r, :], a_row,
                   h_ref.at[direction], direction * SSD_HEADS, direction == 1, False)

    def lat_step(c, direction):
        r = rows_of(c)
        y_ref[direction, r, :] = _ssd_chunk(
            xl_ref[0, r, :], bl_ref[0, r, :], cl_ref[0, r, :], dtl_ref[0, r, :], a_row,
            h_ref.at[direction], direction * SSD_HEADS, direction == 1, True)

    h_ref[...] = jnp.zeros_like(h_ref)

    def ctx_pair(i, carry):
        ctx_step(i, 0)
        ctx_step(nc_ctx - 1 - i, 1)
        return carry

    lax.fori_loop(0, nc_ctx, ctx_pair, 0)

    def lat_pair(i, carry):
        lat_step(i, 0)
        lat_step(nc_lat - 1 - i, 1)
        return carry

    lax.fori_loop(0, nc_lat, lat_pair, 0)

    def finish(c, carry):
        r = rows_of(c)
        zz = z_ref[0, r, :]
        yt = (y_ref[0, r, :] + y_ref[1, r, :] + xl_ref[0, r, :] * dskip_ref[...]) * _silu(zz)
        o_ref[0, r, :] = _rms_norm(yt, ng_ref[...]).astype(BF16)
        return carry

    lax.fori_loop(0, nc_lat, finish, 0)


def _ssd(xs_c, b_c, dt_c, xs_l, b_l, c_l, dt_l, z, alog_row, dskip_row, ng_row):
    bsz, sl, w = xs_l.shape
    sc = xs_c.shape[1]
    assert sl % SSD_CHUNK == 0 and sc % SSD_CHUNK == 0
    gw = SSD_HPG * SSD_HEADDIM

    def full(s, n):
        return pl.BlockSpec((1, s, n), lambda b: (b, 0, 0))

    est = (2 * (sl + sc) * (w * 4 + 2 * SSD_GN * 2 + V7X_LANES * 4) + 2 * sl * w * 4
           + 2 * sl * w * 2 + 2 * sl * w * 4)
    return pl.pallas_call(
        _ssd_kernel,
        out_shape=jax.ShapeDtypeStruct((bsz, sl, w), BF16),
        grid=(bsz,),
        in_specs=[
            full(sc, w), full(sc, SSD_GN), full(sc, V7X_LANES),
            full(sl, w), full(sl, SSD_GN), full(sl, SSD_GN), full(sl, V7X_LANES),
            full(sl, w),
            _resident((1, V7X_LANES)), _resident((1, w)), _resident((1, w)),
        ],
        out_specs=full(sl, w),
        scratch_shapes=[
            pltpu.VMEM((2, sl, w), F32),
            pltpu.VMEM((2, SSD_GROUPS, SSD_STATE, gw), F32),
        ],
        compiler_params=pltpu.CompilerParams(
            dimension_semantics=("arbitrary",),
            vmem_limit_bytes=_vmem_limit(est)),
        name="ssd_scan",
    )(xs_c, b_c, dt_c, xs_l, b_l, c_l, dt_l, z, alog_row, dskip_row, ng_row)


def _fourier_kernel(f_ref, cc_ref, sc_ref, w2_ref, o_ref, y_ref, *, rows):
    seq = f_ref.shape[1]
    for g in range(FN_GROUPS):
        l = slice(g * FN_GDIM, (g + 1) * FN_GDIM)
        fg = f_ref[0, :, l]
        y_ref[0:seq, l] = _dot(fg, cc_ref[...]).astype(BF16)
        y_ref[seq:2 * seq, l] = _dot(fg, sc_ref[...]).astype(BF16)
    norm = 1.0 / math.sqrt(seq * FN_GDIM)

    def tile(i, carry):
        r = pl.ds(pl.multiple_of(i * rows, rows), rows)
        o_ref[0, r, :] = (_dot(w2_ref[r, :], y_ref[...]) * norm).astype(BF16)
        return carry

    lax.fori_loop(0, seq // rows, tile, 0)


def _fourier(f, cc, sc, w2, rows):
    bsz, seq, w = f.shape
    est = seq * 2 * seq * 2 + 4 * seq * w * 2 + 2 * seq * w * 2 + 4 * rows * 2 * seq * 2
    return pl.pallas_call(
        functools.partial(_fourier_kernel, rows=rows),
        out_shape=jax.ShapeDtypeStruct((bsz, seq, w), BF16),
        grid=(bsz,),
        in_specs=[
            pl.BlockSpec((1, seq, w), lambda b: (b, 0, 0)),
            _resident((FN_GDIM, FN_GDIM)), _resident((FN_GDIM, FN_GDIM)),
            _resident((seq, 2 * seq)),
        ],
        out_specs=pl.BlockSpec((1, seq, w), lambda b: (b, 0, 0)),
        scratch_shapes=[pltpu.VMEM((2 * seq, w), BF16)],
        compiler_params=pltpu.CompilerParams(
            dimension_semantics=("arbitrary",),
            vmem_limit_bytes=_vmem_limit(est)),
        name="fourier_mix",
    )(f, cc, sc, w2)


def _rope_tables(seq):
    rows = seq // GRID_W
    row = jnp.repeat(jnp.arange(rows, dtype=F32), GRID_W)
    col = jnp.broadcast_to(jnp.arange(GRID_W, dtype=F32), (rows, GRID_W)).reshape(-1)
    n_freq = HEAD_DIM // 4
    inv = ROPE_THETA ** (-jnp.arange(n_freq, dtype=F32) / n_freq)
    ang = jnp.stack([row, col], axis=-1)[:, :, None] * inv
    cos, sin = jnp.cos(ang), jnp.sin(ang)
    cos_t = jnp.concatenate([cos[:, 0], cos[:, 1], cos[:, 0], cos[:, 1]], axis=-1)
    sin_t = jnp.concatenate([-sin[:, 0], -sin[:, 1], sin[:, 0], sin[:, 1]], axis=-1)
    return cos_t, sin_t


def _rope_head_order(m):
    lead = m.shape[:-1]
    nh = m.shape[-1] // HEAD_DIM
    m = m.reshape(lead + (nh, 2, 2, HEAD_DIM // 4))
    return jnp.swapaxes(m, -3, -2).reshape(lead + (nh * HEAD_DIM,))


def _dft_tables(seq):
    def tw(n):
        i = np.arange(n, dtype=np.int64)
        ang = ((i[:, None] * i[None, :]) % n).astype(np.float64) * (2.0 * math.pi / n)
        return np.cos(ang), np.sin(ang)

    cc, sc = tw(FN_GDIM)
    cl, sl = tw(seq)
    w2 = np.concatenate([cl, -sl], axis=1).astype(np.float32)
    return (jnp.asarray(cc.astype(np.float32), BF16), jnp.asarray(sc.astype(np.float32), BF16),
            jnp.asarray(w2, BF16))


def _ffn_params(w_up, conv_w, conv_b, w_down):
    depth, d, _ = w_up.shape
    half_gate = jnp.concatenate([jnp.ones((D_FF,), F32), jnp.full((D_FF,), 0.5, F32)])
    return (w_up.astype(BF16), conv_w * half_gate, (conv_b * half_gate).reshape(depth, 1, 2 * D_FF),
            w_down.astype(BF16).reshape(depth, D_FF // FF_CHUNK, FF_CHUNK, d))


def kernel(x, c, ctx, c_ctx, mod_w, mod_b, ln1_g, ln1_b, ln2_g, ln2_b, ffn_up, ffn_conv_w, ffn_conv_b,
           ffn_down, ab_w_in, ab_w_out, gm_ln_g, gm_ln_b, gm_ws, gm_bs, q_norm_g, k_norm_g, cd_w_in,
           cd_w_out, ssd_conv_w, ssd_conv_b, ssd_dt_bias, ssd_a_log, ssd_d, ssd_norm_g):
    bsz, seq, d = x.shape
    ctx_len = ctx.shape[1]
    lat_rows = min(1024, seq)
    out_rows = min(2048, seq)
    dft_rows = min(2048, seq)
    ctx_rows = min(256, ctx_len)

    pad = (-(bsz + 1)) % V7X_SUBLANES
    cc = jnp.concatenate([c, c_ctx[None], jnp.zeros((pad, d), F32)], axis=0)
    mod = _modulation(cc, mod_w, mod_b)
    rope_tabs = _rope_tables(seq)

    def row(v):
        return v.reshape(1, -1)

    ffn_p = _ffn_params(ffn_up, ffn_conv_w, ffn_conv_b, ffn_down)
    xl, xc = x, ctx
    for i in range(DEPTH):
        last = i == DEPTH - 1
        j = i // 2
        ml = mod[i, :bsz].reshape(bsz, 6, 1, d)
        mc = jnp.broadcast_to(mod[i, bsz].reshape(1, 6, 1, d), (bsz, 6, 1, d))
        sh1, sc1, g1, sh2, sc2, g2 = (ml[:, k] for k in range(6))
        csh1, csc1, cg1, csh2, csc2, cg2 = (mc[:, k] for k in range(6))
        ln1 = (row(ln1_g[i]), row(ln1_b[i]))
        ln2 = (row(ln2_g[i]), row(ln2_b[i]))

        if i % 2 == 0:
            w_in = ab_w_in[j].astype(BF16)
            qk_lo, qk_hi = 2 * GM_WIDTH, KV_OFF + KV_W
            q_scale = HEAD_DIM ** -0.5 * math.log2(math.e)
            w_in = jnp.concatenate([0.5 * w_in[:, :qk_lo], _rope_head_order(w_in[:, qk_lo:qk_hi]),
                                    w_in[:, qk_hi:]], axis=1)
            w_out = ab_w_out[j].astype(BF16)
            gm = (row(gm_ln_g[j]), row(gm_ln_b[j]), gm_ws[j].astype(BF16),
                  jnp.repeat(gm_bs[j].T, GM_WIDTH // GM_GROUPS, axis=1),
                  row(_rope_head_order(q_norm_g[j]) * q_scale), row(_rope_head_order(k_norm_g[j])))
            kv_len = seq + ctx_len
            a_c, q_c, k_all, v_all = _ab_in(xc, csh1, csc1, w_in, *gm, None, ctx_rows, kv_len, seq)
            a_l, q_l, k_all, v_all = _ab_in(xl, sh1, sc1, w_in, *gm, rope_tabs, out_rows, kv_len, 0,
                                            kv_into=(k_all, v_all))
            mix_l = (a_l, _attention(q_l, k_all, v_all, 0, kv_len, out_rows))
            if not last:
                mix_c = (a_c, _attention(q_c, k_all, v_all, seq, ctx_len, ctx_rows))
        else:
            w_in = cd_w_in[j].astype(BF16)
            w_out = cd_w_out[j].astype(BF16)
            wfz = w_in[:, :SSM_OFF]
            wx = w_in[:, SSM_OFF:SSM_OFF + SSD_CONV_DIM]
            ndt = 2 * SSD_HEADS
            wdt = jnp.pad(w_in[:, SSM_OFF + SSD_CONV_DIM:], ((0, 0), (0, V7X_LANES - ndt)))
            dtb = jnp.pad(ssd_dt_bias[j].reshape(1, ndt), ((0, 0), (0, V7X_LANES - ndt)))
            alog = jnp.pad(ssd_a_log[j].reshape(1, ndt), ((0, 0), (0, V7X_LANES - ndt)))
            cw, cb = 0.5 * ssd_conv_w[j], 0.5 * row(ssd_conv_b[j])
            dskip = row(jnp.repeat(ssd_d[j], SSD_HEADDIM))
            f_l, z_l, xs_l, b_l, c_l, dt_l = _cd_in(xl, sh1, sc1, wfz, wx, wdt, cw, cb, dtb, lat_rows)
            if last:
                n_cd = math.gcd(bsz, max(1, lat_rows // ctx_len))
                xs_c, b_c, _, dt_c = (
                    v.reshape(bsz, ctx_len, v.shape[-1]) for v in
                    _cd_in(xc.reshape(bsz // n_cd, n_cd * ctx_len, d), csh1[:bsz // n_cd], csc1[:bsz // n_cd],
                           None, wx, wdt, cw, cb, dtb, n_cd * ctx_len, seg=ctx_len))
            else:
                raise NotImplementedError("odd layers with a context output are not part of this block")
            s_l = _ssd(xs_c, b_c, dt_c, xs_l, b_l, c_l, dt_l, z_l, alog, dskip, row(ssd_norm_g[j]))
            mix_l = (_fourier(f_l, *_dft_tables(seq), dft_rows), s_l)

        xl = _out_ln(*mix_l, w_out, xl, g1, *ln1, out_rows)
        xl = _ffn(xl, sh2, sc2, g2, *ffn_p, i, *ln2, lat_rows)
        if not last:
            def stacked(v, n):
                return v.reshape(bsz // n, n * ctx_len, v.shape[-1])
            n_out = math.gcd(bsz, max(1, out_rows // ctx_len))
            n_ffn = math.gcd(bsz, max(1, lat_rows // ctx_len))
            xc = _out_ln(*(stacked(m, n_out) for m in mix_c), w_out, stacked(xc, n_out), cg1[:bsz // n_out],
                         *ln1, n_out * ctx_len)
            xc = _ffn(stacked(xc.reshape(bsz, ctx_len, d), n_ffn), csh2[:bsz // n_ffn], csc2[:bsz // n_ffn],
                      cg2[:bsz // n_ffn], *ffn_p, i, *ln2, n_ffn * ctx_len, seg=ctx_len)
            xc = xc.reshape(bsz, ctx_len, d)
    return xl
```

```python
import functools
import math

import jax
import jax.numpy as jnp
import numpy as np
from jax import lax
from jax.experimental import pallas as pl
from jax.experimental.pallas import tpu as pltpu

D_MODEL = 1024
DEPTH = 2
GRID_W = 64
CHUNK = 128
GM_GROUPS = 4
GM_WIDTH = 512
HEAD_DIM = 128
N_Q_HEADS = 4
N_KV_HEADS = 2
ATTN_WIDTH = 512
KV_W = 256
ROPE_THETA = 10000.0
FN_GROUPS = 4
FN_GDIM = 128
FN_WIDTH = 512
SSD_HEADDIM = 64
SSD_HEADS = 8
SSD_GROUPS = 2
SSD_HPG = 4
SSD_STATE = 128
SSD_WIDTH = 512
SSD_GN = 256
SSD_CONV_DIM = 1024
D_FF = 2816
CONV_W = 3
KV_OFF = 2 * GM_WIDTH + ATTN_WIDTH
SSM_OFF = FN_WIDTH + SSD_WIDTH
DN_ALPHA = (2 * DEPTH) ** 0.25
EPS = 1e-6

V7X_LANES = 128
V7X_SUBLANES = 8
V7X_VMEM_BYTES = 64 * 1024 * 1024
VMEM_CHIP_HEADROOM = 3 << 20
VMEM_TEMP_ALLOWANCE = 8 << 20
MOD_COLS = 2048
AB_SUB = 256
FF_CHUNK = 256
SSD_CHUNK = 256

BF16 = jnp.bfloat16
F32 = jnp.float32
HALO = V7X_SUBLANES


def _vmem_limit(nbytes):
    return int(min(nbytes * 3 // 2 + VMEM_TEMP_ALLOWANCE, V7X_VMEM_BYTES - VMEM_CHIP_HEADROOM))


def _resident(shape):
    nd = len(shape)
    return pl.BlockSpec(shape, lambda *_: (0,) * nd, pipeline_mode=pl.Buffered(1))


def _dot(a, b):
    return jnp.dot(a, b, preferred_element_type=F32)


def _dot_nt(a, b):
    return lax.dot_general(a, b, (((1,), (1,)), ((), ())), preferred_element_type=F32)


def _layer_norm(z, g, b, eps=EPS):
    mu = jnp.mean(z, axis=-1, keepdims=True)
    zc = z - mu
    var = jnp.mean(zc * zc, axis=-1, keepdims=True)
    return zc * lax.rsqrt(var + eps) * g + b


def _deepnorm(x, gate, y, g, b):
    return _layer_norm(x + (gate * (1.0 / DN_ALPHA)) * y, g, b, eps=EPS / DN_ALPHA ** 2)


def _rms_norm(z, g):
    return z * lax.rsqrt(jnp.mean(z * z, axis=-1, keepdims=True) + EPS) * g


def _softplus(z):
    return jnp.maximum(z, 0.0) + jnp.log1p(jnp.exp(-jnp.abs(z)))


def _gelu_tanh_of_half(h):
    c1 = math.sqrt(2.0 / math.pi)
    return h + h * jnp.tanh(h * (2.0 * c1 + (8.0 * c1 * 0.044715) * (h * h)))


def _silu(z):
    h = 0.5 * z
    return h + h * jnp.tanh(h)


def _conv3(u, w, b, rows, seg):
    tiles, c = rows // V7X_SUBLANES, u.shape[1]
    u3 = u.reshape(tiles + 2, V7X_SUBLANES, c)
    sub = lax.broadcasted_iota(jnp.int32, (tiles, V7X_SUBLANES, c), 1)
    dn = pltpu.roll(u3, 1, 1)
    up = pltpu.roll(u3, V7X_SUBLANES - 1, 1)
    prev = jnp.where(sub == 0, dn[0:tiles], dn[1:tiles + 1])
    nxt = jnp.where(sub == V7X_SUBLANES - 1, up[2:tiles + 2], up[1:tiles + 1])
    if seg < rows:
        seg_tiles = seg // V7X_SUBLANES
        til = lax.broadcasted_iota(jnp.int32, (tiles, V7X_SUBLANES, c), 0) % seg_tiles
        prev = jnp.where((sub == 0) & (til == 0), 0.0, prev)
        nxt = jnp.where((sub == V7X_SUBLANES - 1) & (til == seg_tiles - 1), 0.0, nxt)
    return (b + w[0:1] * prev + w[1:2] * u3[1:tiles + 1] + w[2:3] * nxt).reshape(rows, c)


def _fill_halo_tile(hm_ref, x_ref, xp_ref, xn_ref, sh, sc, rows):
    t = pl.program_id(1)
    nt = pl.num_programs(1)
    keep_prev = jnp.where(t > 0, 1.0, 0.0)
    keep_next = jnp.where(t < nt - 1, 1.0, 0.0)
    hm_ref[HALO:HALO + rows, :] = (x_ref[0] * (1.0 + sc) + sh).astype(BF16)
    hm_ref[0:HALO, :] = ((xp_ref[0] * (1.0 + sc) + sh) * keep_prev).astype(BF16)
    hm_ref[HALO + rows:2 * HALO + rows, :] = ((xn_ref[0] * (1.0 + sc) + sh) * keep_next).astype(BF16)


def _halo_specs(rows, seq, d):
    nb = rows // HALO
    last = seq // HALO - 1
    return [
        pl.BlockSpec((1, rows, d), lambda b, t: (b, t, 0)),
        pl.BlockSpec((1, HALO, d), lambda b, t: (b, jnp.maximum(t * nb - 1, 0), 0)),
        pl.BlockSpec((1, HALO, d), lambda b, t: (b, jnp.minimum((t + 1) * nb, last), 0)),
    ]


def _row_spec(d):
    return pl.BlockSpec((1, 1, d), lambda b, t: (b, 0, 0))


def _mod_kernel(c_ref, w_ref, b_ref, o_ref):
    c = c_ref[...]
    act = c * jax.nn.sigmoid(c)
    o_ref[0] = jnp.dot(act, w_ref[0], preferred_element_type=F32,
                       precision=lax.Precision.HIGHEST) + b_ref[0]


def _modulation(cc, mod_w, mod_b):
    rows, d = cc.shape
    depth, _, n = mod_w.shape
    tn = MOD_COLS
    return pl.pallas_call(
        _mod_kernel,
        out_shape=jax.ShapeDtypeStruct((depth, rows, n), F32),
        grid=(depth, n // tn),
        in_specs=[
            pl.BlockSpec((rows, d), lambda i, j: (0, 0)),
            pl.BlockSpec((1, d, tn), lambda i, j: (i, 0, j)),
            pl.BlockSpec((1, 1, tn), lambda i, j: (i, 0, j)),
        ],
        out_specs=pl.BlockSpec((1, rows, tn), lambda i, j: (i, 0, j)),
        compiler_params=pltpu.CompilerParams(
            dimension_semantics=("arbitrary", "arbitrary"),
            vmem_limit_bytes=_vmem_limit(2 * d * tn * 4)),
        name="modulation",
    )(cc, mod_w, mod_b.reshape(depth, 1, n))


def _ab_in_kernel(*refs, use_rope, rows):
    (x_ref, sh_ref, sc_ref, w_ref, lng_ref, lnb_ref, ws_ref, bias_ref, qg_ref, kg_ref) = refs[:10]
    if use_rope:
        cos_ref, sin_ref = refs[10:12]
    a_ref, q_ref, k_ref, v_ref = refs[-4:]

    gd = GM_WIDTH // GM_GROUPS
    slab_rows = min(AB_SUB, rows)

    def slab(i, carry):
        r0 = pl.multiple_of(i * slab_rows, slab_rows)
        rs = pl.ds(r0, slab_rows)
        h = (x_ref[0, rs, :] * (1.0 + sc_ref[0]) + sh_ref[0]).astype(BF16)

        u = _gelu_tanh_of_half(_dot(h, w_ref[:, 0:GM_WIDTH]))
        vn = _layer_norm(_gelu_tanh_of_half(_dot(h, w_ref[:, GM_WIDTH:2 * GM_WIDTH])),
                         lng_ref[...], lnb_ref[...])
        vb = vn.astype(BF16)
        for c in range(slab_rows // CHUNK):
            r = slice(c * CHUNK, (c + 1) * CHUNK)
            for g in range(GM_GROUPS):
                l = slice(g * gd, (g + 1) * gd)
                mixed = _dot(ws_ref[g], vb[r, l]) + bias_ref[:, l]
                a_ref[0, pl.ds(r0 + c * CHUNK, CHUNK), l] = (u[r, l] * mixed).astype(BF16)

        def rope(z):
            if not use_rope:
                return z
            return z * cos_ref[rs, :] + pltpu.roll(z, HEAD_DIM // 2, 1) * sin_ref[rs, :]

        pq = _dot(h, w_ref[:, 2 * GM_WIDTH:KV_OFF])
        for hh in range(N_Q_HEADS):
            l = slice(hh * HEAD_DIM, (hh + 1) * HEAD_DIM)
            q_ref[0, rs, l] = rope(_rms_norm(pq[:, l], qg_ref[...])).astype(BF16)
        pk = _dot(h, w_ref[:, KV_OFF:KV_OFF + KV_W])
        for hh in range(N_KV_HEADS):
            l = slice(hh * HEAD_DIM, (hh + 1) * HEAD_DIM)
            k_ref[0, rs, l] = rope(_rms_norm(pk[:, l], kg_ref[...])).astype(BF16)
        v_ref[0, rs, :] = _dot(h, w_ref[:, KV_OFF + KV_W:KV_OFF + 2 * KV_W]).astype(BF16)
        return carry

    lax.fori_loop(0, rows // slab_rows, slab, 0)


def _ab_in(x, sh, sc, w, lng, lnb, ws, bias2d, qg, kg, rope_tabs, rows, kv_len, kv_row0, kv_into=None):
    bsz, seq, d = x.shape
    assert kv_row0 % rows == 0
    kv_blk0 = kv_row0 // rows
    n = w.shape[1]
    use_rope = rope_tabs is not None
    in_specs = [
        pl.BlockSpec((1, rows, d), lambda b, t: (b, t, 0)),
        _row_spec(d), _row_spec(d),
        _resident((d, n)),
        _resident((1, GM_WIDTH)), _resident((1, GM_WIDTH)),
        _resident((GM_GROUPS, CHUNK, CHUNK)),
        _resident((CHUNK, GM_WIDTH)),
        _resident((1, HEAD_DIM)), _resident((1, HEAD_DIM)),
    ]
    args = [x, sh, sc, w, lng, lnb, ws, bias2d, qg, kg]
    if use_rope:
        in_specs += [pl.BlockSpec((rows, HEAD_DIM), lambda b, t: (t, 0))] * 2
        args += list(rope_tabs)
    aliases = {}
    if kv_into is not None:
        aliases = {len(args): 2, len(args) + 1: 3}
        in_specs += [pl.BlockSpec(memory_space=pl.ANY)] * 2
        args += list(kv_into)
    out_shape = [
        jax.ShapeDtypeStruct((bsz, seq, GM_WIDTH), BF16),
        jax.ShapeDtypeStruct((bsz, seq, ATTN_WIDTH), BF16),
        jax.ShapeDtypeStruct((bsz, kv_len, KV_W), BF16),
        jax.ShapeDtypeStruct((bsz, kv_len, KV_W), BF16),
    ]
    out_specs = [
        pl.BlockSpec((1, rows, GM_WIDTH), lambda b, t: (b, t, 0)),
        pl.BlockSpec((1, rows, ATTN_WIDTH), lambda b, t: (b, t, 0)),
        pl.BlockSpec((1, rows, KV_W), lambda b, t: (b, t + kv_blk0, 0)),
        pl.BlockSpec((1, rows, KV_W), lambda b, t: (b, t + kv_blk0, 0)),
    ]
    est = d * n * 2 + 2 * rows * d * 4 + 6 * rows * n * 4
    return pl.pallas_call(
        functools.partial(_ab_in_kernel, use_rope=use_rope, rows=rows),
        out_shape=out_shape,
        grid=(bsz, seq // rows),
        in_specs=in_specs,
        out_specs=out_specs,
        input_output_aliases=aliases,
        compiler_params=pltpu.CompilerParams(
            dimension_semantics=("arbitrary", "arbitrary"),
            vmem_limit_bytes=_vmem_limit(est)),
        name="ab_in_rope" if use_rope else "ab_in_ctx",
    )(*args)


def _attn_kernel(q_ref, k_ref, v_ref, o_ref, vt_ref, *, rows, blk):
    @pl.when(pl.program_id(2) == 0)
    def _():
        vt_ref[...] = v_ref[0].astype(F32).T

    group = N_Q_HEADS // N_KV_HEADS
    for g in range(k_ref.shape[2] // HEAD_DIM):
        k = k_ref[0, :, g * HEAD_DIM:(g + 1) * HEAD_DIM]
        vt = vt_ref[g * HEAD_DIM:(g + 1) * HEAD_DIM, :].astype(BF16)
        for h in range(group):
            l_ = slice((g * group + h) * HEAD_DIM, (g * group + h + 1) * HEAD_DIM)
            for r0 in range(0, rows, blk):
                st = _dot_nt(k, q_ref[0, r0:r0 + blk, l_])
                m = jnp.max(st, axis=0, keepdims=True)
                p = jnp.exp2(st - m)
                l = jnp.sum(p, axis=0, keepdims=True)
                ot = _dot(vt, p.astype(BF16)) / l
                o_ref[0, r0:r0 + blk, l_] = ot.T.astype(BF16)


def _attention(q, k, v, k_row0, sk, rows, blk=2048, kv_heads_per_step=1):
    bsz, sq, _ = q.shape
    assert k_row0 % sk == 0
    kb = k_row0 // sk
    gw = (N_Q_HEADS // N_KV_HEADS) * HEAD_DIM * kv_heads_per_step
    kw = HEAD_DIM * kv_heads_per_step
    est = 4 * blk * sk * 12 + 4 * sk * kw * 2 + 4 * rows * gw * 2 + sk * kw * 4
    return pl.pallas_call(
        functools.partial(_attn_kernel, rows=rows, blk=min(blk, rows)),
        out_shape=jax.ShapeDtypeStruct((bsz, sq, ATTN_WIDTH), BF16),
        grid=(bsz, N_KV_HEADS // kv_heads_per_step, sq // rows),
        in_specs=[
            pl.BlockSpec((1, rows, gw), lambda b, h, t: (b, t, h)),
            pl.BlockSpec((1, sk, kw), lambda b, h, t: (b, kb, h)),
            pl.BlockSpec((1, sk, kw), lambda b, h, t: (b, kb, h)),
        ],
        out_specs=pl.BlockSpec((1, rows, gw), lambda b, h, t: (b, t, h)),
        scratch_shapes=[pltpu.VMEM((kw, sk), F32)],
        compiler_params=pltpu.CompilerParams(
            dimension_semantics=("arbitrary", "arbitrary", "arbitrary"),
            vmem_limit_bytes=_vmem_limit(est)),
        name="attention",
    )(q, k, v)


def _out_ln_kernel(a1_ref, a2_ref, w_ref, x_ref, g_ref, lng_ref, lnb_ref, o_ref):
    half = a1_ref.shape[2]
    y = _dot(a1_ref[0], w_ref[0:half, :]) + _dot(a2_ref[0], w_ref[half:2 * half, :])
    o_ref[0] = _deepnorm(x_ref[0], g_ref[0], y, lng_ref[...], lnb_ref[...])


def _out_ln(a1, a2, w, x, gate, lng, lnb, rows):
    bsz, seq, d = x.shape
    half = a1.shape[2]
    est = 2 * half * d * 2 + 4 * rows * d * 4 + 4 * rows * half * 2 + 2 * rows * d * 4
    return pl.pallas_call(
        _out_ln_kernel,
        out_shape=jax.ShapeDtypeStruct((bsz, seq, d), F32),
        grid=(bsz, seq // rows),
        in_specs=[
            pl.BlockSpec((1, rows, half), lambda b, t: (b, t, 0)),
            pl.BlockSpec((1, rows, half), lambda b, t: (b, t, 0)),
            _resident((2 * half, d)),
            pl.BlockSpec((1, rows, d), lambda b, t: (b, t, 0)),
            _row_spec(d),
            _resident((1, d)), _resident((1, d)),
        ],
        out_specs=pl.BlockSpec((1, rows, d), lambda b, t: (b, t, 0)),
        compiler_params=pltpu.CompilerParams(
            dimension_semantics=("arbitrary", "arbitrary"),
            vmem_limit_bytes=_vmem_limit(est)),
        name="out_ln",
    )(a1, a2, w, x, gate, lng, lnb)


def _ffn_kernel(x_ref, xp_ref, xn_ref, sh_ref, sc_ref, g_ref, wu_ref, cw_ref, cb_ref, wd_ref, lng_ref, lnb_ref,
                o_ref, hm_ref, u_ref, gt_ref, *, rows, seg):
    _fill_halo_tile(hm_ref, x_ref, xp_ref, xn_ref, sh_ref[0], sc_ref[0], rows)
    n_chunks = wd_ref.shape[0]

    def cols(ref, c):
        off = c * FF_CHUNK
        return ref[:, pl.ds(off if isinstance(off, int) else pl.multiple_of(off, FF_CHUNK), FF_CHUNK)]
    tiles = rows // V7X_SUBLANES
    sub = lax.broadcasted_iota(jnp.int32, (tiles, V7X_SUBLANES, FF_CHUNK), 1)
    if seg < rows:
        seg_tiles = seg // V7X_SUBLANES
        til = lax.broadcasted_iota(jnp.int32, (tiles, V7X_SUBLANES, FF_CHUNK), 0) % seg_tiles
        seq_first = (sub == 0) & (til == 0)
        seq_last = (sub == V7X_SUBLANES - 1) & (til == seg_tiles - 1)

    def up_proj(c, slot):
        hm = hm_ref[...]
        u_ref[slot, 0] = _dot(hm, cols(wu_ref, c)).reshape(tiles + 2, V7X_SUBLANES, FF_CHUNK)
        u_ref[slot, 1] = _dot(hm, cols(wu_ref, c + n_chunks)).reshape(tiles + 2, V7X_SUBLANES, FF_CHUNK)

    def conv_gate(c, slot):
        def conv(half):
            u = u_ref[slot, half]
            dn = pltpu.roll(u, 1, 1)
            up = pltpu.roll(u, V7X_SUBLANES - 1, 1)
            prev = jnp.where(sub == 0, dn[0:tiles], dn[1:tiles + 1])
            nxt = jnp.where(sub == V7X_SUBLANES - 1, up[2:tiles + 2], up[1:tiles + 1])
            if seg < rows:
                prev = jnp.where(seq_first, 0.0, prev)
                nxt = jnp.where(seq_last, 0.0, nxt)
            w = cols(cw_ref, c + half * n_chunks)
            return cols(cb_ref, c + half * n_chunks) + w[0:1] * prev + w[1:2] * u[1:tiles + 1] + w[2:3] * nxt

        a1 = conv(0)
        a2 = conv(1)
        gt_ref[c] = (a1 * (a2 + a2 * jnp.tanh(a2))).reshape(rows, FF_CHUNK)

    up_proj(0, 0)

    def pair(k, carry):
        c = 2 * k
        up_proj(c + 1, 1)
        conv_gate(c, 0)
        up_proj(c + 2, 0)
        conv_gate(c + 1, 1)
        return carry

    lax.fori_loop(0, (n_chunks - 1) // 2, pair, 0)
    conv_gate(n_chunks - 1, 0)

    f = _dot(gt_ref[0].astype(BF16), wd_ref[0])
    for c in range(1, n_chunks):
        f = f + _dot(gt_ref[c].astype(BF16), wd_ref[c])
    o_ref[0] = _deepnorm(x_ref[0], g_ref[0], f, lng_ref[...], lnb_ref[...])


def _ffn(x, sh, sc, gate, wu, cw, cb, wd, layer, lng, lnb, rows, seg=None):
    bsz, seq, d = x.shape
    seg = seq if seg is None else seg
    assert seg == seq or (rows == seq and rows % seg == 0 and seg % V7X_SUBLANES == 0)
    nc = wd.shape[1]
    dff = nc * FF_CHUNK
    assert wu.shape[2] == 2 * dff and nc % 2 == 1, "the pipeline peels one chunk and pairs the rest"

    def of_layer(*shape):
        nd = len(shape)
        return pl.BlockSpec((None,) + shape, lambda b, t: (layer,) + (0,) * nd, pipeline_mode=pl.Buffered(1))

    est = (3 * nc * d * FF_CHUNK * 2 + 4 * rows * d * 4 + (rows + 2 * HALO) * d * 2
           + nc * rows * FF_CHUNK * 4 + 8 * (rows + 2 * HALO) * FF_CHUNK * 4 + 2 * rows * d * 4)
    return pl.pallas_call(
        functools.partial(_ffn_kernel, rows=rows, seg=min(seg, rows)),
        out_shape=jax.ShapeDtypeStruct((bsz, seq, d), F32),
        grid=(bsz, seq // rows),
        in_specs=_halo_specs(rows, seq, d) + [
            _row_spec(d), _row_spec(d), _row_spec(d),
            of_layer(d, 2 * dff), of_layer(CONV_W, 2 * dff), of_layer(1, 2 * dff),
            of_layer(nc, FF_CHUNK, d),
            _resident((1, d)), _resident((1, d)),
        ],
        out_specs=pl.BlockSpec((1, rows, d), lambda b, t: (b, t, 0)),
        scratch_shapes=[
            pltpu.VMEM((rows + 2 * HALO, d), BF16),
            pltpu.VMEM((2, 2, rows // HALO + 2, HALO, FF_CHUNK), F32),
            pltpu.VMEM((nc, rows, FF_CHUNK), F32),
        ],
        compiler_params=pltpu.CompilerParams(
            dimension_semantics=("arbitrary", "arbitrary"),
            vmem_limit_bytes=_vmem_limit(est)),
        name="conv_ffn",
    )(x, x, x, sh, sc, gate, wu, cw, cb, wd, lng, lnb)


def _cd_in_kernel(*refs, with_fz, rows, seg):
    x_ref, xp_ref, xn_ref, sh_ref, sc_ref = refs[:5]
    if with_fz:
        wfz_ref, wx_ref, wdt_ref, cw_ref, cb_ref, dtb_ref = refs[5:11]
        f_ref, z_ref, xs_ref, b_ref, c_ref, dt_ref, hm_ref = refs[11:]
    else:
        wx_ref, wdt_ref, cw_ref, cb_ref, dtb_ref = refs[5:10]
        xs_ref, b_ref, c_ref, dt_ref, hm_ref = refs[10:]
    _fill_halo_tile(hm_ref, x_ref, xp_ref, xn_ref, sh_ref[0], sc_ref[0], rows)
    hc = hm_ref[HALO:HALO + rows, :]
    if with_fz:
        pfz = _dot(hc, wfz_ref[...])
        f_ref[0] = pfz[:, 0:FN_WIDTH].astype(BF16)
        z_ref[0] = pfz[:, FN_WIDTH:SSM_OFF]
    a = _conv3(_dot(hm_ref[...], wx_ref[...]), cw_ref[...], cb_ref[...], rows, seg)
    s = a + a * jnp.tanh(a)
    xs_ref[0] = s[:, 0:SSD_WIDTH]
    b_ref[0] = s[:, SSD_WIDTH:SSD_WIDTH + SSD_GN].astype(BF16)
    c_ref[0] = s[:, SSD_WIDTH + SSD_GN:SSD_CONV_DIM].astype(BF16)
    dt_ref[0] = _softplus(_dot(hc, wdt_ref[...]) + dtb_ref[...])


def _cd_in(x, sh, sc, wfz, wx, wdt, cw, cb, dtb, rows, seg=None):
    bsz, seq, d = x.shape
    seg = seq if seg is None else seg
    assert seg == seq or (rows == seq and rows % seg == 0 and seg % V7X_SUBLANES == 0)
    with_fz = wfz is not None
    in_specs = _halo_specs(rows, seq, d) + [_row_spec(d), _row_spec(d)]
    args = [x, x, x, sh, sc]
    out_shape, out_specs = [], []
    if with_fz:
        in_specs.append(_resident((d, SSM_OFF)))
        args.append(wfz)
        out_shape += [jax.ShapeDtypeStruct((bsz, seq, FN_WIDTH), BF16),
                      jax.ShapeDtypeStruct((bsz, seq, SSD_WIDTH), F32)]
        out_specs += [pl.BlockSpec((1, rows, FN_WIDTH), lambda b, t: (b, t, 0)),
                      pl.BlockSpec((1, rows, SSD_WIDTH), lambda b, t: (b, t, 0))]
    in_specs += [_resident((d, SSD_CONV_DIM)), _resident((d, V7X_LANES)),
                 _resident((CONV_W, SSD_CONV_DIM)), _resident((1, SSD_CONV_DIM)),
                 _resident((1, V7X_LANES))]
    args += [wx, wdt, cw, cb, dtb]
    out_shape += [jax.ShapeDtypeStruct((bsz, seq, SSD_WIDTH), F32),
                  jax.ShapeDtypeStruct((bsz, seq, SSD_GN), BF16),
                  jax.ShapeDtypeStruct((bsz, seq, SSD_GN), BF16),
                  jax.ShapeDtypeStruct((bsz, seq, V7X_LANES), F32)]
    out_specs += [pl.BlockSpec((1, rows, SSD_WIDTH), lambda b, t: (b, t, 0)),
                  pl.BlockSpec((1, rows, SSD_GN), lambda b, t: (b, t, 0)),
                  pl.BlockSpec((1, rows, SSD_GN), lambda b, t: (b, t, 0)),
                  pl.BlockSpec((1, rows, V7X_LANES), lambda b, t: (b, t, 0))]
    est = d * 2200 * 2 + 2 * rows * d * 4 + 8 * (rows + 2 * HALO) * SSD_CONV_DIM * 4
    return pl.pallas_call(
        functools.partial(_cd_in_kernel, with_fz=with_fz, rows=rows, seg=min(seg, rows)),
        out_shape=out_shape,
        grid=(bsz, seq // rows),
        in_specs=in_specs,
        out_specs=out_specs,
        scratch_shapes=[pltpu.VMEM((rows + 2 * HALO, d), BF16)],
        compiler_params=pltpu.CompilerParams(
            dimension_semantics=("arbitrary", "arbitrary"),
            vmem_limit_bytes=_vmem_limit(est)),
        name="cd_in" if with_fz else "cd_in_ctx",
    )(*args)


def _cumsum_rows(z, reverse):
    n = z.shape[0]
    idx = lax.broadcasted_iota(jnp.int32, z.shape, 0)
    k = 1
    while k < n:
        if reverse:
            z = z + jnp.where(idx < n - k, pltpu.roll(z, n - k, 0), 0.0)
        else:
            z = z + jnp.where(idx >= k, pltpu.roll(z, k, 0), 0.0)
        k *= 2
    return z


def _lane_expand(row, lane0, width):
    lane = lax.broadcasted_iota(jnp.int32, (1, SSD_HPG * width), 1)
    out = jnp.zeros((1, SSD_HPG * width), F32)
    for j in range(SSD_HPG):
        val = jnp.broadcast_to(row[:, lane0 + j:lane0 + j + 1], (1, SSD_HPG * width))
        out = jnp.where(lane // width == j, val, out)
    return out


def _ssd_chunk(x, bm, cm, dt, a_row, h_ref, lane0, reverse, need_y):
    q = x.shape[0]
    gw = SSD_HPG * SSD_HEADDIM
    acum = _cumsum_rows(dt * a_row, reverse)
    total = acum[0:1, :] if reverse else acum[q - 1:q, :]
    acum_t = acum.T
    dt_t = dt.T
    total_t = jnp.broadcast_to(acum_t[:, 0:1] if reverse else acum_t[:, q - 1:q], acum_t.shape)
    wgt_t = dt_t * jnp.exp2(total_t - acum_t)
    lane_g = lax.broadcasted_iota(jnp.int32, (q, gw), 1) // SSD_HEADDIM
    lane_gh = lax.broadcasted_iota(jnp.int32, (SSD_STATE, gw), 1) // SSD_HEADDIM
    if need_y:
        srow_t = acum_t - jnp.log2(dt_t)
        ti = lax.broadcasted_iota(jnp.int32, (q, q), 0)
        si = lax.broadcasted_iota(jnp.int32, (q, q), 1)
        mask = (si >= ti) if reverse else (si <= ti)
    ys = []
    for g in range(SSD_GROUPS):
        bg = bm[:, g * SSD_STATE:(g + 1) * SSD_STATE]
        xgb = x[:, g * gw:(g + 1) * gw].astype(BF16)
        x_heads = [jnp.where(lane_g == j, xgb, jnp.zeros_like(xgb)) for j in range(SSD_HPG)]
        hg = h_ref[g]
        if need_y:
            cg = cm[:, g * SSD_STATE:(g + 1) * SSD_STATE]
            cb = _dot_nt(cg, bg)
            cgf = cg.astype(F32)
            hgb = hg.astype(BF16)
            lhs, rhs = [], []
            for j in range(SSD_HPG):
                jl = lane0 + g * SSD_HPG + j
                col = jnp.broadcast_to(acum[:, jl:jl + 1], (q, q))
                row = jnp.broadcast_to(srow_t[jl:jl + 1, :], (q, q))
                lhs.append((cb * jnp.exp2(jnp.where(mask, col - row, -jnp.inf))).astype(BF16))
                lhs.append((cgf * jnp.exp2(col[:, 0:SSD_STATE])).astype(BF16))
                rhs.append(x_heads[j])
                rhs.append(jnp.where(lane_gh == j, hgb, jnp.zeros_like(hgb)))
            ys.append(_dot(jnp.concatenate(lhs, axis=1), jnp.concatenate(rhs, axis=0)))
        bgt = bg.astype(F32).T
        wb = [(bgt * jnp.broadcast_to(wgt_t[lane0 + g * SSD_HPG + j:lane0 + g * SSD_HPG + j + 1, :],
                                      (SSD_STATE, q))).astype(BF16) for j in range(SSD_HPG)]
        st = _dot(jnp.concatenate(wb, axis=1), jnp.concatenate(x_heads, axis=0))
        h_ref[g] = hg * jnp.exp2(_lane_expand(total, lane0 + g * SSD_HPG, SSD_HEADDIM)) + st
    if need_y:
        return jnp.concatenate(ys, axis=1)
    return None


def _ssd_kernel(xc_ref, bc_ref, dtc_ref, xl_ref, bl_ref, cl_ref, dtl_ref, z_ref, alog_ref,
                dskip_ref, ng_ref, o_ref, y_ref, h_ref):
    a_row = -jnp.exp(alog_ref[...]) * math.log2(math.e)
    nc_ctx = xc_ref.shape[1] // SSD_CHUNK
    nc_lat = xl_ref.shape[1] // SSD_CHUNK

    def rows_of(c):
        return pl.ds(pl.multiple_of(c * SSD_CHUNK, SSD_CHUNK), SSD_CHUNK)

    def ctx_step(c, direction):
        r = rows_of(c)
        _ssd_chunk(xc_ref[0, r, :], bc_ref[0, r, :], None, dtc_ref[0, r, :], a_row,
                   h_ref.at[direction], direction * SSD_HEADS, direction == 1, False)

    def lat_step(c, direction):
        r = rows_of(c)
        y_ref[direction, r, :] = _ssd_chunk(
            xl_ref[0, r, :], bl_ref[0, r, :], cl_ref[0, r, :], dtl_ref[0, r, :], a_row,
            h_ref.at[direction], direction * SSD_HEADS, direction == 1, True)

    h_ref[...] = jnp.zeros_like(h_ref)

    def ctx_pair(i, carry):
        ctx_step(i, 0)
        ctx_step(nc_ctx - 1 - i, 1)
        return carry

    lax.fori_loop(0, nc_ctx, ctx_pair, 0)

    def lat_pair(i, carry):
        lat_step(i, 0)
        lat_step(nc_lat - 1 - i, 1)
        return carry

    lax.fori_loop(0, nc_lat, lat_pair, 0)

    def finish(c, carry):
        r = rows_of(c)
        zz = z_ref[0, r, :]
        yt = (y_ref[0, r, :] + y_ref[1, r, :] + xl_ref[0, r, :] * dskip_ref[...]) * _silu(zz)
        o_ref[0, r, :] = _rms_norm(yt, ng_ref[...]).astype(BF16)
        return carry

    lax.fori_loop(0, nc_lat, finish, 0)


def _ssd(xs_c, b_c, dt_c, xs_l, b_l, c_l, dt_l, z, alog_row, dskip_row, ng_row):
    bsz, sl, w = xs_l.shape
    sc = xs_c.shape[1]
    assert sl % SSD_CHUNK == 0 and sc % SSD_CHUNK == 0
    gw = SSD_HPG * SSD_HEADDIM

    def full(s, n):
        return pl.BlockSpec((1, s, n), lambda b: (b, 0, 0))

    est = (2 * (sl + sc) * (w * 4 + 2 * SSD_GN * 2 + V7X_LANES * 4) + 2 * sl * w * 4
           + 2 * sl * w * 2 + 2 * sl * w * 4)
    return pl.pallas_call(
        _ssd_kernel,
        out_shape=jax.ShapeDtypeStruct((bsz, sl, w), BF16),
        grid=(bsz,),
        in_specs=[
            full(sc, w), full(sc, SSD_GN), full(sc, V7X_LANES),
            full(sl, w), full(sl, SSD_GN), full(sl, SSD_GN), full(sl, V7X_LANES),
            full(sl, w),
            _resident((1, V7X_LANES)), _resident((1, w)), _resident((1, w)),
        ],
        out_specs=full(sl, w),
        scratch_shapes=[
            pltpu.VMEM((2, sl, w), F32),
            pltpu.VMEM((2, SSD_GROUPS, SSD_STATE, gw), F32),
        ],
        compiler_params=pltpu.CompilerParams(
            dimension_semantics=("arbitrary",),
            vmem_limit_bytes=_vmem_limit(est)),
        name="ssd_scan",
    )(xs_c, b_c, dt_c, xs_l, b_l, c_l, dt_l, z, alog_row, dskip_row, ng_row)


def _fourier_kernel(f_ref, cc_ref, sc_ref, w2_ref, o_ref, y_ref, *, rows):
    seq = f_ref.shape[1]
    for g in range(FN_GROUPS):
        l = slice(g * FN_GDIM, (g + 1) * FN_GDIM)
        fg = f_ref[0, :, l]
        y_ref[0:seq, l] = _dot(fg, cc_ref[...]).astype(BF16)
        y_ref[seq:2 * seq, l] = _dot(fg, sc_ref[...]).astype(BF16)
    norm = 1.0 / math.sqrt(seq * FN_GDIM)

    def tile(i, carry):
        r = pl.ds(pl.multiple_of(i * rows, rows), rows)
        o_ref[0, r, :] = (_dot(w2_ref[r, :], y_ref[...]) * norm).astype(BF16)
        return carry

    lax.fori_loop(0, seq // rows, tile, 0)


def _fourier(f, cc, sc, w2, rows):
    bsz, seq, w = f.shape
    est = seq * 2 * seq * 2 + 4 * seq * w * 2 + 2 * seq * w * 2 + 4 * rows * 2 * seq * 2
    return pl.pallas_call(
        functools.partial(_fourier_kernel, rows=rows),
        out_shape=jax.ShapeDtypeStruct((bsz, seq, w), BF16),
        grid=(bsz,),
        in_specs=[
            pl.BlockSpec((1, seq, w), lambda b: (b, 0, 0)),
            _resident((FN_GDIM, FN_GDIM)), _resident((FN_GDIM, FN_GDIM)),
            _resident((seq, 2 * seq)),
        ],
        out_specs=pl.BlockSpec((1, seq, w), lambda b: (b, 0, 0)),
        scratch_shapes=[pltpu.VMEM((2 * seq, w), BF16)],
        compiler_params=pltpu.CompilerParams(
            dimension_semantics=("arbitrary",),
            vmem_limit_bytes=_vmem_limit(est)),
        name="fourier_mix",
    )(f, cc, sc, w2)


def _rope_tables(seq):
    rows = seq // GRID_W
    row = jnp.repeat(jnp.arange(rows, dtype=F32), GRID_W)
    col = jnp.broadcast_to(jnp.arange(GRID_W, dtype=F32), (rows, GRID_W)).reshape(-1)
    n_freq = HEAD_DIM // 4
    inv = ROPE_THETA ** (-jnp.arange(n_freq, dtype=F32) / n_freq)
    ang = jnp.stack([row, col], axis=-1)[:, :, None] * inv
    cos, sin = jnp.cos(ang), jnp.sin(ang)
    cos_t = jnp.concatenate([cos[:, 0], cos[:, 1], cos[:, 0], cos[:, 1]], axis=-1)
    sin_t = jnp.concatenate([-sin[:, 0], -sin[:, 1], sin[:, 0], sin[:, 1]], axis=-1)
    return cos_t, sin_t


def _rope_head_order(m):
    lead = m.shape[:-1]
    nh = m.shape[-1] // HEAD_DIM
    m = m.reshape(lead + (nh, 2, 2, HEAD_DIM // 4))
    return jnp.swapaxes(m, -3, -2).reshape(lead + (nh * HEAD_DIM,))


def _dft_tables(seq):
    def tw(n):
        i = np.arange(n, dtype=np.int64)
        ang = ((i[:, None] * i[None, :]) % n).astype(np.float64) * (2.0 * math.pi / n)
        return np.cos(ang), np.sin(ang)

    cc, sc = tw(FN_GDIM)
    cl, sl = tw(seq)
    w2 = np.concatenate([cl, -sl], axis=1).astype(np.float32)
    return (jnp.asarray(cc.astype(np.float32), BF16), jnp.asarray(sc.astype(np.float32), BF16),
            jnp.asarray(w2, BF16))


def _ffn_params(w_up, conv_w, conv_b, w_down):
    depth, d, _ = w_up.shape
    half_gate = jnp.concatenate([jnp.ones((D_FF,), F32), jnp.full((D_FF,), 0.5, F32)])
    return (w_up.astype(BF16), conv_w * half_gate, (conv_b * half_gate).reshape(depth, 1, 2 * D_FF),
            w_down.astype(BF16).reshape(depth, D_FF // FF_CHUNK, FF_CHUNK, d))


def kernel(x, c, ctx, c_ctx, mod_w, mod_b, ln1_g, ln1_b, ln2_g, ln2_b, ffn_up, ffn_conv_w, ffn_conv_b,
           ffn_down, ab_w_in, ab_w_out, gm_ln_g, gm_ln_b, gm_ws, gm_bs, q_norm_g, k_norm_g, cd_w_in,
           cd_w_out, ssd_conv_w, ssd_conv_b, ssd_dt_bias, ssd_a_log, ssd_d, ssd_norm_g):
    bsz, seq, d = x.shape
    ctx_len = ctx.shape[1]
    lat_rows = min(1024, seq)
    out_rows = min(2048, seq)
    dft_rows = min(2048, seq)
    ctx_rows = min(256, ctx_len)

    pad = (-(bsz + 1)) % V7X_SUBLANES
    cc = jnp.concatenate([c, c_ctx[None], jnp.zeros((pad, d), F32)], axis=0)
    mod = _modulation(cc, mod_w, mod_b)
    rope_tabs = _rope_tables(seq)

    def row(v):
        return v.reshape(1, -1)

    ffn_p = _ffn_params(ffn_up, ffn_conv_w, ffn_conv_b, ffn_down)
    xl, xc = x, ctx
    for i in range(DEPTH):
        last = i == DEPTH - 1
        j = i // 2
        ml = mod[i, :bsz].reshape(bsz, 6, 1, d)
        mc = jnp.broadcast_to(mod[i, bsz].reshape(1, 6, 1, d), (bsz, 6, 1, d))
        sh1, sc1, g1, sh2, sc2, g2 = (ml[:, k] for k in range(6))
        csh1, csc1, cg1, csh2, csc2, cg2 = (mc[:, k] for k in range(6))
        ln1 = (row(ln1_g[i]), row(ln1_b[i]))
        ln2 = (row(ln2_g[i]), row(ln2_b[i]))

        if i % 2 == 0:
            w_in = ab_w_in[j].astype(BF16)
            qk_lo, qk_hi = 2 * GM_WIDTH, KV_OFF + KV_W
            q_scale = HEAD_DIM ** -0.5 * math.log2(math.e)
            w_in = jnp.concatenate([0.5 * w_in[:, :qk_lo], _rope_head_order(w_in[:, qk_lo:qk_hi]),
                                    w_in[:, qk_hi:]], axis=1)
            w_out = ab_w_out[j].astype(BF16)
            gm = (row(gm_ln_g[j]), row(gm_ln_b[j]), gm_ws[j].astype(BF16),
                  jnp.repeat(gm_bs[j].T, GM_WIDTH // GM_GROUPS, axis=1),
                  row(_rope_head_order(q_norm_g[j]) * q_scale), row(_rope_head_order(k_norm_g[j])))
            kv_len = seq + ctx_len
            a_c, q_c, k_all, v_all = _ab_in(xc, csh1, csc1, w_in, *gm, None, ctx_rows, kv_len, seq)
            a_l, q_l, k_all, v_all = _ab_in(xl, sh1, sc1, w_in, *gm, rope_tabs, out_rows, kv_len, 0,
                                            kv_into=(k_all, v_all))
            mix_l = (a_l, _attention(q_l, k_all, v_all, 0, kv_len, out_rows))
            if not last:
                mix_c = (a_c, _attention(q_c, k_all, v_all, seq, ctx_len, ctx_rows,
                                             kv_heads_per_step=N_KV_HEADS))
        else:
            w_in = cd_w_in[j].astype(BF16)
            w_out = cd_w_out[j].astype(BF16)
            wfz = w_in[:, :SSM_OFF]
            wx = w_in[:, SSM_OFF:SSM_OFF + SSD_CONV_DIM]
            ndt = 2 * SSD_HEADS
            wdt = jnp.pad(w_in[:, SSM_OFF + SSD_CONV_DIM:], ((0, 0), (0, V7X_LANES - ndt)))
            dtb = jnp.pad(ssd_dt_bias[j].reshape(1, ndt), ((0, 0), (0, V7X_LANES - ndt)))
            alog = jnp.pad(ssd_a_log[j].reshape(1, ndt), ((0, 0), (0, V7X_LANES - ndt)))
            cw, cb = 0.5 * ssd_conv_w[j], 0.5 * row(ssd_conv_b[j])
            dskip = row(jnp.repeat(ssd_d[j], SSD_HEADDIM))
            f_l, z_l, xs_l, b_l, c_l, dt_l = _cd_in(xl, sh1, sc1, wfz, wx, wdt, cw, cb, dtb, lat_rows)
            if last:
                n_cd = math.gcd(bsz, max(1, lat_rows // ctx_len))
                xs_c, b_c, _, dt_c = (
                    v.reshape(bsz, ctx_len, v.shape[-1]) for v in
                    _cd_in(xc.reshape(bsz // n_cd, n_cd * ctx_len, d), csh1[:bsz // n_cd], csc1[:bsz // n_cd],
                           None, wx, wdt, cw, cb, dtb, n_cd * ctx_len, seg=ctx_len))
            else:
                raise NotImplementedError("odd layers with a context output are not part of this block")
            s_l = _ssd(xs_c, b_c, dt_c, xs_l, b_l, c_l, dt_l, z_l, alog, dskip, row(ssd_norm_g[j]))
            mix_l = (_fourier(f_l, *_dft_tables(seq), dft_rows), s_l)

        xl = _out_ln(*mix_l, w_out, xl, g1, *ln1, out_rows)
        xl = _ffn(xl, sh2, sc2, g2, *ffn_p, i, *ln2, lat_rows)
        if not last:
            def stacked(v, n):
                return v.reshape(bsz // n, n * ctx_len, v.shape[-1])
            n_out = math.gcd(bsz, max(1, out_rows // ctx_len))
            n_ffn = math.gcd(bsz, max(1, lat_rows // ctx_len))
            xc = _out_ln(*(stacked(m, n_out) for m in mix_c), w_out, stacked(xc, n_out), cg1[:bsz // n_out],
                         *ln1, n_out * ctx_len)
            xc = _ffn(stacked(xc.reshape(bsz, ctx_len, d), n_ffn), csh2[:bsz // n_ffn], csc2[:bsz // n_ffn],
                      cg2[:bsz // n_ffn], *ffn_p, i, *ln2, n_ffn * ctx_len, seg=ctx_len)
            xc = xc.reshape(bsz, ctx_len, d)
    return xl
```

```python
import functools
import math

import jax
import jax.numpy as jnp
import numpy as np
from jax import lax
from jax.experimental import pallas as pl
from jax.experimental.pallas import tpu as pltpu

D_MODEL = 1024
DEPTH = 2
GRID_W = 64
CHUNK = 128
GM_GROUPS = 4
GM_WIDTH = 512
HEAD_DIM = 128
N_Q_HEADS = 4
N_KV_HEADS = 2
ATTN_WIDTH = 512
KV_W = 256
ROPE_THETA = 10000.0
FN_GROUPS = 4
FN_GDIM = 128
FN_WIDTH = 512
SSD_HEADDIM = 64
SSD_HEADS = 8
SSD_GROUPS = 2
SSD_HPG = 4
SSD_STATE = 128
SSD_WIDTH = 512
SSD_GN = 256
SSD_CONV_DIM = 1024
D_FF = 2816
CONV_W = 3
KV_OFF = 2 * GM_WIDTH + ATTN_WIDTH
SSM_OFF = FN_WIDTH + SSD_WIDTH
DN_ALPHA = (2 * DEPTH) ** 0.25
EPS = 1e-6

V7X_LANES = 128
V7X_SUBLANES = 8
V7X_VMEM_BYTES = 64 * 1024 * 1024
VMEM_CHIP_HEADROOM = 3 << 20
VMEM_TEMP_ALLOWANCE = 8 << 20
MOD_COLS = 2048
AB_SUB = 256
FF_CHUNK = 256
SSD_CHUNK = 256

BF16 = jnp.bfloat16
F32 = jnp.float32
HALO = V7X_SUBLANES


def _vmem_limit(nbytes):
    return int(min(nbytes * 3 // 2 + VMEM_TEMP_ALLOWANCE, V7X_VMEM_BYTES - VMEM_CHIP_HEADROOM))


def _resident(shape):
    nd = len(shape)
    return pl.BlockSpec(shape, lambda *_: (0,) * nd, pipeline_mode=pl.Buffered(1))


def _dot(a, b):
    return jnp.dot(a, b, preferred_element_type=F32)


def _dot_nt(a, b):
    return lax.dot_general(a, b, (((1,), (1,)), ((), ())), preferred_element_type=F32)


def _layer_norm(z, g, b, eps=EPS):
    mu = jnp.mean(z, axis=-1, keepdims=True)
    zc = z - mu
    var = jnp.mean(zc * zc, axis=-1, keepdims=True)
    return zc * lax.rsqrt(var + eps) * g + b


def _deepnorm(x, gate, y, g, b):
    return _layer_norm(x + (gate * (1.0 / DN_ALPHA)) * y, g, b, eps=EPS / DN_ALPHA ** 2)


def _rms_norm(z, g):
    return z * lax.rsqrt(jnp.mean(z * z, axis=-1, keepdims=True) + EPS) * g


def _softplus(z):
    return jnp.maximum(z, 0.0) + jnp.log1p(jnp.exp(-jnp.abs(z)))


def _gelu_tanh_of_half(h):
    c1 = math.sqrt(2.0 / math.pi)
    return h + h * jnp.tanh(h * (2.0 * c1 + (8.0 * c1 * 0.044715) * (h * h)))


def _silu(z):
    h = 0.5 * z
    return h + h * jnp.tanh(h)


def _conv3(u, w, b, rows, seg):
    tiles, c = rows // V7X_SUBLANES, u.shape[1]
    u3 = u.reshape(tiles + 2, V7X_SUBLANES, c)
    sub = lax.broadcasted_iota(jnp.int32, (tiles, V7X_SUBLANES, c), 1)
    dn = pltpu.roll(u3, 1, 1)
    up = pltpu.roll(u3, V7X_SUBLANES - 1, 1)
    prev = jnp.where(sub == 0, dn[0:tiles], dn[1:tiles + 1])
    nxt = jnp.where(sub == V7X_SUBLANES - 1, up[2:tiles + 2], up[1:tiles + 1])
    if seg < rows:
        seg_tiles = seg // V7X_SUBLANES
        til = lax.broadcasted_iota(jnp.int32, (tiles, V7X_SUBLANES, c), 0) % seg_tiles
        prev = jnp.where((sub == 0) & (til == 0), 0.0, prev)
        nxt = jnp.where((sub == V7X_SUBLANES - 1) & (til == seg_tiles - 1), 0.0, nxt)
    return (b + w[0:1] * prev + w[1:2] * u3[1:tiles + 1] + w[2:3] * nxt).reshape(rows, c)


def _fill_halo_tile(hm_ref, x_ref, xp_ref, xn_ref, sh, sc, rows):
    t = pl.program_id(1)
    nt = pl.num_programs(1)
    keep_prev = jnp.where(t > 0, 1.0, 0.0)
    keep_next = jnp.where(t < nt - 1, 1.0, 0.0)
    hm_ref[HALO:HALO + rows, :] = (x_ref[0] * (1.0 + sc) + sh).astype(BF16)
    hm_ref[0:HALO, :] = ((xp_ref[0] * (1.0 + sc) + sh) * keep_prev).astype(BF16)
    hm_ref[HALO + rows:2 * HALO + rows, :] = ((xn_ref[0] * (1.0 + sc) + sh) * keep_next).astype(BF16)


def _halo_specs(rows, seq, d):
    nb = rows // HALO
    last = seq // HALO - 1
    return [
        pl.BlockSpec((1, rows, d), lambda b, t: (b, t, 0)),
        pl.BlockSpec((1, HALO, d), lambda b, t: (b, jnp.maximum(t * nb - 1, 0), 0)),
        pl.BlockSpec((1, HALO, d), lambda b, t: (b, jnp.minimum((t + 1) * nb, last), 0)),
    ]


def _row_spec(d):
    return pl.BlockSpec((1, 1, d), lambda b, t: (b, 0, 0))


def _mod_kernel(c_ref, w_ref, b_ref, o_ref):
    c = c_ref[...]
    act = c * jax.nn.sigmoid(c)
    o_ref[0] = jnp.dot(act, w_ref[0], preferred_element_type=F32,
                       precision=lax.Precision.HIGHEST) + b_ref[0]


def _modulation(cc, mod_w, mod_b):
    rows, d = cc.shape
    depth, _, n = mod_w.shape
    tn = MOD_COLS
    return pl.pallas_call(
        _mod_kernel,
        out_shape=jax.ShapeDtypeStruct((depth, rows, n), F32),
        grid=(depth, n // tn),
        in_specs=[
            pl.BlockSpec((rows, d), lambda i, j: (0, 0)),
            pl.BlockSpec((1, d, tn), lambda i, j: (i, 0, j)),
            pl.BlockSpec((1, 1, tn), lambda i, j: (i, 0, j)),
        ],
        out_specs=pl.BlockSpec((1, rows, tn), lambda i, j: (i, 0, j)),
        compiler_params=pltpu.CompilerParams(
            dimension_semantics=("arbitrary", "arbitrary"),
            vmem_limit_bytes=_vmem_limit(2 * d * tn * 4)),
        name="modulation",
    )(cc, mod_w, mod_b.reshape(depth, 1, n))


def _ab_in_kernel(*refs, use_rope, rows):
    (x_ref, sh_ref, sc_ref, w_ref, lng_ref, lnb_ref, ws_ref, bias_ref, qg_ref, kg_ref) = refs[:10]
    if use_rope:
        cos_ref, sin_ref = refs[10:12]
    a_ref, q_ref, k_ref, v_ref = refs[-4:]

    gd = GM_WIDTH // GM_GROUPS
    slab_rows = min(AB_SUB, rows)

    def slab(i, carry):
        r0 = pl.multiple_of(i * slab_rows, slab_rows)
        rs = pl.ds(r0, slab_rows)
        h = (x_ref[0, rs, :] * (1.0 + sc_ref[0]) + sh_ref[0]).astype(BF16)

        u = _gelu_tanh_of_half(_dot(h, w_ref[:, 0:GM_WIDTH]))
        vn = _layer_norm(_gelu_tanh_of_half(_dot(h, w_ref[:, GM_WIDTH:2 * GM_WIDTH])),
                         lng_ref[...], lnb_ref[...])
        vb = vn.astype(BF16)
        for c in range(slab_rows // CHUNK):
            r = slice(c * CHUNK, (c + 1) * CHUNK)
            for g in range(GM_GROUPS):
                l = slice(g * gd, (g + 1) * gd)
                mixed = _dot(ws_ref[g], vb[r, l]) + bias_ref[:, l]
                a_ref[0, pl.ds(r0 + c * CHUNK, CHUNK), l] = (u[r, l] * mixed).astype(BF16)

        def rope(z):
            if not use_rope:
                return z
            return z * cos_ref[rs, :] + pltpu.roll(z, HEAD_DIM // 2, 1) * sin_ref[rs, :]

        pq = _dot(h, w_ref[:, 2 * GM_WIDTH:KV_OFF])
        for hh in range(N_Q_HEADS):
            l = slice(hh * HEAD_DIM, (hh + 1) * HEAD_DIM)
            q_ref[0, rs, l] = rope(_rms_norm(pq[:, l], qg_ref[...])).astype(BF16)
        pk = _dot(h, w_ref[:, KV_OFF:KV_OFF + KV_W])
        for hh in range(N_KV_HEADS):
            l = slice(hh * HEAD_DIM, (hh + 1) * HEAD_DIM)
            k_ref[0, rs, l] = rope(_rms_norm(pk[:, l], kg_ref[...])).astype(BF16)
        v_ref[0, rs, :] = _dot(h, w_ref[:, KV_OFF + KV_W:KV_OFF + 2 * KV_W]).astype(BF16)
        return carry

    lax.fori_loop(0, rows // slab_rows, slab, 0)


def _ab_in(x, sh, sc, w, lng, lnb, ws, bias2d, qg, kg, rope_tabs, rows, kv_len, kv_row0, kv_into=None):
    bsz, seq, d = x.shape
    assert kv_row0 % rows == 0
    kv_blk0 = kv_row0 // rows
    n = w.shape[1]
    use_rope = rope_tabs is not None
    in_specs = [
        pl.BlockSpec((1, rows, d), lambda b, t: (b, t, 0)),
        _row_spec(d), _row_spec(d),
        _resident((d, n)),
        _resident((1, GM_WIDTH)), _resident((1, GM_WIDTH)),
        _resident((GM_GROUPS, CHUNK, CHUNK)),
        _resident((CHUNK, GM_WIDTH)),
        _resident((1, HEAD_DIM)), _resident((1, HEAD_DIM)),
    ]
    args = [x, sh, sc, w, lng, lnb, ws, bias2d, qg, kg]
    if use_rope:
        in_specs += [pl.BlockSpec((rows, HEAD_DIM), lambda b, t: (t, 0))] * 2
        args += list(rope_tabs)
    aliases = {}
    if kv_into is not None:
        aliases = {len(args): 2, len(args) + 1: 3}
        in_specs += [pl.BlockSpec(memory_space=pl.ANY)] * 2
        args += list(kv_into)
    out_shape = [
        jax.ShapeDtypeStruct((bsz, seq, GM_WIDTH), BF16),
        jax.ShapeDtypeStruct((bsz, seq, ATTN_WIDTH), BF16),
        jax.ShapeDtypeStruct((bsz, kv_len, KV_W), BF16),
        jax.ShapeDtypeStruct((bsz, kv_len, KV_W), BF16),
    ]
    out_specs = [
        pl.BlockSpec((1, rows, GM_WIDTH), lambda b, t: (b, t, 0)),
        pl.BlockSpec((1, rows, ATTN_WIDTH), lambda b, t: (b, t, 0)),
        pl.BlockSpec((1, rows, KV_W), lambda b, t: (b, t + kv_blk0, 0)),
        pl.BlockSpec((1, rows, KV_W), lambda b, t: (b, t + kv_blk0, 0)),
    ]
    est = d * n * 2 + 2 * rows * d * 4 + 6 * rows * n * 4
    return pl.pallas_call(
        functools.partial(_ab_in_kernel, use_rope=use_rope, rows=rows),
        out_shape=out_shape,
        grid=(bsz, seq // rows),
        in_specs=in_specs,
        out_specs=out_specs,
        input_output_aliases=aliases,
        compiler_params=pltpu.CompilerParams(
            dimension_semantics=("arbitrary", "arbitrary"),
            vmem_limit_bytes=_vmem_limit(est)),
        name="ab_in_rope" if use_rope else "ab_in_ctx",
    )(*args)


def _attn_kernel(q_ref, k_ref, v_ref, o_ref, vt_ref, *, rows, blk):
    @pl.when(pl.program_id(2) == 0)
    def _():
        vt_ref[...] = v_ref[0].astype(F32).T

    group = N_Q_HEADS // N_KV_HEADS
    for g in range(k_ref.shape[2] // HEAD_DIM):
        k = k_ref[0, :, g * HEAD_DIM:(g + 1) * HEAD_DIM]
        vt = vt_ref[g * HEAD_DIM:(g + 1) * HEAD_DIM, :].astype(BF16)
        for h in range(group):
            l_ = slice((g * group + h) * HEAD_DIM, (g * group + h + 1) * HEAD_DIM)
            for r0 in range(0, rows, blk):
                st = _dot_nt(k, q_ref[0, r0:r0 + blk, l_])
                m = jnp.max(st, axis=0, keepdims=True)
                p = jnp.exp2(st - m)
                l = jnp.sum(p, axis=0, keepdims=True)
                ot = _dot(vt, p.astype(BF16)) / l
                o_ref[0, r0:r0 + blk, l_] = ot.T.astype(BF16)


def _attention(q, k, v, k_row0, sk, rows, blk=2048, kv_heads_per_step=1):
    bsz, sq, _ = q.shape
    assert k_row0 % sk == 0
    kb = k_row0 // sk
    gw = (N_Q_HEADS // N_KV_HEADS) * HEAD_DIM * kv_heads_per_step
    kw = HEAD_DIM * kv_heads_per_step
    est = 4 * blk * sk * 12 + 4 * sk * kw * 2 + 4 * rows * gw * 2 + sk * kw * 4
    return pl.pallas_call(
        functools.partial(_attn_kernel, rows=rows, blk=min(blk, rows)),
        out_shape=jax.ShapeDtypeStruct((bsz, sq, ATTN_WIDTH), BF16),
        grid=(bsz, N_KV_HEADS // kv_heads_per_step, sq // rows),
        in_specs=[
            pl.BlockSpec((1, rows, gw), lambda b, h, t: (b, t, h)),
            pl.BlockSpec((1, sk, kw), lambda b, h, t: (b, kb, h)),
            pl.BlockSpec((1, sk, kw), lambda b, h, t: (b, kb, h)),
        ],
        out_specs=pl.BlockSpec((1, rows, gw), lambda b, h, t: (b, t, h)),
        scratch_shapes=[pltpu.VMEM((kw, sk), F32)],
        compiler_params=pltpu.CompilerParams(
            dimension_semantics=("arbitrary", "arbitrary", "arbitrary"),
            vmem_limit_bytes=_vmem_limit(est)),
        name="attention",
    )(q, k, v)


def _out_ln_kernel(a1_ref, a2_ref, w_ref, x_ref, g_ref, lng_ref, lnb_ref, o_ref):
    half = a1_ref.shape[2]
    y = _dot(a1_ref[0], w_ref[0:half, :]) + _dot(a2_ref[0], w_ref[half:2 * half, :])
    o_ref[0] = _deepnorm(x_ref[0], g_ref[0], y, lng_ref[...], lnb_ref[...])


def _out_ln(a1, a2, w, x, gate, lng, lnb, rows):
    bsz, seq, d = x.shape
    half = a1.shape[2]
    est = 2 * half * d * 2 + 4 * rows * d * 4 + 4 * rows * half * 2 + 2 * rows * d * 4
    return pl.pallas_call(
        _out_ln_kernel,
        out_shape=jax.ShapeDtypeStruct((bsz, seq, d), F32),
        grid=(bsz, seq // rows),
        in_specs=[
            pl.BlockSpec((1, rows, half), lambda b, t: (b, t, 0)),
            pl.BlockSpec((1, rows, half), lambda b, t: (b, t, 0)),
            _resident((2 * half, d)),
            pl.BlockSpec((1, rows, d), lambda b, t: (b, t, 0)),
            _row_spec(d),
            _resident((1, d)), _resident((1, d)),
        ],
        out_specs=pl.BlockSpec((1, rows, d), lambda b, t: (b, t, 0)),
        compiler_params=pltpu.CompilerParams(
            dimension_semantics=("arbitrary", "arbitrary"),
            vmem_limit_bytes=_vmem_limit(est)),
        name="out_ln",
    )(a1, a2, w, x, gate, lng, lnb)


def _ffn_kernel(x_ref, xp_ref, xn_ref, sh_ref, sc_ref, g_ref, wu_ref, cw_ref, cb_ref, wd_ref, lng_ref, lnb_ref,
                o_ref, hm_ref, u_ref, gt_ref, *, rows, seg):
    _fill_halo_tile(hm_ref, x_ref, xp_ref, xn_ref, sh_ref[0], sc_ref[0], rows)
    n_chunks = wd_ref.shape[0]

    def cols(ref, c):
        off = c * FF_CHUNK
        return ref[:, pl.ds(off if isinstance(off, int) else pl.multiple_of(off, FF_CHUNK), FF_CHUNK)]
    tiles = rows // V7X_SUBLANES
    sub = lax.broadcasted_iota(jnp.int32, (tiles, V7X_SUBLANES, FF_CHUNK), 1)
    if seg < rows:
        seg_tiles = seg // V7X_SUBLANES
        til = lax.broadcasted_iota(jnp.int32, (tiles, V7X_SUBLANES, FF_CHUNK), 0) % seg_tiles
        seq_first = (sub == 0) & (til == 0)
        seq_last = (sub == V7X_SUBLANES - 1) & (til == seg_tiles - 1)

    def up_proj(c, slot):
        hm = hm_ref[...]
        u_ref[slot, 0] = _dot(hm, cols(wu_ref, c)).reshape(tiles + 2, V7X_SUBLANES, FF_CHUNK)
        u_ref[slot, 1] = _dot(hm, cols(wu_ref, c + n_chunks)).reshape(tiles + 2, V7X_SUBLANES, FF_CHUNK)

    def conv_gate(c, slot):
        def conv(half):
            u = u_ref[slot, half]
            dn = pltpu.roll(u, 1, 1)
            up = pltpu.roll(u, V7X_SUBLANES - 1, 1)
            prev = jnp.where(sub == 0, dn[0:tiles], dn[1:tiles + 1])
            nxt = jnp.where(sub == V7X_SUBLANES - 1, up[2:tiles + 2], up[1:tiles + 1])
            if seg < rows:
                prev = jnp.where(seq_first, 0.0, prev)
                nxt = jnp.where(seq_last, 0.0, nxt)
            w = cols(cw_ref, c + half * n_chunks)
            return cols(cb_ref, c + half * n_chunks) + w[0:1] * prev + w[1:2] * u[1:tiles + 1] + w[2:3] * nxt

        a1 = conv(0)
        a2 = conv(1)
        gt_ref[c] = (a1 * (a2 + a2 * jnp.tanh(a2))).reshape(rows, FF_CHUNK)

    up_proj(0, 0)

    def pair(k, carry):
        c = 2 * k
        up_proj(c + 1, 1)
        conv_gate(c, 0)
        up_proj(c + 2, 0)
        conv_gate(c + 1, 1)
        return carry

    lax.fori_loop(0, (n_chunks - 1) // 2, pair, 0)
    conv_gate(n_chunks - 1, 0)

    f = _dot(gt_ref[0].astype(BF16), wd_ref[0])
    for c in range(1, n_chunks):
        f = f + _dot(gt_ref[c].astype(BF16), wd_ref[c])
    o_ref[0] = _deepnorm(x_ref[0], g_ref[0], f, lng_ref[...], lnb_ref[...])


def _ffn(x, sh, sc, gate, wu, cw, cb, wd, layer, lng, lnb, rows, seg=None):
    bsz, seq, d = x.shape
    seg = seq if seg is None else seg
    assert seg == seq or (rows == seq and rows % seg == 0 and seg % V7X_SUBLANES == 0)
    nc = wd.shape[1]
    dff = nc * FF_CHUNK
    assert wu.shape[2] == 2 * dff and nc % 2 == 1, "the pipeline peels one chunk and pairs the rest"

    def of_layer(*shape):
        nd = len(shape)
        return pl.BlockSpec((None,) + shape, lambda b, t: (layer,) + (0,) * nd, pipeline_mode=pl.Buffered(1))

    est = (3 * nc * d * FF_CHUNK * 2 + 4 * rows * d * 4 + (rows + 2 * HALO) * d * 2
           + nc * rows * FF_CHUNK * 4 + 8 * (rows + 2 * HALO) * FF_CHUNK * 4 + 2 * rows * d * 4)
    return pl.pallas_call(
        functools.partial(_ffn_kernel, rows=rows, seg=min(seg, rows)),
        out_shape=jax.ShapeDtypeStruct((bsz, seq, d), F32),
        grid=(bsz, seq // rows),
        in_specs=_halo_specs(rows, seq, d) + [
            _row_spec(d), _row_spec(d), _row_spec(d),
            of_layer(d, 2 * dff), of_layer(CONV_W, 2 * dff), of_layer(1, 2 * dff),
            of_layer(nc, FF_CHUNK, d),
            _resident((1, d)), _resident((1, d)),
        ],
        out_specs=pl.BlockSpec((1, rows, d), lambda b, t: (b, t, 0)),
        scratch_shapes=[
            pltpu.VMEM((rows + 2 * HALO, d), BF16),
            pltpu.VMEM((2, 2, rows // HALO + 2, HALO, FF_CHUNK), F32),
            pltpu.VMEM((nc, rows, FF_CHUNK), F32),
        ],
        compiler_params=pltpu.CompilerParams(
            dimension_semantics=("arbitrary", "arbitrary"),
            vmem_limit_bytes=_vmem_limit(est)),
        name="conv_ffn",
    )(x, x, x, sh, sc, gate, wu, cw, cb, wd, lng, lnb)


def _cd_in_kernel(*refs, with_fz, rows, seg):
    x_ref, xp_ref, xn_ref, sh_ref, sc_ref = refs[:5]
    if with_fz:
        wfz_ref, wx_ref, wdt_ref, cw_ref, cb_ref, dtb_ref = refs[5:11]
        f_ref, z_ref, xs_ref, b_ref, c_ref, dt_ref, hm_ref = refs[11:]
    else:
        wx_ref, wdt_ref, cw_ref, cb_ref, dtb_ref = refs[5:10]
        xs_ref, b_ref, c_ref, dt_ref, hm_ref = refs[10:]
    _fill_halo_tile(hm_ref, x_ref, xp_ref, xn_ref, sh_ref[0], sc_ref[0], rows)
    hc = hm_ref[HALO:HALO + rows, :]
    if with_fz:
        pfz = _dot(hc, wfz_ref[...])
        f_ref[0] = pfz[:, 0:FN_WIDTH].astype(BF16)
        z_ref[0] = pfz[:, FN_WIDTH:SSM_OFF]
    a = _conv3(_dot(hm_ref[...], wx_ref[...]), cw_ref[...], cb_ref[...], rows, seg)
    s = a + a * jnp.tanh(a)
    xs_ref[0] = s[:, 0:SSD_WIDTH]
    b_ref[0] = s[:, SSD_WIDTH:SSD_WIDTH + SSD_GN].astype(BF16)
    c_ref[0] = s[:, SSD_WIDTH + SSD_GN:SSD_CONV_DIM].astype(BF16)
    dt_ref[0] = _softplus(_dot(hc, wdt_ref[...]) + dtb_ref[...])


def _cd_in(x, sh, sc, wfz, wx, wdt, cw, cb, dtb, rows, seg=None):
    bsz, seq, d = x.shape
    seg = seq if seg is None else seg
    assert seg == seq or (rows == seq and rows % seg == 0 and seg % V7X_SUBLANES == 0)
    with_fz = wfz is not None
    in_specs = _halo_specs(rows, seq, d) + [_row_spec(d), _row_spec(d)]
    args = [x, x, x, sh, sc]
    out_shape, out_specs = [], []
    if with_fz:
        in_specs.append(_resident((d, SSM_OFF)))
        args.append(wfz)
        out_shape += [jax.ShapeDtypeStruct((bsz, seq, FN_WIDTH), BF16),
                      jax.ShapeDtypeStruct((bsz, seq, SSD_WIDTH), F32)]
        out_specs += [pl.BlockSpec((1, rows, FN_WIDTH), lambda b, t: (b, t, 0)),
                      pl.BlockSpec((1, rows, SSD_WIDTH), lambda b, t: (b, t, 0))]
    in_specs += [_resident((d, SSD_CONV_DIM)), _resident((d, V7X_LANES)),
                 _resident((CONV_W, SSD_CONV_DIM)), _resident((1, SSD_CONV_DIM)),
                 _resident((1, V7X_LANES))]
    args += [wx, wdt, cw, cb, dtb]
    out_shape += [jax.ShapeDtypeStruct((bsz, seq, SSD_WIDTH), F32),
                  jax.ShapeDtypeStruct((bsz, seq, SSD_GN), BF16),
                  jax.ShapeDtypeStruct((bsz, seq, SSD_GN), BF16),
                  jax.ShapeDtypeStruct((bsz, seq, V7X_LANES), F32)]
    out_specs += [pl.BlockSpec((1, rows, SSD_WIDTH), lambda b, t: (b, t, 0)),
                  pl.BlockSpec((1, rows, SSD_GN), lambda b, t: (b, t, 0)),
                  pl.BlockSpec((1, rows, SSD_GN), lambda b, t: (b, t, 0)),
                  pl.BlockSpec((1, rows, V7X_LANES), lambda b, t: (b, t, 0))]
    est = d * 2200 * 2 + 2 * rows * d * 4 + 8 * (rows + 2 * HALO) * SSD_CONV_DIM * 4
    return pl.pallas_call(
        functools.partial(_cd_in_kernel, with_fz=with_fz, rows=rows, seg=min(seg, rows)),
        out_shape=out_shape,
        grid=(bsz, seq // rows),
        in_specs=in_specs,
        out_specs=out_specs,
        scratch_shapes=[pltpu.VMEM((rows + 2 * HALO, d), BF16)],
        compiler_params=pltpu.CompilerParams(
            dimension_semantics=("arbitrary", "arbitrary"),
            vmem_limit_bytes=_vmem_limit(est)),
        name="cd_in" if with_fz else "cd_in_ctx",
    )(*args)


def _cumsum_rows(z, reverse):
    n = z.shape[0]
    idx = lax.broadcasted_iota(jnp.int32, z.shape, 0)
    k = 1
    while k < n:
        if reverse:
            z = z + jnp.where(idx < n - k, pltpu.roll(z, n - k, 0), 0.0)
        else:
            z = z + jnp.where(idx >= k, pltpu.roll(z, k, 0), 0.0)
        k *= 2
    return z


def _lane_expand(row, lane0, width):
    lane = lax.broadcasted_iota(jnp.int32, (1, SSD_HPG * width), 1)
    out = jnp.zeros((1, SSD_HPG * width), F32)
    for j in range(SSD_HPG):
        val = jnp.broadcast_to(row[:, lane0 + j:lane0 + j + 1], (1, SSD_HPG * width))
        out = jnp.where(lane // width == j, val, out)
    return out


def _ssd_chunk(x, bm, cm, dt, a_row, h_ref, lane0, reverse, need_y):
    q = x.shape[0]
    gw = SSD_HPG * SSD_HEADDIM
    acum = _cumsum_rows(dt * a_row, reverse)
    total = acum[0:1, :] if reverse else acum[q - 1:q, :]
    acum_t = acum.T
    dt_t = dt.T
    total_t = jnp.broadcast_to(acum_t[:, 0:1] if reverse else acum_t[:, q - 1:q], acum_t.shape)
    wgt_t = dt_t * jnp.exp2(total_t - acum_t)
    lane_g = lax.broadcasted_iota(jnp.int32, (q, gw), 1) // SSD_HEADDIM
    lane_gh = lax.broadcasted_iota(jnp.int32, (SSD_STATE, gw), 1) // SSD_HEADDIM
    if need_y:
        srow_t = acum_t - jnp.log2(dt_t)
        ti = lax.broadcasted_iota(jnp.int32, (q, q), 0)
        si = lax.broadcasted_iota(jnp.int32, (q, q), 1)
        mask = (si >= ti) if reverse else (si <= ti)
    ys = []
    for g in range(SSD_GROUPS):
        bg = bm[:, g * SSD_STATE:(g + 1) * SSD_STATE]
        xgb = x[:, g * gw:(g + 1) * gw].astype(BF16)
        x_heads = [jnp.where(lane_g == j, xgb, jnp.zeros_like(xgb)) for j in range(SSD_HPG)]
        hg = h_ref[g]
        if need_y:
            cg = cm[:, g * SSD_STATE:(g + 1) * SSD_STATE]
            cb = _dot_nt(cg, bg)
            cgf = cg.astype(F32)
            hgb = hg.astype(BF16)
            lhs, rhs = [], []
            for j in range(SSD_HPG):
                jl = lane0 + g * SSD_HPG + j
                col = jnp.broadcast_to(acum[:, jl:jl + 1], (q, q))
                row = jnp.broadcast_to(srow_t[jl:jl + 1, :], (q, q))
                lhs.append((cb * jnp.exp2(jnp.where(mask, col - row, -jnp.inf))).astype(BF16))
                lhs.append((cgf * jnp.exp2(col[:, 0:SSD_STATE])).astype(BF16))
                rhs.append(x_heads[j])
                rhs.append(jnp.where(lane_gh == j, hgb, jnp.zeros_like(hgb)))
            ys.append(_dot(jnp.concatenate(lhs, axis=1), jnp.concatenate(rhs, axis=0)))
        bgt = bg.astype(F32).T
        wb = [(bgt * jnp.broadcast_to(wgt_t[lane0 + g * SSD_HPG + j:lane0 + g * SSD_HPG + j + 1, :],
                                      (SSD_STATE, q))).astype(BF16) for j in range(SSD_HPG)]
        st = _dot(jnp.concatenate(wb, axis=1), jnp.concatenate(x_heads, axis=0))
        h_ref[g] = hg * jnp.exp2(_lane_expand(total, lane0 + g * SSD_HPG, SSD_HEADDIM)) + st
    if need_y:
        return jnp.concatenate(ys, axis=1)
    return None


def _ssd_kernel(xc_ref, bc_ref, dtc_ref, xl_ref, bl_ref, cl_ref, dtl_ref, z_ref, alog_ref,
                dskip_ref, ng_ref, o_ref, y_ref, h_ref):
    a_row = -jnp.exp(alog_ref[...]) * math.log2(math.e)
    nc_ctx = xc_ref.shape[1] // SSD_CHUNK
    nc_lat = xl_ref.shape[1] // SSD_CHUNK

    def rows_of(c):
        return pl.ds(pl.multiple_of(c * SSD_CHUNK, SSD_CHUNK), SSD_CHUNK)

    def ctx_step(c, direction):
        r = rows_of(c)
        _ssd_chunk(xc_ref[0, r, :], bc_ref[0, r, :], None, dtc_ref[0, r, :], a_row,
                   h_ref.at[direction], direction * SSD_HEADS, direction == 1, False)

    def lat_step(c, direction):
        r = rows_of(c)
        y_ref[direction, r, :] = _ssd_chunk(
            xl_ref[0, r, :], bl_ref[0, r, :], cl_ref[0, r, :], dtl_ref[0, r, :], a_row,
            h_ref.at[direction], direction * SSD_HEADS, direction == 1, True)

    h_ref[...] = jnp.zeros_like(h_ref)

    def ctx_pair(i, carry):
        ctx_step(i, 0)
        ctx_step(nc_ctx - 1 - i, 1)
        return carry

    lax.fori_loop(0, nc_ctx, ctx_pair, 0)

    def lat_pair(i, carry):
        lat_step(i, 0)
        lat_step(nc_lat - 1 - i, 1)
        return carry

    lax.fori_loop(0, nc_lat, lat_pair, 0)

    def finish(c, carry):
        r = rows_of(c)
        zz = z_ref[0, r, :]
        yt = (y_ref[0, r, :] + y_ref[1, r, :] + xl_ref[0, r, :] * dskip_ref[...]) * _silu(zz)
        o_ref[0, r, :] = _rms_norm(yt, ng_ref[...]).astype(BF16)
        return carry

    lax.fori_loop(0, nc_lat, finish, 0)


def _ssd(xs_c, b_c, dt_c, xs_l, b_l, c_l, dt_l, z, alog_row, dskip_row, ng_row):
    bsz, sl, w = xs_l.shape
    sc = xs_c.shape[1]
    assert sl % SSD_CHUNK == 0 and sc % SSD_CHUNK == 0
    gw = SSD_HPG * SSD_HEADDIM

    def full(s, n):
        return pl.BlockSpec((1, s, n), lambda b: (b, 0, 0))

    est = (2 * (sl + sc) * (w * 4 + 2 * SSD_GN * 2 + V7X_LANES * 4) + 2 * sl * w * 4
           + 2 * sl * w * 2 + 2 * sl * w * 4)
    return pl.pallas_call(
        _ssd_kernel,
        out_shape=jax.ShapeDtypeStruct((bsz, sl, w), BF16),
        grid=(bsz,),
        in_specs=[
            full(sc, w), full(sc, SSD_GN), full(sc, V7X_LANES),
            full(sl, w), full(sl, SSD_GN), full(sl, SSD_GN), full(sl, V7X_LANES),
            full(sl, w),
            _resident((1, V7X_LANES)), _resident((1, w)), _resident((1, w)),
        ],
        out_specs=full(sl, w),
        scratch_shapes=[
            pltpu.VMEM((2, sl, w), F32),
            pltpu.VMEM((2, SSD_GROUPS, SSD_STATE, gw), F32),
        ],
        compiler_params=pltpu.CompilerParams(
            dimension_semantics=("arbitrary",),
            vmem_limit_bytes=_vmem_limit(est)),
        name="ssd_scan",
    )(xs_c, b_c, dt_c, xs_l, b_l, c_l, dt_l, z, alog_row, dskip_row, ng_row)


def _fourier_kernel(f_ref, cc_ref, sc_ref, w2_ref, o_ref, y_ref, *, rows):
    seq = f_ref.shape[1]
    for g in range(FN_GROUPS):
        l = slice(g * FN_GDIM, (g + 1) * FN_GDIM)
        fg = f_ref[0, :, l]
        y_ref[0:seq, l] = _dot(fg, cc_ref[...]).astype(BF16)
        y_ref[seq:2 * seq, l] = _dot(fg, sc_ref[...]).astype(BF16)
    norm = 1.0 / math.sqrt(seq * FN_GDIM)

    def tile(i, carry):
        r = pl.ds(pl.multiple_of(i * rows, rows), rows)
        o_ref[0, r, :] = (_dot(w2_ref[r, :], y_ref[...]) * norm).astype(BF16)
        return carry

    lax.fori_loop(0, seq // rows, tile, 0)


def _fourier(f, cc, sc, w2, rows):
    bsz, seq, w = f.shape
    est = seq * 2 * seq * 2 + 4 * seq * w * 2 + 2 * seq * w * 2 + 4 * rows * 2 * seq * 2
    return pl.pallas_call(
        functools.partial(_fourier_kernel, rows=rows),
        out_shape=jax.ShapeDtypeStruct((bsz, seq, w), BF16),
        grid=(bsz,),
        in_specs=[
            pl.BlockSpec((1, seq, w), lambda b: (b, 0, 0)),
            _resident((FN_GDIM, FN_GDIM)), _resident((FN_GDIM, FN_GDIM)),
            _resident((seq, 2 * seq)),
        ],
        out_specs=pl.BlockSpec((1, seq, w), lambda b: (b, 0, 0)),
        scratch_shapes=[pltpu.VMEM((2 * seq, w), BF16)],
        compiler_params=pltpu.CompilerParams(
            dimension_semantics=("arbitrary",),
            vmem_limit_bytes=_vmem_limit(est)),
        name="fourier_mix",
    )(f, cc, sc, w2)


def _rope_tables(seq):
    rows = seq // GRID_W
    row = jnp.repeat(jnp.arange(rows, dtype=F32), GRID_W)
    col = jnp.broadcast_to(jnp.arange(GRID_W, dtype=F32), (rows, GRID_W)).reshape(-1)
    n_freq = HEAD_DIM // 4
    inv = ROPE_THETA ** (-jnp.arange(n_freq, dtype=F32) / n_freq)
    ang = jnp.stack([row, col], axis=-1)[:, :, None] * inv
    cos, sin = jnp.cos(ang), jnp.sin(ang)
    cos_t = jnp.concatenate([cos[:, 0], cos[:, 1], cos[:, 0], cos[:, 1]], axis=-1)
    sin_t = jnp.concatenate([-sin[:, 0], -sin[:, 1], sin[:, 0], sin[:, 1]], axis=-1)
    return cos_t, sin_t


def _rope_head_order(m):
    lead = m.shape[:-1]
    nh = m.shape[-1] // HEAD_DIM
    m = m.reshape(lead + (nh, 2, 2, HEAD_DIM // 4))
    return jnp.swapaxes(m, -3, -2).reshape(lead + (nh * HEAD_DIM,))


def _dft_tables(seq):
    def tw(n):
        i = np.arange(n, dtype=np.int64)
        ang = ((i[:, None] * i[None, :]) % n).astype(np.float64) * (2.0 * math.pi / n)
        return np.cos(ang), np.sin(ang)

    cc, sc = tw(FN_GDIM)
    cl, sl = tw(seq)
    w2 = np.concatenate([cl, -sl], axis=1).astype(np.float32)
    return (jnp.asarray(cc.astype(np.float32), BF16), jnp.asarray(sc.astype(np.float32), BF16),
            jnp.asarray(w2, BF16))


def _ffn_params(w_up, conv_w, conv_b, w_down):
    depth, d, _ = w_up.shape
    half_gate = jnp.concatenate([jnp.ones((D_FF,), F32), jnp.full((D_FF,), 0.5, F32)])
    return (w_up.astype(BF16), conv_w * half_gate, (conv_b * half_gate).reshape(depth, 1, 2 * D_FF),
            w_down.astype(BF16).reshape(depth, D_FF // FF_CHUNK, FF_CHUNK, d))


def kernel(x, c, ctx, c_ctx, mod_w, mod_b, ln1_g, ln1_b, ln2_g, ln2_b, ffn_up, ffn_conv_w, ffn_conv_b,
           ffn_down, ab_w_in, ab_w_out, gm_ln_g, gm_ln_b, gm_ws, gm_bs, q_norm_g, k_norm_g, cd_w_in,
           cd_w_out, ssd_conv_w, ssd_conv_b, ssd_dt_bias, ssd_a_log, ssd_d, ssd_norm_g):
    bsz, seq, d = x.shape
    ctx_len = ctx.shape[1]
    lat_rows = min(1024, seq)
    out_rows = min(2048, seq)
    dft_rows = min(2048, seq)
    ctx_rows = min(256, ctx_len)

    pad = (-(bsz + 1)) % V7X_SUBLANES
    cc = jnp.concatenate([c, c_ctx[None], jnp.zeros((pad, d), F32)], axis=0)
    mod = _modulation(cc, mod_w, mod_b)
    rope_tabs = _rope_tables(seq)

    def row(v):
        return v.reshape(1, -1)

    ffn_p = _ffn_params(ffn_up, ffn_conv_w, ffn_conv_b, ffn_down)
    xl, xc = x, ctx
    for i in range(DEPTH):
        last = i == DEPTH - 1
        j = i // 2
        ml = mod[i, :bsz].reshape(bsz, 6, 1, d)
        mc = jnp.broadcast_to(mod[i, bsz].reshape(1, 6, 1, d), (bsz, 6, 1, d))
        sh1, sc1, g1, sh2, sc2, g2 = (ml[:, k] for k in range(6))
        csh1, csc1, cg1, csh2, csc2, cg2 = (mc[:, k] for k in range(6))
        ln1 = (row(ln1_g[i]), row(ln1_b[i]))
        ln2 = (row(ln2_g[i]), row(ln2_b[i]))

        if i % 2 == 0:
            w_in = ab_w_in[j].astype(BF16)
            qk_lo, qk_hi = 2 * GM_WIDTH, KV_OFF + KV_W
            q_scale = HEAD_DIM ** -0.5 * math.log2(math.e)
            w_in = jnp.concatenate([0.5 * w_in[:, :qk_lo], _rope_head_order(w_in[:, qk_lo:qk_hi]),
                                    w_in[:, qk_hi:]], axis=1)
            w_out = ab_w_out[j].astype(BF16)
            gm = (row(gm_ln_g[j]), row(gm_ln_b[j]), gm_ws[j].astype(BF16),
                  jnp.repeat(gm_bs[j].T, GM_WIDTH // GM_GROUPS, axis=1),
                  row(_rope_head_order(q_norm_g[j]) * q_scale), row(_rope_head_order(k_norm_g[j])))
            kv_len = seq + ctx_len
            a_c, q_c, k_all, v_all = _ab_in(xc, csh1, csc1, w_in, *gm, None, ctx_rows, kv_len, seq)
            a_l, q_l, k_all, v_all = _ab_in(xl, sh1, sc1, w_in, *gm, rope_tabs, out_rows, kv_len, 0,
                                            kv_into=(k_all, v_all))
            mix_l = (a_l, _attention(q_l, k_all, v_all, 0, kv_len, out_rows, kv_heads_per_step=N_KV_HEADS))
            if not last:
                mix_c = (a_c, _attention(q_c, k_all, v_all, seq, ctx_len, ctx_rows,
                                             kv_heads_per_step=N_KV_HEADS))
        else:
            w_in = cd_w_in[j].astype(BF16)
            w_out = cd_w_out[j].astype(BF16)
            wfz = w_in[:, :SSM_OFF]
            wx = w_in[:, SSM_OFF:SSM_OFF + SSD_CONV_DIM]
            ndt = 2 * SSD_HEADS
            wdt = jnp.pad(w_in[:, SSM_OFF + SSD_CONV_DIM:], ((0, 0), (0, V7X_LANES - ndt)))
            dtb = jnp.pad(ssd_dt_bias[j].reshape(1, ndt), ((0, 0), (0, V7X_LANES - ndt)))
            alog = jnp.pad(ssd_a_log[j].reshape(1, ndt), ((0, 0), (0, V7X_LANES - ndt)))
            cw, cb = 0.5 * ssd_conv_w[j], 0.5 * row(ssd_conv_b[j])
            dskip = row(jnp.repeat(ssd_d[j], SSD_HEADDIM))
            f_l, z_l, xs_l, b_l, c_l, dt_l = _cd_in(xl, sh1, sc1, wfz, wx, wdt, cw, cb, dtb, lat_rows)
            if last:
                n_cd = math.gcd(bsz, max(1, lat_rows // ctx_len))
                xs_c, b_c, _, dt_c = (
                    v.reshape(bsz, ctx_len, v.shape[-1]) for v in
                    _cd_in(xc.reshape(bsz // n_cd, n_cd * ctx_len, d), csh1[:bsz // n_cd], csc1[:bsz // n_cd],
                           None, wx, wdt, cw, cb, dtb, n_cd * ctx_len, seg=ctx_len))
            else:
                raise NotImplementedError("odd layers with a context output are not part of this block")
            s_l = _ssd(xs_c, b_c, dt_c, xs_l, b_l, c_l, dt_l, z_l, alog, dskip, row(ssd_norm_g[j]))
            mix_l = (_fourier(f_l, *_dft_tables(seq), dft_rows), s_l)

        xl = _out_ln(*mix_l, w_out, xl, g1, *ln1, out_rows)
        xl = _ffn(xl, sh2, sc2, g2, *ffn_p, i, *ln2, lat_rows)
        if not last:
            def stacked(v, n):
                return v.reshape(bsz // n, n * ctx_len, v.shape[-1])
            n_out = math.gcd(bsz, max(1, out_rows // ctx_len))
            n_ffn = math.gcd(bsz, max(1, lat_rows // ctx_len))
            xc = _out_ln(*(stacked(m, n_out) for m in mix_c), w_out, stacked(xc, n_out), cg1[:bsz // n_out],
                         *ln1, n_out * ctx_len)
            xc = _ffn(stacked(xc.reshape(bsz, ctx_len, d), n_ffn), csh2[:bsz // n_ffn], csc2[:bsz // n_ffn],
                      cg2[:bsz // n_ffn], *ffn_p, i, *ln2, n_ffn * ctx_len, seg=ctx_len)
            xc = xc.reshape(bsz, ctx_len, d)
    return xl
```
